```python
import jax, jax.numpy as jnp
from jax import lax
import numpy as np

D_MODEL = 1024
BATCH = 8
SEQ = 4096
DEPTH = 1

HEAD_DIM = 64
DILATED_GROUPS = ((128, 1), (512, 4), (2048, 16))
N_GROUPS = 3
HEADS_PER_GROUP = 8
ATTN_HEADS = N_GROUPS * HEADS_PER_GROUP
ATTN_QKV_WIDTH = ATTN_HEADS * HEAD_DIM
ATTN_WIDTH = HEADS_PER_GROUP * HEAD_DIM
CONV_WIDTH = D_MODEL // 2
CONV_KERNEL = 31
N_BRANCHES = 2
NORM_EPS = 1e-6
SPLITS = (ATTN_QKV_WIDTH, ATTN_QKV_WIDTH, ATTN_QKV_WIDTH, ATTN_WIDTH, 2 * CONV_WIDTH, CONV_WIDTH, N_BRANCHES * D_MODEL)
IN_WIDTH = 1536 * 3 + 512 + 1024 + 512 + 2048

kernel_name = "hybrid_dilated_attn_conformer_gated_merge"


def rms_norm(x, w):
    xf = x.astype(jnp.float32)
    y = xf * lax.rsqrt(jnp.mean(xf * xf, axis=-1, keepdims=True) + NORM_EPS)
    return (y * w.astype(jnp.float32)).astype(x.dtype)


def layer_norm(x, w, b):
    xf = x.astype(jnp.float32)
    mu = jnp.mean(xf, axis=-1, keepdims=True)
    var = jnp.mean(jnp.square(xf - mu), axis=-1, keepdims=True)
    y = (xf - mu) * lax.rsqrt(var + NORM_EPS)
    return (y * w.astype(jnp.float32) + b.astype(jnp.float32)).astype(x.dtype)


def dilated_window_attention(q, k, v, window, dilation):
    B, S, H, Dh = q.shape
    d = dilation
    n = window // dilation
    L = S // d
    nb = -(-L // n)
    Lp = nb * n

    def to_sub(t):
        t = t.reshape(B, L, d, H, Dh).transpose(0, 2, 1, 3, 4).reshape(B * d, L, H, Dh)
        return jnp.pad(t, ((0, 0), (0, Lp - L), (0, 0), (0, 0)))

    qs, ks, vs = to_sub(q), to_sub(k), to_sub(v)
    BD = B * d
    qb = qs.reshape(BD, nb, n, H, Dh)

    def band(t):
        tp = jnp.pad(t, ((0, 0), (n, 0), (0, 0), (0, 0))).reshape(BD, nb + 1, n, H, Dh)
        return jnp.concatenate([tp[:, :-1], tp[:, 1:]], axis=2)

    kb, vb = band(ks), band(vs)
    s = jnp.einsum('bnqhd,bnkhd->bnhqk', qb, kb).astype(jnp.float32)
    blk = jnp.arange(nb)[:, None, None]
    qi = jnp.arange(n)[None, :, None]
    kj = jnp.arange(2 * n)[None, None, :]
    dist = qi + n - kj
    valid = (dist >= 0) & (dist <= n) & (blk * n - n + kj >= 0)
    s = jnp.where(valid[None, :, None], s, -jnp.inf)
    m = jnp.max(s, axis=-1, keepdims=True)
    p = jnp.exp(s - m)
    denom = jnp.sum(p, axis=-1, keepdims=True)
    o = jnp.einsum('bnhqk,bnkhd->bnqhd', p, vb.astype(jnp.float32))
    o = o / denom.transpose(0, 1, 3, 2, 4)
    lse = (m + jnp.log(denom))[..., 0].transpose(0, 1, 3, 2)

    def from_sub(t):
        rest = t.shape[3:]
        t = t.reshape((B, d, Lp) + rest)[:, :, :L]
        t = jnp.moveaxis(t, 1, 2)
        return t.reshape((B, S) + rest)

    return from_sub(o), from_sub(lse)


def depthwise_causal_conv(u, w, b):
    C = u.shape[-1]
    y = lax.conv_general_dilated(u, w.astype(u.dtype)[:, None, :], window_strides=(1,),
                                 padding=((CONV_KERNEL - 1, 0),),
                                 dimension_numbers=('NWC', 'WIO', 'NWC'),
                                 feature_group_count=C)
    return y + b.astype(u.dtype)


def hybrid_layer(x, c, w_ada, b_ada, norm_w, w_in, b_gate, q_norm_w, k_norm_w,
                 w_attn_proj, conv_w, conv_b, conv_ln_w, conv_ln_b, w_conv_proj, w_out):
    B, S, D = x.shape
    ada = jax.nn.silu(c) @ w_ada + b_ada
    shift, scale, gate = jnp.split(ada, 3, axis=-1)
    h = rms_norm(x, norm_w) * (1 + scale[:, None, :]) + shift[:, None, :]

    proj = h @ w_in
    offsets = tuple(int(o) for o in np.cumsum(SPLITS)[:-1])
    q, k, v, z_attn, u_conv, z_conv, g = jnp.split(proj, offsets, axis=-1)

    q = rms_norm(q.reshape(B, S, N_GROUPS, HEADS_PER_GROUP, HEAD_DIM), q_norm_w) * (HEAD_DIM ** -0.5)
    k = rms_norm(k.reshape(B, S, N_GROUPS, HEADS_PER_GROUP, HEAD_DIM), k_norm_w)
    v = v.reshape(B, S, N_GROUPS, HEADS_PER_GROUP, HEAD_DIM)
    outs, lses = [], []
    for gi, (window, dilation) in enumerate(DILATED_GROUPS):
        o, lse = dilated_window_attention(q[:, :, gi], k[:, :, gi], v[:, :, gi], window, dilation)
        outs.append(o)
        lses.append(lse)
    o_all = jnp.stack(outs, axis=0)
    wts = jax.nn.softmax(jnp.stack(lses, axis=0), axis=0)
    attn = jnp.sum(wts[..., None] * o_all, axis=0).reshape(B, S, ATTN_WIDTH).astype(x.dtype)
    y_attn = (attn * jax.nn.silu(z_attn)) @ w_attn_proj

    a_half, b_half = jnp.split(u_conv, 2, axis=-1)
    glu = a_half * jax.nn.sigmoid(b_half)
    cv = depthwise_causal_conv(glu, conv_w, conv_b)
    cv = jax.nn.silu(layer_norm(cv, conv_ln_w, conv_ln_b))
    y_conv = (cv * jax.nn.silu(z_conv)) @ w_conv_proj

    g_attn, g_conv = jnp.split(g + b_gate, 2, axis=-1)
    merged = jax.nn.sigmoid(g_attn) * y_attn + jax.nn.sigmoid(g_conv) * y_conv
    return x + gate[:, None, :] * (merged @ w_out)


def _fwd_setup_inputs(seed: int = 0) -> dict:
    key = jax.random.key(seed)
    ks = jax.random.split(key, 18)
    f32 = jnp.float32

    def nrm(k, shape, scale):
        return jax.random.normal(k, shape, f32) * scale

    return {
        "x": nrm(ks[0], (BATCH, SEQ, D_MODEL), 1.0),
        "c": nrm(ks[1], (BATCH, D_MODEL), 1.0),
        "w_ada": nrm(ks[2], (DEPTH, D_MODEL, 3 * D_MODEL), D_MODEL ** -0.5),
        "b_ada": nrm(ks[3], (DEPTH, 3 * D_MODEL), 0.02),
        "norm_w": 1.0 + nrm(ks[4], (DEPTH, D_MODEL), 0.02),
        "w_in": nrm(ks[5], (DEPTH, D_MODEL, IN_WIDTH), D_MODEL ** -0.5),
        "b_gate": nrm(ks[6], (DEPTH, N_BRANCHES * D_MODEL), 0.02),
        "q_norm_w": 1.0 + nrm(ks[7], (DEPTH, HEAD_DIM), 0.02),
        "k_norm_w": 1.0 + nrm(ks[8], (DEPTH, HEAD_DIM), 0.02),
        "w_attn_proj": nrm(ks[9], (DEPTH, ATTN_WIDTH, D_MODEL), ATTN_WIDTH ** -0.5),
        "conv_w": nrm(ks[10], (DEPTH, CONV_KERNEL, CONV_WIDTH), CONV_KERNEL ** -0.5),
        "conv_b": nrm(ks[11], (DEPTH, CONV_WIDTH), 0.02),
        "conv_ln_w": 1.0 + nrm(ks[12], (DEPTH, CONV_WIDTH), 0.02),
        "conv_ln_b": nrm(ks[13], (DEPTH, CONV_WIDTH), 0.02),
        "w_conv_proj": nrm(ks[14], (DEPTH, CONV_WIDTH, D_MODEL), CONV_WIDTH ** -0.5),
        "w_out": nrm(ks[15], (DEPTH, D_MODEL, D_MODEL), D_MODEL ** -0.5),
    }


def _fwd_reference(x, c, w_ada, b_ada, norm_w, w_in, b_gate, q_norm_w, k_norm_w,
              w_attn_proj, conv_w, conv_b, conv_ln_w, conv_ln_b, w_conv_proj, w_out):
    for l in range(DEPTH):
        x = hybrid_layer(x, c, w_ada[l], b_ada[l], norm_w[l], w_in[l], b_gate[l],
                         q_norm_w[l], k_norm_w[l], w_attn_proj[l], conv_w[l], conv_b[l],
                         conv_ln_w[l], conv_ln_b[l], w_conv_proj[l], w_out[l])
    return x


import jax as _jax
import jax.numpy as _jnp

TWIN_FORMAT = 'train_step'
FWD_PARAMS = ['x', 'c', 'w_ada', 'b_ada', 'norm_w', 'w_in', 'b_gate', 'q_norm_w', 'k_norm_w', 'w_attn_proj', 'conv_w', 'conv_b', 'conv_ln_w', 'conv_ln_b', 'w_conv_proj', 'w_out']
TWIN_WEIGHTS = ['w_ada', 'b_ada', 'norm_w', 'w_in', 'b_gate', 'q_norm_w', 'k_norm_w', 'w_attn_proj', 'conv_w', 'conv_b', 'conv_ln_w', 'conv_ln_b', 'w_conv_proj', 'w_out']
TWIN_DIFF_INPUT = 'x'
TWIN_INPUTS = ['x', 'c', 'w_ada', 'b_ada', 'norm_w', 'w_in', 'b_gate', 'q_norm_w', 'k_norm_w', 'w_attn_proj', 'conv_w', 'conv_b', 'conv_ln_w', 'conv_ln_b', 'w_conv_proj', 'w_out', 'loss_target', 'm_w_ada', 'm_b_ada', 'm_norm_w', 'm_w_in', 'm_b_gate', 'm_q_norm_w', 'm_k_norm_w', 'm_w_attn_proj', 'm_conv_w', 'm_conv_b', 'm_conv_ln_w', 'm_conv_ln_b', 'm_w_conv_proj', 'm_w_out', 'v_w_ada', 'v_b_ada', 'v_norm_w', 'v_w_in', 'v_b_gate', 'v_q_norm_w', 'v_k_norm_w', 'v_w_attn_proj', 'v_conv_w', 'v_conv_b', 'v_conv_ln_w', 'v_conv_ln_b', 'v_w_conv_proj', 'v_w_out']
TWIN_OUTPUTS = ['loss', 'grad_x', 'grad_w_ada', 'grad_b_ada', 'grad_norm_w', 'grad_w_in', 'grad_b_gate', 'grad_q_norm_w', 'grad_k_norm_w', 'grad_w_attn_proj', 'grad_conv_w', 'grad_conv_b', 'grad_conv_ln_w', 'grad_conv_ln_b', 'grad_w_conv_proj', 'grad_w_out', 'delta_w_ada', 'delta_b_ada', 'delta_norm_w', 'delta_w_in', 'delta_b_gate', 'delta_q_norm_w', 'delta_k_norm_w', 'delta_w_attn_proj', 'delta_conv_w', 'delta_conv_b', 'delta_conv_ln_w', 'delta_conv_ln_b', 'delta_w_conv_proj', 'delta_w_out', 'new_m_w_ada', 'new_m_b_ada', 'new_m_norm_w', 'new_m_w_in', 'new_m_b_gate', 'new_m_q_norm_w', 'new_m_k_norm_w', 'new_m_w_attn_proj', 'new_m_conv_w', 'new_m_conv_b', 'new_m_conv_ln_w', 'new_m_conv_ln_b', 'new_m_w_conv_proj', 'new_m_w_out', 'new_v_w_ada', 'new_v_b_ada', 'new_v_norm_w', 'new_v_w_in', 'new_v_b_gate', 'new_v_q_norm_w', 'new_v_k_norm_w', 'new_v_w_attn_proj', 'new_v_conv_w', 'new_v_conv_b', 'new_v_conv_ln_w', 'new_v_conv_ln_b', 'new_v_w_conv_proj', 'new_v_w_out']
TWIN_LEAF_KINDS = {'loss': 'loss', 'grad_x': 'grad_x', 'grad_w_ada': 'grad_w', 'grad_b_ada': 'grad_w', 'grad_norm_w': 'grad_w', 'grad_w_in': 'grad_w', 'grad_b_gate': 'grad_w', 'grad_q_norm_w': 'grad_w', 'grad_k_norm_w': 'grad_w', 'grad_w_attn_proj': 'grad_w', 'grad_conv_w': 'grad_w', 'grad_conv_b': 'grad_w', 'grad_conv_ln_w': 'grad_w', 'grad_conv_ln_b': 'grad_w', 'grad_w_conv_proj': 'grad_w', 'grad_w_out': 'grad_w', 'delta_w_ada': 'delta_w', 'delta_b_ada': 'delta_w', 'delta_norm_w': 'delta_w', 'delta_w_in': 'delta_w', 'delta_b_gate': 'delta_w', 'delta_q_norm_w': 'delta_w', 'delta_k_norm_w': 'delta_w', 'delta_w_attn_proj': 'delta_w', 'delta_conv_w': 'delta_w', 'delta_conv_b': 'delta_w', 'delta_conv_ln_w': 'delta_w', 'delta_conv_ln_b': 'delta_w', 'delta_w_conv_proj': 'delta_w', 'delta_w_out': 'delta_w', 'new_m_w_ada': 'new_m', 'new_m_b_ada': 'new_m', 'new_m_norm_w': 'new_m', 'new_m_w_in': 'new_m', 'new_m_b_gate': 'new_m', 'new_m_q_norm_w': 'new_m', 'new_m_k_norm_w': 'new_m', 'new_m_w_attn_proj': 'new_m', 'new_m_conv_w': 'new_m', 'new_m_conv_b': 'new_m', 'new_m_conv_ln_w': 'new_m', 'new_m_conv_ln_b': 'new_m', 'new_m_w_conv_proj': 'new_m', 'new_m_w_out': 'new_m', 'new_v_w_ada': 'new_v', 'new_v_b_ada': 'new_v', 'new_v_norm_w': 'new_v', 'new_v_w_in': 'new_v', 'new_v_b_gate': 'new_v', 'new_v_q_norm_w': 'new_v', 'new_v_k_norm_w': 'new_v', 'new_v_w_attn_proj': 'new_v', 'new_v_conv_w': 'new_v', 'new_v_conv_b': 'new_v', 'new_v_conv_ln_w': 'new_v', 'new_v_conv_ln_b': 'new_v', 'new_v_w_conv_proj': 'new_v', 'new_v_w_out': 'new_v'}


def _forward(args):
    return _fwd_reference(*[args[k] for k in FWD_PARAMS])


def _output_shape():
    out = _jax.eval_shape(lambda: _forward(_fwd_setup_inputs(0)))
    return out.shape, out.dtype

N_MICROBATCH = 1
ADAM_LR = 0.001
ADAM_B1 = 0.9
ADAM_B2 = 0.999
ADAM_EPS = 1e-08
ADAM_WD = 0.01
ADAM_STEP = 10
PER_EXAMPLE_BATCH_AXIS = {'x': 0, 'c': 0, 'loss_target': 0}
SHARED_INPUTS = []
_WEIGHT_DTYPES = {'w_ada': _jnp.float32, 'b_ada': _jnp.float32, 'norm_w': _jnp.float32, 'w_in': _jnp.float32, 'b_gate': _jnp.float32, 'q_norm_w': _jnp.float32, 'k_norm_w': _jnp.float32, 'w_attn_proj': _jnp.float32, 'conv_w': _jnp.float32, 'conv_b': _jnp.float32, 'conv_ln_w': _jnp.float32, 'conv_ln_b': _jnp.float32, 'w_conv_proj': _jnp.float32, 'w_out': _jnp.float32}
MOMENT_SCALE = {'w_ada': 6.341543e-01, 'b_ada': 1.309283e+00, 'norm_w': 1.377725e+00, 'w_in': 2.405782e-01, 'b_gate': 2.051835e-01, 'q_norm_w': 7.847716e-01, 'k_norm_w': 7.842670e-01, 'w_attn_proj': 9.611402e-02, 'conv_w': 5.734362e-01, 'conv_b': 1.405012e+00, 'conv_ln_w': 3.272733e+00, 'conv_ln_b': 1.911016e+00, 'w_conv_proj': 1.069546e-01, 'w_out': 1.240571e-01}


def _to_microbatches(a, axis):
    t = _jnp.moveaxis(a, axis, 0)
    t = t.reshape((N_MICROBATCH, t.shape[0] // N_MICROBATCH) + t.shape[1:])
    return _jnp.moveaxis(t, 1, axis + 1)


def setup_inputs(seed: int = 0) -> dict:
    inp = _fwd_setup_inputs(seed)
    key = _jax.random.fold_in(_jax.random.key(seed), 7919)
    shape, _ = _output_shape()
    out = dict(inp)
    out["loss_target"] = _jax.random.normal(_jax.random.fold_in(key, 0), shape, _jnp.float32)
    for i, name in enumerate(TWIN_WEIGHTS):
        w = inp[name].astype(_jnp.float32)
        if MOMENT_SCALE is None:
            s = _jnp.sqrt(_jnp.mean(_jnp.square(w)) + 1e-30)
        else:
            s = MOMENT_SCALE[name]
        km, kv = _jax.random.split(_jax.random.fold_in(key, i + 1))
        out[name] = w
        out["m_" + name] = s * _jax.random.normal(km, w.shape, _jnp.float32)
        out["v_" + name] = (s * s) * _jax.random.uniform(kv, w.shape, _jnp.float32, 0.5, 1.5)
    if N_MICROBATCH > 1:
        for name, axis in PER_EXAMPLE_BATCH_AXIS.items():
            out[name] = _to_microbatches(out[name], axis)
    return {'x': out['x'], 'c': out['c'], 'w_ada': out['w_ada'], 'b_ada': out['b_ada'], 'norm_w': out['norm_w'], 'w_in': out['w_in'], 'b_gate': out['b_gate'], 'q_norm_w': out['q_norm_w'], 'k_norm_w': out['k_norm_w'], 'w_attn_proj': out['w_attn_proj'], 'conv_w': out['conv_w'], 'conv_b': out['conv_b'], 'conv_ln_w': out['conv_ln_w'], 'conv_ln_b': out['conv_ln_b'], 'w_conv_proj': out['w_conv_proj'], 'w_out': out['w_out'], 'loss_target': out['loss_target'], 'm_w_ada': out['m_w_ada'], 'm_b_ada': out['m_b_ada'], 'm_norm_w': out['m_norm_w'], 'm_w_in': out['m_w_in'], 'm_b_gate': out['m_b_gate'], 'm_q_norm_w': out['m_q_norm_w'], 'm_k_norm_w': out['m_k_norm_w'], 'm_w_attn_proj': out['m_w_attn_proj'], 'm_conv_w': out['m_conv_w'], 'm_conv_b': out['m_conv_b'], 'm_conv_ln_w': out['m_conv_ln_w'], 'm_conv_ln_b': out['m_conv_ln_b'], 'm_w_conv_proj': out['m_w_conv_proj'], 'm_w_out': out['m_w_out'], 'v_w_ada': out['v_w_ada'], 'v_b_ada': out['v_b_ada'], 'v_norm_w': out['v_norm_w'], 'v_w_in': out['v_w_in'], 'v_b_gate': out['v_b_gate'], 'v_q_norm_w': out['v_q_norm_w'], 'v_k_norm_w': out['v_k_norm_w'], 'v_w_attn_proj': out['v_w_attn_proj'], 'v_conv_w': out['v_conv_w'], 'v_conv_b': out['v_conv_b'], 'v_conv_ln_w': out['v_conv_ln_w'], 'v_conv_ln_b': out['v_conv_ln_b'], 'v_w_conv_proj': out['v_w_conv_proj'], 'v_w_out': out['v_w_out']}


def _loss(weights, diff, rest, loss_target):
    with _jax.named_scope("forward"):
        args = {**rest, TWIN_DIFF_INPUT: diff, **{k: w.astype(_WEIGHT_DTYPES[k]) for k, w in weights.items()}}
        y = _forward(args)
    with _jax.named_scope("loss_head"):
        err = _jnp.square(y.astype(_jnp.float32) - loss_target)
        return 0.5 * _jnp.sum(_jnp.mean(err, axis=-1)) if err.ndim else 0.5 * err


def _adamw(w, g, m, v):
    m = ADAM_B1 * m + (1.0 - ADAM_B1) * g
    v = ADAM_B2 * v + (1.0 - ADAM_B2) * _jnp.square(g)
    m_hat = m / (1.0 - ADAM_B1 ** ADAM_STEP)
    v_hat = v / (1.0 - ADAM_B2 ** ADAM_STEP)
    delta = -ADAM_LR * (m_hat / (_jnp.sqrt(v_hat) + ADAM_EPS) + ADAM_WD * w)
    return delta, m, v


def reference(x, c, w_ada, b_ada, norm_w, w_in, b_gate, q_norm_w, k_norm_w, w_attn_proj, conv_w, conv_b, conv_ln_w, conv_ln_b, w_conv_proj, w_out, loss_target, m_w_ada, m_b_ada, m_norm_w, m_w_in, m_b_gate, m_q_norm_w, m_k_norm_w, m_w_attn_proj, m_conv_w, m_conv_b, m_conv_ln_w, m_conv_ln_b, m_w_conv_proj, m_w_out, v_w_ada, v_b_ada, v_norm_w, v_w_in, v_b_gate, v_q_norm_w, v_k_norm_w, v_w_attn_proj, v_conv_w, v_conv_b, v_conv_ln_w, v_conv_ln_b, v_w_conv_proj, v_w_out):
    given = dict(x=x, c=c, w_ada=w_ada, b_ada=b_ada, norm_w=norm_w, w_in=w_in, b_gate=b_gate, q_norm_w=q_norm_w, k_norm_w=k_norm_w, w_attn_proj=w_attn_proj, conv_w=conv_w, conv_b=conv_b, conv_ln_w=conv_ln_w, conv_ln_b=conv_ln_b, w_conv_proj=w_conv_proj, w_out=w_out, loss_target=loss_target, m_w_ada=m_w_ada, m_b_ada=m_b_ada, m_norm_w=m_norm_w, m_w_in=m_w_in, m_b_gate=m_b_gate, m_q_norm_w=m_q_norm_w, m_k_norm_w=m_k_norm_w, m_w_attn_proj=m_w_attn_proj, m_conv_w=m_conv_w, m_conv_b=m_conv_b, m_conv_ln_w=m_conv_ln_w, m_conv_ln_b=m_conv_ln_b, m_w_conv_proj=m_w_conv_proj, m_w_out=m_w_out, v_w_ada=v_w_ada, v_b_ada=v_b_ada, v_norm_w=v_norm_w, v_w_in=v_w_in, v_b_gate=v_b_gate, v_q_norm_w=v_q_norm_w, v_k_norm_w=v_k_norm_w, v_w_attn_proj=v_w_attn_proj, v_conv_w=v_conv_w, v_conv_b=v_conv_b, v_conv_ln_w=v_conv_ln_w, v_conv_ln_b=v_conv_ln_b, v_w_conv_proj=v_w_conv_proj, v_w_out=v_w_out)
    weights = {n: given[n] for n in TWIN_WEIGHTS}
    shared = {n: given[n] for n in SHARED_INPUTS}
    per_example = {n: given[n] for n in ['x', 'c']}
    grad_fn = _jax.value_and_grad(_loss, argnums=(0, 1))

    def one_microbatch(ex, loss_target):
        ex = dict(ex)
        diff = ex.pop(TWIN_DIFF_INPUT)
        return grad_fn(weights, diff, {**shared, **ex}, loss_target)

    if N_MICROBATCH == 1:
        loss, (grad_w, grad_x) = one_microbatch(per_example, given["loss_target"])
    else:
        def body(carry, xs):
            loss_sum, grad_sum = carry
            l_k, (gw_k, gx_k) = one_microbatch(xs[0], xs[1])
            with _jax.named_scope("update"):
                return (loss_sum + l_k, _jax.tree.map(_jnp.add, grad_sum, gw_k)), gx_k

        init = (_jnp.zeros((), _jnp.float32), _jax.tree.map(_jnp.zeros_like, weights))
        (loss, grad_w), grad_x = _jax.lax.scan(body, init, (per_example, given["loss_target"]))
    with _jax.named_scope("update"):
        delta_w, new_m, new_v = {}, {}, {}
        for n in TWIN_WEIGHTS:
            delta_w[n], new_m[n], new_v[n] = _adamw(weights[n], grad_w[n], given["m_" + n], given["v_" + n])
    return (loss, grad_x, *[grad_w[n] for n in TWIN_WEIGHTS], *[delta_w[n] for n in TWIN_WEIGHTS],
            *[new_m[n] for n in TWIN_WEIGHTS], *[new_v[n] for n in TWIN_WEIGHTS])
```

```python
import functools

import jax
import jax.numpy as jnp
from jax import lax
from jax.experimental import pallas as pl
from jax.experimental.pallas import tpu as pltpu

F32 = jnp.float32
BF16 = jnp.bfloat16
SDS = jax.ShapeDtypeStruct
MESH = pl.DeviceIdType.MESH

N_DEV = 8
D_MODEL = 1024
HEAD_DIM = 64
N_GROUPS = 3
DILATIONS = (1, 4, 16)
BAND = 128
ATTN_W = 512
CONV_W = 512
CONV_K = 31
CONV_HALO = 32
IN_W = 8704
SHARD_W = IN_W // N_DEV
WIN_W = 1152
PAIR_W = 2 * SHARD_W
Q0, K0, V0, ZA0, U0, ZC0, G0 = 0, 1536, 3072, 4608, 5120, 6144, 6656
EPS = 1e-6
LANE = 128
VMEM_LIMIT = 56 * 1024 * 1024

ADAM_LR, ADAM_B1, ADAM_B2, ADAM_EPS, ADAM_WD, ADAM_STEP = 0.001, 0.9, 0.999, 1e-08, 0.01, 10

_SMALL = (("b_ada", 3072), ("norm_w", 1024), ("b_gate", 2048), ("q_norm_w", 128), ("k_norm_w", 128),
          ("conv_b", 512), ("conv_ln_w", 512), ("conv_ln_b", 512), ("loss", 128))
SMALL_OFF = {}
_o = 0
for _n, _l in _SMALL:
    SMALL_OFF[_n] = (_o, _l)
    _o += _l
SMALL_N = _o
CONVW_FLAT = 2048
PACK_N = SMALL_N + CONVW_FLAT


def _params(**kw):
    return pltpu.CompilerParams(vmem_limit_bytes=VMEM_LIMIT, **kw)


def _sigmoid(z):
    return 1.0 / (1.0 + jnp.exp(-z))


def _dot(a, b):
    return jnp.dot(a, b, preferred_element_type=F32)


def _dot_nt(a, b):
    return lax.dot_general(a, b, (((1,), (1,)), ((), ())), preferred_element_type=F32)


def _dot_tn(a, b):
    return lax.dot_general(a, b, (((0,), (0,)), ((), ())), preferred_element_type=F32)


def _peer(x, y, c, k):
    px = 1 - x if (k >> 2) & 1 else x
    py = 1 - y if (k >> 1) & 1 else y
    pc = 1 - c if k & 1 else c
    return (px, py, pc), 4 * px + 2 * py + pc


def _all_gather(arrays, name, vmem):
    n = len(arrays)
    space = pltpu.VMEM if vmem else pl.ANY

    def body(*refs):
        ins, outs = refs[:n], refs[n:2 * n]
        send_sems, recv_sems, local_sems = refs[2 * n:]
        x, y, c = lax.axis_index("x"), lax.axis_index("y"), lax.axis_index("c")
        me = 4 * x + 2 * y + c
        locals_ = [pltpu.make_async_copy(ins[a], outs[a].at[me], local_sems.at[a]) for a in range(n)]
        for cp in locals_:
            cp.start()
        sends = []
        for k in range(1, N_DEV):
            peer, _ = _peer(x, y, c, k)
            for a in range(n):
                cp = pltpu.make_async_remote_copy(
                    src_ref=ins[a], dst_ref=outs[a].at[me], send_sem=send_sems.at[a, k - 1],
                    recv_sem=recv_sems.at[a, k - 1], device_id=peer, device_id_type=MESH)
                cp.start()
                sends.append(cp)
        for k in range(1, N_DEV):
            peer, pidx = _peer(x, y, c, k)
            for a in range(n):
                pltpu.make_async_remote_copy(
                    src_ref=ins[a], dst_ref=outs[a].at[pidx], send_sem=send_sems.at[a, k - 1],
                    recv_sem=recv_sems.at[a, k - 1], device_id=peer, device_id_type=MESH).wait_recv()
        for cp in sends:
            cp.wait_send()
        for cp in locals_:
            cp.wait()

    return pl.pallas_call(
        body, name=name,
        out_shape=[SDS((N_DEV,) + a.shape, a.dtype) for a in arrays],
        in_specs=[pl.BlockSpec(memory_space=space)] * n,
        out_specs=[pl.BlockSpec(memory_space=space)] * n,
        scratch_shapes=[pltpu.SemaphoreType.DMA((n, N_DEV - 1)), pltpu.SemaphoreType.DMA((n, N_DEV - 1)),
                        pltpu.SemaphoreType.DMA((n,))],
        compiler_params=_params(),
    )(*arrays)


def _exchange_slots(arrays, name):
    n = len(arrays)

    def body(*refs):
        ins, outs = refs[:n], refs[n:2 * n]
        send_sems, recv_sems, local_sems = refs[2 * n:]
        x, y, c = lax.axis_index("x"), lax.axis_index("y"), lax.axis_index("c")
        me = 4 * x + 2 * y + c
        locals_ = [pltpu.make_async_copy(ins[a].at[me], outs[a].at[me], local_sems.at[a]) for a in range(n)]
        for cp in locals_:
            cp.start()
        sends = []
        for k in range(1, N_DEV):
            peer, pidx = _peer(x, y, c, k)
            for a in range(n):
                cp = pltpu.make_async_remote_copy(
                    src_ref=ins[a].at[pidx], dst_ref=outs[a].at[me], send_sem=send_sems.at[a, k - 1],
                    recv_sem=recv_sems.at[a, k - 1], device_id=peer, device_id_type=MESH)
                cp.start()
                sends.append(cp)
        for k in range(1, N_DEV):
            peer, pidx = _peer(x, y, c, k)
            for a in range(n):
                pltpu.make_async_remote_copy(
                    src_ref=ins[a].at[pidx], dst_ref=outs[a].at[pidx], send_sem=send_sems.at[a, k - 1],
                    recv_sem=recv_sems.at[a, k - 1], device_id=peer, device_id_type=MESH).wait_recv()
        for cp in sends:
            cp.wait_send()
        for cp in locals_:
            cp.wait()

    return pl.pallas_call(
        body, name=name,
        out_shape=[SDS(a.shape, a.dtype) for a in arrays],
        in_specs=[pl.BlockSpec(memory_space=pl.ANY)] * n,
        out_specs=[pl.BlockSpec(memory_space=pl.ANY)] * n,
        scratch_shapes=[pltpu.SemaphoreType.DMA((n, N_DEV - 1)), pltpu.SemaphoreType.DMA((n, N_DEV - 1)),
                        pltpu.SemaphoreType.DMA((n,))],
        compiler_params=_params(),
    )(*arrays)


def _to_window(w, c_arr):
    rows = w.shape[0]
    tr = 256

    def body(c_ref, w_ref, o_ref):
        wv = w_ref[...]
        wp = jnp.concatenate([wv, jnp.zeros((tr, WIN_W - SHARD_W), F32)], axis=1)
        ws = jnp.where(c_ref[0] == 1, pltpu.roll(wp, WIN_W - SHARD_W, axis=1), wp)
        o_ref[...] = ws.astype(BF16)

    return pl.pallas_call(
        body, name="to_window", grid=(rows // tr,),
        in_specs=[pl.BlockSpec(memory_space=pltpu.SMEM), pl.BlockSpec((tr, SHARD_W), lambda i: (i, 0))],
        out_specs=pl.BlockSpec((tr, WIN_W), lambda i: (i, 0)),
        out_shape=SDS((rows, WIN_W), BF16), compiler_params=_params(),
    )(c_arr, w)


def _cast_bf16(w, name):
    def body(w_ref, o_ref):
        o_ref[...] = w_ref[...].astype(BF16)

    return pl.pallas_call(body, name=name, out_shape=SDS(w.shape, BF16), compiler_params=_params())(w)


def _cols_from_slots(wg, name):
    _, rows, cols = wg.shape

    def body(w_ref, o_ref):
        for j in range(N_DEV):
            o_ref[:, j * cols:(j + 1) * cols] = w_ref[j]

    return pl.pallas_call(body, name=name, out_shape=SDS((rows, N_DEV * cols), wg.dtype), compiler_params=_params())(wg)


def _ada_fwd(c_all, w_ada):
    def body(c_ref, w_ref, o_ref):
        cv = c_ref[...]
        sc = (cv * _sigmoid(cv)).astype(BF16)
        o_ref[...] = _dot(sc, w_ref[...].astype(BF16))

    return pl.pallas_call(body, name="ada_fwd", out_shape=SDS((N_DEV, w_ada.shape[1]), F32),
                          compiler_params=_params())(c_all, w_ada)


def _ada_bwd(c_all, d_ada_cols):
    def body(c_ref, d_ref, o_ref):
        cv = c_ref[...]
        sc = (cv * _sigmoid(cv)).astype(BF16)
        o_ref[...] = _dot_tn(sc, d_ref[...].astype(BF16))

    return pl.pallas_call(body, name="ada_bwd", out_shape=SDS((D_MODEL, d_ada_cols.shape[1]), F32),
                          compiler_params=_params())(c_all, d_ada_cols)


def _norm_fwd(x, norm_w, scale, shift):
    s = x.shape[0]
    tr = 512

    def body(x_ref, nw_ref, sc_ref, sh_ref, h_ref, ht_ref):
        xv = x_ref[...]
        r = lax.rsqrt(jnp.mean(xv * xv, axis=-1, keepdims=True) + EPS)
        h = (xv * r * nw_ref[...]) * (1.0 + sc_ref[...]) + sh_ref[...]
        h_ref[...] = h.astype(BF16)
        ht_ref[...] = h.T.astype(BF16)

    vec = pl.BlockSpec((1, D_MODEL), lambda i: (0, 0))
    return pl.pallas_call(
        body, name="norm_fwd", grid=(s // tr,),
        in_specs=[pl.BlockSpec((tr, D_MODEL), lambda i: (i, 0)), vec, vec, vec],
        out_specs=[pl.BlockSpec((tr, D_MODEL), lambda i: (i, 0)), pl.BlockSpec((D_MODEL, tr), lambda i: (0, i))],
        out_shape=[SDS((s, D_MODEL), BF16), SDS((D_MODEL, s), BF16)], compiler_params=_params(),
    )(x, norm_w, scale, shift)


def _mm_in(h, wg):
    s = h.shape[0]
    tm = 512

    def body(h_ref, w_ref, o_ref):
        hv = h_ref[...]
        pe = _dot(hv, w_ref[0])
        po = _dot(hv, w_ref[1])
        o_ref[:, 0:1024] = pe[:, 0:1024]
        o_ref[:, 1024:1152] = pe[:, 1024:1152] + po[:, 0:128]
        o_ref[:, 1152:PAIR_W] = po[:, 128:WIN_W]

    return pl.pallas_call(
        body, name="mm_in", grid=(N_DEV // 2, s // tm),
        in_specs=[pl.BlockSpec((tm, D_MODEL), lambda p, m: (m, 0)),
                  pl.BlockSpec((2, D_MODEL, WIN_W), lambda p, m: (p, 0, 0))],
        out_specs=pl.BlockSpec((tm, PAIR_W), lambda p, m: (m, p)),
        out_shape=SDS((s, IN_W), F32), compiler_params=_params(),
    )(h, wg)


def _head_sums(t, lo):
    s_lo = jnp.sum(jnp.where(lo, t, 0.0), axis=-1, keepdims=True)
    s_hi = jnp.sum(jnp.where(lo, 0.0, t), axis=-1, keepdims=True)
    return jnp.where(lo, s_lo, s_hi)


def _band_mask(b):
    qi = lax.broadcasted_iota(jnp.int32, (2 * BAND, 2 * BAND), 0) % BAND
    kj = lax.broadcasted_iota(jnp.int32, (2 * BAND, 2 * BAND), 1)
    dist = qi + BAND - kj
    return (dist >= 0) & (dist <= BAND) & ((kj >= BAND) | (b > 0))


def _deinterleave(src_ref, dst_ref, w_ref, lo, d, sub_len, chunk, scale, dst_off):
    def per_r(r, _):
        def per_c(ci, _):
            t = src_ref[pl.ds(r + ci * (chunk * d), chunk, stride=d), :]
            if w_ref is not None:
                ms = _head_sums(t * t, lo) * (1.0 / HEAD_DIM)
                t = t * lax.rsqrt(ms + EPS) * w_ref[...]
                if scale != 1.0:
                    t = t * scale
            dst_ref[pl.ds(pl.multiple_of(dst_off + r * sub_len + ci * chunk, BAND), chunk), :] = t.astype(dst_ref.dtype)
            return 0
        return lax.fori_loop(0, sub_len // chunk, per_c, 0)
    lax.fori_loop(0, d, per_r, 0)


def _attn_fwd(proj, qw2, kw2, g):
    s = proj.shape[0]
    d = DILATIONS[g]
    sub_len = s // d
    nb = sub_len // BAND
    chunk = min(sub_len, 256)

    def body(q_ref, k_ref, v_ref, qw_ref, kw_ref, o_ref, l_ref, qd, kd, vd, od, ld):
        lo = lax.broadcasted_iota(jnp.int32, (1, LANE), 1) < HEAD_DIM
        kd[0:BAND, :] = jnp.zeros((BAND, LANE), BF16)
        vd[0:BAND, :] = jnp.zeros((BAND, LANE), BF16)
        _deinterleave(q_ref, qd, qw_ref, lo, d, sub_len, chunk, HEAD_DIM ** -0.5, 0)
        _deinterleave(k_ref, kd, kw_ref, lo, d, sub_len, chunk, 1.0, BAND)
        _deinterleave(v_ref, vd, None, lo, d, sub_len, chunk, 1.0, BAND)

        def per_r(r, _):
            def per_b(b, _):
                base = pl.multiple_of(r * sub_len + b * BAND, BAND)
                q = qd[pl.ds(base, BAND), :]
                k2 = kd[pl.ds(base, 2 * BAND), :]
                v2 = vd[pl.ds(base, 2 * BAND), :]
                zero = jnp.zeros_like(q)
                qs = jnp.concatenate([jnp.where(lo, q, zero), jnp.where(lo, zero, q)], axis=0)
                sc = jnp.where(_band_mask(b), _dot_nt(qs, k2), -1e30)
                m = jnp.max(sc, axis=-1, keepdims=True)
                p = jnp.exp(sc - m)
                den = jnp.sum(p, axis=-1, keepdims=True)
                u = _dot(p.astype(BF16), v2) / den
                lse = m + jnp.log(den)
                od[pl.ds(base, BAND), :] = jnp.where(lo, u[:BAND], u[BAND:])
                ld[pl.ds(base, BAND), :] = jnp.where(lo, lse[:BAND], lse[BAND:])
                return 0
            return lax.fori_loop(0, nb, per_b, 0)
        lax.fori_loop(0, d, per_r, 0)

        def back_r(r, _):
            def back_c(ci, _):
                src = pl.ds(pl.multiple_of(r * sub_len + ci * chunk, chunk), chunk)
                dst = pl.ds(r + ci * (chunk * d), chunk, stride=d)
                o_ref[dst, :] = od[src, :]
                l_ref[dst, :] = ld[src, :]
                return 0
            return lax.fori_loop(0, sub_len // chunk, back_c, 0)
        lax.fori_loop(0, d, back_r, 0)

    col = lambda off: pl.BlockSpec((s, LANE), lambda hp, off=off: (0, off // LANE + 4 * g + hp))
    vec = pl.BlockSpec((1, LANE), lambda hp: (0, 0))
    out = pl.BlockSpec((s, LANE), lambda hp: (0, hp))
    return pl.pallas_call(
        body, name=f"attn_fwd{g}", grid=(ATTN_W // LANE,),
        in_specs=[col(Q0), col(K0), col(V0), vec, vec], out_specs=[out, out],
        out_shape=[SDS((s, ATTN_W), F32), SDS((s, ATTN_W), F32)],
        scratch_shapes=[pltpu.VMEM((s, LANE), BF16), pltpu.VMEM((s + BAND, LANE), BF16), pltpu.VMEM((s + BAND, LANE), BF16),
                        pltpu.VMEM((s, LANE), F32), pltpu.VMEM((s, LANE), F32)],
        compiler_params=_params(),
    )(proj, proj, proj, qw2, kw2)


def _attn_bwd(proj, da, delta, lse, qw2, kw2, dproj, g):
    s = proj.shape[0]
    d = DILATIONS[g]
    sub_len = s // d
    nb = sub_len // BAND
    chunk = min(sub_len, 256)

    def body(q_ref, k_ref, v_ref, da_ref, dl_ref, ls_ref, qw_ref, kw_ref, dp_in, dp_out, dqw_ref, dkw_ref,
             qd, kd, vd, dad, dld, lsd, dqd, dkd, dvd, st, stb, sem):
        del dp_in
        hp = pl.program_id(0)
        lo = lax.broadcasted_iota(jnp.int32, (1, LANE), 1) < HEAD_DIM
        kd[0:BAND, :] = jnp.zeros((BAND, LANE), BF16)
        vd[0:BAND, :] = jnp.zeros((BAND, LANE), BF16)
        dkd[...] = jnp.zeros_like(dkd)
        dvd[...] = jnp.zeros_like(dvd)
        _deinterleave(q_ref, qd, qw_ref, lo, d, sub_len, chunk, HEAD_DIM ** -0.5, 0)
        _deinterleave(k_ref, kd, kw_ref, lo, d, sub_len, chunk, 1.0, BAND)
        _deinterleave(v_ref, vd, None, lo, d, sub_len, chunk, 1.0, BAND)
        _deinterleave(da_ref, dad, None, lo, d, sub_len, chunk, 1.0, 0)
        _deinterleave(dl_ref, dld, None, lo, d, sub_len, chunk, 1.0, 0)
        _deinterleave(ls_ref, lsd, None, lo, d, sub_len, chunk, 1.0, 0)

        def per_r(r, _):
            def per_b(b, _):
                base = pl.multiple_of(r * sub_len + b * BAND, BAND)
                q = qd[pl.ds(base, BAND), :]
                k2 = kd[pl.ds(base, 2 * BAND), :]
                v2 = vd[pl.ds(base, 2 * BAND), :]
                dav = dad[pl.ds(base, BAND), :]
                dlv = dld[pl.ds(base, BAND), :]
                lsv = lsd[pl.ds(base, BAND), :]
                zero = jnp.zeros_like(q)
                qs = jnp.concatenate([jnp.where(lo, q, zero), jnp.where(lo, zero, q)], axis=0)
                das = jnp.concatenate([jnp.where(lo, dav, zero), jnp.where(lo, zero, dav)], axis=0)
                ls_col = jnp.concatenate([lsv[:, 0:1], lsv[:, HEAD_DIM:HEAD_DIM + 1]], axis=0)
                dl_col = jnp.concatenate([dlv[:, 0:1], dlv[:, HEAD_DIM:HEAD_DIM + 1]], axis=0)
                sc = jnp.where(_band_mask(b), _dot_nt(qs, k2), -1e30)
                p = jnp.exp(sc - ls_col)
                dp = _dot_nt(das, v2)
                ds = (p * (dp - dl_col)).astype(BF16)
                dvd[pl.ds(base, 2 * BAND), :] += _dot_tn(p.astype(BF16), das)
                dkd[pl.ds(base, 2 * BAND), :] += _dot_tn(ds, qs)
                dq = _dot(ds, k2)
                dqd[pl.ds(base, BAND), :] = jnp.where(lo, dq[:BAND], dq[BAND:])
                return 0
            return lax.fori_loop(0, nb, per_b, 0)
        lax.fori_loop(0, d, per_r, 0)

        def store_cols(col0):
            stb[...] = st[...].astype(BF16)
            cp = pltpu.make_async_copy(
                stb, dp_out.at[:, pl.ds(pl.multiple_of(col0 + LANE * (4 * g + hp), LANE), LANE)], sem)
            cp.start()
            cp.wait()

        def norm_back(src_ref, dy_ref, dy_off, w_ref, scale, dw_ref, col0):
            def per_r2(r, acc):
                def per_c(ci, acc):
                    tok = pl.ds(r + ci * (chunk * d), chunk, stride=d)
                    t = src_ref[tok, :]
                    dy = dy_ref[pl.ds(pl.multiple_of(dy_off + r * sub_len + ci * chunk, BAND), chunk), :]
                    rr = lax.rsqrt(_head_sums(t * t, lo) * (1.0 / HEAD_DIM) + EPS)
                    nrm = t * rr
                    acc = acc + jnp.sum(dy * nrm, axis=0, keepdims=True) * scale
                    dn = dy * (w_ref[...] * scale)
                    st[tok, :] = rr * (dn - nrm * (_head_sums(dn * nrm, lo) * (1.0 / HEAD_DIM)))
                    return acc
                return lax.fori_loop(0, sub_len // chunk, per_c, acc)
            acc = lax.fori_loop(0, d, per_r2, jnp.zeros((1, LANE), F32))
            dw_ref[...] += jnp.broadcast_to(acc, dw_ref.shape)
            store_cols(col0)

        @pl.when(hp == 0)
        def _():
            dqw_ref[...] = jnp.zeros_like(dqw_ref)
            dkw_ref[...] = jnp.zeros_like(dkw_ref)

        norm_back(q_ref, dqd, 0, qw_ref, HEAD_DIM ** -0.5, dqw_ref, Q0)
        norm_back(k_ref, dkd, BAND, kw_ref, 1.0, dkw_ref, K0)

        def v_back(r, _):
            def per_c(ci, _):
                src = pl.ds(pl.multiple_of(BAND + r * sub_len + ci * chunk, BAND), chunk)
                st[pl.ds(r + ci * (chunk * d), chunk, stride=d), :] = dvd[src, :]
                return 0
            return lax.fori_loop(0, sub_len // chunk, per_c, 0)
        lax.fori_loop(0, d, v_back, 0)
        store_cols(V0)

    col = lambda off: pl.BlockSpec((s, LANE), lambda hp, off=off: (0, off // LANE + 4 * g + hp))
    mid = pl.BlockSpec((s, LANE), lambda hp: (0, hp))
    vec = pl.BlockSpec((1, LANE), lambda hp: (0, 0))
    acc = pl.BlockSpec((8, LANE), lambda hp: (0, 0))
    any_ = pl.BlockSpec(memory_space=pl.ANY)
    return pl.pallas_call(
        body, name=f"attn_bwd{g}", grid=(ATTN_W // LANE,),
        in_specs=[col(Q0), col(K0), col(V0), mid, mid, mid, vec, vec, any_],
        out_specs=[any_, acc, acc],
        out_shape=[SDS(dproj.shape, dproj.dtype), SDS((8, LANE), F32), SDS((8, LANE), F32)],
        input_output_aliases={8: 0},
        scratch_shapes=[pltpu.VMEM((s, LANE), BF16), pltpu.VMEM((s + BAND, LANE), BF16), pltpu.VMEM((s + BAND, LANE), BF16),
                        pltpu.VMEM((s, LANE), BF16), pltpu.VMEM((s, LANE), F32), pltpu.VMEM((s, LANE), F32),
                        pltpu.VMEM((s, LANE), F32), pltpu.VMEM((s + BAND, LANE), F32), pltpu.VMEM((s + BAND, LANE), F32),
                        pltpu.VMEM((s, LANE), F32), pltpu.VMEM((s, LANE), BF16), pltpu.SemaphoreType.DMA(())],
        compiler_params=_params(),
    )(proj, proj, proj, da, delta, lse, qw2, kw2, dproj)


def _silu_grad(z, sg):
    return sg * (1.0 + z * (1.0 - sg))


def _glu(u):
    a_h, b_h = u[:, :CONV_W], u[:, CONV_W:]
    sg = _sigmoid(b_h)
    return a_h, sg, a_h * sg


def _tail(x, tgt, proj, o3, l3, wa, wc, wo, gate, bga, bgc, convw, convb, lnw, lnb, bd):
    s = x.shape[0]
    tr = 256

    def body(x_ref, t_ref, za_ref, u_ref, uh_ref, zc_ref, g0_ref, g1_ref, g2_ref, g3_ref,
             o0_ref, o1_ref, o2_ref, l0_ref, l1_ref, l2_ref, wa_ref, wc_ref, wo_ref,
             gate_ref, bga_ref, bgc_ref, cw_ref, cb_ref, lnw_ref, lnb_ref, bd_ref,
             dout_ref, da_ref, dl_ref, lse_ref, dcv_ref, mt_ref, yat_ref, yct_ref, dmo_ref, dya_ref, dyc_ref, dp_ref,
             dgate_ref, dbg_ref, dlnw_ref, dlnb_ref, dcb_ref, loss_ref,
             ext, st_za, st_zc, st_g, sems):
        i = pl.program_id(0)

        @pl.when(i == 0)
        def _():
            for r in (dgate_ref, dbg_ref, dlnw_ref, dlnb_ref, dcb_ref, loss_ref):
                r[...] = jnp.zeros_like(r)

        def acc_rows(ref, v):
            ref[...] += jnp.broadcast_to(jnp.sum(v, axis=0, keepdims=True), ref.shape)

        la, lb, lc = l0_ref[...], l1_ref[...], l2_ref[...]
        mx = jnp.maximum(jnp.maximum(la, lb), lc)
        ea, eb, ec = jnp.exp(la - mx), jnp.exp(lb - mx), jnp.exp(lc - mx)
        den = ea + eb + ec
        inv = 1.0 / den
        attn = (ea * inv) * o0_ref[...] + (eb * inv) * o1_ref[...] + (ec * inv) * o2_ref[...]
        lse_ref[...] = mx + jnp.log(den)

        za = za_ref[...]
        sga = _sigmoid(za)
        sa = za * sga
        ya_in = attn * sa
        y_attn = _dot(ya_in.astype(BF16), wa_ref[...])

        _, _, glu = _glu(u_ref[...])
        _, _, glu_h = _glu(uh_ref[...])
        ext[0:CONV_HALO, :] = jnp.where(i > 0, glu_h, 0.0)
        ext[CONV_HALO:CONV_HALO + tr, :] = glu
        cv = jnp.broadcast_to(cb_ref[...], (tr, CONV_W))
        for j in range(CONV_K):
            off = CONV_HALO - (CONV_K - 1) + j
            cv = cv + cw_ref[j:j + 1, :] * ext[off:off + tr, :]
        mu = jnp.mean(cv, axis=-1, keepdims=True)
        xc = cv - mu
        rstd = lax.rsqrt(jnp.mean(xc * xc, axis=-1, keepdims=True) + EPS)
        nrm = xc * rstd
        ln = nrm * lnw_ref[...] + lnb_ref[...]
        sgl = _sigmoid(ln)
        cs = ln * sgl
        zc = zc_ref[...]
        sgc = _sigmoid(zc)
        scz = zc * sgc
        yc_in = cs * scz
        y_conv = _dot(yc_in.astype(BF16), wc_ref[...])

        ga = _sigmoid(jnp.concatenate([g0_ref[...], g1_ref[...]], axis=1) + bga_ref[...])
        gc = _sigmoid(jnp.concatenate([g2_ref[...], g3_ref[...]], axis=1) + bgc_ref[...])
        merged = ga * y_attn + gc * y_conv
        mo = _dot(merged.astype(BF16), wo_ref[...])
        gate_v = gate_ref[...]
        err = (x_ref[...] + gate_v * mo) - t_ref[...]
        loss_ref[...] += 0.5 * jnp.sum(jnp.mean(err * err, axis=-1, keepdims=True))
        d_out = err * (1.0 / D_MODEL)
        dout_ref[...] = d_out

        acc_rows(dgate_ref, d_out * mo)
        dmo_b = (d_out * gate_v).astype(BF16)
        dmo_ref[...] = dmo_b
        mt_ref[...] = merged.T.astype(BF16)
        d_merged = _dot_nt(dmo_b, wo_ref[...])
        d_ya = (d_merged * ga).astype(BF16)
        d_yc = (d_merged * gc).astype(BF16)
        dya_ref[...] = d_ya
        dyc_ref[...] = d_yc
        dga = d_merged * y_attn * (ga * (1.0 - ga))
        dgc = d_merged * y_conv * (gc * (1.0 - gc))
        dgs = jnp.concatenate([dga, dgc], axis=1)
        acc_rows(dbg_ref, dgs)
        st_g[...] = dgs.astype(BF16)

        yat_ref[...] = ya_in.T.astype(BF16)
        d_ya_in = _dot_nt(d_ya, wa_ref[...])
        d_attn = d_ya_in * sa
        da_ref[...] = d_attn
        st_za[...] = (d_ya_in * attn * _silu_grad(za, sga)).astype(BF16)
        prod = d_attn * attn
        hi = prod.astype(BF16)
        lo_ = (prod - hi.astype(F32)).astype(BF16)
        dl_ref[...] = _dot(hi, bd_ref[...]) + _dot(lo_, bd_ref[...])

        yct_ref[...] = yc_in.T.astype(BF16)
        d_yc_in = _dot_nt(d_yc, wc_ref[...])
        st_zc[...] = (d_yc_in * cs * _silu_grad(zc, sgc)).astype(BF16)
        d_ln = (d_yc_in * scz) * _silu_grad(ln, sgl)
        acc_rows(dlnw_ref, d_ln * nrm)
        acc_rows(dlnb_ref, d_ln)
        d_nrm = d_ln * lnw_ref[...]
        d_cv = rstd * (d_nrm - jnp.mean(d_nrm, axis=-1, keepdims=True)
                       - nrm * jnp.mean(d_nrm * nrm, axis=-1, keepdims=True))
        acc_rows(dcb_ref, d_cv)
        dcv_ref[...] = d_cv

        rows = pl.ds(pl.multiple_of(i * tr, tr), tr)
        cps = [pltpu.make_async_copy(st_za, dp_ref.at[rows, pl.ds(ZA0, ATTN_W)], sems.at[0]),
               pltpu.make_async_copy(st_zc, dp_ref.at[rows, pl.ds(ZC0, CONV_W)], sems.at[1]),
               pltpu.make_async_copy(st_g, dp_ref.at[rows, pl.ds(G0, 2 * D_MODEL)], sems.at[2])]
        for cp in cps:
            cp.start()
        for cp in cps:
            cp.wait()

    def rows(width, colblk=0):
        return pl.BlockSpec((tr, width), lambda i, colblk=colblk: (i, colblk))

    def const(shape):
        return pl.BlockSpec(shape, lambda i: (0,) * len(shape))

    halo = pl.BlockSpec((CONV_HALO, D_MODEL), lambda i: (jnp.maximum(i * (tr // CONV_HALO) - 1, 0), U0 // D_MODEL))
    in_specs = [rows(D_MODEL), rows(D_MODEL), rows(ATTN_W, ZA0 // ATTN_W), rows(D_MODEL, U0 // D_MODEL), halo,
                rows(CONV_W, ZC0 // CONV_W)]
    in_specs += [rows(512, G0 // 512 + j) for j in range(4)]
    in_specs += [rows(ATTN_W)] * 6
    in_specs += [const(wa.shape), const(wc.shape), const(wo.shape), const((1, D_MODEL)), const((1, D_MODEL)),
                 const((1, D_MODEL)), const(convw.shape), const((1, CONV_W)), const((1, CONV_W)), const((1, CONV_W)),
                 const(bd.shape)]
    tcol = lambda width: pl.BlockSpec((width, tr), lambda i: (0, i))
    out_specs = [rows(D_MODEL), rows(ATTN_W), rows(ATTN_W), rows(ATTN_W), rows(CONV_W),
                 tcol(D_MODEL), tcol(ATTN_W), tcol(CONV_W), rows(D_MODEL), rows(D_MODEL), rows(D_MODEL),
                 pl.BlockSpec(memory_space=pl.ANY),
                 const((8, D_MODEL)), const((8, 2 * D_MODEL)), const((8, CONV_W)), const((8, CONV_W)), const((8, CONV_W)),
                 const((8, LANE))]
    out_shape = [SDS((s, D_MODEL), F32), SDS((s, ATTN_W), F32), SDS((s, ATTN_W), F32), SDS((s, ATTN_W), F32),
                 SDS((s, CONV_W), F32),
                 SDS((D_MODEL, s), BF16), SDS((ATTN_W, s), BF16), SDS((CONV_W, s), BF16),
                 SDS((s, D_MODEL), BF16), SDS((s, D_MODEL), BF16), SDS((s, D_MODEL), BF16),
                 SDS((s, IN_W), BF16),
                 SDS((8, D_MODEL), F32), SDS((8, 2 * D_MODEL), F32), SDS((8, CONV_W), F32), SDS((8, CONV_W), F32),
                 SDS((8, CONV_W), F32), SDS((8, LANE), F32)]
    return pl.pallas_call(
        body, name="tail", grid=(s // tr,), in_specs=in_specs, out_specs=out_specs, out_shape=out_shape,
        scratch_shapes=[pltpu.VMEM((CONV_HALO + tr, CONV_W), F32), pltpu.VMEM((tr, ATTN_W), BF16),
                        pltpu.VMEM((tr, CONV_W), BF16), pltpu.VMEM((tr, 2 * D_MODEL), BF16),
                        pltpu.SemaphoreType.DMA((3,))],
        compiler_params=_params(),
    )(x, tgt, proj, proj, proj, proj, proj, proj, proj, proj, *o3, *l3, wa, wc, wo, gate, bga, bgc,
      convw, convb, lnw, lnb, bd)


def _conv_bwd(dcv, proj, convw, dproj):
    s = dcv.shape[0]
    tr = 256
    nt = s // tr

    def body(dcv_ref, dcvn_ref, u_ref, uh_ref, cw_ref, dp_in, dp_out, dw_ref, extg, extd):
        del dp_in
        i = pl.program_id(0)

        @pl.when(i == 0)
        def _():
            dw_ref[...] = jnp.zeros_like(dw_ref)

        a_h, sgb, glu = _glu(u_ref[...])
        _, _, glu_h = _glu(uh_ref[...])
        extg[0:CONV_HALO, :] = jnp.where(i > 0, glu_h, 0.0)
        extg[CONV_HALO:CONV_HALO + tr, :] = glu
        dcv_v = dcv_ref[...]
        extd[0:tr, :] = dcv_v
        extd[tr:tr + CONV_HALO, :] = jnp.where(i < nt - 1, dcvn_ref[...], 0.0)
        dglu = jnp.zeros((tr, CONV_W), F32)
        for j in range(CONV_K):
            back = CONV_K - 1 - j
            dglu = dglu + cw_ref[j:j + 1, :] * extd[back:back + tr, :]
            off = CONV_HALO - (CONV_K - 1) + j
            dw_ref[j:j + 1, :] += jnp.sum(dcv_v * extg[off:off + tr, :], axis=0, keepdims=True)
        d_a = dglu * sgb
        d_b = dglu * a_h * (sgb * (1.0 - sgb))
        dp_out[...] = jnp.concatenate([d_a, d_b], axis=1).astype(BF16)

    ucol = U0 // D_MODEL
    return pl.pallas_call(
        body, name="conv_bwd", grid=(nt,),
        in_specs=[pl.BlockSpec((tr, CONV_W), lambda i: (i, 0)),
                  pl.BlockSpec((CONV_HALO, CONV_W), lambda i: (jnp.minimum((i + 1) * (tr // CONV_HALO), s // CONV_HALO - 1), 0)),
                  pl.BlockSpec((tr, D_MODEL), lambda i: (i, ucol)),
                  pl.BlockSpec((CONV_HALO, D_MODEL), lambda i: (jnp.maximum(i * (tr // CONV_HALO) - 1, 0), ucol)),
                  pl.BlockSpec(convw.shape, lambda i: (0, 0)),
                  pl.BlockSpec(memory_space=pl.ANY)],
        out_specs=[pl.BlockSpec((tr, D_MODEL), lambda i: (i, ucol)), pl.BlockSpec((CONV_HALO, CONV_W), lambda i: (0, 0))],
        out_shape=[SDS(dproj.shape, dproj.dtype), SDS((CONV_HALO, CONV_W), F32)],
        input_output_aliases={5: 0},
        scratch_shapes=[pltpu.VMEM((CONV_HALO + tr, CONV_W), F32), pltpu.VMEM((CONV_HALO + tr, CONV_W), F32)],
        compiler_params=_params(),
    )(dcv, dcv, proj, proj, convw, dproj)


def _mm_acc(at, b, name, col_slots):
    m, s = at.shape
    n = b.shape[1]
    tk = 512
    nk = s // tk

    def body(a_ref, b_ref, o_ref, acc):
        k = pl.program_id(0)

        @pl.when(k == 0)
        def _():
            acc[...] = jnp.zeros_like(acc)

        acc[...] += _dot(a_ref[...], b_ref[...])

        @pl.when(k == nk - 1)
        def _():
            if col_slots:
                w = n // N_DEV
                for j in range(N_DEV):
                    o_ref[j] = acc[:, j * w:(j + 1) * w].astype(BF16)
            else:
                o_ref[...] = acc[...].astype(BF16)

    if col_slots:
        out_shape = SDS((N_DEV, m, n // N_DEV), BF16)
        out_spec = pl.BlockSpec((N_DEV, m, n // N_DEV), lambda k: (0, 0, 0))
    else:
        out_shape = SDS((m, n), BF16)
        out_spec = pl.BlockSpec((m, n), lambda k: (0, 0))
    return pl.pallas_call(
        body, name=name, grid=(nk,),
        in_specs=[pl.BlockSpec((m, tk), lambda k: (0, k)), pl.BlockSpec((tk, n), lambda k: (k, 0))],
        out_specs=out_spec, out_shape=out_shape, scratch_shapes=[pltpu.VMEM((m, n), F32)],
        compiler_params=_params(),
    )(at, b)


def _mm_dw(ht, dproj):
    s = ht.shape[1]
    tk = 512
    nk = s // tk

    def body(a_ref, b_ref, o_ref, acc):
        k = pl.program_id(1)

        @pl.when(k == 0)
        def _():
            acc[...] = jnp.zeros_like(acc)

        acc[...] += _dot(a_ref[...], b_ref[...])

        @pl.when(k == nk - 1)
        def _():
            o_ref[0] = acc[:, 0:WIN_W].astype(BF16)
            o_ref[1] = acc[:, PAIR_W - WIN_W:PAIR_W].astype(BF16)

    return pl.pallas_call(
        body, name="mm_dw", grid=(N_DEV // 2, nk),
        in_specs=[pl.BlockSpec((D_MODEL, tk), lambda p, k: (0, k)), pl.BlockSpec((tk, PAIR_W), lambda p, k: (k, p))],
        out_specs=pl.BlockSpec((2, D_MODEL, WIN_W), lambda p, k: (p, 0, 0)),
        out_shape=SDS((N_DEV, D_MODEL, WIN_W), BF16), scratch_shapes=[pltpu.VMEM((D_MODEL, PAIR_W), F32)],
        compiler_params=_params(),
    )(ht, dproj)


def _mm_dh(dproj, wg):
    s = dproj.shape[0]
    tm = 1024

    def body(dp_ref, w_ref, o_ref):
        p = pl.program_id(1)
        part = (_dot_nt(dp_ref[:, 0:WIN_W], w_ref[0]) + _dot_nt(dp_ref[:, PAIR_W - WIN_W:PAIR_W], w_ref[1]))

        @pl.when(p == 0)
        def _():
            o_ref[...] = part

        @pl.when(p > 0)
        def _():
            o_ref[...] += part

    return pl.pallas_call(
        body, name="mm_dh", grid=(s // tm, N_DEV // 2),
        in_specs=[pl.BlockSpec((tm, PAIR_W), lambda m, p: (m, p)),
                  pl.BlockSpec((2, D_MODEL, WIN_W), lambda m, p: (p, 0, 0))],
        out_specs=pl.BlockSpec((tm, D_MODEL), lambda m, p: (m, 0)),
        out_shape=SDS((s, D_MODEL), F32), compiler_params=_params(),
    )(dproj, wg)


def _norm_bwd(x, dh, dout, norm_w, scale):
    s = x.shape[0]
    tr = 512

    def body(x_ref, dh_ref, do_ref, nw_ref, sc_ref, gx_ref, dsh_ref, dsc_ref, dnw_ref):
        i = pl.program_id(0)

        @pl.when(i == 0)
        def _():
            for r in (dsh_ref, dsc_ref, dnw_ref):
                r[...] = jnp.zeros_like(r)

        def acc_rows(ref, v):
            ref[...] += jnp.broadcast_to(jnp.sum(v, axis=0, keepdims=True), ref.shape)

        xv = x_ref[...]
        dh_v = dh_ref[...]
        r = lax.rsqrt(jnp.mean(xv * xv, axis=-1, keepdims=True) + EPS)
        xn = xv * r
        one_sc = 1.0 + sc_ref[...]
        acc_rows(dsh_ref, dh_v)
        acc_rows(dsc_ref, dh_v * (xn * nw_ref[...]))
        acc_rows(dnw_ref, dh_v * xn * one_sc)
        dxn = dh_v * (nw_ref[...] * one_sc)
        gx_ref[...] = do_ref[...] + r * (dxn - xn * jnp.mean(dxn * xn, axis=-1, keepdims=True))

    blk = pl.BlockSpec((tr, D_MODEL), lambda i: (i, 0))
    vec = pl.BlockSpec((1, D_MODEL), lambda i: (0, 0))
    acc = pl.BlockSpec((8, D_MODEL), lambda i: (0, 0))
    return pl.pallas_call(
        body, name="norm_bwd", grid=(s // tr,), in_specs=[blk, blk, blk, vec, vec],
        out_specs=[blk, acc, acc, acc],
        out_shape=[SDS((s, D_MODEL), F32)] + [SDS((8, D_MODEL), F32)] * 3, compiler_params=_params(),
    )(x, dh, dout, norm_w, scale)


def _adamw(gsrc, w, m, v, name, stacked, c_arr=None):
    rows, cols = w.shape
    tr = rows if rows <= 128 else 128
    unshift = c_arr is not None
    bc1 = 1.0 - ADAM_B1 ** ADAM_STEP
    bc2 = 1.0 - ADAM_B2 ** ADAM_STEP

    def body(*refs):
        if unshift:
            c_ref, refs = refs[0], refs[1:]
        g_ref, w_ref, m_ref, v_ref, go_ref, d_ref, mo_ref, vo_ref = refs
        if stacked:
            g = g_ref[0].astype(F32)
            for j in range(1, N_DEV):
                g = g + g_ref[j].astype(F32)
        else:
            g = g_ref[...]
        if unshift:
            g = jnp.where(c_ref[0] == 1, pltpu.roll(g, SHARD_W, axis=1), g)[:, :SHARD_W]
        m_new = ADAM_B1 * m_ref[...] + (1.0 - ADAM_B1) * g
        v_new = ADAM_B2 * v_ref[...] + (1.0 - ADAM_B2) * (g * g)
        m_hat = m_new / bc1
        v_hat = v_new / bc2
        go_ref[...] = g
        d_ref[...] = -ADAM_LR * (m_hat / (jnp.sqrt(v_hat) + ADAM_EPS) + ADAM_WD * w_ref[...])
        mo_ref[...] = m_new
        vo_ref[...] = v_new

    blk = pl.BlockSpec((tr, cols), lambda i: (i, 0))
    if stacked:
        gcols = gsrc.shape[2]
        gspec = pl.BlockSpec((N_DEV, tr, gcols), lambda i: (0, i, 0))
    else:
        gspec = blk
    in_specs = [gspec, blk, blk, blk]
    args = [gsrc, w, m, v]
    if unshift:
        in_specs = [pl.BlockSpec(memory_space=pltpu.SMEM)] + in_specs
        args = [c_arr] + args
    return pl.pallas_call(
        body, name=name, grid=(rows // tr,), in_specs=in_specs, out_specs=[blk] * 4,
        out_shape=[SDS((rows, cols), F32)] * 4, compiler_params=_params(),
    )(*args)


def _pack_small(parts):
    cols = []
    for name, length in _SMALL:
        p = parts[name].reshape(1, -1)
        cols.append(jnp.pad(p, ((0, 0), (0, length - p.shape[1]))))
    return jnp.concatenate(cols, axis=1)


def _unpack_small(vec, name, n):
    off, _ = SMALL_OFF[name]
    return vec[:, off:off + n]


def kernel(x, c, w_ada, b_ada, norm_w, w_in, b_gate, q_norm_w, k_norm_w, w_attn_proj, conv_w, conv_b, conv_ln_w, conv_ln_b, w_conv_proj, w_out, loss_target, m_w_ada, m_b_ada, m_norm_w, m_w_in, m_b_gate, m_q_norm_w, m_k_norm_w, m_w_attn_proj, m_conv_w, m_conv_b, m_conv_ln_w, m_conv_ln_b, m_w_conv_proj, m_w_out, v_w_ada, v_b_ada, v_norm_w, v_w_in, v_b_gate, v_q_norm_w, v_k_norm_w, v_w_attn_proj, v_conv_w, v_conv_b, v_conv_ln_w, v_conv_ln_b, v_w_conv_proj, v_w_out):
    xi, yi, ci = lax.axis_index("x"), lax.axis_index("y"), lax.axis_index("c")
    me = 4 * xi + 2 * yi + ci
    c_arr = jnp.reshape(ci, (1,)).astype(jnp.int32)
    x2, tgt2 = x[0], loss_target[0]
    s = x2.shape[0]

    cw_flat = jnp.pad(conv_w[0].reshape(1, -1), ((0, 0), (0, CONVW_FLAT - CONV_K * HEAD_DIM)))
    pre = jnp.concatenate([c, cw_flat], axis=1).reshape(8, -1)
    (pre_all,) = _all_gather([pre], "gather_c_convw", vmem=True)
    pre_all = pre_all.reshape(N_DEV, -1)
    c_all = pre_all[:, :D_MODEL]
    convw_full = pre_all[:, D_MODEL:D_MODEL + CONV_K * HEAD_DIM].reshape(N_DEV, CONV_K, HEAD_DIM)
    convw_full = jnp.transpose(convw_full, (1, 0, 2)).reshape(CONV_K, CONV_W)
    convw_pad = jnp.pad(convw_full, ((0, CONV_HALO - CONV_K), (0, 0)))

    ada_part = _ada_fwd(c_all, w_ada[0])
    (ada_all,) = _all_gather([ada_part], "gather_ada", vmem=True)
    ada = lax.dynamic_index_in_dim(ada_all, me, axis=1, keepdims=False).reshape(1, 3 * D_MODEL) + b_ada
    shift, scale, gate = ada[:, :D_MODEL], ada[:, D_MODEL:2 * D_MODEL], ada[:, 2 * D_MODEL:]

    wg, wa_g, wc_g, wo_g = _all_gather(
        [_to_window(w_in[0], c_arr), _cast_bf16(w_attn_proj[0], "cast_wa"), _cast_bf16(w_conv_proj[0], "cast_wc"),
         _cast_bf16(w_out[0], "cast_wo")], "gather_weights", vmem=False)
    wa = _cols_from_slots(wa_g, "cols_wa")
    wc = _cols_from_slots(wc_g, "cols_wc")
    wo = wo_g.reshape(D_MODEL, D_MODEL)

    h, ht = _norm_fwd(x2, norm_w, scale, shift)
    proj = _mm_in(h, wg)
    qw2 = jnp.tile(q_norm_w, (1, 2))
    kw2 = jnp.tile(k_norm_w, (1, 2))
    o3, l3 = [], []
    for g in range(N_GROUPS):
        o_g, l_g = _attn_fwd(proj, qw2, kw2, g)
        o3.append(o_g)
        l3.append(l_g)
    head_id = jnp.arange(ATTN_W) // HEAD_DIM
    bd = (head_id[:, None] == head_id[None, :]).astype(BF16)
    (dout, da, delta, lse, dcv, mt, yat, yct, dmo, dya, dyc, dproj,
     dgate, dbg, dlnw, dlnb, dcb, loss_p) = _tail(
        x2, tgt2, proj, o3, l3, wa, wc, wo, gate, b_gate[:, :D_MODEL], b_gate[:, D_MODEL:], convw_pad,
        conv_b, conv_ln_w, conv_ln_b, bd)

    dproj, dconvw = _conv_bwd(dcv, proj, convw_pad, dproj)
    dqw = jnp.zeros((1, HEAD_DIM), F32)
    dkw = jnp.zeros((1, HEAD_DIM), F32)
    for g in range(N_GROUPS):
        dproj, dqw_g, dkw_g = _attn_bwd(proj, da, delta, lse, qw2, kw2, dproj, g)
        dqw = dqw + dqw_g[0:1, :HEAD_DIM] + dqw_g[0:1, HEAD_DIM:]
        dkw = dkw + dkw_g[0:1, :HEAD_DIM] + dkw_g[0:1, HEAD_DIM:]
    dh = _mm_dh(dproj, wg)
    gx, dsh, dsc, dnw = _norm_bwd(x2, dh, dout, norm_w, scale)
    dw_in_p = _mm_dw(ht, dproj)
    dwo_p = _mm_acc(mt, dmo, "mm_dwo", col_slots=False).reshape(N_DEV, D_MODEL // N_DEV, D_MODEL)
    dwa_p = _mm_acc(yat, dya, "mm_dwa", col_slots=True)
    dwc_p = _mm_acc(yct, dyc, "mm_dwc", col_slots=True)

    r_in, r_wa, r_wc, r_wo = _exchange_slots([dw_in_p, dwa_p, dwc_p, dwo_p], "exchange_grads")
    d_ada = jnp.concatenate([dsh[0:1], dsc[0:1], dgate[0:1]], axis=1)
    small_p = _pack_small({"b_ada": d_ada, "norm_w": dnw[0:1], "b_gate": dbg[0:1], "q_norm_w": dqw, "k_norm_w": dkw,
                           "conv_b": dcb[0:1], "conv_ln_w": dlnw[0:1], "conv_ln_b": dlnb[0:1], "loss": loss_p[0:1]})
    small_all, dconvw_all = _all_gather([jnp.broadcast_to(small_p, (8, SMALL_N)), dconvw], "gather_small", vmem=True)
    small_all = small_all[:, 0:1, :]
    dcw_mine = lax.dynamic_slice_in_dim(dconvw_all[:, :CONV_K, :], me * HEAD_DIM, HEAD_DIM, axis=2)
    dcw_mine = jnp.pad(dcw_mine.reshape(N_DEV, 1, -1), ((0, 0), (0, 0), (0, CONVW_FLAT - CONV_K * HEAD_DIM)))
    pack_g = jnp.concatenate([small_all, dcw_mine], axis=2)

    def pack_params(tree):
        small = _pack_small({"b_ada": tree["b_ada"], "norm_w": tree["norm_w"], "b_gate": tree["b_gate"],
                             "q_norm_w": tree["q_norm_w"], "k_norm_w": tree["k_norm_w"], "conv_b": tree["conv_b"],
                             "conv_ln_w": tree["conv_ln_w"], "conv_ln_b": tree["conv_ln_b"],
                             "loss": jnp.ones((1, 1), F32)})
        cw = jnp.pad(tree["conv_w"][0].reshape(1, -1), ((0, 0), (0, CONVW_FLAT - CONV_K * HEAD_DIM)))
        return jnp.concatenate([small, cw], axis=1)

    names = ("b_ada", "norm_w", "b_gate", "q_norm_w", "k_norm_w", "conv_b", "conv_ln_w", "conv_ln_b", "conv_w")
    w_tree = dict(zip(names, (b_ada, norm_w, b_gate, q_norm_w, k_norm_w, conv_b, conv_ln_w, conv_ln_b, conv_w)))
    m_tree = dict(zip(names, (m_b_ada, m_norm_w, m_b_gate, m_q_norm_w, m_k_norm_w, m_conv_b, m_conv_ln_w, m_conv_ln_b, m_conv_w)))
    v_tree = dict(zip(names, (v_b_ada, v_norm_w, v_b_gate, v_q_norm_w, v_k_norm_w, v_conv_b, v_conv_ln_w, v_conv_ln_b, v_conv_w)))
    pk = _adamw(pack_g, pack_params(w_tree), pack_params(m_tree), pack_params(v_tree), "adamw_small", stacked=True)

    d_ada_all = small_all[:, 0, :3 * D_MODEL]
    d_ada_cols = lax.dynamic_slice_in_dim(d_ada_all, me * (3 * D_MODEL // N_DEV), 3 * D_MODEL // N_DEV, axis=1)
    g_wada = _ada_bwd(c_all, d_ada_cols)
    r_ada = _adamw(g_wada, w_ada[0], m_w_ada[0], v_w_ada[0], "adamw_w_ada", stacked=False)
    r_win = _adamw(r_in, w_in[0], m_w_in[0], v_w_in[0], "adamw_w_in", stacked=True, c_arr=c_arr)
    r_wap = _adamw(r_wa, w_attn_proj[0], m_w_attn_proj[0], v_w_attn_proj[0], "adamw_w_attn_proj", stacked=True)
    r_wcp = _adamw(r_wc, w_conv_proj[0], m_w_conv_proj[0], v_w_conv_proj[0], "adamw_w_conv_proj", stacked=True)
    r_wout = _adamw(r_wo, w_out[0], m_w_out[0], v_w_out[0], "adamw_w_out", stacked=True)

    def small_out(k, name, n):
        return _unpack_small(pk[k], name, n)

    def convw_out(k):
        return pk[k][:, SMALL_N:SMALL_N + CONV_K * HEAD_DIM].reshape(1, CONV_K, HEAD_DIM)

    loss = pk[0][0, SMALL_OFF["loss"][0]]
    outs = [loss, gx[None]]
    for k in range(4):
        outs += [r_ada[k][None], small_out(k, "b_ada", 3 * D_MODEL), small_out(k, "norm_w", D_MODEL), r_win[k][None],
                 small_out(k, "b_gate", 2 * D_MODEL), small_out(k, "q_norm_w", HEAD_DIM), small_out(k, "k_norm_w", HEAD_DIM),
                 r_wap[k][None], convw_out(k), small_out(k, "conv_b", CONV_W), small_out(k, "conv_ln_w", CONV_W),
                 small_out(k, "conv_ln_b", CONV_W), r_wcp[k][None], r_wout[k][None]]
    return tuple(outs)
```

```python
import functools

import jax
import jax.numpy as jnp
from jax import lax
from jax.experimental import pallas as pl
from jax.experimental.pallas import tpu as pltpu

F32 = jnp.float32
BF16 = jnp.bfloat16
SDS = jax.ShapeDtypeStruct
MESH = pl.DeviceIdType.MESH

N_DEV = 8
D_MODEL = 1024
HEAD_DIM = 64
N_GROUPS = 3
DILATIONS = (1, 4, 16)
BAND = 128
ATTN_W = 512
CONV_W = 512
CONV_K = 31
CONV_HALO = 32
IN_W = 8704
SHARD_W = IN_W // N_DEV
WIN_W = 1152
PAIR_W = 2 * SHARD_W
Q0, K0, V0, ZA0, U0, ZC0, G0 = 0, 1536, 3072, 4608, 5120, 6144, 6656
EPS = 1e-6
LANE = 128
VMEM_LIMIT = 56 * 1024 * 1024

ADAM_LR, ADAM_B1, ADAM_B2, ADAM_EPS, ADAM_WD, ADAM_STEP = 0.001, 0.9, 0.999, 1e-08, 0.01, 10

_SMALL = (("b_ada", 3072), ("norm_w", 1024), ("b_gate", 2048), ("q_norm_w", 128), ("k_norm_w", 128),
          ("conv_b", 512), ("conv_ln_w", 512), ("conv_ln_b", 512), ("loss", 128))
SMALL_OFF = {}
_o = 0
for _n, _l in _SMALL:
    SMALL_OFF[_n] = (_o, _l)
    _o += _l
SMALL_N = _o
CONVW_FLAT = 2048
PACK_N = SMALL_N + CONVW_FLAT


def _params(**kw):
    return pltpu.CompilerParams(vmem_limit_bytes=VMEM_LIMIT, **kw)


def _sigmoid(z):
    return 1.0 / (1.0 + jnp.exp(-z))


def _dot(a, b):
    return jnp.dot(a, b, preferred_element_type=F32)


def _dot_nt(a, b):
    return lax.dot_general(a, b, (((1,), (1,)), ((), ())), preferred_element_type=F32)


def _dot_tn(a, b):
    return lax.dot_general(a, b, (((0,), (0,)), ((), ())), preferred_element_type=F32)


def _peer(x, y, c, k):
    px = 1 - x if (k >> 2) & 1 else x
    py = 1 - y if (k >> 1) & 1 else y
    pc = 1 - c if k & 1 else c
    return (px, py, pc), 4 * px + 2 * py + pc


def _all_gather(arrays, name, vmem):
    n = len(arrays)
    space = pltpu.VMEM if vmem else pl.ANY

    def body(*refs):
        ins, outs = refs[:n], refs[n:2 * n]
        send_sems, recv_sems, local_sems = refs[2 * n:]
        x, y, c = lax.axis_index("x"), lax.axis_index("y"), lax.axis_index("c")
        me = 4 * x + 2 * y + c
        locals_ = [pltpu.make_async_copy(ins[a], outs[a].at[me], local_sems.at[a]) for a in range(n)]
        for cp in locals_:
            cp.start()
        sends = []
        for k in range(1, N_DEV):
            peer, _ = _peer(x, y, c, k)
            for a in range(n):
                cp = pltpu.make_async_remote_copy(
                    src_ref=ins[a], dst_ref=outs[a].at[me], send_sem=send_sems.at[a, k - 1],
                    recv_sem=recv_sems.at[a, k - 1], device_id=peer, device_id_type=MESH)
                cp.start()
                sends.append(cp)
        for k in range(1, N_DEV):
            peer, pidx = _peer(x, y, c, k)
            for a in range(n):
                pltpu.make_async_remote_copy(
                    src_ref=ins[a], dst_ref=outs[a].at[pidx], send_sem=send_sems.at[a, k - 1],
                    recv_sem=recv_sems.at[a, k - 1], device_id=peer, device_id_type=MESH).wait_recv()
        for cp in sends:
            cp.wait_send()
        for cp in locals_:
            cp.wait()

    return pl.pallas_call(
        body, name=name,
        out_shape=[SDS((N_DEV,) + a.shape, a.dtype) for a in arrays],
        in_specs=[pl.BlockSpec(memory_space=space)] * n,
        out_specs=[pl.BlockSpec(memory_space=space)] * n,
        scratch_shapes=[pltpu.SemaphoreType.DMA((n, N_DEV - 1)), pltpu.SemaphoreType.DMA((n, N_DEV - 1)),
                        pltpu.SemaphoreType.DMA((n,))],
        compiler_params=_params(),
    )(*arrays)


CHIP_K = (2, 4, 6)


def _all_gather_chips(arrays, name):
    n = len(arrays)

    def body(*refs):
        ins, outs = refs[:n], refs[n:2 * n]
        send_sems, recv_sems, local_sems = refs[2 * n:]
        x, y, c = lax.axis_index("x"), lax.axis_index("y"), lax.axis_index("c")
        me = 4 * x + 2 * y + c
        sib, sib_idx = _peer(x, y, c, 1)

        def copy(a, slot, block, to, src=None):
            return pltpu.make_async_remote_copy(
                src_ref=outs[a].at[block] if src is None else src, dst_ref=outs[a].at[block],
                send_sem=send_sems.at[a, slot], recv_sem=recv_sems.at[a, slot], device_id=to, device_id_type=MESH)

        locals_ = [pltpu.make_async_copy(ins[a], outs[a].at[me], local_sems.at[a]) for a in range(n)]
        for cp in locals_:
            cp.start()
        sends = [copy(a, 0, me, sib, src=ins[a]) for a in range(n)]
        for j, k in enumerate(CHIP_K):
            peer, _ = _peer(x, y, c, k)
            sends += [copy(a, 1 + j, me, peer, src=ins[a]) for a in range(n)]
        for cp in sends:
            cp.start()
        for j, k in enumerate(CHIP_K):
            peer, pidx = _peer(x, y, c, k)
            for a in range(n):
                copy(a, 1 + j, pidx, peer).wait_recv()
                fwd = copy(a, 4 + j, pidx, sib)
                fwd.start()
                sends.append(fwd)
        for a in range(n):
            copy(a, 0, sib_idx, sib).wait_recv()
        for j, k in enumerate(CHIP_K):
            _, pidx = _peer(x, y, 1 - c, k)
            for a in range(n):
                copy(a, 4 + j, pidx, sib).wait_recv()
        for cp in sends:
            cp.wait_send()
        for cp in locals_:
            cp.wait()

    return pl.pallas_call(
        body, name=name,
        out_shape=[SDS((N_DEV,) + a.shape, a.dtype) for a in arrays],
        in_specs=[pl.BlockSpec(memory_space=pl.ANY)] * n,
        out_specs=[pl.BlockSpec(memory_space=pl.ANY)] * n,
        scratch_shapes=[pltpu.SemaphoreType.DMA((n, N_DEV - 1)), pltpu.SemaphoreType.DMA((n, N_DEV - 1)),
                        pltpu.SemaphoreType.DMA((n,))],
        compiler_params=_params(),
    )(*arrays)


def _exchange_sibling(arrays, name):
    n = len(arrays)
    ks = (0,) + CHIP_K

    def body(*refs):
        ins, outs = refs[:n], refs[n:2 * n]
        send_sems, recv_sems = refs[2 * n:]
        x, y, c = lax.axis_index("x"), lax.axis_index("y"), lax.axis_index("c")
        sib, sib_idx = _peer(x, y, c, 1)
        sends = []
        for i, k in enumerate(ks):
            _, tgt = _peer(x, y, 1 - c, k) if k else (None, sib_idx)
            for a in range(n):
                cp = pltpu.make_async_remote_copy(
                    src_ref=ins[a].at[tgt], dst_ref=outs[a].at[i], send_sem=send_sems.at[a, i],
                    recv_sem=recv_sems.at[a, i], device_id=sib, device_id_type=MESH)
                cp.start()
                sends.append(cp)
        for cp in sends:
            cp.wait_recv()
        for cp in sends:
            cp.wait_send()

    return pl.pallas_call(
        body, name=name,
        out_shape=[SDS((len(ks),) + a.shape[1:], a.dtype) for a in arrays],
        in_specs=[pl.BlockSpec(memory_space=pl.ANY)] * n,
        out_specs=[pl.BlockSpec(memory_space=pl.ANY)] * n,
        scratch_shapes=[pltpu.SemaphoreType.DMA((n, len(ks))), pltpu.SemaphoreType.DMA((n, len(ks)))],
        compiler_params=_params(),
    )(*arrays)


def _presum(mine, from_sib, me_arr, name):
    _, rows, cols = mine.shape
    tr = min(rows, 256)

    def body(me_ref, a_ref, b_ref, o_ref):
        del me_ref
        o_ref[...] = (a_ref[...].astype(F32) + b_ref[...].astype(F32)).astype(o_ref.dtype)

    grid_spec = pltpu.PrefetchScalarGridSpec(
        num_scalar_prefetch=1, grid=(len(CHIP_K), rows // tr),
        in_specs=[pl.BlockSpec((1, tr, cols), lambda j, i, me: (jnp.bitwise_xor(me[0], 2 * (j + 1)), i, 0)),
                  pl.BlockSpec((1, tr, cols), lambda j, i, me: (j + 1, i, 0))],
        out_specs=pl.BlockSpec((1, tr, cols), lambda j, i, me: (j, i, 0)))
    return pl.pallas_call(body, name=name, grid_spec=grid_spec, out_shape=SDS((len(CHIP_K), rows, cols), mine.dtype),
                          compiler_params=_params())(me_arr, mine, from_sib)


def _exchange_chips(mine, from_sib, presums, name):
    n = len(mine)

    def body(*refs):
        mine_r, sib_r, pre_r, outs = refs[:n], refs[n:2 * n], refs[2 * n:3 * n], refs[3 * n:4 * n]
        send_sems, recv_sems, local_sems = refs[4 * n:]
        x, y, c = lax.axis_index("x"), lax.axis_index("y"), lax.axis_index("c")
        me = 4 * x + 2 * y + c
        locals_ = []
        for a in range(n):
            locals_.append(pltpu.make_async_copy(mine_r[a].at[me], outs[a].at[0], local_sems.at[a, 0]))
            locals_.append(pltpu.make_async_copy(sib_r[a].at[0], outs[a].at[1], local_sems.at[a, 1]))
        for cp in locals_:
            cp.start()
        sends = []
        for j, k in enumerate(CHIP_K):
            peer, _ = _peer(x, y, c, k)
            for a in range(n):
                cp = pltpu.make_async_remote_copy(
                    src_ref=pre_r[a].at[j], dst_ref=outs[a].at[2 + j], send_sem=send_sems.at[a, j],
                    recv_sem=recv_sems.at[a, j], device_id=peer, device_id_type=MESH)
                cp.start()
                sends.append(cp)
        for cp in sends:
            cp.wait_recv()
        for cp in sends:
            cp.wait_send()
        for cp in locals_:
            cp.wait()

    nk = len(CHIP_K)
    return pl.pallas_call(
        body, name=name,
        out_shape=[SDS((2 + nk,) + a.shape[1:], a.dtype) for a in mine],
        in_specs=[pl.BlockSpec(memory_space=pl.ANY)] * (3 * n),
        out_specs=[pl.BlockSpec(memory_space=pl.ANY)] * n,
        scratch_shapes=[pltpu.SemaphoreType.DMA((n, nk)), pltpu.SemaphoreType.DMA((n, nk)),
                        pltpu.SemaphoreType.DMA((n, 2))],
        compiler_params=_params(),
    )(*mine, *from_sib, *presums)


def _to_window(w, c_arr):
    rows = w.shape[0]
    tr = 256

    def body(c_ref, w_ref, o_ref):
        wv = w_ref[...]
        wp = jnp.concatenate([wv, jnp.zeros((tr, WIN_W - SHARD_W), F32)], axis=1)
        ws = jnp.where(c_ref[0] == 1, pltpu.roll(wp, WIN_W - SHARD_W, axis=1), wp)
        o_ref[...] = ws.astype(BF16)

    return pl.pallas_call(
        body, name="to_window", grid=(rows // tr,),
        in_specs=[pl.BlockSpec(memory_space=pltpu.SMEM), pl.BlockSpec((tr, SHARD_W), lambda i: (i, 0))],
        out_specs=pl.BlockSpec((tr, WIN_W), lambda i: (i, 0)),
        out_shape=SDS((rows, WIN_W), BF16), compiler_params=_params(),
    )(c_arr, w)


def _cast_bf16(w, name):
    def body(w_ref, o_ref):
        o_ref[...] = w_ref[...].astype(BF16)

    return pl.pallas_call(body, name=name, out_shape=SDS(w.shape, BF16), compiler_params=_params())(w)


def _cols_from_slots(wg, name):
    _, rows, cols = wg.shape

    def body(w_ref, o_ref):
        for j in range(N_DEV):
            o_ref[:, j * cols:(j + 1) * cols] = w_ref[j]

    return pl.pallas_call(body, name=name, out_shape=SDS((rows, N_DEV * cols), wg.dtype), compiler_params=_params())(wg)


def _ada_fwd(c_all, w_ada):
    def body(c_ref, w_ref, o_ref):
        cv = c_ref[...]
        sc = (cv * _sigmoid(cv)).astype(BF16)
        o_ref[...] = _dot(sc, w_ref[...].astype(BF16))

    return pl.pallas_call(body, name="ada_fwd", out_shape=SDS((N_DEV, w_ada.shape[1]), F32),
                          compiler_params=_params())(c_all, w_ada)


def _ada_bwd(c_all, d_ada_cols):
    def body(c_ref, d_ref, o_ref):
        cv = c_ref[...]
        sc = (cv * _sigmoid(cv)).astype(BF16)
        o_ref[...] = _dot_tn(sc, d_ref[...].astype(BF16))

    return pl.pallas_call(body, name="ada_bwd", out_shape=SDS((D_MODEL, d_ada_cols.shape[1]), F32),
                          compiler_params=_params())(c_all, d_ada_cols)


def _norm_fwd(x, norm_w, scale, shift):
    s = x.shape[0]
    tr = 512

    def body(x_ref, nw_ref, sc_ref, sh_ref, h_ref, ht_ref):
        xv = x_ref[...]
        r = lax.rsqrt(jnp.mean(xv * xv, axis=-1, keepdims=True) + EPS)
        h = (xv * r * nw_ref[...]) * (1.0 + sc_ref[...]) + sh_ref[...]
        h_ref[...] = h.astype(BF16)
        ht_ref[...] = h.T.astype(BF16)

    vec = pl.BlockSpec((1, D_MODEL), lambda i: (0, 0))
    return pl.pallas_call(
        body, name="norm_fwd", grid=(s // tr,),
        in_specs=[pl.BlockSpec((tr, D_MODEL), lambda i: (i, 0)), vec, vec, vec],
        out_specs=[pl.BlockSpec((tr, D_MODEL), lambda i: (i, 0)), pl.BlockSpec((D_MODEL, tr), lambda i: (0, i))],
        out_shape=[SDS((s, D_MODEL), BF16), SDS((D_MODEL, s), BF16)], compiler_params=_params(),
    )(x, norm_w, scale, shift)


def _mm_in(h, wg):
    s = h.shape[0]
    tm = 512

    def body(h_ref, w_ref, o_ref):
        hv = h_ref[...]
        pe = _dot(hv, w_ref[0])
        po = _dot(hv, w_ref[1])
        o_ref[:, 0:1024] = pe[:, 0:1024]
        o_ref[:, 1024:1152] = pe[:, 1024:1152] + po[:, 0:128]
        o_ref[:, 1152:PAIR_W] = po[:, 128:WIN_W]

    return pl.pallas_call(
        body, name="mm_in", grid=(N_DEV // 2, s // tm),
        in_specs=[pl.BlockSpec((tm, D_MODEL), lambda p, m: (m, 0)),
                  pl.BlockSpec((2, D_MODEL, WIN_W), lambda p, m: (p, 0, 0))],
        out_specs=pl.BlockSpec((tm, PAIR_W), lambda p, m: (m, p)),
        out_shape=SDS((s, IN_W), F32), compiler_params=_params(),
    )(h, wg)


def _head_ones():
    a = lax.broadcasted_iota(jnp.int32, (LANE, LANE), 0) // HEAD_DIM
    b = lax.broadcasted_iota(jnp.int32, (LANE, LANE), 1) // HEAD_DIM
    return (a == b).astype(BF16)


def _head_sums(t, ones):
    return _dot(t.astype(BF16), ones)


def _band_bias(bias):
    qi = lax.broadcasted_iota(jnp.int32, (2 * BAND, 2 * BAND), 0) % BAND
    kj = lax.broadcasted_iota(jnp.int32, (2 * BAND, 2 * BAND), 1)
    dist = qi + BAND - kj
    valid = (dist >= 0) & (dist <= BAND)
    bias[1] = jnp.where(valid, 0.0, -1e30)
    bias[0] = jnp.where(valid & (kj >= BAND), 0.0, -1e30)


def _token_rows(j, d, chunk, per_r):
    return pl.ds(j // per_r + (j % per_r) * (chunk * d), chunk, stride=d)


def _deinterleave(src_ref, dst_ref, w_ref, ones, d, sub_len, chunk, scale, dst_off):
    per_r = sub_len // chunk

    def step(j, _):
        t = src_ref[_token_rows(j, d, chunk, per_r), :]
        if w_ref is not None:
            ms = _head_sums(t * t, ones) * (1.0 / HEAD_DIM)
            t = t * lax.rsqrt(ms + EPS) * (w_ref[...] * scale)
        dst_ref[pl.ds(pl.multiple_of(dst_off + j * chunk, BAND), chunk), :] = t.astype(dst_ref.dtype)
        return 0
    lax.fori_loop(0, d * per_r, step, 0, unroll=2)


def _attn_fwd(proj, qw2, kw2, g):
    s = proj.shape[0]
    d = DILATIONS[g]
    sub_len = s // d
    nb = sub_len // BAND
    chunk = min(sub_len, 256)

    def body(q_ref, k_ref, v_ref, qw_ref, kw_ref, o_ref, l_ref, qd, kd, vd, od, ld, bias):
        lo = lax.broadcasted_iota(jnp.int32, (1, LANE), 1) < HEAD_DIM
        ones = _head_ones()

        @pl.when(pl.program_id(0) == 0)
        def _():
            _band_bias(bias)

        kd[0:BAND, :] = jnp.zeros((BAND, LANE), BF16)
        vd[0:BAND, :] = jnp.zeros((BAND, LANE), BF16)
        _deinterleave(q_ref, qd, qw_ref, ones, d, sub_len, chunk, HEAD_DIM ** -0.5, 0)
        _deinterleave(k_ref, kd, kw_ref, ones, d, sub_len, chunk, 1.0, BAND)
        _deinterleave(v_ref, vd, None, ones, d, sub_len, chunk, 1.0, BAND)

        def block(t, _):
            base = pl.multiple_of(t * BAND, BAND)
            q = qd[pl.ds(base, BAND), :]
            k2 = kd[pl.ds(base, 2 * BAND), :]
            v2 = vd[pl.ds(base, 2 * BAND), :]
            zero = jnp.zeros_like(q)
            qs = jnp.concatenate([jnp.where(lo, q, zero), jnp.where(lo, zero, q)], axis=0)
            sc = _dot_nt(qs, k2) + bias[jnp.minimum(t % nb, 1)]
            m = jnp.max(sc, axis=-1, keepdims=True)
            p = jnp.exp(sc - m)
            den = jnp.sum(p, axis=-1, keepdims=True)
            u = _dot(p.astype(BF16), v2) * (1.0 / den)
            lse = m + jnp.log(den)
            od[pl.ds(base, BAND), :] = jnp.where(lo, u[:BAND], u[BAND:])
            ld[pl.ds(base, BAND), :] = jnp.where(lo, lse[:BAND], lse[BAND:])
            return 0
        lax.fori_loop(0, s // BAND, block, 0, unroll=2)

        per_r = sub_len // chunk

        def back(j, _):
            src = pl.ds(pl.multiple_of(j * chunk, chunk), chunk)
            dst = _token_rows(j, d, chunk, per_r)
            o_ref[dst, :] = od[src, :]
            l_ref[dst, :] = ld[src, :]
            return 0
        lax.fori_loop(0, d * per_r, back, 0, unroll=2)

    col = lambda off: pl.BlockSpec((s, LANE), lambda hp, off=off: (0, off // LANE + 4 * g + hp))
    vec = pl.BlockSpec((1, LANE), lambda hp: (0, 0))
    out = pl.BlockSpec((s, LANE), lambda hp: (0, hp))
    return pl.pallas_call(
        body, name=f"attn_fwd{g}", grid=(ATTN_W // LANE,),
        in_specs=[col(Q0), col(K0), col(V0), vec, vec], out_specs=[out, out],
        out_shape=[SDS((s, ATTN_W), F32), SDS((s, ATTN_W), F32)],
        scratch_shapes=[pltpu.VMEM((s, LANE), BF16), pltpu.VMEM((s + BAND, LANE), BF16), pltpu.VMEM((s + BAND, LANE), BF16),
                        pltpu.VMEM((s, LANE), F32), pltpu.VMEM((s, LANE), F32),
                        pltpu.VMEM((2, 2 * BAND, 2 * BAND), F32)],
        compiler_params=_params(),
    )(proj, proj, proj, qw2, kw2)


def _attn_bwd(proj, da, delta, lse, qw2, kw2, dproj, g):
    s = proj.shape[0]
    d = DILATIONS[g]
    sub_len = s // d
    nb = sub_len // BAND
    chunk = min(sub_len, 256)

    def body(q_ref, k_ref, v_ref, da_ref, dl_ref, ls_ref, qw_ref, kw_ref, dp_in, dp_out, dqw_ref, dkw_ref,
             qd, kd, vd, dad, dld, lsd, dqd, dkd, dvd, st, stb, bias, sem):
        del dp_in
        hp = pl.program_id(0)
        lo = lax.broadcasted_iota(jnp.int32, (1, LANE), 1) < HEAD_DIM
        ones = _head_ones()
        per_r = sub_len // chunk

        @pl.when(hp == 0)
        def _():
            _band_bias(bias)

        kd[0:BAND, :] = jnp.zeros((BAND, LANE), BF16)
        vd[0:BAND, :] = jnp.zeros((BAND, LANE), BF16)
        dkd[...] = jnp.zeros_like(dkd)
        dvd[...] = jnp.zeros_like(dvd)
        _deinterleave(q_ref, qd, qw_ref, ones, d, sub_len, chunk, HEAD_DIM ** -0.5, 0)
        _deinterleave(k_ref, kd, kw_ref, ones, d, sub_len, chunk, 1.0, BAND)
        _deinterleave(v_ref, vd, None, ones, d, sub_len, chunk, 1.0, BAND)
        _deinterleave(da_ref, dad, None, ones, d, sub_len, chunk, 1.0, 0)
        _deinterleave(dl_ref, dld, None, ones, d, sub_len, chunk, 1.0, 0)
        _deinterleave(ls_ref, lsd, None, ones, d, sub_len, chunk, 1.0, 0)

        def block(t, _):
            base = pl.multiple_of(t * BAND, BAND)
            q = qd[pl.ds(base, BAND), :]
            k2 = kd[pl.ds(base, 2 * BAND), :]
            v2 = vd[pl.ds(base, 2 * BAND), :]
            dav = dad[pl.ds(base, BAND), :]
            dlv = dld[pl.ds(base, BAND), :]
            lsv = lsd[pl.ds(base, BAND), :]
            zero = jnp.zeros_like(q)
            qs = jnp.concatenate([jnp.where(lo, q, zero), jnp.where(lo, zero, q)], axis=0)
            das = jnp.concatenate([jnp.where(lo, dav, zero), jnp.where(lo, zero, dav)], axis=0)
            ls_col = jnp.concatenate([lsv[:, 0:1], lsv[:, HEAD_DIM:HEAD_DIM + 1]], axis=0)
            dl_col = jnp.concatenate([dlv[:, 0:1], dlv[:, HEAD_DIM:HEAD_DIM + 1]], axis=0)
            sc = _dot_nt(qs, k2) + bias[jnp.minimum(t % nb, 1)]
            p = jnp.exp(sc - ls_col)
            dp = _dot_nt(das, v2)
            ds = (p * (dp - dl_col)).astype(BF16)
            dvd[pl.ds(base, 2 * BAND), :] += _dot_tn(p.astype(BF16), das)
            dkd[pl.ds(base, 2 * BAND), :] += _dot_tn(ds, qs)
            dq = _dot(ds, k2)
            dqd[pl.ds(base, BAND), :] = jnp.where(lo, dq[:BAND], dq[BAND:])
            return 0
        lax.fori_loop(0, s // BAND, block, 0, unroll=2)

        def store_cols(col0):
            stb[...] = st[...].astype(BF16)
            cp = pltpu.make_async_copy(
                stb, dp_out.at[:, pl.ds(pl.multiple_of(col0 + LANE * (4 * g + hp), LANE), LANE)], sem)
            cp.start()
            cp.wait()

        def norm_back(src_ref, dy_ref, dy_off, w_ref, scale, dw_ref, col0):
            def step(j, acc):
                tok = _token_rows(j, d, chunk, per_r)
                t = src_ref[tok, :]
                dy = dy_ref[pl.ds(pl.multiple_of(dy_off + j * chunk, BAND), chunk), :]
                rr = lax.rsqrt(_head_sums(t * t, ones) * (1.0 / HEAD_DIM) + EPS)
                nrm = t * rr
                acc = acc + jnp.sum(dy * nrm, axis=0, keepdims=True)
                dn = dy * (w_ref[...] * scale)
                st[tok, :] = rr * (dn - nrm * (_head_sums(dn * nrm, ones) * (1.0 / HEAD_DIM)))
                return acc
            acc = lax.fori_loop(0, d * per_r, step, jnp.zeros((1, LANE), F32), unroll=2)
            dw_ref[...] += jnp.broadcast_to(acc * scale, dw_ref.shape)
            store_cols(col0)

        @pl.when(hp == 0)
        def _():
            dqw_ref[...] = jnp.zeros_like(dqw_ref)
            dkw_ref[...] = jnp.zeros_like(dkw_ref)

        norm_back(q_ref, dqd, 0, qw_ref, HEAD_DIM ** -0.5, dqw_ref, Q0)
        norm_back(k_ref, dkd, BAND, kw_ref, 1.0, dkw_ref, K0)

        def v_back(j, _):
            src = pl.ds(pl.multiple_of(BAND + j * chunk, BAND), chunk)
            st[_token_rows(j, d, chunk, per_r), :] = dvd[src, :]
            return 0
        lax.fori_loop(0, d * per_r, v_back, 0, unroll=2)
        store_cols(V0)

    col = lambda off: pl.BlockSpec((s, LANE), lambda hp, off=off: (0, off // LANE + 4 * g + hp))
    mid = pl.BlockSpec((s, LANE), lambda hp: (0, hp))
    vec = pl.BlockSpec((1, LANE), lambda hp: (0, 0))
    acc = pl.BlockSpec((8, LANE), lambda hp: (0, 0))
    any_ = pl.BlockSpec(memory_space=pl.ANY)
    return pl.pallas_call(
        body, name=f"attn_bwd{g}", grid=(ATTN_W // LANE,),
        in_specs=[col(Q0), col(K0), col(V0), mid, mid, mid, vec, vec, any_],
        out_specs=[any_, acc, acc],
        out_shape=[SDS(dproj.shape, dproj.dtype), SDS((8, LANE), F32), SDS((8, LANE), F32)],
        input_output_aliases={8: 0},
        scratch_shapes=[pltpu.VMEM((s, LANE), BF16), pltpu.VMEM((s + BAND, LANE), BF16), pltpu.VMEM((s + BAND, LANE), BF16),
                        pltpu.VMEM((s, LANE), BF16), pltpu.VMEM((s, LANE), F32), pltpu.VMEM((s, LANE), F32),
                        pltpu.VMEM((s, LANE), F32), pltpu.VMEM((s + BAND, LANE), F32), pltpu.VMEM((s + BAND, LANE), F32),
                        pltpu.VMEM((s, LANE), F32), pltpu.VMEM((s, LANE), BF16),
                        pltpu.VMEM((2, 2 * BAND, 2 * BAND), F32), pltpu.SemaphoreType.DMA(())],
        compiler_params=_params(),
    )(proj, proj, proj, da, delta, lse, qw2, kw2, dproj)


def _silu_grad(z, sg):
    return sg * (1.0 + z * (1.0 - sg))


def _glu(u):
    a_h, b_h = u[:, :CONV_W], u[:, CONV_W:]
    sg = _sigmoid(b_h)
    return a_h, sg, a_h * sg


def _tail(x, tgt, proj, o3, l3, wa, wc, wo, gate, bga, bgc, convw, convb, lnw, lnb, bd):
    s = x.shape[0]
    tr = 256

    def body(x_ref, t_ref, za_ref, u_ref, uh_ref, zc_ref, g0_ref, g1_ref, g2_ref, g3_ref,
             o0_ref, o1_ref, o2_ref, l0_ref, l1_ref, l2_ref, wa_ref, wc_ref, wo_ref,
             gate_ref, bga_ref, bgc_ref, cw_ref, cb_ref, lnw_ref, lnb_ref, bd_ref,
             dout_ref, da_ref, dl_ref, lse_ref, dcv_ref, mt_ref, yat_ref, yct_ref, dmo_ref, dya_ref, dyc_ref, dp_ref,
             dgate_ref, dbg_ref, dlnw_ref, dlnb_ref, dcb_ref, loss_ref,
             ext, st_za, st_zc, st_g, sems):
        i = pl.program_id(0)

        @pl.when(i == 0)
        def _():
            for r in (dgate_ref, dbg_ref, dlnw_ref, dlnb_ref, dcb_ref, loss_ref):
                r[...] = jnp.zeros_like(r)

        def acc_rows(ref, v):
            ref[...] += jnp.broadcast_to(jnp.sum(v, axis=0, keepdims=True), ref.shape)

        la, lb, lc = l0_ref[...], l1_ref[...], l2_ref[...]
        mx = jnp.maximum(jnp.maximum(la, lb), lc)
        ea, eb, ec = jnp.exp(la - mx), jnp.exp(lb - mx), jnp.exp(lc - mx)
        den = ea + eb + ec
        inv = 1.0 / den
        attn = (ea * inv) * o0_ref[...] + (eb * inv) * o1_ref[...] + (ec * inv) * o2_ref[...]
        lse_ref[...] = mx + jnp.log(den)

        za = za_ref[...]
        sga = _sigmoid(za)
        sa = za * sga
        ya_in = attn * sa
        y_attn = _dot(ya_in.astype(BF16), wa_ref[...])

        _, _, glu = _glu(u_ref[...])
        _, _, glu_h = _glu(uh_ref[...])
        ext[0:CONV_HALO, :] = jnp.where(i > 0, glu_h, 0.0)
        ext[CONV_HALO:CONV_HALO + tr, :] = glu
        cv = jnp.broadcast_to(cb_ref[...], (tr, CONV_W))
        for j in range(CONV_K):
            off = CONV_HALO - (CONV_K - 1) + j
            cv = cv + cw_ref[j:j + 1, :] * ext[off:off + tr, :]
        mu = jnp.mean(cv, axis=-1, keepdims=True)
        xc = cv - mu
        rstd = lax.rsqrt(jnp.mean(xc * xc, axis=-1, keepdims=True) + EPS)
        nrm = xc * rstd
        ln = nrm * lnw_ref[...] + lnb_ref[...]
        sgl = _sigmoid(ln)
        cs = ln * sgl
        zc = zc_ref[...]
        sgc = _sigmoid(zc)
        scz = zc * sgc
        yc_in = cs * scz
        y_conv = _dot(yc_in.astype(BF16), wc_ref[...])

        ga = _sigmoid(jnp.concatenate([g0_ref[...], g1_ref[...]], axis=1) + bga_ref[...])
        gc = _sigmoid(jnp.concatenate([g2_ref[...], g3_ref[...]], axis=1) + bgc_ref[...])
        merged = ga * y_attn + gc * y_conv
        mo = _dot(merged.astype(BF16), wo_ref[...])
        gate_v = gate_ref[...]
        err = (x_ref[...] + gate_v * mo) - t_ref[...]
        loss_ref[...] += 0.5 * jnp.sum(jnp.mean(err * err, axis=-1, keepdims=True))
        d_out = err * (1.0 / D_MODEL)
        dout_ref[...] = d_out

        acc_rows(dgate_ref, d_out * mo)
        dmo_b = (d_out * gate_v).astype(BF16)
        dmo_ref[...] = dmo_b
        mt_ref[...] = merged.T.astype(BF16)
        d_merged = _dot_nt(dmo_b, wo_ref[...])
        d_ya = (d_merged * ga).astype(BF16)
        d_yc = (d_merged * gc).astype(BF16)
        dya_ref[...] = d_ya
        dyc_ref[...] = d_yc
        dga = d_merged * y_attn * (ga * (1.0 - ga))
        dgc = d_merged * y_conv * (gc * (1.0 - gc))
        dgs = jnp.concatenate([dga, dgc], axis=1)
        acc_rows(dbg_ref, dgs)
        st_g[...] = dgs.astype(BF16)

        yat_ref[...] = ya_in.T.astype(BF16)
        d_ya_in = _dot_nt(d_ya, wa_ref[...])
        d_attn = d_ya_in * sa
        da_ref[...] = d_attn
        st_za[...] = (d_ya_in * attn * _silu_grad(za, sga)).astype(BF16)
        prod = d_attn * attn
        hi = prod.astype(BF16)
        lo_ = (prod - hi.astype(F32)).astype(BF16)
        dl_ref[...] = _dot(hi, bd_ref[...]) + _dot(lo_, bd_ref[...])

        yct_ref[...] = yc_in.T.astype(BF16)
        d_yc_in = _dot_nt(d_yc, wc_ref[...])
        st_zc[...] = (d_yc_in * cs * _silu_grad(zc, sgc)).astype(BF16)
        d_ln = (d_yc_in * scz) * _silu_grad(ln, sgl)
        acc_rows(dlnw_ref, d_ln * nrm)
        acc_rows(dlnb_ref, d_ln)
        d_nrm = d_ln * lnw_ref[...]
        d_cv = rstd * (d_nrm - jnp.mean(d_nrm, axis=-1, keepdims=True)
                       - nrm * jnp.mean(d_nrm * nrm, axis=-1, keepdims=True))
        acc_rows(dcb_ref, d_cv)
        dcv_ref[...] = d_cv

        rows = pl.ds(pl.multiple_of(i * tr, tr), tr)
        cps = [pltpu.make_async_copy(st_za, dp_ref.at[rows, pl.ds(ZA0, ATTN_W)], sems.at[0]),
               pltpu.make_async_copy(st_zc, dp_ref.at[rows, pl.ds(ZC0, CONV_W)], sems.at[1]),
               pltpu.make_async_copy(st_g, dp_ref.at[rows, pl.ds(G0, 2 * D_MODEL)], sems.at[2])]
        for cp in cps:
            cp.start()
        for cp in cps:
            cp.wait()

    def rows(width, colblk=0):
        return pl.BlockSpec((tr, width), lambda i, colblk=colblk: (i, colblk))

    def const(shape):
        return pl.BlockSpec(shape, lambda i: (0,) * len(shape))

    halo = pl.BlockSpec((CONV_HALO, D_MODEL), lambda i: (jnp.maximum(i * (tr // CONV_HALO) - 1, 0), U0 // D_MODEL))
    in_specs = [rows(D_MODEL), rows(D_MODEL), rows(ATTN_W, ZA0 // ATTN_W), rows(D_MODEL, U0 // D_MODEL), halo,
                rows(CONV_W, ZC0 // CONV_W)]
    in_specs += [rows(512, G0 // 512 + j) for j in range(4)]
    in_specs += [rows(ATTN_W)] * 6
    in_specs += [const(wa.shape), const(wc.shape), const(wo.shape), const((1, D_MODEL)), const((1, D_MODEL)),
                 const((1, D_MODEL)), const(convw.shape), const((1, CONV_W)), const((1, CONV_W)), const((1, CONV_W)),
                 const(bd.shape)]
    tcol = lambda width: pl.BlockSpec((width, tr), lambda i: (0, i))
    out_specs = [rows(D_MODEL), rows(ATTN_W), rows(ATTN_W), rows(ATTN_W), rows(CONV_W),
                 tcol(D_MODEL), tcol(ATTN_W), tcol(CONV_W), rows(D_MODEL), rows(D_MODEL), rows(D_MODEL),
                 pl.BlockSpec(memory_space=pl.ANY),
                 const((8, D_MODEL)), const((8, 2 * D_MODEL)), const((8, CONV_W)), const((8, CONV_W)), const((8, CONV_W)),
                 const((8, LANE))]
    out_shape = [SDS((s, D_MODEL), F32), SDS((s, ATTN_W), F32), SDS((s, ATTN_W), F32), SDS((s, ATTN_W), F32),
                 SDS((s, CONV_W), F32),
                 SDS((D_MODEL, s), BF16), SDS((ATTN_W, s), BF16), SDS((CONV_W, s), BF16),
                 SDS((s, D_MODEL), BF16), SDS((s, D_MODEL), BF16), SDS((s, D_MODEL), BF16),
                 SDS((s, IN_W), BF16),
                 SDS((8, D_MODEL), F32), SDS((8, 2 * D_MODEL), F32), SDS((8, CONV_W), F32), SDS((8, CONV_W), F32),
                 SDS((8, CONV_W), F32), SDS((8, LANE), F32)]
    return pl.pallas_call(
        body, name="tail", grid=(s // tr,), in_specs=in_specs, out_specs=out_specs, out_shape=out_shape,
        scratch_shapes=[pltpu.VMEM((CONV_HALO + tr, CONV_W), F32), pltpu.VMEM((tr, ATTN_W), BF16),
                        pltpu.VMEM((tr, CONV_W), BF16), pltpu.VMEM((tr, 2 * D_MODEL), BF16),
                        pltpu.SemaphoreType.DMA((3,))],
        compiler_params=_params(),
    )(x, tgt, proj, proj, proj, proj, proj, proj, proj, proj, *o3, *l3, wa, wc, wo, gate, bga, bgc,
      convw, convb, lnw, lnb, bd)


def _conv_bwd(dcv, proj, convw, dproj):
    s = dcv.shape[0]
    tr = 256
    nt = s // tr

    def body(dcv_ref, dcvn_ref, u_ref, uh_ref, cw_ref, dp_in, dp_out, dw_ref, extg, extd):
        del dp_in
        i = pl.program_id(0)

        @pl.when(i == 0)
        def _():
            dw_ref[...] = jnp.zeros_like(dw_ref)

        a_h, sgb, glu = _glu(u_ref[...])
        _, _, glu_h = _glu(uh_ref[...])
        extg[0:CONV_HALO, :] = jnp.where(i > 0, glu_h, 0.0)
        extg[CONV_HALO:CONV_HALO + tr, :] = glu
        dcv_v = dcv_ref[...]
        extd[0:tr, :] = dcv_v
        extd[tr:tr + CONV_HALO, :] = jnp.where(i < nt - 1, dcvn_ref[...], 0.0)
        dglu = jnp.zeros((tr, CONV_W), F32)
        for j in range(CONV_K):
            back = CONV_K - 1 - j
            dglu = dglu + cw_ref[j:j + 1, :] * extd[back:back + tr, :]
            off = CONV_HALO - (CONV_K - 1) + j
            dw_ref[j:j + 1, :] += jnp.sum(dcv_v * extg[off:off + tr, :], axis=0, keepdims=True)
        d_a = dglu * sgb
        d_b = dglu * a_h * (sgb * (1.0 - sgb))
        dp_out[...] = jnp.concatenate([d_a, d_b], axis=1).astype(BF16)

    ucol = U0 // D_MODEL
    return pl.pallas_call(
        body, name="conv_bwd", grid=(nt,),
        in_specs=[pl.BlockSpec((tr, CONV_W), lambda i: (i, 0)),
                  pl.BlockSpec((CONV_HALO, CONV_W), lambda i: (jnp.minimum((i + 1) * (tr // CONV_HALO), s // CONV_HALO - 1), 0)),
                  pl.BlockSpec((tr, D_MODEL), lambda i: (i, ucol)),
                  pl.BlockSpec((CONV_HALO, D_MODEL), lambda i: (jnp.maximum(i * (tr // CONV_HALO) - 1, 0), ucol)),
                  pl.BlockSpec(convw.shape, lambda i: (0, 0)),
                  pl.BlockSpec(memory_space=pl.ANY)],
        out_specs=[pl.BlockSpec((tr, D_MODEL), lambda i: (i, ucol)), pl.BlockSpec((CONV_HALO, CONV_W), lambda i: (0, 0))],
        out_shape=[SDS(dproj.shape, dproj.dtype), SDS((CONV_HALO, CONV_W), F32)],
        input_output_aliases={5: 0},
        scratch_shapes=[pltpu.VMEM((CONV_HALO + tr, CONV_W), F32), pltpu.VMEM((CONV_HALO + tr, CONV_W), F32)],
        compiler_params=_params(),
    )(dcv, dcv, proj, proj, convw, dproj)


def _mm_acc(at, b, name, col_slots):
    m, s = at.shape
    n = b.shape[1]
    tk = 512
    nk = s // tk

    def body(a_ref, b_ref, o_ref, acc):
        k = pl.program_id(0)

        @pl.when(k == 0)
        def _():
            acc[...] = jnp.zeros_like(acc)

        acc[...] += _dot(a_ref[...], b_ref[...])

        @pl.when(k == nk - 1)
        def _():
            if col_slots:
                w = n // N_DEV
                for j in range(N_DEV):
                    o_ref[j] = acc[:, j * w:(j + 1) * w].astype(BF16)
            else:
                o_ref[...] = acc[...].astype(BF16)

    if col_slots:
        out_shape = SDS((N_DEV, m, n // N_DEV), BF16)
        out_spec = pl.BlockSpec((N_DEV, m, n // N_DEV), lambda k: (0, 0, 0))
    else:
        out_shape = SDS((m, n), BF16)
        out_spec = pl.BlockSpec((m, n), lambda k: (0, 0))
    return pl.pallas_call(
        body, name=name, grid=(nk,),
        in_specs=[pl.BlockSpec((m, tk), lambda k: (0, k)), pl.BlockSpec((tk, n), lambda k: (k, 0))],
        out_specs=out_spec, out_shape=out_shape, scratch_shapes=[pltpu.VMEM((m, n), F32)],
        compiler_params=_params(),
    )(at, b)


def _mm_dw(ht, dproj):
    s = ht.shape[1]
    tk = 512
    nk = s // tk

    def body(a_ref, b_ref, o_ref, acc):
        k = pl.program_id(1)

        @pl.when(k == 0)
        def _():
            acc[...] = jnp.zeros_like(acc)

        acc[...] += _dot(a_ref[...], b_ref[...])

        @pl.when(k == nk - 1)
        def _():
            o_ref[0] = acc[:, 0:WIN_W].astype(BF16)
            o_ref[1] = acc[:, PAIR_W - WIN_W:PAIR_W].astype(BF16)

    return pl.pallas_call(
        body, name="mm_dw", grid=(N_DEV // 2, nk),
        in_specs=[pl.BlockSpec((D_MODEL, tk), lambda p, k: (0, k)), pl.BlockSpec((tk, PAIR_W), lambda p, k: (k, p))],
        out_specs=pl.BlockSpec((2, D_MODEL, WIN_W), lambda p, k: (p, 0, 0)),
        out_shape=SDS((N_DEV, D_MODEL, WIN_W), BF16), scratch_shapes=[pltpu.VMEM((D_MODEL, PAIR_W), F32)],
        compiler_params=_params(),
    )(ht, dproj)


def _mm_dh(dproj, wg):
    s = dproj.shape[0]
    tm = 1024

    def body(dp_ref, w_ref, o_ref):
        p = pl.program_id(1)
        part = (_dot_nt(dp_ref[:, 0:WIN_W], w_ref[0]) + _dot_nt(dp_ref[:, PAIR_W - WIN_W:PAIR_W], w_ref[1]))

        @pl.when(p == 0)
        def _():
            o_ref[...] = part

        @pl.when(p > 0)
        def _():
            o_ref[...] += part

    return pl.pallas_call(
        body, name="mm_dh", grid=(s // tm, N_DEV // 2),
        in_specs=[pl.BlockSpec((tm, PAIR_W), lambda m, p: (m, p)),
                  pl.BlockSpec((2, D_MODEL, WIN_W), lambda m, p: (p, 0, 0))],
        out_specs=pl.BlockSpec((tm, D_MODEL), lambda m, p: (m, 0)),
        out_shape=SDS((s, D_MODEL), F32), compiler_params=_params(),
    )(dproj, wg)


def _norm_bwd(x, dh, dout, norm_w, scale):
    s = x.shape[0]
    tr = 512

    def body(x_ref, dh_ref, do_ref, nw_ref, sc_ref, gx_ref, dsh_ref, dsc_ref, dnw_ref):
        i = pl.program_id(0)

        @pl.when(i == 0)
        def _():
            for r in (dsh_ref, dsc_ref, dnw_ref):
                r[...] = jnp.zeros_like(r)

        def acc_rows(ref, v):
            ref[...] += jnp.broadcast_to(jnp.sum(v, axis=0, keepdims=True), ref.shape)

        xv = x_ref[...]
        dh_v = dh_ref[...]
        r = lax.rsqrt(jnp.mean(xv * xv, axis=-1, keepdims=True) + EPS)
        xn = xv * r
        one_sc = 1.0 + sc_ref[...]
        acc_rows(dsh_ref, dh_v)
        acc_rows(dsc_ref, dh_v * (xn * nw_ref[...]))
        acc_rows(dnw_ref, dh_v * xn * one_sc)
        dxn = dh_v * (nw_ref[...] * one_sc)
        gx_ref[...] = do_ref[...] + r * (dxn - xn * jnp.mean(dxn * xn, axis=-1, keepdims=True))

    blk = pl.BlockSpec((tr, D_MODEL), lambda i: (i, 0))
    vec = pl.BlockSpec((1, D_MODEL), lambda i: (0, 0))
    acc = pl.BlockSpec((8, D_MODEL), lambda i: (0, 0))
    return pl.pallas_call(
        body, name="norm_bwd", grid=(s // tr,), in_specs=[blk, blk, blk, vec, vec],
        out_specs=[blk, acc, acc, acc],
        out_shape=[SDS((s, D_MODEL), F32)] + [SDS((8, D_MODEL), F32)] * 3, compiler_params=_params(),
    )(x, dh, dout, norm_w, scale)


def _adamw(gsrc, w, m, v, name, stacked, c_arr=None):
    rows, cols = w.shape
    tr = rows if rows <= 128 else 128
    unshift = c_arr is not None
    bc1 = 1.0 - ADAM_B1 ** ADAM_STEP
    bc2 = 1.0 - ADAM_B2 ** ADAM_STEP

    def body(*refs):
        if unshift:
            c_ref, refs = refs[0], refs[1:]
        g_ref, w_ref, m_ref, v_ref, go_ref, d_ref, mo_ref, vo_ref = refs
        if stacked:
            g = g_ref[0].astype(F32)
            for j in range(1, gsrc.shape[0]):
                g = g + g_ref[j].astype(F32)
        else:
            g = g_ref[...]
        if unshift:
            g = jnp.where(c_ref[0] == 1, pltpu.roll(g, SHARD_W, axis=1), g)[:, :SHARD_W]
        m_new = ADAM_B1 * m_ref[...] + (1.0 - ADAM_B1) * g
        v_new = ADAM_B2 * v_ref[...] + (1.0 - ADAM_B2) * (g * g)
        m_hat = m_new / bc1
        v_hat = v_new / bc2
        go_ref[...] = g
        d_ref[...] = -ADAM_LR * (m_hat / (jnp.sqrt(v_hat) + ADAM_EPS) + ADAM_WD * w_ref[...])
        mo_ref[...] = m_new
        vo_ref[...] = v_new

    blk = pl.BlockSpec((tr, cols), lambda i: (i, 0))
    if stacked:
        gcols = gsrc.shape[2]
        gspec = pl.BlockSpec((gsrc.shape[0], tr, gcols), lambda i: (0, i, 0))
    else:
        gspec = blk
    in_specs = [gspec, blk, blk, blk]
    args = [gsrc, w, m, v]
    if unshift:
        in_specs = [pl.BlockSpec(memory_space=pltpu.SMEM)] + in_specs
        args = [c_arr] + args
    return pl.pallas_call(
        body, name=name, grid=(rows // tr,), in_specs=in_specs, out_specs=[blk] * 4,
        out_shape=[SDS((rows, cols), F32)] * 4, compiler_params=_params(),
    )(*args)


def _pack_small(parts):
    cols = []
    for name, length in _SMALL:
        p = parts[name].reshape(1, -1)
        cols.append(jnp.pad(p, ((0, 0), (0, length - p.shape[1]))))
    return jnp.concatenate(cols, axis=1)


def _unpack_small(vec, name, n):
    off, _ = SMALL_OFF[name]
    return vec[:, off:off + n]


def kernel(x, c, w_ada, b_ada, norm_w, w_in, b_gate, q_norm_w, k_norm_w, w_attn_proj, conv_w, conv_b, conv_ln_w, conv_ln_b, w_conv_proj, w_out, loss_target, m_w_ada, m_b_ada, m_norm_w, m_w_in, m_b_gate, m_q_norm_w, m_k_norm_w, m_w_attn_proj, m_conv_w, m_conv_b, m_conv_ln_w, m_conv_ln_b, m_w_conv_proj, m_w_out, v_w_ada, v_b_ada, v_norm_w, v_w_in, v_b_gate, v_q_norm_w, v_k_norm_w, v_w_attn_proj, v_conv_w, v_conv_b, v_conv_ln_w, v_conv_ln_b, v_w_conv_proj, v_w_out):
    xi, yi, ci = lax.axis_index("x"), lax.axis_index("y"), lax.axis_index("c")
    me = 4 * xi + 2 * yi + ci
    c_arr = jnp.reshape(ci, (1,)).astype(jnp.int32)
    x2, tgt2 = x[0], loss_target[0]
    s = x2.shape[0]

    cw_flat = jnp.pad(conv_w[0].reshape(1, -1), ((0, 0), (0, CONVW_FLAT - CONV_K * HEAD_DIM)))
    pre = jnp.concatenate([c, cw_flat], axis=1).reshape(8, -1)
    (pre_all,) = _all_gather([pre], "gather_c_convw", vmem=True)
    pre_all = pre_all.reshape(N_DEV, -1)
    c_all = pre_all[:, :D_MODEL]
    convw_full = pre_all[:, D_MODEL:D_MODEL + CONV_K * HEAD_DIM].reshape(N_DEV, CONV_K, HEAD_DIM)
    convw_full = jnp.transpose(convw_full, (1, 0, 2)).reshape(CONV_K, CONV_W)
    convw_pad = jnp.pad(convw_full, ((0, CONV_HALO - CONV_K), (0, 0)))

    ada_part = _ada_fwd(c_all, w_ada[0])
    (ada_all,) = _all_gather([ada_part], "gather_ada", vmem=True)
    ada = lax.dynamic_index_in_dim(ada_all, me, axis=1, keepdims=False).reshape(1, 3 * D_MODEL) + b_ada
    shift, scale, gate = ada[:, :D_MODEL], ada[:, D_MODEL:2 * D_MODEL], ada[:, 2 * D_MODEL:]

    wg, wa_g, wc_g, wo_g = _all_gather_chips(
        [_to_window(w_in[0], c_arr), _cast_bf16(w_attn_proj[0], "cast_wa"), _cast_bf16(w_conv_proj[0], "cast_wc"),
         _cast_bf16(w_out[0], "cast_wo")], "gather_weights")
    wa = _cols_from_slots(wa_g, "cols_wa")
    wc = _cols_from_slots(wc_g, "cols_wc")
    wo = wo_g.reshape(D_MODEL, D_MODEL)

    h, ht = _norm_fwd(x2, norm_w, scale, shift)
    proj = _mm_in(h, wg)
    qw2 = jnp.tile(q_norm_w, (1, 2))
    kw2 = jnp.tile(k_norm_w, (1, 2))
    o3, l3 = [], []
    for g in range(N_GROUPS):
        o_g, l_g = _attn_fwd(proj, qw2, kw2, g)
        o3.append(o_g)
        l3.append(l_g)
    head_id = jnp.arange(ATTN_W) // HEAD_DIM
    bd = (head_id[:, None] == head_id[None, :]).astype(BF16)
    (dout, da, delta, lse, dcv, mt, yat, yct, dmo, dya, dyc, dproj,
     dgate, dbg, dlnw, dlnb, dcb, loss_p) = _tail(
        x2, tgt2, proj, o3, l3, wa, wc, wo, gate, b_gate[:, :D_MODEL], b_gate[:, D_MODEL:], convw_pad,
        conv_b, conv_ln_w, conv_ln_b, bd)

    dproj, dconvw = _conv_bwd(dcv, proj, convw_pad, dproj)
    dqw = jnp.zeros((1, HEAD_DIM), F32)
    dkw = jnp.zeros((1, HEAD_DIM), F32)
    for g in range(N_GROUPS):
        dproj, dqw_g, dkw_g = _attn_bwd(proj, da, delta, lse, qw2, kw2, dproj, g)
        dqw = dqw + dqw_g[0:1, :HEAD_DIM] + dqw_g[0:1, HEAD_DIM:]
        dkw = dkw + dkw_g[0:1, :HEAD_DIM] + dkw_g[0:1, HEAD_DIM:]
    dh = _mm_dh(dproj, wg)
    gx, dsh, dsc, dnw = _norm_bwd(x2, dh, dout, norm_w, scale)
    dw_in_p = _mm_dw(ht, dproj)
    dwo_p = _mm_acc(mt, dmo, "mm_dwo", col_slots=False).reshape(N_DEV, D_MODEL // N_DEV, D_MODEL)
    dwa_p = _mm_acc(yat, dya, "mm_dwa", col_slots=True)
    dwc_p = _mm_acc(yct, dyc, "mm_dwc", col_slots=True)

    partials = [dw_in_p, dwa_p, dwc_p, dwo_p]
    me_arr = jnp.reshape(me, (1,)).astype(jnp.int32)
    from_sib = _exchange_sibling(partials, "exchange_sibling")
    presums = [_presum(p, f, me_arr, f"presum{i}") for i, (p, f) in enumerate(zip(partials, from_sib))]
    r_in, r_wa, r_wc, r_wo = _exchange_chips(partials, from_sib, presums, "exchange_chips")
    d_ada = jnp.concatenate([dsh[0:1], dsc[0:1], dgate[0:1]], axis=1)
    small_p = _pack_small({"b_ada": d_ada, "norm_w": dnw[0:1], "b_gate": dbg[0:1], "q_norm_w": dqw, "k_norm_w": dkw,
                           "conv_b": dcb[0:1], "conv_ln_w": dlnw[0:1], "conv_ln_b": dlnb[0:1], "loss": loss_p[0:1]})
    small_all, dconvw_all = _all_gather([jnp.broadcast_to(small_p, (8, SMALL_N)), dconvw], "gather_small", vmem=True)
    small_all = small_all[:, 0:1, :]
    dcw_mine = lax.dynamic_slice_in_dim(dconvw_all[:, :CONV_K, :], me * HEAD_DIM, HEAD_DIM, axis=2)
    dcw_mine = jnp.pad(dcw_mine.reshape(N_DEV, 1, -1), ((0, 0), (0, 0), (0, CONVW_FLAT - CONV_K * HEAD_DIM)))
    pack_g = jnp.concatenate([small_all, dcw_mine], axis=2)

    def pack_params(tree):
        small = _pack_small({"b_ada": tree["b_ada"], "norm_w": tree["norm_w"], "b_gate": tree["b_gate"],
                             "q_norm_w": tree["q_norm_w"], "k_norm_w": tree["k_norm_w"], "conv_b": tree["conv_b"],
                             "conv_ln_w": tree["conv_ln_w"], "conv_ln_b": tree["conv_ln_b"],
                             "loss": jnp.ones((1, 1), F32)})
        cw = jnp.pad(tree["conv_w"][0].reshape(1, -1), ((0, 0), (0, CONVW_FLAT - CONV_K * HEAD_DIM)))
        return jnp.concatenate([small, cw], axis=1)

    names = ("b_ada", "norm_w", "b_gate", "q_norm_w", "k_norm_w", "conv_b", "conv_ln_w", "conv_ln_b", "conv_w")
    w_tree = dict(zip(names, (b_ada, norm_w, b_gate, q_norm_w, k_norm_w, conv_b, conv_ln_w, conv_ln_b, conv_w)))
    m_tree = dict(zip(names, (m_b_ada, m_norm_w, m_b_gate, m_q_norm_w, m_k_norm_w, m_conv_b, m_conv_ln_w, m_conv_ln_b, m_conv_w)))
    v_tree = dict(zip(names, (v_b_ada, v_norm_w, v_b_gate, v_q_norm_w, v_k_norm_w, v_conv_b, v_conv_ln_w, v_conv_ln_b, v_conv_w)))
    pk = _adamw(pack_g, pack_params(w_tree), pack_params(m_tree), pack_params(v_tree), "adamw_small", stacked=True)

    d_ada_all = small_all[:, 0, :3 * D_MODEL]
    d_ada_cols = lax.dynamic_slice_in_dim(d_ada_all, me * (3 * D_MODEL // N_DEV), 3 * D_MODEL // N_DEV, axis=1)
    g_wada = _ada_bwd(c_all, d_ada_cols)
    r_ada = _adamw(g_wada, w_ada[0], m_w_ada[0], v_w_ada[0], "adamw_w_ada", stacked=False)
    r_win = _adamw(r_in, w_in[0], m_w_in[0], v_w_in[0], "adamw_w_in", stacked=True, c_arr=c_arr)
    r_wap = _adamw(r_wa, w_attn_proj[0], m_w_attn_proj[0], v_w_attn_proj[0], "adamw_w_attn_proj", stacked=True)
    r_wcp = _adamw(r_wc, w_conv_proj[0], m_w_conv_proj[0], v_w_conv_proj[0], "adamw_w_conv_proj", stacked=True)
    r_wout = _adamw(r_wo, w_out[0], m_w_out[0], v_w_out[0], "adamw_w_out", stacked=True)

    def small_out(k, name, n):
        return _unpack_small(pk[k], name, n)

    def convw_out(k):
        return pk[k][:, SMALL_N:SMALL_N + CONV_K * HEAD_DIM].reshape(1, CONV_K, HEAD_DIM)

    loss = pk[0][0, SMALL_OFF["loss"][0]]
    outs = [loss, gx[None]]
    for k in range(4):
        outs += [r_ada[k][None], small_out(k, "b_ada", 3 * D_MODEL), small_out(k, "norm_w", D_MODEL), r_win[k][None],
                 small_out(k, "b_gate", 2 * D_MODEL), small_out(k, "q_norm_w", HEAD_DIM), small_out(k, "k_norm_w", HEAD_DIM),
                 r_wap[k][None], convw_out(k), small_out(k, "conv_b", CONV_W), small_out(k, "conv_ln_w", CONV_W),
                 small_out(k, "conv_ln_b", CONV_W), r_wcp[k][None], r_wout[k][None]]
    return tuple(outs)
```

```python
import functools

import jax
import jax.numpy as jnp
from jax import lax
from jax.experimental import pallas as pl
from jax.experimental.pallas import tpu as pltpu

F32 = jnp.float32
BF16 = jnp.bfloat16
SDS = jax.ShapeDtypeStruct
MESH = pl.DeviceIdType.MESH

N_DEV = 8
D_MODEL = 1024
HEAD_DIM = 64
N_GROUPS = 3
DILATIONS = (1, 4, 16)
BAND = 128
ATTN_W = 512
CONV_W = 512
CONV_K = 31
CONV_HALO = 32
IN_W = 8704
SHARD_W = IN_W // N_DEV
WIN_W = 1152
PAIR_W = 2 * SHARD_W
Q0, K0, V0, ZA0, U0, ZC0, G0 = 0, 1536, 3072, 4608, 5120, 6144, 6656
EPS = 1e-6
LANE = 128
VMEM_LIMIT = 56 * 1024 * 1024

ADAM_LR, ADAM_B1, ADAM_B2, ADAM_EPS, ADAM_WD, ADAM_STEP = 0.001, 0.9, 0.999, 1e-08, 0.01, 10

_SMALL = (("b_ada", 3072), ("norm_w", 1024), ("b_gate", 2048), ("q_norm_w", 128), ("k_norm_w", 128),
          ("conv_b", 512), ("conv_ln_w", 512), ("conv_ln_b", 512), ("loss", 128))
SMALL_OFF = {}
_o = 0
for _n, _l in _SMALL:
    SMALL_OFF[_n] = (_o, _l)
    _o += _l
SMALL_N = _o
CONVW_FLAT = 2048
PACK_N = SMALL_N + CONVW_FLAT


def _params(**kw):
    return pltpu.CompilerParams(vmem_limit_bytes=VMEM_LIMIT, **kw)


def _sigmoid(z):
    return 1.0 / (1.0 + jnp.exp(-z))


def _dot(a, b):
    return jnp.dot(a, b, preferred_element_type=F32)


def _dot_nt(a, b):
    return lax.dot_general(a, b, (((1,), (1,)), ((), ())), preferred_element_type=F32)


def _dot_tn(a, b):
    return lax.dot_general(a, b, (((0,), (0,)), ((), ())), preferred_element_type=F32)


def _peer(x, y, c, k):
    px = 1 - x if (k >> 2) & 1 else x
    py = 1 - y if (k >> 1) & 1 else y
    pc = 1 - c if k & 1 else c
    return (px, py, pc), 4 * px + 2 * py + pc


def _all_gather(arrays, name, vmem):
    n = len(arrays)
    space = pltpu.VMEM if vmem else pl.ANY

    def body(*refs):
        ins, outs = refs[:n], refs[n:2 * n]
        send_sems, recv_sems, local_sems = refs[2 * n:]
        x, y, c = lax.axis_index("x"), lax.axis_index("y"), lax.axis_index("c")
        me = 4 * x + 2 * y + c
        locals_ = [pltpu.make_async_copy(ins[a], outs[a].at[me], local_sems.at[a]) for a in range(n)]
        for cp in locals_:
            cp.start()
        sends = []
        for k in range(1, N_DEV):
            peer, _ = _peer(x, y, c, k)
            for a in range(n):
                cp = pltpu.make_async_remote_copy(
                    src_ref=ins[a], dst_ref=outs[a].at[me], send_sem=send_sems.at[a, k - 1],
                    recv_sem=recv_sems.at[a, k - 1], device_id=peer, device_id_type=MESH)
                cp.start()
                sends.append(cp)
        for k in range(1, N_DEV):
            peer, pidx = _peer(x, y, c, k)
            for a in range(n):
                pltpu.make_async_remote_copy(
                    src_ref=ins[a], dst_ref=outs[a].at[pidx], send_sem=send_sems.at[a, k - 1],
                    recv_sem=recv_sems.at[a, k - 1], device_id=peer, device_id_type=MESH).wait_recv()
        for cp in sends:
            cp.wait_send()
        for cp in locals_:
            cp.wait()

    return pl.pallas_call(
        body, name=name,
        out_shape=[SDS((N_DEV,) + a.shape, a.dtype) for a in arrays],
        in_specs=[pl.BlockSpec(memory_space=space)] * n,
        out_specs=[pl.BlockSpec(memory_space=space)] * n,
        scratch_shapes=[pltpu.SemaphoreType.DMA((n, N_DEV - 1)), pltpu.SemaphoreType.DMA((n, N_DEV - 1)),
                        pltpu.SemaphoreType.DMA((n,))],
        compiler_params=_params(),
    )(*arrays)


CHIP_K = (2, 4, 6)


def _all_gather_chips(arrays, name):
    n = len(arrays)

    def body(*refs):
        ins, outs = refs[:n], refs[n:2 * n]
        send_sems, recv_sems, local_sems = refs[2 * n:]
        x, y, c = lax.axis_index("x"), lax.axis_index("y"), lax.axis_index("c")
        me = 4 * x + 2 * y + c
        sib, sib_idx = _peer(x, y, c, 1)

        def copy(a, slot, block, to, src=None):
            return pltpu.make_async_remote_copy(
                src_ref=outs[a].at[block] if src is None else src, dst_ref=outs[a].at[block],
                send_sem=send_sems.at[a, slot], recv_sem=recv_sems.at[a, slot], device_id=to, device_id_type=MESH)

        locals_ = [pltpu.make_async_copy(ins[a], outs[a].at[me], local_sems.at[a]) for a in range(n)]
        for cp in locals_:
            cp.start()
        sends = [copy(a, 0, me, sib, src=ins[a]) for a in range(n)]
        for j, k in enumerate(CHIP_K):
            peer, _ = _peer(x, y, c, k)
            sends += [copy(a, 1 + j, me, peer, src=ins[a]) for a in range(n)]
        for cp in sends:
            cp.start()
        for j, k in enumerate(CHIP_K):
            peer, pidx = _peer(x, y, c, k)
            for a in range(n):
                copy(a, 1 + j, pidx, peer).wait_recv()
                fwd = copy(a, 4 + j, pidx, sib)
                fwd.start()
                sends.append(fwd)
        for a in range(n):
            copy(a, 0, sib_idx, sib).wait_recv()
        for j, k in enumerate(CHIP_K):
            _, pidx = _peer(x, y, 1 - c, k)
            for a in range(n):
                copy(a, 4 + j, pidx, sib).wait_recv()
        for cp in sends:
            cp.wait_send()
        for cp in locals_:
            cp.wait()

    return pl.pallas_call(
        body, name=name,
        out_shape=[SDS((N_DEV,) + a.shape, a.dtype) for a in arrays],
        in_specs=[pl.BlockSpec(memory_space=pl.ANY)] * n,
        out_specs=[pl.BlockSpec(memory_space=pl.ANY)] * n,
        scratch_shapes=[pltpu.SemaphoreType.DMA((n, N_DEV - 1)), pltpu.SemaphoreType.DMA((n, N_DEV - 1)),
                        pltpu.SemaphoreType.DMA((n,))],
        compiler_params=_params(),
    )(*arrays)


def _exchange_sibling(arrays, name):
    n = len(arrays)
    ks = (0,) + CHIP_K

    def body(*refs):
        ins, outs = refs[:n], refs[n:2 * n]
        send_sems, recv_sems = refs[2 * n:]
        x, y, c = lax.axis_index("x"), lax.axis_index("y"), lax.axis_index("c")
        sib, sib_idx = _peer(x, y, c, 1)
        sends = []
        for i, k in enumerate(ks):
            _, tgt = _peer(x, y, 1 - c, k) if k else (None, sib_idx)
            for a in range(n):
                cp = pltpu.make_async_remote_copy(
                    src_ref=ins[a].at[tgt], dst_ref=outs[a].at[i], send_sem=send_sems.at[a, i],
                    recv_sem=recv_sems.at[a, i], device_id=sib, device_id_type=MESH)
                cp.start()
                sends.append(cp)
        for cp in sends:
            cp.wait_recv()
        for cp in sends:
            cp.wait_send()

    return pl.pallas_call(
        body, name=name,
        out_shape=[SDS((len(ks),) + a.shape[1:], a.dtype) for a in arrays],
        in_specs=[pl.BlockSpec(memory_space=pl.ANY)] * n,
        out_specs=[pl.BlockSpec(memory_space=pl.ANY)] * n,
        scratch_shapes=[pltpu.SemaphoreType.DMA((n, len(ks))), pltpu.SemaphoreType.DMA((n, len(ks)))],
        compiler_params=_params(),
    )(*arrays)


def _presum(mine, from_sib, me_arr, name):
    _, rows, cols = mine.shape
    tr = min(rows, 256)
    ns = 1 + len(CHIP_K)

    def body(me_ref, a_ref, b_ref, o_ref):
        del me_ref
        o_ref[...] = (a_ref[...].astype(F32) + b_ref[...].astype(F32)).astype(o_ref.dtype)

    grid_spec = pltpu.PrefetchScalarGridSpec(
        num_scalar_prefetch=1, grid=(ns, rows // tr),
        in_specs=[pl.BlockSpec((1, tr, cols), lambda j, i, me: (jnp.bitwise_xor(me[0], 2 * j), i, 0)),
                  pl.BlockSpec((1, tr, cols), lambda j, i, me: (j, i, 0))],
        out_specs=pl.BlockSpec((1, tr, cols), lambda j, i, me: (j, i, 0)))
    return pl.pallas_call(body, name=name, grid_spec=grid_spec, out_shape=SDS((ns, rows, cols), mine.dtype),
                          compiler_params=_params())(me_arr, mine, from_sib)


HBM_SPEC = pl.BlockSpec(memory_space=pltpu.HBM)
SEM_SPEC = pl.BlockSpec(memory_space=pltpu.SEMAPHORE)
SIDE_EFFECT = pltpu.SideEffectType.DATAFLOW_SIDE_EFFECTING


def _chips_copies(pre_refs, land_refs, send_sems, recv_sems):
    x, y, c = lax.axis_index("x"), lax.axis_index("y"), lax.axis_index("c")
    copies = []
    for j, k in enumerate(CHIP_K):
        peer, _ = _peer(x, y, c, k)
        for a in range(len(pre_refs)):
            copies.append(pltpu.make_async_remote_copy(
                src_ref=pre_refs[a].at[1 + j], dst_ref=land_refs[a].at[j], send_sem=send_sems.at[a * len(CHIP_K) + j],
                recv_sem=recv_sems.at[a * len(CHIP_K) + j], device_id=peer, device_id_type=MESH))
    return copies


def _exchange_chips_start(presums, name):
    n = len(presums)

    def body(*refs):
        pre, land = refs[:n], refs[n:2 * n]
        send_sems, recv_sems = refs[2 * n], refs[2 * n + 1]
        token = refs[-1]
        for cp in _chips_copies(pre, land, send_sems, recv_sems):
            cp.start()
        token[...] = jnp.zeros_like(token)

    nk = len(CHIP_K)
    hbm = [pltpu.HBM(p.shape, p.dtype) for p in presums]
    hbm_land = [pltpu.HBM((nk,) + p.shape[1:], p.dtype) for p in presums]
    res = pl.pallas_call(
        body, name=name,
        out_shape=(pltpu.SemaphoreType.DMA((n * nk,)), pltpu.SemaphoreType.DMA((n * nk,)), *hbm, *hbm_land, SDS((8, LANE), F32)),
        in_specs=[HBM_SPEC] * (2 * n),
        out_specs=(SEM_SPEC, SEM_SPEC, *([HBM_SPEC] * (2 * n)), pl.BlockSpec(memory_space=pltpu.VMEM)),
        input_output_aliases={i: 2 + i for i in range(2 * n)},
        compiler_params=pltpu.CompilerParams(has_side_effects=SIDE_EFFECT),
    )(*[pltpu.with_memory_space_constraint(p, pltpu.HBM) for p in presums],
      *[pltpu.with_memory_space_constraint(lax.empty((nk,) + p.shape[1:], p.dtype), pltpu.HBM) for p in presums])
    return res[0], res[1], res[2:2 + n], res[2 + n:2 + 2 * n], res[-1]


def _exchange_chips_wait(send_sems, recv_sems, pre_thru, land_thru, after, name):
    n = len(pre_thru)

    def body(*refs):
        pre, land = refs[:n], refs[n:2 * n]
        s_sems, r_sems = refs[2 * n], refs[2 * n + 1]
        for cp in _chips_copies(pre, land, s_sems, r_sems):
            cp.wait_send()
            cp.wait_recv()

    hbm = [pltpu.HBM(p.shape, p.dtype) for p in (*pre_thru, *land_thru)]
    res = pl.pallas_call(
        body, name=name, out_shape=tuple(hbm),
        in_specs=[HBM_SPEC] * (2 * n) + [SEM_SPEC, SEM_SPEC, pl.BlockSpec(memory_space=pl.ANY)],
        out_specs=tuple([HBM_SPEC] * (2 * n)),
        input_output_aliases={i: i for i in range(2 * n)},
        compiler_params=pltpu.CompilerParams(has_side_effects=SIDE_EFFECT),
    )(*pre_thru, *land_thru, send_sems, recv_sems, after)
    return res[:n], res[n:]


def _exchange_chips(presums, name):
    n = len(presums)
    nk = len(CHIP_K)

    def body(*refs):
        pre, land = refs[:n], refs[n:2 * n]
        send_sems, recv_sems = refs[2 * n:]
        copies = _chips_copies(pre, land, send_sems, recv_sems)
        for cp in copies:
            cp.start()
        for cp in copies:
            cp.wait_recv()
        for cp in copies:
            cp.wait_send()

    return pl.pallas_call(
        body, name=name,
        out_shape=[SDS((nk,) + p.shape[1:], p.dtype) for p in presums],
        in_specs=[pl.BlockSpec(memory_space=pl.ANY)] * n,
        out_specs=[pl.BlockSpec(memory_space=pl.ANY)] * n,
        scratch_shapes=[pltpu.SemaphoreType.DMA((n * nk,)), pltpu.SemaphoreType.DMA((n * nk,))],
        compiler_params=_params(),
    )(*presums)


def _to_window(w, c_arr):
    rows = w.shape[0]
    tr = 256

    def body(c_ref, w_ref, o_ref):
        wv = w_ref[...]
        wp = jnp.concatenate([wv, jnp.zeros((tr, WIN_W - SHARD_W), F32)], axis=1)
        ws = jnp.where(c_ref[0] == 1, pltpu.roll(wp, WIN_W - SHARD_W, axis=1), wp)
        o_ref[...] = ws.astype(BF16)

    return pl.pallas_call(
        body, name="to_window", grid=(rows // tr,),
        in_specs=[pl.BlockSpec(memory_space=pltpu.SMEM), pl.BlockSpec((tr, SHARD_W), lambda i: (i, 0))],
        out_specs=pl.BlockSpec((tr, WIN_W), lambda i: (i, 0)),
        out_shape=SDS((rows, WIN_W), BF16), compiler_params=_params(),
    )(c_arr, w)


def _cast_bf16(w, name):
    def body(w_ref, o_ref):
        o_ref[...] = w_ref[...].astype(BF16)

    return pl.pallas_call(body, name=name, out_shape=SDS(w.shape, BF16), compiler_params=_params())(w)


def _cols_from_slots(wg, name):
    _, rows, cols = wg.shape

    def body(w_ref, o_ref):
        for j in range(N_DEV):
            o_ref[:, j * cols:(j + 1) * cols] = w_ref[j]

    return pl.pallas_call(body, name=name, out_shape=SDS((rows, N_DEV * cols), wg.dtype), compiler_params=_params())(wg)


def _ada_fwd(c_all, w_ada):
    def body(c_ref, w_ref, o_ref):
        cv = c_ref[...]
        sc = (cv * _sigmoid(cv)).astype(BF16)
        o_ref[...] = _dot(sc, w_ref[...].astype(BF16))

    return pl.pallas_call(body, name="ada_fwd", out_shape=SDS((N_DEV, w_ada.shape[1]), F32),
                          compiler_params=_params())(c_all, w_ada)


def _ada_bwd(c_all, d_ada_cols):
    def body(c_ref, d_ref, o_ref):
        cv = c_ref[...]
        sc = (cv * _sigmoid(cv)).astype(BF16)
        o_ref[...] = _dot_tn(sc, d_ref[...].astype(BF16))

    return pl.pallas_call(body, name="ada_bwd", out_shape=SDS((D_MODEL, d_ada_cols.shape[1]), F32),
                          compiler_params=_params())(c_all, d_ada_cols)


def _norm_fwd(x, norm_w, scale, shift):
    s = x.shape[0]
    tr = 512

    def body(x_ref, nw_ref, sc_ref, sh_ref, h_ref, ht_ref):
        xv = x_ref[...]
        r = lax.rsqrt(jnp.mean(xv * xv, axis=-1, keepdims=True) + EPS)
        h = (xv * r * nw_ref[...]) * (1.0 + sc_ref[...]) + sh_ref[...]
        h_ref[...] = h.astype(BF16)
        ht_ref[...] = h.T.astype(BF16)

    vec = pl.BlockSpec((1, D_MODEL), lambda i: (0, 0))
    return pl.pallas_call(
        body, name="norm_fwd", grid=(s // tr,),
        in_specs=[pl.BlockSpec((tr, D_MODEL), lambda i: (i, 0)), vec, vec, vec],
        out_specs=[pl.BlockSpec((tr, D_MODEL), lambda i: (i, 0)), pl.BlockSpec((D_MODEL, tr), lambda i: (0, i))],
        out_shape=[SDS((s, D_MODEL), BF16), SDS((D_MODEL, s), BF16)], compiler_params=_params(),
    )(x, norm_w, scale, shift)


def _mm_in(h, wg):
    s = h.shape[0]
    tm = 512

    def body(h_ref, w_ref, o_ref):
        hv = h_ref[...]
        pe = _dot(hv, w_ref[0])
        po = _dot(hv, w_ref[1])
        o_ref[:, 0:1024] = pe[:, 0:1024]
        o_ref[:, 1024:1152] = pe[:, 1024:1152] + po[:, 0:128]
        o_ref[:, 1152:PAIR_W] = po[:, 128:WIN_W]

    return pl.pallas_call(
        body, name="mm_in", grid=(N_DEV // 2, s // tm),
        in_specs=[pl.BlockSpec((tm, D_MODEL), lambda p, m: (m, 0)),
                  pl.BlockSpec((2, D_MODEL, WIN_W), lambda p, m: (p, 0, 0))],
        out_specs=pl.BlockSpec((tm, PAIR_W), lambda p, m: (m, p)),
        out_shape=SDS((s, IN_W), F32), compiler_params=_params(),
    )(h, wg)


def _head_ones():
    a = lax.broadcasted_iota(jnp.int32, (LANE, LANE), 0) // HEAD_DIM
    b = lax.broadcasted_iota(jnp.int32, (LANE, LANE), 1) // HEAD_DIM
    return (a == b).astype(BF16)


def _head_sums(t, ones):
    return _dot(t.astype(BF16), ones)


def _band_bias(bias):
    qi = lax.broadcasted_iota(jnp.int32, (2 * BAND, 2 * BAND), 0) % BAND
    kj = lax.broadcasted_iota(jnp.int32, (2 * BAND, 2 * BAND), 1)
    dist = qi + BAND - kj
    valid = (dist >= 0) & (dist <= BAND)
    bias[1] = jnp.where(valid, 0.0, -1e30)
    bias[0] = jnp.where(valid & (kj >= BAND), 0.0, -1e30)


def _token_rows(j, d, chunk, per_r):
    return pl.ds(j // per_r + (j % per_r) * (chunk * d), chunk, stride=d)


def _deinterleave(src_ref, dst_ref, w_ref, ones, d, sub_len, chunk, scale, dst_off):
    per_r = sub_len // chunk

    def step(j, _):
        t = src_ref[_token_rows(j, d, chunk, per_r), :]
        if w_ref is not None:
            ms = _head_sums(t * t, ones) * (1.0 / HEAD_DIM)
            t = t * lax.rsqrt(ms + EPS) * (w_ref[...] * scale)
        dst_ref[pl.ds(pl.multiple_of(dst_off + j * chunk, BAND), chunk), :] = t.astype(dst_ref.dtype)
        return 0
    lax.fori_loop(0, d * per_r, step, 0, unroll=2)


def _attn_fwd(proj, qw2, kw2, g):
    s = proj.shape[0]
    d = DILATIONS[g]
    sub_len = s // d
    nb = sub_len // BAND
    chunk = min(sub_len, 256)

    def body(q_ref, k_ref, v_ref, qw_ref, kw_ref, o_ref, l_ref, qd, kd, vd, od, ld, bias):
        lo = lax.broadcasted_iota(jnp.int32, (1, LANE), 1) < HEAD_DIM
        ones = _head_ones()

        @pl.when(pl.program_id(0) == 0)
        def _():
            _band_bias(bias)

        kd[0:BAND, :] = jnp.zeros((BAND, LANE), BF16)
        vd[0:BAND, :] = jnp.zeros((BAND, LANE), BF16)
        _deinterleave(q_ref, qd, qw_ref, ones, d, sub_len, chunk, HEAD_DIM ** -0.5, 0)
        _deinterleave(k_ref, kd, kw_ref, ones, d, sub_len, chunk, 1.0, BAND)
        _deinterleave(v_ref, vd, None, ones, d, sub_len, chunk, 1.0, BAND)

        def block(t, _):
            base = pl.multiple_of(t * BAND, BAND)
            q = qd[pl.ds(base, BAND), :]
            k2 = kd[pl.ds(base, 2 * BAND), :]
            v2 = vd[pl.ds(base, 2 * BAND), :]
            zero = jnp.zeros_like(q)
            qs = jnp.concatenate([jnp.where(lo, q, zero), jnp.where(lo, zero, q)], axis=0)
            sc = _dot_nt(qs, k2) + bias[jnp.minimum(t % nb, 1)]
            m = jnp.max(sc, axis=-1, keepdims=True)
            p = jnp.exp(sc - m)
            den = jnp.sum(p, axis=-1, keepdims=True)
            u = _dot(p.astype(BF16), v2) * (1.0 / den)
            lse = m + jnp.log(den)
            od[pl.ds(base, BAND), :] = jnp.where(lo, u[:BAND], u[BAND:])
            ld[pl.ds(base, BAND), :] = jnp.where(lo, lse[:BAND], lse[BAND:])
            return 0
        lax.fori_loop(0, s // BAND, block, 0, unroll=2)

        per_r = sub_len // chunk

        def back(j, _):
            src = pl.ds(pl.multiple_of(j * chunk, chunk), chunk)
            dst = _token_rows(j, d, chunk, per_r)
            o_ref[dst, :] = od[src, :]
            l_ref[dst, :] = ld[src, :]
            return 0
        lax.fori_loop(0, d * per_r, back, 0, unroll=2)

    col = lambda off: pl.BlockSpec((s, LANE), lambda hp, off=off: (0, off // LANE + 4 * g + hp))
    vec = pl.BlockSpec((1, LANE), lambda hp: (0, 0))
    out = pl.BlockSpec((s, LANE), lambda hp: (0, hp))
    return pl.pallas_call(
        body, name=f"attn_fwd{g}", grid=(ATTN_W // LANE,),
        in_specs=[col(Q0), col(K0), col(V0), vec, vec], out_specs=[out, out],
        out_shape=[SDS((s, ATTN_W), F32), SDS((s, ATTN_W), F32)],
        scratch_shapes=[pltpu.VMEM((s, LANE), BF16), pltpu.VMEM((s + BAND, LANE), BF16), pltpu.VMEM((s + BAND, LANE), BF16),
                        pltpu.VMEM((s, LANE), F32), pltpu.VMEM((s, LANE), F32),
                        pltpu.VMEM((2, 2 * BAND, 2 * BAND), F32)],
        compiler_params=_params(),
    )(proj, proj, proj, qw2, kw2)


def _attn_bwd(proj, da, delta, lse, qw2, kw2, dproj, g):
    s = proj.shape[0]
    d = DILATIONS[g]
    sub_len = s // d
    nb = sub_len // BAND
    chunk = min(sub_len, 256)

    def body(q_ref, k_ref, v_ref, da_ref, dl_ref, ls_ref, qw_ref, kw_ref, dp_in, dp_out, dqw_ref, dkw_ref,
             qd, kd, vd, dad, dld, lsd, dqd, dkd, dvd, st, stb, bias, sem):
        del dp_in
        hp = pl.program_id(0)
        lo = lax.broadcasted_iota(jnp.int32, (1, LANE), 1) < HEAD_DIM
        ones = _head_ones()
        per_r = sub_len // chunk

        @pl.when(hp == 0)
        def _():
            _band_bias(bias)

        kd[0:BAND, :] = jnp.zeros((BAND, LANE), BF16)
        vd[0:BAND, :] = jnp.zeros((BAND, LANE), BF16)
        dkd[...] = jnp.zeros_like(dkd)
        dvd[...] = jnp.zeros_like(dvd)
        _deinterleave(q_ref, qd, qw_ref, ones, d, sub_len, chunk, HEAD_DIM ** -0.5, 0)
        _deinterleave(k_ref, kd, kw_ref, ones, d, sub_len, chunk, 1.0, BAND)
        _deinterleave(v_ref, vd, None, ones, d, sub_len, chunk, 1.0, BAND)
        _deinterleave(da_ref, dad, None, ones, d, sub_len, chunk, 1.0, 0)
        _deinterleave(dl_ref, dld, None, ones, d, sub_len, chunk, 1.0, 0)
        _deinterleave(ls_ref, lsd, None, ones, d, sub_len, chunk, 1.0, 0)

        def block(t, _):
            base = pl.multiple_of(t * BAND, BAND)
            q = qd[pl.ds(base, BAND), :]
            k2 = kd[pl.ds(base, 2 * BAND), :]
            v2 = vd[pl.ds(base, 2 * BAND), :]
            dav = dad[pl.ds(base, BAND), :]
            dlv = dld[pl.ds(base, BAND), :]
            lsv = lsd[pl.ds(base, BAND), :]
            zero = jnp.zeros_like(q)
            qs = jnp.concatenate([jnp.where(lo, q, zero), jnp.where(lo, zero, q)], axis=0)
            das = jnp.concatenate([jnp.where(lo, dav, zero), jnp.where(lo, zero, dav)], axis=0)
            ls_col = jnp.concatenate([lsv[:, 0:1], lsv[:, HEAD_DIM:HEAD_DIM + 1]], axis=0)
            dl_col = jnp.concatenate([dlv[:, 0:1], dlv[:, HEAD_DIM:HEAD_DIM + 1]], axis=0)
            sc = _dot_nt(qs, k2) + bias[jnp.minimum(t % nb, 1)]
            p = jnp.exp(sc - ls_col)
            dp = _dot_nt(das, v2)
            ds = (p * (dp - dl_col)).astype(BF16)
            dvd[pl.ds(base, 2 * BAND), :] += _dot_tn(p.astype(BF16), das)
            dkd[pl.ds(base, 2 * BAND), :] += _dot_tn(ds, qs)
            dq = _dot(ds, k2)
            dqd[pl.ds(base, BAND), :] = jnp.where(lo, dq[:BAND], dq[BAND:])
            return 0
        lax.fori_loop(0, s // BAND, block, 0, unroll=2)

        def store_cols(col0):
            stb[...] = st[...].astype(BF16)
            cp = pltpu.make_async_copy(
                stb, dp_out.at[:, pl.ds(pl.multiple_of(col0 + LANE * (4 * g + hp), LANE), LANE)], sem)
            cp.start()
            cp.wait()

        def norm_back(src_ref, dy_ref, dy_off, w_ref, scale, dw_ref, col0):
            def step(j, acc):
                tok = _token_rows(j, d, chunk, per_r)
                t = src_ref[tok, :]
                dy = dy_ref[pl.ds(pl.multiple_of(dy_off + j * chunk, BAND), chunk), :]
                rr = lax.rsqrt(_head_sums(t * t, ones) * (1.0 / HEAD_DIM) + EPS)
                nrm = t * rr
                acc = acc + jnp.sum(dy * nrm, axis=0, keepdims=True)
                dn = dy * (w_ref[...] * scale)
                st[tok, :] = rr * (dn - nrm * (_head_sums(dn * nrm, ones) * (1.0 / HEAD_DIM)))
                return acc
            acc = lax.fori_loop(0, d * per_r, step, jnp.zeros((1, LANE), F32), unroll=2)
            dw_ref[...] += jnp.broadcast_to(acc * scale, dw_ref.shape)
            store_cols(col0)

        @pl.when(hp == 0)
        def _():
            dqw_ref[...] = jnp.zeros_like(dqw_ref)
            dkw_ref[...] = jnp.zeros_like(dkw_ref)

        norm_back(q_ref, dqd, 0, qw_ref, HEAD_DIM ** -0.5, dqw_ref, Q0)
        norm_back(k_ref, dkd, BAND, kw_ref, 1.0, dkw_ref, K0)

        def v_back(j, _):
            src = pl.ds(pl.multiple_of(BAND + j * chunk, BAND), chunk)
            st[_token_rows(j, d, chunk, per_r), :] = dvd[src, :]
            return 0
        lax.fori_loop(0, d * per_r, v_back, 0, unroll=2)
        store_cols(V0)

    col = lambda off: pl.BlockSpec((s, LANE), lambda hp, off=off: (0, off // LANE + 4 * g + hp))
    mid = pl.BlockSpec((s, LANE), lambda hp: (0, hp))
    vec = pl.BlockSpec((1, LANE), lambda hp: (0, 0))
    acc = pl.BlockSpec((8, LANE), lambda hp: (0, 0))
    any_ = pl.BlockSpec(memory_space=pl.ANY)
    return pl.pallas_call(
        body, name=f"attn_bwd{g}", grid=(ATTN_W // LANE,),
        in_specs=[col(Q0), col(K0), col(V0), mid, mid, mid, vec, vec, any_],
        out_specs=[any_, acc, acc],
        out_shape=[SDS(dproj.shape, dproj.dtype), SDS((8, LANE), F32), SDS((8, LANE), F32)],
        input_output_aliases={8: 0},
        scratch_shapes=[pltpu.VMEM((s, LANE), BF16), pltpu.VMEM((s + BAND, LANE), BF16), pltpu.VMEM((s + BAND, LANE), BF16),
                        pltpu.VMEM((s, LANE), BF16), pltpu.VMEM((s, LANE), F32), pltpu.VMEM((s, LANE), F32),
                        pltpu.VMEM((s, LANE), F32), pltpu.VMEM((s + BAND, LANE), F32), pltpu.VMEM((s + BAND, LANE), F32),
                        pltpu.VMEM((s, LANE), F32), pltpu.VMEM((s, LANE), BF16),
                        pltpu.VMEM((2, 2 * BAND, 2 * BAND), F32), pltpu.SemaphoreType.DMA(())],
        compiler_params=_params(),
    )(proj, proj, proj, da, delta, lse, qw2, kw2, dproj)


def _silu_grad(z, sg):
    return sg * (1.0 + z * (1.0 - sg))


def _glu(u):
    a_h, b_h = u[:, :CONV_W], u[:, CONV_W:]
    sg = _sigmoid(b_h)
    return a_h, sg, a_h * sg


def _tail(x, tgt, proj, o3, l3, wa, wc, wo, gate, bga, bgc, convw, convb, lnw, lnb, bd):
    s = x.shape[0]
    tr = 256

    def body(x_ref, t_ref, za_ref, u_ref, uh_ref, zc_ref, g0_ref, g1_ref, g2_ref, g3_ref,
             o0_ref, o1_ref, o2_ref, l0_ref, l1_ref, l2_ref, wa_ref, wc_ref, wo_ref,
             gate_ref, bga_ref, bgc_ref, cw_ref, cb_ref, lnw_ref, lnb_ref, bd_ref,
             dout_ref, da_ref, dl_ref, lse_ref, dcv_ref, mt_ref, yat_ref, yct_ref, dmo_ref, dya_ref, dyc_ref, dp_ref,
             dgate_ref, dbg_ref, dlnw_ref, dlnb_ref, dcb_ref, loss_ref,
             ext, st_za, st_zc, st_g, sems):
        i = pl.program_id(0)

        @pl.when(i == 0)
        def _():
            for r in (dgate_ref, dbg_ref, dlnw_ref, dlnb_ref, dcb_ref, loss_ref):
                r[...] = jnp.zeros_like(r)

        def acc_rows(ref, v):
            ref[...] += jnp.broadcast_to(jnp.sum(v, axis=0, keepdims=True), ref.shape)

        la, lb, lc = l0_ref[...], l1_ref[...], l2_ref[...]
        mx = jnp.maximum(jnp.maximum(la, lb), lc)
        ea, eb, ec = jnp.exp(la - mx), jnp.exp(lb - mx), jnp.exp(lc - mx)
        den = ea + eb + ec
        inv = 1.0 / den
        attn = (ea * inv) * o0_ref[...] + (eb * inv) * o1_ref[...] + (ec * inv) * o2_ref[...]
        lse_ref[...] = mx + jnp.log(den)

        za = za_ref[...]
        sga = _sigmoid(za)
        sa = za * sga
        ya_in = attn * sa
        y_attn = _dot(ya_in.astype(BF16), wa_ref[...])

        _, _, glu = _glu(u_ref[...])
        _, _, glu_h = _glu(uh_ref[...])
        ext[0:CONV_HALO, :] = jnp.where(i > 0, glu_h, 0.0)
        ext[CONV_HALO:CONV_HALO + tr, :] = glu
        cv = jnp.broadcast_to(cb_ref[...], (tr, CONV_W))
        for j in range(CONV_K):
            off = CONV_HALO - (CONV_K - 1) + j
            cv = cv + cw_ref[j:j + 1, :] * ext[off:off + tr, :]
        mu = jnp.mean(cv, axis=-1, keepdims=True)
        xc = cv - mu
        rstd = lax.rsqrt(jnp.mean(xc * xc, axis=-1, keepdims=True) + EPS)
        nrm = xc * rstd
        ln = nrm * lnw_ref[...] + lnb_ref[...]
        sgl = _sigmoid(ln)
        cs = ln * sgl
        zc = zc_ref[...]
        sgc = _sigmoid(zc)
        scz = zc * sgc
        yc_in = cs * scz
        y_conv = _dot(yc_in.astype(BF16), wc_ref[...])

        ga = _sigmoid(jnp.concatenate([g0_ref[...], g1_ref[...]], axis=1) + bga_ref[...])
        gc = _sigmoid(jnp.concatenate([g2_ref[...], g3_ref[...]], axis=1) + bgc_ref[...])
        merged = ga * y_attn + gc * y_conv
        mo = _dot(merged.astype(BF16), wo_ref[...])
        gate_v = gate_ref[...]
        err = (x_ref[...] + gate_v * mo) - t_ref[...]
        loss_ref[...] += 0.5 * jnp.sum(jnp.mean(err * err, axis=-1, keepdims=True))
        d_out = err * (1.0 / D_MODEL)
        dout_ref[...] = d_out

        acc_rows(dgate_ref, d_out * mo)
        dmo_b = (d_out * gate_v).astype(BF16)
        dmo_ref[...] = dmo_b
        mt_ref[...] = merged.T.astype(BF16)
        d_merged = _dot_nt(dmo_b, wo_ref[...])
        d_ya = (d_merged * ga).astype(BF16)
        d_yc = (d_merged * gc).astype(BF16)
        dya_ref[...] = d_ya
        dyc_ref[...] = d_yc
        dga = d_merged * y_attn * (ga * (1.0 - ga))
        dgc = d_merged * y_conv * (gc * (1.0 - gc))
        dgs = jnp.concatenate([dga, dgc], axis=1)
        acc_rows(dbg_ref, dgs)
        st_g[...] = dgs.astype(BF16)

        yat_ref[...] = ya_in.T.astype(BF16)
        d_ya_in = _dot_nt(d_ya, wa_ref[...])
        d_attn = d_ya_in * sa
        da_ref[...] = d_attn
        st_za[...] = (d_ya_in * attn * _silu_grad(za, sga)).astype(BF16)
        prod = d_attn * attn
        hi = prod.astype(BF16)
        lo_ = (prod - hi.astype(F32)).astype(BF16)
        dl_ref[...] = _dot(hi, bd_ref[...]) + _dot(lo_, bd_ref[...])

        yct_ref[...] = yc_in.T.astype(BF16)
        d_yc_in = _dot_nt(d_yc, wc_ref[...])
        st_zc[...] = (d_yc_in * cs * _silu_grad(zc, sgc)).astype(BF16)
        d_ln = (d_yc_in * scz) * _silu_grad(ln, sgl)
        acc_rows(dlnw_ref, d_ln * nrm)
        acc_rows(dlnb_ref, d_ln)
        d_nrm = d_ln * lnw_ref[...]
        d_cv = rstd * (d_nrm - jnp.mean(d_nrm, axis=-1, keepdims=True)
                       - nrm * jnp.mean(d_nrm * nrm, axis=-1, keepdims=True))
        acc_rows(dcb_ref, d_cv)
        dcv_ref[...] = d_cv

        rows = pl.ds(pl.multiple_of(i * tr, tr), tr)
        cps = [pltpu.make_async_copy(st_za, dp_ref.at[rows, pl.ds(ZA0, ATTN_W)], sems.at[0]),
               pltpu.make_async_copy(st_zc, dp_ref.at[rows, pl.ds(ZC0, CONV_W)], sems.at[1]),
               pltpu.make_async_copy(st_g, dp_ref.at[rows, pl.ds(G0, 2 * D_MODEL)], sems.at[2])]
        for cp in cps:
            cp.start()
        for cp in cps:
            cp.wait()

    def rows(width, colblk=0):
        return pl.BlockSpec((tr, width), lambda i, colblk=colblk: (i, colblk))

    def const(shape):
        return pl.BlockSpec(shape, lambda i: (0,) * len(shape))

    halo = pl.BlockSpec((CONV_HALO, D_MODEL), lambda i: (jnp.maximum(i * (tr // CONV_HALO) - 1, 0), U0 // D_MODEL))
    in_specs = [rows(D_MODEL), rows(D_MODEL), rows(ATTN_W, ZA0 // ATTN_W), rows(D_MODEL, U0 // D_MODEL), halo,
                rows(CONV_W, ZC0 // CONV_W)]
    in_specs += [rows(512, G0 // 512 + j) for j in range(4)]
    in_specs += [rows(ATTN_W)] * 6
    in_specs += [const(wa.shape), const(wc.shape), const(wo.shape), const((1, D_MODEL)), const((1, D_MODEL)),
                 const((1, D_MODEL)), const(convw.shape), const((1, CONV_W)), const((1, CONV_W)), const((1, CONV_W)),
                 const(bd.shape)]
    tcol = lambda width: pl.BlockSpec((width, tr), lambda i: (0, i))
    out_specs = [rows(D_MODEL), rows(ATTN_W), rows(ATTN_W), rows(ATTN_W), rows(CONV_W),
                 tcol(D_MODEL), tcol(ATTN_W), tcol(CONV_W), rows(D_MODEL), rows(D_MODEL), rows(D_MODEL),
                 pl.BlockSpec(memory_space=pl.ANY),
                 const((8, D_MODEL)), const((8, 2 * D_MODEL)), const((8, CONV_W)), const((8, CONV_W)), const((8, CONV_W)),
                 const((8, LANE))]
    out_shape = [SDS((s, D_MODEL), F32), SDS((s, ATTN_W), F32), SDS((s, ATTN_W), F32), SDS((s, ATTN_W), F32),
                 SDS((s, CONV_W), F32),
                 SDS((D_MODEL, s), BF16), SDS((ATTN_W, s), BF16), SDS((CONV_W, s), BF16),
                 SDS((s, D_MODEL), BF16), SDS((s, D_MODEL), BF16), SDS((s, D_MODEL), BF16),
                 SDS((s, IN_W), BF16),
                 SDS((8, D_MODEL), F32), SDS((8, 2 * D_MODEL), F32), SDS((8, CONV_W), F32), SDS((8, CONV_W), F32),
                 SDS((8, CONV_W), F32), SDS((8, LANE), F32)]
    return pl.pallas_call(
        body, name="tail", grid=(s // tr,), in_specs=in_specs, out_specs=out_specs, out_shape=out_shape,
        scratch_shapes=[pltpu.VMEM((CONV_HALO + tr, CONV_W), F32), pltpu.VMEM((tr, ATTN_W), BF16),
                        pltpu.VMEM((tr, CONV_W), BF16), pltpu.VMEM((tr, 2 * D_MODEL), BF16),
                        pltpu.SemaphoreType.DMA((3,))],
        compiler_params=_params(),
    )(x, tgt, proj, proj, proj, proj, proj, proj, proj, proj, *o3, *l3, wa, wc, wo, gate, bga, bgc,
      convw, convb, lnw, lnb, bd)


def _conv_bwd(dcv, proj, convw, dproj):
    s = dcv.shape[0]
    tr = 256
    nt = s // tr

    def body(dcv_ref, dcvn_ref, u_ref, uh_ref, cw_ref, dp_in, dp_out, dw_ref, extg, extd):
        del dp_in
        i = pl.program_id(0)

        @pl.when(i == 0)
        def _():
            dw_ref[...] = jnp.zeros_like(dw_ref)

        a_h, sgb, glu = _glu(u_ref[...])
        _, _, glu_h = _glu(uh_ref[...])
        extg[0:CONV_HALO, :] = jnp.where(i > 0, glu_h, 0.0)
        extg[CONV_HALO:CONV_HALO + tr, :] = glu
        dcv_v = dcv_ref[...]
        extd[0:tr, :] = dcv_v
        extd[tr:tr + CONV_HALO, :] = jnp.where(i < nt - 1, dcvn_ref[...], 0.0)
        dglu = jnp.zeros((tr, CONV_W), F32)
        for j in range(CONV_K):
            back = CONV_K - 1 - j
            dglu = dglu + cw_ref[j:j + 1, :] * extd[back:back + tr, :]
            off = CONV_HALO - (CONV_K - 1) + j
            dw_ref[j:j + 1, :] += jnp.sum(dcv_v * extg[off:off + tr, :], axis=0, keepdims=True)
        d_a = dglu * sgb
        d_b = dglu * a_h * (sgb * (1.0 - sgb))
        dp_out[...] = jnp.concatenate([d_a, d_b], axis=1).astype(BF16)

    ucol = U0 // D_MODEL
    return pl.pallas_call(
        body, name="conv_bwd", grid=(nt,),
        in_specs=[pl.BlockSpec((tr, CONV_W), lambda i: (i, 0)),
                  pl.BlockSpec((CONV_HALO, CONV_W), lambda i: (jnp.minimum((i + 1) * (tr // CONV_HALO), s // CONV_HALO - 1), 0)),
                  pl.BlockSpec((tr, D_MODEL), lambda i: (i, ucol)),
                  pl.BlockSpec((CONV_HALO, D_MODEL), lambda i: (jnp.maximum(i * (tr // CONV_HALO) - 1, 0), ucol)),
                  pl.BlockSpec(convw.shape, lambda i: (0, 0)),
                  pl.BlockSpec(memory_space=pl.ANY)],
        out_specs=[pl.BlockSpec((tr, D_MODEL), lambda i: (i, ucol)), pl.BlockSpec((CONV_HALO, CONV_W), lambda i: (0, 0))],
        out_shape=[SDS(dproj.shape, dproj.dtype), SDS((CONV_HALO, CONV_W), F32)],
        input_output_aliases={5: 0},
        scratch_shapes=[pltpu.VMEM((CONV_HALO + tr, CONV_W), F32), pltpu.VMEM((CONV_HALO + tr, CONV_W), F32)],
        compiler_params=_params(),
    )(dcv, dcv, proj, proj, convw, dproj)


def _mm_acc(at, b, name, col_slots):
    m, s = at.shape
    n = b.shape[1]
    tk = 512
    nk = s // tk

    def body(a_ref, b_ref, o_ref, acc):
        k = pl.program_id(0)

        @pl.when(k == 0)
        def _():
            acc[...] = jnp.zeros_like(acc)

        acc[...] += _dot(a_ref[...], b_ref[...])

        @pl.when(k == nk - 1)
        def _():
            if col_slots:
                w = n // N_DEV
                for j in range(N_DEV):
                    o_ref[j] = acc[:, j * w:(j + 1) * w].astype(BF16)
            else:
                o_ref[...] = acc[...].astype(BF16)

    if col_slots:
        out_shape = SDS((N_DEV, m, n // N_DEV), BF16)
        out_spec = pl.BlockSpec((N_DEV, m, n // N_DEV), lambda k: (0, 0, 0))
    else:
        out_shape = SDS((m, n), BF16)
        out_spec = pl.BlockSpec((m, n), lambda k: (0, 0))
    return pl.pallas_call(
        body, name=name, grid=(nk,),
        in_specs=[pl.BlockSpec((m, tk), lambda k: (0, k)), pl.BlockSpec((tk, n), lambda k: (k, 0))],
        out_specs=out_spec, out_shape=out_shape, scratch_shapes=[pltpu.VMEM((m, n), F32)],
        compiler_params=_params(),
    )(at, b)


def _mm_dw(ht, dproj):
    s = ht.shape[1]
    tk = 512
    nk = s // tk

    def body(a_ref, b_ref, o_ref, acc):
        k = pl.program_id(1)

        @pl.when(k == 0)
        def _():
            acc[...] = jnp.zeros_like(acc)

        acc[...] += _dot(a_ref[...], b_ref[...])

        @pl.when(k == nk - 1)
        def _():
            o_ref[0] = acc[:, 0:WIN_W].astype(BF16)
            o_ref[1] = acc[:, PAIR_W - WIN_W:PAIR_W].astype(BF16)

    return pl.pallas_call(
        body, name="mm_dw", grid=(N_DEV // 2, nk),
        in_specs=[pl.BlockSpec((D_MODEL, tk), lambda p, k: (0, k)), pl.BlockSpec((tk, PAIR_W), lambda p, k: (k, p))],
        out_specs=pl.BlockSpec((2, D_MODEL, WIN_W), lambda p, k: (p, 0, 0)),
        out_shape=SDS((N_DEV, D_MODEL, WIN_W), BF16), scratch_shapes=[pltpu.VMEM((D_MODEL, PAIR_W), F32)],
        compiler_params=_params(),
    )(ht, dproj)


def _mm_dh(dproj, wg, token):
    s = dproj.shape[0]
    tm = 1024

    def body(dp_ref, w_ref, tok_ref, o_ref):
        del tok_ref
        p = pl.program_id(1)
        part = (_dot_nt(dp_ref[:, 0:WIN_W], w_ref[0]) + _dot_nt(dp_ref[:, PAIR_W - WIN_W:PAIR_W], w_ref[1]))

        @pl.when(p == 0)
        def _():
            o_ref[...] = part

        @pl.when(p > 0)
        def _():
            o_ref[...] += part

    return pl.pallas_call(
        body, name="mm_dh", grid=(s // tm, N_DEV // 2),
        in_specs=[pl.BlockSpec((tm, PAIR_W), lambda m, p: (m, p)),
                  pl.BlockSpec((2, D_MODEL, WIN_W), lambda m, p: (p, 0, 0)),
                  pl.BlockSpec(token.shape, lambda m, p: (0, 0))],
        out_specs=pl.BlockSpec((tm, D_MODEL), lambda m, p: (m, 0)),
        out_shape=SDS((s, D_MODEL), F32), compiler_params=_params(),
    )(dproj, wg, token)


def _norm_bwd(x, dh, dout, norm_w, scale):
    s = x.shape[0]
    tr = 512

    def body(x_ref, dh_ref, do_ref, nw_ref, sc_ref, gx_ref, dsh_ref, dsc_ref, dnw_ref):
        i = pl.program_id(0)

        @pl.when(i == 0)
        def _():
            for r in (dsh_ref, dsc_ref, dnw_ref):
                r[...] = jnp.zeros_like(r)

        def acc_rows(ref, v):
            ref[...] += jnp.broadcast_to(jnp.sum(v, axis=0, keepdims=True), ref.shape)

        xv = x_ref[...]
        dh_v = dh_ref[...]
        r = lax.rsqrt(jnp.mean(xv * xv, axis=-1, keepdims=True) + EPS)
        xn = xv * r
        one_sc = 1.0 + sc_ref[...]
        acc_rows(dsh_ref, dh_v)
        acc_rows(dsc_ref, dh_v * (xn * nw_ref[...]))
        acc_rows(dnw_ref, dh_v * xn * one_sc)
        dxn = dh_v * (nw_ref[...] * one_sc)
        gx_ref[...] = do_ref[...] + r * (dxn - xn * jnp.mean(dxn * xn, axis=-1, keepdims=True))

    blk = pl.BlockSpec((tr, D_MODEL), lambda i: (i, 0))
    vec = pl.BlockSpec((1, D_MODEL), lambda i: (0, 0))
    acc = pl.BlockSpec((8, D_MODEL), lambda i: (0, 0))
    return pl.pallas_call(
        body, name="norm_bwd", grid=(s // tr,), in_specs=[blk, blk, blk, vec, vec],
        out_specs=[blk, acc, acc, acc],
        out_shape=[SDS((s, D_MODEL), F32)] + [SDS((8, D_MODEL), F32)] * 3, compiler_params=_params(),
    )(x, dh, dout, norm_w, scale)


def _adamw(gsrc, w, m, v, name, stacked, c_arr=None):
    rows, cols = w.shape
    tr = rows if rows <= 128 else 128
    unshift = c_arr is not None
    bc1 = 1.0 - ADAM_B1 ** ADAM_STEP
    bc2 = 1.0 - ADAM_B2 ** ADAM_STEP
    n_src = len(gsrc) if stacked else 1

    def body(*refs):
        if unshift:
            c_ref, refs = refs[0], refs[1:]
        g_refs, (w_ref, m_ref, v_ref, go_ref, d_ref, mo_ref, vo_ref) = refs[:n_src], refs[n_src:]
        if stacked:
            g = None
            for g_ref, (_, slots) in zip(g_refs, gsrc):
                for j in range(slots):
                    t = g_ref[j].astype(F32)
                    g = t if g is None else g + t
        else:
            g = g_refs[0][...]
        if unshift:
            g = jnp.where(c_ref[0] == 1, pltpu.roll(g, SHARD_W, axis=1), g)[:, :SHARD_W]
        m_new = ADAM_B1 * m_ref[...] + (1.0 - ADAM_B1) * g
        v_new = ADAM_B2 * v_ref[...] + (1.0 - ADAM_B2) * (g * g)
        m_hat = m_new / bc1
        v_hat = v_new / bc2
        go_ref[...] = g
        d_ref[...] = -ADAM_LR * (m_hat / (jnp.sqrt(v_hat) + ADAM_EPS) + ADAM_WD * w_ref[...])
        mo_ref[...] = m_new
        vo_ref[...] = v_new

    blk = pl.BlockSpec((tr, cols), lambda i: (i, 0))
    if stacked:
        gspecs = [pl.BlockSpec((slots, tr, arr.shape[2]), lambda i: (0, i, 0)) for arr, slots in gsrc]
        gargs = [arr for arr, _ in gsrc]
    else:
        gspecs, gargs = [blk], [gsrc]
    in_specs = gspecs + [blk, blk, blk]
    args = gargs + [w, m, v]
    if unshift:
        in_specs = [pl.BlockSpec(memory_space=pltpu.SMEM)] + in_specs
        args = [c_arr] + args
    return pl.pallas_call(
        body, name=name, grid=(rows // tr,), in_specs=in_specs, out_specs=[blk] * 4,
        out_shape=[SDS((rows, cols), F32)] * 4, compiler_params=_params(),
    )(*args)


def _pack_small(parts):
    cols = []
    for name, length in _SMALL:
        p = parts[name].reshape(1, -1)
        cols.append(jnp.pad(p, ((0, 0), (0, length - p.shape[1]))))
    return jnp.concatenate(cols, axis=1)


def _unpack_small(vec, name, n):
    off, _ = SMALL_OFF[name]
    return vec[:, off:off + n]


def kernel(x, c, w_ada, b_ada, norm_w, w_in, b_gate, q_norm_w, k_norm_w, w_attn_proj, conv_w, conv_b, conv_ln_w, conv_ln_b, w_conv_proj, w_out, loss_target, m_w_ada, m_b_ada, m_norm_w, m_w_in, m_b_gate, m_q_norm_w, m_k_norm_w, m_w_attn_proj, m_conv_w, m_conv_b, m_conv_ln_w, m_conv_ln_b, m_w_conv_proj, m_w_out, v_w_ada, v_b_ada, v_norm_w, v_w_in, v_b_gate, v_q_norm_w, v_k_norm_w, v_w_attn_proj, v_conv_w, v_conv_b, v_conv_ln_w, v_conv_ln_b, v_w_conv_proj, v_w_out):
    xi, yi, ci = lax.axis_index("x"), lax.axis_index("y"), lax.axis_index("c")
    me = 4 * xi + 2 * yi + ci
    c_arr = jnp.reshape(ci, (1,)).astype(jnp.int32)
    x2, tgt2 = x[0], loss_target[0]
    s = x2.shape[0]

    cw_flat = jnp.pad(conv_w[0].reshape(1, -1), ((0, 0), (0, CONVW_FLAT - CONV_K * HEAD_DIM)))
    pre = jnp.concatenate([c, cw_flat], axis=1).reshape(8, -1)
    (pre_all,) = _all_gather([pre], "gather_c_convw", vmem=True)
    pre_all = pre_all.reshape(N_DEV, -1)
    c_all = pre_all[:, :D_MODEL]
    convw_full = pre_all[:, D_MODEL:D_MODEL + CONV_K * HEAD_DIM].reshape(N_DEV, CONV_K, HEAD_DIM)
    convw_full = jnp.transpose(convw_full, (1, 0, 2)).reshape(CONV_K, CONV_W)
    convw_pad = jnp.pad(convw_full, ((0, CONV_HALO - CONV_K), (0, 0)))

    ada_part = _ada_fwd(c_all, w_ada[0])
    (ada_all,) = _all_gather([ada_part], "gather_ada", vmem=True)
    ada = lax.dynamic_index_in_dim(ada_all, me, axis=1, keepdims=False).reshape(1, 3 * D_MODEL) + b_ada
    shift, scale, gate = ada[:, :D_MODEL], ada[:, D_MODEL:2 * D_MODEL], ada[:, 2 * D_MODEL:]

    wg, wa_g, wc_g, wo_g = _all_gather_chips(
        [_to_window(w_in[0], c_arr), _cast_bf16(w_attn_proj[0], "cast_wa"), _cast_bf16(w_conv_proj[0], "cast_wc"),
         _cast_bf16(w_out[0], "cast_wo")], "gather_weights")
    wa = _cols_from_slots(wa_g, "cols_wa")
    wc = _cols_from_slots(wc_g, "cols_wc")
    wo = wo_g.reshape(D_MODEL, D_MODEL)

    h, ht = _norm_fwd(x2, norm_w, scale, shift)
    proj = _mm_in(h, wg)
    qw2 = jnp.tile(q_norm_w, (1, 2))
    kw2 = jnp.tile(k_norm_w, (1, 2))
    o3, l3 = [], []
    for g in range(N_GROUPS):
        o_g, l_g = _attn_fwd(proj, qw2, kw2, g)
        o3.append(o_g)
        l3.append(l_g)
    head_id = jnp.arange(ATTN_W) // HEAD_DIM
    bd = (head_id[:, None] == head_id[None, :]).astype(BF16)
    (dout, da, delta, lse, dcv, mt, yat, yct, dmo, dya, dyc, dproj,
     dgate, dbg, dlnw, dlnb, dcb, loss_p) = _tail(
        x2, tgt2, proj, o3, l3, wa, wc, wo, gate, b_gate[:, :D_MODEL], b_gate[:, D_MODEL:], convw_pad,
        conv_b, conv_ln_w, conv_ln_b, bd)

    dproj, dconvw = _conv_bwd(dcv, proj, convw_pad, dproj)
    dqw = jnp.zeros((1, HEAD_DIM), F32)
    dkw = jnp.zeros((1, HEAD_DIM), F32)
    for g in range(N_GROUPS):
        dproj, dqw_g, dkw_g = _attn_bwd(proj, da, delta, lse, qw2, kw2, dproj, g)
        dqw = dqw + dqw_g[0:1, :HEAD_DIM] + dqw_g[0:1, HEAD_DIM:]
        dkw = dkw + dkw_g[0:1, :HEAD_DIM] + dkw_g[0:1, HEAD_DIM:]
    dw_in_p = _mm_dw(ht, dproj)
    dwo_p = _mm_acc(mt, dmo, "mm_dwo", col_slots=False).reshape(N_DEV, D_MODEL // N_DEV, D_MODEL)
    dwa_p = _mm_acc(yat, dya, "mm_dwa", col_slots=True)
    dwc_p = _mm_acc(yct, dyc, "mm_dwc", col_slots=True)

    partials = [dw_in_p, dwa_p, dwc_p, dwo_p]
    me_arr = jnp.reshape(me, (1,)).astype(jnp.int32)
    from_sib = _exchange_sibling(partials, "exchange_sibling")
    presums = [_presum(p, f, me_arr, f"presum{i}") for i, (p, f) in enumerate(zip(partials, from_sib))]
    s_sems, r_sems, pre_thru, land_thru, token = _exchange_chips_start(presums, "exchange_chips_start")
    dh = _mm_dh(dproj, wg, token)
    gx, dsh, dsc, dnw = _norm_bwd(x2, dh, dout, norm_w, scale)
    d_ada = jnp.concatenate([dsh[0:1], dsc[0:1], dgate[0:1]], axis=1)
    small_p = _pack_small({"b_ada": d_ada, "norm_w": dnw[0:1], "b_gate": dbg[0:1], "q_norm_w": dqw, "k_norm_w": dkw,
                           "conv_b": dcb[0:1], "conv_ln_w": dlnw[0:1], "conv_ln_b": dlnb[0:1], "loss": loss_p[0:1]})
    small_all, dconvw_all = _all_gather([jnp.broadcast_to(small_p, (8, SMALL_N)), dconvw], "gather_small", vmem=True)
    small_all = small_all[:, 0:1, :]
    dcw_mine = lax.dynamic_slice_in_dim(dconvw_all[:, :CONV_K, :], me * HEAD_DIM, HEAD_DIM, axis=2)
    dcw_mine = jnp.pad(dcw_mine.reshape(N_DEV, 1, -1), ((0, 0), (0, 0), (0, CONVW_FLAT - CONV_K * HEAD_DIM)))
    pack_g = jnp.concatenate([small_all, dcw_mine], axis=2)

    def pack_params(tree):
        small = _pack_small({"b_ada": tree["b_ada"], "norm_w": tree["norm_w"], "b_gate": tree["b_gate"],
                             "q_norm_w": tree["q_norm_w"], "k_norm_w": tree["k_norm_w"], "conv_b": tree["conv_b"],
                             "conv_ln_w": tree["conv_ln_w"], "conv_ln_b": tree["conv_ln_b"],
                             "loss": jnp.ones((1, 1), F32)})
        cw = jnp.pad(tree["conv_w"][0].reshape(1, -1), ((0, 0), (0, CONVW_FLAT - CONV_K * HEAD_DIM)))
        return jnp.concatenate([small, cw], axis=1)

    names = ("b_ada", "norm_w", "b_gate", "q_norm_w", "k_norm_w", "conv_b", "conv_ln_w", "conv_ln_b", "conv_w")
    w_tree = dict(zip(names, (b_ada, norm_w, b_gate, q_norm_w, k_norm_w, conv_b, conv_ln_w, conv_ln_b, conv_w)))
    m_tree = dict(zip(names, (m_b_ada, m_norm_w, m_b_gate, m_q_norm_w, m_k_norm_w, m_conv_b, m_conv_ln_w, m_conv_ln_b, m_conv_w)))
    v_tree = dict(zip(names, (v_b_ada, v_norm_w, v_b_gate, v_q_norm_w, v_k_norm_w, v_conv_b, v_conv_ln_w, v_conv_ln_b, v_conv_w)))
    pk = _adamw([(pack_g, N_DEV)], pack_params(w_tree), pack_params(m_tree), pack_params(v_tree), "adamw_small",
                stacked=True)

    d_ada_all = small_all[:, 0, :3 * D_MODEL]
    d_ada_cols = lax.dynamic_slice_in_dim(d_ada_all, me * (3 * D_MODEL // N_DEV), 3 * D_MODEL // N_DEV, axis=1)
    g_wada = _ada_bwd(c_all, d_ada_cols)
    r_ada = _adamw(g_wada, w_ada[0], m_w_ada[0], v_w_ada[0], "adamw_w_ada", stacked=False)
    pres, lands = _exchange_chips_wait(s_sems, r_sems, pre_thru, land_thru, r_ada[1], "exchange_chips_wait")
    terms = [[(p, 1), (l, len(CHIP_K))] for p, l in zip(pres, lands)]
    r_win = _adamw(terms[0], w_in[0], m_w_in[0], v_w_in[0], "adamw_w_in", stacked=True, c_arr=c_arr)
    r_wap = _adamw(terms[1], w_attn_proj[0], m_w_attn_proj[0], v_w_attn_proj[0], "adamw_w_attn_proj", stacked=True)
    r_wcp = _adamw(terms[2], w_conv_proj[0], m_w_conv_proj[0], v_w_conv_proj[0], "adamw_w_conv_proj", stacked=True)
    r_wout = _adamw(terms[3], w_out[0], m_w_out[0], v_w_out[0], "adamw_w_out", stacked=True)

    def small_out(k, name, n):
        return _unpack_small(pk[k], name, n)

    def convw_out(k):
        return pk[k][:, SMALL_N:SMALL_N + CONV_K * HEAD_DIM].reshape(1, CONV_K, HEAD_DIM)

    loss = pk[0][0, SMALL_OFF["loss"][0]]
    outs = [loss, gx[None]]
    for k in range(4):
        outs += [r_ada[k][None], small_out(k, "b_ada", 3 * D_MODEL), small_out(k, "norm_w", D_MODEL), r_win[k][None],
                 small_out(k, "b_gate", 2 * D_MODEL), small_out(k, "q_norm_w", HEAD_DIM), small_out(k, "k_norm_w", HEAD_DIM),
                 r_wap[k][None], convw_out(k), small_out(k, "conv_b", CONV_W), small_out(k, "conv_ln_w", CONV_W),
                 small_out(k, "conv_ln_b", CONV_W), r_wcp[k][None], r_wout[k][None]]
    return tuple(outs)
```

```python
import functools

import jax
import jax.numpy as jnp
from jax import lax
from jax.experimental import pallas as pl
from jax.experimental.pallas import tpu as pltpu

F32 = jnp.float32
BF16 = jnp.bfloat16
SDS = jax.ShapeDtypeStruct
MESH = pl.DeviceIdType.MESH

N_DEV = 8
D_MODEL = 1024
HEAD_DIM = 64
N_GROUPS = 3
DILATIONS = (1, 4, 16)
BAND = 128
BWD_UNROLL = 8
ATTN_W = 512
CONV_W = 512
CONV_K = 31
CONV_HALO = 32
IN_W = 8704
SHARD_W = IN_W // N_DEV
WIN_W = 1152
PAIR_W = 2 * SHARD_W
Q0, K0, V0, ZA0, U0, ZC0, G0 = 0, 1536, 3072, 4608, 5120, 6144, 6656
EPS = 1e-6
LANE = 128
VMEM_LIMIT = 56 * 1024 * 1024

ADAM_LR, ADAM_B1, ADAM_B2, ADAM_EPS, ADAM_WD, ADAM_STEP = 0.001, 0.9, 0.999, 1e-08, 0.01, 10

_SMALL = (("b_ada", 3072), ("norm_w", 1024), ("b_gate", 2048), ("q_norm_w", 128), ("k_norm_w", 128),
          ("conv_b", 512), ("conv_ln_w", 512), ("conv_ln_b", 512), ("loss", 128))
SMALL_OFF = {}
_o = 0
for _n, _l in _SMALL:
    SMALL_OFF[_n] = (_o, _l)
    _o += _l
SMALL_N = _o
CONVW_FLAT = 2048
PACK_N = SMALL_N + CONVW_FLAT


def _params(**kw):
    return pltpu.CompilerParams(vmem_limit_bytes=VMEM_LIMIT, **kw)


def _sigmoid(z):
    return 1.0 / (1.0 + jnp.exp(-z))


def _dot(a, b):
    return jnp.dot(a, b, preferred_element_type=F32)


def _dot_nt(a, b):
    return lax.dot_general(a, b, (((1,), (1,)), ((), ())), preferred_element_type=F32)


def _dot_tn(a, b):
    return lax.dot_general(a, b, (((0,), (0,)), ((), ())), preferred_element_type=F32)


def _peer(x, y, c, k):
    px = 1 - x if (k >> 2) & 1 else x
    py = 1 - y if (k >> 1) & 1 else y
    pc = 1 - c if k & 1 else c
    return (px, py, pc), 4 * px + 2 * py + pc


def _all_gather(arrays, name, vmem):
    n = len(arrays)
    space = pltpu.VMEM if vmem else pl.ANY

    def body(*refs):
        ins, outs = refs[:n], refs[n:2 * n]
        send_sems, recv_sems, local_sems = refs[2 * n:]
        x, y, c = lax.axis_index("x"), lax.axis_index("y"), lax.axis_index("c")
        me = 4 * x + 2 * y + c
        locals_ = [pltpu.make_async_copy(ins[a], outs[a].at[me], local_sems.at[a]) for a in range(n)]
        for cp in locals_:
            cp.start()
        sends = []
        for k in range(1, N_DEV):
            peer, _ = _peer(x, y, c, k)
            for a in range(n):
                cp = pltpu.make_async_remote_copy(
                    src_ref=ins[a], dst_ref=outs[a].at[me], send_sem=send_sems.at[a, k - 1],
                    recv_sem=recv_sems.at[a, k - 1], device_id=peer, device_id_type=MESH)
                cp.start()
                sends.append(cp)
        for k in range(1, N_DEV):
            peer, pidx = _peer(x, y, c, k)
            for a in range(n):
                pltpu.make_async_remote_copy(
                    src_ref=ins[a], dst_ref=outs[a].at[pidx], send_sem=send_sems.at[a, k - 1],
                    recv_sem=recv_sems.at[a, k - 1], device_id=peer, device_id_type=MESH).wait_recv()
        for cp in sends:
            cp.wait_send()
        for cp in locals_:
            cp.wait()

    return pl.pallas_call(
        body, name=name,
        out_shape=[SDS((N_DEV,) + a.shape, a.dtype) for a in arrays],
        in_specs=[pl.BlockSpec(memory_space=space)] * n,
        out_specs=[pl.BlockSpec(memory_space=space)] * n,
        scratch_shapes=[pltpu.SemaphoreType.DMA((n, N_DEV - 1)), pltpu.SemaphoreType.DMA((n, N_DEV - 1)),
                        pltpu.SemaphoreType.DMA((n,))],
        compiler_params=_params(),
    )(*arrays)


CHIP_K = (2, 4, 6)


def _all_gather_chips(arrays, name):
    n = len(arrays)

    def body(*refs):
        ins, outs = refs[:n], refs[n:2 * n]
        send_sems, recv_sems, local_sems = refs[2 * n:]
        x, y, c = lax.axis_index("x"), lax.axis_index("y"), lax.axis_index("c")
        me = 4 * x + 2 * y + c
        sib, sib_idx = _peer(x, y, c, 1)

        def copy(a, slot, block, to, src=None):
            return pltpu.make_async_remote_copy(
                src_ref=outs[a].at[block] if src is None else src, dst_ref=outs[a].at[block],
                send_sem=send_sems.at[a, slot], recv_sem=recv_sems.at[a, slot], device_id=to, device_id_type=MESH)

        locals_ = [pltpu.make_async_copy(ins[a], outs[a].at[me], local_sems.at[a]) for a in range(n)]
        for cp in locals_:
            cp.start()
        sends = [copy(a, 0, me, sib, src=ins[a]) for a in range(n)]
        for j, k in enumerate(CHIP_K):
            peer, _ = _peer(x, y, c, k)
            sends += [copy(a, 1 + j, me, peer, src=ins[a]) for a in range(n)]
        for cp in sends:
            cp.start()
        for j, k in enumerate(CHIP_K):
            peer, pidx = _peer(x, y, c, k)
            for a in range(n):
                copy(a, 1 + j, pidx, peer).wait_recv()
                fwd = copy(a, 4 + j, pidx, sib)
                fwd.start()
                sends.append(fwd)
        for a in range(n):
            copy(a, 0, sib_idx, sib).wait_recv()
        for j, k in enumerate(CHIP_K):
            _, pidx = _peer(x, y, 1 - c, k)
            for a in range(n):
                copy(a, 4 + j, pidx, sib).wait_recv()
        for cp in sends:
            cp.wait_send()
        for cp in locals_:
            cp.wait()

    return pl.pallas_call(
        body, name=name,
        out_shape=[SDS((N_DEV,) + a.shape, a.dtype) for a in arrays],
        in_specs=[pl.BlockSpec(memory_space=pl.ANY)] * n,
        out_specs=[pl.BlockSpec(memory_space=pl.ANY)] * n,
        scratch_shapes=[pltpu.SemaphoreType.DMA((n, N_DEV - 1)), pltpu.SemaphoreType.DMA((n, N_DEV - 1)),
                        pltpu.SemaphoreType.DMA((n,))],
        compiler_params=_params(),
    )(*arrays)


def _exchange_sibling(arrays, name):
    n = len(arrays)
    ks = (0,) + CHIP_K

    def body(*refs):
        ins, outs = refs[:n], refs[n:2 * n]
        send_sems, recv_sems = refs[2 * n:]
        x, y, c = lax.axis_index("x"), lax.axis_index("y"), lax.axis_index("c")
        sib, sib_idx = _peer(x, y, c, 1)
        sends = []
        for i, k in enumerate(ks):
            _, tgt = _peer(x, y, 1 - c, k) if k else (None, sib_idx)
            for a in range(n):
                cp = pltpu.make_async_remote_copy(
                    src_ref=ins[a].at[tgt], dst_ref=outs[a].at[i], send_sem=send_sems.at[a, i],
                    recv_sem=recv_sems.at[a, i], device_id=sib, device_id_type=MESH)
                cp.start()
                sends.append(cp)
        for cp in sends:
            cp.wait_recv()
        for cp in sends:
            cp.wait_send()

    return pl.pallas_call(
        body, name=name,
        out_shape=[SDS((len(ks),) + a.shape[1:], a.dtype) for a in arrays],
        in_specs=[pl.BlockSpec(memory_space=pl.ANY)] * n,
        out_specs=[pl.BlockSpec(memory_space=pl.ANY)] * n,
        scratch_shapes=[pltpu.SemaphoreType.DMA((n, len(ks))), pltpu.SemaphoreType.DMA((n, len(ks)))],
        compiler_params=_params(),
    )(*arrays)


def _presum(mine, from_sib, me_arr, name):
    _, rows, cols = mine.shape
    tr = min(rows, 256)
    ns = 1 + len(CHIP_K)

    def body(me_ref, a_ref, b_ref, o_ref):
        del me_ref
        o_ref[...] = (a_ref[...].astype(F32) + b_ref[...].astype(F32)).astype(o_ref.dtype)

    grid_spec = pltpu.PrefetchScalarGridSpec(
        num_scalar_prefetch=1, grid=(ns, rows // tr),
        in_specs=[pl.BlockSpec((1, tr, cols), lambda j, i, me: (jnp.bitwise_xor(me[0], 2 * j), i, 0)),
                  pl.BlockSpec((1, tr, cols), lambda j, i, me: (j, i, 0))],
        out_specs=pl.BlockSpec((1, tr, cols), lambda j, i, me: (j, i, 0)))
    return pl.pallas_call(body, name=name, grid_spec=grid_spec, out_shape=SDS((ns, rows, cols), mine.dtype),
                          compiler_params=_params())(me_arr, mine, from_sib)


HBM_SPEC = pl.BlockSpec(memory_space=pltpu.HBM)
SEM_SPEC = pl.BlockSpec(memory_space=pltpu.SEMAPHORE)
SIDE_EFFECT = pltpu.SideEffectType.DATAFLOW_SIDE_EFFECTING


def _chips_copies(pre_refs, land_refs, send_sems, recv_sems):
    x, y, c = lax.axis_index("x"), lax.axis_index("y"), lax.axis_index("c")
    copies = []
    for j, k in enumerate(CHIP_K):
        peer, _ = _peer(x, y, c, k)
        for a in range(len(pre_refs)):
            copies.append(pltpu.make_async_remote_copy(
                src_ref=pre_refs[a].at[1 + j], dst_ref=land_refs[a].at[j], send_sem=send_sems.at[a * len(CHIP_K) + j],
                recv_sem=recv_sems.at[a * len(CHIP_K) + j], device_id=peer, device_id_type=MESH))
    return copies


def _exchange_chips_start(presums, name):
    n = len(presums)

    def body(*refs):
        pre, land = refs[:n], refs[n:2 * n]
        send_sems, recv_sems = refs[2 * n], refs[2 * n + 1]
        token = refs[-1]
        for cp in _chips_copies(pre, land, send_sems, recv_sems):
            cp.start()
        token[...] = jnp.zeros_like(token)

    nk = len(CHIP_K)
    hbm = [pltpu.HBM(p.shape, p.dtype) for p in presums]
    hbm_land = [pltpu.HBM((nk,) + p.shape[1:], p.dtype) for p in presums]
    res = pl.pallas_call(
        body, name=name,
        out_shape=(pltpu.SemaphoreType.DMA((n * nk,)), pltpu.SemaphoreType.DMA((n * nk,)), *hbm, *hbm_land, SDS((8, LANE), F32)),
        in_specs=[HBM_SPEC] * (2 * n),
        out_specs=(SEM_SPEC, SEM_SPEC, *([HBM_SPEC] * (2 * n)), pl.BlockSpec(memory_space=pltpu.VMEM)),
        input_output_aliases={i: 2 + i for i in range(2 * n)},
        compiler_params=pltpu.CompilerParams(has_side_effects=SIDE_EFFECT),
    )(*[pltpu.with_memory_space_constraint(p, pltpu.HBM) for p in presums],
      *[pltpu.with_memory_space_constraint(lax.empty((nk,) + p.shape[1:], p.dtype), pltpu.HBM) for p in presums])
    return res[0], res[1], res[2:2 + n], res[2 + n:2 + 2 * n], res[-1]


def _exchange_chips_wait(send_sems, recv_sems, pre_thru, land_thru, after, name):
    n = len(pre_thru)

    def body(*refs):
        pre, land = refs[:n], refs[n:2 * n]
        s_sems, r_sems = refs[2 * n], refs[2 * n + 1]
        for cp in _chips_copies(pre, land, s_sems, r_sems):
            cp.wait_send()
            cp.wait_recv()

    hbm = [pltpu.HBM(p.shape, p.dtype) for p in (*pre_thru, *land_thru)]
    res = pl.pallas_call(
        body, name=name, out_shape=tuple(hbm),
        in_specs=[HBM_SPEC] * (2 * n) + [SEM_SPEC, SEM_SPEC, pl.BlockSpec(memory_space=pl.ANY)],
        out_specs=tuple([HBM_SPEC] * (2 * n)),
        input_output_aliases={i: i for i in range(2 * n)},
        compiler_params=pltpu.CompilerParams(has_side_effects=SIDE_EFFECT),
    )(*pre_thru, *land_thru, send_sems, recv_sems, after)
    return res[:n], res[n:]


def _exchange_chips(presums, name):
    n = len(presums)
    nk = len(CHIP_K)

    def body(*refs):
        pre, land = refs[:n], refs[n:2 * n]
        send_sems, recv_sems = refs[2 * n:]
        copies = _chips_copies(pre, land, send_sems, recv_sems)
        for cp in copies:
            cp.start()
        for cp in copies:
            cp.wait_recv()
        for cp in copies:
            cp.wait_send()

    return pl.pallas_call(
        body, name=name,
        out_shape=[SDS((nk,) + p.shape[1:], p.dtype) for p in presums],
        in_specs=[pl.BlockSpec(memory_space=pl.ANY)] * n,
        out_specs=[pl.BlockSpec(memory_space=pl.ANY)] * n,
        scratch_shapes=[pltpu.SemaphoreType.DMA((n * nk,)), pltpu.SemaphoreType.DMA((n * nk,))],
        compiler_params=_params(),
    )(*presums)


def _to_window(w, c_arr):
    rows = w.shape[0]
    tr = 256

    def body(c_ref, w_ref, o_ref):
        wv = w_ref[...]
        wp = jnp.concatenate([wv, jnp.zeros((tr, WIN_W - SHARD_W), F32)], axis=1)
        ws = jnp.where(c_ref[0] == 1, pltpu.roll(wp, WIN_W - SHARD_W, axis=1), wp)
        o_ref[...] = ws.astype(BF16)

    return pl.pallas_call(
        body, name="to_window", grid=(rows // tr,),
        in_specs=[pl.BlockSpec(memory_space=pltpu.SMEM), pl.BlockSpec((tr, SHARD_W), lambda i: (i, 0))],
        out_specs=pl.BlockSpec((tr, WIN_W), lambda i: (i, 0)),
        out_shape=SDS((rows, WIN_W), BF16), compiler_params=_params(),
    )(c_arr, w)


def _cast_bf16(w, name):
    def body(w_ref, o_ref):
        o_ref[...] = w_ref[...].astype(BF16)

    return pl.pallas_call(body, name=name, out_shape=SDS(w.shape, BF16), compiler_params=_params())(w)


def _cols_from_slots(wg, name):
    _, rows, cols = wg.shape

    def body(w_ref, o_ref):
        for j in range(N_DEV):
            o_ref[:, j * cols:(j + 1) * cols] = w_ref[j]

    return pl.pallas_call(body, name=name, out_shape=SDS((rows, N_DEV * cols), wg.dtype), compiler_params=_params())(wg)


def _ada_fwd(c_all, w_ada):
    def body(c_ref, w_ref, o_ref):
        cv = c_ref[...]
        sc = (cv * _sigmoid(cv)).astype(BF16)
        o_ref[...] = _dot(sc, w_ref[...].astype(BF16))

    return pl.pallas_call(body, name="ada_fwd", out_shape=SDS((N_DEV, w_ada.shape[1]), F32),
                          compiler_params=_params())(c_all, w_ada)


def _ada_bwd(c_all, d_ada_cols):
    def body(c_ref, d_ref, o_ref):
        cv = c_ref[...]
        sc = (cv * _sigmoid(cv)).astype(BF16)
        o_ref[...] = _dot_tn(sc, d_ref[...].astype(BF16))

    return pl.pallas_call(body, name="ada_bwd", out_shape=SDS((D_MODEL, d_ada_cols.shape[1]), F32),
                          compiler_params=_params())(c_all, d_ada_cols)


def _norm_fwd(x, norm_w, scale, shift):
    s = x.shape[0]
    tr = 512

    def body(x_ref, nw_ref, sc_ref, sh_ref, h_ref, ht_ref):
        xv = x_ref[...]
        r = lax.rsqrt(jnp.mean(xv * xv, axis=-1, keepdims=True) + EPS)
        h = (xv * r * nw_ref[...]) * (1.0 + sc_ref[...]) + sh_ref[...]
        h_ref[...] = h.astype(BF16)
        ht_ref[...] = h.T.astype(BF16)

    vec = pl.BlockSpec((1, D_MODEL), lambda i: (0, 0))
    return pl.pallas_call(
        body, name="norm_fwd", grid=(s // tr,),
        in_specs=[pl.BlockSpec((tr, D_MODEL), lambda i: (i, 0)), vec, vec, vec],
        out_specs=[pl.BlockSpec((tr, D_MODEL), lambda i: (i, 0)), pl.BlockSpec((D_MODEL, tr), lambda i: (0, i))],
        out_shape=[SDS((s, D_MODEL), BF16), SDS((D_MODEL, s), BF16)], compiler_params=_params(),
    )(x, norm_w, scale, shift)


def _mm_in(h, wg):
    s = h.shape[0]
    tm = 512

    def body(h_ref, w_ref, o_ref):
        hv = h_ref[...]
        pe = _dot(hv, w_ref[0])
        po = _dot(hv, w_ref[1])
        o_ref[:, 0:1024] = pe[:, 0:1024]
        o_ref[:, 1024:1152] = pe[:, 1024:1152] + po[:, 0:128]
        o_ref[:, 1152:PAIR_W] = po[:, 128:WIN_W]

    return pl.pallas_call(
        body, name="mm_in", grid=(N_DEV // 2, s // tm),
        in_specs=[pl.BlockSpec((tm, D_MODEL), lambda p, m: (m, 0)),
                  pl.BlockSpec((2, D_MODEL, WIN_W), lambda p, m: (p, 0, 0))],
        out_specs=pl.BlockSpec((tm, PAIR_W), lambda p, m: (m, p)),
        out_shape=SDS((s, IN_W), F32), compiler_params=_params(),
    )(h, wg)


def _head_ones():
    a = lax.broadcasted_iota(jnp.int32, (LANE, LANE), 0) // HEAD_DIM
    b = lax.broadcasted_iota(jnp.int32, (LANE, LANE), 1) // HEAD_DIM
    return (a == b).astype(BF16)


def _head_sums(t, ones):
    return _dot(t.astype(BF16), ones)


def _band_bias(bias):
    qi = lax.broadcasted_iota(jnp.int32, (2 * BAND, 2 * BAND), 0) % BAND
    kj = lax.broadcasted_iota(jnp.int32, (2 * BAND, 2 * BAND), 1)
    dist = qi + BAND - kj
    valid = (dist >= 0) & (dist <= BAND)
    bias[1] = jnp.where(valid, 0.0, -1e30)
    bias[0] = jnp.where(valid & (kj >= BAND), 0.0, -1e30)


def _token_rows(j, d, chunk, per_r):
    return pl.ds(j // per_r + (j % per_r) * (chunk * d), chunk, stride=d)


def _deinterleave(src_ref, dst_ref, w_ref, ones, d, sub_len, chunk, scale, dst_off):
    per_r = sub_len // chunk

    def step(j, _):
        t = src_ref[_token_rows(j, d, chunk, per_r), :]
        if w_ref is not None:
            ms = _head_sums(t * t, ones) * (1.0 / HEAD_DIM)
            t = t * lax.rsqrt(ms + EPS) * (w_ref[...] * scale)
        dst_ref[pl.ds(pl.multiple_of(dst_off + j * chunk, BAND), chunk), :] = t.astype(dst_ref.dtype)
        return 0
    lax.fori_loop(0, d * per_r, step, 0, unroll=4)


def _attn_fwd(proj, qw2, kw2, g):
    s = proj.shape[0]
    d = DILATIONS[g]
    sub_len = s // d
    nb = sub_len // BAND
    chunk = min(sub_len, 256)

    def body(q_ref, k_ref, v_ref, qw_ref, kw_ref, o_ref, l_ref, qd, kd, vd, od, ld, bias):
        lo = lax.broadcasted_iota(jnp.int32, (1, LANE), 1) < HEAD_DIM
        ones = _head_ones()

        @pl.when(pl.program_id(0) == 0)
        def _():
            _band_bias(bias)

        kd[0:BAND, :] = jnp.zeros((BAND, LANE), BF16)
        vd[0:BAND, :] = jnp.zeros((BAND, LANE), BF16)
        _deinterleave(q_ref, qd, qw_ref, ones, d, sub_len, chunk, HEAD_DIM ** -0.5, 0)
        _deinterleave(k_ref, kd, kw_ref, ones, d, sub_len, chunk, 1.0, BAND)
        _deinterleave(v_ref, vd, None, ones, d, sub_len, chunk, 1.0, BAND)

        def block(t, _):
            base = pl.multiple_of(t * BAND, BAND)
            q = qd[pl.ds(base, BAND), :]
            k2 = kd[pl.ds(base, 2 * BAND), :]
            v2 = vd[pl.ds(base, 2 * BAND), :]
            zero = jnp.zeros_like(q)
            qs = jnp.concatenate([jnp.where(lo, q, zero), jnp.where(lo, zero, q)], axis=0)
            sc = _dot_nt(qs, k2) + bias[jnp.minimum(t % nb, 1)]
            m = jnp.max(sc, axis=-1, keepdims=True)
            p = jnp.exp(sc - m)
            den = jnp.sum(p, axis=-1, keepdims=True)
            u = _dot(p.astype(BF16), v2) * (1.0 / den)
            lse = m + jnp.log(den)
            od[pl.ds(base, BAND), :] = jnp.where(lo, u[:BAND], u[BAND:])
            ld[pl.ds(base, BAND), :] = jnp.where(lo, lse[:BAND], lse[BAND:])
            return 0
        lax.fori_loop(0, s // BAND, block, 0, unroll=16)

        per_r = sub_len // chunk

        def back(j, _):
            src = pl.ds(pl.multiple_of(j * chunk, chunk), chunk)
            dst = _token_rows(j, d, chunk, per_r)
            o_ref[dst, :] = od[src, :]
            l_ref[dst, :] = ld[src, :]
            return 0
        lax.fori_loop(0, d * per_r, back, 0, unroll=2)

    col = lambda off: pl.BlockSpec((s, LANE), lambda hp, off=off: (0, off // LANE + 4 * g + hp))
    vec = pl.BlockSpec((1, LANE), lambda hp: (0, 0))
    out = pl.BlockSpec((s, LANE), lambda hp: (0, hp))
    return pl.pallas_call(
        body, name=f"attn_fwd{g}", grid=(ATTN_W // LANE,),
        in_specs=[col(Q0), col(K0), col(V0), vec, vec], out_specs=[out, out],
        out_shape=[SDS((s, ATTN_W), F32), SDS((s, ATTN_W), F32)],
        scratch_shapes=[pltpu.VMEM((s, LANE), BF16), pltpu.VMEM((s + BAND, LANE), BF16), pltpu.VMEM((s + BAND, LANE), BF16),
                        pltpu.VMEM((s, LANE), F32), pltpu.VMEM((s, LANE), F32),
                        pltpu.VMEM((2, 2 * BAND, 2 * BAND), F32)],
        compiler_params=_params(),
    )(proj, proj, proj, qw2, kw2)


def _attn_bwd(proj, da, delta, lse, qw2, kw2, dproj, g):
    s = proj.shape[0]
    d = DILATIONS[g]
    sub_len = s // d
    nb = sub_len // BAND
    chunk = min(sub_len, 256)

    def body(q_ref, k_ref, v_ref, da_ref, dl_ref, ls_ref, qw_ref, kw_ref, dp_in, dp_out, dqw_ref, dkw_ref,
             qd, kd, vd, dad, dld, lsd, dqd, dkd, dvd, st, stb, bias, wacc, sem):
        del dp_in
        hp = pl.program_id(0)
        lo = lax.broadcasted_iota(jnp.int32, (1, LANE), 1) < HEAD_DIM
        ones = _head_ones()
        per_r = sub_len // chunk

        @pl.when(hp == 0)
        def _():
            _band_bias(bias)

        kd[0:BAND, :] = jnp.zeros((BAND, LANE), BF16)
        vd[0:BAND, :] = jnp.zeros((BAND, LANE), BF16)
        _deinterleave(q_ref, qd, qw_ref, ones, d, sub_len, chunk, HEAD_DIM ** -0.5, 0)
        _deinterleave(k_ref, kd, kw_ref, ones, d, sub_len, chunk, 1.0, BAND)
        _deinterleave(v_ref, vd, None, ones, d, sub_len, chunk, 1.0, BAND)
        _deinterleave(da_ref, dad, None, ones, d, sub_len, chunk, 1.0, 0)
        _deinterleave(dl_ref, dld, None, ones, d, sub_len, chunk, 1.0, 0)
        _deinterleave(ls_ref, lsd, None, ones, d, sub_len, chunk, 1.0, 0)

        def block(t, carry):
            ck, cv = carry
            base = pl.multiple_of(t * BAND, BAND)
            q = qd[pl.ds(base, BAND), :]
            k2 = kd[pl.ds(base, 2 * BAND), :]
            v2 = vd[pl.ds(base, 2 * BAND), :]
            dav = dad[pl.ds(base, BAND), :]
            dlv = dld[pl.ds(base, BAND), :]
            lsv = lsd[pl.ds(base, BAND), :]
            zero = jnp.zeros_like(q)
            qs = jnp.concatenate([jnp.where(lo, q, zero), jnp.where(lo, zero, q)], axis=0)
            das = jnp.concatenate([jnp.where(lo, dav, zero), jnp.where(lo, zero, dav)], axis=0)
            ls_col = jnp.concatenate([lsv[:, 0:1], lsv[:, HEAD_DIM:HEAD_DIM + 1]], axis=0)
            dl_col = jnp.concatenate([dlv[:, 0:1], dlv[:, HEAD_DIM:HEAD_DIM + 1]], axis=0)
            sc = _dot_nt(qs, k2) + bias[jnp.minimum(t % nb, 1)]
            p = jnp.exp(sc - ls_col)
            dp = _dot_nt(das, v2)
            ds = (p * (dp - dl_col)).astype(BF16)
            dv2 = _dot_tn(p.astype(BF16), das)
            dk2 = _dot_tn(ds, qs)
            dvd[pl.ds(base, BAND), :] = cv + dv2[:BAND]
            dkd[pl.ds(base, BAND), :] = ck + dk2[:BAND]
            dq = _dot(ds, k2)
            dqd[pl.ds(base, BAND), :] = jnp.where(lo, dq[:BAND], dq[BAND:])
            return dk2[BAND:], dv2[BAND:]
        def blocks(i, carry):
            for u in range(BWD_UNROLL):
                carry = block(i * BWD_UNROLL + u, carry)
            return carry
        zeros = jnp.zeros((BAND, LANE), F32)
        ck, cv = lax.fori_loop(0, s // (BAND * BWD_UNROLL), blocks, (zeros, zeros))
        dkd[s:s + BAND, :] = ck
        dvd[s:s + BAND, :] = cv

        def store_cols(col0):
            stb[...] = st[...].astype(BF16)
            cp = pltpu.make_async_copy(
                stb, dp_out.at[:, pl.ds(pl.multiple_of(col0 + LANE * (4 * g + hp), LANE), LANE)], sem)
            cp.start()
            cp.wait()

        def norm_back(src_ref, dy_ref, dy_off, w_ref, scale, dw_ref, col0):
            wacc[...] = jnp.zeros_like(wacc)

            def step(j, _):
                tok = _token_rows(j, d, chunk, per_r)
                t = src_ref[tok, :]
                dy = dy_ref[pl.ds(pl.multiple_of(dy_off + j * chunk, BAND), chunk), :]
                rr = lax.rsqrt(_head_sums(t * t, ones) * (1.0 / HEAD_DIM) + EPS)
                nrm = t * rr
                wacc[...] += jnp.sum((dy * nrm).reshape(chunk // 8, 8, LANE), axis=0)
                dn = dy * (w_ref[...] * scale)
                st[tok, :] = rr * (dn - nrm * (_head_sums(dn * nrm, ones) * (1.0 / HEAD_DIM)))
                return 0
            lax.fori_loop(0, d * per_r, step, 0, unroll=4)
            dw_ref[...] += jnp.broadcast_to(jnp.sum(wacc[...], axis=0, keepdims=True) * scale, dw_ref.shape)
            store_cols(col0)

        @pl.when(hp == 0)
        def _():
            dqw_ref[...] = jnp.zeros_like(dqw_ref)
            dkw_ref[...] = jnp.zeros_like(dkw_ref)

        norm_back(q_ref, dqd, 0, qw_ref, HEAD_DIM ** -0.5, dqw_ref, Q0)
        norm_back(k_ref, dkd, BAND, kw_ref, 1.0, dkw_ref, K0)

        def v_back(j, _):
            src = pl.ds(pl.multiple_of(BAND + j * chunk, BAND), chunk)
            st[_token_rows(j, d, chunk, per_r), :] = dvd[src, :]
            return 0
        lax.fori_loop(0, d * per_r, v_back, 0, unroll=2)
        store_cols(V0)

    col = lambda off: pl.BlockSpec((s, LANE), lambda hp, off=off: (0, off // LANE + 4 * g + hp))
    mid = pl.BlockSpec((s, LANE), lambda hp: (0, hp))
    vec = pl.BlockSpec((1, LANE), lambda hp: (0, 0))
    acc = pl.BlockSpec((8, LANE), lambda hp: (0, 0))
    any_ = pl.BlockSpec(memory_space=pl.ANY)
    return pl.pallas_call(
        body, name=f"attn_bwd{g}", grid=(ATTN_W // LANE,),
        in_specs=[col(Q0), col(K0), col(V0), mid, mid, mid, vec, vec, any_],
        out_specs=[any_, acc, acc],
        out_shape=[SDS(dproj.shape, dproj.dtype), SDS((8, LANE), F32), SDS((8, LANE), F32)],
        input_output_aliases={8: 0},
        scratch_shapes=[pltpu.VMEM((s, LANE), BF16), pltpu.VMEM((s + BAND, LANE), BF16), pltpu.VMEM((s + BAND, LANE), BF16),
                        pltpu.VMEM((s, LANE), BF16), pltpu.VMEM((s, LANE), F32), pltpu.VMEM((s, LANE), F32),
                        pltpu.VMEM((s, LANE), F32), pltpu.VMEM((s + BAND, LANE), F32), pltpu.VMEM((s + BAND, LANE), F32),
                        pltpu.VMEM((s, LANE), F32), pltpu.VMEM((s, LANE), BF16),
                        pltpu.VMEM((2, 2 * BAND, 2 * BAND), F32), pltpu.VMEM((8, LANE), F32),
                        pltpu.SemaphoreType.DMA(())],
        compiler_params=_params(),
    )(proj, proj, proj, da, delta, lse, qw2, kw2, dproj)


def _silu_grad(z, sg):
    return sg * (1.0 + z * (1.0 - sg))


def _glu(u):
    a_h, b_h = u[:, :CONV_W], u[:, CONV_W:]
    sg = _sigmoid(b_h)
    return a_h, sg, a_h * sg


def _tail(x, tgt, proj, o3, l3, wa, wc, wo, gate, bga, bgc, convw, convb, lnw, lnb, bd):
    s = x.shape[0]
    tr = 256

    def body(x_ref, t_ref, za_ref, u_ref, uh_ref, zc_ref, g0_ref, g1_ref, g2_ref, g3_ref,
             o0_ref, o1_ref, o2_ref, l0_ref, l1_ref, l2_ref, wa_ref, wc_ref, wo_ref,
             gate_ref, bga_ref, bgc_ref, cw_ref, cb_ref, lnw_ref, lnb_ref, bd_ref,
             dout_ref, da_ref, dl_ref, lse_ref, dcv_ref, mt_ref, yat_ref, yct_ref, dmo_ref, dya_ref, dyc_ref, dp_ref,
             dgate_ref, dbg_ref, dlnw_ref, dlnb_ref, dcb_ref, loss_ref,
             ext, st_za, st_zc, st_g, sems):
        i = pl.program_id(0)

        @pl.when(i == 0)
        def _():
            for r in (dgate_ref, dbg_ref, dlnw_ref, dlnb_ref, dcb_ref, loss_ref):
                r[...] = jnp.zeros_like(r)

        def acc_rows(ref, v):
            ref[...] += jnp.broadcast_to(jnp.sum(v, axis=0, keepdims=True), ref.shape)

        la, lb, lc = l0_ref[...], l1_ref[...], l2_ref[...]
        mx = jnp.maximum(jnp.maximum(la, lb), lc)
        ea, eb, ec = jnp.exp(la - mx), jnp.exp(lb - mx), jnp.exp(lc - mx)
        den = ea + eb + ec
        inv = 1.0 / den
        attn = (ea * inv) * o0_ref[...] + (eb * inv) * o1_ref[...] + (ec * inv) * o2_ref[...]
        lse_ref[...] = mx + jnp.log(den)

        za = za_ref[...]
        sga = _sigmoid(za)
        sa = za * sga
        ya_in = attn * sa
        y_attn = _dot(ya_in.astype(BF16), wa_ref[...])

        _, _, glu = _glu(u_ref[...])
        _, _, glu_h = _glu(uh_ref[...])
        ext[0:CONV_HALO, :] = jnp.where(i > 0, glu_h, 0.0)
        ext[CONV_HALO:CONV_HALO + tr, :] = glu
        cv = jnp.broadcast_to(cb_ref[...], (tr, CONV_W))
        for j in range(CONV_K):
            off = CONV_HALO - (CONV_K - 1) + j
            cv = cv + cw_ref[j:j + 1, :] * ext[off:off + tr, :]
        mu = jnp.mean(cv, axis=-1, keepdims=True)
        xc = cv - mu
        rstd = lax.rsqrt(jnp.mean(xc * xc, axis=-1, keepdims=True) + EPS)
        nrm = xc * rstd
        ln = nrm * lnw_ref[...] + lnb_ref[...]
        sgl = _sigmoid(ln)
        cs = ln * sgl
        zc = zc_ref[...]
        sgc = _sigmoid(zc)
        scz = zc * sgc
        yc_in = cs * scz
        y_conv = _dot(yc_in.astype(BF16), wc_ref[...])

        ga = _sigmoid(jnp.concatenate([g0_ref[...], g1_ref[...]], axis=1) + bga_ref[...])
        gc = _sigmoid(jnp.concatenate([g2_ref[...], g3_ref[...]], axis=1) + bgc_ref[...])
        merged = ga * y_attn + gc * y_conv
        mo = _dot(merged.astype(BF16), wo_ref[...])
        gate_v = gate_ref[...]
        err = (x_ref[...] + gate_v * mo) - t_ref[...]
        loss_ref[...] += 0.5 * jnp.sum(jnp.mean(err * err, axis=-1, keepdims=True))
        d_out = err * (1.0 / D_MODEL)
        dout_ref[...] = d_out

        acc_rows(dgate_ref, d_out * mo)
        dmo_b = (d_out * gate_v).astype(BF16)
        dmo_ref[...] = dmo_b
        mt_ref[...] = merged.T.astype(BF16)
        d_merged = _dot_nt(dmo_b, wo_ref[...])
        d_ya = (d_merged * ga).astype(BF16)
        d_yc = (d_merged * gc).astype(BF16)
        dya_ref[...] = d_ya
        dyc_ref[...] = d_yc
        dga = d_merged * y_attn * (ga * (1.0 - ga))
        dgc = d_merged * y_conv * (gc * (1.0 - gc))
        dgs = jnp.concatenate([dga, dgc], axis=1)
        acc_rows(dbg_ref, dgs)
        st_g[...] = dgs.astype(BF16)

        yat_ref[...] = ya_in.T.astype(BF16)
        d_ya_in = _dot_nt(d_ya, wa_ref[...])
        d_attn = d_ya_in * sa
        da_ref[...] = d_attn
        st_za[...] = (d_ya_in * attn * _silu_grad(za, sga)).astype(BF16)
        prod = d_attn * attn
        hi = prod.astype(BF16)
        lo_ = (prod - hi.astype(F32)).astype(BF16)
        dl_ref[...] = _dot(hi, bd_ref[...]) + _dot(lo_, bd_ref[...])

        yct_ref[...] = yc_in.T.astype(BF16)
        d_yc_in = _dot_nt(d_yc, wc_ref[...])
        st_zc[...] = (d_yc_in * cs * _silu_grad(zc, sgc)).astype(BF16)
        d_ln = (d_yc_in * scz) * _silu_grad(ln, sgl)
        acc_rows(dlnw_ref, d_ln * nrm)
        acc_rows(dlnb_ref, d_ln)
        d_nrm = d_ln * lnw_ref[...]
        d_cv = rstd * (d_nrm - jnp.mean(d_nrm, axis=-1, keepdims=True)
                       - nrm * jnp.mean(d_nrm * nrm, axis=-1, keepdims=True))
        acc_rows(dcb_ref, d_cv)
        dcv_ref[...] = d_cv

        rows = pl.ds(pl.multiple_of(i * tr, tr), tr)
        cps = [pltpu.make_async_copy(st_za, dp_ref.at[rows, pl.ds(ZA0, ATTN_W)], sems.at[0]),
               pltpu.make_async_copy(st_zc, dp_ref.at[rows, pl.ds(ZC0, CONV_W)], sems.at[1]),
               pltpu.make_async_copy(st_g, dp_ref.at[rows, pl.ds(G0, 2 * D_MODEL)], sems.at[2])]
        for cp in cps:
            cp.start()
        for cp in cps:
            cp.wait()

    def rows(width, colblk=0):
        return pl.BlockSpec((tr, width), lambda i, colblk=colblk: (i, colblk))

    def const(shape):
        return pl.BlockSpec(shape, lambda i: (0,) * len(shape))

    halo = pl.BlockSpec((CONV_HALO, D_MODEL), lambda i: (jnp.maximum(i * (tr // CONV_HALO) - 1, 0), U0 // D_MODEL))
    in_specs = [rows(D_MODEL), rows(D_MODEL), rows(ATTN_W, ZA0 // ATTN_W), rows(D_MODEL, U0 // D_MODEL), halo,
                rows(CONV_W, ZC0 // CONV_W)]
    in_specs += [rows(512, G0 // 512 + j) for j in range(4)]
    in_specs += [rows(ATTN_W)] * 6
    in_specs += [const(wa.shape), const(wc.shape), const(wo.shape), const((1, D_MODEL)), const((1, D_MODEL)),
                 const((1, D_MODEL)), const(convw.shape), const((1, CONV_W)), const((1, CONV_W)), const((1, CONV_W)),
                 const(bd.shape)]
    tcol = lambda width: pl.BlockSpec((width, tr), lambda i: (0, i))
    out_specs = [rows(D_MODEL), rows(ATTN_W), rows(ATTN_W), rows(ATTN_W), rows(CONV_W),
                 tcol(D_MODEL), tcol(ATTN_W), tcol(CONV_W), rows(D_MODEL), rows(D_MODEL), rows(D_MODEL),
                 pl.BlockSpec(memory_space=pl.ANY),
                 const((8, D_MODEL)), const((8, 2 * D_MODEL)), const((8, CONV_W)), const((8, CONV_W)), const((8, CONV_W)),
                 const((8, LANE))]
    out_shape = [SDS((s, D_MODEL), F32), SDS((s, ATTN_W), F32), SDS((s, ATTN_W), F32), SDS((s, ATTN_W), F32),
                 SDS((s, CONV_W), F32),
                 SDS((D_MODEL, s), BF16), SDS((ATTN_W, s), BF16), SDS((CONV_W, s), BF16),
                 SDS((s, D_MODEL), BF16), SDS((s, D_MODEL), BF16), SDS((s, D_MODEL), BF16),
                 SDS((s, IN_W), BF16),
                 SDS((8, D_MODEL), F32), SDS((8, 2 * D_MODEL), F32), SDS((8, CONV_W), F32), SDS((8, CONV_W), F32),
                 SDS((8, CONV_W), F32), SDS((8, LANE), F32)]
    return pl.pallas_call(
        body, name="tail", grid=(s // tr,), in_specs=in_specs, out_specs=out_specs, out_shape=out_shape,
        scratch_shapes=[pltpu.VMEM((CONV_HALO + tr, CONV_W), F32), pltpu.VMEM((tr, ATTN_W), BF16),
                        pltpu.VMEM((tr, CONV_W), BF16), pltpu.VMEM((tr, 2 * D_MODEL), BF16),
                        pltpu.SemaphoreType.DMA((3,))],
        compiler_params=_params(),
    )(x, tgt, proj, proj, proj, proj, proj, proj, proj, proj, *o3, *l3, wa, wc, wo, gate, bga, bgc,
      convw, convb, lnw, lnb, bd)


def _conv_bwd(dcv, proj, convw, dproj):
    s = dcv.shape[0]
    tr = 256
    nt = s // tr

    def body(dcv_ref, dcvn_ref, u_ref, uh_ref, cw_ref, dp_in, dp_out, dw_ref, extg, extd):
        del dp_in
        i = pl.program_id(0)

        @pl.when(i == 0)
        def _():
            dw_ref[...] = jnp.zeros_like(dw_ref)

        a_h, sgb, glu = _glu(u_ref[...])
        _, _, glu_h = _glu(uh_ref[...])
        extg[0:CONV_HALO, :] = jnp.where(i > 0, glu_h, 0.0)
        extg[CONV_HALO:CONV_HALO + tr, :] = glu
        dcv_v = dcv_ref[...]
        extd[0:tr, :] = dcv_v
        extd[tr:tr + CONV_HALO, :] = jnp.where(i < nt - 1, dcvn_ref[...], 0.0)
        dglu = jnp.zeros((tr, CONV_W), F32)
        for j in range(CONV_K):
            back = CONV_K - 1 - j
            dglu = dglu + cw_ref[j:j + 1, :] * extd[back:back + tr, :]
            off = CONV_HALO - (CONV_K - 1) + j
            dw_ref[j:j + 1, :] += jnp.sum(dcv_v * extg[off:off + tr, :], axis=0, keepdims=True)
        d_a = dglu * sgb
        d_b = dglu * a_h * (sgb * (1.0 - sgb))
        dp_out[...] = jnp.concatenate([d_a, d_b], axis=1).astype(BF16)

    ucol = U0 // D_MODEL
    return pl.pallas_call(
        body, name="conv_bwd", grid=(nt,),
        in_specs=[pl.BlockSpec((tr, CONV_W), lambda i: (i, 0)),
                  pl.BlockSpec((CONV_HALO, CONV_W), lambda i: (jnp.minimum((i + 1) * (tr // CONV_HALO), s // CONV_HALO - 1), 0)),
                  pl.BlockSpec((tr, D_MODEL), lambda i: (i, ucol)),
                  pl.BlockSpec((CONV_HALO, D_MODEL), lambda i: (jnp.maximum(i * (tr // CONV_HALO) - 1, 0), ucol)),
                  pl.BlockSpec(convw.shape, lambda i: (0, 0)),
                  pl.BlockSpec(memory_space=pl.ANY)],
        out_specs=[pl.BlockSpec((tr, D_MODEL), lambda i: (i, ucol)), pl.BlockSpec((CONV_HALO, CONV_W), lambda i: (0, 0))],
        out_shape=[SDS(dproj.shape, dproj.dtype), SDS((CONV_HALO, CONV_W), F32)],
        input_output_aliases={5: 0},
        scratch_shapes=[pltpu.VMEM((CONV_HALO + tr, CONV_W), F32), pltpu.VMEM((CONV_HALO + tr, CONV_W), F32)],
        compiler_params=_params(),
    )(dcv, dcv, proj, proj, convw, dproj)


def _mm_acc(at, b, name, col_slots):
    m, s = at.shape
    n = b.shape[1]
    tk = 512
    nk = s // tk

    def body(a_ref, b_ref, o_ref, acc):
        k = pl.program_id(0)

        @pl.when(k == 0)
        def _():
            acc[...] = jnp.zeros_like(acc)

        acc[...] += _dot(a_ref[...], b_ref[...])

        @pl.when(k == nk - 1)
        def _():
            if col_slots:
                w = n // N_DEV
                for j in range(N_DEV):
                    o_ref[j] = acc[:, j * w:(j + 1) * w].astype(BF16)
            else:
                o_ref[...] = acc[...].astype(BF16)

    if col_slots:
        out_shape = SDS((N_DEV, m, n // N_DEV), BF16)
        out_spec = pl.BlockSpec((N_DEV, m, n // N_DEV), lambda k: (0, 0, 0))
    else:
        out_shape = SDS((m, n), BF16)
        out_spec = pl.BlockSpec((m, n), lambda k: (0, 0))
    return pl.pallas_call(
        body, name=name, grid=(nk,),
        in_specs=[pl.BlockSpec((m, tk), lambda k: (0, k)), pl.BlockSpec((tk, n), lambda k: (k, 0))],
        out_specs=out_spec, out_shape=out_shape, scratch_shapes=[pltpu.VMEM((m, n), F32)],
        compiler_params=_params(),
    )(at, b)


def _mm_dw(ht, dproj):
    s = ht.shape[1]
    tk = 512
    nk = s // tk

    def body(a_ref, b_ref, o_ref, acc):
        k = pl.program_id(1)

        @pl.when(k == 0)
        def _():
            acc[...] = jnp.zeros_like(acc)

        acc[...] += _dot(a_ref[...], b_ref[...])

        @pl.when(k == nk - 1)
        def _():
            o_ref[0] = acc[:, 0:WIN_W].astype(BF16)
            o_ref[1] = acc[:, PAIR_W - WIN_W:PAIR_W].astype(BF16)

    return pl.pallas_call(
        body, name="mm_dw", grid=(N_DEV // 2, nk),
        in_specs=[pl.BlockSpec((D_MODEL, tk), lambda p, k: (0, k)), pl.BlockSpec((tk, PAIR_W), lambda p, k: (k, p))],
        out_specs=pl.BlockSpec((2, D_MODEL, WIN_W), lambda p, k: (p, 0, 0)),
        out_shape=SDS((N_DEV, D_MODEL, WIN_W), BF16), scratch_shapes=[pltpu.VMEM((D_MODEL, PAIR_W), F32)],
        compiler_params=_params(),
    )(ht, dproj)


def _mm_dh(dproj, wg, token):
    s = dproj.shape[0]
    tm = 1024

    def body(dp_ref, w_ref, tok_ref, o_ref):
        del tok_ref
        p = pl.program_id(1)
        part = (_dot_nt(dp_ref[:, 0:WIN_W], w_ref[0]) + _dot_nt(dp_ref[:, PAIR_W - WIN_W:PAIR_W], w_ref[1]))

        @pl.when(p == 0)
        def _():
            o_ref[...] = part

        @pl.when(p > 0)
        def _():
            o_ref[...] += part

    return pl.pallas_call(
        body, name="mm_dh", grid=(s // tm, N_DEV // 2),
        in_specs=[pl.BlockSpec((tm, PAIR_W), lambda m, p: (m, p)),
                  pl.BlockSpec((2, D_MODEL, WIN_W), lambda m, p: (p, 0, 0)),
                  pl.BlockSpec(token.shape, lambda m, p: (0, 0))],
        out_specs=pl.BlockSpec((tm, D_MODEL), lambda m, p: (m, 0)),
        out_shape=SDS((s, D_MODEL), F32), compiler_params=_params(),
    )(dproj, wg, token)


def _norm_bwd(x, dh, dout, norm_w, scale):
    s = x.shape[0]
    tr = 512

    def body(x_ref, dh_ref, do_ref, nw_ref, sc_ref, gx_ref, dsh_ref, dsc_ref, dnw_ref):
        i = pl.program_id(0)

        @pl.when(i == 0)
        def _():
            for r in (dsh_ref, dsc_ref, dnw_ref):
                r[...] = jnp.zeros_like(r)

        def acc_rows(ref, v):
            ref[...] += jnp.broadcast_to(jnp.sum(v, axis=0, keepdims=True), ref.shape)

        xv = x_ref[...]
        dh_v = dh_ref[...]
        r = lax.rsqrt(jnp.mean(xv * xv, axis=-1, keepdims=True) + EPS)
        xn = xv * r
        one_sc = 1.0 + sc_ref[...]
        acc_rows(dsh_ref, dh_v)
        acc_rows(dsc_ref, dh_v * (xn * nw_ref[...]))
        acc_rows(dnw_ref, dh_v * xn * one_sc)
        dxn = dh_v * (nw_ref[...] * one_sc)
        gx_ref[...] = do_ref[...] + r * (dxn - xn * jnp.mean(dxn * xn, axis=-1, keepdims=True))

    blk = pl.BlockSpec((tr, D_MODEL), lambda i: (i, 0))
    vec = pl.BlockSpec((1, D_MODEL), lambda i: (0, 0))
    acc = pl.BlockSpec((8, D_MODEL), lambda i: (0, 0))
    return pl.pallas_call(
        body, name="norm_bwd", grid=(s // tr,), in_specs=[blk, blk, blk, vec, vec],
        out_specs=[blk, acc, acc, acc],
        out_shape=[SDS((s, D_MODEL), F32)] + [SDS((8, D_MODEL), F32)] * 3, compiler_params=_params(),
    )(x, dh, dout, norm_w, scale)


def _adamw(gsrc, w, m, v, name, stacked, c_arr=None):
    rows, cols = w.shape
    tr = rows if rows <= 128 else 128
    unshift = c_arr is not None
    bc1 = 1.0 - ADAM_B1 ** ADAM_STEP
    bc2 = 1.0 - ADAM_B2 ** ADAM_STEP
    n_src = len(gsrc) if stacked else 1

    def body(*refs):
        if unshift:
            c_ref, refs = refs[0], refs[1:]
        g_refs, (w_ref, m_ref, v_ref, go_ref, d_ref, mo_ref, vo_ref) = refs[:n_src], refs[n_src:]
        if stacked:
            g = None
            for g_ref, (_, slots) in zip(g_refs, gsrc):
                for j in range(slots):
                    t = g_ref[j].astype(F32)
                    g = t if g is None else g + t
        else:
            g = g_refs[0][...]
        if unshift:
            g = jnp.where(c_ref[0] == 1, pltpu.roll(g, SHARD_W, axis=1), g)[:, :SHARD_W]
        m_new = ADAM_B1 * m_ref[...] + (1.0 - ADAM_B1) * g
        v_new = ADAM_B2 * v_ref[...] + (1.0 - ADAM_B2) * (g * g)
        m_hat = m_new / bc1
        v_hat = v_new / bc2
        go_ref[...] = g
        d_ref[...] = -ADAM_LR * (m_hat / (jnp.sqrt(v_hat) + ADAM_EPS) + ADAM_WD * w_ref[...])
        mo_ref[...] = m_new
        vo_ref[...] = v_new

    blk = pl.BlockSpec((tr, cols), lambda i: (i, 0))
    if stacked:
        gspecs = [pl.BlockSpec((slots, tr, arr.shape[2]), lambda i: (0, i, 0)) for arr, slots in gsrc]
        gargs = [arr for arr, _ in gsrc]
    else:
        gspecs, gargs = [blk], [gsrc]
    in_specs = gspecs + [blk, blk, blk]
    args = gargs + [w, m, v]
    if unshift:
        in_specs = [pl.BlockSpec(memory_space=pltpu.SMEM)] + in_specs
        args = [c_arr] + args
    return pl.pallas_call(
        body, name=name, grid=(rows // tr,), in_specs=in_specs, out_specs=[blk] * 4,
        out_shape=[SDS((rows, cols), F32)] * 4, compiler_params=_params(),
    )(*args)


def _pack_small(parts):
    cols = []
    for name, length in _SMALL:
        p = parts[name].reshape(1, -1)
        cols.append(jnp.pad(p, ((0, 0), (0, length - p.shape[1]))))
    return jnp.concatenate(cols, axis=1)


def _unpack_small(vec, name, n):
    off, _ = SMALL_OFF[name]
    return vec[:, off:off + n]


def kernel(x, c, w_ada, b_ada, norm_w, w_in, b_gate, q_norm_w, k_norm_w, w_attn_proj, conv_w, conv_b, conv_ln_w, conv_ln_b, w_conv_proj, w_out, loss_target, m_w_ada, m_b_ada, m_norm_w, m_w_in, m_b_gate, m_q_norm_w, m_k_norm_w, m_w_attn_proj, m_conv_w, m_conv_b, m_conv_ln_w, m_conv_ln_b, m_w_conv_proj, m_w_out, v_w_ada, v_b_ada, v_norm_w, v_w_in, v_b_gate, v_q_norm_w, v_k_norm_w, v_w_attn_proj, v_conv_w, v_conv_b, v_conv_ln_w, v_conv_ln_b, v_w_conv_proj, v_w_out):
    xi, yi, ci = lax.axis_index("x"), lax.axis_index("y"), lax.axis_index("c")
    me = 4 * xi + 2 * yi + ci
    c_arr = jnp.reshape(ci, (1,)).astype(jnp.int32)
    x2, tgt2 = x[0], loss_target[0]
    s = x2.shape[0]

    cw_flat = jnp.pad(conv_w[0].reshape(1, -1), ((0, 0), (0, CONVW_FLAT - CONV_K * HEAD_DIM)))
    pre = jnp.concatenate([c, cw_flat], axis=1).reshape(8, -1)
    (pre_all,) = _all_gather([pre], "gather_c_convw", vmem=True)
    pre_all = pre_all.reshape(N_DEV, -1)
    c_all = pre_all[:, :D_MODEL]
    convw_full = pre_all[:, D_MODEL:D_MODEL + CONV_K * HEAD_DIM].reshape(N_DEV, CONV_K, HEAD_DIM)
    convw_full = jnp.transpose(convw_full, (1, 0, 2)).reshape(CONV_K, CONV_W)
    convw_pad = jnp.pad(convw_full, ((0, CONV_HALO - CONV_K), (0, 0)))

    ada_part = _ada_fwd(c_all, w_ada[0])
    (ada_all,) = _all_gather([ada_part], "gather_ada", vmem=True)
    ada = lax.dynamic_index_in_dim(ada_all, me, axis=1, keepdims=False).reshape(1, 3 * D_MODEL) + b_ada
    shift, scale, gate = ada[:, :D_MODEL], ada[:, D_MODEL:2 * D_MODEL], ada[:, 2 * D_MODEL:]

    wg, wa_g, wc_g, wo_g = _all_gather_chips(
        [_to_window(w_in[0], c_arr), _cast_bf16(w_attn_proj[0], "cast_wa"), _cast_bf16(w_conv_proj[0], "cast_wc"),
         _cast_bf16(w_out[0], "cast_wo")], "gather_weights")
    wa = _cols_from_slots(wa_g, "cols_wa")
    wc = _cols_from_slots(wc_g, "cols_wc")
    wo = wo_g.reshape(D_MODEL, D_MODEL)

    h, ht = _norm_fwd(x2, norm_w, scale, shift)
    proj = _mm_in(h, wg)
    qw2 = jnp.tile(q_norm_w, (1, 2))
    kw2 = jnp.tile(k_norm_w, (1, 2))
    o3, l3 = [], []
    for g in range(N_GROUPS):
        o_g, l_g = _attn_fwd(proj, qw2, kw2, g)
        o3.append(o_g)
        l3.append(l_g)
    head_id = jnp.arange(ATTN_W) // HEAD_DIM
    bd = (head_id[:, None] == head_id[None, :]).astype(BF16)
    (dout, da, delta, lse, dcv, mt, yat, yct, dmo, dya, dyc, dproj,
     dgate, dbg, dlnw, dlnb, dcb, loss_p) = _tail(
        x2, tgt2, proj, o3, l3, wa, wc, wo, gate, b_gate[:, :D_MODEL], b_gate[:, D_MODEL:], convw_pad,
        conv_b, conv_ln_w, conv_ln_b, bd)

    dproj, dconvw = _conv_bwd(dcv, proj, convw_pad, dproj)
    dqw = jnp.zeros((1, HEAD_DIM), F32)
    dkw = jnp.zeros((1, HEAD_DIM), F32)
    for g in range(N_GROUPS):
        dproj, dqw_g, dkw_g = _attn_bwd(proj, da, delta, lse, qw2, kw2, dproj, g)
        dqw = dqw + dqw_g[0:1, :HEAD_DIM] + dqw_g[0:1, HEAD_DIM:]
        dkw = dkw + dkw_g[0:1, :HEAD_DIM] + dkw_g[0:1, HEAD_DIM:]
    dw_in_p = _mm_dw(ht, dproj)
    dwo_p = _mm_acc(mt, dmo, "mm_dwo", col_slots=False).reshape(N_DEV, D_MODEL // N_DEV, D_MODEL)
    dwa_p = _mm_acc(yat, dya, "mm_dwa", col_slots=True)
    dwc_p = _mm_acc(yct, dyc, "mm_dwc", col_slots=True)

    partials = [dw_in_p, dwa_p, dwc_p, dwo_p]
    me_arr = jnp.reshape(me, (1,)).astype(jnp.int32)
    from_sib = _exchange_sibling(partials, "exchange_sibling")
    presums = [_presum(p, f, me_arr, f"presum{i}") for i, (p, f) in enumerate(zip(partials, from_sib))]
    s_sems, r_sems, pre_thru, land_thru, token = _exchange_chips_start(presums, "exchange_chips_start")
    dh = _mm_dh(dproj, wg, token)
    gx, dsh, dsc, dnw = _norm_bwd(x2, dh, dout, norm_w, scale)
    d_ada = jnp.concatenate([dsh[0:1], dsc[0:1], dgate[0:1]], axis=1)
    small_p = _pack_small({"b_ada": d_ada, "norm_w": dnw[0:1], "b_gate": dbg[0:1], "q_norm_w": dqw, "k_norm_w": dkw,
                           "conv_b": dcb[0:1], "conv_ln_w": dlnw[0:1], "conv_ln_b": dlnb[0:1], "loss": loss_p[0:1]})
    small_all, dconvw_all = _all_gather([jnp.broadcast_to(small_p, (8, SMALL_N)), dconvw], "gather_small", vmem=True)
    small_all = small_all[:, 0:1, :]
    dcw_mine = lax.dynamic_slice_in_dim(dconvw_all[:, :CONV_K, :], me * HEAD_DIM, HEAD_DIM, axis=2)
    dcw_mine = jnp.pad(dcw_mine.reshape(N_DEV, 1, -1), ((0, 0), (0, 0), (0, CONVW_FLAT - CONV_K * HEAD_DIM)))
    pack_g = jnp.concatenate([small_all, dcw_mine], axis=2)

    def pack_params(tree):
        small = _pack_small({"b_ada": tree["b_ada"], "norm_w": tree["norm_w"], "b_gate": tree["b_gate"],
                             "q_norm_w": tree["q_norm_w"], "k_norm_w": tree["k_norm_w"], "conv_b": tree["conv_b"],
                             "conv_ln_w": tree["conv_ln_w"], "conv_ln_b": tree["conv_ln_b"],
                             "loss": jnp.ones((1, 1), F32)})
        cw = jnp.pad(tree["conv_w"][0].reshape(1, -1), ((0, 0), (0, CONVW_FLAT - CONV_K * HEAD_DIM)))
        return jnp.concatenate([small, cw], axis=1)

    names = ("b_ada", "norm_w", "b_gate", "q_norm_w", "k_norm_w", "conv_b", "conv_ln_w", "conv_ln_b", "conv_w")
    w_tree = dict(zip(names, (b_ada, norm_w, b_gate, q_norm_w, k_norm_w, conv_b, conv_ln_w, conv_ln_b, conv_w)))
    m_tree = dict(zip(names, (m_b_ada, m_norm_w, m_b_gate, m_q_norm_w, m_k_norm_w, m_conv_b, m_conv_ln_w, m_conv_ln_b, m_conv_w)))
    v_tree = dict(zip(names, (v_b_ada, v_norm_w, v_b_gate, v_q_norm_w, v_k_norm_w, v_conv_b, v_conv_ln_w, v_conv_ln_b, v_conv_w)))
    pk = _adamw([(pack_g, N_DEV)], pack_params(w_tree), pack_params(m_tree), pack_params(v_tree), "adamw_small",
                stacked=True)

    d_ada_all = small_all[:, 0, :3 * D_MODEL]
    d_ada_cols = lax.dynamic_slice_in_dim(d_ada_all, me * (3 * D_MODEL // N_DEV), 3 * D_MODEL // N_DEV, axis=1)
    g_wada = _ada_bwd(c_all, d_ada_cols)
    r_ada = _adamw(g_wada, w_ada[0], m_w_ada[0], v_w_ada[0], "adamw_w_ada", stacked=False)
    pres, lands = _exchange_chips_wait(s_sems, r_sems, pre_thru, land_thru, r_ada[1], "exchange_chips_wait")
    terms = [[(p, 1), (l, len(CHIP_K))] for p, l in zip(pres, lands)]
    r_win = _adamw(terms[0], w_in[0], m_w_in[0], v_w_in[0], "adamw_w_in", stacked=True, c_arr=c_arr)
    r_wap = _adamw(terms[1], w_attn_proj[0], m_w_attn_proj[0], v_w_attn_proj[0], "adamw_w_attn_proj", stacked=True)
    r_wcp = _adamw(terms[2], w_conv_proj[0], m_w_conv_proj[0], v_w_conv_proj[0], "adamw_w_conv_proj", stacked=True)
    r_wout = _adamw(terms[3], w_out[0], m_w_out[0], v_w_out[0], "adamw_w_out", stacked=True)

    def small_out(k, name, n):
        return _unpack_small(pk[k], name, n)

    def convw_out(k):
        return pk[k][:, SMALL_N:SMALL_N + CONV_K * HEAD_DIM].reshape(1, CONV_K, HEAD_DIM)

    loss = pk[0][0, SMALL_OFF["loss"][0]]
    outs = [loss, gx[None]]
    for k in range(4):
        outs += [r_ada[k][None], small_out(k, "b_ada", 3 * D_MODEL), small_out(k, "norm_w", D_MODEL), r_win[k][None],
                 small_out(k, "b_gate", 2 * D_MODEL), small_out(k, "q_norm_w", HEAD_DIM), small_out(k, "k_norm_w", HEAD_DIM),
                 r_wap[k][None], convw_out(k), small_out(k, "conv_b", CONV_W), small_out(k, "conv_ln_w", CONV_W),
                 small_out(k, "conv_ln_b", CONV_W), r_wcp[k][None], r_wout[k][None]]
    return tuple(outs)
```

```python
import functools

import jax
import jax.numpy as jnp
from jax import lax
from jax.experimental import pallas as pl
from jax.experimental.pallas import tpu as pltpu

F32 = jnp.float32
BF16 = jnp.bfloat16
SDS = jax.ShapeDtypeStruct
MESH = pl.DeviceIdType.MESH

N_DEV = 8
D_MODEL = 1024
HEAD_DIM = 64
N_GROUPS = 3
DILATIONS = (1, 4, 16)
BAND = 128
BWD_UNROLL = 8
ATTN_W = 512
CONV_W = 512
CONV_K = 31
CONV_HALO = 32
IN_W = 8704
SHARD_W = IN_W // N_DEV
PAIR_W = 2 * SHARD_W
Q0, K0, V0, ZA0, U0, ZC0, G0 = 0, 1536, 3072, 4608, 5120, 6144, 6656
EPS = 1e-6
LANE = 128
VMEM_LIMIT = 56 * 1024 * 1024

ADAM_LR, ADAM_B1, ADAM_B2, ADAM_EPS, ADAM_WD, ADAM_STEP = 0.001, 0.9, 0.999, 1e-08, 0.01, 10

_SMALL = (("b_ada", 3072), ("norm_w", 1024), ("b_gate", 2048), ("q_norm_w", 128), ("k_norm_w", 128),
          ("conv_b", 512), ("conv_ln_w", 512), ("conv_ln_b", 512), ("loss", 128))
SMALL_OFF = {}
_o = 0
for _n, _l in _SMALL:
    SMALL_OFF[_n] = (_o, _l)
    _o += _l
SMALL_N = _o
CONVW_FLAT = 2048
PACK_N = SMALL_N + CONVW_FLAT


def _params(**kw):
    return pltpu.CompilerParams(vmem_limit_bytes=VMEM_LIMIT, **kw)


def _sigmoid(z):
    return 0.5 * jnp.tanh(0.5 * z) + 0.5


def _dot(a, b):
    return jnp.dot(a, b, preferred_element_type=F32)


def _dot_nt(a, b):
    return lax.dot_general(a, b, (((1,), (1,)), ((), ())), preferred_element_type=F32)


def _dot_tn(a, b):
    return lax.dot_general(a, b, (((0,), (0,)), ((), ())), preferred_element_type=F32)


def _peer(x, y, c, k):
    px = 1 - x if (k >> 2) & 1 else x
    py = 1 - y if (k >> 1) & 1 else y
    pc = 1 - c if k & 1 else c
    return (px, py, pc), 4 * px + 2 * py + pc


def _all_gather(arrays, name, vmem):
    n = len(arrays)
    space = pltpu.VMEM if vmem else pl.ANY

    def body(*refs):
        ins, outs = refs[:n], refs[n:2 * n]
        send_sems, recv_sems, local_sems = refs[2 * n:]
        x, y, c = lax.axis_index("x"), lax.axis_index("y"), lax.axis_index("c")
        me = 4 * x + 2 * y + c
        locals_ = [pltpu.make_async_copy(ins[a], outs[a].at[me], local_sems.at[a]) for a in range(n)]
        for cp in locals_:
            cp.start()
        sends = []
        for k in range(1, N_DEV):
            peer, _ = _peer(x, y, c, k)
            for a in range(n):
                cp = pltpu.make_async_remote_copy(
                    src_ref=ins[a], dst_ref=outs[a].at[me], send_sem=send_sems.at[a, k - 1],
                    recv_sem=recv_sems.at[a, k - 1], device_id=peer, device_id_type=MESH)
                cp.start()
                sends.append(cp)
        for k in range(1, N_DEV):
            peer, pidx = _peer(x, y, c, k)
            for a in range(n):
                pltpu.make_async_remote_copy(
                    src_ref=ins[a], dst_ref=outs[a].at[pidx], send_sem=send_sems.at[a, k - 1],
                    recv_sem=recv_sems.at[a, k - 1], device_id=peer, device_id_type=MESH).wait_recv()
        for cp in sends:
            cp.wait_send()
        for cp in locals_:
            cp.wait()

    return pl.pallas_call(
        body, name=name,
        out_shape=[SDS((N_DEV,) + a.shape, a.dtype) for a in arrays],
        in_specs=[pl.BlockSpec(memory_space=space)] * n,
        out_specs=[pl.BlockSpec(memory_space=space)] * n,
        scratch_shapes=[pltpu.SemaphoreType.DMA((n, N_DEV - 1)), pltpu.SemaphoreType.DMA((n, N_DEV - 1)),
                        pltpu.SemaphoreType.DMA((n,))],
        compiler_params=_params(),
    )(*arrays)


CHIP_K = (2, 4, 6)


def _all_gather_chips(arrays, name):
    n = len(arrays)

    def body(*refs):
        ins, outs = refs[:n], refs[n:2 * n]
        send_sems, recv_sems, local_sems = refs[2 * n:]
        x, y, c = lax.axis_index("x"), lax.axis_index("y"), lax.axis_index("c")
        me = 4 * x + 2 * y + c
        sib, sib_idx = _peer(x, y, c, 1)

        def copy(a, slot, block, to, src=None):
            return pltpu.make_async_remote_copy(
                src_ref=outs[a].at[block] if src is None else src, dst_ref=outs[a].at[block],
                send_sem=send_sems.at[a, slot], recv_sem=recv_sems.at[a, slot], device_id=to, device_id_type=MESH)

        locals_ = [pltpu.make_async_copy(ins[a], outs[a].at[me], local_sems.at[a]) for a in range(n)]
        for cp in locals_:
            cp.start()
        sends = [copy(a, 0, me, sib, src=ins[a]) for a in range(n)]
        for j, k in enumerate(CHIP_K):
            peer, _ = _peer(x, y, c, k)
            sends += [copy(a, 1 + j, me, peer, src=ins[a]) for a in range(n)]
        for cp in sends:
            cp.start()
        for j, k in enumerate(CHIP_K):
            peer, pidx = _peer(x, y, c, k)
            for a in range(n):
                copy(a, 1 + j, pidx, peer).wait_recv()
                fwd = copy(a, 4 + j, pidx, sib)
                fwd.start()
                sends.append(fwd)
        for a in range(n):
            copy(a, 0, sib_idx, sib).wait_recv()
        for j, k in enumerate(CHIP_K):
            _, pidx = _peer(x, y, 1 - c, k)
            for a in range(n):
                copy(a, 4 + j, pidx, sib).wait_recv()
        for cp in sends:
            cp.wait_send()
        for cp in locals_:
            cp.wait()

    return pl.pallas_call(
        body, name=name,
        out_shape=[SDS((N_DEV,) + a.shape, a.dtype) for a in arrays],
        in_specs=[pl.BlockSpec(memory_space=pl.ANY)] * n,
        out_specs=[pl.BlockSpec(memory_space=pl.ANY)] * n,
        scratch_shapes=[pltpu.SemaphoreType.DMA((n, N_DEV - 1)), pltpu.SemaphoreType.DMA((n, N_DEV - 1)),
                        pltpu.SemaphoreType.DMA((n,))],
        compiler_params=_params(),
    )(*arrays)


def _exchange_sibling(arrays, name):
    n = len(arrays)
    ks = (0,) + CHIP_K

    def body(*refs):
        ins, outs = refs[:n], refs[n:2 * n]
        send_sems, recv_sems = refs[2 * n:]
        x, y, c = lax.axis_index("x"), lax.axis_index("y"), lax.axis_index("c")
        sib, sib_idx = _peer(x, y, c, 1)
        sends = []
        for i, k in enumerate(ks):
            _, tgt = _peer(x, y, 1 - c, k) if k else (None, sib_idx)
            for a in range(n):
                cp = pltpu.make_async_remote_copy(
                    src_ref=ins[a].at[tgt], dst_ref=outs[a].at[i], send_sem=send_sems.at[a, i],
                    recv_sem=recv_sems.at[a, i], device_id=sib, device_id_type=MESH)
                cp.start()
                sends.append(cp)
        for cp in sends:
            cp.wait_recv()
        for cp in sends:
            cp.wait_send()

    return pl.pallas_call(
        body, name=name,
        out_shape=[SDS((len(ks),) + a.shape[1:], a.dtype) for a in arrays],
        in_specs=[pl.BlockSpec(memory_space=pl.ANY)] * n,
        out_specs=[pl.BlockSpec(memory_space=pl.ANY)] * n,
        scratch_shapes=[pltpu.SemaphoreType.DMA((n, len(ks))), pltpu.SemaphoreType.DMA((n, len(ks)))],
        compiler_params=_params(),
    )(*arrays)


def _presum(mine, from_sib, me_arr, name):
    _, rows, cols = mine.shape
    tr = _row_tile(rows)
    ns = 1 + len(CHIP_K)

    def body(me_ref, a_ref, b_ref, o_ref):
        del me_ref
        o_ref[...] = (a_ref[...].astype(F32) + b_ref[...].astype(F32)).astype(o_ref.dtype)

    grid_spec = pltpu.PrefetchScalarGridSpec(
        num_scalar_prefetch=1, grid=(ns, rows // tr),
        in_specs=[pl.BlockSpec((1, tr, cols), lambda j, i, me: (jnp.bitwise_xor(me[0], 2 * j), i, 0)),
                  pl.BlockSpec((1, tr, cols), lambda j, i, me: (j, i, 0))],
        out_specs=pl.BlockSpec((1, tr, cols), lambda j, i, me: (j, i, 0)))
    return pl.pallas_call(body, name=name, grid_spec=grid_spec, out_shape=SDS((ns, rows, cols), mine.dtype),
                          compiler_params=_params())(me_arr, mine, from_sib)


HBM_SPEC = pl.BlockSpec(memory_space=pltpu.HBM)
SEM_SPEC = pl.BlockSpec(memory_space=pltpu.SEMAPHORE)
SIDE_EFFECT = pltpu.SideEffectType.DATAFLOW_SIDE_EFFECTING


def _chips_copies(pre_refs, land_refs, send_sems, recv_sems):
    x, y, c = lax.axis_index("x"), lax.axis_index("y"), lax.axis_index("c")
    copies = []
    for j, k in enumerate(CHIP_K):
        peer, _ = _peer(x, y, c, k)
        for a in range(len(pre_refs)):
            copies.append(pltpu.make_async_remote_copy(
                src_ref=pre_refs[a].at[1 + j], dst_ref=land_refs[a].at[j], send_sem=send_sems.at[a * len(CHIP_K) + j],
                recv_sem=recv_sems.at[a * len(CHIP_K) + j], device_id=peer, device_id_type=MESH))
    return copies


def _exchange_chips_start(presums, name):
    n = len(presums)

    def body(*refs):
        pre, land = refs[:n], refs[n:2 * n]
        send_sems, recv_sems = refs[2 * n], refs[2 * n + 1]
        token = refs[-1]
        for cp in _chips_copies(pre, land, send_sems, recv_sems):
            cp.start()
        token[...] = jnp.zeros_like(token)

    nk = len(CHIP_K)
    hbm = [pltpu.HBM(p.shape, p.dtype) for p in presums]
    hbm_land = [pltpu.HBM((nk,) + p.shape[1:], p.dtype) for p in presums]
    res = pl.pallas_call(
        body, name=name,
        out_shape=(pltpu.SemaphoreType.DMA((n * nk,)), pltpu.SemaphoreType.DMA((n * nk,)), *hbm, *hbm_land, SDS((8, LANE), F32)),
        in_specs=[HBM_SPEC] * (2 * n),
        out_specs=(SEM_SPEC, SEM_SPEC, *([HBM_SPEC] * (2 * n)), pl.BlockSpec(memory_space=pltpu.VMEM)),
        input_output_aliases={i: 2 + i for i in range(2 * n)},
        compiler_params=pltpu.CompilerParams(has_side_effects=SIDE_EFFECT),
    )(*[pltpu.with_memory_space_constraint(p, pltpu.HBM) for p in presums],
      *[pltpu.with_memory_space_constraint(lax.empty((nk,) + p.shape[1:], p.dtype), pltpu.HBM) for p in presums])
    return res[0], res[1], res[2:2 + n], res[2 + n:2 + 2 * n], res[-1]


def _exchange_chips_wait(send_sems, recv_sems, pre_thru, land_thru, after, name):
    n = len(pre_thru)

    def body(*refs):
        pre, land = refs[:n], refs[n:2 * n]
        s_sems, r_sems = refs[2 * n], refs[2 * n + 1]
        for cp in _chips_copies(pre, land, s_sems, r_sems):
            cp.wait_send()
            cp.wait_recv()

    hbm = [pltpu.HBM(p.shape, p.dtype) for p in (*pre_thru, *land_thru)]
    res = pl.pallas_call(
        body, name=name, out_shape=tuple(hbm),
        in_specs=[HBM_SPEC] * (2 * n) + [SEM_SPEC, SEM_SPEC, pl.BlockSpec(memory_space=pl.ANY)],
        out_specs=tuple([HBM_SPEC] * (2 * n)),
        input_output_aliases={i: i for i in range(2 * n)},
        compiler_params=pltpu.CompilerParams(has_side_effects=SIDE_EFFECT),
    )(*pre_thru, *land_thru, send_sems, recv_sems, after)
    return res[:n], res[n:]


def _exchange_chips(presums, name):
    n = len(presums)
    nk = len(CHIP_K)

    def body(*refs):
        pre, land = refs[:n], refs[n:2 * n]
        send_sems, recv_sems = refs[2 * n:]
        copies = _chips_copies(pre, land, send_sems, recv_sems)
        for cp in copies:
            cp.start()
        for cp in copies:
            cp.wait_recv()
        for cp in copies:
            cp.wait_send()

    return pl.pallas_call(
        body, name=name,
        out_shape=[SDS((nk,) + p.shape[1:], p.dtype) for p in presums],
        in_specs=[pl.BlockSpec(memory_space=pl.ANY)] * n,
        out_specs=[pl.BlockSpec(memory_space=pl.ANY)] * n,
        scratch_shapes=[pltpu.SemaphoreType.DMA((n * nk,)), pltpu.SemaphoreType.DMA((n * nk,))],
        compiler_params=_params(),
    )(*presums)


def _cast_bf16(w, name):
    def body(w_ref, o_ref):
        o_ref[...] = w_ref[...].astype(BF16)

    return pl.pallas_call(body, name=name, out_shape=SDS(w.shape, BF16), compiler_params=_params())(w)


def _cols_from_slots(wg, name):
    _, rows, cols = wg.shape

    def body(w_ref, o_ref):
        for j in range(N_DEV):
            o_ref[:, j * cols:(j + 1) * cols] = w_ref[j]

    return pl.pallas_call(body, name=name, out_shape=SDS((rows, N_DEV * cols), wg.dtype), compiler_params=_params())(wg)


def _ada_fwd(c_all, w_ada):
    def body(c_ref, w_ref, o_ref):
        cv = c_ref[...]
        sc = (cv * _sigmoid(cv)).astype(BF16)
        o_ref[...] = _dot(sc, w_ref[...].astype(BF16))

    return pl.pallas_call(body, name="ada_fwd", out_shape=SDS((N_DEV, w_ada.shape[1]), F32),
                          compiler_params=_params())(c_all, w_ada)


def _ada_bwd(c_all, d_ada_cols):
    def body(c_ref, d_ref, o_ref):
        cv = c_ref[...]
        sc = (cv * _sigmoid(cv)).astype(BF16)
        o_ref[...] = _dot_tn(sc, d_ref[...].astype(BF16))

    return pl.pallas_call(body, name="ada_bwd", out_shape=SDS((D_MODEL, d_ada_cols.shape[1]), F32),
                          compiler_params=_params())(c_all, d_ada_cols)


def _norm_fwd(x, norm_w, scale, shift):
    s = x.shape[0]
    tr = 512

    def body(x_ref, nw_ref, sc_ref, sh_ref, h_ref, ht_ref):
        xv = x_ref[...]
        r = lax.rsqrt(jnp.mean(xv * xv, axis=-1, keepdims=True) + EPS)
        h = (xv * r * nw_ref[...]) * (1.0 + sc_ref[...]) + sh_ref[...]
        h_ref[...] = h.astype(BF16)
        ht_ref[...] = h.T.astype(BF16)

    vec = pl.BlockSpec((1, D_MODEL), lambda i: (0, 0))
    return pl.pallas_call(
        body, name="norm_fwd", grid=(s // tr,),
        in_specs=[pl.BlockSpec((tr, D_MODEL), lambda i: (i, 0)), vec, vec, vec],
        out_specs=[pl.BlockSpec((tr, D_MODEL), lambda i: (i, 0)), pl.BlockSpec((D_MODEL, tr), lambda i: (0, i))],
        out_shape=[SDS((s, D_MODEL), BF16), SDS((D_MODEL, s), BF16)], compiler_params=_params(),
    )(x, norm_w, scale, shift)


def _mm_in(h, wt):
    s = h.shape[0]
    tm = 512

    def body(h_ref, w_ref, o_ref):
        o_ref[...] = _dot_nt(h_ref[...], w_ref[...])

    return pl.pallas_call(
        body, name="mm_in", grid=(IN_W // PAIR_W, s // tm),
        in_specs=[pl.BlockSpec((tm, D_MODEL), lambda p, m: (m, 0)),
                  pl.BlockSpec((PAIR_W, D_MODEL), lambda p, m: (p, 0))],
        out_specs=pl.BlockSpec((tm, PAIR_W), lambda p, m: (m, p)),
        out_shape=SDS((s, IN_W), F32), compiler_params=_params(),
    )(h, wt)


def _head_ones():
    a = lax.broadcasted_iota(jnp.int32, (LANE, LANE), 0) // HEAD_DIM
    b = lax.broadcasted_iota(jnp.int32, (LANE, LANE), 1) // HEAD_DIM
    return (a == b).astype(BF16)


def _head_sums(t, ones):
    return _dot(t.astype(BF16), ones)


def _band_bias(bias):
    qi = lax.broadcasted_iota(jnp.int32, (2 * BAND, 2 * BAND), 0) % BAND
    kj = lax.broadcasted_iota(jnp.int32, (2 * BAND, 2 * BAND), 1)
    dist = qi + BAND - kj
    valid = (dist >= 0) & (dist <= BAND)
    bias[1] = jnp.where(valid, 0.0, -1e30)
    bias[0] = jnp.where(valid & (kj >= BAND), 0.0, -1e30)


def _token_rows(j, d, chunk, per_r):
    return pl.ds(j // per_r + (j % per_r) * (chunk * d), chunk, stride=d)


def _deinterleave(src_ref, dst_ref, w_ref, ones, d, sub_len, chunk, scale, dst_off):
    per_r = sub_len // chunk

    def step(j, _):
        t = src_ref[_token_rows(j, d, chunk, per_r), :]
        if w_ref is not None:
            ms = _head_sums(t * t, ones) * (1.0 / HEAD_DIM)
            t = t * lax.rsqrt(ms + EPS) * (w_ref[...] * scale)
        dst_ref[pl.ds(pl.multiple_of(dst_off + j * chunk, BAND), chunk), :] = t.astype(dst_ref.dtype)
        return 0
    lax.fori_loop(0, d * per_r, step, 0, unroll=4)


def _attn_fwd(proj, qw2, kw2, g):
    s = proj.shape[0]
    d = DILATIONS[g]
    sub_len = s // d
    nb = sub_len // BAND
    chunk = min(sub_len, 256)

    def body(q_ref, k_ref, v_ref, qw_ref, kw_ref, o_ref, l_ref, qd, kd, vd, od, ld, bias):
        lo = lax.broadcasted_iota(jnp.int32, (1, LANE), 1) < HEAD_DIM
        ones = _head_ones()

        @pl.when(pl.program_id(0) == 0)
        def _():
            _band_bias(bias)

        kd[0:BAND, :] = jnp.zeros((BAND, LANE), BF16)
        vd[0:BAND, :] = jnp.zeros((BAND, LANE), BF16)
        _deinterleave(q_ref, qd, qw_ref, ones, d, sub_len, chunk, HEAD_DIM ** -0.5, 0)
        _deinterleave(k_ref, kd, kw_ref, ones, d, sub_len, chunk, 1.0, BAND)
        _deinterleave(v_ref, vd, None, ones, d, sub_len, chunk, 1.0, BAND)

        def block(t, _):
            base = pl.multiple_of(t * BAND, BAND)
            q = qd[pl.ds(base, BAND), :]
            k2 = kd[pl.ds(base, 2 * BAND), :]
            v2 = vd[pl.ds(base, 2 * BAND), :]
            zero = jnp.zeros_like(q)
            qs = jnp.concatenate([jnp.where(lo, q, zero), jnp.where(lo, zero, q)], axis=0)
            sc = _dot_nt(qs, k2) + bias[jnp.minimum(t % nb, 1)]
            m = jnp.max(sc, axis=-1, keepdims=True)
            p = jnp.exp(sc - m)
            den = jnp.sum(p, axis=-1, keepdims=True)
            u = _dot(p.astype(BF16), v2) * (1.0 / den)
            lse = m + jnp.log(den)
            od[pl.ds(base, BAND), :] = jnp.where(lo, u[:BAND], u[BAND:])
            ld[pl.ds(base, BAND), :] = jnp.where(lo, lse[:BAND], lse[BAND:])
            return 0
        lax.fori_loop(0, s // BAND, block, 0, unroll=16)

        per_r = sub_len // chunk

        def back(j, _):
            src = pl.ds(pl.multiple_of(j * chunk, chunk), chunk)
            dst = _token_rows(j, d, chunk, per_r)
            o_ref[dst, :] = od[src, :]
            l_ref[dst, :] = ld[src, :]
            return 0
        lax.fori_loop(0, d * per_r, back, 0, unroll=2)

    col = lambda off: pl.BlockSpec((s, LANE), lambda hp, off=off: (0, off // LANE + 4 * g + hp))
    vec = pl.BlockSpec((1, LANE), lambda hp: (0, 0))
    out = pl.BlockSpec((s, LANE), lambda hp: (0, hp))
    return pl.pallas_call(
        body, name=f"attn_fwd{g}", grid=(ATTN_W // LANE,),
        in_specs=[col(Q0), col(K0), col(V0), vec, vec], out_specs=[out, out],
        out_shape=[SDS((s, ATTN_W), F32), SDS((s, ATTN_W), F32)],
        scratch_shapes=[pltpu.VMEM((s, LANE), BF16), pltpu.VMEM((s + BAND, LANE), BF16), pltpu.VMEM((s + BAND, LANE), BF16),
                        pltpu.VMEM((s, LANE), F32), pltpu.VMEM((s, LANE), F32),
                        pltpu.VMEM((2, 2 * BAND, 2 * BAND), F32)],
        compiler_params=_params(),
    )(proj, proj, proj, qw2, kw2)


def _attn_bwd(proj, da, delta, lse, qw2, kw2, dproj, g):
    s = proj.shape[0]
    d = DILATIONS[g]
    sub_len = s // d
    nb = sub_len // BAND
    chunk = min(sub_len, 256)

    def body(q_ref, k_ref, v_ref, da_ref, dl_ref, ls_ref, qw_ref, kw_ref, dp_in, dp_out, dqw_ref, dkw_ref,
             qd, kd, vd, dad, dld, lsd, dqd, dkd, dvd, st, stb, bias, wacc, sem):
        del dp_in
        hp = pl.program_id(0)
        lo = lax.broadcasted_iota(jnp.int32, (1, LANE), 1) < HEAD_DIM
        ones = _head_ones()
        per_r = sub_len // chunk

        @pl.when(hp == 0)
        def _():
            _band_bias(bias)

        kd[0:BAND, :] = jnp.zeros((BAND, LANE), BF16)
        vd[0:BAND, :] = jnp.zeros((BAND, LANE), BF16)
        _deinterleave(q_ref, qd, qw_ref, ones, d, sub_len, chunk, HEAD_DIM ** -0.5, 0)
        _deinterleave(k_ref, kd, kw_ref, ones, d, sub_len, chunk, 1.0, BAND)
        _deinterleave(v_ref, vd, None, ones, d, sub_len, chunk, 1.0, BAND)
        _deinterleave(da_ref, dad, None, ones, d, sub_len, chunk, 1.0, 0)
        _deinterleave(dl_ref, dld, None, ones, d, sub_len, chunk, 1.0, 0)
        _deinterleave(ls_ref, lsd, None, ones, d, sub_len, chunk, 1.0, 0)

        def block(t, carry):
            ck, cv = carry
            base = pl.multiple_of(t * BAND, BAND)
            q = qd[pl.ds(base, BAND), :]
            k2 = kd[pl.ds(base, 2 * BAND), :]
            v2 = vd[pl.ds(base, 2 * BAND), :]
            dav = dad[pl.ds(base, BAND), :]
            dlv = dld[pl.ds(base, BAND), :]
            lsv = lsd[pl.ds(base, BAND), :]
            zero = jnp.zeros_like(q)
            qs = jnp.concatenate([jnp.where(lo, q, zero), jnp.where(lo, zero, q)], axis=0)
            das = jnp.concatenate([jnp.where(lo, dav, zero), jnp.where(lo, zero, dav)], axis=0)
            ls_col = jnp.concatenate([lsv[:, 0:1], lsv[:, HEAD_DIM:HEAD_DIM + 1]], axis=0)
            dl_col = jnp.concatenate([dlv[:, 0:1], dlv[:, HEAD_DIM:HEAD_DIM + 1]], axis=0)
            sc = _dot_nt(qs, k2) + bias[jnp.minimum(t % nb, 1)]
            p = jnp.exp(sc - ls_col)
            dp = _dot_nt(das, v2)
            ds = (p * (dp - dl_col)).astype(BF16)
            dv2 = _dot_tn(p.astype(BF16), das)
            dk2 = _dot_tn(ds, qs)
            dvd[pl.ds(base, BAND), :] = cv + dv2[:BAND]
            dkd[pl.ds(base, BAND), :] = ck + dk2[:BAND]
            dq = _dot(ds, k2)
            dqd[pl.ds(base, BAND), :] = jnp.where(lo, dq[:BAND], dq[BAND:])
            return dk2[BAND:], dv2[BAND:]
        def blocks(i, carry):
            for u in range(BWD_UNROLL):
                carry = block(i * BWD_UNROLL + u, carry)
            return carry
        zeros = jnp.zeros((BAND, LANE), F32)
        ck, cv = lax.fori_loop(0, s // (BAND * BWD_UNROLL), blocks, (zeros, zeros))
        dkd[s:s + BAND, :] = ck
        dvd[s:s + BAND, :] = cv

        def store_cols(col0):
            stb[...] = st[...].astype(BF16)
            cp = pltpu.make_async_copy(
                stb, dp_out.at[:, pl.ds(pl.multiple_of(col0 + LANE * (4 * g + hp), LANE), LANE)], sem)
            cp.start()
            cp.wait()

        def norm_back(src_ref, dy_ref, dy_off, w_ref, scale, dw_ref, col0):
            wacc[...] = jnp.zeros_like(wacc)

            def step(j, _):
                tok = _token_rows(j, d, chunk, per_r)
                t = src_ref[tok, :]
                dy = dy_ref[pl.ds(pl.multiple_of(dy_off + j * chunk, BAND), chunk), :]
                rr = lax.rsqrt(_head_sums(t * t, ones) * (1.0 / HEAD_DIM) + EPS)
                nrm = t * rr
                wacc[...] += jnp.sum((dy * nrm).reshape(chunk // 8, 8, LANE), axis=0)
                dn = dy * (w_ref[...] * scale)
                st[tok, :] = rr * (dn - nrm * (_head_sums(dn * nrm, ones) * (1.0 / HEAD_DIM)))
                return 0
            lax.fori_loop(0, d * per_r, step, 0, unroll=4)
            dw_ref[...] += jnp.broadcast_to(jnp.sum(wacc[...], axis=0, keepdims=True) * scale, dw_ref.shape)
            store_cols(col0)

        @pl.when(hp == 0)
        def _():
            dqw_ref[...] = jnp.zeros_like(dqw_ref)
            dkw_ref[...] = jnp.zeros_like(dkw_ref)

        norm_back(q_ref, dqd, 0, qw_ref, HEAD_DIM ** -0.5, dqw_ref, Q0)
        norm_back(k_ref, dkd, BAND, kw_ref, 1.0, dkw_ref, K0)

        def v_back(j, _):
            src = pl.ds(pl.multiple_of(BAND + j * chunk, BAND), chunk)
            st[_token_rows(j, d, chunk, per_r), :] = dvd[src, :]
            return 0
        lax.fori_loop(0, d * per_r, v_back, 0, unroll=2)
        store_cols(V0)

    col = lambda off: pl.BlockSpec((s, LANE), lambda hp, off=off: (0, off // LANE + 4 * g + hp))
    mid = pl.BlockSpec((s, LANE), lambda hp: (0, hp))
    vec = pl.BlockSpec((1, LANE), lambda hp: (0, 0))
    acc = pl.BlockSpec((8, LANE), lambda hp: (0, 0))
    any_ = pl.BlockSpec(memory_space=pl.ANY)
    return pl.pallas_call(
        body, name=f"attn_bwd{g}", grid=(ATTN_W // LANE,),
        in_specs=[col(Q0), col(K0), col(V0), mid, mid, mid, vec, vec, any_],
        out_specs=[any_, acc, acc],
        out_shape=[SDS(dproj.shape, dproj.dtype), SDS((8, LANE), F32), SDS((8, LANE), F32)],
        input_output_aliases={8: 0},
        scratch_shapes=[pltpu.VMEM((s, LANE), BF16), pltpu.VMEM((s + BAND, LANE), BF16), pltpu.VMEM((s + BAND, LANE), BF16),
                        pltpu.VMEM((s, LANE), BF16), pltpu.VMEM((s, LANE), F32), pltpu.VMEM((s, LANE), F32),
                        pltpu.VMEM((s, LANE), F32), pltpu.VMEM((s + BAND, LANE), F32), pltpu.VMEM((s + BAND, LANE), F32),
                        pltpu.VMEM((s, LANE), F32), pltpu.VMEM((s, LANE), BF16),
                        pltpu.VMEM((2, 2 * BAND, 2 * BAND), F32), pltpu.VMEM((8, LANE), F32),
                        pltpu.SemaphoreType.DMA(())],
        compiler_params=_params(),
    )(proj, proj, proj, da, delta, lse, qw2, kw2, dproj)


def _tap_views(ext_ref, sh_ref, offsets, tr, cols):
    for b in range(8):
        group = [j for j, o in enumerate(offsets) if o % 8 == b]
        if not group:
            continue
        first = min(offsets[j] for j in group)
        span = tr + max(offsets[j] for j in group) - first
        sh_ref[0:span, cols] = ext_ref[first:first + span, cols]
        for j in group:
            yield j, sh_ref[offsets[j] - first:offsets[j] - first + tr, cols]


def _silu_grad(z, sg):
    return sg * (1.0 + z * (1.0 - sg))


def _glu(u):
    a_h, b_h = u[:, :CONV_W], u[:, CONV_W:]
    sg = _sigmoid(b_h)
    return a_h, sg, a_h * sg


def _tail(x, tgt, proj, o3, l3, wa, wc, wo, gate, bga, bgc, convw, convb, lnw, lnb, bd):
    s = x.shape[0]
    tr = 256

    def body(x_ref, t_ref, za_ref, u_ref, uh_ref, zc_ref, g0_ref, g1_ref, g2_ref, g3_ref,
             o0_ref, o1_ref, o2_ref, l0_ref, l1_ref, l2_ref, wa_ref, wc_ref, wo_ref,
             gate_ref, bga_ref, bgc_ref, cw_ref, cb_ref, lnw_ref, lnb_ref, bd_ref,
             dout_ref, da_ref, dl_ref, lse_ref, dcv_ref, mt_ref, yat_ref, yct_ref, dmo_ref, dya_ref, dyc_ref, dp_ref,
             dgate_ref, dbg_ref, dlnw_ref, dlnb_ref, dcb_ref, loss_ref,
             ext, sh, st_za, st_zc, st_g, sems):
        i = pl.program_id(0)

        @pl.when(i == 0)
        def _():
            for r in (dgate_ref, dbg_ref, dlnw_ref, dlnb_ref, dcb_ref, loss_ref):
                r[...] = jnp.zeros_like(r)

        def acc_rows(ref, v):
            ref[...] += jnp.broadcast_to(jnp.sum(v, axis=0, keepdims=True), ref.shape)

        la, lb, lc = l0_ref[...], l1_ref[...], l2_ref[...]
        mx = jnp.maximum(jnp.maximum(la, lb), lc)
        ea, eb, ec = jnp.exp(la - mx), jnp.exp(lb - mx), jnp.exp(lc - mx)
        den = ea + eb + ec
        inv = 1.0 / den
        attn = (ea * inv) * o0_ref[...] + (eb * inv) * o1_ref[...] + (ec * inv) * o2_ref[...]
        lse_ref[...] = mx + jnp.log(den)

        za = za_ref[...]
        sga = _sigmoid(za)
        sa = za * sga
        ya_in = attn * sa
        y_attn = _dot(ya_in.astype(BF16), wa_ref[...])

        _, _, glu = _glu(u_ref[...])
        _, _, glu_h = _glu(uh_ref[...])
        ext[0:CONV_HALO, :] = jnp.where(i > 0, glu_h, 0.0)
        ext[CONV_HALO:CONV_HALO + tr, :] = glu
        cv_blocks = []
        for cb in range(CONV_W // LANE):
            cols = slice(cb * LANE, (cb + 1) * LANE)
            cv_c = jnp.broadcast_to(cb_ref[:, cols], (tr, LANE))
            for j, rows in _tap_views(ext, sh, [CONV_HALO - (CONV_K - 1) + j for j in range(CONV_K)], tr, cols):
                cv_c = cv_c + cw_ref[j:j + 1, cols] * rows
            cv_blocks.append(cv_c)
        cv = jnp.concatenate(cv_blocks, axis=1)
        mu = jnp.mean(cv, axis=-1, keepdims=True)
        xc = cv - mu
        rstd = lax.rsqrt(jnp.mean(xc * xc, axis=-1, keepdims=True) + EPS)
        nrm = xc * rstd
        ln = nrm * lnw_ref[...] + lnb_ref[...]
        sgl = _sigmoid(ln)
        cs = ln * sgl
        zc = zc_ref[...]
        sgc = _sigmoid(zc)
        scz = zc * sgc
        yc_in = cs * scz
        y_conv = _dot(yc_in.astype(BF16), wc_ref[...])

        ga = _sigmoid(jnp.concatenate([g0_ref[...], g1_ref[...]], axis=1) + bga_ref[...])
        gc = _sigmoid(jnp.concatenate([g2_ref[...], g3_ref[...]], axis=1) + bgc_ref[...])
        merged = ga * y_attn + gc * y_conv
        mo = _dot(merged.astype(BF16), wo_ref[...])
        gate_v = gate_ref[...]
        err = (x_ref[...] + gate_v * mo) - t_ref[...]
        loss_ref[...] += 0.5 * jnp.sum(jnp.mean(err * err, axis=-1, keepdims=True))
        d_out = err * (1.0 / D_MODEL)
        dout_ref[...] = d_out

        acc_rows(dgate_ref, d_out * mo)
        dmo_b = (d_out * gate_v).astype(BF16)
        dmo_ref[...] = dmo_b
        mt_ref[...] = merged.T.astype(BF16)
        d_merged = _dot_nt(dmo_b, wo_ref[...])
        d_ya = (d_merged * ga).astype(BF16)
        d_yc = (d_merged * gc).astype(BF16)
        dya_ref[...] = d_ya
        dyc_ref[...] = d_yc
        dga = d_merged * y_attn * (ga * (1.0 - ga))
        dgc = d_merged * y_conv * (gc * (1.0 - gc))
        dgs = jnp.concatenate([dga, dgc], axis=1)
        acc_rows(dbg_ref, dgs)
        st_g[...] = dgs.astype(BF16)

        yat_ref[...] = ya_in.T.astype(BF16)
        d_ya_in = _dot_nt(d_ya, wa_ref[...])
        d_attn = d_ya_in * sa
        da_ref[...] = d_attn
        st_za[...] = (d_ya_in * attn * _silu_grad(za, sga)).astype(BF16)
        prod = d_attn * attn
        hi = prod.astype(BF16)
        lo_ = (prod - hi.astype(F32)).astype(BF16)
        dl_ref[...] = _dot(hi, bd_ref[...]) + _dot(lo_, bd_ref[...])

        yct_ref[...] = yc_in.T.astype(BF16)
        d_yc_in = _dot_nt(d_yc, wc_ref[...])
        st_zc[...] = (d_yc_in * cs * _silu_grad(zc, sgc)).astype(BF16)
        d_ln = (d_yc_in * scz) * _silu_grad(ln, sgl)
        acc_rows(dlnw_ref, d_ln * nrm)
        acc_rows(dlnb_ref, d_ln)
        d_nrm = d_ln * lnw_ref[...]
        d_cv = rstd * (d_nrm - jnp.mean(d_nrm, axis=-1, keepdims=True)
                       - nrm * jnp.mean(d_nrm * nrm, axis=-1, keepdims=True))
        acc_rows(dcb_ref, d_cv)
        dcv_ref[...] = d_cv

        rows = pl.ds(pl.multiple_of(i * tr, tr), tr)
        cps = [pltpu.make_async_copy(st_za, dp_ref.at[rows, pl.ds(ZA0, ATTN_W)], sems.at[0]),
               pltpu.make_async_copy(st_zc, dp_ref.at[rows, pl.ds(ZC0, CONV_W)], sems.at[1]),
               pltpu.make_async_copy(st_g, dp_ref.at[rows, pl.ds(G0, 2 * D_MODEL)], sems.at[2])]
        for cp in cps:
            cp.start()
        for cp in cps:
            cp.wait()

    def rows(width, colblk=0):
        return pl.BlockSpec((tr, width), lambda i, colblk=colblk: (i, colblk))

    def const(shape):
        return pl.BlockSpec(shape, lambda i: (0,) * len(shape))

    halo = pl.BlockSpec((CONV_HALO, D_MODEL), lambda i: (jnp.maximum(i * (tr // CONV_HALO) - 1, 0), U0 // D_MODEL))
    in_specs = [rows(D_MODEL), rows(D_MODEL), rows(ATTN_W, ZA0 // ATTN_W), rows(D_MODEL, U0 // D_MODEL), halo,
                rows(CONV_W, ZC0 // CONV_W)]
    in_specs += [rows(512, G0 // 512 + j) for j in range(4)]
    in_specs += [rows(ATTN_W)] * 6
    in_specs += [const(wa.shape), const(wc.shape), const(wo.shape), const((1, D_MODEL)), const((1, D_MODEL)),
                 const((1, D_MODEL)), const(convw.shape), const((1, CONV_W)), const((1, CONV_W)), const((1, CONV_W)),
                 const(bd.shape)]
    tcol = lambda width: pl.BlockSpec((width, tr), lambda i: (0, i))
    out_specs = [rows(D_MODEL), rows(ATTN_W), rows(ATTN_W), rows(ATTN_W), rows(CONV_W),
                 tcol(D_MODEL), tcol(ATTN_W), tcol(CONV_W), rows(D_MODEL), rows(D_MODEL), rows(D_MODEL),
                 pl.BlockSpec(memory_space=pl.ANY),
                 const((8, D_MODEL)), const((8, 2 * D_MODEL)), const((8, CONV_W)), const((8, CONV_W)), const((8, CONV_W)),
                 const((8, LANE))]
    out_shape = [SDS((s, D_MODEL), F32), SDS((s, ATTN_W), F32), SDS((s, ATTN_W), F32), SDS((s, ATTN_W), F32),
                 SDS((s, CONV_W), F32),
                 SDS((D_MODEL, s), BF16), SDS((ATTN_W, s), BF16), SDS((CONV_W, s), BF16),
                 SDS((s, D_MODEL), BF16), SDS((s, D_MODEL), BF16), SDS((s, D_MODEL), BF16),
                 SDS((s, IN_W), BF16),
                 SDS((8, D_MODEL), F32), SDS((8, 2 * D_MODEL), F32), SDS((8, CONV_W), F32), SDS((8, CONV_W), F32),
                 SDS((8, CONV_W), F32), SDS((8, LANE), F32)]
    return pl.pallas_call(
        body, name="tail", grid=(s // tr,), in_specs=in_specs, out_specs=out_specs, out_shape=out_shape,
        scratch_shapes=[pltpu.VMEM((CONV_HALO + tr, CONV_W), F32), pltpu.VMEM((CONV_HALO + tr, CONV_W), F32),
                        pltpu.VMEM((tr, ATTN_W), BF16),
                        pltpu.VMEM((tr, CONV_W), BF16), pltpu.VMEM((tr, 2 * D_MODEL), BF16),
                        pltpu.SemaphoreType.DMA((3,))],
        compiler_params=_params(),
    )(x, tgt, proj, proj, proj, proj, proj, proj, proj, proj, *o3, *l3, wa, wc, wo, gate, bga, bgc,
      convw, convb, lnw, lnb, bd)


def _conv_bwd(dcv, proj, convw, dproj):
    s = dcv.shape[0]
    tr = 128
    nt = s // tr

    def body(dcv_ref, dcvn_ref, u_ref, uh_ref, cw_ref, dp_in, dp_out, dw_ref, extg, extd, sh):
        del dp_in
        i = pl.program_id(0)

        @pl.when(i == 0)
        def _():
            dw_ref[...] = jnp.zeros_like(dw_ref)

        _, _, glu = _glu(u_ref[...])
        _, _, glu_h = _glu(uh_ref[...])
        extg[0:CONV_HALO, :] = jnp.where(i > 0, glu_h, 0.0)
        extg[CONV_HALO:CONV_HALO + tr, :] = glu
        extd[0:tr, :] = dcv_ref[...]
        extd[tr:tr + CONV_HALO, :] = jnp.where(i < nt - 1, dcvn_ref[...], 0.0)
        for cb in range(CONV_W // LANE):
            cols = slice(cb * LANE, (cb + 1) * LANE)
            dglu = jnp.zeros((tr, LANE), F32)
            for j, rows in _tap_views(extd, sh, [CONV_K - 1 - j for j in range(CONV_K)], tr, cols):
                dglu = dglu + cw_ref[j:j + 1, cols] * rows
            dcv_c = dcv_ref[:, cols]
            for j, rows in _tap_views(extg, sh, [CONV_HALO - (CONV_K - 1) + j for j in range(CONV_K)], tr, cols):
                dw_ref[8 * j:8 * j + 8, cols] += jnp.sum((dcv_c * rows).reshape(tr // 8, 8, LANE), axis=0)
            a_h = u_ref[:, cols]
            sgb = _sigmoid(u_ref[:, CONV_W + cb * LANE:CONV_W + (cb + 1) * LANE])
            dp_out[:, cols] = (dglu * sgb).astype(BF16)
            dp_out[:, CONV_W + cb * LANE:CONV_W + (cb + 1) * LANE] = (dglu * a_h * (sgb * (1.0 - sgb))).astype(BF16)

    ucol = U0 // D_MODEL
    return pl.pallas_call(
        body, name="conv_bwd", grid=(nt,),
        in_specs=[pl.BlockSpec((tr, CONV_W), lambda i: (i, 0)),
                  pl.BlockSpec((CONV_HALO, CONV_W), lambda i: (jnp.minimum((i + 1) * (tr // CONV_HALO), s // CONV_HALO - 1), 0)),
                  pl.BlockSpec((tr, D_MODEL), lambda i: (i, ucol)),
                  pl.BlockSpec((CONV_HALO, D_MODEL), lambda i: (jnp.maximum(i * (tr // CONV_HALO) - 1, 0), ucol)),
                  pl.BlockSpec(convw.shape, lambda i: (0, 0)),
                  pl.BlockSpec(memory_space=pl.ANY)],
        out_specs=[pl.BlockSpec((tr, D_MODEL), lambda i: (i, ucol)), pl.BlockSpec((8 * CONV_HALO, CONV_W), lambda i: (0, 0))],
        out_shape=[SDS(dproj.shape, dproj.dtype), SDS((8 * CONV_HALO, CONV_W), F32)],
        input_output_aliases={5: 0},
        scratch_shapes=[pltpu.VMEM((CONV_HALO + tr, CONV_W), F32)] * 3,
        compiler_params=_params(),
    )(dcv, dcv, proj, proj, convw, dproj)


def _mm_acc(at, b, name, col_slots):
    m, s = at.shape
    n = b.shape[1]
    tk = 512
    nk = s // tk

    def body(a_ref, b_ref, o_ref, acc):
        k = pl.program_id(0)

        @pl.when(k == 0)
        def _():
            acc[...] = jnp.zeros_like(acc)

        acc[...] += _dot(a_ref[...], b_ref[...])

        @pl.when(k == nk - 1)
        def _():
            if col_slots:
                w = n // N_DEV
                for j in range(N_DEV):
                    o_ref[j] = acc[:, j * w:(j + 1) * w].astype(BF16)
            else:
                o_ref[...] = acc[...].astype(BF16)

    if col_slots:
        out_shape = SDS((N_DEV, m, n // N_DEV), BF16)
        out_spec = pl.BlockSpec((N_DEV, m, n // N_DEV), lambda k: (0, 0, 0))
    else:
        out_shape = SDS((m, n), BF16)
        out_spec = pl.BlockSpec((m, n), lambda k: (0, 0))
    return pl.pallas_call(
        body, name=name, grid=(nk,),
        in_specs=[pl.BlockSpec((m, tk), lambda k: (0, k)), pl.BlockSpec((tk, n), lambda k: (k, 0))],
        out_specs=out_spec, out_shape=out_shape, scratch_shapes=[pltpu.VMEM((m, n), F32)],
        compiler_params=_params(),
    )(at, b)


def _mm_dw(ht, dproj):
    s = ht.shape[1]
    tk = 512
    nk = s // tk

    def body(a_ref, b_ref, o_ref, acc):
        k = pl.program_id(1)

        @pl.when(k == 0)
        def _():
            acc[...] = jnp.zeros_like(acc)

        acc[...] += _dot(a_ref[...], b_ref[...])

        @pl.when(k == nk - 1)
        def _():
            o_ref[...] = acc[...].T.astype(BF16)

    return pl.pallas_call(
        body, name="mm_dw", grid=(IN_W // PAIR_W, nk),
        in_specs=[pl.BlockSpec((D_MODEL, tk), lambda p, k: (0, k)), pl.BlockSpec((tk, PAIR_W), lambda p, k: (k, p))],
        out_specs=pl.BlockSpec((PAIR_W, D_MODEL), lambda p, k: (p, 0)),
        out_shape=SDS((IN_W, D_MODEL), BF16), scratch_shapes=[pltpu.VMEM((D_MODEL, PAIR_W), F32)],
        compiler_params=_params(),
    )(ht, dproj)


def _mm_dh(dproj, wt, token):
    s = dproj.shape[0]
    tm = 1024

    def body(dp_ref, w_ref, tok_ref, o_ref):
        del tok_ref
        p = pl.program_id(1)
        part = _dot(dp_ref[...], w_ref[...])

        @pl.when(p == 0)
        def _():
            o_ref[...] = part

        @pl.when(p > 0)
        def _():
            o_ref[...] += part

    return pl.pallas_call(
        body, name="mm_dh", grid=(s // tm, IN_W // PAIR_W),
        in_specs=[pl.BlockSpec((tm, PAIR_W), lambda m, p: (m, p)),
                  pl.BlockSpec((PAIR_W, D_MODEL), lambda m, p: (p, 0)),
                  pl.BlockSpec(token.shape, lambda m, p: (0, 0))],
        out_specs=pl.BlockSpec((tm, D_MODEL), lambda m, p: (m, 0)),
        out_shape=SDS((s, D_MODEL), F32), compiler_params=_params(),
    )(dproj, wt, token)


def _norm_bwd(x, dh, dout, norm_w, scale):
    s = x.shape[0]
    tr = 512

    def body(x_ref, dh_ref, do_ref, nw_ref, sc_ref, gx_ref, dsh_ref, dsc_ref, dnw_ref):
        i = pl.program_id(0)

        @pl.when(i == 0)
        def _():
            for r in (dsh_ref, dsc_ref, dnw_ref):
                r[...] = jnp.zeros_like(r)

        def acc_rows(ref, v):
            ref[...] += jnp.broadcast_to(jnp.sum(v, axis=0, keepdims=True), ref.shape)

        xv = x_ref[...]
        dh_v = dh_ref[...]
        r = lax.rsqrt(jnp.mean(xv * xv, axis=-1, keepdims=True) + EPS)
        xn = xv * r
        one_sc = 1.0 + sc_ref[...]
        acc_rows(dsh_ref, dh_v)
        acc_rows(dsc_ref, dh_v * (xn * nw_ref[...]))
        acc_rows(dnw_ref, dh_v * xn * one_sc)
        dxn = dh_v * (nw_ref[...] * one_sc)
        gx_ref[...] = do_ref[...] + r * (dxn - xn * jnp.mean(dxn * xn, axis=-1, keepdims=True))

    blk = pl.BlockSpec((tr, D_MODEL), lambda i: (i, 0))
    vec = pl.BlockSpec((1, D_MODEL), lambda i: (0, 0))
    acc = pl.BlockSpec((8, D_MODEL), lambda i: (0, 0))
    return pl.pallas_call(
        body, name="norm_bwd", grid=(s // tr,), in_specs=[blk, blk, blk, vec, vec],
        out_specs=[blk, acc, acc, acc],
        out_shape=[SDS((s, D_MODEL), F32)] + [SDS((8, D_MODEL), F32)] * 3, compiler_params=_params(),
    )(x, dh, dout, norm_w, scale)


def _row_tile(rows):
    if rows <= 128:
        return rows
    return 128 if rows % 128 == 0 else SHARD_W // 4


def _adamw(gsrc, w, m, v, name, stacked):
    rows, cols = w.shape
    tr = _row_tile(rows)
    bc1 = 1.0 - ADAM_B1 ** ADAM_STEP
    bc2 = 1.0 - ADAM_B2 ** ADAM_STEP
    n_src = len(gsrc) if stacked else 1

    def body(*refs):
        g_refs, (w_ref, m_ref, v_ref, go_ref, d_ref, mo_ref, vo_ref) = refs[:n_src], refs[n_src:]
        if stacked:
            g = None
            for g_ref, (_, slots) in zip(g_refs, gsrc):
                for j in range(slots):
                    t = g_ref[j].astype(F32)
                    g = t if g is None else g + t
        else:
            g = g_refs[0][...]
        m_new = ADAM_B1 * m_ref[...] + (1.0 - ADAM_B1) * g
        v_new = ADAM_B2 * v_ref[...] + (1.0 - ADAM_B2) * (g * g)
        m_hat = m_new / bc1
        v_hat = v_new / bc2
        go_ref[...] = g
        d_ref[...] = -ADAM_LR * (m_hat / (jnp.sqrt(v_hat) + ADAM_EPS) + ADAM_WD * w_ref[...])
        mo_ref[...] = m_new
        vo_ref[...] = v_new

    blk = pl.BlockSpec((tr, cols), lambda i: (i, 0))
    if stacked:
        gspecs = [pl.BlockSpec((slots, tr, arr.shape[2]), lambda i: (0, i, 0)) for arr, slots in gsrc]
        gargs = [arr for arr, _ in gsrc]
    else:
        gspecs, gargs = [blk], [gsrc]
    in_specs = gspecs + [blk, blk, blk]
    args = gargs + [w, m, v]
    return pl.pallas_call(
        body, name=name, grid=(rows // tr,), in_specs=in_specs, out_specs=[blk] * 4,
        out_shape=[SDS((rows, cols), F32)] * 4, compiler_params=_params(),
    )(*args)


def _pack_small(parts):
    cols = []
    for name, length in _SMALL:
        p = parts[name].reshape(1, -1)
        cols.append(jnp.pad(p, ((0, 0), (0, length - p.shape[1]))))
    return jnp.concatenate(cols, axis=1)


def _unpack_small(vec, name, n):
    off, _ = SMALL_OFF[name]
    return vec[:, off:off + n]


def kernel(x, c, w_ada, b_ada, norm_w, w_in, b_gate, q_norm_w, k_norm_w, w_attn_proj, conv_w, conv_b, conv_ln_w, conv_ln_b, w_conv_proj, w_out, loss_target, m_w_ada, m_b_ada, m_norm_w, m_w_in, m_b_gate, m_q_norm_w, m_k_norm_w, m_w_attn_proj, m_conv_w, m_conv_b, m_conv_ln_w, m_conv_ln_b, m_w_conv_proj, m_w_out, v_w_ada, v_b_ada, v_norm_w, v_w_in, v_b_gate, v_q_norm_w, v_k_norm_w, v_w_attn_proj, v_conv_w, v_conv_b, v_conv_ln_w, v_conv_ln_b, v_w_conv_proj, v_w_out):
    xi, yi, ci = lax.axis_index("x"), lax.axis_index("y"), lax.axis_index("c")
    me = 4 * xi + 2 * yi + ci
    x2, tgt2 = x[0], loss_target[0]
    w_in_t, m_w_in_t, v_w_in_t = (jnp.transpose(a[0]) for a in (w_in, m_w_in, v_w_in))
    s = x2.shape[0]

    cw_flat = jnp.pad(conv_w[0].reshape(1, -1), ((0, 0), (0, CONVW_FLAT - CONV_K * HEAD_DIM)))
    pre = jnp.concatenate([c, cw_flat], axis=1).reshape(8, -1)
    (pre_all,) = _all_gather([pre], "gather_c_convw", vmem=True)
    pre_all = pre_all.reshape(N_DEV, -1)
    c_all = pre_all[:, :D_MODEL]
    convw_full = pre_all[:, D_MODEL:D_MODEL + CONV_K * HEAD_DIM].reshape(N_DEV, CONV_K, HEAD_DIM)
    convw_full = jnp.transpose(convw_full, (1, 0, 2)).reshape(CONV_K, CONV_W)
    convw_pad = jnp.pad(convw_full, ((0, CONV_HALO - CONV_K), (0, 0)))

    ada_part = _ada_fwd(c_all, w_ada[0])
    (ada_all,) = _all_gather([ada_part], "gather_ada", vmem=True)
    ada = lax.dynamic_index_in_dim(ada_all, me, axis=1, keepdims=False).reshape(1, 3 * D_MODEL) + b_ada
    shift, scale, gate = ada[:, :D_MODEL], ada[:, D_MODEL:2 * D_MODEL], ada[:, 2 * D_MODEL:]

    wt_g, wa_g, wc_g, wo_g = _all_gather_chips(
        [_cast_bf16(w_in_t, "cast_win"), _cast_bf16(w_attn_proj[0], "cast_wa"), _cast_bf16(w_conv_proj[0], "cast_wc"),
         _cast_bf16(w_out[0], "cast_wo")], "gather_weights")
    wt = wt_g.reshape(IN_W, D_MODEL)
    wa = _cols_from_slots(wa_g, "cols_wa")
    wc = _cols_from_slots(wc_g, "cols_wc")
    wo = wo_g.reshape(D_MODEL, D_MODEL)

    h, ht = _norm_fwd(x2, norm_w, scale, shift)
    proj = _mm_in(h, wt)
    qw2 = jnp.tile(q_norm_w, (1, 2))
    kw2 = jnp.tile(k_norm_w, (1, 2))
    o3, l3 = [], []
    for g in range(N_GROUPS):
        o_g, l_g = _attn_fwd(proj, qw2, kw2, g)
        o3.append(o_g)
        l3.append(l_g)
    head_id = jnp.arange(ATTN_W) // HEAD_DIM
    bd = (head_id[:, None] == head_id[None, :]).astype(BF16)
    (dout, da, delta, lse, dcv, mt, yat, yct, dmo, dya, dyc, dproj,
     dgate, dbg, dlnw, dlnb, dcb, loss_p) = _tail(
        x2, tgt2, proj, o3, l3, wa, wc, wo, gate, b_gate[:, :D_MODEL], b_gate[:, D_MODEL:], convw_pad,
        conv_b, conv_ln_w, conv_ln_b, bd)

    dproj, dconvw8 = _conv_bwd(dcv, proj, convw_pad, dproj)
    dconvw = jnp.sum(dconvw8.reshape(CONV_HALO, 8, CONV_W), axis=1)
    dqw = jnp.zeros((1, HEAD_DIM), F32)
    dkw = jnp.zeros((1, HEAD_DIM), F32)
    for g in range(N_GROUPS):
        dproj, dqw_g, dkw_g = _attn_bwd(proj, da, delta, lse, qw2, kw2, dproj, g)
        dqw = dqw + dqw_g[0:1, :HEAD_DIM] + dqw_g[0:1, HEAD_DIM:]
        dkw = dkw + dkw_g[0:1, :HEAD_DIM] + dkw_g[0:1, HEAD_DIM:]
    dw_in_p = _mm_dw(ht, dproj).reshape(N_DEV, SHARD_W, D_MODEL)
    dwo_p = _mm_acc(mt, dmo, "mm_dwo", col_slots=False).reshape(N_DEV, D_MODEL // N_DEV, D_MODEL)
    dwa_p = _mm_acc(yat, dya, "mm_dwa", col_slots=True)
    dwc_p = _mm_acc(yct, dyc, "mm_dwc", col_slots=True)

    partials = [dw_in_p, dwa_p, dwc_p, dwo_p]
    me_arr = jnp.reshape(me, (1,)).astype(jnp.int32)
    from_sib = _exchange_sibling(partials, "exchange_sibling")
    presums = [_presum(p, f, me_arr, f"presum{i}") for i, (p, f) in enumerate(zip(partials, from_sib))]
    s_sems, r_sems, pre_thru, land_thru, token = _exchange_chips_start(presums, "exchange_chips_start")
    dh = _mm_dh(dproj, wt, token)
    gx, dsh, dsc, dnw = _norm_bwd(x2, dh, dout, norm_w, scale)
    d_ada = jnp.concatenate([dsh[0:1], dsc[0:1], dgate[0:1]], axis=1)
    small_p = _pack_small({"b_ada": d_ada, "norm_w": dnw[0:1], "b_gate": dbg[0:1], "q_norm_w": dqw, "k_norm_w": dkw,
                           "conv_b": dcb[0:1], "conv_ln_w": dlnw[0:1], "conv_ln_b": dlnb[0:1], "loss": loss_p[0:1]})
    small_all, dconvw_all = _all_gather([jnp.broadcast_to(small_p, (8, SMALL_N)), dconvw], "gather_small", vmem=True)
    small_all = small_all[:, 0:1, :]
    dcw_mine = lax.dynamic_slice_in_dim(dconvw_all[:, :CONV_K, :], me * HEAD_DIM, HEAD_DIM, axis=2)
    dcw_mine = jnp.pad(dcw_mine.reshape(N_DEV, 1, -1), ((0, 0), (0, 0), (0, CONVW_FLAT - CONV_K * HEAD_DIM)))
    pack_g = jnp.concatenate([small_all, dcw_mine], axis=2)

    def pack_params(tree):
        small = _pack_small({"b_ada": tree["b_ada"], "norm_w": tree["norm_w"], "b_gate": tree["b_gate"],
                             "q_norm_w": tree["q_norm_w"], "k_norm_w": tree["k_norm_w"], "conv_b": tree["conv_b"],
                             "conv_ln_w": tree["conv_ln_w"], "conv_ln_b": tree["conv_ln_b"],
                             "loss": jnp.ones((1, 1), F32)})
        cw = jnp.pad(tree["conv_w"][0].reshape(1, -1), ((0, 0), (0, CONVW_FLAT - CONV_K * HEAD_DIM)))
        return jnp.concatenate([small, cw], axis=1)

    names = ("b_ada", "norm_w", "b_gate", "q_norm_w", "k_norm_w", "conv_b", "conv_ln_w", "conv_ln_b", "conv_w")
    w_tree = dict(zip(names, (b_ada, norm_w, b_gate, q_norm_w, k_norm_w, conv_b, conv_ln_w, conv_ln_b, conv_w)))
    m_tree = dict(zip(names, (m_b_ada, m_norm_w, m_b_gate, m_q_norm_w, m_k_norm_w, m_conv_b, m_conv_ln_w, m_conv_ln_b, m_conv_w)))
    v_tree = dict(zip(names, (v_b_ada, v_norm_w, v_b_gate, v_q_norm_w, v_k_norm_w, v_conv_b, v_conv_ln_w, v_conv_ln_b, v_conv_w)))
    pk = _adamw([(pack_g, N_DEV)], pack_params(w_tree), pack_params(m_tree), pack_params(v_tree), "adamw_small",
                stacked=True)

    d_ada_all = small_all[:, 0, :3 * D_MODEL]
    d_ada_cols = lax.dynamic_slice_in_dim(d_ada_all, me * (3 * D_MODEL // N_DEV), 3 * D_MODEL // N_DEV, axis=1)
    g_wada = _ada_bwd(c_all, d_ada_cols)
    r_ada = _adamw(g_wada, w_ada[0], m_w_ada[0], v_w_ada[0], "adamw_w_ada", stacked=False)
    pres, lands = _exchange_chips_wait(s_sems, r_sems, pre_thru, land_thru, r_ada[1], "exchange_chips_wait")
    terms = [[(p, 1), (l, len(CHIP_K))] for p, l in zip(pres, lands)]
    r_win = [jnp.transpose(r) for r in _adamw(terms[0], w_in_t, m_w_in_t, v_w_in_t, "adamw_w_in", stacked=True)]
    r_wap = _adamw(terms[1], w_attn_proj[0], m_w_attn_proj[0], v_w_attn_proj[0], "adamw_w_attn_proj", stacked=True)
    r_wcp = _adamw(terms[2], w_conv_proj[0], m_w_conv_proj[0], v_w_conv_proj[0], "adamw_w_conv_proj", stacked=True)
    r_wout = _adamw(terms[3], w_out[0], m_w_out[0], v_w_out[0], "adamw_w_out", stacked=True)

    def small_out(k, name, n):
        return _unpack_small(pk[k], name, n)

    def convw_out(k):
        return pk[k][:, SMALL_N:SMALL_N + CONV_K * HEAD_DIM].reshape(1, CONV_K, HEAD_DIM)

    loss = pk[0][0, SMALL_OFF["loss"][0]]
    outs = [loss, gx[None]]
    for k in range(4):
        outs += [r_ada[k][None], small_out(k, "b_ada", 3 * D_MODEL), small_out(k, "norm_w", D_MODEL), r_win[k][None],
                 small_out(k, "b_gate", 2 * D_MODEL), small_out(k, "q_norm_w", HEAD_DIM), small_out(k, "k_norm_w", HEAD_DIM),
                 r_wap[k][None], convw_out(k), small_out(k, "conv_b", CONV_W), small_out(k, "conv_ln_w", CONV_W),
                 small_out(k, "conv_ln_b", CONV_W), r_wcp[k][None], r_wout[k][None]]
    return tuple(outs)
```

```python
import functools

import jax
import jax.numpy as jnp
from jax import lax
from jax.experimental import pallas as pl
from jax.experimental.pallas import tpu as pltpu

F32 = jnp.float32
BF16 = jnp.bfloat16
SDS = jax.ShapeDtypeStruct
MESH = pl.DeviceIdType.MESH

N_DEV = 8
D_MODEL = 1024
HEAD_DIM = 64
N_GROUPS = 3
DILATIONS = (1, 4, 16)
BAND = 128
BWD_UNROLL = 8
ATTN_W = 512
CONV_W = 512
CONV_K = 31
CONV_HALO = 32
IN_W = 8704
SHARD_W = IN_W // N_DEV
PAIR_W = 2 * SHARD_W
Q0, K0, V0, ZA0, U0, ZC0, G0 = 0, 1536, 3072, 4608, 5120, 6144, 6656
EPS = 1e-6
LANE = 128
VMEM_LIMIT = 56 * 1024 * 1024

ADAM_LR, ADAM_B1, ADAM_B2, ADAM_EPS, ADAM_WD, ADAM_STEP = 0.001, 0.9, 0.999, 1e-08, 0.01, 10

CONVW_FLAT = 2048


def _params(**kw):
    return pltpu.CompilerParams(vmem_limit_bytes=VMEM_LIMIT, **kw)


def _sigmoid(z):
    return 0.5 * jnp.tanh(0.5 * z) + 0.5


def _dot(a, b):
    return jnp.dot(a, b, preferred_element_type=F32)


def _dot_nt(a, b):
    return lax.dot_general(a, b, (((1,), (1,)), ((), ())), preferred_element_type=F32)


def _dot_tn(a, b):
    return lax.dot_general(a, b, (((0,), (0,)), ((), ())), preferred_element_type=F32)


def _peer(x, y, c, k):
    px = 1 - x if (k >> 2) & 1 else x
    py = 1 - y if (k >> 1) & 1 else y
    pc = 1 - c if k & 1 else c
    return (px, py, pc), 4 * px + 2 * py + pc


def _all_gather(arrays, name, vmem):
    n = len(arrays)
    space = pltpu.VMEM if vmem else pl.ANY

    def body(*refs):
        ins, outs = refs[:n], refs[n:2 * n]
        send_sems, recv_sems, local_sems = refs[2 * n:]
        x, y, c = lax.axis_index("x"), lax.axis_index("y"), lax.axis_index("c")
        me = 4 * x + 2 * y + c
        locals_ = [pltpu.make_async_copy(ins[a], outs[a].at[me], local_sems.at[a]) for a in range(n)]
        for cp in locals_:
            cp.start()
        sends = []
        for k in range(1, N_DEV):
            peer, _ = _peer(x, y, c, k)
            for a in range(n):
                cp = pltpu.make_async_remote_copy(
                    src_ref=ins[a], dst_ref=outs[a].at[me], send_sem=send_sems.at[a, k - 1],
                    recv_sem=recv_sems.at[a, k - 1], device_id=peer, device_id_type=MESH)
                cp.start()
                sends.append(cp)
        for k in range(1, N_DEV):
            peer, pidx = _peer(x, y, c, k)
            for a in range(n):
                pltpu.make_async_remote_copy(
                    src_ref=ins[a], dst_ref=outs[a].at[pidx], send_sem=send_sems.at[a, k - 1],
                    recv_sem=recv_sems.at[a, k - 1], device_id=peer, device_id_type=MESH).wait_recv()
        for cp in sends:
            cp.wait_send()
        for cp in locals_:
            cp.wait()

    return pl.pallas_call(
        body, name=name,
        out_shape=[SDS((N_DEV,) + a.shape, a.dtype) for a in arrays],
        in_specs=[pl.BlockSpec(memory_space=space)] * n,
        out_specs=[pl.BlockSpec(memory_space=space)] * n,
        scratch_shapes=[pltpu.SemaphoreType.DMA((n, N_DEV - 1)), pltpu.SemaphoreType.DMA((n, N_DEV - 1)),
                        pltpu.SemaphoreType.DMA((n,))],
        compiler_params=_params(),
    )(*arrays)


CHIP_K = (2, 4, 6)


def _all_gather_chips(arrays, name):
    n = len(arrays)

    def body(*refs):
        ins, outs = refs[:n], refs[n:2 * n]
        send_sems, recv_sems, local_sems = refs[2 * n:]
        x, y, c = lax.axis_index("x"), lax.axis_index("y"), lax.axis_index("c")
        me = 4 * x + 2 * y + c
        sib, sib_idx = _peer(x, y, c, 1)

        def copy(a, slot, block, to, src=None):
            return pltpu.make_async_remote_copy(
                src_ref=outs[a].at[block] if src is None else src, dst_ref=outs[a].at[block],
                send_sem=send_sems.at[a, slot], recv_sem=recv_sems.at[a, slot], device_id=to, device_id_type=MESH)

        locals_ = [pltpu.make_async_copy(ins[a], outs[a].at[me], local_sems.at[a]) for a in range(n)]
        for cp in locals_:
            cp.start()
        sends = [copy(a, 0, me, sib, src=ins[a]) for a in range(n)]
        for j, k in enumerate(CHIP_K):
            peer, _ = _peer(x, y, c, k)
            sends += [copy(a, 1 + j, me, peer, src=ins[a]) for a in range(n)]
        for cp in sends:
            cp.start()
        for j, k in enumerate(CHIP_K):
            peer, pidx = _peer(x, y, c, k)
            for a in range(n):
                copy(a, 1 + j, pidx, peer).wait_recv()
                fwd = copy(a, 4 + j, pidx, sib)
                fwd.start()
                sends.append(fwd)
        for a in range(n):
            copy(a, 0, sib_idx, sib).wait_recv()
        for j, k in enumerate(CHIP_K):
            _, pidx = _peer(x, y, 1 - c, k)
            for a in range(n):
                copy(a, 4 + j, pidx, sib).wait_recv()
        for cp in sends:
            cp.wait_send()
        for cp in locals_:
            cp.wait()

    return pl.pallas_call(
        body, name=name,
        out_shape=[SDS((N_DEV,) + a.shape, a.dtype) for a in arrays],
        in_specs=[pl.BlockSpec(memory_space=pl.ANY)] * n,
        out_specs=[pl.BlockSpec(memory_space=pl.ANY)] * n,
        scratch_shapes=[pltpu.SemaphoreType.DMA((n, N_DEV - 1)), pltpu.SemaphoreType.DMA((n, N_DEV - 1)),
                        pltpu.SemaphoreType.DMA((n,))],
        compiler_params=_params(),
    )(*arrays)


def _exchange_sibling(arrays, name):
    n = len(arrays)
    ks = (0,) + CHIP_K

    def body(*refs):
        ins, outs = refs[:n], refs[n:2 * n]
        send_sems, recv_sems = refs[2 * n:]
        x, y, c = lax.axis_index("x"), lax.axis_index("y"), lax.axis_index("c")
        sib, sib_idx = _peer(x, y, c, 1)
        sends = []
        for i, k in enumerate(ks):
            _, tgt = _peer(x, y, 1 - c, k) if k else (None, sib_idx)
            for a in range(n):
                cp = pltpu.make_async_remote_copy(
                    src_ref=ins[a].at[tgt], dst_ref=outs[a].at[i], send_sem=send_sems.at[a, i],
                    recv_sem=recv_sems.at[a, i], device_id=sib, device_id_type=MESH)
                cp.start()
                sends.append(cp)
        for cp in sends:
            cp.wait_recv()
        for cp in sends:
            cp.wait_send()

    return pl.pallas_call(
        body, name=name,
        out_shape=[SDS((len(ks),) + a.shape[1:], a.dtype) for a in arrays],
        in_specs=[pl.BlockSpec(memory_space=pl.ANY)] * n,
        out_specs=[pl.BlockSpec(memory_space=pl.ANY)] * n,
        scratch_shapes=[pltpu.SemaphoreType.DMA((n, len(ks))), pltpu.SemaphoreType.DMA((n, len(ks)))],
        compiler_params=_params(),
    )(*arrays)


def _presum(mine, from_sib, me_arr, name):
    _, rows, cols = mine.shape
    tr = _row_tile(rows)
    ns = 1 + len(CHIP_K)

    def body(me_ref, a_ref, b_ref, o_ref):
        del me_ref
        o_ref[...] = (a_ref[...].astype(F32) + b_ref[...].astype(F32)).astype(o_ref.dtype)

    grid_spec = pltpu.PrefetchScalarGridSpec(
        num_scalar_prefetch=1, grid=(ns, rows // tr),
        in_specs=[pl.BlockSpec((1, tr, cols), lambda j, i, me: (jnp.bitwise_xor(me[0], 2 * j), i, 0)),
                  pl.BlockSpec((1, tr, cols), lambda j, i, me: (j, i, 0))],
        out_specs=pl.BlockSpec((1, tr, cols), lambda j, i, me: (j, i, 0)))
    return pl.pallas_call(body, name=name, grid_spec=grid_spec, out_shape=SDS((ns, rows, cols), mine.dtype),
                          compiler_params=_params())(me_arr, mine, from_sib)


HBM_SPEC = pl.BlockSpec(memory_space=pltpu.HBM)
SEM_SPEC = pl.BlockSpec(memory_space=pltpu.SEMAPHORE)
SIDE_EFFECT = pltpu.SideEffectType.DATAFLOW_SIDE_EFFECTING


def _chips_copies(pre_refs, land_refs, send_sems, recv_sems):
    x, y, c = lax.axis_index("x"), lax.axis_index("y"), lax.axis_index("c")
    copies = []
    for j, k in enumerate(CHIP_K):
        peer, _ = _peer(x, y, c, k)
        for a in range(len(pre_refs)):
            copies.append(pltpu.make_async_remote_copy(
                src_ref=pre_refs[a].at[1 + j], dst_ref=land_refs[a].at[j], send_sem=send_sems.at[a * len(CHIP_K) + j],
                recv_sem=recv_sems.at[a * len(CHIP_K) + j], device_id=peer, device_id_type=MESH))
    return copies


def _exchange_chips_start(presums, name):
    n = len(presums)

    def body(*refs):
        pre, land = refs[:n], refs[n:2 * n]
        send_sems, recv_sems = refs[2 * n], refs[2 * n + 1]
        token = refs[-1]
        for cp in _chips_copies(pre, land, send_sems, recv_sems):
            cp.start()
        token[...] = jnp.zeros_like(token)

    nk = len(CHIP_K)
    hbm = [pltpu.HBM(p.shape, p.dtype) for p in presums]
    hbm_land = [pltpu.HBM((nk,) + p.shape[1:], p.dtype) for p in presums]
    res = pl.pallas_call(
        body, name=name,
        out_shape=(pltpu.SemaphoreType.DMA((n * nk,)), pltpu.SemaphoreType.DMA((n * nk,)), *hbm, *hbm_land, SDS((8, LANE), F32)),
        in_specs=[HBM_SPEC] * (2 * n),
        out_specs=(SEM_SPEC, SEM_SPEC, *([HBM_SPEC] * (2 * n)), pl.BlockSpec(memory_space=pltpu.VMEM)),
        input_output_aliases={i: 2 + i for i in range(2 * n)},
        compiler_params=pltpu.CompilerParams(has_side_effects=SIDE_EFFECT),
    )(*[pltpu.with_memory_space_constraint(p, pltpu.HBM) for p in presums],
      *[pltpu.with_memory_space_constraint(lax.empty((nk,) + p.shape[1:], p.dtype), pltpu.HBM) for p in presums])
    return res[0], res[1], res[2:2 + n], res[2 + n:2 + 2 * n], res[-1]


def _exchange_chips_wait(send_sems, recv_sems, pre_thru, land_thru, after, name):
    n = len(pre_thru)

    def body(*refs):
        pre, land = refs[:n], refs[n:2 * n]
        s_sems, r_sems = refs[2 * n], refs[2 * n + 1]
        for cp in _chips_copies(pre, land, s_sems, r_sems):
            cp.wait_send()
            cp.wait_recv()

    hbm = [pltpu.HBM(p.shape, p.dtype) for p in (*pre_thru, *land_thru)]
    res = pl.pallas_call(
        body, name=name, out_shape=tuple(hbm),
        in_specs=[HBM_SPEC] * (2 * n) + [SEM_SPEC, SEM_SPEC, pl.BlockSpec(memory_space=pl.ANY)],
        out_specs=tuple([HBM_SPEC] * (2 * n)),
        input_output_aliases={i: i for i in range(2 * n)},
        compiler_params=pltpu.CompilerParams(has_side_effects=SIDE_EFFECT),
    )(*pre_thru, *land_thru, send_sems, recv_sems, after)
    return res[:n], res[n:]


def _exchange_chips(presums, name):
    n = len(presums)
    nk = len(CHIP_K)

    def body(*refs):
        pre, land = refs[:n], refs[n:2 * n]
        send_sems, recv_sems = refs[2 * n:]
        copies = _chips_copies(pre, land, send_sems, recv_sems)
        for cp in copies:
            cp.start()
        for cp in copies:
            cp.wait_recv()
        for cp in copies:
            cp.wait_send()

    return pl.pallas_call(
        body, name=name,
        out_shape=[SDS((nk,) + p.shape[1:], p.dtype) for p in presums],
        in_specs=[pl.BlockSpec(memory_space=pl.ANY)] * n,
        out_specs=[pl.BlockSpec(memory_space=pl.ANY)] * n,
        scratch_shapes=[pltpu.SemaphoreType.DMA((n * nk,)), pltpu.SemaphoreType.DMA((n * nk,))],
        compiler_params=_params(),
    )(*presums)


def _cast_bf16(w, name):
    def body(w_ref, o_ref):
        o_ref[...] = w_ref[...].astype(BF16)

    return pl.pallas_call(body, name=name, out_shape=SDS(w.shape, BF16), compiler_params=_params())(w)


def _cols_from_slots(wg, name):
    _, rows, cols = wg.shape

    def body(w_ref, o_ref):
        for j in range(N_DEV):
            o_ref[:, j * cols:(j + 1) * cols] = w_ref[j]

    return pl.pallas_call(body, name=name, out_shape=SDS((rows, N_DEV * cols), wg.dtype), compiler_params=_params())(wg)


def _ada_fwd(c_all, w_ada):
    def body(c_ref, w_ref, o_ref):
        cv = c_ref[...]
        sc = (cv * _sigmoid(cv)).astype(BF16)
        o_ref[...] = _dot(sc, w_ref[...].astype(BF16))

    return pl.pallas_call(body, name="ada_fwd", out_shape=SDS((N_DEV, w_ada.shape[1]), F32),
                          compiler_params=_params())(c_all, w_ada)


def _ada_bwd(c_all, d_ada_cols):
    def body(c_ref, d_ref, o_ref):
        cv = c_ref[...]
        sc = (cv * _sigmoid(cv)).astype(BF16)
        o_ref[...] = _dot_tn(sc, d_ref[...].astype(BF16))

    return pl.pallas_call(body, name="ada_bwd", out_shape=SDS((D_MODEL, d_ada_cols.shape[1]), F32),
                          compiler_params=_params())(c_all, d_ada_cols)


def _norm_fwd(x, norm_w, scale, shift):
    s = x.shape[0]
    tr = 512

    def body(x_ref, nw_ref, sc_ref, sh_ref, h_ref, ht_ref):
        xv = x_ref[...]
        r = lax.rsqrt(jnp.mean(xv * xv, axis=-1, keepdims=True) + EPS)
        h = (xv * r * nw_ref[...]) * (1.0 + sc_ref[...]) + sh_ref[...]
        h_ref[...] = h.astype(BF16)
        ht_ref[...] = h.T.astype(BF16)

    vec = pl.BlockSpec((1, D_MODEL), lambda i: (0, 0))
    return pl.pallas_call(
        body, name="norm_fwd", grid=(s // tr,),
        in_specs=[pl.BlockSpec((tr, D_MODEL), lambda i: (i, 0)), vec, vec, vec],
        out_specs=[pl.BlockSpec((tr, D_MODEL), lambda i: (i, 0)), pl.BlockSpec((D_MODEL, tr), lambda i: (0, i))],
        out_shape=[SDS((s, D_MODEL), BF16), SDS((D_MODEL, s), BF16)], compiler_params=_params(),
    )(x, norm_w, scale, shift)


def _mm_in(h, wt):
    s = h.shape[0]
    tm = 512

    def body(h_ref, w_ref, o_ref):
        o_ref[...] = _dot_nt(h_ref[...], w_ref[...])

    return pl.pallas_call(
        body, name="mm_in", grid=(IN_W // PAIR_W, s // tm),
        in_specs=[pl.BlockSpec((tm, D_MODEL), lambda p, m: (m, 0)),
                  pl.BlockSpec((PAIR_W, D_MODEL), lambda p, m: (p, 0))],
        out_specs=pl.BlockSpec((tm, PAIR_W), lambda p, m: (m, p)),
        out_shape=SDS((s, IN_W), F32), compiler_params=_params(),
    )(h, wt)


def _head_ones():
    a = lax.broadcasted_iota(jnp.int32, (LANE, LANE), 0) // HEAD_DIM
    b = lax.broadcasted_iota(jnp.int32, (LANE, LANE), 1) // HEAD_DIM
    return (a == b).astype(BF16)


def _head_sums(t, ones):
    return _dot(t.astype(BF16), ones)


def _band_bias(bias):
    qi = lax.broadcasted_iota(jnp.int32, (2 * BAND, 2 * BAND), 0) % BAND
    kj = lax.broadcasted_iota(jnp.int32, (2 * BAND, 2 * BAND), 1)
    dist = qi + BAND - kj
    valid = (dist >= 0) & (dist <= BAND)
    bias[1] = jnp.where(valid, 0.0, -1e30)
    bias[0] = jnp.where(valid & (kj >= BAND), 0.0, -1e30)


def _token_rows(j, d, chunk, per_r):
    return pl.ds(j // per_r + (j % per_r) * (chunk * d), chunk, stride=d)


def _deinterleave(src_ref, dst_ref, w_ref, ones, d, sub_len, chunk, scale, dst_off):
    per_r = sub_len // chunk

    def step(j, _):
        t = src_ref[_token_rows(j, d, chunk, per_r), :]
        if w_ref is not None:
            ms = _head_sums(t * t, ones) * (1.0 / HEAD_DIM)
            t = t * lax.rsqrt(ms + EPS) * (w_ref[...] * scale)
        dst_ref[pl.ds(pl.multiple_of(dst_off + j * chunk, BAND), chunk), :] = t.astype(dst_ref.dtype)
        return 0
    lax.fori_loop(0, d * per_r, step, 0, unroll=4)


def _attn_fwd(proj, qw2, kw2, g):
    s = proj.shape[0]
    d = DILATIONS[g]
    sub_len = s // d
    nb = sub_len // BAND
    chunk = min(sub_len, 256)

    def body(q_ref, k_ref, v_ref, qw_ref, kw_ref, o_ref, l_ref, qd, kd, vd, od, ld, bias):
        lo = lax.broadcasted_iota(jnp.int32, (1, LANE), 1) < HEAD_DIM
        ones = _head_ones()

        @pl.when(pl.program_id(0) == 0)
        def _():
            _band_bias(bias)

        kd[0:BAND, :] = jnp.zeros((BAND, LANE), BF16)
        vd[0:BAND, :] = jnp.zeros((BAND, LANE), BF16)
        _deinterleave(q_ref, qd, qw_ref, ones, d, sub_len, chunk, HEAD_DIM ** -0.5, 0)
        _deinterleave(k_ref, kd, kw_ref, ones, d, sub_len, chunk, 1.0, BAND)
        _deinterleave(v_ref, vd, None, ones, d, sub_len, chunk, 1.0, BAND)

        def block(t, _):
            base = pl.multiple_of(t * BAND, BAND)
            q = qd[pl.ds(base, BAND), :]
            k2 = kd[pl.ds(base, 2 * BAND), :]
            v2 = vd[pl.ds(base, 2 * BAND), :]
            zero = jnp.zeros_like(q)
            qs = jnp.concatenate([jnp.where(lo, q, zero), jnp.where(lo, zero, q)], axis=0)
            sc = _dot_nt(qs, k2) + bias[jnp.minimum(t % nb, 1)]
            m = jnp.max(sc, axis=-1, keepdims=True)
            p = jnp.exp(sc - m)
            den = jnp.sum(p, axis=-1, keepdims=True)
            u = _dot(p.astype(BF16), v2) * (1.0 / den)
            lse = m + jnp.log(den)
            od[pl.ds(base, BAND), :] = jnp.where(lo, u[:BAND], u[BAND:])
            ld[pl.ds(base, BAND), :] = jnp.where(lo, lse[:BAND], lse[BAND:])
            return 0
        lax.fori_loop(0, s // BAND, block, 0, unroll=16)

        per_r = sub_len // chunk

        def back(j, _):
            src = pl.ds(pl.multiple_of(j * chunk, chunk), chunk)
            dst = _token_rows(j, d, chunk, per_r)
            o_ref[dst, :] = od[src, :]
            l_ref[dst, :] = ld[src, :]
            return 0
        lax.fori_loop(0, d * per_r, back, 0, unroll=2)

    col = lambda off: pl.BlockSpec((s, LANE), lambda hp, off=off: (0, off // LANE + 4 * g + hp))
    vec = pl.BlockSpec((1, LANE), lambda hp: (0, 0))
    out = pl.BlockSpec((s, LANE), lambda hp: (0, hp))
    return pl.pallas_call(
        body, name=f"attn_fwd{g}", grid=(ATTN_W // LANE,),
        in_specs=[col(Q0), col(K0), col(V0), vec, vec], out_specs=[out, out],
        out_shape=[SDS((s, ATTN_W), F32), SDS((s, ATTN_W), F32)],
        scratch_shapes=[pltpu.VMEM((s, LANE), BF16), pltpu.VMEM((s + BAND, LANE), BF16), pltpu.VMEM((s + BAND, LANE), BF16),
                        pltpu.VMEM((s, LANE), F32), pltpu.VMEM((s, LANE), F32),
                        pltpu.VMEM((2, 2 * BAND, 2 * BAND), F32)],
        compiler_params=_params(),
    )(proj, proj, proj, qw2, kw2)


def _attn_bwd(proj, da, delta, lse, qw2, kw2, dproj, g):
    s = proj.shape[0]
    d = DILATIONS[g]
    sub_len = s // d
    nb = sub_len // BAND
    chunk = min(sub_len, 256)

    def body(q_ref, k_ref, v_ref, da_ref, dl_ref, ls_ref, qw_ref, kw_ref, dp_in, dp_out, dqw_ref, dkw_ref,
             qd, kd, vd, dad, dld, lsd, dqd, dkd, dvd, st, stb, bias, wacc, sem):
        del dp_in
        hp = pl.program_id(0)
        lo = lax.broadcasted_iota(jnp.int32, (1, LANE), 1) < HEAD_DIM
        ones = _head_ones()
        per_r = sub_len // chunk

        @pl.when(hp == 0)
        def _():
            _band_bias(bias)

        kd[0:BAND, :] = jnp.zeros((BAND, LANE), BF16)
        vd[0:BAND, :] = jnp.zeros((BAND, LANE), BF16)
        _deinterleave(q_ref, qd, qw_ref, ones, d, sub_len, chunk, HEAD_DIM ** -0.5, 0)
        _deinterleave(k_ref, kd, kw_ref, ones, d, sub_len, chunk, 1.0, BAND)
        _deinterleave(v_ref, vd, None, ones, d, sub_len, chunk, 1.0, BAND)
        _deinterleave(da_ref, dad, None, ones, d, sub_len, chunk, 1.0, 0)
        _deinterleave(dl_ref, dld, None, ones, d, sub_len, chunk, 1.0, 0)
        _deinterleave(ls_ref, lsd, None, ones, d, sub_len, chunk, 1.0, 0)

        def block(t, carry):
            ck, cv = carry
            base = pl.multiple_of(t * BAND, BAND)
            q = qd[pl.ds(base, BAND), :]
            k2 = kd[pl.ds(base, 2 * BAND), :]
            v2 = vd[pl.ds(base, 2 * BAND), :]
            dav = dad[pl.ds(base, BAND), :]
            dlv = dld[pl.ds(base, BAND), :]
            lsv = lsd[pl.ds(base, BAND), :]
            zero = jnp.zeros_like(q)
            qs = jnp.concatenate([jnp.where(lo, q, zero), jnp.where(lo, zero, q)], axis=0)
            das = jnp.concatenate([jnp.where(lo, dav, zero), jnp.where(lo, zero, dav)], axis=0)
            ls_col = jnp.concatenate([lsv[:, 0:1], lsv[:, HEAD_DIM:HEAD_DIM + 1]], axis=0)
            dl_col = jnp.concatenate([dlv[:, 0:1], dlv[:, HEAD_DIM:HEAD_DIM + 1]], axis=0)
            sc = _dot_nt(qs, k2) + bias[jnp.minimum(t % nb, 1)]
            p = jnp.exp(sc - ls_col)
            dp = _dot_nt(das, v2)
            ds = (p * (dp - dl_col)).astype(BF16)
            dv2 = _dot_tn(p.astype(BF16), das)
            dk2 = _dot_tn(ds, qs)
            dvd[pl.ds(base, BAND), :] = cv + dv2[:BAND]
            dkd[pl.ds(base, BAND), :] = ck + dk2[:BAND]
            dq = _dot(ds, k2)
            dqd[pl.ds(base, BAND), :] = jnp.where(lo, dq[:BAND], dq[BAND:])
            return dk2[BAND:], dv2[BAND:]
        def blocks(i, carry):
            for u in range(BWD_UNROLL):
                carry = block(i * BWD_UNROLL + u, carry)
            return carry
        zeros = jnp.zeros((BAND, LANE), F32)
        ck, cv = lax.fori_loop(0, s // (BAND * BWD_UNROLL), blocks, (zeros, zeros))
        dkd[s:s + BAND, :] = ck
        dvd[s:s + BAND, :] = cv

        def store_cols(col0):
            stb[...] = st[...].astype(BF16)
            cp = pltpu.make_async_copy(
                stb, dp_out.at[:, pl.ds(pl.multiple_of(col0 + LANE * (4 * g + hp), LANE), LANE)], sem)
            cp.start()
            cp.wait()

        def norm_back(src_ref, dy_ref, dy_off, w_ref, scale, dw_ref, col0):
            wacc[...] = jnp.zeros_like(wacc)

            def step(j, _):
                tok = _token_rows(j, d, chunk, per_r)
                t = src_ref[tok, :]
                dy = dy_ref[pl.ds(pl.multiple_of(dy_off + j * chunk, BAND), chunk), :]
                rr = lax.rsqrt(_head_sums(t * t, ones) * (1.0 / HEAD_DIM) + EPS)
                nrm = t * rr
                wacc[...] += jnp.sum((dy * nrm).reshape(chunk // 8, 8, LANE), axis=0)
                dn = dy * (w_ref[...] * scale)
                st[tok, :] = rr * (dn - nrm * (_head_sums(dn * nrm, ones) * (1.0 / HEAD_DIM)))
                return 0
            lax.fori_loop(0, d * per_r, step, 0, unroll=4)
            dw_ref[...] += jnp.broadcast_to(jnp.sum(wacc[...], axis=0, keepdims=True) * scale, dw_ref.shape)
            store_cols(col0)

        @pl.when(hp == 0)
        def _():
            dqw_ref[...] = jnp.zeros_like(dqw_ref)
            dkw_ref[...] = jnp.zeros_like(dkw_ref)

        norm_back(q_ref, dqd, 0, qw_ref, HEAD_DIM ** -0.5, dqw_ref, Q0)
        norm_back(k_ref, dkd, BAND, kw_ref, 1.0, dkw_ref, K0)

        def v_back(j, _):
            src = pl.ds(pl.multiple_of(BAND + j * chunk, BAND), chunk)
            st[_token_rows(j, d, chunk, per_r), :] = dvd[src, :]
            return 0
        lax.fori_loop(0, d * per_r, v_back, 0, unroll=2)
        store_cols(V0)

    col = lambda off: pl.BlockSpec((s, LANE), lambda hp, off=off: (0, off // LANE + 4 * g + hp))
    mid = pl.BlockSpec((s, LANE), lambda hp: (0, hp))
    vec = pl.BlockSpec((1, LANE), lambda hp: (0, 0))
    acc = pl.BlockSpec((8, LANE), lambda hp: (0, 0))
    any_ = pl.BlockSpec(memory_space=pl.ANY)
    return pl.pallas_call(
        body, name=f"attn_bwd{g}", grid=(ATTN_W // LANE,),
        in_specs=[col(Q0), col(K0), col(V0), mid, mid, mid, vec, vec, any_],
        out_specs=[any_, acc, acc],
        out_shape=[SDS(dproj.shape, dproj.dtype), SDS((8, LANE), F32), SDS((8, LANE), F32)],
        input_output_aliases={8: 0},
        scratch_shapes=[pltpu.VMEM((s, LANE), BF16), pltpu.VMEM((s + BAND, LANE), BF16), pltpu.VMEM((s + BAND, LANE), BF16),
                        pltpu.VMEM((s, LANE), BF16), pltpu.VMEM((s, LANE), F32), pltpu.VMEM((s, LANE), F32),
                        pltpu.VMEM((s, LANE), F32), pltpu.VMEM((s + BAND, LANE), F32), pltpu.VMEM((s + BAND, LANE), F32),
                        pltpu.VMEM((s, LANE), F32), pltpu.VMEM((s, LANE), BF16),
                        pltpu.VMEM((2, 2 * BAND, 2 * BAND), F32), pltpu.VMEM((8, LANE), F32),
                        pltpu.SemaphoreType.DMA(())],
        compiler_params=_params(),
    )(proj, proj, proj, da, delta, lse, qw2, kw2, dproj)


def _tap_views(ext_ref, sh_ref, offsets, tr, cols):
    for b in range(8):
        group = [j for j, o in enumerate(offsets) if o % 8 == b]
        if not group:
            continue
        first = min(offsets[j] for j in group)
        span = tr + max(offsets[j] for j in group) - first
        sh_ref[0:span, cols] = ext_ref[first:first + span, cols]
        for j in group:
            yield j, sh_ref[offsets[j] - first:offsets[j] - first + tr, cols]


def _silu_grad(z, sg):
    return sg * (1.0 + z * (1.0 - sg))


def _glu(u):
    a_h, b_h = u[:, :CONV_W], u[:, CONV_W:]
    sg = _sigmoid(b_h)
    return a_h, sg, a_h * sg


def _tail(x, tgt, proj, o3, l3, wa, wc, wo, gate, bga, bgc, convw, convb, lnw, lnb, bd):
    s = x.shape[0]
    tr = 256

    def body(x_ref, t_ref, za_ref, u_ref, uh_ref, zc_ref, g0_ref, g1_ref, g2_ref, g3_ref,
             o0_ref, o1_ref, o2_ref, l0_ref, l1_ref, l2_ref, wa_ref, wc_ref, wo_ref,
             gate_ref, bga_ref, bgc_ref, cw_ref, cb_ref, lnw_ref, lnb_ref, bd_ref,
             dout_ref, da_ref, dl_ref, lse_ref, dcv_ref, mt_ref, yat_ref, yct_ref, dmo_ref, dya_ref, dyc_ref, dp_ref,
             dgate_ref, dbg_ref, dlnw_ref, dlnb_ref, dcb_ref, loss_ref,
             ext, sh, st_za, st_zc, st_g, sems):
        i = pl.program_id(0)

        @pl.when(i == 0)
        def _():
            for r in (dgate_ref, dbg_ref, dlnw_ref, dlnb_ref, dcb_ref, loss_ref):
                r[...] = jnp.zeros_like(r)

        def acc_rows(ref, v):
            ref[...] += jnp.broadcast_to(jnp.sum(v, axis=0, keepdims=True), ref.shape)

        la, lb, lc = l0_ref[...], l1_ref[...], l2_ref[...]
        mx = jnp.maximum(jnp.maximum(la, lb), lc)
        ea, eb, ec = jnp.exp(la - mx), jnp.exp(lb - mx), jnp.exp(lc - mx)
        den = ea + eb + ec
        inv = 1.0 / den
        attn = (ea * inv) * o0_ref[...] + (eb * inv) * o1_ref[...] + (ec * inv) * o2_ref[...]
        lse_ref[...] = mx + jnp.log(den)

        za = za_ref[...]
        sga = _sigmoid(za)
        sa = za * sga
        ya_in = attn * sa
        y_attn = _dot(ya_in.astype(BF16), wa_ref[...])

        _, _, glu = _glu(u_ref[...])
        _, _, glu_h = _glu(uh_ref[...])
        ext[0:CONV_HALO, :] = jnp.where(i > 0, glu_h, 0.0)
        ext[CONV_HALO:CONV_HALO + tr, :] = glu
        cv_blocks = []
        for cb in range(CONV_W // LANE):
            cols = slice(cb * LANE, (cb + 1) * LANE)
            cv_c = jnp.broadcast_to(cb_ref[:, cols], (tr, LANE))
            for j, rows in _tap_views(ext, sh, [CONV_HALO - (CONV_K - 1) + j for j in range(CONV_K)], tr, cols):
                cv_c = cv_c + cw_ref[j:j + 1, cols] * rows
            cv_blocks.append(cv_c)
        cv = jnp.concatenate(cv_blocks, axis=1)
        mu = jnp.mean(cv, axis=-1, keepdims=True)
        xc = cv - mu
        rstd = lax.rsqrt(jnp.mean(xc * xc, axis=-1, keepdims=True) + EPS)
        nrm = xc * rstd
        ln = nrm * lnw_ref[...] + lnb_ref[...]
        sgl = _sigmoid(ln)
        cs = ln * sgl
        zc = zc_ref[...]
        sgc = _sigmoid(zc)
        scz = zc * sgc
        yc_in = cs * scz
        y_conv = _dot(yc_in.astype(BF16), wc_ref[...])

        ga = _sigmoid(jnp.concatenate([g0_ref[...], g1_ref[...]], axis=1) + bga_ref[...])
        gc = _sigmoid(jnp.concatenate([g2_ref[...], g3_ref[...]], axis=1) + bgc_ref[...])
        merged = ga * y_attn + gc * y_conv
        mo = _dot(merged.astype(BF16), wo_ref[...])
        gate_v = gate_ref[...]
        err = (x_ref[...] + gate_v * mo) - t_ref[...]
        loss_ref[...] += 0.5 * jnp.sum(jnp.mean(err * err, axis=-1, keepdims=True))
        d_out = err * (1.0 / D_MODEL)
        dout_ref[...] = d_out

        acc_rows(dgate_ref, d_out * mo)
        dmo_b = (d_out * gate_v).astype(BF16)
        dmo_ref[...] = dmo_b
        mt_ref[...] = merged.T.astype(BF16)
        d_merged = _dot_nt(dmo_b, wo_ref[...])
        d_ya = (d_merged * ga).astype(BF16)
        d_yc = (d_merged * gc).astype(BF16)
        dya_ref[...] = d_ya
        dyc_ref[...] = d_yc
        dga = d_merged * y_attn * (ga * (1.0 - ga))
        dgc = d_merged * y_conv * (gc * (1.0 - gc))
        dgs = jnp.concatenate([dga, dgc], axis=1)
        acc_rows(dbg_ref, dgs)
        st_g[...] = dgs.astype(BF16)

        yat_ref[...] = ya_in.T.astype(BF16)
        d_ya_in = _dot_nt(d_ya, wa_ref[...])
        d_attn = d_ya_in * sa
        da_ref[...] = d_attn
        st_za[...] = (d_ya_in * attn * _silu_grad(za, sga)).astype(BF16)
        prod = d_attn * attn
        hi = prod.astype(BF16)
        lo_ = (prod - hi.astype(F32)).astype(BF16)
        dl_ref[...] = _dot(hi, bd_ref[...]) + _dot(lo_, bd_ref[...])

        yct_ref[...] = yc_in.T.astype(BF16)
        d_yc_in = _dot_nt(d_yc, wc_ref[...])
        st_zc[...] = (d_yc_in * cs * _silu_grad(zc, sgc)).astype(BF16)
        d_ln = (d_yc_in * scz) * _silu_grad(ln, sgl)
        acc_rows(dlnw_ref, d_ln * nrm)
        acc_rows(dlnb_ref, d_ln)
        d_nrm = d_ln * lnw_ref[...]
        d_cv = rstd * (d_nrm - jnp.mean(d_nrm, axis=-1, keepdims=True)
                       - nrm * jnp.mean(d_nrm * nrm, axis=-1, keepdims=True))
        acc_rows(dcb_ref, d_cv)
        dcv_ref[...] = d_cv

        rows = pl.ds(pl.multiple_of(i * tr, tr), tr)
        cps = [pltpu.make_async_copy(st_za, dp_ref.at[rows, pl.ds(ZA0, ATTN_W)], sems.at[0]),
               pltpu.make_async_copy(st_zc, dp_ref.at[rows, pl.ds(ZC0, CONV_W)], sems.at[1]),
               pltpu.make_async_copy(st_g, dp_ref.at[rows, pl.ds(G0, 2 * D_MODEL)], sems.at[2])]
        for cp in cps:
            cp.start()
        for cp in cps:
            cp.wait()

    def rows(width, colblk=0):
        return pl.BlockSpec((tr, width), lambda i, colblk=colblk: (i, colblk))

    def const(shape):
        return pl.BlockSpec(shape, lambda i: (0,) * len(shape))

    halo = pl.BlockSpec((CONV_HALO, D_MODEL), lambda i: (jnp.maximum(i * (tr // CONV_HALO) - 1, 0), U0 // D_MODEL))
    in_specs = [rows(D_MODEL), rows(D_MODEL), rows(ATTN_W, ZA0 // ATTN_W), rows(D_MODEL, U0 // D_MODEL), halo,
                rows(CONV_W, ZC0 // CONV_W)]
    in_specs += [rows(512, G0 // 512 + j) for j in range(4)]
    in_specs += [rows(ATTN_W)] * 6
    in_specs += [const(wa.shape), const(wc.shape), const(wo.shape), const((1, D_MODEL)), const((1, D_MODEL)),
                 const((1, D_MODEL)), const(convw.shape), const((1, CONV_W)), const((1, CONV_W)), const((1, CONV_W)),
                 const(bd.shape)]
    tcol = lambda width: pl.BlockSpec((width, tr), lambda i: (0, i))
    out_specs = [rows(D_MODEL), rows(ATTN_W), rows(ATTN_W), rows(ATTN_W), rows(CONV_W),
                 tcol(D_MODEL), tcol(ATTN_W), tcol(CONV_W), rows(D_MODEL), rows(D_MODEL), rows(D_MODEL),
                 pl.BlockSpec(memory_space=pl.ANY),
                 const((8, D_MODEL)), const((8, 2 * D_MODEL)), const((8, CONV_W)), const((8, CONV_W)), const((8, CONV_W)),
                 const((8, LANE))]
    out_shape = [SDS((s, D_MODEL), F32), SDS((s, ATTN_W), F32), SDS((s, ATTN_W), F32), SDS((s, ATTN_W), F32),
                 SDS((s, CONV_W), F32),
                 SDS((D_MODEL, s), BF16), SDS((ATTN_W, s), BF16), SDS((CONV_W, s), BF16),
                 SDS((s, D_MODEL), BF16), SDS((s, D_MODEL), BF16), SDS((s, D_MODEL), BF16),
                 SDS((s, IN_W), BF16),
                 SDS((8, D_MODEL), F32), SDS((8, 2 * D_MODEL), F32), SDS((8, CONV_W), F32), SDS((8, CONV_W), F32),
                 SDS((8, CONV_W), F32), SDS((8, LANE), F32)]
    return pl.pallas_call(
        body, name="tail", grid=(s // tr,), in_specs=in_specs, out_specs=out_specs, out_shape=out_shape,
        scratch_shapes=[pltpu.VMEM((CONV_HALO + tr, CONV_W), F32), pltpu.VMEM((CONV_HALO + tr, CONV_W), F32),
                        pltpu.VMEM((tr, ATTN_W), BF16),
                        pltpu.VMEM((tr, CONV_W), BF16), pltpu.VMEM((tr, 2 * D_MODEL), BF16),
                        pltpu.SemaphoreType.DMA((3,))],
        compiler_params=_params(),
    )(x, tgt, proj, proj, proj, proj, proj, proj, proj, proj, *o3, *l3, wa, wc, wo, gate, bga, bgc,
      convw, convb, lnw, lnb, bd)


def _conv_bwd(dcv, proj, convw, dproj):
    s = dcv.shape[0]
    tr = 128
    nt = s // tr

    def body(dcv_ref, dcvn_ref, u_ref, uh_ref, cw_ref, dp_in, dp_out, dw_ref, extg, extd, sh):
        del dp_in
        i = pl.program_id(0)

        @pl.when(i == 0)
        def _():
            dw_ref[...] = jnp.zeros_like(dw_ref)

        _, _, glu = _glu(u_ref[...])
        _, _, glu_h = _glu(uh_ref[...])
        extg[0:CONV_HALO, :] = jnp.where(i > 0, glu_h, 0.0)
        extg[CONV_HALO:CONV_HALO + tr, :] = glu
        extd[0:tr, :] = dcv_ref[...]
        extd[tr:tr + CONV_HALO, :] = jnp.where(i < nt - 1, dcvn_ref[...], 0.0)
        for cb in range(CONV_W // LANE):
            cols = slice(cb * LANE, (cb + 1) * LANE)
            dglu = jnp.zeros((tr, LANE), F32)
            for j, rows in _tap_views(extd, sh, [CONV_K - 1 - j for j in range(CONV_K)], tr, cols):
                dglu = dglu + cw_ref[j:j + 1, cols] * rows
            dcv_c = dcv_ref[:, cols]
            for j, rows in _tap_views(extg, sh, [CONV_HALO - (CONV_K - 1) + j for j in range(CONV_K)], tr, cols):
                dw_ref[8 * j:8 * j + 8, cols] += jnp.sum((dcv_c * rows).reshape(tr // 8, 8, LANE), axis=0)
            a_h = u_ref[:, cols]
            sgb = _sigmoid(u_ref[:, CONV_W + cb * LANE:CONV_W + (cb + 1) * LANE])
            dp_out[:, cols] = (dglu * sgb).astype(BF16)
            dp_out[:, CONV_W + cb * LANE:CONV_W + (cb + 1) * LANE] = (dglu * a_h * (sgb * (1.0 - sgb))).astype(BF16)

    ucol = U0 // D_MODEL
    return pl.pallas_call(
        body, name="conv_bwd", grid=(nt,),
        in_specs=[pl.BlockSpec((tr, CONV_W), lambda i: (i, 0)),
                  pl.BlockSpec((CONV_HALO, CONV_W), lambda i: (jnp.minimum((i + 1) * (tr // CONV_HALO), s // CONV_HALO - 1), 0)),
                  pl.BlockSpec((tr, D_MODEL), lambda i: (i, ucol)),
                  pl.BlockSpec((CONV_HALO, D_MODEL), lambda i: (jnp.maximum(i * (tr // CONV_HALO) - 1, 0), ucol)),
                  pl.BlockSpec(convw.shape, lambda i: (0, 0)),
                  pl.BlockSpec(memory_space=pl.ANY)],
        out_specs=[pl.BlockSpec((tr, D_MODEL), lambda i: (i, ucol)), pl.BlockSpec((8 * CONV_HALO, CONV_W), lambda i: (0, 0))],
        out_shape=[SDS(dproj.shape, dproj.dtype), SDS((8 * CONV_HALO, CONV_W), F32)],
        input_output_aliases={5: 0},
        scratch_shapes=[pltpu.VMEM((CONV_HALO + tr, CONV_W), F32)] * 3,
        compiler_params=_params(),
    )(dcv, dcv, proj, proj, convw, dproj)


def _mm_acc(at, b, name, col_slots):
    m, s = at.shape
    n = b.shape[1]
    tk = 512
    nk = s // tk

    def body(a_ref, b_ref, o_ref, acc):
        k = pl.program_id(0)

        @pl.when(k == 0)
        def _():
            acc[...] = jnp.zeros_like(acc)

        acc[...] += _dot(a_ref[...], b_ref[...])

        @pl.when(k == nk - 1)
        def _():
            if col_slots:
                w = n // N_DEV
                for j in range(N_DEV):
                    o_ref[j] = acc[:, j * w:(j + 1) * w].astype(BF16)
            else:
                o_ref[...] = acc[...].astype(BF16)

    if col_slots:
        out_shape = SDS((N_DEV, m, n // N_DEV), BF16)
        out_spec = pl.BlockSpec((N_DEV, m, n // N_DEV), lambda k: (0, 0, 0))
    else:
        out_shape = SDS((m, n), BF16)
        out_spec = pl.BlockSpec((m, n), lambda k: (0, 0))
    return pl.pallas_call(
        body, name=name, grid=(nk,),
        in_specs=[pl.BlockSpec((m, tk), lambda k: (0, k)), pl.BlockSpec((tk, n), lambda k: (k, 0))],
        out_specs=out_spec, out_shape=out_shape, scratch_shapes=[pltpu.VMEM((m, n), F32)],
        compiler_params=_params(),
    )(at, b)


def _mm_dw(ht, dproj):
    s = ht.shape[1]
    tk = 512
    nk = s // tk

    def body(a_ref, b_ref, o_ref, acc):
        k = pl.program_id(1)

        @pl.when(k == 0)
        def _():
            acc[...] = jnp.zeros_like(acc)

        acc[...] += _dot(a_ref[...], b_ref[...])

        @pl.when(k == nk - 1)
        def _():
            o_ref[...] = acc[...].T.astype(BF16)

    return pl.pallas_call(
        body, name="mm_dw", grid=(IN_W // PAIR_W, nk),
        in_specs=[pl.BlockSpec((D_MODEL, tk), lambda p, k: (0, k)), pl.BlockSpec((tk, PAIR_W), lambda p, k: (k, p))],
        out_specs=pl.BlockSpec((PAIR_W, D_MODEL), lambda p, k: (p, 0)),
        out_shape=SDS((IN_W, D_MODEL), BF16), scratch_shapes=[pltpu.VMEM((D_MODEL, PAIR_W), F32)],
        compiler_params=_params(),
    )(ht, dproj)


def _mm_dh(dproj, wt, token):
    s = dproj.shape[0]
    tm = 1024

    def body(dp_ref, w_ref, tok_ref, o_ref):
        del tok_ref
        p = pl.program_id(1)
        part = _dot(dp_ref[...], w_ref[...])

        @pl.when(p == 0)
        def _():
            o_ref[...] = part

        @pl.when(p > 0)
        def _():
            o_ref[...] += part

    return pl.pallas_call(
        body, name="mm_dh", grid=(s // tm, IN_W // PAIR_W),
        in_specs=[pl.BlockSpec((tm, PAIR_W), lambda m, p: (m, p)),
                  pl.BlockSpec((PAIR_W, D_MODEL), lambda m, p: (p, 0)),
                  pl.BlockSpec(token.shape, lambda m, p: (0, 0))],
        out_specs=pl.BlockSpec((tm, D_MODEL), lambda m, p: (m, 0)),
        out_shape=SDS((s, D_MODEL), F32), compiler_params=_params(),
    )(dproj, wt, token)


def _norm_bwd(x, dh, dout, norm_w, scale):
    s = x.shape[0]
    tr = 512

    def body(x_ref, dh_ref, do_ref, nw_ref, sc_ref, gx_ref, dsh_ref, dsc_ref, dnw_ref):
        i = pl.program_id(0)

        @pl.when(i == 0)
        def _():
            for r in (dsh_ref, dsc_ref, dnw_ref):
                r[...] = jnp.zeros_like(r)

        def acc_rows(ref, v):
            ref[...] += jnp.broadcast_to(jnp.sum(v, axis=0, keepdims=True), ref.shape)

        xv = x_ref[...]
        dh_v = dh_ref[...]
        r = lax.rsqrt(jnp.mean(xv * xv, axis=-1, keepdims=True) + EPS)
        xn = xv * r
        one_sc = 1.0 + sc_ref[...]
        acc_rows(dsh_ref, dh_v)
        acc_rows(dsc_ref, dh_v * (xn * nw_ref[...]))
        acc_rows(dnw_ref, dh_v * xn * one_sc)
        dxn = dh_v * (nw_ref[...] * one_sc)
        gx_ref[...] = do_ref[...] + r * (dxn - xn * jnp.mean(dxn * xn, axis=-1, keepdims=True))

    blk = pl.BlockSpec((tr, D_MODEL), lambda i: (i, 0))
    vec = pl.BlockSpec((1, D_MODEL), lambda i: (0, 0))
    acc = pl.BlockSpec((8, D_MODEL), lambda i: (0, 0))
    return pl.pallas_call(
        body, name="norm_bwd", grid=(s // tr,), in_specs=[blk, blk, blk, vec, vec],
        out_specs=[blk, acc, acc, acc],
        out_shape=[SDS((s, D_MODEL), F32)] + [SDS((8, D_MODEL), F32)] * 3, compiler_params=_params(),
    )(x, dh, dout, norm_w, scale)


SMALL_ROWS = 8
QN_COL, KN_COL, CB_COL, LOSS_COL = 0, LANE, 2 * LANE, 2 * LANE + CONV_W


def _pack_partials(dsh, dsc, dgate, dnw, dbg, dqw3, dkw3, dcb, dlnw, dlnb, loss_p):
    n3 = len(dqw3)

    def body(*refs):
        dsh_r, dsc_r, dgate_r, dnw_r, dbg_r = refs[:5]
        dq_r, dk_r = refs[5:5 + n3], refs[5 + n3:5 + 2 * n3]
        dcb_r, dlnw_r, dlnb_r, loss_r, o_ref = refs[5 + 2 * n3:]

        def both_heads(rs):
            t = rs[0][0:1, :]
            for r in rs[1:]:
                t = t + r[0:1, :]
            return t + pltpu.roll(t, HEAD_DIM, axis=1)

        o_ref[0:1, :] = dsh_r[0:1, :]
        o_ref[1:2, :] = dsc_r[0:1, :]
        o_ref[2:3, :] = dgate_r[0:1, :]
        o_ref[3:4, :] = dnw_r[0:1, :]
        o_ref[4:5, :] = dbg_r[0:1, 0:D_MODEL]
        o_ref[5:6, :] = dbg_r[0:1, D_MODEL:]
        o_ref[6:7, QN_COL:QN_COL + LANE] = both_heads(dq_r)
        o_ref[6:7, KN_COL:KN_COL + LANE] = both_heads(dk_r)
        o_ref[6:7, CB_COL:CB_COL + CONV_W] = dcb_r[0:1, :]
        o_ref[6:7, LOSS_COL:LOSS_COL + LANE] = loss_r[0:1, :]
        o_ref[6:7, LOSS_COL + LANE:] = jnp.zeros((1, D_MODEL - LOSS_COL - LANE), F32)
        o_ref[7:8, 0:CONV_W] = dlnw_r[0:1, :]
        o_ref[7:8, CONV_W:] = dlnb_r[0:1, :]

    return pl.pallas_call(body, name="pack_partials", out_shape=SDS((SMALL_ROWS, D_MODEL), F32),
                          compiler_params=_params())(dsh, dsc, dgate, dnw, dbg, *dqw3, *dkw3, dcb, dlnw, dlnb, loss_p)


def _adamw_update(g, w, m, v):
    bc1 = 1.0 - ADAM_B1 ** ADAM_STEP
    bc2 = 1.0 - ADAM_B2 ** ADAM_STEP
    m_new = ADAM_B1 * m + (1.0 - ADAM_B1) * g
    v_new = ADAM_B2 * v + (1.0 - ADAM_B2) * (g * g)
    delta = -ADAM_LR * ((m_new / bc1) / (jnp.sqrt(v_new / bc2) + ADAM_EPS) + ADAM_WD * w)
    return delta, m_new, v_new


def _adamw_small(small_all, ws, ms, vs):
    n = len(ws)
    where = [(slice(0, 3), None), (slice(3, 4), None), (slice(4, 6), None), (6, QN_COL), (6, KN_COL), (6, CB_COL),
             (7, 0), (7, CONV_W)]

    def body(*refs):
        g_ref = refs[0]
        w_r, m_r, v_r = refs[1:1 + n], refs[1 + n:1 + 2 * n], refs[1 + 2 * n:1 + 3 * n]
        outs = refs[1 + 3 * n:]
        g_o, d_o, m_o, v_o, loss_o = outs[:n], outs[n:2 * n], outs[2 * n:3 * n], outs[3 * n:4 * n], outs[4 * n]
        gsum = g_ref[0]
        for dev in range(1, N_DEV):
            gsum = gsum + g_ref[dev]
        loss_o[...] = gsum[6:7, LOSS_COL:LOSS_COL + LANE]
        for i, (rows, col) in enumerate(where):
            width = w_r[i].shape[1]
            if col is None:
                g = jnp.concatenate([gsum[r:r + 1, :] for r in range(rows.start, rows.stop)], axis=1)
            else:
                g = gsum[rows:rows + 1, col:col + width]
            delta, m_new, v_new = _adamw_update(g, w_r[i][...], m_r[i][...], v_r[i][...])
            g_o[i][...] = g
            d_o[i][...] = delta
            m_o[i][...] = m_new
            v_o[i][...] = v_new

    shapes = [SDS(w.shape, F32) for w in ws]
    res = pl.pallas_call(body, name="adamw_small", out_shape=shapes * 4 + [SDS((1, LANE), F32)],
                         compiler_params=_params())(small_all, *ws, *ms, *vs)
    return [res[k * n:(k + 1) * n] for k in range(4)], res[4 * n]


def _row_tile(rows):
    if rows <= 128:
        return rows
    return 128 if rows % 128 == 0 else SHARD_W // 4


def _adamw(gsrc, w, m, v, name, stacked):
    rows, cols = w.shape
    tr = _row_tile(rows)
    n_src = len(gsrc) if stacked else 1

    def body(*refs):
        g_refs, (w_ref, m_ref, v_ref, go_ref, d_ref, mo_ref, vo_ref) = refs[:n_src], refs[n_src:]
        if stacked:
            g = None
            for g_ref, (_, slots) in zip(g_refs, gsrc):
                for j in range(slots):
                    t = g_ref[j].astype(F32)
                    g = t if g is None else g + t
        else:
            g = g_refs[0][...]
        delta, m_new, v_new = _adamw_update(g, w_ref[...], m_ref[...], v_ref[...])
        go_ref[...] = g
        d_ref[...] = delta
        mo_ref[...] = m_new
        vo_ref[...] = v_new

    blk = pl.BlockSpec((tr, cols), lambda i: (i, 0))
    if stacked:
        gspecs = [pl.BlockSpec((slots, tr, arr.shape[2]), lambda i: (0, i, 0)) for arr, slots in gsrc]
        gargs = [arr for arr, _ in gsrc]
    else:
        gspecs, gargs = [blk], [gsrc]
    in_specs = gspecs + [blk, blk, blk]
    args = gargs + [w, m, v]
    return pl.pallas_call(
        body, name=name, grid=(rows // tr,), in_specs=in_specs, out_specs=[blk] * 4,
        out_shape=[SDS((rows, cols), F32)] * 4, compiler_params=_params(),
    )(*args)


def kernel(x, c, w_ada, b_ada, norm_w, w_in, b_gate, q_norm_w, k_norm_w, w_attn_proj, conv_w, conv_b, conv_ln_w, conv_ln_b, w_conv_proj, w_out, loss_target, m_w_ada, m_b_ada, m_norm_w, m_w_in, m_b_gate, m_q_norm_w, m_k_norm_w, m_w_attn_proj, m_conv_w, m_conv_b, m_conv_ln_w, m_conv_ln_b, m_w_conv_proj, m_w_out, v_w_ada, v_b_ada, v_norm_w, v_w_in, v_b_gate, v_q_norm_w, v_k_norm_w, v_w_attn_proj, v_conv_w, v_conv_b, v_conv_ln_w, v_conv_ln_b, v_w_conv_proj, v_w_out):
    xi, yi, ci = lax.axis_index("x"), lax.axis_index("y"), lax.axis_index("c")
    me = 4 * xi + 2 * yi + ci
    x2, tgt2 = x[0], loss_target[0]
    w_in_t, m_w_in_t, v_w_in_t = (jnp.transpose(a[0]) for a in (w_in, m_w_in, v_w_in))
    s = x2.shape[0]

    cw_flat = jnp.pad(conv_w[0].reshape(1, -1), ((0, 0), (0, CONVW_FLAT - CONV_K * HEAD_DIM)))
    pre = jnp.concatenate([c, cw_flat], axis=1).reshape(8, -1)
    (pre_all,) = _all_gather([pre], "gather_c_convw", vmem=True)
    pre_all = pre_all.reshape(N_DEV, -1)
    c_all = pre_all[:, :D_MODEL]
    convw_full = pre_all[:, D_MODEL:D_MODEL + CONV_K * HEAD_DIM].reshape(N_DEV, CONV_K, HEAD_DIM)
    convw_full = jnp.transpose(convw_full, (1, 0, 2)).reshape(CONV_K, CONV_W)
    convw_pad = jnp.pad(convw_full, ((0, CONV_HALO - CONV_K), (0, 0)))

    ada_part = _ada_fwd(c_all, w_ada[0])
    (ada_all,) = _all_gather([ada_part], "gather_ada", vmem=True)
    ada = lax.dynamic_index_in_dim(ada_all, me, axis=1, keepdims=False).reshape(1, 3 * D_MODEL) + b_ada
    shift, scale, gate = ada[:, :D_MODEL], ada[:, D_MODEL:2 * D_MODEL], ada[:, 2 * D_MODEL:]

    wt_g, wa_g, wc_g, wo_g = _all_gather_chips(
        [_cast_bf16(w_in_t, "cast_win"), _cast_bf16(w_attn_proj[0], "cast_wa"), _cast_bf16(w_conv_proj[0], "cast_wc"),
         _cast_bf16(w_out[0], "cast_wo")], "gather_weights")
    wt = wt_g.reshape(IN_W, D_MODEL)
    wa = _cols_from_slots(wa_g, "cols_wa")
    wc = _cols_from_slots(wc_g, "cols_wc")
    wo = wo_g.reshape(D_MODEL, D_MODEL)

    h, ht = _norm_fwd(x2, norm_w, scale, shift)
    proj = _mm_in(h, wt)
    qw2 = jnp.tile(q_norm_w, (1, 2))
    kw2 = jnp.tile(k_norm_w, (1, 2))
    o3, l3 = [], []
    for g in range(N_GROUPS):
        o_g, l_g = _attn_fwd(proj, qw2, kw2, g)
        o3.append(o_g)
        l3.append(l_g)
    head_id = jnp.arange(ATTN_W) // HEAD_DIM
    bd = (head_id[:, None] == head_id[None, :]).astype(BF16)
    (dout, da, delta, lse, dcv, mt, yat, yct, dmo, dya, dyc, dproj,
     dgate, dbg, dlnw, dlnb, dcb, loss_p) = _tail(
        x2, tgt2, proj, o3, l3, wa, wc, wo, gate, b_gate[:, :D_MODEL], b_gate[:, D_MODEL:], convw_pad,
        conv_b, conv_ln_w, conv_ln_b, bd)

    dproj, dconvw8 = _conv_bwd(dcv, proj, convw_pad, dproj)
    dconvw = jnp.sum(dconvw8.reshape(CONV_HALO, 8, CONV_W), axis=1)
    dqw_g3, dkw_g3 = [], []
    for g in range(N_GROUPS):
        dproj, dqw_g, dkw_g = _attn_bwd(proj, da, delta, lse, qw2, kw2, dproj, g)
        dqw_g3.append(dqw_g)
        dkw_g3.append(dkw_g)
    dw_in_p = _mm_dw(ht, dproj).reshape(N_DEV, SHARD_W, D_MODEL)
    dwo_p = _mm_acc(mt, dmo, "mm_dwo", col_slots=False).reshape(N_DEV, D_MODEL // N_DEV, D_MODEL)
    dwa_p = _mm_acc(yat, dya, "mm_dwa", col_slots=True)
    dwc_p = _mm_acc(yct, dyc, "mm_dwc", col_slots=True)

    partials = [dw_in_p, dwa_p, dwc_p, dwo_p]
    me_arr = jnp.reshape(me, (1,)).astype(jnp.int32)
    from_sib = _exchange_sibling(partials, "exchange_sibling")
    presums = [_presum(p, f, me_arr, f"presum{i}") for i, (p, f) in enumerate(zip(partials, from_sib))]
    s_sems, r_sems, pre_thru, land_thru, token = _exchange_chips_start(presums, "exchange_chips_start")
    dh = _mm_dh(dproj, wt, token)
    gx, dsh, dsc, dnw = _norm_bwd(x2, dh, dout, norm_w, scale)
    small_p = _pack_partials(dsh, dsc, dgate, dnw, dbg, dqw_g3, dkw_g3, dcb, dlnw, dlnb, loss_p)
    small_all, dconvw_all = _all_gather([small_p, dconvw], "gather_small", vmem=True)

    small_w = (b_ada, norm_w, b_gate, q_norm_w, k_norm_w, conv_b, conv_ln_w, conv_ln_b)
    small_m = (m_b_ada, m_norm_w, m_b_gate, m_q_norm_w, m_k_norm_w, m_conv_b, m_conv_ln_w, m_conv_ln_b)
    small_v = (v_b_ada, v_norm_w, v_b_gate, v_q_norm_w, v_k_norm_w, v_conv_b, v_conv_ln_w, v_conv_ln_b)
    r_small, loss_row = _adamw_small(small_all, small_w, small_m, small_v)
    dcw_mine = lax.dynamic_slice_in_dim(dconvw_all[:, :CONV_K, :], me * HEAD_DIM, HEAD_DIM, axis=2)
    r_convw = _adamw([(dcw_mine, N_DEV)], conv_w[0], m_conv_w[0], v_conv_w[0], "adamw_conv_w", stacked=True)

    d_ada_all = small_all[:, 0:3, :].reshape(N_DEV, 3 * D_MODEL)
    d_ada_cols = lax.dynamic_slice_in_dim(d_ada_all, me * (3 * D_MODEL // N_DEV), 3 * D_MODEL // N_DEV, axis=1)
    g_wada = _ada_bwd(c_all, d_ada_cols)
    r_ada = _adamw(g_wada, w_ada[0], m_w_ada[0], v_w_ada[0], "adamw_w_ada", stacked=False)
    pres, lands = _exchange_chips_wait(s_sems, r_sems, pre_thru, land_thru, r_ada[1], "exchange_chips_wait")
    terms = [[(p, 1), (l, len(CHIP_K))] for p, l in zip(pres, lands)]
    r_win = [jnp.transpose(r) for r in _adamw(terms[0], w_in_t, m_w_in_t, v_w_in_t, "adamw_w_in", stacked=True)]
    r_wap = _adamw(terms[1], w_attn_proj[0], m_w_attn_proj[0], v_w_attn_proj[0], "adamw_w_attn_proj", stacked=True)
    r_wcp = _adamw(terms[2], w_conv_proj[0], m_w_conv_proj[0], v_w_conv_proj[0], "adamw_w_conv_proj", stacked=True)
    r_wout = _adamw(terms[3], w_out[0], m_w_out[0], v_w_out[0], "adamw_w_out", stacked=True)

    outs = [loss_row[0, 0], gx[None]]
    for k in range(4):
        b_ada_k, norm_w_k, b_gate_k, qn_k, kn_k, conv_b_k, ln_w_k, ln_b_k = r_small[k]
        outs += [r_ada[k][None], b_ada_k, norm_w_k, r_win[k][None], b_gate_k, qn_k, kn_k, r_wap[k][None],
                 r_convw[k][None], conv_b_k, ln_w_k, ln_b_k, r_wcp[k][None], r_wout[k][None]]
    return tuple(outs)
```

```python
import functools

import jax
import jax.numpy as jnp
from jax import lax
from jax.experimental import pallas as pl
from jax.experimental.pallas import tpu as pltpu

F32 = jnp.float32
BF16 = jnp.bfloat16
SDS = jax.ShapeDtypeStruct
MESH = pl.DeviceIdType.MESH

N_DEV = 8
D_MODEL = 1024
HEAD_DIM = 64
N_GROUPS = 3
DILATIONS = (1, 4, 16)
BAND = 128
BWD_UNROLL = 8
ATTN_W = 512
CONV_W = 512
CONV_K = 31
CONV_HALO = 32
IN_W = 8704
SHARD_W = IN_W // N_DEV
PAIR_W = 2 * SHARD_W
Q0, K0, V0, ZA0, U0, ZC0, G0 = 0, 1536, 3072, 4608, 5120, 6144, 6656
EPS = 1e-6
LANE = 128
VMEM_LIMIT = 56 * 1024 * 1024

ADAM_LR, ADAM_B1, ADAM_B2, ADAM_EPS, ADAM_WD, ADAM_STEP = 0.001, 0.9, 0.999, 1e-08, 0.01, 10

CONVW_FLAT = 2048


def _params(**kw):
    return pltpu.CompilerParams(vmem_limit_bytes=VMEM_LIMIT, **kw)


def _sigmoid(z):
    return 0.5 * jnp.tanh(0.5 * z) + 0.5


def _dot(a, b):
    return jnp.dot(a, b, preferred_element_type=F32)


def _dot_nt(a, b):
    return lax.dot_general(a, b, (((1,), (1,)), ((), ())), preferred_element_type=F32)


def _dot_tn(a, b):
    return lax.dot_general(a, b, (((0,), (0,)), ((), ())), preferred_element_type=F32)


def _peer(x, y, c, k):
    px = 1 - x if (k >> 2) & 1 else x
    py = 1 - y if (k >> 1) & 1 else y
    pc = 1 - c if k & 1 else c
    return (px, py, pc), 4 * px + 2 * py + pc


def _all_gather(arrays, name, vmem):
    n = len(arrays)
    space = pltpu.VMEM if vmem else pl.ANY

    def body(*refs):
        ins, outs = refs[:n], refs[n:2 * n]
        send_sems, recv_sems, local_sems = refs[2 * n:]
        x, y, c = lax.axis_index("x"), lax.axis_index("y"), lax.axis_index("c")
        me = 4 * x + 2 * y + c
        locals_ = [pltpu.make_async_copy(ins[a], outs[a].at[me], local_sems.at[a]) for a in range(n)]
        for cp in locals_:
            cp.start()
        sends = []
        for k in range(1, N_DEV):
            peer, _ = _peer(x, y, c, k)
            for a in range(n):
                cp = pltpu.make_async_remote_copy(
                    src_ref=ins[a], dst_ref=outs[a].at[me], send_sem=send_sems.at[a, k - 1],
                    recv_sem=recv_sems.at[a, k - 1], device_id=peer, device_id_type=MESH)
                cp.start()
                sends.append(cp)
        for k in range(1, N_DEV):
            peer, pidx = _peer(x, y, c, k)
            for a in range(n):
                pltpu.make_async_remote_copy(
                    src_ref=ins[a], dst_ref=outs[a].at[pidx], send_sem=send_sems.at[a, k - 1],
                    recv_sem=recv_sems.at[a, k - 1], device_id=peer, device_id_type=MESH).wait_recv()
        for cp in sends:
            cp.wait_send()
        for cp in locals_:
            cp.wait()

    return pl.pallas_call(
        body, name=name,
        out_shape=[SDS((N_DEV,) + a.shape, a.dtype) for a in arrays],
        in_specs=[pl.BlockSpec(memory_space=space)] * n,
        out_specs=[pl.BlockSpec(memory_space=space)] * n,
        scratch_shapes=[pltpu.SemaphoreType.DMA((n, N_DEV - 1)), pltpu.SemaphoreType.DMA((n, N_DEV - 1)),
                        pltpu.SemaphoreType.DMA((n,))],
        compiler_params=_params(),
    )(*arrays)


CHIP_K = (2, 4, 6)


def _all_gather_chips(arrays, name):
    n = len(arrays)

    def body(*refs):
        ins, outs = refs[:n], refs[n:2 * n]
        send_sems, recv_sems, local_sems = refs[2 * n:]
        x, y, c = lax.axis_index("x"), lax.axis_index("y"), lax.axis_index("c")
        me = 4 * x + 2 * y + c
        sib, sib_idx = _peer(x, y, c, 1)

        def copy(a, slot, block, to, src=None):
            return pltpu.make_async_remote_copy(
                src_ref=outs[a].at[block] if src is None else src, dst_ref=outs[a].at[block],
                send_sem=send_sems.at[a, slot], recv_sem=recv_sems.at[a, slot], device_id=to, device_id_type=MESH)

        locals_ = [pltpu.make_async_copy(ins[a], outs[a].at[me], local_sems.at[a]) for a in range(n)]
        for cp in locals_:
            cp.start()
        sends = [copy(a, 0, me, sib, src=ins[a]) for a in range(n)]
        for j, k in enumerate(CHIP_K):
            peer, _ = _peer(x, y, c, k)
            sends += [copy(a, 1 + j, me, peer, src=ins[a]) for a in range(n)]
        for cp in sends:
            cp.start()
        for j, k in enumerate(CHIP_K):
            peer, pidx = _peer(x, y, c, k)
            for a in range(n):
                copy(a, 1 + j, pidx, peer).wait_recv()
                fwd = copy(a, 4 + j, pidx, sib)
                fwd.start()
                sends.append(fwd)
        for a in range(n):
            copy(a, 0, sib_idx, sib).wait_recv()
        for j, k in enumerate(CHIP_K):
            _, pidx = _peer(x, y, 1 - c, k)
            for a in range(n):
                copy(a, 4 + j, pidx, sib).wait_recv()
        for cp in sends:
            cp.wait_send()
        for cp in locals_:
            cp.wait()

    return pl.pallas_call(
        body, name=name,
        out_shape=[SDS((N_DEV,) + a.shape, a.dtype) for a in arrays],
        in_specs=[pl.BlockSpec(memory_space=pl.ANY)] * n,
        out_specs=[pl.BlockSpec(memory_space=pl.ANY)] * n,
        scratch_shapes=[pltpu.SemaphoreType.DMA((n, N_DEV - 1)), pltpu.SemaphoreType.DMA((n, N_DEV - 1)),
                        pltpu.SemaphoreType.DMA((n,))],
        compiler_params=_params(),
    )(*arrays)


N_SLOT = 1 + len(CHIP_K)


def _gather_copy(outs, send_sems, recv_sems, a, slot, block, to, src=None):
    return pltpu.make_async_remote_copy(
        src_ref=outs[a].at[block] if src is None else src, dst_ref=outs[a].at[block],
        send_sem=send_sems.at[a * N_SLOT + slot], recv_sem=recv_sems.at[a * N_SLOT + slot],
        device_id=to, device_id_type=MESH)


def _gather_start(arrays, name):
    n = len(arrays)

    def body(*refs):
        ins, outs = refs[:n], refs[n:2 * n]
        send_sems, recv_sems = refs[2 * n], refs[2 * n + 1]
        token = refs[-1]
        x, y, c = lax.axis_index("x"), lax.axis_index("y"), lax.axis_index("c")
        me = 4 * x + 2 * y + c
        for slot, k in enumerate((1,) + CHIP_K):
            peer, _ = _peer(x, y, c, k)
            for a in range(n):
                _gather_copy(outs, send_sems, recv_sems, a, slot, me, peer, src=ins[a]).start()
        token[...] = jnp.zeros_like(token)

    hbm_in = [pltpu.HBM(a.shape, a.dtype) for a in arrays]
    hbm_out = [pltpu.HBM((N_DEV,) + a.shape, a.dtype) for a in arrays]
    res = pl.pallas_call(
        body, name=name,
        out_shape=(pltpu.SemaphoreType.DMA((n * N_SLOT,)), pltpu.SemaphoreType.DMA((n * N_SLOT,)), *hbm_in, *hbm_out,
                   SDS((8, LANE), F32)),
        in_specs=[HBM_SPEC] * (2 * n),
        out_specs=(SEM_SPEC, SEM_SPEC, *([HBM_SPEC] * (2 * n)), pl.BlockSpec(memory_space=pltpu.VMEM)),
        input_output_aliases={i: 2 + i for i in range(2 * n)},
        compiler_params=pltpu.CompilerParams(has_side_effects=SIDE_EFFECT),
    )(*[pltpu.with_memory_space_constraint(a, pltpu.HBM) for a in arrays],
      *[pltpu.with_memory_space_constraint(lax.empty((N_DEV,) + a.shape, a.dtype), pltpu.HBM) for a in arrays])
    return res[0], res[1], res[2:2 + n], res[2 + n:2 + 2 * n], res[-1]


def _gather_arrivals(send_sems, recv_sems, ins_thru, outs_thru, after, name):
    n = len(ins_thru)

    def body(*refs):
        outs = refs[n:2 * n]
        s_sems, r_sems = refs[2 * n], refs[2 * n + 1]
        x, y, c = lax.axis_index("x"), lax.axis_index("y"), lax.axis_index("c")
        for j, k in enumerate(CHIP_K):
            peer, pidx = _peer(x, y, c, k)
            for a in range(n):
                _gather_copy(outs, s_sems, r_sems, a, 1 + j, pidx, peer).wait_recv()

    hbm = [pltpu.HBM(a.shape, a.dtype) for a in (*ins_thru, *outs_thru)]
    res = pl.pallas_call(
        body, name=name, out_shape=tuple(hbm),
        in_specs=[HBM_SPEC] * (2 * n) + [SEM_SPEC, SEM_SPEC, pl.BlockSpec(memory_space=pl.ANY)],
        out_specs=tuple([HBM_SPEC] * (2 * n)),
        input_output_aliases={i: i for i in range(2 * n)},
        compiler_params=pltpu.CompilerParams(has_side_effects=SIDE_EFFECT),
    )(*ins_thru, *outs_thru, send_sems, recv_sems, after)
    return res[:n], res[n:]


def _gather_pass_on(outs_thru, name):
    n = len(outs_thru)
    nk = len(CHIP_K)

    def body(*refs):
        outs = refs[:n]
        s_sems, r_sems = refs[n], refs[n + 1]
        x, y, c = lax.axis_index("x"), lax.axis_index("y"), lax.axis_index("c")
        sib, _ = _peer(x, y, c, 1)
        for j, k in enumerate(CHIP_K):
            _, pidx = _peer(x, y, c, k)
            for a in range(n):
                pltpu.make_async_remote_copy(
                    src_ref=outs[a].at[pidx], dst_ref=outs[a].at[pidx], send_sem=s_sems.at[a * nk + j],
                    recv_sem=r_sems.at[a * nk + j], device_id=sib, device_id_type=MESH).start()

    hbm = [pltpu.HBM(a.shape, a.dtype) for a in outs_thru]
    res = pl.pallas_call(
        body, name=name,
        out_shape=(pltpu.SemaphoreType.DMA((n * nk,)), pltpu.SemaphoreType.DMA((n * nk,)), *hbm),
        in_specs=[HBM_SPEC] * n,
        out_specs=(SEM_SPEC, SEM_SPEC, *([HBM_SPEC] * n)),
        input_output_aliases={i: 2 + i for i in range(n)},
        compiler_params=pltpu.CompilerParams(has_side_effects=SIDE_EFFECT),
    )(*outs_thru)
    return res[0], res[1], res[2:]


def _gather_wait(send_sems, recv_sems, fwd_send, fwd_recv, ins_thru, outs_thru, name):
    n = len(ins_thru)
    nk = len(CHIP_K)

    def body(*refs):
        ins, outs = refs[:n], refs[n:2 * n]
        s_sems, r_sems, fs_sems, fr_sems = refs[2 * n:2 * n + 4]
        local_sems = refs[-1]
        x, y, c = lax.axis_index("x"), lax.axis_index("y"), lax.axis_index("c")
        me = 4 * x + 2 * y + c
        sib, sib_idx = _peer(x, y, c, 1)
        locals_ = [pltpu.make_async_copy(ins[a], outs[a].at[me], local_sems.at[a]) for a in range(n)]
        for cp in locals_:
            cp.start()
        for a in range(n):
            _gather_copy(outs, s_sems, r_sems, a, 0, sib_idx, sib).wait_recv()
        for slot in range(1 + nk):
            for a in range(n):
                _gather_copy(outs, s_sems, r_sems, a, slot, me, sib, src=ins[a]).wait_send()
        for j in range(nk):
            for a in range(n):
                fwd = pltpu.make_async_remote_copy(
                    src_ref=outs[a].at[me], dst_ref=outs[a].at[me], send_sem=fs_sems.at[a * nk + j],
                    recv_sem=fr_sems.at[a * nk + j], device_id=sib, device_id_type=MESH)
                fwd.wait_send()
                fwd.wait_recv()
        for cp in locals_:
            cp.wait()

    hbm = [pltpu.HBM(a.shape, a.dtype) for a in (*ins_thru, *outs_thru)]
    res = pl.pallas_call(
        body, name=name, out_shape=tuple(hbm),
        in_specs=[HBM_SPEC] * (2 * n) + [SEM_SPEC] * 4,
        out_specs=tuple([HBM_SPEC] * (2 * n)),
        input_output_aliases={i: i for i in range(2 * n)},
        scratch_shapes=[pltpu.SemaphoreType.DMA((n,))],
        compiler_params=pltpu.CompilerParams(has_side_effects=SIDE_EFFECT),
    )(*ins_thru, *outs_thru, send_sems, recv_sems, fwd_send, fwd_recv)
    return res[n:]


def _exchange_sibling(arrays, name):
    n = len(arrays)
    ks = (0,) + CHIP_K

    def body(*refs):
        ins, outs = refs[:n], refs[n:2 * n]
        send_sems, recv_sems = refs[2 * n:]
        x, y, c = lax.axis_index("x"), lax.axis_index("y"), lax.axis_index("c")
        sib, sib_idx = _peer(x, y, c, 1)
        sends = []
        for i, k in enumerate(ks):
            _, tgt = _peer(x, y, 1 - c, k) if k else (None, sib_idx)
            for a in range(n):
                cp = pltpu.make_async_remote_copy(
                    src_ref=ins[a].at[tgt], dst_ref=outs[a].at[i], send_sem=send_sems.at[a, i],
                    recv_sem=recv_sems.at[a, i], device_id=sib, device_id_type=MESH)
                cp.start()
                sends.append(cp)
        for cp in sends:
            cp.wait_recv()
        for cp in sends:
            cp.wait_send()

    return pl.pallas_call(
        body, name=name,
        out_shape=[SDS((len(ks),) + a.shape[1:], a.dtype) for a in arrays],
        in_specs=[pl.BlockSpec(memory_space=pl.ANY)] * n,
        out_specs=[pl.BlockSpec(memory_space=pl.ANY)] * n,
        scratch_shapes=[pltpu.SemaphoreType.DMA((n, len(ks))), pltpu.SemaphoreType.DMA((n, len(ks)))],
        compiler_params=_params(),
    )(*arrays)


def _presum(mine, from_sib, me_arr, name):
    _, rows, cols = mine.shape
    tr = _row_tile(rows)
    ns = 1 + len(CHIP_K)

    def body(me_ref, a_ref, b_ref, o_ref):
        del me_ref
        o_ref[...] = (a_ref[...].astype(F32) + b_ref[...].astype(F32)).astype(o_ref.dtype)

    grid_spec = pltpu.PrefetchScalarGridSpec(
        num_scalar_prefetch=1, grid=(ns, rows // tr),
        in_specs=[pl.BlockSpec((1, tr, cols), lambda j, i, me: (jnp.bitwise_xor(me[0], 2 * j), i, 0)),
                  pl.BlockSpec((1, tr, cols), lambda j, i, me: (j, i, 0))],
        out_specs=pl.BlockSpec((1, tr, cols), lambda j, i, me: (j, i, 0)))
    return pl.pallas_call(body, name=name, grid_spec=grid_spec, out_shape=SDS((ns, rows, cols), mine.dtype),
                          compiler_params=_params())(me_arr, mine, from_sib)


HBM_SPEC = pl.BlockSpec(memory_space=pltpu.HBM)
SEM_SPEC = pl.BlockSpec(memory_space=pltpu.SEMAPHORE)
SIDE_EFFECT = pltpu.SideEffectType.DATAFLOW_SIDE_EFFECTING


def _chips_copies(pre_refs, land_refs, send_sems, recv_sems):
    x, y, c = lax.axis_index("x"), lax.axis_index("y"), lax.axis_index("c")
    copies = []
    for j, k in enumerate(CHIP_K):
        peer, _ = _peer(x, y, c, k)
        for a in range(len(pre_refs)):
            copies.append(pltpu.make_async_remote_copy(
                src_ref=pre_refs[a].at[1 + j], dst_ref=land_refs[a].at[j], send_sem=send_sems.at[a * len(CHIP_K) + j],
                recv_sem=recv_sems.at[a * len(CHIP_K) + j], device_id=peer, device_id_type=MESH))
    return copies


def _exchange_chips_start(presums, name):
    n = len(presums)

    def body(*refs):
        pre, land = refs[:n], refs[n:2 * n]
        send_sems, recv_sems = refs[2 * n], refs[2 * n + 1]
        token = refs[-1]
        for cp in _chips_copies(pre, land, send_sems, recv_sems):
            cp.start()
        token[...] = jnp.zeros_like(token)

    nk = len(CHIP_K)
    hbm = [pltpu.HBM(p.shape, p.dtype) for p in presums]
    hbm_land = [pltpu.HBM((nk,) + p.shape[1:], p.dtype) for p in presums]
    res = pl.pallas_call(
        body, name=name,
        out_shape=(pltpu.SemaphoreType.DMA((n * nk,)), pltpu.SemaphoreType.DMA((n * nk,)), *hbm, *hbm_land, SDS((8, LANE), F32)),
        in_specs=[HBM_SPEC] * (2 * n),
        out_specs=(SEM_SPEC, SEM_SPEC, *([HBM_SPEC] * (2 * n)), pl.BlockSpec(memory_space=pltpu.VMEM)),
        input_output_aliases={i: 2 + i for i in range(2 * n)},
        compiler_params=pltpu.CompilerParams(has_side_effects=SIDE_EFFECT),
    )(*[pltpu.with_memory_space_constraint(p, pltpu.HBM) for p in presums],
      *[pltpu.with_memory_space_constraint(lax.empty((nk,) + p.shape[1:], p.dtype), pltpu.HBM) for p in presums])
    return res[0], res[1], res[2:2 + n], res[2 + n:2 + 2 * n], res[-1]


def _exchange_chips_wait(send_sems, recv_sems, pre_thru, land_thru, after, name):
    n = len(pre_thru)

    def body(*refs):
        pre, land = refs[:n], refs[n:2 * n]
        s_sems, r_sems = refs[2 * n], refs[2 * n + 1]
        for cp in _chips_copies(pre, land, s_sems, r_sems):
            cp.wait_send()
            cp.wait_recv()

    hbm = [pltpu.HBM(p.shape, p.dtype) for p in (*pre_thru, *land_thru)]
    res = pl.pallas_call(
        body, name=name, out_shape=tuple(hbm),
        in_specs=[HBM_SPEC] * (2 * n) + [SEM_SPEC, SEM_SPEC, pl.BlockSpec(memory_space=pl.ANY)],
        out_specs=tuple([HBM_SPEC] * (2 * n)),
        input_output_aliases={i: i for i in range(2 * n)},
        compiler_params=pltpu.CompilerParams(has_side_effects=SIDE_EFFECT),
    )(*pre_thru, *land_thru, send_sems, recv_sems, after)
    return res[:n], res[n:]


def _exchange_chips(presums, name):
    n = len(presums)
    nk = len(CHIP_K)

    def body(*refs):
        pre, land = refs[:n], refs[n:2 * n]
        send_sems, recv_sems = refs[2 * n:]
        copies = _chips_copies(pre, land, send_sems, recv_sems)
        for cp in copies:
            cp.start()
        for cp in copies:
            cp.wait_recv()
        for cp in copies:
            cp.wait_send()

    return pl.pallas_call(
        body, name=name,
        out_shape=[SDS((nk,) + p.shape[1:], p.dtype) for p in presums],
        in_specs=[pl.BlockSpec(memory_space=pl.ANY)] * n,
        out_specs=[pl.BlockSpec(memory_space=pl.ANY)] * n,
        scratch_shapes=[pltpu.SemaphoreType.DMA((n * nk,)), pltpu.SemaphoreType.DMA((n * nk,))],
        compiler_params=_params(),
    )(*presums)


def _cast_bf16(w, name):
    def body(w_ref, o_ref):
        o_ref[...] = w_ref[...].astype(BF16)

    return pl.pallas_call(body, name=name, out_shape=SDS(w.shape, BF16), compiler_params=_params())(w)


def _cols_from_slots(wg, name):
    _, rows, cols = wg.shape

    def body(w_ref, o_ref):
        for j in range(N_DEV):
            o_ref[:, j * cols:(j + 1) * cols] = w_ref[j]

    return pl.pallas_call(body, name=name, out_shape=SDS((rows, N_DEV * cols), wg.dtype), compiler_params=_params())(wg)


def _ada_fwd(c_all, w_ada):
    def body(c_ref, w_ref, o_ref):
        cv = c_ref[...]
        sc = (cv * _sigmoid(cv)).astype(BF16)
        o_ref[...] = _dot(sc, w_ref[...].astype(BF16))

    return pl.pallas_call(body, name="ada_fwd", out_shape=SDS((N_DEV, w_ada.shape[1]), F32),
                          compiler_params=_params())(c_all, w_ada)


def _ada_bwd(c_all, d_ada_cols):
    def body(c_ref, d_ref, o_ref):
        cv = c_ref[...]
        sc = (cv * _sigmoid(cv)).astype(BF16)
        o_ref[...] = _dot_tn(sc, d_ref[...].astype(BF16))

    return pl.pallas_call(body, name="ada_bwd", out_shape=SDS((D_MODEL, d_ada_cols.shape[1]), F32),
                          compiler_params=_params())(c_all, d_ada_cols)


def _norm_fwd(x, norm_w, scale, shift):
    s = x.shape[0]
    tr = 512

    def body(x_ref, nw_ref, sc_ref, sh_ref, h_ref, ht_ref):
        xv = x_ref[...]
        r = lax.rsqrt(jnp.mean(xv * xv, axis=-1, keepdims=True) + EPS)
        h = (xv * r * nw_ref[...]) * (1.0 + sc_ref[...]) + sh_ref[...]
        h_ref[...] = h.astype(BF16)
        ht_ref[...] = h.T.astype(BF16)

    vec = pl.BlockSpec((1, D_MODEL), lambda i: (0, 0))
    return pl.pallas_call(
        body, name="norm_fwd", grid=(s // tr,),
        in_specs=[pl.BlockSpec((tr, D_MODEL), lambda i: (i, 0)), vec, vec, vec],
        out_specs=[pl.BlockSpec((tr, D_MODEL), lambda i: (i, 0)), pl.BlockSpec((D_MODEL, tr), lambda i: (0, i))],
        out_shape=[SDS((s, D_MODEL), BF16), SDS((D_MODEL, s), BF16)], compiler_params=_params(),
    )(x, norm_w, scale, shift)


def _mm_in(h, wt):
    s = h.shape[0]
    tm = 512

    def body(h_ref, w_ref, o_ref):
        o_ref[...] = _dot_nt(h_ref[...], w_ref[...])

    return pl.pallas_call(
        body, name="mm_in", grid=(IN_W // PAIR_W, s // tm),
        in_specs=[pl.BlockSpec((tm, D_MODEL), lambda p, m: (m, 0)),
                  pl.BlockSpec((PAIR_W, D_MODEL), lambda p, m: (p, 0))],
        out_specs=pl.BlockSpec((tm, PAIR_W), lambda p, m: (m, p)),
        out_shape=SDS((s, IN_W), F32), compiler_params=_params(),
    )(h, wt)


def _head_ones():
    a = lax.broadcasted_iota(jnp.int32, (LANE, LANE), 0) // HEAD_DIM
    b = lax.broadcasted_iota(jnp.int32, (LANE, LANE), 1) // HEAD_DIM
    return (a == b).astype(BF16)


def _head_sums(t, ones):
    return _dot(t.astype(BF16), ones)


def _band_bias(bias):
    qi = lax.broadcasted_iota(jnp.int32, (2 * BAND, 2 * BAND), 0) % BAND
    kj = lax.broadcasted_iota(jnp.int32, (2 * BAND, 2 * BAND), 1)
    dist = qi + BAND - kj
    valid = (dist >= 0) & (dist <= BAND)
    bias[1] = jnp.where(valid, 0.0, -1e30)
    bias[0] = jnp.where(valid & (kj >= BAND), 0.0, -1e30)


def _token_rows(j, d, chunk, per_r):
    return pl.ds(j // per_r + (j % per_r) * (chunk * d), chunk, stride=d)


def _deinterleave(src_ref, dst_ref, w_ref, ones, d, sub_len, chunk, scale, dst_off):
    per_r = sub_len // chunk

    def step(j, _):
        t = src_ref[_token_rows(j, d, chunk, per_r), :]
        if w_ref is not None:
            ms = _head_sums(t * t, ones) * (1.0 / HEAD_DIM)
            t = t * lax.rsqrt(ms + EPS) * (w_ref[...] * scale)
        dst_ref[pl.ds(pl.multiple_of(dst_off + j * chunk, BAND), chunk), :] = t.astype(dst_ref.dtype)
        return 0
    lax.fori_loop(0, d * per_r, step, 0, unroll=4)


def _attn_fwd(proj, qw2, kw2, g):
    s = proj.shape[0]
    d = DILATIONS[g]
    sub_len = s // d
    nb = sub_len // BAND
    chunk = min(sub_len, 256)

    def body(q_ref, k_ref, v_ref, qw_ref, kw_ref, o_ref, l_ref, qd, kd, vd, od, ld, bias):
        lo = lax.broadcasted_iota(jnp.int32, (1, LANE), 1) < HEAD_DIM
        ones = _head_ones()

        @pl.when(pl.program_id(0) == 0)
        def _():
            _band_bias(bias)

        kd[0:BAND, :] = jnp.zeros((BAND, LANE), BF16)
        vd[0:BAND, :] = jnp.zeros((BAND, LANE), BF16)
        _deinterleave(q_ref, qd, qw_ref, ones, d, sub_len, chunk, HEAD_DIM ** -0.5, 0)
        _deinterleave(k_ref, kd, kw_ref, ones, d, sub_len, chunk, 1.0, BAND)
        _deinterleave(v_ref, vd, None, ones, d, sub_len, chunk, 1.0, BAND)

        def block(t, _):
            base = pl.multiple_of(t * BAND, BAND)
            q = qd[pl.ds(base, BAND), :]
            k2 = kd[pl.ds(base, 2 * BAND), :]
            v2 = vd[pl.ds(base, 2 * BAND), :]
            zero = jnp.zeros_like(q)
            qs = jnp.concatenate([jnp.where(lo, q, zero), jnp.where(lo, zero, q)], axis=0)
            sc = _dot_nt(qs, k2) + bias[jnp.minimum(t % nb, 1)]
            m = jnp.max(sc, axis=-1, keepdims=True)
            p = jnp.exp(sc - m)
            den = jnp.sum(p, axis=-1, keepdims=True)
            u = _dot(p.astype(BF16), v2) * (1.0 / den)
            lse = m + jnp.log(den)
            od[pl.ds(base, BAND), :] = jnp.where(lo, u[:BAND], u[BAND:])
            ld[pl.ds(base, BAND), :] = jnp.where(lo, lse[:BAND], lse[BAND:])
            return 0
        lax.fori_loop(0, s // BAND, block, 0, unroll=16)

        per_r = sub_len // chunk

        def back(j, _):
            src = pl.ds(pl.multiple_of(j * chunk, chunk), chunk)
            dst = _token_rows(j, d, chunk, per_r)
            o_ref[dst, :] = od[src, :]
            l_ref[dst, :] = ld[src, :]
            return 0
        lax.fori_loop(0, d * per_r, back, 0, unroll=2)

    col = lambda off: pl.BlockSpec((s, LANE), lambda hp, off=off: (0, off // LANE + 4 * g + hp))
    vec = pl.BlockSpec((1, LANE), lambda hp: (0, 0))
    out = pl.BlockSpec((s, LANE), lambda hp: (0, hp))
    return pl.pallas_call(
        body, name=f"attn_fwd{g}", grid=(ATTN_W // LANE,),
        in_specs=[col(Q0), col(K0), col(V0), vec, vec], out_specs=[out, out],
        out_shape=[SDS((s, ATTN_W), F32), SDS((s, ATTN_W), F32)],
        scratch_shapes=[pltpu.VMEM((s, LANE), BF16), pltpu.VMEM((s + BAND, LANE), BF16), pltpu.VMEM((s + BAND, LANE), BF16),
                        pltpu.VMEM((s, LANE), F32), pltpu.VMEM((s, LANE), F32),
                        pltpu.VMEM((2, 2 * BAND, 2 * BAND), F32)],
        compiler_params=_params(),
    )(proj, proj, proj, qw2, kw2)


def _attn_bwd(proj, da, delta, lse, qw2, kw2, dproj, g):
    s = proj.shape[0]
    d = DILATIONS[g]
    sub_len = s // d
    nb = sub_len // BAND
    chunk = min(sub_len, 256)

    def body(q_ref, k_ref, v_ref, da_ref, dl_ref, ls_ref, qw_ref, kw_ref, dp_in, dp_out, dqw_ref, dkw_ref,
             qd, kd, vd, dad, dld, lsd, dqd, dkd, dvd, st, stb, bias, wacc, sem):
        del dp_in
        hp = pl.program_id(0)
        lo = lax.broadcasted_iota(jnp.int32, (1, LANE), 1) < HEAD_DIM
        ones = _head_ones()
        per_r = sub_len // chunk

        @pl.when(hp == 0)
        def _():
            _band_bias(bias)

        kd[0:BAND, :] = jnp.zeros((BAND, LANE), BF16)
        vd[0:BAND, :] = jnp.zeros((BAND, LANE), BF16)
        _deinterleave(q_ref, qd, qw_ref, ones, d, sub_len, chunk, HEAD_DIM ** -0.5, 0)
        _deinterleave(k_ref, kd, kw_ref, ones, d, sub_len, chunk, 1.0, BAND)
        _deinterleave(v_ref, vd, None, ones, d, sub_len, chunk, 1.0, BAND)
        _deinterleave(da_ref, dad, None, ones, d, sub_len, chunk, 1.0, 0)
        _deinterleave(dl_ref, dld, None, ones, d, sub_len, chunk, 1.0, 0)
        _deinterleave(ls_ref, lsd, None, ones, d, sub_len, chunk, 1.0, 0)

        def block(t, carry):
            ck, cv = carry
            base = pl.multiple_of(t * BAND, BAND)
            q = qd[pl.ds(base, BAND), :]
            k2 = kd[pl.ds(base, 2 * BAND), :]
            v2 = vd[pl.ds(base, 2 * BAND), :]
            dav = dad[pl.ds(base, BAND), :]
            dlv = dld[pl.ds(base, BAND), :]
            lsv = lsd[pl.ds(base, BAND), :]
            zero = jnp.zeros_like(q)
            qs = jnp.concatenate([jnp.where(lo, q, zero), jnp.where(lo, zero, q)], axis=0)
            das = jnp.concatenate([jnp.where(lo, dav, zero), jnp.where(lo, zero, dav)], axis=0)
            ls_col = jnp.concatenate([lsv[:, 0:1], lsv[:, HEAD_DIM:HEAD_DIM + 1]], axis=0)
            dl_col = jnp.concatenate([dlv[:, 0:1], dlv[:, HEAD_DIM:HEAD_DIM + 1]], axis=0)
            sc = _dot_nt(qs, k2) + bias[jnp.minimum(t % nb, 1)]
            p = jnp.exp(sc - ls_col)
            dp = _dot_nt(das, v2)
            ds = (p * (dp - dl_col)).astype(BF16)
            dv2 = _dot_tn(p.astype(BF16), das)
            dk2 = _dot_tn(ds, qs)
            dvd[pl.ds(base, BAND), :] = cv + dv2[:BAND]
            dkd[pl.ds(base, BAND), :] = ck + dk2[:BAND]
            dq = _dot(ds, k2)
            dqd[pl.ds(base, BAND), :] = jnp.where(lo, dq[:BAND], dq[BAND:])
            return dk2[BAND:], dv2[BAND:]
        def blocks(i, carry):
            for u in range(BWD_UNROLL):
                carry = block(i * BWD_UNROLL + u, carry)
            return carry
        zeros = jnp.zeros((BAND, LANE), F32)
        ck, cv = lax.fori_loop(0, s // (BAND * BWD_UNROLL), blocks, (zeros, zeros))
        dkd[s:s + BAND, :] = ck
        dvd[s:s + BAND, :] = cv

        def store_cols(col0):
            stb[...] = st[...].astype(BF16)
            cp = pltpu.make_async_copy(
                stb, dp_out.at[:, pl.ds(pl.multiple_of(col0 + LANE * (4 * g + hp), LANE), LANE)], sem)
            cp.start()
            cp.wait()

        def norm_back(src_ref, dy_ref, dy_off, w_ref, scale, dw_ref, col0):
            wacc[...] = jnp.zeros_like(wacc)

            def step(j, _):
                tok = _token_rows(j, d, chunk, per_r)
                t = src_ref[tok, :]
                dy = dy_ref[pl.ds(pl.multiple_of(dy_off + j * chunk, BAND), chunk), :]
                rr = lax.rsqrt(_head_sums(t * t, ones) * (1.0 / HEAD_DIM) + EPS)
                nrm = t * rr
                wacc[...] += jnp.sum((dy * nrm).reshape(chunk // 8, 8, LANE), axis=0)
                dn = dy * (w_ref[...] * scale)
                st[tok, :] = rr * (dn - nrm * (_head_sums(dn * nrm, ones) * (1.0 / HEAD_DIM)))
                return 0
            lax.fori_loop(0, d * per_r, step, 0, unroll=4)
            dw_ref[...] += jnp.broadcast_to(jnp.sum(wacc[...], axis=0, keepdims=True) * scale, dw_ref.shape)
            store_cols(col0)

        @pl.when(hp == 0)
        def _():
            dqw_ref[...] = jnp.zeros_like(dqw_ref)
            dkw_ref[...] = jnp.zeros_like(dkw_ref)

        norm_back(q_ref, dqd, 0, qw_ref, HEAD_DIM ** -0.5, dqw_ref, Q0)
        norm_back(k_ref, dkd, BAND, kw_ref, 1.0, dkw_ref, K0)

        def v_back(j, _):
            src = pl.ds(pl.multiple_of(BAND + j * chunk, BAND), chunk)
            st[_token_rows(j, d, chunk, per_r), :] = dvd[src, :]
            return 0
        lax.fori_loop(0, d * per_r, v_back, 0, unroll=2)
        store_cols(V0)

    col = lambda off: pl.BlockSpec((s, LANE), lambda hp, off=off: (0, off // LANE + 4 * g + hp))
    mid = pl.BlockSpec((s, LANE), lambda hp: (0, hp))
    vec = pl.BlockSpec((1, LANE), lambda hp: (0, 0))
    acc = pl.BlockSpec((8, LANE), lambda hp: (0, 0))
    any_ = pl.BlockSpec(memory_space=pl.ANY)
    return pl.pallas_call(
        body, name=f"attn_bwd{g}", grid=(ATTN_W // LANE,),
        in_specs=[col(Q0), col(K0), col(V0), mid, mid, mid, vec, vec, any_],
        out_specs=[any_, acc, acc],
        out_shape=[SDS(dproj.shape, dproj.dtype), SDS((8, LANE), F32), SDS((8, LANE), F32)],
        input_output_aliases={8: 0},
        scratch_shapes=[pltpu.VMEM((s, LANE), BF16), pltpu.VMEM((s + BAND, LANE), BF16), pltpu.VMEM((s + BAND, LANE), BF16),
                        pltpu.VMEM((s, LANE), BF16), pltpu.VMEM((s, LANE), F32), pltpu.VMEM((s, LANE), F32),
                        pltpu.VMEM((s, LANE), F32), pltpu.VMEM((s + BAND, LANE), F32), pltpu.VMEM((s + BAND, LANE), F32),
                        pltpu.VMEM((s, LANE), F32), pltpu.VMEM((s, LANE), BF16),
                        pltpu.VMEM((2, 2 * BAND, 2 * BAND), F32), pltpu.VMEM((8, LANE), F32),
                        pltpu.SemaphoreType.DMA(())],
        compiler_params=_params(),
    )(proj, proj, proj, da, delta, lse, qw2, kw2, dproj)


def _tap_views(ext_ref, sh_ref, offsets, tr, cols):
    for b in range(8):
        group = [j for j, o in enumerate(offsets) if o % 8 == b]
        if not group:
            continue
        first = min(offsets[j] for j in group)
        span = tr + max(offsets[j] for j in group) - first
        sh_ref[0:span, cols] = ext_ref[first:first + span, cols]
        for j in group:
            yield j, sh_ref[offsets[j] - first:offsets[j] - first + tr, cols]


def _silu_grad(z, sg):
    return sg * (1.0 + z * (1.0 - sg))


def _glu(u):
    a_h, b_h = u[:, :CONV_W], u[:, CONV_W:]
    sg = _sigmoid(b_h)
    return a_h, sg, a_h * sg


def _tail(x, tgt, proj, o3, l3, wa, wc, wo, gate, bga, bgc, convw, convb, lnw, lnb, bd):
    s = x.shape[0]
    tr = 256

    def body(x_ref, t_ref, za_ref, u_ref, uh_ref, zc_ref, g0_ref, g1_ref, g2_ref, g3_ref,
             o0_ref, o1_ref, o2_ref, l0_ref, l1_ref, l2_ref, wa_ref, wc_ref, wo_ref,
             gate_ref, bga_ref, bgc_ref, cw_ref, cb_ref, lnw_ref, lnb_ref, bd_ref,
             dout_ref, da_ref, dl_ref, lse_ref, dcv_ref, mt_ref, yat_ref, yct_ref, dmo_ref, dya_ref, dyc_ref, dp_ref,
             dgate_ref, dbg_ref, dlnw_ref, dlnb_ref, dcb_ref, loss_ref,
             ext, sh, st_za, st_zc, st_g, sems):
        i = pl.program_id(0)

        @pl.when(i == 0)
        def _():
            for r in (dgate_ref, dbg_ref, dlnw_ref, dlnb_ref, dcb_ref, loss_ref):
                r[...] = jnp.zeros_like(r)

        def acc_rows(ref, v):
            ref[...] += jnp.broadcast_to(jnp.sum(v, axis=0, keepdims=True), ref.shape)

        la, lb, lc = l0_ref[...], l1_ref[...], l2_ref[...]
        mx = jnp.maximum(jnp.maximum(la, lb), lc)
        ea, eb, ec = jnp.exp(la - mx), jnp.exp(lb - mx), jnp.exp(lc - mx)
        den = ea + eb + ec
        inv = 1.0 / den
        attn = (ea * inv) * o0_ref[...] + (eb * inv) * o1_ref[...] + (ec * inv) * o2_ref[...]
        lse_ref[...] = mx + jnp.log(den)

        za = za_ref[...]
        sga = _sigmoid(za)
        sa = za * sga
        ya_in = attn * sa
        y_attn = _dot(ya_in.astype(BF16), wa_ref[...])

        _, _, glu = _glu(u_ref[...])
        _, _, glu_h = _glu(uh_ref[...])
        ext[0:CONV_HALO, :] = jnp.where(i > 0, glu_h, 0.0)
        ext[CONV_HALO:CONV_HALO + tr, :] = glu
        cv_blocks = []
        for cb in range(CONV_W // LANE):
            cols = slice(cb * LANE, (cb + 1) * LANE)
            cv_c = jnp.broadcast_to(cb_ref[:, cols], (tr, LANE))
            for j, rows in _tap_views(ext, sh, [CONV_HALO - (CONV_K - 1) + j for j in range(CONV_K)], tr, cols):
                cv_c = cv_c + cw_ref[j:j + 1, cols] * rows
            cv_blocks.append(cv_c)
        cv = jnp.concatenate(cv_blocks, axis=1)
        mu = jnp.mean(cv, axis=-1, keepdims=True)
        xc = cv - mu
        rstd = lax.rsqrt(jnp.mean(xc * xc, axis=-1, keepdims=True) + EPS)
        nrm = xc * rstd
        ln = nrm * lnw_ref[...] + lnb_ref[...]
        sgl = _sigmoid(ln)
        cs = ln * sgl
        zc = zc_ref[...]
        sgc = _sigmoid(zc)
        scz = zc * sgc
        yc_in = cs * scz
        y_conv = _dot(yc_in.astype(BF16), wc_ref[...])

        ga = _sigmoid(jnp.concatenate([g0_ref[...], g1_ref[...]], axis=1) + bga_ref[...])
        gc = _sigmoid(jnp.concatenate([g2_ref[...], g3_ref[...]], axis=1) + bgc_ref[...])
        merged = ga * y_attn + gc * y_conv
        mo = _dot(merged.astype(BF16), wo_ref[...])
        gate_v = gate_ref[...]
        err = (x_ref[...] + gate_v * mo) - t_ref[...]
        loss_ref[...] += 0.5 * jnp.sum(jnp.mean(err * err, axis=-1, keepdims=True))
        d_out = err * (1.0 / D_MODEL)
        dout_ref[...] = d_out

        acc_rows(dgate_ref, d_out * mo)
        dmo_b = (d_out * gate_v).astype(BF16)
        dmo_ref[...] = dmo_b
        mt_ref[...] = merged.T.astype(BF16)
        d_merged = _dot_nt(dmo_b, wo_ref[...])
        d_ya = (d_merged * ga).astype(BF16)
        d_yc = (d_merged * gc).astype(BF16)
        dya_ref[...] = d_ya
        dyc_ref[...] = d_yc
        dga = d_merged * y_attn * (ga * (1.0 - ga))
        dgc = d_merged * y_conv * (gc * (1.0 - gc))
        dgs = jnp.concatenate([dga, dgc], axis=1)
        acc_rows(dbg_ref, dgs)
        st_g[...] = dgs.astype(BF16)

        yat_ref[...] = ya_in.T.astype(BF16)
        d_ya_in = _dot_nt(d_ya, wa_ref[...])
        d_attn = d_ya_in * sa
        da_ref[...] = d_attn
        st_za[...] = (d_ya_in * attn * _silu_grad(za, sga)).astype(BF16)
        prod = d_attn * attn
        hi = prod.astype(BF16)
        lo_ = (prod - hi.astype(F32)).astype(BF16)
        dl_ref[...] = _dot(hi, bd_ref[...]) + _dot(lo_, bd_ref[...])

        yct_ref[...] = yc_in.T.astype(BF16)
        d_yc_in = _dot_nt(d_yc, wc_ref[...])
        st_zc[...] = (d_yc_in * cs * _silu_grad(zc, sgc)).astype(BF16)
        d_ln = (d_yc_in * scz) * _silu_grad(ln, sgl)
        acc_rows(dlnw_ref, d_ln * nrm)
        acc_rows(dlnb_ref, d_ln)
        d_nrm = d_ln * lnw_ref[...]
        d_cv = rstd * (d_nrm - jnp.mean(d_nrm, axis=-1, keepdims=True)
                       - nrm * jnp.mean(d_nrm * nrm, axis=-1, keepdims=True))
        acc_rows(dcb_ref, d_cv)
        dcv_ref[...] = d_cv

        rows = pl.ds(pl.multiple_of(i * tr, tr), tr)
        cps = [pltpu.make_async_copy(st_za, dp_ref.at[rows, pl.ds(ZA0, ATTN_W)], sems.at[0]),
               pltpu.make_async_copy(st_zc, dp_ref.at[rows, pl.ds(ZC0, CONV_W)], sems.at[1]),
               pltpu.make_async_copy(st_g, dp_ref.at[rows, pl.ds(G0, 2 * D_MODEL)], sems.at[2])]
        for cp in cps:
            cp.start()
        for cp in cps:
            cp.wait()

    def rows(width, colblk=0):
        return pl.BlockSpec((tr, width), lambda i, colblk=colblk: (i, colblk))

    def const(shape):
        return pl.BlockSpec(shape, lambda i: (0,) * len(shape))

    halo = pl.BlockSpec((CONV_HALO, D_MODEL), lambda i: (jnp.maximum(i * (tr // CONV_HALO) - 1, 0), U0 // D_MODEL))
    in_specs = [rows(D_MODEL), rows(D_MODEL), rows(ATTN_W, ZA0 // ATTN_W), rows(D_MODEL, U0 // D_MODEL), halo,
                rows(CONV_W, ZC0 // CONV_W)]
    in_specs += [rows(512, G0 // 512 + j) for j in range(4)]
    in_specs += [rows(ATTN_W)] * 6
    in_specs += [const(wa.shape), const(wc.shape), const(wo.shape), const((1, D_MODEL)), const((1, D_MODEL)),
                 const((1, D_MODEL)), const(convw.shape), const((1, CONV_W)), const((1, CONV_W)), const((1, CONV_W)),
                 const(bd.shape)]
    tcol = lambda width: pl.BlockSpec((width, tr), lambda i: (0, i))
    out_specs = [rows(D_MODEL), rows(ATTN_W), rows(ATTN_W), rows(ATTN_W), rows(CONV_W),
                 tcol(D_MODEL), tcol(ATTN_W), tcol(CONV_W), rows(D_MODEL), rows(D_MODEL), rows(D_MODEL),
                 pl.BlockSpec(memory_space=pl.ANY),
                 const((8, D_MODEL)), const((8, 2 * D_MODEL)), const((8, CONV_W)), const((8, CONV_W)), const((8, CONV_W)),
                 const((8, LANE))]
    out_shape = [SDS((s, D_MODEL), F32), SDS((s, ATTN_W), F32), SDS((s, ATTN_W), F32), SDS((s, ATTN_W), F32),
                 SDS((s, CONV_W), F32),
                 SDS((D_MODEL, s), BF16), SDS((ATTN_W, s), BF16), SDS((CONV_W, s), BF16),
                 SDS((s, D_MODEL), BF16), SDS((s, D_MODEL), BF16), SDS((s, D_MODEL), BF16),
                 SDS((s, IN_W), BF16),
                 SDS((8, D_MODEL), F32), SDS((8, 2 * D_MODEL), F32), SDS((8, CONV_W), F32), SDS((8, CONV_W), F32),
                 SDS((8, CONV_W), F32), SDS((8, LANE), F32)]
    return pl.pallas_call(
        body, name="tail", grid=(s // tr,), in_specs=in_specs, out_specs=out_specs, out_shape=out_shape,
        scratch_shapes=[pltpu.VMEM((CONV_HALO + tr, CONV_W), F32), pltpu.VMEM((CONV_HALO + tr, CONV_W), F32),
                        pltpu.VMEM((tr, ATTN_W), BF16),
                        pltpu.VMEM((tr, CONV_W), BF16), pltpu.VMEM((tr, 2 * D_MODEL), BF16),
                        pltpu.SemaphoreType.DMA((3,))],
        compiler_params=_params(),
    )(x, tgt, proj, proj, proj, proj, proj, proj, proj, proj, *o3, *l3, wa, wc, wo, gate, bga, bgc,
      convw, convb, lnw, lnb, bd)


def _conv_bwd(dcv, proj, convw, dproj):
    s = dcv.shape[0]
    tr = 128
    nt = s // tr

    def body(dcv_ref, dcvn_ref, u_ref, uh_ref, cw_ref, dp_in, dp_out, dw_ref, extg, extd, sh):
        del dp_in
        i = pl.program_id(0)

        @pl.when(i == 0)
        def _():
            dw_ref[...] = jnp.zeros_like(dw_ref)

        _, _, glu = _glu(u_ref[...])
        _, _, glu_h = _glu(uh_ref[...])
        extg[0:CONV_HALO, :] = jnp.where(i > 0, glu_h, 0.0)
        extg[CONV_HALO:CONV_HALO + tr, :] = glu
        extd[0:tr, :] = dcv_ref[...]
        extd[tr:tr + CONV_HALO, :] = jnp.where(i < nt - 1, dcvn_ref[...], 0.0)
        for cb in range(CONV_W // LANE):
            cols = slice(cb * LANE, (cb + 1) * LANE)
            dglu = jnp.zeros((tr, LANE), F32)
            for j, rows in _tap_views(extd, sh, [CONV_K - 1 - j for j in range(CONV_K)], tr, cols):
                dglu = dglu + cw_ref[j:j + 1, cols] * rows
            dcv_c = dcv_ref[:, cols]
            for j, rows in _tap_views(extg, sh, [CONV_HALO - (CONV_K - 1) + j for j in range(CONV_K)], tr, cols):
                dw_ref[8 * j:8 * j + 8, cols] += jnp.sum((dcv_c * rows).reshape(tr // 8, 8, LANE), axis=0)
            a_h = u_ref[:, cols]
            sgb = _sigmoid(u_ref[:, CONV_W + cb * LANE:CONV_W + (cb + 1) * LANE])
            dp_out[:, cols] = (dglu * sgb).astype(BF16)
            dp_out[:, CONV_W + cb * LANE:CONV_W + (cb + 1) * LANE] = (dglu * a_h * (sgb * (1.0 - sgb))).astype(BF16)

    ucol = U0 // D_MODEL
    return pl.pallas_call(
        body, name="conv_bwd", grid=(nt,),
        in_specs=[pl.BlockSpec((tr, CONV_W), lambda i: (i, 0)),
                  pl.BlockSpec((CONV_HALO, CONV_W), lambda i: (jnp.minimum((i + 1) * (tr // CONV_HALO), s // CONV_HALO - 1), 0)),
                  pl.BlockSpec((tr, D_MODEL), lambda i: (i, ucol)),
                  pl.BlockSpec((CONV_HALO, D_MODEL), lambda i: (jnp.maximum(i * (tr // CONV_HALO) - 1, 0), ucol)),
                  pl.BlockSpec(convw.shape, lambda i: (0, 0)),
                  pl.BlockSpec(memory_space=pl.ANY)],
        out_specs=[pl.BlockSpec((tr, D_MODEL), lambda i: (i, ucol)), pl.BlockSpec((8 * CONV_HALO, CONV_W), lambda i: (0, 0))],
        out_shape=[SDS(dproj.shape, dproj.dtype), SDS((8 * CONV_HALO, CONV_W), F32)],
        input_output_aliases={5: 0},
        scratch_shapes=[pltpu.VMEM((CONV_HALO + tr, CONV_W), F32)] * 3,
        compiler_params=_params(),
    )(dcv, dcv, proj, proj, convw, dproj)


def _mm_acc(at, b, name, col_slots):
    m, s = at.shape
    n = b.shape[1]
    tk = 512
    nk = s // tk

    def body(a_ref, b_ref, o_ref, acc):
        k = pl.program_id(0)

        @pl.when(k == 0)
        def _():
            acc[...] = jnp.zeros_like(acc)

        acc[...] += _dot(a_ref[...], b_ref[...])

        @pl.when(k == nk - 1)
        def _():
            if col_slots:
                w = n // N_DEV
                for j in range(N_DEV):
                    o_ref[j] = acc[:, j * w:(j + 1) * w].astype(BF16)
            else:
                o_ref[...] = acc[...].astype(BF16)

    if col_slots:
        out_shape = SDS((N_DEV, m, n // N_DEV), BF16)
        out_spec = pl.BlockSpec((N_DEV, m, n // N_DEV), lambda k: (0, 0, 0))
    else:
        out_shape = SDS((m, n), BF16)
        out_spec = pl.BlockSpec((m, n), lambda k: (0, 0))
    return pl.pallas_call(
        body, name=name, grid=(nk,),
        in_specs=[pl.BlockSpec((m, tk), lambda k: (0, k)), pl.BlockSpec((tk, n), lambda k: (k, 0))],
        out_specs=out_spec, out_shape=out_shape, scratch_shapes=[pltpu.VMEM((m, n), F32)],
        compiler_params=_params(),
    )(at, b)


def _mm_dw(ht, dproj):
    s = ht.shape[1]
    tk = 512
    nk = s // tk

    def body(a_ref, b_ref, o_ref, acc):
        k = pl.program_id(1)

        @pl.when(k == 0)
        def _():
            acc[...] = jnp.zeros_like(acc)

        acc[...] += _dot(a_ref[...], b_ref[...])

        @pl.when(k == nk - 1)
        def _():
            o_ref[...] = acc[...].T.astype(BF16)

    return pl.pallas_call(
        body, name="mm_dw", grid=(IN_W // PAIR_W, nk),
        in_specs=[pl.BlockSpec((D_MODEL, tk), lambda p, k: (0, k)), pl.BlockSpec((tk, PAIR_W), lambda p, k: (k, p))],
        out_specs=pl.BlockSpec((PAIR_W, D_MODEL), lambda p, k: (p, 0)),
        out_shape=SDS((IN_W, D_MODEL), BF16), scratch_shapes=[pltpu.VMEM((D_MODEL, PAIR_W), F32)],
        compiler_params=_params(),
    )(ht, dproj)


def _mm_dh(dproj, wt, token):
    s = dproj.shape[0]
    tm = 1024

    def body(dp_ref, w_ref, tok_ref, o_ref):
        del tok_ref
        p = pl.program_id(1)
        part = _dot(dp_ref[...], w_ref[...])

        @pl.when(p == 0)
        def _():
            o_ref[...] = part

        @pl.when(p > 0)
        def _():
            o_ref[...] += part

    return pl.pallas_call(
        body, name="mm_dh", grid=(s // tm, IN_W // PAIR_W),
        in_specs=[pl.BlockSpec((tm, PAIR_W), lambda m, p: (m, p)),
                  pl.BlockSpec((PAIR_W, D_MODEL), lambda m, p: (p, 0)),
                  pl.BlockSpec(token.shape, lambda m, p: (0, 0))],
        out_specs=pl.BlockSpec((tm, D_MODEL), lambda m, p: (m, 0)),
        out_shape=SDS((s, D_MODEL), F32), compiler_params=_params(),
    )(dproj, wt, token)


def _norm_bwd(x, dh, dout, norm_w, scale):
    s = x.shape[0]
    tr = 512

    def body(x_ref, dh_ref, do_ref, nw_ref, sc_ref, gx_ref, dsh_ref, dsc_ref, dnw_ref):
        i = pl.program_id(0)

        @pl.when(i == 0)
        def _():
            for r in (dsh_ref, dsc_ref, dnw_ref):
                r[...] = jnp.zeros_like(r)

        def acc_rows(ref, v):
            ref[...] += jnp.broadcast_to(jnp.sum(v, axis=0, keepdims=True), ref.shape)

        xv = x_ref[...]
        dh_v = dh_ref[...]
        r = lax.rsqrt(jnp.mean(xv * xv, axis=-1, keepdims=True) + EPS)
        xn = xv * r
        one_sc = 1.0 + sc_ref[...]
        acc_rows(dsh_ref, dh_v)
        acc_rows(dsc_ref, dh_v * (xn * nw_ref[...]))
        acc_rows(dnw_ref, dh_v * xn * one_sc)
        dxn = dh_v * (nw_ref[...] * one_sc)
        gx_ref[...] = do_ref[...] + r * (dxn - xn * jnp.mean(dxn * xn, axis=-1, keepdims=True))

    blk = pl.BlockSpec((tr, D_MODEL), lambda i: (i, 0))
    vec = pl.BlockSpec((1, D_MODEL), lambda i: (0, 0))
    acc = pl.BlockSpec((8, D_MODEL), lambda i: (0, 0))
    return pl.pallas_call(
        body, name="norm_bwd", grid=(s // tr,), in_specs=[blk, blk, blk, vec, vec],
        out_specs=[blk, acc, acc, acc],
        out_shape=[SDS((s, D_MODEL), F32)] + [SDS((8, D_MODEL), F32)] * 3, compiler_params=_params(),
    )(x, dh, dout, norm_w, scale)


SMALL_ROWS = 8
QN_COL, KN_COL, CB_COL, LOSS_COL = 0, LANE, 2 * LANE, 2 * LANE + CONV_W


def _pack_partials(dsh, dsc, dgate, dnw, dbg, dqw3, dkw3, dcb, dlnw, dlnb, loss_p):
    n3 = len(dqw3)

    def body(*refs):
        dsh_r, dsc_r, dgate_r, dnw_r, dbg_r = refs[:5]
        dq_r, dk_r = refs[5:5 + n3], refs[5 + n3:5 + 2 * n3]
        dcb_r, dlnw_r, dlnb_r, loss_r, o_ref = refs[5 + 2 * n3:]

        def both_heads(rs):
            t = rs[0][0:1, :]
            for r in rs[1:]:
                t = t + r[0:1, :]
            return t + pltpu.roll(t, HEAD_DIM, axis=1)

        o_ref[0:1, :] = dsh_r[0:1, :]
        o_ref[1:2, :] = dsc_r[0:1, :]
        o_ref[2:3, :] = dgate_r[0:1, :]
        o_ref[3:4, :] = dnw_r[0:1, :]
        o_ref[4:5, :] = dbg_r[0:1, 0:D_MODEL]
        o_ref[5:6, :] = dbg_r[0:1, D_MODEL:]
        o_ref[6:7, QN_COL:QN_COL + LANE] = both_heads(dq_r)
        o_ref[6:7, KN_COL:KN_COL + LANE] = both_heads(dk_r)
        o_ref[6:7, CB_COL:CB_COL + CONV_W] = dcb_r[0:1, :]
        o_ref[6:7, LOSS_COL:LOSS_COL + LANE] = loss_r[0:1, :]
        o_ref[6:7, LOSS_COL + LANE:] = jnp.zeros((1, D_MODEL - LOSS_COL - LANE), F32)
        o_ref[7:8, 0:CONV_W] = dlnw_r[0:1, :]
        o_ref[7:8, CONV_W:] = dlnb_r[0:1, :]

    return pl.pallas_call(body, name="pack_partials", out_shape=SDS((SMALL_ROWS, D_MODEL), F32),
                          compiler_params=_params())(dsh, dsc, dgate, dnw, dbg, *dqw3, *dkw3, dcb, dlnw, dlnb, loss_p)


def _adamw_update(g, w, m, v):
    bc1 = 1.0 - ADAM_B1 ** ADAM_STEP
    bc2 = 1.0 - ADAM_B2 ** ADAM_STEP
    m_new = ADAM_B1 * m + (1.0 - ADAM_B1) * g
    v_new = ADAM_B2 * v + (1.0 - ADAM_B2) * (g * g)
    delta = -ADAM_LR * ((m_new / bc1) / (jnp.sqrt(v_new / bc2) + ADAM_EPS) + ADAM_WD * w)
    return delta, m_new, v_new


def _adamw_small(small_all, ws, ms, vs):
    n = len(ws)
    where = [(slice(0, 3), None), (slice(3, 4), None), (slice(4, 6), None), (6, QN_COL), (6, KN_COL), (6, CB_COL),
             (7, 0), (7, CONV_W)]

    def body(*refs):
        g_ref = refs[0]
        w_r, m_r, v_r = refs[1:1 + n], refs[1 + n:1 + 2 * n], refs[1 + 2 * n:1 + 3 * n]
        outs = refs[1 + 3 * n:]
        g_o, d_o, m_o, v_o, loss_o = outs[:n], outs[n:2 * n], outs[2 * n:3 * n], outs[3 * n:4 * n], outs[4 * n]
        gsum = g_ref[0]
        for dev in range(1, N_DEV):
            gsum = gsum + g_ref[dev]
        loss_o[...] = gsum[6:7, LOSS_COL:LOSS_COL + LANE]
        for i, (rows, col) in enumerate(where):
            width = w_r[i].shape[1]
            if col is None:
                g = jnp.concatenate([gsum[r:r + 1, :] for r in range(rows.start, rows.stop)], axis=1)
            else:
                g = gsum[rows:rows + 1, col:col + width]
            delta, m_new, v_new = _adamw_update(g, w_r[i][...], m_r[i][...], v_r[i][...])
            g_o[i][...] = g
            d_o[i][...] = delta
            m_o[i][...] = m_new
            v_o[i][...] = v_new

    shapes = [SDS(w.shape, F32) for w in ws]
    res = pl.pallas_call(body, name="adamw_small", out_shape=shapes * 4 + [SDS((1, LANE), F32)],
                         compiler_params=_params())(small_all, *ws, *ms, *vs)
    return [res[k * n:(k + 1) * n] for k in range(4)], res[4 * n]


def _row_tile(rows):
    if rows <= 128:
        return rows
    return 128 if rows % 128 == 0 else SHARD_W // 4


def _adamw(gsrc, w, m, v, name, stacked):
    rows, cols = w.shape
    tr = _row_tile(rows)
    n_src = len(gsrc) if stacked else 1

    def body(*refs):
        g_refs, (w_ref, m_ref, v_ref, go_ref, d_ref, mo_ref, vo_ref) = refs[:n_src], refs[n_src:]
        if stacked:
            g = None
            for g_ref, (_, slots) in zip(g_refs, gsrc):
                for j in range(slots):
                    t = g_ref[j].astype(F32)
                    g = t if g is None else g + t
        else:
            g = g_refs[0][...]
        delta, m_new, v_new = _adamw_update(g, w_ref[...], m_ref[...], v_ref[...])
        go_ref[...] = g
        d_ref[...] = delta
        mo_ref[...] = m_new
        vo_ref[...] = v_new

    blk = pl.BlockSpec((tr, cols), lambda i: (i, 0))
    if stacked:
        gspecs = [pl.BlockSpec((slots, tr, arr.shape[2]), lambda i: (0, i, 0)) for arr, slots in gsrc]
        gargs = [arr for arr, _ in gsrc]
    else:
        gspecs, gargs = [blk], [gsrc]
    in_specs = gspecs + [blk, blk, blk]
    args = gargs + [w, m, v]
    return pl.pallas_call(
        body, name=name, grid=(rows // tr,), in_specs=in_specs, out_specs=[blk] * 4,
        out_shape=[SDS((rows, cols), F32)] * 4, compiler_params=_params(),
    )(*args)


def kernel(x, c, w_ada, b_ada, norm_w, w_in, b_gate, q_norm_w, k_norm_w, w_attn_proj, conv_w, conv_b, conv_ln_w, conv_ln_b, w_conv_proj, w_out, loss_target, m_w_ada, m_b_ada, m_norm_w, m_w_in, m_b_gate, m_q_norm_w, m_k_norm_w, m_w_attn_proj, m_conv_w, m_conv_b, m_conv_ln_w, m_conv_ln_b, m_w_conv_proj, m_w_out, v_w_ada, v_b_ada, v_norm_w, v_w_in, v_b_gate, v_q_norm_w, v_k_norm_w, v_w_attn_proj, v_conv_w, v_conv_b, v_conv_ln_w, v_conv_ln_b, v_w_conv_proj, v_w_out):
    xi, yi, ci = lax.axis_index("x"), lax.axis_index("y"), lax.axis_index("c")
    me = 4 * xi + 2 * yi + ci
    x2, tgt2 = x[0], loss_target[0]
    w_in_t, m_w_in_t, v_w_in_t = (jnp.transpose(a[0]) for a in (w_in, m_w_in, v_w_in))
    s = x2.shape[0]

    g_send, g_recv, g_ins, g_outs, g_token = _gather_start(
        [_cast_bf16(w_in_t, "cast_win"), _cast_bf16(w_attn_proj[0], "cast_wa"), _cast_bf16(w_conv_proj[0], "cast_wc"),
         _cast_bf16(w_out[0], "cast_wo")], "gather_weights_start")

    cw_flat = jnp.pad(conv_w[0].reshape(1, -1), ((0, 0), (0, CONVW_FLAT - CONV_K * HEAD_DIM)))
    pre = jnp.concatenate([c, cw_flat], axis=1).reshape(8, -1)
    pre_all, _ = _all_gather([pre, g_token], "gather_c_convw", vmem=True)
    pre_all = pre_all.reshape(N_DEV, -1)
    c_all = pre_all[:, :D_MODEL]
    convw_full = pre_all[:, D_MODEL:D_MODEL + CONV_K * HEAD_DIM].reshape(N_DEV, CONV_K, HEAD_DIM)
    convw_full = jnp.transpose(convw_full, (1, 0, 2)).reshape(CONV_K, CONV_W)
    convw_pad = jnp.pad(convw_full, ((0, CONV_HALO - CONV_K), (0, 0)))

    ada_part = _ada_fwd(c_all, w_ada[0])
    (ada_all,) = _all_gather([ada_part], "gather_ada", vmem=True)
    ada = lax.dynamic_index_in_dim(ada_all, me, axis=1, keepdims=False).reshape(1, 3 * D_MODEL) + b_ada
    shift, scale, gate = ada[:, :D_MODEL], ada[:, D_MODEL:2 * D_MODEL], ada[:, 2 * D_MODEL:]

    h, ht = _norm_fwd(x2, norm_w, scale, shift)
    g_ins, g_outs = _gather_arrivals(g_send, g_recv, g_ins, g_outs, h, "gather_weights_arrivals")
    f_send, f_recv, g_outs = _gather_pass_on(g_outs, "gather_weights_pass_on")
    wt_g, wa_g, wc_g, wo_g = _gather_wait(g_send, g_recv, f_send, f_recv, g_ins, g_outs, "gather_weights_wait")
    wt = wt_g.reshape(IN_W, D_MODEL)
    wa = _cols_from_slots(wa_g, "cols_wa")
    wc = _cols_from_slots(wc_g, "cols_wc")
    wo = wo_g.reshape(D_MODEL, D_MODEL)
    proj = _mm_in(h, wt)
    qw2 = jnp.tile(q_norm_w, (1, 2))
    kw2 = jnp.tile(k_norm_w, (1, 2))
    o3, l3 = [], []
    for g in range(N_GROUPS):
        o_g, l_g = _attn_fwd(proj, qw2, kw2, g)
        o3.append(o_g)
        l3.append(l_g)
    head_id = jnp.arange(ATTN_W) // HEAD_DIM
    bd = (head_id[:, None] == head_id[None, :]).astype(BF16)
    (dout, da, delta, lse, dcv, mt, yat, yct, dmo, dya, dyc, dproj,
     dgate, dbg, dlnw, dlnb, dcb, loss_p) = _tail(
        x2, tgt2, proj, o3, l3, wa, wc, wo, gate, b_gate[:, :D_MODEL], b_gate[:, D_MODEL:], convw_pad,
        conv_b, conv_ln_w, conv_ln_b, bd)

    dproj, dconvw8 = _conv_bwd(dcv, proj, convw_pad, dproj)
    dconvw = jnp.sum(dconvw8.reshape(CONV_HALO, 8, CONV_W), axis=1)
    dqw_g3, dkw_g3 = [], []
    for g in range(N_GROUPS):
        dproj, dqw_g, dkw_g = _attn_bwd(proj, da, delta, lse, qw2, kw2, dproj, g)
        dqw_g3.append(dqw_g)
        dkw_g3.append(dkw_g)
    dw_in_p = _mm_dw(ht, dproj).reshape(N_DEV, SHARD_W, D_MODEL)
    dwo_p = _mm_acc(mt, dmo, "mm_dwo", col_slots=False).reshape(N_DEV, D_MODEL // N_DEV, D_MODEL)
    dwa_p = _mm_acc(yat, dya, "mm_dwa", col_slots=True)
    dwc_p = _mm_acc(yct, dyc, "mm_dwc", col_slots=True)

    partials = [dw_in_p, dwa_p, dwc_p, dwo_p]
    me_arr = jnp.reshape(me, (1,)).astype(jnp.int32)
    from_sib = _exchange_sibling(partials, "exchange_sibling")
    presums = [_presum(p, f, me_arr, f"presum{i}") for i, (p, f) in enumerate(zip(partials, from_sib))]
    s_sems, r_sems, pre_thru, land_thru, token = _exchange_chips_start(presums, "exchange_chips_start")
    dh = _mm_dh(dproj, wt, token)
    gx, dsh, dsc, dnw = _norm_bwd(x2, dh, dout, norm_w, scale)
    small_p = _pack_partials(dsh, dsc, dgate, dnw, dbg, dqw_g3, dkw_g3, dcb, dlnw, dlnb, loss_p)
    small_all, dconvw_all = _all_gather([small_p, dconvw], "gather_small", vmem=True)

    small_w = (b_ada, norm_w, b_gate, q_norm_w, k_norm_w, conv_b, conv_ln_w, conv_ln_b)
    small_m = (m_b_ada, m_norm_w, m_b_gate, m_q_norm_w, m_k_norm_w, m_conv_b, m_conv_ln_w, m_conv_ln_b)
    small_v = (v_b_ada, v_norm_w, v_b_gate, v_q_norm_w, v_k_norm_w, v_conv_b, v_conv_ln_w, v_conv_ln_b)
    r_small, loss_row = _adamw_small(small_all, small_w, small_m, small_v)
    dcw_mine = lax.dynamic_slice_in_dim(dconvw_all[:, :CONV_K, :], me * HEAD_DIM, HEAD_DIM, axis=2)
    r_convw = _adamw([(dcw_mine, N_DEV)], conv_w[0], m_conv_w[0], v_conv_w[0], "adamw_conv_w", stacked=True)

    d_ada_all = small_all[:, 0:3, :].reshape(N_DEV, 3 * D_MODEL)
    d_ada_cols = lax.dynamic_slice_in_dim(d_ada_all, me * (3 * D_MODEL // N_DEV), 3 * D_MODEL // N_DEV, axis=1)
    g_wada = _ada_bwd(c_all, d_ada_cols)
    r_ada = _adamw(g_wada, w_ada[0], m_w_ada[0], v_w_ada[0], "adamw_w_ada", stacked=False)
    pres, lands = _exchange_chips_wait(s_sems, r_sems, pre_thru, land_thru, r_ada[1], "exchange_chips_wait")
    terms = [[(p, 1), (l, len(CHIP_K))] for p, l in zip(pres, lands)]
    r_win = [jnp.transpose(r) for r in _adamw(terms[0], w_in_t, m_w_in_t, v_w_in_t, "adamw_w_in", stacked=True)]
    r_wap = _adamw(terms[1], w_attn_proj[0], m_w_attn_proj[0], v_w_attn_proj[0], "adamw_w_attn_proj", stacked=True)
    r_wcp = _adamw(terms[2], w_conv_proj[0], m_w_conv_proj[0], v_w_conv_proj[0], "adamw_w_conv_proj", stacked=True)
    r_wout = _adamw(terms[3], w_out[0], m_w_out[0], v_w_out[0], "adamw_w_out", stacked=True)

    outs = [loss_row[0, 0], gx[None]]
    for k in range(4):
        b_ada_k, norm_w_k, b_gate_k, qn_k, kn_k, conv_b_k, ln_w_k, ln_b_k = r_small[k]
        outs += [r_ada[k][None], b_ada_k, norm_w_k, r_win[k][None], b_gate_k, qn_k, kn_k, r_wap[k][None],
                 r_convw[k][None], conv_b_k, ln_w_k, ln_b_k, r_wcp[k][None], r_wout[k][None]]
    return tuple(outs)
```

```python
import functools

import jax
import jax.numpy as jnp
from jax import lax
from jax.experimental import pallas as pl
from jax.experimental.pallas import tpu as pltpu

F32 = jnp.float32
BF16 = jnp.bfloat16
SDS = jax.ShapeDtypeStruct
MESH = pl.DeviceIdType.MESH

N_DEV = 8
D_MODEL = 1024
HEAD_DIM = 64
N_GROUPS = 3
DILATIONS = (1, 4, 16)
BAND = 128
BWD_UNROLL = 8
ATTN_W = 512
CONV_W = 512
CONV_K = 31
CONV_HALO = 32
IN_W = 8704
SHARD_W = IN_W // N_DEV
PAIR_W = 2 * SHARD_W
Q0, K0, V0, ZA0, U0, ZC0, G0 = 0, 1536, 3072, 4608, 5120, 6144, 6656
EPS = 1e-6
LANE = 128
VMEM_LIMIT = 56 * 1024 * 1024

ADAM_LR, ADAM_B1, ADAM_B2, ADAM_EPS, ADAM_WD, ADAM_STEP = 0.001, 0.9, 0.999, 1e-08, 0.01, 10

CONVW_FLAT = 2048


def _params(**kw):
    return pltpu.CompilerParams(vmem_limit_bytes=VMEM_LIMIT, **kw)


def _sigmoid(z):
    return 0.5 * jnp.tanh(0.5 * z) + 0.5


def _dot(a, b):
    return jnp.dot(a, b, preferred_element_type=F32)


def _dot_nt(a, b):
    return lax.dot_general(a, b, (((1,), (1,)), ((), ())), preferred_element_type=F32)


def _dot_tn(a, b):
    return lax.dot_general(a, b, (((0,), (0,)), ((), ())), preferred_element_type=F32)


def _peer(x, y, c, k):
    px = 1 - x if (k >> 2) & 1 else x
    py = 1 - y if (k >> 1) & 1 else y
    pc = 1 - c if k & 1 else c
    return (px, py, pc), 4 * px + 2 * py + pc


def _all_gather(arrays, name, vmem):
    n = len(arrays)
    space = pltpu.VMEM if vmem else pl.ANY

    def body(*refs):
        ins, outs = refs[:n], refs[n:2 * n]
        send_sems, recv_sems, local_sems = refs[2 * n:]
        x, y, c = lax.axis_index("x"), lax.axis_index("y"), lax.axis_index("c")
        me = 4 * x + 2 * y + c
        locals_ = [pltpu.make_async_copy(ins[a], outs[a].at[me], local_sems.at[a]) for a in range(n)]
        for cp in locals_:
            cp.start()
        sends = []
        for k in range(1, N_DEV):
            peer, _ = _peer(x, y, c, k)
            for a in range(n):
                cp = pltpu.make_async_remote_copy(
                    src_ref=ins[a], dst_ref=outs[a].at[me], send_sem=send_sems.at[a, k - 1],
                    recv_sem=recv_sems.at[a, k - 1], device_id=peer, device_id_type=MESH)
                cp.start()
                sends.append(cp)
        for k in range(1, N_DEV):
            peer, pidx = _peer(x, y, c, k)
            for a in range(n):
                pltpu.make_async_remote_copy(
                    src_ref=ins[a], dst_ref=outs[a].at[pidx], send_sem=send_sems.at[a, k - 1],
                    recv_sem=recv_sems.at[a, k - 1], device_id=peer, device_id_type=MESH).wait_recv()
        for cp in sends:
            cp.wait_send()
        for cp in locals_:
            cp.wait()

    return pl.pallas_call(
        body, name=name,
        out_shape=[SDS((N_DEV,) + a.shape, a.dtype) for a in arrays],
        in_specs=[pl.BlockSpec(memory_space=space)] * n,
        out_specs=[pl.BlockSpec(memory_space=space)] * n,
        scratch_shapes=[pltpu.SemaphoreType.DMA((n, N_DEV - 1)), pltpu.SemaphoreType.DMA((n, N_DEV - 1)),
                        pltpu.SemaphoreType.DMA((n,))],
        compiler_params=_params(),
    )(*arrays)


CHIP_K = (2, 4, 6)


def _all_gather_chips(arrays, name):
    n = len(arrays)

    def body(*refs):
        ins, outs = refs[:n], refs[n:2 * n]
        send_sems, recv_sems, local_sems = refs[2 * n:]
        x, y, c = lax.axis_index("x"), lax.axis_index("y"), lax.axis_index("c")
        me = 4 * x + 2 * y + c
        sib, sib_idx = _peer(x, y, c, 1)

        def copy(a, slot, block, to, src=None):
            return pltpu.make_async_remote_copy(
                src_ref=outs[a].at[block] if src is None else src, dst_ref=outs[a].at[block],
                send_sem=send_sems.at[a, slot], recv_sem=recv_sems.at[a, slot], device_id=to, device_id_type=MESH)

        locals_ = [pltpu.make_async_copy(ins[a], outs[a].at[me], local_sems.at[a]) for a in range(n)]
        for cp in locals_:
            cp.start()
        sends = [copy(a, 0, me, sib, src=ins[a]) for a in range(n)]
        for j, k in enumerate(CHIP_K):
            peer, _ = _peer(x, y, c, k)
            sends += [copy(a, 1 + j, me, peer, src=ins[a]) for a in range(n)]
        for cp in sends:
            cp.start()
        for j, k in enumerate(CHIP_K):
            peer, pidx = _peer(x, y, c, k)
            for a in range(n):
                copy(a, 1 + j, pidx, peer).wait_recv()
                fwd = copy(a, 4 + j, pidx, sib)
                fwd.start()
                sends.append(fwd)
        for a in range(n):
            copy(a, 0, sib_idx, sib).wait_recv()
        for j, k in enumerate(CHIP_K):
            _, pidx = _peer(x, y, 1 - c, k)
            for a in range(n):
                copy(a, 4 + j, pidx, sib).wait_recv()
        for cp in sends:
            cp.wait_send()
        for cp in locals_:
            cp.wait()

    return pl.pallas_call(
        body, name=name,
        out_shape=[SDS((N_DEV,) + a.shape, a.dtype) for a in arrays],
        in_specs=[pl.BlockSpec(memory_space=pl.ANY)] * n,
        out_specs=[pl.BlockSpec(memory_space=pl.ANY)] * n,
        scratch_shapes=[pltpu.SemaphoreType.DMA((n, N_DEV - 1)), pltpu.SemaphoreType.DMA((n, N_DEV - 1)),
                        pltpu.SemaphoreType.DMA((n,))],
        compiler_params=_params(),
    )(*arrays)


def _exchange_sibling(arrays, name):
    n = len(arrays)
    ks = (0,) + CHIP_K

    def body(*refs):
        ins, outs = refs[:n], refs[n:2 * n]
        send_sems, recv_sems = refs[2 * n:]
        x, y, c = lax.axis_index("x"), lax.axis_index("y"), lax.axis_index("c")
        sib, sib_idx = _peer(x, y, c, 1)
        sends = []
        for i, k in enumerate(ks):
            _, tgt = _peer(x, y, 1 - c, k) if k else (None, sib_idx)
            for a in range(n):
                cp = pltpu.make_async_remote_copy(
                    src_ref=ins[a].at[tgt], dst_ref=outs[a].at[i], send_sem=send_sems.at[a, i],
                    recv_sem=recv_sems.at[a, i], device_id=sib, device_id_type=MESH)
                cp.start()
                sends.append(cp)
        for cp in sends:
            cp.wait_recv()
        for cp in sends:
            cp.wait_send()

    return pl.pallas_call(
        body, name=name,
        out_shape=[SDS((len(ks),) + a.shape[1:], a.dtype) for a in arrays],
        in_specs=[pl.BlockSpec(memory_space=pl.ANY)] * n,
        out_specs=[pl.BlockSpec(memory_space=pl.ANY)] * n,
        scratch_shapes=[pltpu.SemaphoreType.DMA((n, len(ks))), pltpu.SemaphoreType.DMA((n, len(ks)))],
        compiler_params=_params(),
    )(*arrays)


def _presum(mine, from_sib, me_arr, name):
    _, rows, cols = mine.shape
    tr = _row_tile(rows)
    ns = 1 + len(CHIP_K)

    def body(me_ref, a_ref, b_ref, o_ref):
        del me_ref
        o_ref[...] = (a_ref[...].astype(F32) + b_ref[...].astype(F32)).astype(o_ref.dtype)

    grid_spec = pltpu.PrefetchScalarGridSpec(
        num_scalar_prefetch=1, grid=(ns, rows // tr),
        in_specs=[pl.BlockSpec((1, tr, cols), lambda j, i, me: (jnp.bitwise_xor(me[0], 2 * j), i, 0)),
                  pl.BlockSpec((1, tr, cols), lambda j, i, me: (j, i, 0))],
        out_specs=pl.BlockSpec((1, tr, cols), lambda j, i, me: (j, i, 0)))
    return pl.pallas_call(body, name=name, grid_spec=grid_spec, out_shape=SDS((ns, rows, cols), mine.dtype),
                          compiler_params=_params())(me_arr, mine, from_sib)


HBM_SPEC = pl.BlockSpec(memory_space=pltpu.HBM)
SEM_SPEC = pl.BlockSpec(memory_space=pltpu.SEMAPHORE)
SIDE_EFFECT = pltpu.SideEffectType.DATAFLOW_SIDE_EFFECTING


def _chips_copies(pre_refs, land_refs, send_sems, recv_sems):
    x, y, c = lax.axis_index("x"), lax.axis_index("y"), lax.axis_index("c")
    copies = []
    for j, k in enumerate(CHIP_K):
        peer, _ = _peer(x, y, c, k)
        for a in range(len(pre_refs)):
            copies.append(pltpu.make_async_remote_copy(
                src_ref=pre_refs[a].at[1 + j], dst_ref=land_refs[a].at[j], send_sem=send_sems.at[a * len(CHIP_K) + j],
                recv_sem=recv_sems.at[a * len(CHIP_K) + j], device_id=peer, device_id_type=MESH))
    return copies


def _exchange_chips_start(presums, name):
    n = len(presums)

    def body(*refs):
        pre, land = refs[:n], refs[n:2 * n]
        send_sems, recv_sems = refs[2 * n], refs[2 * n + 1]
        token = refs[-1]
        for cp in _chips_copies(pre, land, send_sems, recv_sems):
            cp.start()
        token[...] = jnp.zeros_like(token)

    nk = len(CHIP_K)
    hbm = [pltpu.HBM(p.shape, p.dtype) for p in presums]
    hbm_land = [pltpu.HBM((nk,) + p.shape[1:], p.dtype) for p in presums]
    res = pl.pallas_call(
        body, name=name,
        out_shape=(pltpu.SemaphoreType.DMA((n * nk,)), pltpu.SemaphoreType.DMA((n * nk,)), *hbm, *hbm_land, SDS((8, LANE), F32)),
        in_specs=[HBM_SPEC] * (2 * n),
        out_specs=(SEM_SPEC, SEM_SPEC, *([HBM_SPEC] * (2 * n)), pl.BlockSpec(memory_space=pltpu.VMEM)),
        input_output_aliases={i: 2 + i for i in range(2 * n)},
        compiler_params=pltpu.CompilerParams(has_side_effects=SIDE_EFFECT),
    )(*[pltpu.with_memory_space_constraint(p, pltpu.HBM) for p in presums],
      *[pltpu.with_memory_space_constraint(lax.empty((nk,) + p.shape[1:], p.dtype), pltpu.HBM) for p in presums])
    return res[0], res[1], res[2:2 + n], res[2 + n:2 + 2 * n], res[-1]


def _exchange_chips_wait(send_sems, recv_sems, pre_thru, land_thru, after, name):
    n = len(pre_thru)

    def body(*refs):
        pre, land = refs[:n], refs[n:2 * n]
        s_sems, r_sems = refs[2 * n], refs[2 * n + 1]
        for cp in _chips_copies(pre, land, s_sems, r_sems):
            cp.wait_send()
            cp.wait_recv()

    hbm = [pltpu.HBM(p.shape, p.dtype) for p in (*pre_thru, *land_thru)]
    res = pl.pallas_call(
        body, name=name, out_shape=tuple(hbm),
        in_specs=[HBM_SPEC] * (2 * n) + [SEM_SPEC, SEM_SPEC, pl.BlockSpec(memory_space=pl.ANY)],
        out_specs=tuple([HBM_SPEC] * (2 * n)),
        input_output_aliases={i: i for i in range(2 * n)},
        compiler_params=pltpu.CompilerParams(has_side_effects=SIDE_EFFECT),
    )(*pre_thru, *land_thru, send_sems, recv_sems, after)
    return res[:n], res[n:]


def _exchange_chips(presums, name):
    n = len(presums)
    nk = len(CHIP_K)

    def body(*refs):
        pre, land = refs[:n], refs[n:2 * n]
        send_sems, recv_sems = refs[2 * n:]
        copies = _chips_copies(pre, land, send_sems, recv_sems)
        for cp in copies:
            cp.start()
        for cp in copies:
            cp.wait_recv()
        for cp in copies:
            cp.wait_send()

    return pl.pallas_call(
        body, name=name,
        out_shape=[SDS((nk,) + p.shape[1:], p.dtype) for p in presums],
        in_specs=[pl.BlockSpec(memory_space=pl.ANY)] * n,
        out_specs=[pl.BlockSpec(memory_space=pl.ANY)] * n,
        scratch_shapes=[pltpu.SemaphoreType.DMA((n * nk,)), pltpu.SemaphoreType.DMA((n * nk,))],
        compiler_params=_params(),
    )(*presums)


def _cast_bf16(w, name):
    def body(w_ref, o_ref):
        o_ref[...] = w_ref[...].astype(BF16)

    return pl.pallas_call(body, name=name, out_shape=SDS(w.shape, BF16), compiler_params=_params())(w)


def _cols_from_slots(wg, name):
    _, rows, cols = wg.shape

    def body(w_ref, o_ref):
        for j in range(N_DEV):
            o_ref[:, j * cols:(j + 1) * cols] = w_ref[j]

    return pl.pallas_call(body, name=name, out_shape=SDS((rows, N_DEV * cols), wg.dtype), compiler_params=_params())(wg)


def _ada_fwd(c_all, w_ada):
    def body(c_ref, w_ref, o_ref):
        cv = c_ref[...]
        sc = (cv * _sigmoid(cv)).astype(BF16)
        o_ref[...] = _dot(sc, w_ref[...].astype(BF16))

    return pl.pallas_call(body, name="ada_fwd", out_shape=SDS((N_DEV, w_ada.shape[1]), F32),
                          compiler_params=_params())(c_all, w_ada)


def _ada_bwd(c_all, d_ada_cols):
    def body(c_ref, d_ref, o_ref):
        cv = c_ref[...]
        sc = (cv * _sigmoid(cv)).astype(BF16)
        o_ref[...] = _dot_tn(sc, d_ref[...].astype(BF16))

    return pl.pallas_call(body, name="ada_bwd", out_shape=SDS((D_MODEL, d_ada_cols.shape[1]), F32),
                          compiler_params=_params())(c_all, d_ada_cols)


def _norm_fwd(x, norm_w, scale, shift):
    s = x.shape[0]
    tr = 512

    def body(x_ref, nw_ref, sc_ref, sh_ref, h_ref, ht_ref):
        xv = x_ref[...]
        r = lax.rsqrt(jnp.mean(xv * xv, axis=-1, keepdims=True) + EPS)
        h = (xv * r * nw_ref[...]) * (1.0 + sc_ref[...]) + sh_ref[...]
        h_ref[...] = h.astype(BF16)
        ht_ref[...] = h.T.astype(BF16)

    vec = pl.BlockSpec((1, D_MODEL), lambda i: (0, 0))
    return pl.pallas_call(
        body, name="norm_fwd", grid=(s // tr,),
        in_specs=[pl.BlockSpec((tr, D_MODEL), lambda i: (i, 0)), vec, vec, vec],
        out_specs=[pl.BlockSpec((tr, D_MODEL), lambda i: (i, 0)), pl.BlockSpec((D_MODEL, tr), lambda i: (0, i))],
        out_shape=[SDS((s, D_MODEL), BF16), SDS((D_MODEL, s), BF16)], compiler_params=_params(),
    )(x, norm_w, scale, shift)


def _mm_in(h, wt):
    s = h.shape[0]
    tm = 512

    def body(h_ref, w_ref, o_ref):
        o_ref[...] = _dot_nt(h_ref[...], w_ref[...])

    return pl.pallas_call(
        body, name="mm_in", grid=(IN_W // PAIR_W, s // tm),
        in_specs=[pl.BlockSpec((tm, D_MODEL), lambda p, m: (m, 0)),
                  pl.BlockSpec((PAIR_W, D_MODEL), lambda p, m: (p, 0))],
        out_specs=pl.BlockSpec((tm, PAIR_W), lambda p, m: (m, p)),
        out_shape=SDS((s, IN_W), F32), compiler_params=_params(),
    )(h, wt)


def _head_ones():
    a = lax.broadcasted_iota(jnp.int32, (LANE, LANE), 0) // HEAD_DIM
    b = lax.broadcasted_iota(jnp.int32, (LANE, LANE), 1) // HEAD_DIM
    return (a == b).astype(BF16)


def _head_sums(t, ones):
    return _dot(t.astype(BF16), ones)


def _band_bias(bias, transposed=False):
    qi = lax.broadcasted_iota(jnp.int32, (2 * BAND, 2 * BAND), 1 if transposed else 0) % BAND
    kj = lax.broadcasted_iota(jnp.int32, (2 * BAND, 2 * BAND), 0 if transposed else 1)
    dist = qi + BAND - kj
    valid = (dist >= 0) & (dist <= BAND)
    bias[1] = jnp.where(valid, 0.0, -1e30)
    bias[0] = jnp.where(valid & (kj >= BAND), 0.0, -1e30)


def _token_rows(j, d, chunk, per_r):
    return pl.ds(j // per_r + (j % per_r) * (chunk * d), chunk, stride=d)


def _deinterleave(src_ref, dst_ref, w_ref, ones, d, sub_len, chunk, scale, dst_off):
    per_r = sub_len // chunk

    def step(j, _):
        t = src_ref[_token_rows(j, d, chunk, per_r), :]
        if w_ref is not None:
            ms = _head_sums(t * t, ones) * (1.0 / HEAD_DIM)
            t = t * lax.rsqrt(ms + EPS) * (w_ref[...] * scale)
        dst_ref[pl.ds(pl.multiple_of(dst_off + j * chunk, BAND), chunk), :] = t.astype(dst_ref.dtype)
        return 0
    lax.fori_loop(0, d * per_r, step, 0, unroll=4)


def _attn_fwd(proj, qw2, kw2, g):
    s = proj.shape[0]
    d = DILATIONS[g]
    sub_len = s // d
    nb = sub_len // BAND
    chunk = min(sub_len, 256)

    def body(q_ref, k_ref, v_ref, qw_ref, kw_ref, o_ref, l_ref, qd, kd, vd, od, ld, bias):
        lo = lax.broadcasted_iota(jnp.int32, (1, LANE), 1) < HEAD_DIM
        ones = _head_ones()

        @pl.when(pl.program_id(0) == 0)
        def _():
            _band_bias(bias)

        kd[0:BAND, :] = jnp.zeros((BAND, LANE), BF16)
        vd[0:BAND, :] = jnp.zeros((BAND, LANE), BF16)
        _deinterleave(q_ref, qd, qw_ref, ones, d, sub_len, chunk, HEAD_DIM ** -0.5, 0)
        _deinterleave(k_ref, kd, kw_ref, ones, d, sub_len, chunk, 1.0, BAND)
        _deinterleave(v_ref, vd, None, ones, d, sub_len, chunk, 1.0, BAND)

        def block(t, _):
            base = pl.multiple_of(t * BAND, BAND)
            q = qd[pl.ds(base, BAND), :]
            k2 = kd[pl.ds(base, 2 * BAND), :]
            v2 = vd[pl.ds(base, 2 * BAND), :]
            zero = jnp.zeros_like(q)
            qs = jnp.concatenate([jnp.where(lo, q, zero), jnp.where(lo, zero, q)], axis=0)
            sc = _dot_nt(qs, k2) + bias[jnp.minimum(t % nb, 1)]
            m = jnp.max(sc, axis=-1, keepdims=True)
            p = jnp.exp(sc - m)
            den = jnp.sum(p, axis=-1, keepdims=True)
            u = _dot(p.astype(BF16), v2) * (1.0 / den)
            lse = m + jnp.log(den)
            od[pl.ds(base, BAND), :] = jnp.where(lo, u[:BAND], u[BAND:])
            ld[pl.ds(base, BAND), :] = jnp.where(lo, lse[:BAND], lse[BAND:])
            return 0
        lax.fori_loop(0, s // BAND, block, 0, unroll=16)

        per_r = sub_len // chunk

        def back(j, _):
            src = pl.ds(pl.multiple_of(j * chunk, chunk), chunk)
            dst = _token_rows(j, d, chunk, per_r)
            o_ref[dst, :] = od[src, :]
            l_ref[dst, :] = ld[src, :]
            return 0
        lax.fori_loop(0, d * per_r, back, 0, unroll=2)

    col = lambda off: pl.BlockSpec((s, LANE), lambda hp, off=off: (0, off // LANE + 4 * g + hp))
    vec = pl.BlockSpec((1, LANE), lambda hp: (0, 0))
    out = pl.BlockSpec((s, LANE), lambda hp: (0, hp))
    return pl.pallas_call(
        body, name=f"attn_fwd{g}", grid=(ATTN_W // LANE,),
        in_specs=[col(Q0), col(K0), col(V0), vec, vec], out_specs=[out, out],
        out_shape=[SDS((s, ATTN_W), F32), SDS((s, ATTN_W), F32)],
        scratch_shapes=[pltpu.VMEM((s, LANE), BF16), pltpu.VMEM((s + BAND, LANE), BF16), pltpu.VMEM((s + BAND, LANE), BF16),
                        pltpu.VMEM((s, LANE), F32), pltpu.VMEM((s, LANE), F32),
                        pltpu.VMEM((2, 2 * BAND, 2 * BAND), F32)],
        compiler_params=_params(),
    )(proj, proj, proj, qw2, kw2)


def _attn_bwd(proj, da, delta, lse, qw2, kw2, dproj, g):
    s = proj.shape[0]
    d = DILATIONS[g]
    sub_len = s // d
    nb = sub_len // BAND
    chunk = min(sub_len, 256)

    def body(q_ref, k_ref, v_ref, da_ref, dl_ref, ls_ref, qw_ref, kw_ref, dp_in, dp_out, dqw_ref, dkw_ref,
             qd, kd, vd, kdt, dad, lst, dlt, dqt, dqd, dkd, dvd, st, stb, bias_t, wacc, sem):
        del dp_in
        hp = pl.program_id(0)
        lo = lax.broadcasted_iota(jnp.int32, (1, LANE), 1) < HEAD_DIM
        row_lo = lax.broadcasted_iota(jnp.int32, (LANE, 1), 0) < HEAD_DIM
        ones = _head_ones()
        per_r = sub_len // chunk
        cblk = chunk // BAND

        @pl.when(hp == 0)
        def _():
            _band_bias(bias_t, transposed=True)

        kd[0:BAND, :] = jnp.zeros((BAND, LANE), BF16)
        vd[0:BAND, :] = jnp.zeros((BAND, LANE), BF16)
        kdt[0] = jnp.zeros((LANE, BAND), BF16)
        _deinterleave(q_ref, qd, qw_ref, ones, d, sub_len, chunk, HEAD_DIM ** -0.5, 0)

        def k_step(j, _):
            t = k_ref[_token_rows(j, d, chunk, per_r), :]
            t = t * lax.rsqrt(_head_sums(t * t, ones) * (1.0 / HEAD_DIM) + EPS) * kw_ref[...]
            kd[pl.ds(pl.multiple_of(BAND + j * chunk, BAND), chunk), :] = t.astype(BF16)
            tt = t.T.astype(BF16)
            for u in range(cblk):
                kdt[1 + j * cblk + u] = tt[:, u * BAND:(u + 1) * BAND]
            return 0
        lax.fori_loop(0, d * per_r, k_step, 0, unroll=4)
        _deinterleave(v_ref, vd, None, ones, d, sub_len, chunk, 1.0, BAND)
        _deinterleave(da_ref, dad, None, ones, d, sub_len, chunk, 1.0, 0)

        half = (lax.broadcasted_iota(jnp.int32, (1, LANE), 1) % HEAD_DIM) < HEAD_DIM // 2

        def rows_step(j, _):
            tok = _token_rows(j, d, chunk, per_r)
            tt = jnp.where(half, ls_ref[tok, :], dl_ref[tok, :]).T
            for u in range(cblk):
                cols = slice(u * BAND, (u + 1) * BAND)
                lst[j * cblk + u, 0:1, :] = tt[0:1, cols]
                lst[j * cblk + u, 1:2, :] = tt[HEAD_DIM:HEAD_DIM + 1, cols]
                dlt[j * cblk + u, 0:1, :] = tt[HEAD_DIM // 2:HEAD_DIM // 2 + 1, cols]
                dlt[j * cblk + u, 1:2, :] = tt[HEAD_DIM + HEAD_DIM // 2:HEAD_DIM + HEAD_DIM // 2 + 1, cols]
            return 0
        lax.fori_loop(0, d * per_r, rows_step, 0, unroll=4)

        def block(t, carry):
            ck, cv = carry
            base = pl.multiple_of(t * BAND, BAND)
            q = qd[pl.ds(base, BAND), :]
            k2 = kd[pl.ds(base, 2 * BAND), :]
            v2 = vd[pl.ds(base, 2 * BAND), :]
            k2t = jnp.concatenate([kdt[t], kdt[t + 1]], axis=1)
            dav = dad[pl.ds(base, BAND), :]
            zero = jnp.zeros_like(q)
            qs = jnp.concatenate([jnp.where(lo, q, zero), jnp.where(lo, zero, q)], axis=0)
            das = jnp.concatenate([jnp.where(lo, dav, zero), jnp.where(lo, zero, dav)], axis=0)
            ls_row = jnp.concatenate([lst[t, 0:1, :], lst[t, 1:2, :]], axis=1)
            dl_row = jnp.concatenate([dlt[t, 0:1, :], dlt[t, 1:2, :]], axis=1)
            sc_t = _dot_nt(k2, qs) + bias_t[jnp.minimum(t % nb, 1)]
            p_t = jnp.exp(sc_t - ls_row)
            dp_t = _dot_nt(v2, das)
            ds_t = (p_t * (dp_t - dl_row)).astype(BF16)
            dv2 = _dot(p_t.astype(BF16), das)
            dk2 = _dot(ds_t, qs)
            dvd[pl.ds(base, BAND), :] = cv + dv2[:BAND]
            dkd[pl.ds(base, BAND), :] = ck + dk2[:BAND]
            dq_t = _dot(k2t, ds_t)
            dqt[t] = jnp.where(row_lo, dq_t[:, :BAND], dq_t[:, BAND:])
            return dk2[BAND:], dv2[BAND:]

        def blocks(i, carry):
            for u in range(BWD_UNROLL):
                carry = block(i * BWD_UNROLL + u, carry)
            return carry
        zeros = jnp.zeros((BAND, LANE), F32)
        ck, cv = lax.fori_loop(0, s // (BAND * BWD_UNROLL), blocks, (zeros, zeros))
        dkd[s:s + BAND, :] = ck
        dvd[s:s + BAND, :] = cv

        def dq_rows(t, _):
            dqd[pl.ds(pl.multiple_of(t * BAND, BAND), BAND), :] = dqt[t].T
            return 0
        lax.fori_loop(0, s // BAND, dq_rows, 0, unroll=4)

        def store_cols(col0):
            stb[...] = st[...].astype(BF16)
            cp = pltpu.make_async_copy(
                stb, dp_out.at[:, pl.ds(pl.multiple_of(col0 + LANE * (4 * g + hp), LANE), LANE)], sem)
            cp.start()
            cp.wait()

        def norm_back(src_ref, dy_ref, dy_off, w_ref, scale, dw_ref, col0):
            wacc[...] = jnp.zeros_like(wacc)

            def step(j, _):
                tok = _token_rows(j, d, chunk, per_r)
                t = src_ref[tok, :]
                dy = dy_ref[pl.ds(pl.multiple_of(dy_off + j * chunk, BAND), chunk), :]
                rr = lax.rsqrt(_head_sums(t * t, ones) * (1.0 / HEAD_DIM) + EPS)
                nrm = t * rr
                wacc[...] += jnp.sum((dy * nrm).reshape(chunk // 8, 8, LANE), axis=0)
                dn = dy * (w_ref[...] * scale)
                st[tok, :] = rr * (dn - nrm * (_head_sums(dn * nrm, ones) * (1.0 / HEAD_DIM)))
                return 0
            lax.fori_loop(0, d * per_r, step, 0, unroll=4)
            dw_ref[...] += jnp.broadcast_to(jnp.sum(wacc[...], axis=0, keepdims=True) * scale, dw_ref.shape)
            store_cols(col0)

        @pl.when(hp == 0)
        def _():
            dqw_ref[...] = jnp.zeros_like(dqw_ref)
            dkw_ref[...] = jnp.zeros_like(dkw_ref)

        norm_back(q_ref, dqd, 0, qw_ref, HEAD_DIM ** -0.5, dqw_ref, Q0)
        norm_back(k_ref, dkd, BAND, kw_ref, 1.0, dkw_ref, K0)

        def v_back(j, _):
            src = pl.ds(pl.multiple_of(BAND + j * chunk, BAND), chunk)
            st[_token_rows(j, d, chunk, per_r), :] = dvd[src, :]
            return 0
        lax.fori_loop(0, d * per_r, v_back, 0, unroll=2)
        store_cols(V0)

    col = lambda off: pl.BlockSpec((s, LANE), lambda hp, off=off: (0, off // LANE + 4 * g + hp))
    mid = pl.BlockSpec((s, LANE), lambda hp: (0, hp))
    vec = pl.BlockSpec((1, LANE), lambda hp: (0, 0))
    acc = pl.BlockSpec((8, LANE), lambda hp: (0, 0))
    any_ = pl.BlockSpec(memory_space=pl.ANY)
    return pl.pallas_call(
        body, name=f"attn_bwd{g}", grid=(ATTN_W // LANE,),
        in_specs=[col(Q0), col(K0), col(V0), mid, mid, mid, vec, vec, any_],
        out_specs=[any_, acc, acc],
        out_shape=[SDS(dproj.shape, dproj.dtype), SDS((8, LANE), F32), SDS((8, LANE), F32)],
        input_output_aliases={8: 0},
        scratch_shapes=[pltpu.VMEM((s, LANE), BF16), pltpu.VMEM((s + BAND, LANE), BF16), pltpu.VMEM((s + BAND, LANE), BF16),
                        pltpu.VMEM((s // BAND + 1, LANE, BAND), BF16), pltpu.VMEM((s, LANE), BF16),
                        pltpu.VMEM((s // BAND, 8, BAND), F32), pltpu.VMEM((s // BAND, 8, BAND), F32),
                        pltpu.VMEM((s // BAND, LANE, BAND), F32),
                        pltpu.VMEM((s, LANE), F32), pltpu.VMEM((s + BAND, LANE), F32), pltpu.VMEM((s + BAND, LANE), F32),
                        pltpu.VMEM((s, LANE), F32), pltpu.VMEM((s, LANE), BF16),
                        pltpu.VMEM((2, 2 * BAND, 2 * BAND), F32), pltpu.VMEM((8, LANE), F32),
                        pltpu.SemaphoreType.DMA(())],
        compiler_params=_params(),
    )(proj, proj, proj, da, delta, lse, qw2, kw2, dproj)


def _tap_views(ext_ref, sh_ref, offsets, tr, cols):
    for b in range(8):
        group = [j for j, o in enumerate(offsets) if o % 8 == b]
        if not group:
            continue
        first = min(offsets[j] for j in group)
        span = tr + max(offsets[j] for j in group) - first
        sh_ref[0:span, cols] = ext_ref[first:first + span, cols]
        for j in group:
            yield j, sh_ref[offsets[j] - first:offsets[j] - first + tr, cols]


def _silu_grad(z, sg):
    return sg * (1.0 + z * (1.0 - sg))


def _glu(u):
    a_h, b_h = u[:, :CONV_W], u[:, CONV_W:]
    sg = _sigmoid(b_h)
    return a_h, sg, a_h * sg


def _tail(x, tgt, proj, o3, l3, wa, wc, wo, gate, bga, bgc, convw, convb, lnw, lnb, bd):
    s = x.shape[0]
    tr = 256

    def body(x_ref, t_ref, za_ref, u_ref, uh_ref, zc_ref, g0_ref, g1_ref, g2_ref, g3_ref,
             o0_ref, o1_ref, o2_ref, l0_ref, l1_ref, l2_ref, wa_ref, wc_ref, wo_ref,
             gate_ref, bga_ref, bgc_ref, cw_ref, cb_ref, lnw_ref, lnb_ref, bd_ref,
             dout_ref, da_ref, dl_ref, lse_ref, dcv_ref, mt_ref, yat_ref, yct_ref, dmo_ref, dya_ref, dyc_ref, dp_ref,
             dgate_ref, dbg_ref, dlnw_ref, dlnb_ref, dcb_ref, loss_ref,
             ext, sh, st_za, st_zc, st_g, sems):
        i = pl.program_id(0)

        @pl.when(i == 0)
        def _():
            for r in (dgate_ref, dbg_ref, dlnw_ref, dlnb_ref, dcb_ref, loss_ref):
                r[...] = jnp.zeros_like(r)

        def acc_rows(ref, v):
            ref[...] += jnp.broadcast_to(jnp.sum(v, axis=0, keepdims=True), ref.shape)

        la, lb, lc = l0_ref[...], l1_ref[...], l2_ref[...]
        mx = jnp.maximum(jnp.maximum(la, lb), lc)
        ea, eb, ec = jnp.exp(la - mx), jnp.exp(lb - mx), jnp.exp(lc - mx)
        den = ea + eb + ec
        inv = 1.0 / den
        attn = (ea * inv) * o0_ref[...] + (eb * inv) * o1_ref[...] + (ec * inv) * o2_ref[...]
        lse_ref[...] = mx + jnp.log(den)

        za = za_ref[...]
        sga = _sigmoid(za)
        sa = za * sga
        ya_in = attn * sa
        y_attn = _dot(ya_in.astype(BF16), wa_ref[...])

        _, _, glu = _glu(u_ref[...])
        _, _, glu_h = _glu(uh_ref[...])
        ext[0:CONV_HALO, :] = jnp.where(i > 0, glu_h, 0.0)
        ext[CONV_HALO:CONV_HALO + tr, :] = glu
        cv_blocks = []
        for cb in range(CONV_W // LANE):
            cols = slice(cb * LANE, (cb + 1) * LANE)
            cv_c = jnp.broadcast_to(cb_ref[:, cols], (tr, LANE))
            for j, rows in _tap_views(ext, sh, [CONV_HALO - (CONV_K - 1) + j for j in range(CONV_K)], tr, cols):
                cv_c = cv_c + cw_ref[j:j + 1, cols] * rows
            cv_blocks.append(cv_c)
        cv = jnp.concatenate(cv_blocks, axis=1)
        mu = jnp.mean(cv, axis=-1, keepdims=True)
        xc = cv - mu
        rstd = lax.rsqrt(jnp.mean(xc * xc, axis=-1, keepdims=True) + EPS)
        nrm = xc * rstd
        ln = nrm * lnw_ref[...] + lnb_ref[...]
        sgl = _sigmoid(ln)
        cs = ln * sgl
        zc = zc_ref[...]
        sgc = _sigmoid(zc)
        scz = zc * sgc
        yc_in = cs * scz
        y_conv = _dot(yc_in.astype(BF16), wc_ref[...])

        ga = _sigmoid(jnp.concatenate([g0_ref[...], g1_ref[...]], axis=1) + bga_ref[...])
        gc = _sigmoid(jnp.concatenate([g2_ref[...], g3_ref[...]], axis=1) + bgc_ref[...])
        merged = ga * y_attn + gc * y_conv
        mo = _dot(merged.astype(BF16), wo_ref[...])
        gate_v = gate_ref[...]
        err = (x_ref[...] + gate_v * mo) - t_ref[...]
        loss_ref[...] += 0.5 * jnp.sum(jnp.mean(err * err, axis=-1, keepdims=True))
        d_out = err * (1.0 / D_MODEL)
        dout_ref[...] = d_out

        acc_rows(dgate_ref, d_out * mo)
        dmo_b = (d_out * gate_v).astype(BF16)
        dmo_ref[...] = dmo_b
        mt_ref[...] = merged.T.astype(BF16)
        d_merged = _dot_nt(dmo_b, wo_ref[...])
        d_ya = (d_merged * ga).astype(BF16)
        d_yc = (d_merged * gc).astype(BF16)
        dya_ref[...] = d_ya
        dyc_ref[...] = d_yc
        dga = d_merged * y_attn * (ga * (1.0 - ga))
        dgc = d_merged * y_conv * (gc * (1.0 - gc))
        dgs = jnp.concatenate([dga, dgc], axis=1)
        acc_rows(dbg_ref, dgs)
        st_g[...] = dgs.astype(BF16)

        yat_ref[...] = ya_in.T.astype(BF16)
        d_ya_in = _dot_nt(d_ya, wa_ref[...])
        d_attn = d_ya_in * sa
        da_ref[...] = d_attn
        st_za[...] = (d_ya_in * attn * _silu_grad(za, sga)).astype(BF16)
        prod = d_attn * attn
        hi = prod.astype(BF16)
        lo_ = (prod - hi.astype(F32)).astype(BF16)
        dl_ref[...] = _dot(hi, bd_ref[...]) + _dot(lo_, bd_ref[...])

        yct_ref[...] = yc_in.T.astype(BF16)
        d_yc_in = _dot_nt(d_yc, wc_ref[...])
        st_zc[...] = (d_yc_in * cs * _silu_grad(zc, sgc)).astype(BF16)
        d_ln = (d_yc_in * scz) * _silu_grad(ln, sgl)
        acc_rows(dlnw_ref, d_ln * nrm)
        acc_rows(dlnb_ref, d_ln)
        d_nrm = d_ln * lnw_ref[...]
        d_cv = rstd * (d_nrm - jnp.mean(d_nrm, axis=-1, keepdims=True)
                       - nrm * jnp.mean(d_nrm * nrm, axis=-1, keepdims=True))
        acc_rows(dcb_ref, d_cv)
        dcv_ref[...] = d_cv

        rows = pl.ds(pl.multiple_of(i * tr, tr), tr)
        cps = [pltpu.make_async_copy(st_za, dp_ref.at[rows, pl.ds(ZA0, ATTN_W)], sems.at[0]),
               pltpu.make_async_copy(st_zc, dp_ref.at[rows, pl.ds(ZC0, CONV_W)], sems.at[1]),
               pltpu.make_async_copy(st_g, dp_ref.at[rows, pl.ds(G0, 2 * D_MODEL)], sems.at[2])]
        for cp in cps:
            cp.start()
        for cp in cps:
            cp.wait()

    def rows(width, colblk=0):
        return pl.BlockSpec((tr, width), lambda i, colblk=colblk: (i, colblk))

    def const(shape):
        return pl.BlockSpec(shape, lambda i: (0,) * len(shape))

    halo = pl.BlockSpec((CONV_HALO, D_MODEL), lambda i: (jnp.maximum(i * (tr // CONV_HALO) - 1, 0), U0 // D_MODEL))
    in_specs = [rows(D_MODEL), rows(D_MODEL), rows(ATTN_W, ZA0 // ATTN_W), rows(D_MODEL, U0 // D_MODEL), halo,
                rows(CONV_W, ZC0 // CONV_W)]
    in_specs += [rows(512, G0 // 512 + j) for j in range(4)]
    in_specs += [rows(ATTN_W)] * 6
    in_specs += [const(wa.shape), const(wc.shape), const(wo.shape), const((1, D_MODEL)), const((1, D_MODEL)),
                 const((1, D_MODEL)), const(convw.shape), const((1, CONV_W)), const((1, CONV_W)), const((1, CONV_W)),
                 const(bd.shape)]
    tcol = lambda width: pl.BlockSpec((width, tr), lambda i: (0, i))
    out_specs = [rows(D_MODEL), rows(ATTN_W), rows(ATTN_W), rows(ATTN_W), rows(CONV_W),
                 tcol(D_MODEL), tcol(ATTN_W), tcol(CONV_W), rows(D_MODEL), rows(D_MODEL), rows(D_MODEL),
                 pl.BlockSpec(memory_space=pl.ANY),
                 const((8, D_MODEL)), const((8, 2 * D_MODEL)), const((8, CONV_W)), const((8, CONV_W)), const((8, CONV_W)),
                 const((8, LANE))]
    out_shape = [SDS((s, D_MODEL), F32), SDS((s, ATTN_W), F32), SDS((s, ATTN_W), F32), SDS((s, ATTN_W), F32),
                 SDS((s, CONV_W), F32),
                 SDS((D_MODEL, s), BF16), SDS((ATTN_W, s), BF16), SDS((CONV_W, s), BF16),
                 SDS((s, D_MODEL), BF16), SDS((s, D_MODEL), BF16), SDS((s, D_MODEL), BF16),
                 SDS((s, IN_W), BF16),
                 SDS((8, D_MODEL), F32), SDS((8, 2 * D_MODEL), F32), SDS((8, CONV_W), F32), SDS((8, CONV_W), F32),
                 SDS((8, CONV_W), F32), SDS((8, LANE), F32)]
    return pl.pallas_call(
        body, name="tail", grid=(s // tr,), in_specs=in_specs, out_specs=out_specs, out_shape=out_shape,
        scratch_shapes=[pltpu.VMEM((CONV_HALO + tr, CONV_W), F32), pltpu.VMEM((CONV_HALO + tr, CONV_W), F32),
                        pltpu.VMEM((tr, ATTN_W), BF16),
                        pltpu.VMEM((tr, CONV_W), BF16), pltpu.VMEM((tr, 2 * D_MODEL), BF16),
                        pltpu.SemaphoreType.DMA((3,))],
        compiler_params=_params(),
    )(x, tgt, proj, proj, proj, proj, proj, proj, proj, proj, *o3, *l3, wa, wc, wo, gate, bga, bgc,
      convw, convb, lnw, lnb, bd)


def _conv_bwd(dcv, proj, convw, dproj):
    s = dcv.shape[0]
    tr = 128
    nt = s // tr

    def body(dcv_ref, dcvn_ref, u_ref, uh_ref, cw_ref, dp_in, dp_out, dw_ref, extg, extd, sh):
        del dp_in
        i = pl.program_id(0)

        @pl.when(i == 0)
        def _():
            dw_ref[...] = jnp.zeros_like(dw_ref)

        _, _, glu = _glu(u_ref[...])
        _, _, glu_h = _glu(uh_ref[...])
        extg[0:CONV_HALO, :] = jnp.where(i > 0, glu_h, 0.0)
        extg[CONV_HALO:CONV_HALO + tr, :] = glu
        extd[0:tr, :] = dcv_ref[...]
        extd[tr:tr + CONV_HALO, :] = jnp.where(i < nt - 1, dcvn_ref[...], 0.0)
        for cb in range(CONV_W // LANE):
            cols = slice(cb * LANE, (cb + 1) * LANE)
            dglu = jnp.zeros((tr, LANE), F32)
            for j, rows in _tap_views(extd, sh, [CONV_K - 1 - j for j in range(CONV_K)], tr, cols):
                dglu = dglu + cw_ref[j:j + 1, cols] * rows
            dcv_c = dcv_ref[:, cols]
            for j, rows in _tap_views(extg, sh, [CONV_HALO - (CONV_K - 1) + j for j in range(CONV_K)], tr, cols):
                dw_ref[8 * j:8 * j + 8, cols] += jnp.sum((dcv_c * rows).reshape(tr // 8, 8, LANE), axis=0)
            a_h = u_ref[:, cols]
            sgb = _sigmoid(u_ref[:, CONV_W + cb * LANE:CONV_W + (cb + 1) * LANE])
            dp_out[:, cols] = (dglu * sgb).astype(BF16)
            dp_out[:, CONV_W + cb * LANE:CONV_W + (cb + 1) * LANE] = (dglu * a_h * (sgb * (1.0 - sgb))).astype(BF16)

    ucol = U0 // D_MODEL
    return pl.pallas_call(
        body, name="conv_bwd", grid=(nt,),
        in_specs=[pl.BlockSpec((tr, CONV_W), lambda i: (i, 0)),
                  pl.BlockSpec((CONV_HALO, CONV_W), lambda i: (jnp.minimum((i + 1) * (tr // CONV_HALO), s // CONV_HALO - 1), 0)),
                  pl.BlockSpec((tr, D_MODEL), lambda i: (i, ucol)),
                  pl.BlockSpec((CONV_HALO, D_MODEL), lambda i: (jnp.maximum(i * (tr // CONV_HALO) - 1, 0), ucol)),
                  pl.BlockSpec(convw.shape, lambda i: (0, 0)),
                  pl.BlockSpec(memory_space=pl.ANY)],
        out_specs=[pl.BlockSpec((tr, D_MODEL), lambda i: (i, ucol)), pl.BlockSpec((8 * CONV_HALO, CONV_W), lambda i: (0, 0))],
        out_shape=[SDS(dproj.shape, dproj.dtype), SDS((8 * CONV_HALO, CONV_W), F32)],
        input_output_aliases={5: 0},
        scratch_shapes=[pltpu.VMEM((CONV_HALO + tr, CONV_W), F32)] * 3,
        compiler_params=_params(),
    )(dcv, dcv, proj, proj, convw, dproj)


def _mm_acc(at, b, name, col_slots):
    m, s = at.shape
    n = b.shape[1]
    tk = 512
    nk = s // tk

    def body(a_ref, b_ref, o_ref, acc):
        k = pl.program_id(0)

        @pl.when(k == 0)
        def _():
            acc[...] = jnp.zeros_like(acc)

        acc[...] += _dot(a_ref[...], b_ref[...])

        @pl.when(k == nk - 1)
        def _():
            if col_slots:
                w = n // N_DEV
                for j in range(N_DEV):
                    o_ref[j] = acc[:, j * w:(j + 1) * w].astype(BF16)
            else:
                o_ref[...] = acc[...].astype(BF16)

    if col_slots:
        out_shape = SDS((N_DEV, m, n // N_DEV), BF16)
        out_spec = pl.BlockSpec((N_DEV, m, n // N_DEV), lambda k: (0, 0, 0))
    else:
        out_shape = SDS((m, n), BF16)
        out_spec = pl.BlockSpec((m, n), lambda k: (0, 0))
    return pl.pallas_call(
        body, name=name, grid=(nk,),
        in_specs=[pl.BlockSpec((m, tk), lambda k: (0, k)), pl.BlockSpec((tk, n), lambda k: (k, 0))],
        out_specs=out_spec, out_shape=out_shape, scratch_shapes=[pltpu.VMEM((m, n), F32)],
        compiler_params=_params(),
    )(at, b)


def _mm_dw(ht, dproj):
    s = ht.shape[1]
    tk = 512
    nk = s // tk

    def body(a_ref, b_ref, o_ref, acc):
        k = pl.program_id(1)

        @pl.when(k == 0)
        def _():
            acc[...] = jnp.zeros_like(acc)

        acc[...] += _dot(a_ref[...], b_ref[...])

        @pl.when(k == nk - 1)
        def _():
            o_ref[...] = acc[...].T.astype(BF16)

    return pl.pallas_call(
        body, name="mm_dw", grid=(IN_W // PAIR_W, nk),
        in_specs=[pl.BlockSpec((D_MODEL, tk), lambda p, k: (0, k)), pl.BlockSpec((tk, PAIR_W), lambda p, k: (k, p))],
        out_specs=pl.BlockSpec((PAIR_W, D_MODEL), lambda p, k: (p, 0)),
        out_shape=SDS((IN_W, D_MODEL), BF16), scratch_shapes=[pltpu.VMEM((D_MODEL, PAIR_W), F32)],
        compiler_params=_params(),
    )(ht, dproj)


def _mm_dh(dproj, wt, token):
    s = dproj.shape[0]
    tm = 1024

    def body(dp_ref, w_ref, tok_ref, o_ref):
        del tok_ref
        p = pl.program_id(1)
        part = _dot(dp_ref[...], w_ref[...])

        @pl.when(p == 0)
        def _():
            o_ref[...] = part

        @pl.when(p > 0)
        def _():
            o_ref[...] += part

    return pl.pallas_call(
        body, name="mm_dh", grid=(s // tm, IN_W // PAIR_W),
        in_specs=[pl.BlockSpec((tm, PAIR_W), lambda m, p: (m, p)),
                  pl.BlockSpec((PAIR_W, D_MODEL), lambda m, p: (p, 0)),
                  pl.BlockSpec(token.shape, lambda m, p: (0, 0))],
        out_specs=pl.BlockSpec((tm, D_MODEL), lambda m, p: (m, 0)),
        out_shape=SDS((s, D_MODEL), F32), compiler_params=_params(),
    )(dproj, wt, token)


def _norm_bwd(x, dh, dout, norm_w, scale):
    s = x.shape[0]
    tr = 512

    def body(x_ref, dh_ref, do_ref, nw_ref, sc_ref, gx_ref, dsh_ref, dsc_ref, dnw_ref):
        i = pl.program_id(0)

        @pl.when(i == 0)
        def _():
            for r in (dsh_ref, dsc_ref, dnw_ref):
                r[...] = jnp.zeros_like(r)

        def acc_rows(ref, v):
            ref[...] += jnp.broadcast_to(jnp.sum(v, axis=0, keepdims=True), ref.shape)

        xv = x_ref[...]
        dh_v = dh_ref[...]
        r = lax.rsqrt(jnp.mean(xv * xv, axis=-1, keepdims=True) + EPS)
        xn = xv * r
        one_sc = 1.0 + sc_ref[...]
        acc_rows(dsh_ref, dh_v)
        acc_rows(dsc_ref, dh_v * (xn * nw_ref[...]))
        acc_rows(dnw_ref, dh_v * xn * one_sc)
        dxn = dh_v * (nw_ref[...] * one_sc)
        gx_ref[...] = do_ref[...] + r * (dxn - xn * jnp.mean(dxn * xn, axis=-1, keepdims=True))

    blk = pl.BlockSpec((tr, D_MODEL), lambda i: (i, 0))
    vec = pl.BlockSpec((1, D_MODEL), lambda i: (0, 0))
    acc = pl.BlockSpec((8, D_MODEL), lambda i: (0, 0))
    return pl.pallas_call(
        body, name="norm_bwd", grid=(s // tr,), in_specs=[blk, blk, blk, vec, vec],
        out_specs=[blk, acc, acc, acc],
        out_shape=[SDS((s, D_MODEL), F32)] + [SDS((8, D_MODEL), F32)] * 3, compiler_params=_params(),
    )(x, dh, dout, norm_w, scale)


SMALL_ROWS = 8
QN_COL, KN_COL, CB_COL, LOSS_COL = 0, LANE, 2 * LANE, 2 * LANE + CONV_W


def _pack_partials(dsh, dsc, dgate, dnw, dbg, dqw3, dkw3, dcb, dlnw, dlnb, loss_p):
    n3 = len(dqw3)

    def body(*refs):
        dsh_r, dsc_r, dgate_r, dnw_r, dbg_r = refs[:5]
        dq_r, dk_r = refs[5:5 + n3], refs[5 + n3:5 + 2 * n3]
        dcb_r, dlnw_r, dlnb_r, loss_r, o_ref = refs[5 + 2 * n3:]

        def both_heads(rs):
            t = rs[0][0:1, :]
            for r in rs[1:]:
                t = t + r[0:1, :]
            return t + pltpu.roll(t, HEAD_DIM, axis=1)

        o_ref[0:1, :] = dsh_r[0:1, :]
        o_ref[1:2, :] = dsc_r[0:1, :]
        o_ref[2:3, :] = dgate_r[0:1, :]
        o_ref[3:4, :] = dnw_r[0:1, :]
        o_ref[4:5, :] = dbg_r[0:1, 0:D_MODEL]
        o_ref[5:6, :] = dbg_r[0:1, D_MODEL:]
        o_ref[6:7, QN_COL:QN_COL + LANE] = both_heads(dq_r)
        o_ref[6:7, KN_COL:KN_COL + LANE] = both_heads(dk_r)
        o_ref[6:7, CB_COL:CB_COL + CONV_W] = dcb_r[0:1, :]
        o_ref[6:7, LOSS_COL:LOSS_COL + LANE] = loss_r[0:1, :]
        o_ref[6:7, LOSS_COL + LANE:] = jnp.zeros((1, D_MODEL - LOSS_COL - LANE), F32)
        o_ref[7:8, 0:CONV_W] = dlnw_r[0:1, :]
        o_ref[7:8, CONV_W:] = dlnb_r[0:1, :]

    return pl.pallas_call(body, name="pack_partials", out_shape=SDS((SMALL_ROWS, D_MODEL), F32),
                          compiler_params=_params())(dsh, dsc, dgate, dnw, dbg, *dqw3, *dkw3, dcb, dlnw, dlnb, loss_p)


def _adamw_update(g, w, m, v):
    bc1 = 1.0 - ADAM_B1 ** ADAM_STEP
    bc2 = 1.0 - ADAM_B2 ** ADAM_STEP
    m_new = ADAM_B1 * m + (1.0 - ADAM_B1) * g
    v_new = ADAM_B2 * v + (1.0 - ADAM_B2) * (g * g)
    delta = -ADAM_LR * ((m_new / bc1) / (jnp.sqrt(v_new / bc2) + ADAM_EPS) + ADAM_WD * w)
    return delta, m_new, v_new


def _adamw_small(small_all, ws, ms, vs):
    n = len(ws)
    where = [(slice(0, 3), None), (slice(3, 4), None), (slice(4, 6), None), (6, QN_COL), (6, KN_COL), (6, CB_COL),
             (7, 0), (7, CONV_W)]

    def body(*refs):
        g_ref = refs[0]
        w_r, m_r, v_r = refs[1:1 + n], refs[1 + n:1 + 2 * n], refs[1 + 2 * n:1 + 3 * n]
        outs = refs[1 + 3 * n:]
        g_o, d_o, m_o, v_o, loss_o = outs[:n], outs[n:2 * n], outs[2 * n:3 * n], outs[3 * n:4 * n], outs[4 * n]
        gsum = g_ref[0]
        for dev in range(1, N_DEV):
            gsum = gsum + g_ref[dev]
        loss_o[...] = gsum[6:7, LOSS_COL:LOSS_COL + LANE]
        for i, (rows, col) in enumerate(where):
            width = w_r[i].shape[1]
            if col is None:
                g = jnp.concatenate([gsum[r:r + 1, :] for r in range(rows.start, rows.stop)], axis=1)
            else:
                g = gsum[rows:rows + 1, col:col + width]
            delta, m_new, v_new = _adamw_update(g, w_r[i][...], m_r[i][...], v_r[i][...])
            g_o[i][...] = g
            d_o[i][...] = delta
            m_o[i][...] = m_new
            v_o[i][...] = v_new

    shapes = [SDS(w.shape, F32) for w in ws]
    res = pl.pallas_call(body, name="adamw_small", out_shape=shapes * 4 + [SDS((1, LANE), F32)],
                         compiler_params=_params())(small_all, *ws, *ms, *vs)
    return [res[k * n:(k + 1) * n] for k in range(4)], res[4 * n]


def _row_tile(rows):
    if rows <= 128:
        return rows
    return 128 if rows % 128 == 0 else SHARD_W // 4


def _adamw(gsrc, w, m, v, name, stacked):
    rows, cols = w.shape
    tr = _row_tile(rows)
    n_src = len(gsrc) if stacked else 1

    def body(*refs):
        g_refs, (w_ref, m_ref, v_ref, go_ref, d_ref, mo_ref, vo_ref) = refs[:n_src], refs[n_src:]
        if stacked:
            g = None
            for g_ref, (_, slots) in zip(g_refs, gsrc):
                for j in range(slots):
                    t = g_ref[j].astype(F32)
                    g = t if g is None else g + t
        else:
            g = g_refs[0][...]
        delta, m_new, v_new = _adamw_update(g, w_ref[...], m_ref[...], v_ref[...])
        go_ref[...] = g
        d_ref[...] = delta
        mo_ref[...] = m_new
        vo_ref[...] = v_new

    blk = pl.BlockSpec((tr, cols), lambda i: (i, 0))
    if stacked:
        gspecs = [pl.BlockSpec((slots, tr, arr.shape[2]), lambda i: (0, i, 0)) for arr, slots in gsrc]
        gargs = [arr for arr, _ in gsrc]
    else:
        gspecs, gargs = [blk], [gsrc]
    in_specs = gspecs + [blk, blk, blk]
    args = gargs + [w, m, v]
    return pl.pallas_call(
        body, name=name, grid=(rows // tr,), in_specs=in_specs, out_specs=[blk] * 4,
        out_shape=[SDS((rows, cols), F32)] * 4, compiler_params=_params(),
    )(*args)


def kernel(x, c, w_ada, b_ada, norm_w, w_in, b_gate, q_norm_w, k_norm_w, w_attn_proj, conv_w, conv_b, conv_ln_w, conv_ln_b, w_conv_proj, w_out, loss_target, m_w_ada, m_b_ada, m_norm_w, m_w_in, m_b_gate, m_q_norm_w, m_k_norm_w, m_w_attn_proj, m_conv_w, m_conv_b, m_conv_ln_w, m_conv_ln_b, m_w_conv_proj, m_w_out, v_w_ada, v_b_ada, v_norm_w, v_w_in, v_b_gate, v_q_norm_w, v_k_norm_w, v_w_attn_proj, v_conv_w, v_conv_b, v_conv_ln_w, v_conv_ln_b, v_w_conv_proj, v_w_out):
    xi, yi, ci = lax.axis_index("x"), lax.axis_index("y"), lax.axis_index("c")
    me = 4 * xi + 2 * yi + ci
    x2, tgt2 = x[0], loss_target[0]
    w_in_t, m_w_in_t, v_w_in_t = (jnp.transpose(a[0]) for a in (w_in, m_w_in, v_w_in))
    s = x2.shape[0]

    cw_flat = jnp.pad(conv_w[0].reshape(1, -1), ((0, 0), (0, CONVW_FLAT - CONV_K * HEAD_DIM)))
    pre = jnp.concatenate([c, cw_flat], axis=1).reshape(8, -1)
    (pre_all,) = _all_gather([pre], "gather_c_convw", vmem=True)
    pre_all = pre_all.reshape(N_DEV, -1)
    c_all = pre_all[:, :D_MODEL]
    convw_full = pre_all[:, D_MODEL:D_MODEL + CONV_K * HEAD_DIM].reshape(N_DEV, CONV_K, HEAD_DIM)
    convw_full = jnp.transpose(convw_full, (1, 0, 2)).reshape(CONV_K, CONV_W)
    convw_pad = jnp.pad(convw_full, ((0, CONV_HALO - CONV_K), (0, 0)))

    ada_part = _ada_fwd(c_all, w_ada[0])
    (ada_all,) = _all_gather([ada_part], "gather_ada", vmem=True)
    ada = lax.dynamic_index_in_dim(ada_all, me, axis=1, keepdims=False).reshape(1, 3 * D_MODEL) + b_ada
    shift, scale, gate = ada[:, :D_MODEL], ada[:, D_MODEL:2 * D_MODEL], ada[:, 2 * D_MODEL:]

    wt_g, wa_g, wc_g, wo_g = _all_gather_chips(
        [_cast_bf16(w_in_t, "cast_win"), _cast_bf16(w_attn_proj[0], "cast_wa"), _cast_bf16(w_conv_proj[0], "cast_wc"),
         _cast_bf16(w_out[0], "cast_wo")], "gather_weights")
    wt = wt_g.reshape(IN_W, D_MODEL)
    wa = _cols_from_slots(wa_g, "cols_wa")
    wc = _cols_from_slots(wc_g, "cols_wc")
    wo = wo_g.reshape(D_MODEL, D_MODEL)

    h, ht = _norm_fwd(x2, norm_w, scale, shift)
    proj = _mm_in(h, wt)
    qw2 = jnp.tile(q_norm_w, (1, 2))
    kw2 = jnp.tile(k_norm_w, (1, 2))
    o3, l3 = [], []
    for g in range(N_GROUPS):
        o_g, l_g = _attn_fwd(proj, qw2, kw2, g)
        o3.append(o_g)
        l3.append(l_g)
    head_id = jnp.arange(ATTN_W) // HEAD_DIM
    bd = (head_id[:, None] == head_id[None, :]).astype(BF16)
    (dout, da, delta, lse, dcv, mt, yat, yct, dmo, dya, dyc, dproj,
     dgate, dbg, dlnw, dlnb, dcb, loss_p) = _tail(
        x2, tgt2, proj, o3, l3, wa, wc, wo, gate, b_gate[:, :D_MODEL], b_gate[:, D_MODEL:], convw_pad,
        conv_b, conv_ln_w, conv_ln_b, bd)

    dproj, dconvw8 = _conv_bwd(dcv, proj, convw_pad, dproj)
    dconvw = jnp.sum(dconvw8.reshape(CONV_HALO, 8, CONV_W), axis=1)
    dqw_g3, dkw_g3 = [], []
    for g in range(N_GROUPS):
        dproj, dqw_g, dkw_g = _attn_bwd(proj, da, delta, lse, qw2, kw2, dproj, g)
        dqw_g3.append(dqw_g)
        dkw_g3.append(dkw_g)
    dw_in_p = _mm_dw(ht, dproj).reshape(N_DEV, SHARD_W, D_MODEL)
    dwo_p = _mm_acc(mt, dmo, "mm_dwo", col_slots=False).reshape(N_DEV, D_MODEL // N_DEV, D_MODEL)
    dwa_p = _mm_acc(yat, dya, "mm_dwa", col_slots=True)
    dwc_p = _mm_acc(yct, dyc, "mm_dwc", col_slots=True)

    partials = [dw_in_p, dwa_p, dwc_p, dwo_p]
    me_arr = jnp.reshape(me, (1,)).astype(jnp.int32)
    from_sib = _exchange_sibling(partials, "exchange_sibling")
    presums = [_presum(p, f, me_arr, f"presum{i}") for i, (p, f) in enumerate(zip(partials, from_sib))]
    s_sems, r_sems, pre_thru, land_thru, token = _exchange_chips_start(presums, "exchange_chips_start")
    dh = _mm_dh(dproj, wt, token)
    gx, dsh, dsc, dnw = _norm_bwd(x2, dh, dout, norm_w, scale)
    small_p = _pack_partials(dsh, dsc, dgate, dnw, dbg, dqw_g3, dkw_g3, dcb, dlnw, dlnb, loss_p)
    small_all, dconvw_all = _all_gather([small_p, dconvw], "gather_small", vmem=True)

    small_w = (b_ada, norm_w, b_gate, q_norm_w, k_norm_w, conv_b, conv_ln_w, conv_ln_b)
    small_m = (m_b_ada, m_norm_w, m_b_gate, m_q_norm_w, m_k_norm_w, m_conv_b, m_conv_ln_w, m_conv_ln_b)
    small_v = (v_b_ada, v_norm_w, v_b_gate, v_q_norm_w, v_k_norm_w, v_conv_b, v_conv_ln_w, v_conv_ln_b)
    r_small, loss_row = _adamw_small(small_all, small_w, small_m, small_v)
    dcw_mine = lax.dynamic_slice_in_dim(dconvw_all[:, :CONV_K, :], me * HEAD_DIM, HEAD_DIM, axis=2)
    r_convw = _adamw([(dcw_mine, N_DEV)], conv_w[0], m_conv_w[0], v_conv_w[0], "adamw_conv_w", stacked=True)

    d_ada_all = small_all[:, 0:3, :].reshape(N_DEV, 3 * D_MODEL)
    d_ada_cols = lax.dynamic_slice_in_dim(d_ada_all, me * (3 * D_MODEL // N_DEV), 3 * D_MODEL // N_DEV, axis=1)
    g_wada = _ada_bwd(c_all, d_ada_cols)
    r_ada = _adamw(g_wada, w_ada[0], m_w_ada[0], v_w_ada[0], "adamw_w_ada", stacked=False)
    pres, lands = _exchange_chips_wait(s_sems, r_sems, pre_thru, land_thru, r_ada[1], "exchange_chips_wait")
    terms = [[(p, 1), (l, len(CHIP_K))] for p, l in zip(pres, lands)]
    r_win = [jnp.transpose(r) for r in _adamw(terms[0], w_in_t, m_w_in_t, v_w_in_t, "adamw_w_in", stacked=True)]
    r_wap = _adamw(terms[1], w_attn_proj[0], m_w_attn_proj[0], v_w_attn_proj[0], "adamw_w_attn_proj", stacked=True)
    r_wcp = _adamw(terms[2], w_conv_proj[0], m_w_conv_proj[0], v_w_conv_proj[0], "adamw_w_conv_proj", stacked=True)
    r_wout = _adamw(terms[3], w_out[0], m_w_out[0], v_w_out[0], "adamw_w_out", stacked=True)

    outs = [loss_row[0, 0], gx[None]]
    for k in range(4):
        b_ada_k, norm_w_k, b_gate_k, qn_k, kn_k, conv_b_k, ln_w_k, ln_b_k = r_small[k]
        outs += [r_ada[k][None], b_ada_k, norm_w_k, r_win[k][None], b_gate_k, qn_k, kn_k, r_wap[k][None],
                 r_convw[k][None], conv_b_k, ln_w_k, ln_b_k, r_wcp[k][None], r_wout[k][None]]
    return tuple(outs)
```

```python
import functools

import jax
import jax.numpy as jnp
from jax import lax
from jax.experimental import pallas as pl
from jax.experimental.pallas import tpu as pltpu

F32 = jnp.float32
BF16 = jnp.bfloat16
SDS = jax.ShapeDtypeStruct
MESH = pl.DeviceIdType.MESH

N_DEV = 8
D_MODEL = 1024
HEAD_DIM = 64
N_GROUPS = 3
DILATIONS = (1, 4, 16)
BAND = 128
BWD_UNROLL = 8
ATTN_W = 512
CONV_W = 512
CONV_K = 31
CONV_HALO = 32
IN_W = 8704
SHARD_W = IN_W // N_DEV
PAIR_W = 2 * SHARD_W
Q0, K0, V0, ZA0, U0, ZC0, G0 = 0, 1536, 3072, 4608, 5120, 6144, 6656
EPS = 1e-6
LANE = 128
VMEM_LIMIT = 56 * 1024 * 1024

ADAM_LR, ADAM_B1, ADAM_B2, ADAM_EPS, ADAM_WD, ADAM_STEP = 0.001, 0.9, 0.999, 1e-08, 0.01, 10

CONVW_FLAT = 2048


def _params(**kw):
    return pltpu.CompilerParams(vmem_limit_bytes=VMEM_LIMIT, **kw)


def _sigmoid(z):
    return 0.5 * jnp.tanh(0.5 * z) + 0.5


def _dot(a, b):
    return jnp.dot(a, b, preferred_element_type=F32)


def _dot_nt(a, b):
    return lax.dot_general(a, b, (((1,), (1,)), ((), ())), preferred_element_type=F32)


def _dot_tn(a, b):
    return lax.dot_general(a, b, (((0,), (0,)), ((), ())), preferred_element_type=F32)


def _peer(x, y, c, k):
    px = 1 - x if (k >> 2) & 1 else x
    py = 1 - y if (k >> 1) & 1 else y
    pc = 1 - c if k & 1 else c
    return (px, py, pc), 4 * px + 2 * py + pc


def _all_gather(arrays, name, vmem):
    n = len(arrays)
    space = pltpu.VMEM if vmem else pl.ANY

    def body(*refs):
        ins, outs = refs[:n], refs[n:2 * n]
        send_sems, recv_sems, local_sems = refs[2 * n:]
        x, y, c = lax.axis_index("x"), lax.axis_index("y"), lax.axis_index("c")
        me = 4 * x + 2 * y + c
        locals_ = [pltpu.make_async_copy(ins[a], outs[a].at[me], local_sems.at[a]) for a in range(n)]
        for cp in locals_:
            cp.start()
        sends = []
        for k in range(1, N_DEV):
            peer, _ = _peer(x, y, c, k)
            for a in range(n):
                cp = pltpu.make_async_remote_copy(
                    src_ref=ins[a], dst_ref=outs[a].at[me], send_sem=send_sems.at[a, k - 1],
                    recv_sem=recv_sems.at[a, k - 1], device_id=peer, device_id_type=MESH)
                cp.start()
                sends.append(cp)
        for k in range(1, N_DEV):
            peer, pidx = _peer(x, y, c, k)
            for a in range(n):
                pltpu.make_async_remote_copy(
                    src_ref=ins[a], dst_ref=outs[a].at[pidx], send_sem=send_sems.at[a, k - 1],
                    recv_sem=recv_sems.at[a, k - 1], device_id=peer, device_id_type=MESH).wait_recv()
        for cp in sends:
            cp.wait_send()
        for cp in locals_:
            cp.wait()

    return pl.pallas_call(
        body, name=name,
        out_shape=[SDS((N_DEV,) + a.shape, a.dtype) for a in arrays],
        in_specs=[pl.BlockSpec(memory_space=space)] * n,
        out_specs=[pl.BlockSpec(memory_space=space)] * n,
        scratch_shapes=[pltpu.SemaphoreType.DMA((n, N_DEV - 1)), pltpu.SemaphoreType.DMA((n, N_DEV - 1)),
                        pltpu.SemaphoreType.DMA((n,))],
        compiler_params=_params(),
    )(*arrays)


CHIP_K = (2, 4, 6)


def _all_gather_chips(arrays, name):
    n = len(arrays)

    def body(*refs):
        ins, outs = refs[:n], refs[n:2 * n]
        send_sems, recv_sems, local_sems = refs[2 * n:]
        x, y, c = lax.axis_index("x"), lax.axis_index("y"), lax.axis_index("c")
        me = 4 * x + 2 * y + c
        sib, sib_idx = _peer(x, y, c, 1)

        def copy(a, slot, block, to, src=None):
            return pltpu.make_async_remote_copy(
                src_ref=outs[a].at[block] if src is None else src, dst_ref=outs[a].at[block],
                send_sem=send_sems.at[a, slot], recv_sem=recv_sems.at[a, slot], device_id=to, device_id_type=MESH)

        locals_ = [pltpu.make_async_copy(ins[a], outs[a].at[me], local_sems.at[a]) for a in range(n)]
        for cp in locals_:
            cp.start()
        sends = [copy(a, 0, me, sib, src=ins[a]) for a in range(n)]
        for j, k in enumerate(CHIP_K):
            peer, _ = _peer(x, y, c, k)
            sends += [copy(a, 1 + j, me, peer, src=ins[a]) for a in range(n)]
        for cp in sends:
            cp.start()
        for j, k in enumerate(CHIP_K):
            peer, pidx = _peer(x, y, c, k)
            for a in range(n):
                copy(a, 1 + j, pidx, peer).wait_recv()
                fwd = copy(a, 4 + j, pidx, sib)
                fwd.start()
                sends.append(fwd)
        for a in range(n):
            copy(a, 0, sib_idx, sib).wait_recv()
        for j, k in enumerate(CHIP_K):
            _, pidx = _peer(x, y, 1 - c, k)
            for a in range(n):
                copy(a, 4 + j, pidx, sib).wait_recv()
        for cp in sends:
            cp.wait_send()
        for cp in locals_:
            cp.wait()

    return pl.pallas_call(
        body, name=name,
        out_shape=[SDS((N_DEV,) + a.shape, a.dtype) for a in arrays],
        in_specs=[pl.BlockSpec(memory_space=pl.ANY)] * n,
        out_specs=[pl.BlockSpec(memory_space=pl.ANY)] * n,
        scratch_shapes=[pltpu.SemaphoreType.DMA((n, N_DEV - 1)), pltpu.SemaphoreType.DMA((n, N_DEV - 1)),
                        pltpu.SemaphoreType.DMA((n,))],
        compiler_params=_params(),
    )(*arrays)


def _exchange_sibling(arrays, name):
    n = len(arrays)
    ks = (0,) + CHIP_K

    def body(*refs):
        ins, outs = refs[:n], refs[n:2 * n]
        send_sems, recv_sems = refs[2 * n:]
        x, y, c = lax.axis_index("x"), lax.axis_index("y"), lax.axis_index("c")
        sib, sib_idx = _peer(x, y, c, 1)
        sends = []
        for i, k in enumerate(ks):
            _, tgt = _peer(x, y, 1 - c, k) if k else (None, sib_idx)
            for a in range(n):
                cp = pltpu.make_async_remote_copy(
                    src_ref=ins[a].at[tgt], dst_ref=outs[a].at[i], send_sem=send_sems.at[a, i],
                    recv_sem=recv_sems.at[a, i], device_id=sib, device_id_type=MESH)
                cp.start()
                sends.append(cp)
        for cp in sends:
            cp.wait_recv()
        for cp in sends:
            cp.wait_send()

    return pl.pallas_call(
        body, name=name,
        out_shape=[SDS((len(ks),) + a.shape[1:], a.dtype) for a in arrays],
        in_specs=[pl.BlockSpec(memory_space=pl.ANY)] * n,
        out_specs=[pl.BlockSpec(memory_space=pl.ANY)] * n,
        scratch_shapes=[pltpu.SemaphoreType.DMA((n, len(ks))), pltpu.SemaphoreType.DMA((n, len(ks)))],
        compiler_params=_params(),
    )(*arrays)


def _presum(mine, from_sib, me_arr, name):
    _, rows, cols = mine.shape
    tr = _row_tile(rows)
    ns = 1 + len(CHIP_K)

    def body(me_ref, a_ref, b_ref, o_ref):
        del me_ref
        o_ref[...] = (a_ref[...].astype(F32) + b_ref[...].astype(F32)).astype(o_ref.dtype)

    grid_spec = pltpu.PrefetchScalarGridSpec(
        num_scalar_prefetch=1, grid=(ns, rows // tr),
        in_specs=[pl.BlockSpec((1, tr, cols), lambda j, i, me: (jnp.bitwise_xor(me[0], 2 * j), i, 0)),
                  pl.BlockSpec((1, tr, cols), lambda j, i, me: (j, i, 0))],
        out_specs=pl.BlockSpec((1, tr, cols), lambda j, i, me: (j, i, 0)))
    return pl.pallas_call(body, name=name, grid_spec=grid_spec, out_shape=SDS((ns, rows, cols), mine.dtype),
                          compiler_params=_params())(me_arr, mine, from_sib)


HBM_SPEC = pl.BlockSpec(memory_space=pltpu.HBM)
SEM_SPEC = pl.BlockSpec(memory_space=pltpu.SEMAPHORE)
SIDE_EFFECT = pltpu.SideEffectType.DATAFLOW_SIDE_EFFECTING


def _chips_copies(pre_refs, land_refs, send_sems, recv_sems):
    x, y, c = lax.axis_index("x"), lax.axis_index("y"), lax.axis_index("c")
    copies = []
    for j, k in enumerate(CHIP_K):
        peer, _ = _peer(x, y, c, k)
        for a in range(len(pre_refs)):
            copies.append(pltpu.make_async_remote_copy(
                src_ref=pre_refs[a].at[1 + j], dst_ref=land_refs[a].at[j], send_sem=send_sems.at[a * len(CHIP_K) + j],
                recv_sem=recv_sems.at[a * len(CHIP_K) + j], device_id=peer, device_id_type=MESH))
    return copies


def _exchange_chips_start(presums, name):
    n = len(presums)

    def body(*refs):
        pre, land = refs[:n], refs[n:2 * n]
        send_sems, recv_sems = refs[2 * n], refs[2 * n + 1]
        token = refs[-1]
        for cp in _chips_copies(pre, land, send_sems, recv_sems):
            cp.start()
        token[...] = jnp.zeros_like(token)

    nk = len(CHIP_K)
    hbm = [pltpu.HBM(p.shape, p.dtype) for p in presums]
    hbm_land = [pltpu.HBM((nk,) + p.shape[1:], p.dtype) for p in presums]
    res = pl.pallas_call(
        body, name=name,
        out_shape=(pltpu.SemaphoreType.DMA((n * nk,)), pltpu.SemaphoreType.DMA((n * nk,)), *hbm, *hbm_land, SDS((8, LANE), F32)),
        in_specs=[HBM_SPEC] * (2 * n),
        out_specs=(SEM_SPEC, SEM_SPEC, *([HBM_SPEC] * (2 * n)), pl.BlockSpec(memory_space=pltpu.VMEM)),
        input_output_aliases={i: 2 + i for i in range(2 * n)},
        compiler_params=pltpu.CompilerParams(has_side_effects=SIDE_EFFECT),
    )(*[pltpu.with_memory_space_constraint(p, pltpu.HBM) for p in presums],
      *[pltpu.with_memory_space_constraint(lax.empty((nk,) + p.shape[1:], p.dtype), pltpu.HBM) for p in presums])
    return res[0], res[1], res[2:2 + n], res[2 + n:2 + 2 * n], res[-1]


def _exchange_chips_wait(send_sems, recv_sems, pre_thru, land_thru, after, name):
    n = len(pre_thru)

    def body(*refs):
        pre, land = refs[:n], refs[n:2 * n]
        s_sems, r_sems = refs[2 * n], refs[2 * n + 1]
        for cp in _chips_copies(pre, land, s_sems, r_sems):
            cp.wait_send()
            cp.wait_recv()

    hbm = [pltpu.HBM(p.shape, p.dtype) for p in (*pre_thru, *land_thru)]
    res = pl.pallas_call(
        body, name=name, out_shape=tuple(hbm),
        in_specs=[HBM_SPEC] * (2 * n) + [SEM_SPEC, SEM_SPEC, pl.BlockSpec(memory_space=pl.ANY)],
        out_specs=tuple([HBM_SPEC] * (2 * n)),
        input_output_aliases={i: i for i in range(2 * n)},
        compiler_params=pltpu.CompilerParams(has_side_effects=SIDE_EFFECT),
    )(*pre_thru, *land_thru, send_sems, recv_sems, after)
    return res[:n], res[n:]


def _exchange_chips(presums, name):
    n = len(presums)
    nk = len(CHIP_K)

    def body(*refs):
        pre, land = refs[:n], refs[n:2 * n]
        send_sems, recv_sems = refs[2 * n:]
        copies = _chips_copies(pre, land, send_sems, recv_sems)
        for cp in copies:
            cp.start()
        for cp in copies:
            cp.wait_recv()
        for cp in copies:
            cp.wait_send()

    return pl.pallas_call(
        body, name=name,
        out_shape=[SDS((nk,) + p.shape[1:], p.dtype) for p in presums],
        in_specs=[pl.BlockSpec(memory_space=pl.ANY)] * n,
        out_specs=[pl.BlockSpec(memory_space=pl.ANY)] * n,
        scratch_shapes=[pltpu.SemaphoreType.DMA((n * nk,)), pltpu.SemaphoreType.DMA((n * nk,))],
        compiler_params=_params(),
    )(*presums)


def _cast_bf16(w, name):
    def body(w_ref, o_ref):
        o_ref[...] = w_ref[...].astype(BF16)

    return pl.pallas_call(body, name=name, out_shape=SDS(w.shape, BF16), compiler_params=_params())(w)


def _cols_from_slots(wg, name):
    _, rows, cols = wg.shape

    def body(w_ref, o_ref):
        for j in range(N_DEV):
            o_ref[:, j * cols:(j + 1) * cols] = w_ref[j]

    return pl.pallas_call(body, name=name, out_shape=SDS((rows, N_DEV * cols), wg.dtype), compiler_params=_params())(wg)


def _ada_fwd(c_all, w_ada):
    def body(c_ref, w_ref, o_ref):
        cv = c_ref[...]
        sc = (cv * _sigmoid(cv)).astype(BF16)
        o_ref[...] = _dot(sc, w_ref[...].astype(BF16))

    return pl.pallas_call(body, name="ada_fwd", out_shape=SDS((N_DEV, w_ada.shape[1]), F32),
                          compiler_params=_params())(c_all, w_ada)


def _ada_bwd(c_all, d_ada_cols):
    def body(c_ref, d_ref, o_ref):
        cv = c_ref[...]
        sc = (cv * _sigmoid(cv)).astype(BF16)
        o_ref[...] = _dot_tn(sc, d_ref[...].astype(BF16))

    return pl.pallas_call(body, name="ada_bwd", out_shape=SDS((D_MODEL, d_ada_cols.shape[1]), F32),
                          compiler_params=_params())(c_all, d_ada_cols)


def _norm_fwd(x, norm_w, scale, shift):
    s = x.shape[0]
    tr = 512

    def body(x_ref, nw_ref, sc_ref, sh_ref, h_ref, ht_ref):
        xv = x_ref[...]
        r = lax.rsqrt(jnp.mean(xv * xv, axis=-1, keepdims=True) + EPS)
        h = (xv * r * nw_ref[...]) * (1.0 + sc_ref[...]) + sh_ref[...]
        h_ref[...] = h.astype(BF16)
        ht_ref[...] = h.T.astype(BF16)

    vec = pl.BlockSpec((1, D_MODEL), lambda i: (0, 0))
    return pl.pallas_call(
        body, name="norm_fwd", grid=(s // tr,),
        in_specs=[pl.BlockSpec((tr, D_MODEL), lambda i: (i, 0)), vec, vec, vec],
        out_specs=[pl.BlockSpec((tr, D_MODEL), lambda i: (i, 0)), pl.BlockSpec((D_MODEL, tr), lambda i: (0, i))],
        out_shape=[SDS((s, D_MODEL), BF16), SDS((D_MODEL, s), BF16)], compiler_params=_params(),
    )(x, norm_w, scale, shift)


def _mm_in(h, wt):
    s = h.shape[0]
    tm = 512

    def body(h_ref, w_ref, o_ref):
        o_ref[...] = _dot_nt(h_ref[...], w_ref[...])

    return pl.pallas_call(
        body, name="mm_in", grid=(IN_W // PAIR_W, s // tm),
        in_specs=[pl.BlockSpec((tm, D_MODEL), lambda p, m: (m, 0)),
                  pl.BlockSpec((PAIR_W, D_MODEL), lambda p, m: (p, 0))],
        out_specs=pl.BlockSpec((tm, PAIR_W), lambda p, m: (m, p)),
        out_shape=SDS((s, IN_W), F32), compiler_params=_params(),
    )(h, wt)


def _head_ones():
    a = lax.broadcasted_iota(jnp.int32, (LANE, LANE), 0) // HEAD_DIM
    b = lax.broadcasted_iota(jnp.int32, (LANE, LANE), 1) // HEAD_DIM
    return (a == b).astype(BF16)


def _head_sums(t, ones):
    return _dot(t.astype(BF16), ones)


def _band_bias(bias, transposed=False):
    qi = lax.broadcasted_iota(jnp.int32, (2 * BAND, 2 * BAND), 1 if transposed else 0) % BAND
    kj = lax.broadcasted_iota(jnp.int32, (2 * BAND, 2 * BAND), 0 if transposed else 1)
    dist = qi + BAND - kj
    valid = (dist >= 0) & (dist <= BAND)
    bias[1] = jnp.where(valid, 0.0, -1e30)
    bias[0] = jnp.where(valid & (kj >= BAND), 0.0, -1e30)


def _token_rows(j, d, chunk, per_r):
    return pl.ds(j // per_r + (j % per_r) * (chunk * d), chunk, stride=d)


def _deinterleave(src_ref, dst_ref, w_ref, ones, d, sub_len, chunk, scale, dst_off):
    per_r = sub_len // chunk

    def step(j, _):
        t = src_ref[_token_rows(j, d, chunk, per_r), :]
        if w_ref is not None:
            ms = _head_sums(t * t, ones) * (1.0 / HEAD_DIM)
            t = t * lax.rsqrt(ms + EPS) * (w_ref[...] * scale)
        dst_ref[pl.ds(pl.multiple_of(dst_off + j * chunk, BAND), chunk), :] = t.astype(dst_ref.dtype)
        return 0
    lax.fori_loop(0, d * per_r, step, 0, unroll=4)


def _attn_fwd(proj, qw2, kw2, g):
    s = proj.shape[0]
    d = DILATIONS[g]
    sub_len = s // d
    nb = sub_len // BAND
    chunk = min(sub_len, 256)

    def body(q_ref, k_ref, v_ref, qw_ref, kw_ref, o_ref, l_ref, qd, kd, vd, od, ld, bias):
        lo = lax.broadcasted_iota(jnp.int32, (1, LANE), 1) < HEAD_DIM
        ones = _head_ones()

        @pl.when(pl.program_id(0) == 0)
        def _():
            _band_bias(bias)

        kd[0:BAND, :] = jnp.zeros((BAND, LANE), BF16)
        vd[0:BAND, :] = jnp.zeros((BAND, LANE), BF16)
        _deinterleave(q_ref, qd, qw_ref, ones, d, sub_len, chunk, HEAD_DIM ** -0.5, 0)
        _deinterleave(k_ref, kd, kw_ref, ones, d, sub_len, chunk, 1.0, BAND)
        _deinterleave(v_ref, vd, None, ones, d, sub_len, chunk, 1.0, BAND)

        def block(t, _):
            base = pl.multiple_of(t * BAND, BAND)
            q = qd[pl.ds(base, BAND), :]
            k2 = kd[pl.ds(base, 2 * BAND), :]
            v2 = vd[pl.ds(base, 2 * BAND), :]
            zero = jnp.zeros_like(q)
            qs = jnp.concatenate([jnp.where(lo, q, zero), jnp.where(lo, zero, q)], axis=0)
            sc = _dot_nt(qs, k2) + bias[jnp.minimum(t % nb, 1)]
            m = jnp.max(sc, axis=-1, keepdims=True)
            p = jnp.exp(sc - m)
            den = jnp.sum(p, axis=-1, keepdims=True)
            u = _dot(p.astype(BF16), v2) * (1.0 / den)
            lse = m + jnp.log(den)
            od[pl.ds(base, BAND), :] = jnp.where(lo, u[:BAND], u[BAND:])
            ld[pl.ds(base, BAND), :] = jnp.where(lo, lse[:BAND], lse[BAND:])
            return 0
        lax.fori_loop(0, s // BAND, block, 0, unroll=16)

        per_r = sub_len // chunk

        def back(j, _):
            src = pl.ds(pl.multiple_of(j * chunk, chunk), chunk)
            dst = _token_rows(j, d, chunk, per_r)
            o_ref[dst, :] = od[src, :]
            l_ref[dst, :] = ld[src, :]
            return 0
        lax.fori_loop(0, d * per_r, back, 0, unroll=2)

    col = lambda off: pl.BlockSpec((s, LANE), lambda hp, off=off: (0, off // LANE + 4 * g + hp))
    vec = pl.BlockSpec((1, LANE), lambda hp: (0, 0))
    out = pl.BlockSpec((s, LANE), lambda hp: (0, hp))
    return pl.pallas_call(
        body, name=f"attn_fwd{g}", grid=(ATTN_W // LANE,),
        in_specs=[col(Q0), col(K0), col(V0), vec, vec], out_specs=[out, out],
        out_shape=[SDS((s, ATTN_W), F32), SDS((s, ATTN_W), F32)],
        scratch_shapes=[pltpu.VMEM((s, LANE), BF16), pltpu.VMEM((s + BAND, LANE), BF16), pltpu.VMEM((s + BAND, LANE), BF16),
                        pltpu.VMEM((s, LANE), F32), pltpu.VMEM((s, LANE), F32),
                        pltpu.VMEM((2, 2 * BAND, 2 * BAND), F32)],
        compiler_params=_params(),
    )(proj, proj, proj, qw2, kw2)


def _attn_bwd(proj, da, delta, lse, qw2, kw2, dproj, g):
    s = proj.shape[0]
    d = DILATIONS[g]
    sub_len = s // d
    nb = sub_len // BAND
    chunk = min(sub_len, 256)

    def body(q_ref, k_ref, v_ref, da_ref, dl_ref, ls_ref, qw_ref, kw_ref, dp_in, dp_out, dqw_ref, dkw_ref,
             qd, kd, vd, kdt, dad, lst, dlt, dqt, dqd, dkd, dvd, st, stb, bias_t, wacc, sem):
        del dp_in
        hp = pl.program_id(0)
        lo = lax.broadcasted_iota(jnp.int32, (1, LANE), 1) < HEAD_DIM
        row_lo = lax.broadcasted_iota(jnp.int32, (LANE, 1), 0) < HEAD_DIM
        ones = _head_ones()
        per_r = sub_len // chunk
        cblk = chunk // BAND

        @pl.when(hp == 0)
        def _():
            _band_bias(bias_t, transposed=True)

        kd[0:BAND, :] = jnp.zeros((BAND, LANE), BF16)
        vd[0:BAND, :] = jnp.zeros((BAND, LANE), BF16)
        kdt[0] = jnp.zeros((LANE, BAND), BF16)
        _deinterleave(q_ref, qd, qw_ref, ones, d, sub_len, chunk, HEAD_DIM ** -0.5, 0)

        def k_step(j, _):
            t = k_ref[_token_rows(j, d, chunk, per_r), :]
            t = t * lax.rsqrt(_head_sums(t * t, ones) * (1.0 / HEAD_DIM) + EPS) * kw_ref[...]
            kd[pl.ds(pl.multiple_of(BAND + j * chunk, BAND), chunk), :] = t.astype(BF16)
            tt = t.T.astype(BF16)
            for u in range(cblk):
                kdt[1 + j * cblk + u] = tt[:, u * BAND:(u + 1) * BAND]
            return 0
        lax.fori_loop(0, d * per_r, k_step, 0, unroll=4)
        _deinterleave(v_ref, vd, None, ones, d, sub_len, chunk, 1.0, BAND)
        _deinterleave(da_ref, dad, None, ones, d, sub_len, chunk, 1.0, 0)

        half = (lax.broadcasted_iota(jnp.int32, (1, LANE), 1) % HEAD_DIM) < HEAD_DIM // 2

        def rows_step(j, _):
            tok = _token_rows(j, d, chunk, per_r)
            tt = jnp.where(half, ls_ref[tok, :], dl_ref[tok, :]).T
            for u in range(cblk):
                cols = slice(u * BAND, (u + 1) * BAND)
                lst[j * cblk + u, 0:1, :] = tt[0:1, cols]
                lst[j * cblk + u, 1:2, :] = tt[HEAD_DIM:HEAD_DIM + 1, cols]
                dlt[j * cblk + u, 0:1, :] = tt[HEAD_DIM // 2:HEAD_DIM // 2 + 1, cols]
                dlt[j * cblk + u, 1:2, :] = tt[HEAD_DIM + HEAD_DIM // 2:HEAD_DIM + HEAD_DIM // 2 + 1, cols]
            return 0
        lax.fori_loop(0, d * per_r, rows_step, 0, unroll=4)

        def block(t, carry):
            ck, cv = carry
            base = pl.multiple_of(t * BAND, BAND)
            q = qd[pl.ds(base, BAND), :]
            k2 = kd[pl.ds(base, 2 * BAND), :]
            v2 = vd[pl.ds(base, 2 * BAND), :]
            k2t = jnp.concatenate([kdt[t], kdt[t + 1]], axis=1)
            dav = dad[pl.ds(base, BAND), :]
            zero = jnp.zeros_like(q)
            qs = jnp.concatenate([jnp.where(lo, q, zero), jnp.where(lo, zero, q)], axis=0)
            das = jnp.concatenate([jnp.where(lo, dav, zero), jnp.where(lo, zero, dav)], axis=0)
            ls_row = jnp.concatenate([lst[t, 0:1, :], lst[t, 1:2, :]], axis=1)
            dl_row = jnp.concatenate([dlt[t, 0:1, :], dlt[t, 1:2, :]], axis=1)
            sc_t = _dot_nt(k2, qs) + bias_t[jnp.minimum(t % nb, 1)]
            p_t = jnp.exp(sc_t - ls_row)
            dp_t = _dot_nt(v2, das)
            ds_t = (p_t * (dp_t - dl_row)).astype(BF16)
            dv2 = _dot(p_t.astype(BF16), das)
            dk2 = _dot(ds_t, qs)
            dvd[pl.ds(base, BAND), :] = cv + dv2[:BAND]
            dkd[pl.ds(base, BAND), :] = ck + dk2[:BAND]
            dq_t = _dot(k2t, ds_t)
            dqt[t] = jnp.where(row_lo, dq_t[:, :BAND], dq_t[:, BAND:])
            return dk2[BAND:], dv2[BAND:]

        def blocks(i, carry):
            for u in range(BWD_UNROLL):
                carry = block(i * BWD_UNROLL + u, carry)
            return carry
        zeros = jnp.zeros((BAND, LANE), F32)
        ck, cv = lax.fori_loop(0, s // (BAND * BWD_UNROLL), blocks, (zeros, zeros))
        dkd[s:s + BAND, :] = ck
        dvd[s:s + BAND, :] = cv

        def dq_rows(t, _):
            dqd[pl.ds(pl.multiple_of(t * BAND, BAND), BAND), :] = dqt[t].T
            return 0
        lax.fori_loop(0, s // BAND, dq_rows, 0, unroll=4)

        def col_copy(slot, col0):
            return pltpu.make_async_copy(
                stb.at[slot], dp_out.at[:, pl.ds(pl.multiple_of(col0 + LANE * (4 * g + hp), LANE), LANE)], sem.at[slot])

        def store_cols(slot, col0):
            @pl.when(hp > 0)
            def _():
                col_copy(slot, col0).wait()
            stb[slot] = st[...].astype(BF16)
            col_copy(slot, col0).start()

        def norm_back(src_ref, dy_ref, dy_off, w_ref, scale, dw_ref, slot, col0):
            wacc[...] = jnp.zeros_like(wacc)

            def step(j, _):
                tok = _token_rows(j, d, chunk, per_r)
                t = src_ref[tok, :]
                dy = dy_ref[pl.ds(pl.multiple_of(dy_off + j * chunk, BAND), chunk), :]
                rr = lax.rsqrt(_head_sums(t * t, ones) * (1.0 / HEAD_DIM) + EPS)
                nrm = t * rr
                wacc[...] += jnp.sum((dy * nrm).reshape(chunk // 8, 8, LANE), axis=0)
                dn = dy * (w_ref[...] * scale)
                st[tok, :] = rr * (dn - nrm * (_head_sums(dn * nrm, ones) * (1.0 / HEAD_DIM)))
                return 0
            lax.fori_loop(0, d * per_r, step, 0, unroll=4)
            dw_ref[...] += jnp.broadcast_to(jnp.sum(wacc[...], axis=0, keepdims=True) * scale, dw_ref.shape)
            store_cols(slot, col0)

        @pl.when(hp == 0)
        def _():
            dqw_ref[...] = jnp.zeros_like(dqw_ref)
            dkw_ref[...] = jnp.zeros_like(dkw_ref)

        norm_back(q_ref, dqd, 0, qw_ref, HEAD_DIM ** -0.5, dqw_ref, 0, Q0)
        norm_back(k_ref, dkd, BAND, kw_ref, 1.0, dkw_ref, 1, K0)

        def v_back(j, _):
            src = pl.ds(pl.multiple_of(BAND + j * chunk, BAND), chunk)
            st[_token_rows(j, d, chunk, per_r), :] = dvd[src, :]
            return 0
        lax.fori_loop(0, d * per_r, v_back, 0, unroll=2)
        store_cols(2, V0)

        @pl.when(hp == ATTN_W // LANE - 1)
        def _():
            for slot, col0 in enumerate((Q0, K0, V0)):
                col_copy(slot, col0).wait()

    col = lambda off: pl.BlockSpec((s, LANE), lambda hp, off=off: (0, off // LANE + 4 * g + hp))
    mid = pl.BlockSpec((s, LANE), lambda hp: (0, hp))
    vec = pl.BlockSpec((1, LANE), lambda hp: (0, 0))
    acc = pl.BlockSpec((8, LANE), lambda hp: (0, 0))
    any_ = pl.BlockSpec(memory_space=pl.ANY)
    return pl.pallas_call(
        body, name=f"attn_bwd{g}", grid=(ATTN_W // LANE,),
        in_specs=[col(Q0), col(K0), col(V0), mid, mid, mid, vec, vec, any_],
        out_specs=[any_, acc, acc],
        out_shape=[SDS(dproj.shape, dproj.dtype), SDS((8, LANE), F32), SDS((8, LANE), F32)],
        input_output_aliases={8: 0},
        scratch_shapes=[pltpu.VMEM((s, LANE), BF16), pltpu.VMEM((s + BAND, LANE), BF16), pltpu.VMEM((s + BAND, LANE), BF16),
                        pltpu.VMEM((s // BAND + 1, LANE, BAND), BF16), pltpu.VMEM((s, LANE), BF16),
                        pltpu.VMEM((s // BAND, 8, BAND), F32), pltpu.VMEM((s // BAND, 8, BAND), F32),
                        pltpu.VMEM((s // BAND, LANE, BAND), F32),
                        pltpu.VMEM((s, LANE), F32), pltpu.VMEM((s + BAND, LANE), F32), pltpu.VMEM((s + BAND, LANE), F32),
                        pltpu.VMEM((s, LANE), F32), pltpu.VMEM((3, s, LANE), BF16),
                        pltpu.VMEM((2, 2 * BAND, 2 * BAND), F32), pltpu.VMEM((8, LANE), F32),
                        pltpu.SemaphoreType.DMA((3,))],
        compiler_params=_params(),
    )(proj, proj, proj, da, delta, lse, qw2, kw2, dproj)


def _tap_views(ext_ref, sh_ref, offsets, tr, cols):
    for b in range(8):
        group = [j for j, o in enumerate(offsets) if o % 8 == b]
        if not group:
            continue
        first = min(offsets[j] for j in group)
        span = tr + max(offsets[j] for j in group) - first
        sh_ref[0:span, cols] = ext_ref[first:first + span, cols]
        for j in group:
            yield j, sh_ref[offsets[j] - first:offsets[j] - first + tr, cols]


def _silu_grad(z, sg):
    return sg * (1.0 + z * (1.0 - sg))


def _glu(u):
    a_h, b_h = u[:, :CONV_W], u[:, CONV_W:]
    sg = _sigmoid(b_h)
    return a_h, sg, a_h * sg


def _tail(x, tgt, proj, o3, l3, wa, wc, wo, gate, bga, bgc, convw, convb, lnw, lnb, bd):
    s = x.shape[0]
    tr = 256

    def body(x_ref, t_ref, za_ref, u_ref, uh_ref, zc_ref, g0_ref, g1_ref, g2_ref, g3_ref,
             o0_ref, o1_ref, o2_ref, l0_ref, l1_ref, l2_ref, wa_ref, wc_ref, wo_ref,
             gate_ref, bga_ref, bgc_ref, cw_ref, cb_ref, lnw_ref, lnb_ref, bd_ref,
             dout_ref, da_ref, dl_ref, lse_ref, dcv_ref, mt_ref, yat_ref, yct_ref, dmo_ref, dya_ref, dyc_ref, dp_ref,
             dgate_ref, dbg_ref, dlnw_ref, dlnb_ref, dcb_ref, loss_ref,
             ext, sh, st_za, st_zc, st_g, sems):
        i = pl.program_id(0)

        @pl.when(i == 0)
        def _():
            for r in (dgate_ref, dbg_ref, dlnw_ref, dlnb_ref, dcb_ref, loss_ref):
                r[...] = jnp.zeros_like(r)

        def acc_rows(ref, v):
            ref[...] += jnp.broadcast_to(jnp.sum(v, axis=0, keepdims=True), ref.shape)

        la, lb, lc = l0_ref[...], l1_ref[...], l2_ref[...]
        mx = jnp.maximum(jnp.maximum(la, lb), lc)
        ea, eb, ec = jnp.exp(la - mx), jnp.exp(lb - mx), jnp.exp(lc - mx)
        den = ea + eb + ec
        inv = 1.0 / den
        attn = (ea * inv) * o0_ref[...] + (eb * inv) * o1_ref[...] + (ec * inv) * o2_ref[...]
        lse_ref[...] = mx + jnp.log(den)

        za = za_ref[...]
        sga = _sigmoid(za)
        sa = za * sga
        ya_in = attn * sa
        y_attn = _dot(ya_in.astype(BF16), wa_ref[...])

        _, _, glu = _glu(u_ref[...])
        _, _, glu_h = _glu(uh_ref[...])
        ext[0:CONV_HALO, :] = jnp.where(i > 0, glu_h, 0.0)
        ext[CONV_HALO:CONV_HALO + tr, :] = glu
        cv_blocks = []
        for cb in range(CONV_W // LANE):
            cols = slice(cb * LANE, (cb + 1) * LANE)
            cv_c = jnp.broadcast_to(cb_ref[:, cols], (tr, LANE))
            for j, rows in _tap_views(ext, sh, [CONV_HALO - (CONV_K - 1) + j for j in range(CONV_K)], tr, cols):
                cv_c = cv_c + cw_ref[j:j + 1, cols] * rows
            cv_blocks.append(cv_c)
        cv = jnp.concatenate(cv_blocks, axis=1)
        mu = jnp.mean(cv, axis=-1, keepdims=True)
        xc = cv - mu
        rstd = lax.rsqrt(jnp.mean(xc * xc, axis=-1, keepdims=True) + EPS)
        nrm = xc * rstd
        ln = nrm * lnw_ref[...] + lnb_ref[...]
        sgl = _sigmoid(ln)
        cs = ln * sgl
        zc = zc_ref[...]
        sgc = _sigmoid(zc)
        scz = zc * sgc
        yc_in = cs * scz
        y_conv = _dot(yc_in.astype(BF16), wc_ref[...])

        ga = _sigmoid(jnp.concatenate([g0_ref[...], g1_ref[...]], axis=1) + bga_ref[...])
        gc = _sigmoid(jnp.concatenate([g2_ref[...], g3_ref[...]], axis=1) + bgc_ref[...])
        merged = ga * y_attn + gc * y_conv
        mo = _dot(merged.astype(BF16), wo_ref[...])
        gate_v = gate_ref[...]
        err = (x_ref[...] + gate_v * mo) - t_ref[...]
        loss_ref[...] += 0.5 * jnp.sum(jnp.mean(err * err, axis=-1, keepdims=True))
        d_out = err * (1.0 / D_MODEL)
        dout_ref[...] = d_out

        rows = pl.ds(pl.multiple_of(i * tr, tr), tr)
        cps = [pltpu.make_async_copy(st_za, dp_ref.at[rows, pl.ds(ZA0, ATTN_W)], sems.at[0]),
               pltpu.make_async_copy(st_zc, dp_ref.at[rows, pl.ds(ZC0, CONV_W)], sems.at[1]),
               pltpu.make_async_copy(st_g, dp_ref.at[rows, pl.ds(G0, 2 * D_MODEL)], sems.at[2])]

        @pl.when(i > 0)
        def _():
            for cp in cps:
                cp.wait()

        acc_rows(dgate_ref, d_out * mo)
        dmo_b = (d_out * gate_v).astype(BF16)
        dmo_ref[...] = dmo_b
        mt_ref[...] = merged.T.astype(BF16)
        d_merged = _dot_nt(dmo_b, wo_ref[...])
        d_ya = (d_merged * ga).astype(BF16)
        d_yc = (d_merged * gc).astype(BF16)
        dya_ref[...] = d_ya
        dyc_ref[...] = d_yc
        dga = d_merged * y_attn * (ga * (1.0 - ga))
        dgc = d_merged * y_conv * (gc * (1.0 - gc))
        dgs = jnp.concatenate([dga, dgc], axis=1)
        acc_rows(dbg_ref, dgs)
        st_g[...] = dgs.astype(BF16)

        yat_ref[...] = ya_in.T.astype(BF16)
        d_ya_in = _dot_nt(d_ya, wa_ref[...])
        d_attn = d_ya_in * sa
        da_ref[...] = d_attn
        st_za[...] = (d_ya_in * attn * _silu_grad(za, sga)).astype(BF16)
        prod = d_attn * attn
        hi = prod.astype(BF16)
        lo_ = (prod - hi.astype(F32)).astype(BF16)
        dl_ref[...] = _dot(hi, bd_ref[...]) + _dot(lo_, bd_ref[...])

        yct_ref[...] = yc_in.T.astype(BF16)
        d_yc_in = _dot_nt(d_yc, wc_ref[...])
        st_zc[...] = (d_yc_in * cs * _silu_grad(zc, sgc)).astype(BF16)
        d_ln = (d_yc_in * scz) * _silu_grad(ln, sgl)
        acc_rows(dlnw_ref, d_ln * nrm)
        acc_rows(dlnb_ref, d_ln)
        d_nrm = d_ln * lnw_ref[...]
        d_cv = rstd * (d_nrm - jnp.mean(d_nrm, axis=-1, keepdims=True)
                       - nrm * jnp.mean(d_nrm * nrm, axis=-1, keepdims=True))
        acc_rows(dcb_ref, d_cv)
        dcv_ref[...] = d_cv

        for cp in cps:
            cp.start()

        @pl.when(i == s // tr - 1)
        def _():
            for cp in cps:
                cp.wait()

    def rows(width, colblk=0):
        return pl.BlockSpec((tr, width), lambda i, colblk=colblk: (i, colblk))

    def const(shape):
        return pl.BlockSpec(shape, lambda i: (0,) * len(shape))

    halo = pl.BlockSpec((CONV_HALO, D_MODEL), lambda i: (jnp.maximum(i * (tr // CONV_HALO) - 1, 0), U0 // D_MODEL))
    in_specs = [rows(D_MODEL), rows(D_MODEL), rows(ATTN_W, ZA0 // ATTN_W), rows(D_MODEL, U0 // D_MODEL), halo,
                rows(CONV_W, ZC0 // CONV_W)]
    in_specs += [rows(512, G0 // 512 + j) for j in range(4)]
    in_specs += [rows(ATTN_W)] * 6
    in_specs += [const(wa.shape), const(wc.shape), const(wo.shape), const((1, D_MODEL)), const((1, D_MODEL)),
                 const((1, D_MODEL)), const(convw.shape), const((1, CONV_W)), const((1, CONV_W)), const((1, CONV_W)),
                 const(bd.shape)]
    tcol = lambda width: pl.BlockSpec((width, tr), lambda i: (0, i))
    out_specs = [rows(D_MODEL), rows(ATTN_W), rows(ATTN_W), rows(ATTN_W), rows(CONV_W),
                 tcol(D_MODEL), tcol(ATTN_W), tcol(CONV_W), rows(D_MODEL), rows(D_MODEL), rows(D_MODEL),
                 pl.BlockSpec(memory_space=pl.ANY),
                 const((8, D_MODEL)), const((8, 2 * D_MODEL)), const((8, CONV_W)), const((8, CONV_W)), const((8, CONV_W)),
                 const((8, LANE))]
    out_shape = [SDS((s, D_MODEL), F32), SDS((s, ATTN_W), F32), SDS((s, ATTN_W), F32), SDS((s, ATTN_W), F32),
                 SDS((s, CONV_W), F32),
                 SDS((D_MODEL, s), BF16), SDS((ATTN_W, s), BF16), SDS((CONV_W, s), BF16),
                 SDS((s, D_MODEL), BF16), SDS((s, D_MODEL), BF16), SDS((s, D_MODEL), BF16),
                 SDS((s, IN_W), BF16),
                 SDS((8, D_MODEL), F32), SDS((8, 2 * D_MODEL), F32), SDS((8, CONV_W), F32), SDS((8, CONV_W), F32),
                 SDS((8, CONV_W), F32), SDS((8, LANE), F32)]
    return pl.pallas_call(
        body, name="tail", grid=(s // tr,), in_specs=in_specs, out_specs=out_specs, out_shape=out_shape,
        scratch_shapes=[pltpu.VMEM((CONV_HALO + tr, CONV_W), F32), pltpu.VMEM((CONV_HALO + tr, CONV_W), F32),
                        pltpu.VMEM((tr, ATTN_W), BF16),
                        pltpu.VMEM((tr, CONV_W), BF16), pltpu.VMEM((tr, 2 * D_MODEL), BF16),
                        pltpu.SemaphoreType.DMA((3,))],
        compiler_params=_params(),
    )(x, tgt, proj, proj, proj, proj, proj, proj, proj, proj, *o3, *l3, wa, wc, wo, gate, bga, bgc,
      convw, convb, lnw, lnb, bd)


def _conv_bwd(dcv, proj, convw, dproj):
    s = dcv.shape[0]
    tr = 128
    nt = s // tr

    def body(dcv_ref, dcvn_ref, u_ref, uh_ref, cw_ref, dp_in, dp_out, dw_ref, extg, extd, sh):
        del dp_in
        i = pl.program_id(0)

        @pl.when(i == 0)
        def _():
            dw_ref[...] = jnp.zeros_like(dw_ref)

        _, _, glu = _glu(u_ref[...])
        _, _, glu_h = _glu(uh_ref[...])
        extg[0:CONV_HALO, :] = jnp.where(i > 0, glu_h, 0.0)
        extg[CONV_HALO:CONV_HALO + tr, :] = glu
        extd[0:tr, :] = dcv_ref[...]
        extd[tr:tr + CONV_HALO, :] = jnp.where(i < nt - 1, dcvn_ref[...], 0.0)
        for cb in range(CONV_W // LANE):
            cols = slice(cb * LANE, (cb + 1) * LANE)
            dglu = jnp.zeros((tr, LANE), F32)
            for j, rows in _tap_views(extd, sh, [CONV_K - 1 - j for j in range(CONV_K)], tr, cols):
                dglu = dglu + cw_ref[j:j + 1, cols] * rows
            dcv_c = dcv_ref[:, cols]
            for j, rows in _tap_views(extg, sh, [CONV_HALO - (CONV_K - 1) + j for j in range(CONV_K)], tr, cols):
                dw_ref[8 * j:8 * j + 8, cols] += jnp.sum((dcv_c * rows).reshape(tr // 8, 8, LANE), axis=0)
            a_h = u_ref[:, cols]
            sgb = _sigmoid(u_ref[:, CONV_W + cb * LANE:CONV_W + (cb + 1) * LANE])
            dp_out[:, cols] = (dglu * sgb).astype(BF16)
            dp_out[:, CONV_W + cb * LANE:CONV_W + (cb + 1) * LANE] = (dglu * a_h * (sgb * (1.0 - sgb))).astype(BF16)

    ucol = U0 // D_MODEL
    return pl.pallas_call(
        body, name="conv_bwd", grid=(nt,),
        in_specs=[pl.BlockSpec((tr, CONV_W), lambda i: (i, 0)),
                  pl.BlockSpec((CONV_HALO, CONV_W), lambda i: (jnp.minimum((i + 1) * (tr // CONV_HALO), s // CONV_HALO - 1), 0)),
                  pl.BlockSpec((tr, D_MODEL), lambda i: (i, ucol)),
                  pl.BlockSpec((CONV_HALO, D_MODEL), lambda i: (jnp.maximum(i * (tr // CONV_HALO) - 1, 0), ucol)),
                  pl.BlockSpec(convw.shape, lambda i: (0, 0)),
                  pl.BlockSpec(memory_space=pl.ANY)],
        out_specs=[pl.BlockSpec((tr, D_MODEL), lambda i: (i, ucol)), pl.BlockSpec((8 * CONV_HALO, CONV_W), lambda i: (0, 0))],
        out_shape=[SDS(dproj.shape, dproj.dtype), SDS((8 * CONV_HALO, CONV_W), F32)],
        input_output_aliases={5: 0},
        scratch_shapes=[pltpu.VMEM((CONV_HALO + tr, CONV_W), F32)] * 3,
        compiler_params=_params(),
    )(dcv, dcv, proj, proj, convw, dproj)


def _mm_acc(at, b, name, col_slots):
    m, s = at.shape
    n = b.shape[1]
    tk = 512
    nk = s // tk

    def body(a_ref, b_ref, o_ref, acc):
        k = pl.program_id(0)

        @pl.when(k == 0)
        def _():
            acc[...] = jnp.zeros_like(acc)

        acc[...] += _dot(a_ref[...], b_ref[...])

        @pl.when(k == nk - 1)
        def _():
            if col_slots:
                w = n // N_DEV
                for j in range(N_DEV):
                    o_ref[j] = acc[:, j * w:(j + 1) * w].astype(BF16)
            else:
                o_ref[...] = acc[...].astype(BF16)

    if col_slots:
        out_shape = SDS((N_DEV, m, n // N_DEV), BF16)
        out_spec = pl.BlockSpec((N_DEV, m, n // N_DEV), lambda k: (0, 0, 0))
    else:
        out_shape = SDS((m, n), BF16)
        out_spec = pl.BlockSpec((m, n), lambda k: (0, 0))
    return pl.pallas_call(
        body, name=name, grid=(nk,),
        in_specs=[pl.BlockSpec((m, tk), lambda k: (0, k)), pl.BlockSpec((tk, n), lambda k: (k, 0))],
        out_specs=out_spec, out_shape=out_shape, scratch_shapes=[pltpu.VMEM((m, n), F32)],
        compiler_params=_params(),
    )(at, b)


def _mm_dw(ht, dproj):
    s = ht.shape[1]
    tk = 512
    nk = s // tk

    def body(a_ref, b_ref, o_ref, acc):
        k = pl.program_id(1)

        @pl.when(k == 0)
        def _():
            acc[...] = jnp.zeros_like(acc)

        acc[...] += _dot(a_ref[...], b_ref[...])

        @pl.when(k == nk - 1)
        def _():
            o_ref[...] = acc[...].T.astype(BF16)

    return pl.pallas_call(
        body, name="mm_dw", grid=(IN_W // PAIR_W, nk),
        in_specs=[pl.BlockSpec((D_MODEL, tk), lambda p, k: (0, k)), pl.BlockSpec((tk, PAIR_W), lambda p, k: (k, p))],
        out_specs=pl.BlockSpec((PAIR_W, D_MODEL), lambda p, k: (p, 0)),
        out_shape=SDS((IN_W, D_MODEL), BF16), scratch_shapes=[pltpu.VMEM((D_MODEL, PAIR_W), F32)],
        compiler_params=_params(),
    )(ht, dproj)


def _mm_dh(dproj, wt, token):
    s = dproj.shape[0]
    tm = 1024

    def body(dp_ref, w_ref, tok_ref, o_ref):
        del tok_ref
        p = pl.program_id(1)
        part = _dot(dp_ref[...], w_ref[...])

        @pl.when(p == 0)
        def _():
            o_ref[...] = part

        @pl.when(p > 0)
        def _():
            o_ref[...] += part

    return pl.pallas_call(
        body, name="mm_dh", grid=(s // tm, IN_W // PAIR_W),
        in_specs=[pl.BlockSpec((tm, PAIR_W), lambda m, p: (m, p)),
                  pl.BlockSpec((PAIR_W, D_MODEL), lambda m, p: (p, 0)),
                  pl.BlockSpec(token.shape, lambda m, p: (0, 0))],
        out_specs=pl.BlockSpec((tm, D_MODEL), lambda m, p: (m, 0)),
        out_shape=SDS((s, D_MODEL), F32), compiler_params=_params(),
    )(dproj, wt, token)


def _norm_bwd(x, dh, dout, norm_w, scale):
    s = x.shape[0]
    tr = 512

    def body(x_ref, dh_ref, do_ref, nw_ref, sc_ref, gx_ref, dsh_ref, dsc_ref, dnw_ref):
        i = pl.program_id(0)

        @pl.when(i == 0)
        def _():
            for r in (dsh_ref, dsc_ref, dnw_ref):
                r[...] = jnp.zeros_like(r)

        def acc_rows(ref, v):
            ref[...] += jnp.broadcast_to(jnp.sum(v, axis=0, keepdims=True), ref.shape)

        xv = x_ref[...]
        dh_v = dh_ref[...]
        r = lax.rsqrt(jnp.mean(xv * xv, axis=-1, keepdims=True) + EPS)
        xn = xv * r
        one_sc = 1.0 + sc_ref[...]
        acc_rows(dsh_ref, dh_v)
        acc_rows(dsc_ref, dh_v * (xn * nw_ref[...]))
        acc_rows(dnw_ref, dh_v * xn * one_sc)
        dxn = dh_v * (nw_ref[...] * one_sc)
        gx_ref[...] = do_ref[...] + r * (dxn - xn * jnp.mean(dxn * xn, axis=-1, keepdims=True))

    blk = pl.BlockSpec((tr, D_MODEL), lambda i: (i, 0))
    vec = pl.BlockSpec((1, D_MODEL), lambda i: (0, 0))
    acc = pl.BlockSpec((8, D_MODEL), lambda i: (0, 0))
    return pl.pallas_call(
        body, name="norm_bwd", grid=(s // tr,), in_specs=[blk, blk, blk, vec, vec],
        out_specs=[blk, acc, acc, acc],
        out_shape=[SDS((s, D_MODEL), F32)] + [SDS((8, D_MODEL), F32)] * 3, compiler_params=_params(),
    )(x, dh, dout, norm_w, scale)


SMALL_ROWS = 8
QN_COL, KN_COL, CB_COL, LOSS_COL = 0, LANE, 2 * LANE, 2 * LANE + CONV_W


def _pack_partials(dsh, dsc, dgate, dnw, dbg, dqw3, dkw3, dcb, dlnw, dlnb, loss_p):
    n3 = len(dqw3)

    def body(*refs):
        dsh_r, dsc_r, dgate_r, dnw_r, dbg_r = refs[:5]
        dq_r, dk_r = refs[5:5 + n3], refs[5 + n3:5 + 2 * n3]
        dcb_r, dlnw_r, dlnb_r, loss_r, o_ref = refs[5 + 2 * n3:]

        def both_heads(rs):
            t = rs[0][0:1, :]
            for r in rs[1:]:
                t = t + r[0:1, :]
            return t + pltpu.roll(t, HEAD_DIM, axis=1)

        o_ref[0:1, :] = dsh_r[0:1, :]
        o_ref[1:2, :] = dsc_r[0:1, :]
        o_ref[2:3, :] = dgate_r[0:1, :]
        o_ref[3:4, :] = dnw_r[0:1, :]
        o_ref[4:5, :] = dbg_r[0:1, 0:D_MODEL]
        o_ref[5:6, :] = dbg_r[0:1, D_MODEL:]
        o_ref[6:7, QN_COL:QN_COL + LANE] = both_heads(dq_r)
        o_ref[6:7, KN_COL:KN_COL + LANE] = both_heads(dk_r)
        o_ref[6:7, CB_COL:CB_COL + CONV_W] = dcb_r[0:1, :]
        o_ref[6:7, LOSS_COL:LOSS_COL + LANE] = loss_r[0:1, :]
        o_ref[6:7, LOSS_COL + LANE:] = jnp.zeros((1, D_MODEL - LOSS_COL - LANE), F32)
        o_ref[7:8, 0:CONV_W] = dlnw_r[0:1, :]
        o_ref[7:8, CONV_W:] = dlnb_r[0:1, :]

    return pl.pallas_call(body, name="pack_partials", out_shape=SDS((SMALL_ROWS, D_MODEL), F32),
                          compiler_params=_params())(dsh, dsc, dgate, dnw, dbg, *dqw3, *dkw3, dcb, dlnw, dlnb, loss_p)


def _adamw_update(g, w, m, v):
    bc1 = 1.0 - ADAM_B1 ** ADAM_STEP
    bc2 = 1.0 - ADAM_B2 ** ADAM_STEP
    m_new = ADAM_B1 * m + (1.0 - ADAM_B1) * g
    v_new = ADAM_B2 * v + (1.0 - ADAM_B2) * (g * g)
    delta = -ADAM_LR * ((m_new / bc1) / (jnp.sqrt(v_new / bc2) + ADAM_EPS) + ADAM_WD * w)
    return delta, m_new, v_new


def _adamw_small(small_all, ws, ms, vs):
    n = len(ws)
    where = [(slice(0, 3), None), (slice(3, 4), None), (slice(4, 6), None), (6, QN_COL), (6, KN_COL), (6, CB_COL),
             (7, 0), (7, CONV_W)]

    def body(*refs):
        g_ref = refs[0]
        w_r, m_r, v_r = refs[1:1 + n], refs[1 + n:1 + 2 * n], refs[1 + 2 * n:1 + 3 * n]
        outs = refs[1 + 3 * n:]
        g_o, d_o, m_o, v_o, loss_o = outs[:n], outs[n:2 * n], outs[2 * n:3 * n], outs[3 * n:4 * n], outs[4 * n]
        gsum = g_ref[0]
        for dev in range(1, N_DEV):
            gsum = gsum + g_ref[dev]
        loss_o[...] = gsum[6:7, LOSS_COL:LOSS_COL + LANE]
        for i, (rows, col) in enumerate(where):
            width = w_r[i].shape[1]
            if col is None:
                g = jnp.concatenate([gsum[r:r + 1, :] for r in range(rows.start, rows.stop)], axis=1)
            else:
                g = gsum[rows:rows + 1, col:col + width]
            delta, m_new, v_new = _adamw_update(g, w_r[i][...], m_r[i][...], v_r[i][...])
            g_o[i][...] = g
            d_o[i][...] = delta
            m_o[i][...] = m_new
            v_o[i][...] = v_new

    shapes = [SDS(w.shape, F32) for w in ws]
    res = pl.pallas_call(body, name="adamw_small", out_shape=shapes * 4 + [SDS((1, LANE), F32)],
                         compiler_params=_params())(small_all, *ws, *ms, *vs)
    return [res[k * n:(k + 1) * n] for k in range(4)], res[4 * n]


def _row_tile(rows):
    if rows <= 128:
        return rows
    return 128 if rows % 128 == 0 else SHARD_W // 4


def _adamw(gsrc, w, m, v, name, stacked):
    rows, cols = w.shape
    tr = _row_tile(rows)
    n_src = len(gsrc) if stacked else 1

    def body(*refs):
        g_refs, (w_ref, m_ref, v_ref, go_ref, d_ref, mo_ref, vo_ref) = refs[:n_src], refs[n_src:]
        if stacked:
            g = None
            for g_ref, (_, slots) in zip(g_refs, gsrc):
                for j in range(slots):
                    t = g_ref[j].astype(F32)
                    g = t if g is None else g + t
        else:
            g = g_refs[0][...]
        delta, m_new, v_new = _adamw_update(g, w_ref[...], m_ref[...], v_ref[...])
        go_ref[...] = g
        d_ref[...] = delta
        mo_ref[...] = m_new
        vo_ref[...] = v_new

    blk = pl.BlockSpec((tr, cols), lambda i: (i, 0))
    if stacked:
        gspecs = [pl.BlockSpec((slots, tr, arr.shape[2]), lambda i: (0, i, 0)) for arr, slots in gsrc]
        gargs = [arr for arr, _ in gsrc]
    else:
        gspecs, gargs = [blk], [gsrc]
    in_specs = gspecs + [blk, blk, blk]
    args = gargs + [w, m, v]
    return pl.pallas_call(
        body, name=name, grid=(rows // tr,), in_specs=in_specs, out_specs=[blk] * 4,
        out_shape=[SDS((rows, cols), F32)] * 4, compiler_params=_params(),
    )(*args)


def kernel(x, c, w_ada, b_ada, norm_w, w_in, b_gate, q_norm_w, k_norm_w, w_attn_proj, conv_w, conv_b, conv_ln_w, conv_ln_b, w_conv_proj, w_out, loss_target, m_w_ada, m_b_ada, m_norm_w, m_w_in, m_b_gate, m_q_norm_w, m_k_norm_w, m_w_attn_proj, m_conv_w, m_conv_b, m_conv_ln_w, m_conv_ln_b, m_w_conv_proj, m_w_out, v_w_ada, v_b_ada, v_norm_w, v_w_in, v_b_gate, v_q_norm_w, v_k_norm_w, v_w_attn_proj, v_conv_w, v_conv_b, v_conv_ln_w, v_conv_ln_b, v_w_conv_proj, v_w_out):
    xi, yi, ci = lax.axis_index("x"), lax.axis_index("y"), lax.axis_index("c")
    me = 4 * xi + 2 * yi + ci
    x2, tgt2 = x[0], loss_target[0]
    w_in_t, m_w_in_t, v_w_in_t = (jnp.transpose(a[0]) for a in (w_in, m_w_in, v_w_in))
    s = x2.shape[0]

    cw_flat = jnp.pad(conv_w[0].reshape(1, -1), ((0, 0), (0, CONVW_FLAT - CONV_K * HEAD_DIM)))
    pre = jnp.concatenate([c, cw_flat], axis=1).reshape(8, -1)
    (pre_all,) = _all_gather([pre], "gather_c_convw", vmem=True)
    pre_all = pre_all.reshape(N_DEV, -1)
    c_all = pre_all[:, :D_MODEL]
    convw_full = pre_all[:, D_MODEL:D_MODEL + CONV_K * HEAD_DIM].reshape(N_DEV, CONV_K, HEAD_DIM)
    convw_full = jnp.transpose(convw_full, (1, 0, 2)).reshape(CONV_K, CONV_W)
    convw_pad = jnp.pad(convw_full, ((0, CONV_HALO - CONV_K), (0, 0)))

    ada_part = _ada_fwd(c_all, w_ada[0])
    (ada_all,) = _all_gather([ada_part], "gather_ada", vmem=True)
    ada = lax.dynamic_index_in_dim(ada_all, me, axis=1, keepdims=False).reshape(1, 3 * D_MODEL) + b_ada
    shift, scale, gate = ada[:, :D_MODEL], ada[:, D_MODEL:2 * D_MODEL], ada[:, 2 * D_MODEL:]

    wt_g, wa_g, wc_g, wo_g = _all_gather_chips(
        [_cast_bf16(w_in_t, "cast_win"), _cast_bf16(w_attn_proj[0], "cast_wa"), _cast_bf16(w_conv_proj[0], "cast_wc"),
         _cast_bf16(w_out[0], "cast_wo")], "gather_weights")
    wt = wt_g.reshape(IN_W, D_MODEL)
    wa = _cols_from_slots(wa_g, "cols_wa")
    wc = _cols_from_slots(wc_g, "cols_wc")
    wo = wo_g.reshape(D_MODEL, D_MODEL)

    h, ht = _norm_fwd(x2, norm_w, scale, shift)
    proj = _mm_in(h, wt)
    qw2 = jnp.tile(q_norm_w, (1, 2))
    kw2 = jnp.tile(k_norm_w, (1, 2))
    o3, l3 = [], []
    for g in range(N_GROUPS):
        o_g, l_g = _attn_fwd(proj, qw2, kw2, g)
        o3.append(o_g)
        l3.append(l_g)
    head_id = jnp.arange(ATTN_W) // HEAD_DIM
    bd = (head_id[:, None] == head_id[None, :]).astype(BF16)
    (dout, da, delta, lse, dcv, mt, yat, yct, dmo, dya, dyc, dproj,
     dgate, dbg, dlnw, dlnb, dcb, loss_p) = _tail(
        x2, tgt2, proj, o3, l3, wa, wc, wo, gate, b_gate[:, :D_MODEL], b_gate[:, D_MODEL:], convw_pad,
        conv_b, conv_ln_w, conv_ln_b, bd)

    dproj, dconvw8 = _conv_bwd(dcv, proj, convw_pad, dproj)
    dconvw = jnp.sum(dconvw8.reshape(CONV_HALO, 8, CONV_W), axis=1)
    dqw_g3, dkw_g3 = [], []
    for g in range(N_GROUPS):
        dproj, dqw_g, dkw_g = _attn_bwd(proj, da, delta, lse, qw2, kw2, dproj, g)
        dqw_g3.append(dqw_g)
        dkw_g3.append(dkw_g)
    dw_in_p = _mm_dw(ht, dproj).reshape(N_DEV, SHARD_W, D_MODEL)
    dwo_p = _mm_acc(mt, dmo, "mm_dwo", col_slots=False).reshape(N_DEV, D_MODEL // N_DEV, D_MODEL)
    dwa_p = _mm_acc(yat, dya, "mm_dwa", col_slots=True)
    dwc_p = _mm_acc(yct, dyc, "mm_dwc", col_slots=True)

    partials = [dw_in_p, dwa_p, dwc_p, dwo_p]
    me_arr = jnp.reshape(me, (1,)).astype(jnp.int32)
    from_sib = _exchange_sibling(partials, "exchange_sibling")
    presums = [_presum(p, f, me_arr, f"presum{i}") for i, (p, f) in enumerate(zip(partials, from_sib))]
    s_sems, r_sems, pre_thru, land_thru, token = _exchange_chips_start(presums, "exchange_chips_start")
    dh = _mm_dh(dproj, wt, token)
    gx, dsh, dsc, dnw = _norm_bwd(x2, dh, dout, norm_w, scale)
    small_p = _pack_partials(dsh, dsc, dgate, dnw, dbg, dqw_g3, dkw_g3, dcb, dlnw, dlnb, loss_p)
    small_all, dconvw_all = _all_gather([small_p, dconvw], "gather_small", vmem=True)

    small_w = (b_ada, norm_w, b_gate, q_norm_w, k_norm_w, conv_b, conv_ln_w, conv_ln_b)
    small_m = (m_b_ada, m_norm_w, m_b_gate, m_q_norm_w, m_k_norm_w, m_conv_b, m_conv_ln_w, m_conv_ln_b)
    small_v = (v_b_ada, v_norm_w, v_b_gate, v_q_norm_w, v_k_norm_w, v_conv_b, v_conv_ln_w, v_conv_ln_b)
    r_small, loss_row = _adamw_small(small_all, small_w, small_m, small_v)
    dcw_mine = lax.dynamic_slice_in_dim(dconvw_all[:, :CONV_K, :], me * HEAD_DIM, HEAD_DIM, axis=2)
    r_convw = _adamw([(dcw_mine, N_DEV)], conv_w[0], m_conv_w[0], v_conv_w[0], "adamw_conv_w", stacked=True)

    d_ada_all = small_all[:, 0:3, :].reshape(N_DEV, 3 * D_MODEL)
    d_ada_cols = lax.dynamic_slice_in_dim(d_ada_all, me * (3 * D_MODEL // N_DEV), 3 * D_MODEL // N_DEV, axis=1)
    g_wada = _ada_bwd(c_all, d_ada_cols)
    r_ada = _adamw(g_wada, w_ada[0], m_w_ada[0], v_w_ada[0], "adamw_w_ada", stacked=False)
    pres, lands = _exchange_chips_wait(s_sems, r_sems, pre_thru, land_thru, r_ada[1], "exchange_chips_wait")
    terms = [[(p, 1), (l, len(CHIP_K))] for p, l in zip(pres, lands)]
    r_win = [jnp.transpose(r) for r in _adamw(terms[0], w_in_t, m_w_in_t, v_w_in_t, "adamw_w_in", stacked=True)]
    r_wap = _adamw(terms[1], w_attn_proj[0], m_w_attn_proj[0], v_w_attn_proj[0], "adamw_w_attn_proj", stacked=True)
    r_wcp = _adamw(terms[2], w_conv_proj[0], m_w_conv_proj[0], v_w_conv_proj[0], "adamw_w_conv_proj", stacked=True)
    r_wout = _adamw(terms[3], w_out[0], m_w_out[0], v_w_out[0], "adamw_w_out", stacked=True)

    outs = [loss_row[0, 0], gx[None]]
    for k in range(4):
        b_ada_k, norm_w_k, b_gate_k, qn_k, kn_k, conv_b_k, ln_w_k, ln_b_k = r_small[k]
        outs += [r_ada[k][None], b_ada_k, norm_w_k, r_win[k][None], b_gate_k, qn_k, kn_k, r_wap[k][None],
                 r_convw[k][None], conv_b_k, ln_w_k, ln_b_k, r_wcp[k][None], r_wout[k][None]]
    return tuple(outs)
```

```python
import functools

import jax
import jax.numpy as jnp
from jax import lax
from jax.experimental import pallas as pl
from jax.experimental.pallas import tpu as pltpu

F32 = jnp.float32
BF16 = jnp.bfloat16
SDS = jax.ShapeDtypeStruct
MESH = pl.DeviceIdType.MESH

N_DEV = 8
D_MODEL = 1024
HEAD_DIM = 64
N_GROUPS = 3
DILATIONS = (1, 4, 16)
BAND = 128
BWD_UNROLL = 8
ATTN_W = 512
CONV_W = 512
CONV_K = 31
CONV_HALO = 32
IN_W = 8704
SHARD_W = IN_W // N_DEV
PAIR_W = 2 * SHARD_W
Q0, K0, V0, ZA0, U0, ZC0, G0 = 0, 1536, 3072, 4608, 5120, 6144, 6656
EPS = 1e-6
LANE = 128
VMEM_LIMIT = 56 * 1024 * 1024

ADAM_LR, ADAM_B1, ADAM_B2, ADAM_EPS, ADAM_WD, ADAM_STEP = 0.001, 0.9, 0.999, 1e-08, 0.01, 10

CONVW_FLAT = 2048


def _params(**kw):
    return pltpu.CompilerParams(vmem_limit_bytes=VMEM_LIMIT, **kw)


def _sigmoid(z):
    return 0.5 * jnp.tanh(0.5 * z) + 0.5


def _dot(a, b):
    return jnp.dot(a, b, preferred_element_type=F32)


def _dot_nt(a, b):
    return lax.dot_general(a, b, (((1,), (1,)), ((), ())), preferred_element_type=F32)


def _dot_tn(a, b):
    return lax.dot_general(a, b, (((0,), (0,)), ((), ())), preferred_element_type=F32)


def _peer(x, y, c, k):
    px = 1 - x if (k >> 2) & 1 else x
    py = 1 - y if (k >> 1) & 1 else y
    pc = 1 - c if k & 1 else c
    return (px, py, pc), 4 * px + 2 * py + pc


def _all_gather(arrays, name, vmem):
    n = len(arrays)
    space = pltpu.VMEM if vmem else pl.ANY

    def body(*refs):
        ins, outs = refs[:n], refs[n:2 * n]
        send_sems, recv_sems, local_sems = refs[2 * n:]
        x, y, c = lax.axis_index("x"), lax.axis_index("y"), lax.axis_index("c")
        me = 4 * x + 2 * y + c
        locals_ = [pltpu.make_async_copy(ins[a], outs[a].at[me], local_sems.at[a]) for a in range(n)]
        for cp in locals_:
            cp.start()
        sends = []
        for k in range(1, N_DEV):
            peer, _ = _peer(x, y, c, k)
            for a in range(n):
                cp = pltpu.make_async_remote_copy(
                    src_ref=ins[a], dst_ref=outs[a].at[me], send_sem=send_sems.at[a, k - 1],
                    recv_sem=recv_sems.at[a, k - 1], device_id=peer, device_id_type=MESH)
                cp.start()
                sends.append(cp)
        for k in range(1, N_DEV):
            peer, pidx = _peer(x, y, c, k)
            for a in range(n):
                pltpu.make_async_remote_copy(
                    src_ref=ins[a], dst_ref=outs[a].at[pidx], send_sem=send_sems.at[a, k - 1],
                    recv_sem=recv_sems.at[a, k - 1], device_id=peer, device_id_type=MESH).wait_recv()
        for cp in sends:
            cp.wait_send()
        for cp in locals_:
            cp.wait()

    return pl.pallas_call(
        body, name=name,
        out_shape=[SDS((N_DEV,) + a.shape, a.dtype) for a in arrays],
        in_specs=[pl.BlockSpec(memory_space=space)] * n,
        out_specs=[pl.BlockSpec(memory_space=space)] * n,
        scratch_shapes=[pltpu.SemaphoreType.DMA((n, N_DEV - 1)), pltpu.SemaphoreType.DMA((n, N_DEV - 1)),
                        pltpu.SemaphoreType.DMA((n,))],
        compiler_params=_params(),
    )(*arrays)


CHIP_K = (2, 4, 6)


def _all_gather_chips(arrays, name):
    n = len(arrays)
    k_y, k_x, k_d = CHIP_K

    def body(*refs):
        ins, outs = refs[:n], refs[n:2 * n]
        send_sems, recv_sems, local_sems = refs[2 * n:]
        x, y, c = lax.axis_index("x"), lax.axis_index("y"), lax.axis_index("c")
        me = 4 * x + 2 * y + c
        sib, sib_idx = _peer(x, y, c, 1)
        nbr_y, idx_y = _peer(x, y, c, k_y)
        nbr_x, idx_x = _peer(x, y, c, k_x)
        _, idx_d = _peer(x, y, c, k_d)

        def copy(a, slot, block, to, src=None):
            return pltpu.make_async_remote_copy(
                src_ref=outs[a].at[block] if src is None else src, dst_ref=outs[a].at[block],
                send_sem=send_sems.at[a, slot], recv_sem=recv_sems.at[a, slot], device_id=to, device_id_type=MESH)

        locals_ = [pltpu.make_async_copy(ins[a], outs[a].at[me], local_sems.at[a]) for a in range(n)]
        for cp in locals_:
            cp.start()
        for a in range(n):
            copy(a, 0, me, sib, src=ins[a]).start()
            copy(a, 1, me, nbr_y, src=ins[a]).start()
            copy(a, 2, me, nbr_x, src=ins[a]).start()

        def arrived(slot, block, frm, send_on_to=None):
            for a in range(n):
                copy(a, slot, block, frm).wait_recv()
                if send_on_to is not None:
                    copy(a, 3, block, send_on_to).start()
                copy(a, 3 + slot, block, sib).start()

        @pl.when(c == 0)
        def _():
            arrived(1, idx_y, nbr_y, send_on_to=nbr_x)
            arrived(2, idx_x, nbr_x)

        @pl.when(c == 1)
        def _():
            arrived(2, idx_x, nbr_x, send_on_to=nbr_y)
            arrived(1, idx_y, nbr_y)

        arrived(3, idx_d, nbr_x)
        for a in range(n):
            copy(a, 0, sib_idx, sib).wait_recv()
        for slot, k in ((4, k_y), (5, k_x), (6, k_d)):
            _, pidx = _peer(x, y, 1 - c, k)
            for a in range(n):
                copy(a, slot, pidx, sib).wait_recv()
        for slot in range(N_DEV - 1):
            for a in range(n):
                copy(a, slot, me, sib).wait_send()
        for cp in locals_:
            cp.wait()

    return pl.pallas_call(
        body, name=name,
        out_shape=[SDS((N_DEV,) + a.shape, a.dtype) for a in arrays],
        in_specs=[pl.BlockSpec(memory_space=pl.ANY)] * n,
        out_specs=[pl.BlockSpec(memory_space=pl.ANY)] * n,
        scratch_shapes=[pltpu.SemaphoreType.DMA((n, N_DEV - 1)), pltpu.SemaphoreType.DMA((n, N_DEV - 1)),
                        pltpu.SemaphoreType.DMA((n,))],
        compiler_params=_params(),
    )(*arrays)


def _exchange_sibling(arrays, name):
    n = len(arrays)
    ks = (0,) + CHIP_K

    def body(*refs):
        ins, outs = refs[:n], refs[n:2 * n]
        send_sems, recv_sems = refs[2 * n:]
        x, y, c = lax.axis_index("x"), lax.axis_index("y"), lax.axis_index("c")
        sib, sib_idx = _peer(x, y, c, 1)
        sends = []
        for i, k in enumerate(ks):
            _, tgt = _peer(x, y, 1 - c, k) if k else (None, sib_idx)
            for a in range(n):
                cp = pltpu.make_async_remote_copy(
                    src_ref=ins[a].at[tgt], dst_ref=outs[a].at[i], send_sem=send_sems.at[a, i],
                    recv_sem=recv_sems.at[a, i], device_id=sib, device_id_type=MESH)
                cp.start()
                sends.append(cp)
        for cp in sends:
            cp.wait_recv()
        for cp in sends:
            cp.wait_send()

    return pl.pallas_call(
        body, name=name,
        out_shape=[SDS((len(ks),) + a.shape[1:], a.dtype) for a in arrays],
        in_specs=[pl.BlockSpec(memory_space=pl.ANY)] * n,
        out_specs=[pl.BlockSpec(memory_space=pl.ANY)] * n,
        scratch_shapes=[pltpu.SemaphoreType.DMA((n, len(ks))), pltpu.SemaphoreType.DMA((n, len(ks)))],
        compiler_params=_params(),
    )(*arrays)


def _presum(mine, from_sib, me_arr, name):
    _, rows, cols = mine.shape
    tr = _row_tile(rows)
    ns = 1 + len(CHIP_K)

    def body(me_ref, a_ref, b_ref, o_ref):
        del me_ref
        o_ref[...] = (a_ref[...].astype(F32) + b_ref[...].astype(F32)).astype(o_ref.dtype)

    grid_spec = pltpu.PrefetchScalarGridSpec(
        num_scalar_prefetch=1, grid=(ns, rows // tr),
        in_specs=[pl.BlockSpec((1, tr, cols), lambda j, i, me: (jnp.bitwise_xor(me[0], 2 * j), i, 0)),
                  pl.BlockSpec((1, tr, cols), lambda j, i, me: (j, i, 0))],
        out_specs=pl.BlockSpec((1, tr, cols), lambda j, i, me: (j, i, 0)))
    return pl.pallas_call(body, name=name, grid_spec=grid_spec, out_shape=SDS((ns, rows, cols), mine.dtype),
                          compiler_params=_params())(me_arr, mine, from_sib)


HBM_SPEC = pl.BlockSpec(memory_space=pltpu.HBM)
SEM_SPEC = pl.BlockSpec(memory_space=pltpu.SEMAPHORE)
SIDE_EFFECT = pltpu.SideEffectType.DATAFLOW_SIDE_EFFECTING


def _chips_copies(pre_refs, land_refs, send_sems, recv_sems):
    x, y, c = lax.axis_index("x"), lax.axis_index("y"), lax.axis_index("c")
    copies = []
    for j, k in enumerate(CHIP_K):
        peer, _ = _peer(x, y, c, k)
        for a in range(len(pre_refs)):
            copies.append(pltpu.make_async_remote_copy(
                src_ref=pre_refs[a].at[1 + j], dst_ref=land_refs[a].at[j], send_sem=send_sems.at[a * len(CHIP_K) + j],
                recv_sem=recv_sems.at[a * len(CHIP_K) + j], device_id=peer, device_id_type=MESH))
    return copies


def _exchange_chips_start(presums, name):
    n = len(presums)

    def body(*refs):
        pre, land = refs[:n], refs[n:2 * n]
        send_sems, recv_sems = refs[2 * n], refs[2 * n + 1]
        token = refs[-1]
        for cp in _chips_copies(pre, land, send_sems, recv_sems):
            cp.start()
        token[...] = jnp.zeros_like(token)

    nk = len(CHIP_K)
    hbm = [pltpu.HBM(p.shape, p.dtype) for p in presums]
    hbm_land = [pltpu.HBM((nk,) + p.shape[1:], p.dtype) for p in presums]
    res = pl.pallas_call(
        body, name=name,
        out_shape=(pltpu.SemaphoreType.DMA((n * nk,)), pltpu.SemaphoreType.DMA((n * nk,)), *hbm, *hbm_land, SDS((8, LANE), F32)),
        in_specs=[HBM_SPEC] * (2 * n),
        out_specs=(SEM_SPEC, SEM_SPEC, *([HBM_SPEC] * (2 * n)), pl.BlockSpec(memory_space=pltpu.VMEM)),
        input_output_aliases={i: 2 + i for i in range(2 * n)},
        compiler_params=pltpu.CompilerParams(has_side_effects=SIDE_EFFECT),
    )(*[pltpu.with_memory_space_constraint(p, pltpu.HBM) for p in presums],
      *[pltpu.with_memory_space_constraint(lax.empty((nk,) + p.shape[1:], p.dtype), pltpu.HBM) for p in presums])
    return res[0], res[1], res[2:2 + n], res[2 + n:2 + 2 * n], res[-1]


def _exchange_chips_wait(send_sems, recv_sems, pre_thru, land_thru, after, name):
    n = len(pre_thru)

    def body(*refs):
        pre, land = refs[:n], refs[n:2 * n]
        s_sems, r_sems = refs[2 * n], refs[2 * n + 1]
        for cp in _chips_copies(pre, land, s_sems, r_sems):
            cp.wait_send()
            cp.wait_recv()

    hbm = [pltpu.HBM(p.shape, p.dtype) for p in (*pre_thru, *land_thru)]
    res = pl.pallas_call(
        body, name=name, out_shape=tuple(hbm),
        in_specs=[HBM_SPEC] * (2 * n) + [SEM_SPEC, SEM_SPEC, pl.BlockSpec(memory_space=pl.ANY)],
        out_specs=tuple([HBM_SPEC] * (2 * n)),
        input_output_aliases={i: i for i in range(2 * n)},
        compiler_params=pltpu.CompilerParams(has_side_effects=SIDE_EFFECT),
    )(*pre_thru, *land_thru, send_sems, recv_sems, after)
    return res[:n], res[n:]


def _exchange_chips(presums, name):
    n = len(presums)
    nk = len(CHIP_K)

    def body(*refs):
        pre, land = refs[:n], refs[n:2 * n]
        send_sems, recv_sems = refs[2 * n:]
        copies = _chips_copies(pre, land, send_sems, recv_sems)
        for cp in copies:
            cp.start()
        for cp in copies:
            cp.wait_recv()
        for cp in copies:
            cp.wait_send()

    return pl.pallas_call(
        body, name=name,
        out_shape=[SDS((nk,) + p.shape[1:], p.dtype) for p in presums],
        in_specs=[pl.BlockSpec(memory_space=pl.ANY)] * n,
        out_specs=[pl.BlockSpec(memory_space=pl.ANY)] * n,
        scratch_shapes=[pltpu.SemaphoreType.DMA((n * nk,)), pltpu.SemaphoreType.DMA((n * nk,))],
        compiler_params=_params(),
    )(*presums)


def _cast_bf16(w, name):
    def body(w_ref, o_ref):
        o_ref[...] = w_ref[...].astype(BF16)

    return pl.pallas_call(body, name=name, out_shape=SDS(w.shape, BF16), compiler_params=_params())(w)


def _cols_from_slots(wg, name):
    _, rows, cols = wg.shape

    def body(w_ref, o_ref):
        for j in range(N_DEV):
            o_ref[:, j * cols:(j + 1) * cols] = w_ref[j]

    return pl.pallas_call(body, name=name, out_shape=SDS((rows, N_DEV * cols), wg.dtype), compiler_params=_params())(wg)


def _ada_fwd(c_all, w_ada):
    def body(c_ref, w_ref, o_ref):
        cv = c_ref[...]
        sc = (cv * _sigmoid(cv)).astype(BF16)
        o_ref[...] = _dot(sc, w_ref[...].astype(BF16))

    return pl.pallas_call(body, name="ada_fwd", out_shape=SDS((N_DEV, w_ada.shape[1]), F32),
                          compiler_params=_params())(c_all, w_ada)


def _ada_bwd(c_all, d_ada_cols):
    def body(c_ref, d_ref, o_ref):
        cv = c_ref[...]
        sc = (cv * _sigmoid(cv)).astype(BF16)
        o_ref[...] = _dot_tn(sc, d_ref[...].astype(BF16))

    return pl.pallas_call(body, name="ada_bwd", out_shape=SDS((D_MODEL, d_ada_cols.shape[1]), F32),
                          compiler_params=_params())(c_all, d_ada_cols)


def _norm_fwd(x, norm_w, scale, shift):
    s = x.shape[0]
    tr = 512

    def body(x_ref, nw_ref, sc_ref, sh_ref, h_ref, ht_ref):
        xv = x_ref[...]
        r = lax.rsqrt(jnp.mean(xv * xv, axis=-1, keepdims=True) + EPS)
        h = (xv * r * nw_ref[...]) * (1.0 + sc_ref[...]) + sh_ref[...]
        h_ref[...] = h.astype(BF16)
        ht_ref[...] = h.T.astype(BF16)

    vec = pl.BlockSpec((1, D_MODEL), lambda i: (0, 0))
    return pl.pallas_call(
        body, name="norm_fwd", grid=(s // tr,),
        in_specs=[pl.BlockSpec((tr, D_MODEL), lambda i: (i, 0)), vec, vec, vec],
        out_specs=[pl.BlockSpec((tr, D_MODEL), lambda i: (i, 0)), pl.BlockSpec((D_MODEL, tr), lambda i: (0, i))],
        out_shape=[SDS((s, D_MODEL), BF16), SDS((D_MODEL, s), BF16)], compiler_params=_params(),
    )(x, norm_w, scale, shift)


def _mm_in(h, wt):
    s = h.shape[0]
    tm = 512

    def body(h_ref, w_ref, o_ref):
        o_ref[...] = _dot_nt(h_ref[...], w_ref[...])

    return pl.pallas_call(
        body, name="mm_in", grid=(IN_W // PAIR_W, s // tm),
        in_specs=[pl.BlockSpec((tm, D_MODEL), lambda p, m: (m, 0)),
                  pl.BlockSpec((PAIR_W, D_MODEL), lambda p, m: (p, 0))],
        out_specs=pl.BlockSpec((tm, PAIR_W), lambda p, m: (m, p)),
        out_shape=SDS((s, IN_W), F32), compiler_params=_params(),
    )(h, wt)


def _head_ones():
    a = lax.broadcasted_iota(jnp.int32, (LANE, LANE), 0) // HEAD_DIM
    b = lax.broadcasted_iota(jnp.int32, (LANE, LANE), 1) // HEAD_DIM
    return (a == b).astype(BF16)


def _head_sums(t, ones):
    return _dot(t.astype(BF16), ones)


def _band_bias(bias, transposed=False):
    qi = lax.broadcasted_iota(jnp.int32, (2 * BAND, 2 * BAND), 1 if transposed else 0) % BAND
    kj = lax.broadcasted_iota(jnp.int32, (2 * BAND, 2 * BAND), 0 if transposed else 1)
    dist = qi + BAND - kj
    valid = (dist >= 0) & (dist <= BAND)
    bias[1] = jnp.where(valid, 0.0, -1e30)
    bias[0] = jnp.where(valid & (kj >= BAND), 0.0, -1e30)


def _token_rows(j, d, chunk, per_r):
    return pl.ds(j // per_r + (j % per_r) * (chunk * d), chunk, stride=d)


def _deinterleave(src_ref, dst_ref, w_ref, ones, d, sub_len, chunk, scale, dst_off):
    per_r = sub_len // chunk

    def step(j, _):
        t = src_ref[_token_rows(j, d, chunk, per_r), :]
        if w_ref is not None:
            ms = _head_sums(t * t, ones) * (1.0 / HEAD_DIM)
            t = t * lax.rsqrt(ms + EPS) * (w_ref[...] * scale)
        dst_ref[pl.ds(pl.multiple_of(dst_off + j * chunk, BAND), chunk), :] = t.astype(dst_ref.dtype)
        return 0
    lax.fori_loop(0, d * per_r, step, 0, unroll=4)


def _attn_fwd(proj, qw2, kw2, g):
    s = proj.shape[0]
    d = DILATIONS[g]
    sub_len = s // d
    nb = sub_len // BAND
    chunk = min(sub_len, 256)

    def body(q_ref, k_ref, v_ref, qw_ref, kw_ref, o_ref, l_ref, qd, kd, vd, od, ld, bias):
        lo = lax.broadcasted_iota(jnp.int32, (1, LANE), 1) < HEAD_DIM
        ones = _head_ones()

        @pl.when(pl.program_id(0) == 0)
        def _():
            _band_bias(bias)

        kd[0:BAND, :] = jnp.zeros((BAND, LANE), BF16)
        vd[0:BAND, :] = jnp.zeros((BAND, LANE), BF16)
        _deinterleave(q_ref, qd, qw_ref, ones, d, sub_len, chunk, HEAD_DIM ** -0.5, 0)
        _deinterleave(k_ref, kd, kw_ref, ones, d, sub_len, chunk, 1.0, BAND)
        _deinterleave(v_ref, vd, None, ones, d, sub_len, chunk, 1.0, BAND)

        def block(t, _):
            base = pl.multiple_of(t * BAND, BAND)
            q = qd[pl.ds(base, BAND), :]
            k2 = kd[pl.ds(base, 2 * BAND), :]
            v2 = vd[pl.ds(base, 2 * BAND), :]
            zero = jnp.zeros_like(q)
            qs = jnp.concatenate([jnp.where(lo, q, zero), jnp.where(lo, zero, q)], axis=0)
            sc = _dot_nt(qs, k2) + bias[jnp.minimum(t % nb, 1)]
            m = jnp.max(sc, axis=-1, keepdims=True)
            p = jnp.exp(sc - m)
            den = jnp.sum(p, axis=-1, keepdims=True)
            u = _dot(p.astype(BF16), v2) * (1.0 / den)
            lse = m + jnp.log(den)
            od[pl.ds(base, BAND), :] = jnp.where(lo, u[:BAND], u[BAND:])
            ld[pl.ds(base, BAND), :] = jnp.where(lo, lse[:BAND], lse[BAND:])
            return 0
        lax.fori_loop(0, s // BAND, block, 0, unroll=16)

        per_r = sub_len // chunk

        def back(j, _):
            src = pl.ds(pl.multiple_of(j * chunk, chunk), chunk)
            dst = _token_rows(j, d, chunk, per_r)
            o_ref[dst, :] = od[src, :]
            l_ref[dst, :] = ld[src, :]
            return 0
        lax.fori_loop(0, d * per_r, back, 0, unroll=2)

    col = lambda off: pl.BlockSpec((s, LANE), lambda hp, off=off: (0, off // LANE + 4 * g + hp))
    vec = pl.BlockSpec((1, LANE), lambda hp: (0, 0))
    out = pl.BlockSpec((s, LANE), lambda hp: (0, hp))
    return pl.pallas_call(
        body, name=f"attn_fwd{g}", grid=(ATTN_W // LANE,),
        in_specs=[col(Q0), col(K0), col(V0), vec, vec], out_specs=[out, out],
        out_shape=[SDS((s, ATTN_W), F32), SDS((s, ATTN_W), F32)],
        scratch_shapes=[pltpu.VMEM((s, LANE), BF16), pltpu.VMEM((s + BAND, LANE), BF16), pltpu.VMEM((s + BAND, LANE), BF16),
                        pltpu.VMEM((s, LANE), F32), pltpu.VMEM((s, LANE), F32),
                        pltpu.VMEM((2, 2 * BAND, 2 * BAND), F32)],
        compiler_params=_params(),
    )(proj, proj, proj, qw2, kw2)


def _attn_bwd(proj, da, lse_delta, qw2, kw2, dproj, g):
    s = proj.shape[0]
    d = DILATIONS[g]
    sub_len = s // d
    nb = sub_len // BAND
    chunk = min(sub_len, 256)

    def body(q_ref, k_ref, v_ref, da_ref, ld_ref, qw_ref, kw_ref, dp_in, dp_out, dqw_ref, dkw_ref,
             qd, kd, vd, kdt, dad, lst, dlt, dqt, dqd, dkd, dvd, st, stb, bias_t, wacc, sem):
        del dp_in
        hp = pl.program_id(0)
        lo = lax.broadcasted_iota(jnp.int32, (1, LANE), 1) < HEAD_DIM
        row_lo = lax.broadcasted_iota(jnp.int32, (LANE, 1), 0) < HEAD_DIM
        ones = _head_ones()
        per_r = sub_len // chunk
        cblk = chunk // BAND

        @pl.when(hp == 0)
        def _():
            _band_bias(bias_t, transposed=True)

        kd[0:BAND, :] = jnp.zeros((BAND, LANE), BF16)
        vd[0:BAND, :] = jnp.zeros((BAND, LANE), BF16)
        kdt[0] = jnp.zeros((LANE, BAND), BF16)
        _deinterleave(q_ref, qd, qw_ref, ones, d, sub_len, chunk, HEAD_DIM ** -0.5, 0)

        def k_step(j, _):
            t = k_ref[_token_rows(j, d, chunk, per_r), :]
            t = t * lax.rsqrt(_head_sums(t * t, ones) * (1.0 / HEAD_DIM) + EPS) * kw_ref[...]
            kd[pl.ds(pl.multiple_of(BAND + j * chunk, BAND), chunk), :] = t.astype(BF16)
            tt = t.T.astype(BF16)
            for u in range(cblk):
                kdt[1 + j * cblk + u] = tt[:, u * BAND:(u + 1) * BAND]
            return 0
        lax.fori_loop(0, d * per_r, k_step, 0, unroll=4)
        _deinterleave(v_ref, vd, None, ones, d, sub_len, chunk, 1.0, BAND)
        _deinterleave(da_ref, dad, None, ones, d, sub_len, chunk, 1.0, 0)

        def rows_step(j, _):
            tok = _token_rows(j, d, chunk, per_r)
            tt = ld_ref[tok, :].T
            for u in range(cblk):
                cols = slice(u * BAND, (u + 1) * BAND)
                lst[j * cblk + u, 0:1, :] = tt[0:1, cols]
                lst[j * cblk + u, 1:2, :] = tt[HEAD_DIM:HEAD_DIM + 1, cols]
                dlt[j * cblk + u, 0:1, :] = tt[HEAD_DIM // 2:HEAD_DIM // 2 + 1, cols]
                dlt[j * cblk + u, 1:2, :] = tt[HEAD_DIM + HEAD_DIM // 2:HEAD_DIM + HEAD_DIM // 2 + 1, cols]
            return 0
        lax.fori_loop(0, d * per_r, rows_step, 0, unroll=4)

        def block(t, carry):
            ck, cv = carry
            base = pl.multiple_of(t * BAND, BAND)
            q = qd[pl.ds(base, BAND), :]
            k2 = kd[pl.ds(base, 2 * BAND), :]
            v2 = vd[pl.ds(base, 2 * BAND), :]
            k2t = jnp.concatenate([kdt[t], kdt[t + 1]], axis=1)
            dav = dad[pl.ds(base, BAND), :]
            zero = jnp.zeros_like(q)
            qs = jnp.concatenate([jnp.where(lo, q, zero), jnp.where(lo, zero, q)], axis=0)
            das = jnp.concatenate([jnp.where(lo, dav, zero), jnp.where(lo, zero, dav)], axis=0)
            ls_row = jnp.concatenate([lst[t, 0:1, :], lst[t, 1:2, :]], axis=1)
            dl_row = jnp.concatenate([dlt[t, 0:1, :], dlt[t, 1:2, :]], axis=1)
            sc_t = _dot_nt(k2, qs) + bias_t[jnp.minimum(t % nb, 1)]
            p_t = jnp.exp(sc_t - ls_row)
            dp_t = _dot_nt(v2, das)
            ds_t = (p_t * (dp_t - dl_row)).astype(BF16)
            dv2 = _dot(p_t.astype(BF16), das)
            dk2 = _dot(ds_t, qs)
            dvd[pl.ds(base, BAND), :] = cv + dv2[:BAND]
            dkd[pl.ds(base, BAND), :] = ck + dk2[:BAND]
            dq_t = _dot(k2t, ds_t)
            dqt[t] = jnp.where(row_lo, dq_t[:, :BAND], dq_t[:, BAND:])
            return dk2[BAND:], dv2[BAND:]

        def blocks(i, carry):
            for u in range(BWD_UNROLL):
                carry = block(i * BWD_UNROLL + u, carry)
            return carry
        zeros = jnp.zeros((BAND, LANE), F32)
        ck, cv = lax.fori_loop(0, s // (BAND * BWD_UNROLL), blocks, (zeros, zeros))
        dkd[s:s + BAND, :] = ck
        dvd[s:s + BAND, :] = cv

        def dq_rows(t, _):
            dqd[pl.ds(pl.multiple_of(t * BAND, BAND), BAND), :] = dqt[t].T
            return 0
        lax.fori_loop(0, s // BAND, dq_rows, 0, unroll=4)

        def col_copy(slot, col0):
            return pltpu.make_async_copy(
                stb.at[slot], dp_out.at[:, pl.ds(pl.multiple_of(col0 + LANE * (4 * g + hp), LANE), LANE)], sem.at[slot])

        def store_cols(slot, col0):
            @pl.when(hp > 0)
            def _():
                col_copy(slot, col0).wait()
            stb[slot] = st[...].astype(BF16)
            col_copy(slot, col0).start()

        def norm_back(src_ref, dy_ref, dy_off, w_ref, scale, dw_ref, slot, col0):
            wacc[...] = jnp.zeros_like(wacc)

            def step(j, _):
                tok = _token_rows(j, d, chunk, per_r)
                t = src_ref[tok, :]
                dy = dy_ref[pl.ds(pl.multiple_of(dy_off + j * chunk, BAND), chunk), :]
                rr = lax.rsqrt(_head_sums(t * t, ones) * (1.0 / HEAD_DIM) + EPS)
                nrm = t * rr
                wacc[...] += jnp.sum((dy * nrm).reshape(chunk // 8, 8, LANE), axis=0)
                dn = dy * (w_ref[...] * scale)
                st[tok, :] = rr * (dn - nrm * (_head_sums(dn * nrm, ones) * (1.0 / HEAD_DIM)))
                return 0
            lax.fori_loop(0, d * per_r, step, 0, unroll=4)
            dw_ref[...] += jnp.broadcast_to(jnp.sum(wacc[...], axis=0, keepdims=True) * scale, dw_ref.shape)
            store_cols(slot, col0)

        @pl.when(hp == 0)
        def _():
            dqw_ref[...] = jnp.zeros_like(dqw_ref)
            dkw_ref[...] = jnp.zeros_like(dkw_ref)

        norm_back(q_ref, dqd, 0, qw_ref, HEAD_DIM ** -0.5, dqw_ref, 0, Q0)
        norm_back(k_ref, dkd, BAND, kw_ref, 1.0, dkw_ref, 1, K0)

        def v_back(j, _):
            src = pl.ds(pl.multiple_of(BAND + j * chunk, BAND), chunk)
            st[_token_rows(j, d, chunk, per_r), :] = dvd[src, :]
            return 0
        lax.fori_loop(0, d * per_r, v_back, 0, unroll=2)
        store_cols(2, V0)

        @pl.when(hp == ATTN_W // LANE - 1)
        def _():
            for slot, col0 in enumerate((Q0, K0, V0)):
                col_copy(slot, col0).wait()

    col = lambda off: pl.BlockSpec((s, LANE), lambda hp, off=off: (0, off // LANE + 4 * g + hp))
    mid = pl.BlockSpec((s, LANE), lambda hp: (0, hp))
    vec = pl.BlockSpec((1, LANE), lambda hp: (0, 0))
    acc = pl.BlockSpec((8, LANE), lambda hp: (0, 0))
    any_ = pl.BlockSpec(memory_space=pl.ANY)
    return pl.pallas_call(
        body, name=f"attn_bwd{g}", grid=(ATTN_W // LANE,),
        in_specs=[col(Q0), col(K0), col(V0), mid, mid, vec, vec, any_],
        out_specs=[any_, acc, acc],
        out_shape=[SDS(dproj.shape, dproj.dtype), SDS((8, LANE), F32), SDS((8, LANE), F32)],
        input_output_aliases={7: 0},
        scratch_shapes=[pltpu.VMEM((s, LANE), BF16), pltpu.VMEM((s + BAND, LANE), BF16), pltpu.VMEM((s + BAND, LANE), BF16),
                        pltpu.VMEM((s // BAND + 1, LANE, BAND), BF16), pltpu.VMEM((s, LANE), BF16),
                        pltpu.VMEM((s // BAND, 8, BAND), F32), pltpu.VMEM((s // BAND, 8, BAND), F32),
                        pltpu.VMEM((s // BAND, LANE, BAND), F32),
                        pltpu.VMEM((s, LANE), F32), pltpu.VMEM((s + BAND, LANE), F32), pltpu.VMEM((s + BAND, LANE), F32),
                        pltpu.VMEM((s, LANE), F32), pltpu.VMEM((3, s, LANE), BF16),
                        pltpu.VMEM((2, 2 * BAND, 2 * BAND), F32), pltpu.VMEM((8, LANE), F32),
                        pltpu.SemaphoreType.DMA((3,))],
        compiler_params=_params(),
    )(proj, proj, proj, da, lse_delta, qw2, kw2, dproj)


def _tap_views(ext_ref, sh_ref, offsets, tr, cols):
    for b in range(8):
        group = [j for j, o in enumerate(offsets) if o % 8 == b]
        if not group:
            continue
        first = min(offsets[j] for j in group)
        span = tr + max(offsets[j] for j in group) - first
        sh_ref[0:span, cols] = ext_ref[first:first + span, cols]
        for j in group:
            yield j, sh_ref[offsets[j] - first:offsets[j] - first + tr, cols]


def _silu_grad(z, sg):
    return sg * (1.0 + z * (1.0 - sg))


def _glu(u):
    a_h, b_h = u[:, :CONV_W], u[:, CONV_W:]
    sg = _sigmoid(b_h)
    return a_h, sg, a_h * sg


def _tail(x, tgt, proj, o3, l3, wa, wc, wo, gate, bga, bgc, convw, convb, lnw, lnb, bd):
    s = x.shape[0]
    tr = 256

    def body(x_ref, t_ref, za_ref, u_ref, uh_ref, zc_ref, g0_ref, g1_ref, g2_ref, g3_ref,
             o0_ref, o1_ref, o2_ref, l0_ref, l1_ref, l2_ref, wa_ref, wc_ref, wo_ref,
             gate_ref, bga_ref, bgc_ref, cw_ref, cb_ref, lnw_ref, lnb_ref, bd_ref,
             dout_ref, da_ref, ld_ref, dcv_ref, mt_ref, yat_ref, yct_ref, dmo_ref, dya_ref, dyc_ref, dp_ref,
             dgate_ref, dbg_ref, dlnw_ref, dlnb_ref, dcb_ref, loss_ref,
             ext, sh, st_za, st_zc, st_g, sems):
        i = pl.program_id(0)

        @pl.when(i == 0)
        def _():
            for r in (dgate_ref, dbg_ref, dlnw_ref, dlnb_ref, dcb_ref, loss_ref):
                r[...] = jnp.zeros_like(r)

        def acc_rows(ref, v):
            ref[...] += jnp.broadcast_to(jnp.sum(v, axis=0, keepdims=True), ref.shape)

        la, lb, lc = l0_ref[...], l1_ref[...], l2_ref[...]
        mx = jnp.maximum(jnp.maximum(la, lb), lc)
        ea, eb, ec = jnp.exp(la - mx), jnp.exp(lb - mx), jnp.exp(lc - mx)
        den = ea + eb + ec
        inv = 1.0 / den
        attn = (ea * inv) * o0_ref[...] + (eb * inv) * o1_ref[...] + (ec * inv) * o2_ref[...]
        lse = mx + jnp.log(den)

        za = za_ref[...]
        sga = _sigmoid(za)
        sa = za * sga
        ya_in = attn * sa
        y_attn = _dot(ya_in.astype(BF16), wa_ref[...])

        _, _, glu = _glu(u_ref[...])
        _, _, glu_h = _glu(uh_ref[...])
        ext[0:CONV_HALO, :] = jnp.where(i > 0, glu_h, 0.0)
        ext[CONV_HALO:CONV_HALO + tr, :] = glu
        cv_blocks = []
        for cb in range(CONV_W // LANE):
            cols = slice(cb * LANE, (cb + 1) * LANE)
            cv_c = jnp.broadcast_to(cb_ref[:, cols], (tr, LANE))
            for j, rows in _tap_views(ext, sh, [CONV_HALO - (CONV_K - 1) + j for j in range(CONV_K)], tr, cols):
                cv_c = cv_c + cw_ref[j:j + 1, cols] * rows
            cv_blocks.append(cv_c)
        cv = jnp.concatenate(cv_blocks, axis=1)
        mu = jnp.mean(cv, axis=-1, keepdims=True)
        xc = cv - mu
        rstd = lax.rsqrt(jnp.mean(xc * xc, axis=-1, keepdims=True) + EPS)
        nrm = xc * rstd
        ln = nrm * lnw_ref[...] + lnb_ref[...]
        sgl = _sigmoid(ln)
        cs = ln * sgl
        zc = zc_ref[...]
        sgc = _sigmoid(zc)
        scz = zc * sgc
        yc_in = cs * scz
        y_conv = _dot(yc_in.astype(BF16), wc_ref[...])

        ga = _sigmoid(jnp.concatenate([g0_ref[...], g1_ref[...]], axis=1) + bga_ref[...])
        gc = _sigmoid(jnp.concatenate([g2_ref[...], g3_ref[...]], axis=1) + bgc_ref[...])
        merged = ga * y_attn + gc * y_conv
        mo = _dot(merged.astype(BF16), wo_ref[...])
        gate_v = gate_ref[...]
        err = (x_ref[...] + gate_v * mo) - t_ref[...]
        loss_ref[...] += 0.5 * jnp.sum(jnp.mean(err * err, axis=-1, keepdims=True))
        d_out = err * (1.0 / D_MODEL)
        dout_ref[...] = d_out

        rows = pl.ds(pl.multiple_of(i * tr, tr), tr)
        cps = [pltpu.make_async_copy(st_za, dp_ref.at[rows, pl.ds(ZA0, ATTN_W)], sems.at[0]),
               pltpu.make_async_copy(st_zc, dp_ref.at[rows, pl.ds(ZC0, CONV_W)], sems.at[1]),
               pltpu.make_async_copy(st_g, dp_ref.at[rows, pl.ds(G0, 2 * D_MODEL)], sems.at[2])]

        @pl.when(i > 0)
        def _():
            for cp in cps:
                cp.wait()

        acc_rows(dgate_ref, d_out * mo)
        dmo_b = (d_out * gate_v).astype(BF16)
        dmo_ref[...] = dmo_b
        mt_ref[...] = merged.T.astype(BF16)
        d_merged = _dot_nt(dmo_b, wo_ref[...])
        d_ya = (d_merged * ga).astype(BF16)
        d_yc = (d_merged * gc).astype(BF16)
        dya_ref[...] = d_ya
        dyc_ref[...] = d_yc
        dga = d_merged * y_attn * (ga * (1.0 - ga))
        dgc = d_merged * y_conv * (gc * (1.0 - gc))
        dgs = jnp.concatenate([dga, dgc], axis=1)
        acc_rows(dbg_ref, dgs)
        st_g[...] = dgs.astype(BF16)

        yat_ref[...] = ya_in.T.astype(BF16)
        d_ya_in = _dot_nt(d_ya, wa_ref[...])
        d_attn = d_ya_in * sa
        da_ref[...] = d_attn
        st_za[...] = (d_ya_in * attn * _silu_grad(za, sga)).astype(BF16)
        prod = d_attn * attn
        hi = prod.astype(BF16)
        lo_ = (prod - hi.astype(F32)).astype(BF16)
        delta = _dot(hi, bd_ref[...]) + _dot(lo_, bd_ref[...])
        first_half = (lax.broadcasted_iota(jnp.int32, (1, ATTN_W), 1) % HEAD_DIM) < HEAD_DIM // 2
        ld_ref[...] = jnp.where(first_half, lse, delta)

        yct_ref[...] = yc_in.T.astype(BF16)
        d_yc_in = _dot_nt(d_yc, wc_ref[...])
        st_zc[...] = (d_yc_in * cs * _silu_grad(zc, sgc)).astype(BF16)
        d_ln = (d_yc_in * scz) * _silu_grad(ln, sgl)
        acc_rows(dlnw_ref, d_ln * nrm)
        acc_rows(dlnb_ref, d_ln)
        d_nrm = d_ln * lnw_ref[...]
        d_cv = rstd * (d_nrm - jnp.mean(d_nrm, axis=-1, keepdims=True)
                       - nrm * jnp.mean(d_nrm * nrm, axis=-1, keepdims=True))
        acc_rows(dcb_ref, d_cv)
        dcv_ref[...] = d_cv

        for cp in cps:
            cp.start()

        @pl.when(i == s // tr - 1)
        def _():
            for cp in cps:
                cp.wait()

    def rows(width, colblk=0):
        return pl.BlockSpec((tr, width), lambda i, colblk=colblk: (i, colblk))

    def const(shape):
        return pl.BlockSpec(shape, lambda i: (0,) * len(shape))

    halo = pl.BlockSpec((CONV_HALO, D_MODEL), lambda i: (jnp.maximum(i * (tr // CONV_HALO) - 1, 0), U0 // D_MODEL))
    in_specs = [rows(D_MODEL), rows(D_MODEL), rows(ATTN_W, ZA0 // ATTN_W), rows(D_MODEL, U0 // D_MODEL), halo,
                rows(CONV_W, ZC0 // CONV_W)]
    in_specs += [rows(512, G0 // 512 + j) for j in range(4)]
    in_specs += [rows(ATTN_W)] * 6
    in_specs += [const(wa.shape), const(wc.shape), const(wo.shape), const((1, D_MODEL)), const((1, D_MODEL)),
                 const((1, D_MODEL)), const(convw.shape), const((1, CONV_W)), const((1, CONV_W)), const((1, CONV_W)),
                 const(bd.shape)]
    tcol = lambda width: pl.BlockSpec((width, tr), lambda i: (0, i))
    out_specs = [rows(D_MODEL), rows(ATTN_W), rows(ATTN_W), rows(CONV_W),
                 tcol(D_MODEL), tcol(ATTN_W), tcol(CONV_W), rows(D_MODEL), rows(D_MODEL), rows(D_MODEL),
                 pl.BlockSpec(memory_space=pl.ANY),
                 const((8, D_MODEL)), const((8, 2 * D_MODEL)), const((8, CONV_W)), const((8, CONV_W)), const((8, CONV_W)),
                 const((8, LANE))]
    out_shape = [SDS((s, D_MODEL), F32), SDS((s, ATTN_W), F32), SDS((s, ATTN_W), F32),
                 SDS((s, CONV_W), F32),
                 SDS((D_MODEL, s), BF16), SDS((ATTN_W, s), BF16), SDS((CONV_W, s), BF16),
                 SDS((s, D_MODEL), BF16), SDS((s, D_MODEL), BF16), SDS((s, D_MODEL), BF16),
                 SDS((s, IN_W), BF16),
                 SDS((8, D_MODEL), F32), SDS((8, 2 * D_MODEL), F32), SDS((8, CONV_W), F32), SDS((8, CONV_W), F32),
                 SDS((8, CONV_W), F32), SDS((8, LANE), F32)]
    return pl.pallas_call(
        body, name="tail", grid=(s // tr,), in_specs=in_specs, out_specs=out_specs, out_shape=out_shape,
        scratch_shapes=[pltpu.VMEM((CONV_HALO + tr, CONV_W), F32), pltpu.VMEM((CONV_HALO + tr, CONV_W), F32),
                        pltpu.VMEM((tr, ATTN_W), BF16),
                        pltpu.VMEM((tr, CONV_W), BF16), pltpu.VMEM((tr, 2 * D_MODEL), BF16),
                        pltpu.SemaphoreType.DMA((3,))],
        compiler_params=_params(),
    )(x, tgt, proj, proj, proj, proj, proj, proj, proj, proj, *o3, *l3, wa, wc, wo, gate, bga, bgc,
      convw, convb, lnw, lnb, bd)


def _conv_bwd(dcv, proj, convw, dproj):
    s = dcv.shape[0]
    tr = 128
    nt = s // tr

    def body(dcv_ref, dcvn_ref, u_ref, uh_ref, cw_ref, dp_in, dp_out, dw_ref, extg, extd, sh):
        del dp_in
        i = pl.program_id(0)

        @pl.when(i == 0)
        def _():
            dw_ref[...] = jnp.zeros_like(dw_ref)

        _, _, glu = _glu(u_ref[...])
        _, _, glu_h = _glu(uh_ref[...])
        extg[0:CONV_HALO, :] = jnp.where(i > 0, glu_h, 0.0)
        extg[CONV_HALO:CONV_HALO + tr, :] = glu
        extd[0:tr, :] = dcv_ref[...]
        extd[tr:tr + CONV_HALO, :] = jnp.where(i < nt - 1, dcvn_ref[...], 0.0)
        for cb in range(CONV_W // LANE):
            cols = slice(cb * LANE, (cb + 1) * LANE)
            dglu = jnp.zeros((tr, LANE), F32)
            for j, rows in _tap_views(extd, sh, [CONV_K - 1 - j for j in range(CONV_K)], tr, cols):
                dglu = dglu + cw_ref[j:j + 1, cols] * rows
            dcv_c = dcv_ref[:, cols]
            for j, rows in _tap_views(extg, sh, [CONV_HALO - (CONV_K - 1) + j for j in range(CONV_K)], tr, cols):
                dw_ref[8 * j:8 * j + 8, cols] += jnp.sum((dcv_c * rows).reshape(tr // 8, 8, LANE), axis=0)
            a_h = u_ref[:, cols]
            sgb = _sigmoid(u_ref[:, CONV_W + cb * LANE:CONV_W + (cb + 1) * LANE])
            dp_out[:, cols] = (dglu * sgb).astype(BF16)
            dp_out[:, CONV_W + cb * LANE:CONV_W + (cb + 1) * LANE] = (dglu * a_h * (sgb * (1.0 - sgb))).astype(BF16)

    ucol = U0 // D_MODEL
    return pl.pallas_call(
        body, name="conv_bwd", grid=(nt,),
        in_specs=[pl.BlockSpec((tr, CONV_W), lambda i: (i, 0)),
                  pl.BlockSpec((CONV_HALO, CONV_W), lambda i: (jnp.minimum((i + 1) * (tr // CONV_HALO), s // CONV_HALO - 1), 0)),
                  pl.BlockSpec((tr, D_MODEL), lambda i: (i, ucol)),
                  pl.BlockSpec((CONV_HALO, D_MODEL), lambda i: (jnp.maximum(i * (tr // CONV_HALO) - 1, 0), ucol)),
                  pl.BlockSpec(convw.shape, lambda i: (0, 0)),
                  pl.BlockSpec(memory_space=pl.ANY)],
        out_specs=[pl.BlockSpec((tr, D_MODEL), lambda i: (i, ucol)), pl.BlockSpec((8 * CONV_HALO, CONV_W), lambda i: (0, 0))],
        out_shape=[SDS(dproj.shape, dproj.dtype), SDS((8 * CONV_HALO, CONV_W), F32)],
        input_output_aliases={5: 0},
        scratch_shapes=[pltpu.VMEM((CONV_HALO + tr, CONV_W), F32)] * 3,
        compiler_params=_params(),
    )(dcv, dcv, proj, proj, convw, dproj)


def _mm_acc(at, b, name, col_slots):
    m, s = at.shape
    n = b.shape[1]
    tk = 512
    nk = s // tk

    def body(a_ref, b_ref, o_ref, acc):
        k = pl.program_id(0)

        @pl.when(k == 0)
        def _():
            acc[...] = jnp.zeros_like(acc)

        acc[...] += _dot(a_ref[...], b_ref[...])

        @pl.when(k == nk - 1)
        def _():
            if col_slots:
                w = n // N_DEV
                for j in range(N_DEV):
                    o_ref[j] = acc[:, j * w:(j + 1) * w].astype(BF16)
            else:
                o_ref[...] = acc[...].astype(BF16)

    if col_slots:
        out_shape = SDS((N_DEV, m, n // N_DEV), BF16)
        out_spec = pl.BlockSpec((N_DEV, m, n // N_DEV), lambda k: (0, 0, 0))
    else:
        out_shape = SDS((m, n), BF16)
        out_spec = pl.BlockSpec((m, n), lambda k: (0, 0))
    return pl.pallas_call(
        body, name=name, grid=(nk,),
        in_specs=[pl.BlockSpec((m, tk), lambda k: (0, k)), pl.BlockSpec((tk, n), lambda k: (k, 0))],
        out_specs=out_spec, out_shape=out_shape, scratch_shapes=[pltpu.VMEM((m, n), F32)],
        compiler_params=_params(),
    )(at, b)


def _mm_dw(ht, dproj):
    s = ht.shape[1]
    tk = 512
    nk = s // tk

    def body(a_ref, b_ref, o_ref, acc):
        k = pl.program_id(1)

        @pl.when(k == 0)
        def _():
            acc[...] = jnp.zeros_like(acc)

        acc[...] += _dot(a_ref[...], b_ref[...])

        @pl.when(k == nk - 1)
        def _():
            o_ref[...] = acc[...].T.astype(BF16)

    return pl.pallas_call(
        body, name="mm_dw", grid=(IN_W // PAIR_W, nk),
        in_specs=[pl.BlockSpec((D_MODEL, tk), lambda p, k: (0, k)), pl.BlockSpec((tk, PAIR_W), lambda p, k: (k, p))],
        out_specs=pl.BlockSpec((PAIR_W, D_MODEL), lambda p, k: (p, 0)),
        out_shape=SDS((IN_W, D_MODEL), BF16), scratch_shapes=[pltpu.VMEM((D_MODEL, PAIR_W), F32)],
        compiler_params=_params(),
    )(ht, dproj)


def _mm_dh(dproj, wt, token):
    s = dproj.shape[0]
    tm = 1024

    def body(dp_ref, w_ref, tok_ref, o_ref):
        del tok_ref
        p = pl.program_id(1)
        part = _dot(dp_ref[...], w_ref[...])

        @pl.when(p == 0)
        def _():
            o_ref[...] = part

        @pl.when(p > 0)
        def _():
            o_ref[...] += part

    return pl.pallas_call(
        body, name="mm_dh", grid=(s // tm, IN_W // PAIR_W),
        in_specs=[pl.BlockSpec((tm, PAIR_W), lambda m, p: (m, p)),
                  pl.BlockSpec((PAIR_W, D_MODEL), lambda m, p: (p, 0)),
                  pl.BlockSpec(token.shape, lambda m, p: (0, 0))],
        out_specs=pl.BlockSpec((tm, D_MODEL), lambda m, p: (m, 0)),
        out_shape=SDS((s, D_MODEL), F32), compiler_params=_params(),
    )(dproj, wt, token)


def _norm_bwd(x, dh, dout, norm_w, scale):
    s = x.shape[0]
    tr = 512

    def body(x_ref, dh_ref, do_ref, nw_ref, sc_ref, gx_ref, dsh_ref, dsc_ref, dnw_ref):
        i = pl.program_id(0)

        @pl.when(i == 0)
        def _():
            for r in (dsh_ref, dsc_ref, dnw_ref):
                r[...] = jnp.zeros_like(r)

        def acc_rows(ref, v):
            ref[...] += jnp.broadcast_to(jnp.sum(v, axis=0, keepdims=True), ref.shape)

        xv = x_ref[...]
        dh_v = dh_ref[...]
        r = lax.rsqrt(jnp.mean(xv * xv, axis=-1, keepdims=True) + EPS)
        xn = xv * r
        one_sc = 1.0 + sc_ref[...]
        acc_rows(dsh_ref, dh_v)
        acc_rows(dsc_ref, dh_v * (xn * nw_ref[...]))
        acc_rows(dnw_ref, dh_v * xn * one_sc)
        dxn = dh_v * (nw_ref[...] * one_sc)
        gx_ref[...] = do_ref[...] + r * (dxn - xn * jnp.mean(dxn * xn, axis=-1, keepdims=True))

    blk = pl.BlockSpec((tr, D_MODEL), lambda i: (i, 0))
    vec = pl.BlockSpec((1, D_MODEL), lambda i: (0, 0))
    acc = pl.BlockSpec((8, D_MODEL), lambda i: (0, 0))
    return pl.pallas_call(
        body, name="norm_bwd", grid=(s // tr,), in_specs=[blk, blk, blk, vec, vec],
        out_specs=[blk, acc, acc, acc],
        out_shape=[SDS((s, D_MODEL), F32)] + [SDS((8, D_MODEL), F32)] * 3, compiler_params=_params(),
    )(x, dh, dout, norm_w, scale)


SMALL_ROWS = 8
QN_COL, KN_COL, CB_COL, LOSS_COL = 0, LANE, 2 * LANE, 2 * LANE + CONV_W


def _pack_partials(dsh, dsc, dgate, dnw, dbg, dqw3, dkw3, dcb, dlnw, dlnb, loss_p):
    n3 = len(dqw3)

    def body(*refs):
        dsh_r, dsc_r, dgate_r, dnw_r, dbg_r = refs[:5]
        dq_r, dk_r = refs[5:5 + n3], refs[5 + n3:5 + 2 * n3]
        dcb_r, dlnw_r, dlnb_r, loss_r, o_ref = refs[5 + 2 * n3:]

        def both_heads(rs):
            t = rs[0][0:1, :]
            for r in rs[1:]:
                t = t + r[0:1, :]
            return t + pltpu.roll(t, HEAD_DIM, axis=1)

        o_ref[0:1, :] = dsh_r[0:1, :]
        o_ref[1:2, :] = dsc_r[0:1, :]
        o_ref[2:3, :] = dgate_r[0:1, :]
        o_ref[3:4, :] = dnw_r[0:1, :]
        o_ref[4:5, :] = dbg_r[0:1, 0:D_MODEL]
        o_ref[5:6, :] = dbg_r[0:1, D_MODEL:]
        o_ref[6:7, QN_COL:QN_COL + LANE] = both_heads(dq_r)
        o_ref[6:7, KN_COL:KN_COL + LANE] = both_heads(dk_r)
        o_ref[6:7, CB_COL:CB_COL + CONV_W] = dcb_r[0:1, :]
        o_ref[6:7, LOSS_COL:LOSS_COL + LANE] = loss_r[0:1, :]
        o_ref[6:7, LOSS_COL + LANE:] = jnp.zeros((1, D_MODEL - LOSS_COL - LANE), F32)
        o_ref[7:8, 0:CONV_W] = dlnw_r[0:1, :]
        o_ref[7:8, CONV_W:] = dlnb_r[0:1, :]

    return pl.pallas_call(body, name="pack_partials", out_shape=SDS((SMALL_ROWS, D_MODEL), F32),
                          compiler_params=_params())(dsh, dsc, dgate, dnw, dbg, *dqw3, *dkw3, dcb, dlnw, dlnb, loss_p)


def _adamw_update(g, w, m, v):
    bc1 = 1.0 - ADAM_B1 ** ADAM_STEP
    bc2 = 1.0 - ADAM_B2 ** ADAM_STEP
    m_new = ADAM_B1 * m + (1.0 - ADAM_B1) * g
    v_new = ADAM_B2 * v + (1.0 - ADAM_B2) * (g * g)
    delta = -ADAM_LR * ((m_new / bc1) / (jnp.sqrt(v_new / bc2) + ADAM_EPS) + ADAM_WD * w)
    return delta, m_new, v_new


def _adamw_small(small_all, ws, ms, vs):
    n = len(ws)
    where = [(slice(0, 3), None), (slice(3, 4), None), (slice(4, 6), None), (6, QN_COL), (6, KN_COL), (6, CB_COL),
             (7, 0), (7, CONV_W)]

    def body(*refs):
        g_ref = refs[0]
        w_r, m_r, v_r = refs[1:1 + n], refs[1 + n:1 + 2 * n], refs[1 + 2 * n:1 + 3 * n]
        outs = refs[1 + 3 * n:]
        g_o, d_o, m_o, v_o, loss_o = outs[:n], outs[n:2 * n], outs[2 * n:3 * n], outs[3 * n:4 * n], outs[4 * n]
        gsum = g_ref[0]
        for dev in range(1, N_DEV):
            gsum = gsum + g_ref[dev]
        loss_o[...] = gsum[6:7, LOSS_COL:LOSS_COL + LANE]
        for i, (rows, col) in enumerate(where):
            width = w_r[i].shape[1]
            if col is None:
                g = jnp.concatenate([gsum[r:r + 1, :] for r in range(rows.start, rows.stop)], axis=1)
            else:
                g = gsum[rows:rows + 1, col:col + width]
            delta, m_new, v_new = _adamw_update(g, w_r[i][...], m_r[i][...], v_r[i][...])
            g_o[i][...] = g
            d_o[i][...] = delta
            m_o[i][...] = m_new
            v_o[i][...] = v_new

    shapes = [SDS(w.shape, F32) for w in ws]
    res = pl.pallas_call(body, name="adamw_small", out_shape=shapes * 4 + [SDS((1, LANE), F32)],
                         compiler_params=_params())(small_all, *ws, *ms, *vs)
    return [res[k * n:(k + 1) * n] for k in range(4)], res[4 * n]


def _row_tile(rows):
    if rows <= 128:
        return rows
    return 128 if rows % 128 == 0 else SHARD_W // 4


def _adamw(gsrc, w, m, v, name, stacked):
    rows, cols = w.shape
    tr = _row_tile(rows)
    n_src = len(gsrc) if stacked else 1

    def body(*refs):
        g_refs, (w_ref, m_ref, v_ref, go_ref, d_ref, mo_ref, vo_ref) = refs[:n_src], refs[n_src:]
        if stacked:
            g = None
            for g_ref, (_, slots) in zip(g_refs, gsrc):
                for j in range(slots):
                    t = g_ref[j].astype(F32)
                    g = t if g is None else g + t
        else:
            g = g_refs[0][...]
        delta, m_new, v_new = _adamw_update(g, w_ref[...], m_ref[...], v_ref[...])
        go_ref[...] = g
        d_ref[...] = delta
        mo_ref[...] = m_new
        vo_ref[...] = v_new

    blk = pl.BlockSpec((tr, cols), lambda i: (i, 0))
    if stacked:
        gspecs = [pl.BlockSpec((slots, tr, arr.shape[2]), lambda i: (0, i, 0)) for arr, slots in gsrc]
        gargs = [arr for arr, _ in gsrc]
    else:
        gspecs, gargs = [blk], [gsrc]
    in_specs = gspecs + [blk, blk, blk]
    args = gargs + [w, m, v]
    return pl.pallas_call(
        body, name=name, grid=(rows // tr,), in_specs=in_specs, out_specs=[blk] * 4,
        out_shape=[SDS((rows, cols), F32)] * 4, compiler_params=_params(),
    )(*args)


def kernel(x, c, w_ada, b_ada, norm_w, w_in, b_gate, q_norm_w, k_norm_w, w_attn_proj, conv_w, conv_b, conv_ln_w, conv_ln_b, w_conv_proj, w_out, loss_target, m_w_ada, m_b_ada, m_norm_w, m_w_in, m_b_gate, m_q_norm_w, m_k_norm_w, m_w_attn_proj, m_conv_w, m_conv_b, m_conv_ln_w, m_conv_ln_b, m_w_conv_proj, m_w_out, v_w_ada, v_b_ada, v_norm_w, v_w_in, v_b_gate, v_q_norm_w, v_k_norm_w, v_w_attn_proj, v_conv_w, v_conv_b, v_conv_ln_w, v_conv_ln_b, v_w_conv_proj, v_w_out):
    xi, yi, ci = lax.axis_index("x"), lax.axis_index("y"), lax.axis_index("c")
    me = 4 * xi + 2 * yi + ci
    x2, tgt2 = x[0], loss_target[0]
    w_in_t, m_w_in_t, v_w_in_t = (jnp.transpose(a[0]) for a in (w_in, m_w_in, v_w_in))
    s = x2.shape[0]

    cw_flat = jnp.pad(conv_w[0].reshape(1, -1), ((0, 0), (0, CONVW_FLAT - CONV_K * HEAD_DIM)))
    pre = jnp.concatenate([c, cw_flat], axis=1).reshape(8, -1)
    (pre_all,) = _all_gather([pre], "gather_c_convw", vmem=True)
    pre_all = pre_all.reshape(N_DEV, -1)
    c_all = pre_all[:, :D_MODEL]
    convw_full = pre_all[:, D_MODEL:D_MODEL + CONV_K * HEAD_DIM].reshape(N_DEV, CONV_K, HEAD_DIM)
    convw_full = jnp.transpose(convw_full, (1, 0, 2)).reshape(CONV_K, CONV_W)
    convw_pad = jnp.pad(convw_full, ((0, CONV_HALO - CONV_K), (0, 0)))

    ada_part = _ada_fwd(c_all, w_ada[0])
    (ada_all,) = _all_gather([ada_part], "gather_ada", vmem=True)
    ada = lax.dynamic_index_in_dim(ada_all, me, axis=1, keepdims=False).reshape(1, 3 * D_MODEL) + b_ada
    shift, scale, gate = ada[:, :D_MODEL], ada[:, D_MODEL:2 * D_MODEL], ada[:, 2 * D_MODEL:]

    wt_g, wa_g, wc_g, wo_g = _all_gather_chips(
        [_cast_bf16(w_in_t, "cast_win"), _cast_bf16(w_attn_proj[0], "cast_wa"), _cast_bf16(w_conv_proj[0], "cast_wc"),
         _cast_bf16(w_out[0], "cast_wo")], "gather_weights")
    wt = wt_g.reshape(IN_W, D_MODEL)
    wa = _cols_from_slots(wa_g, "cols_wa")
    wc = _cols_from_slots(wc_g, "cols_wc")
    wo = wo_g.reshape(D_MODEL, D_MODEL)

    h, ht = _norm_fwd(x2, norm_w, scale, shift)
    proj = _mm_in(h, wt)
    qw2 = jnp.tile(q_norm_w, (1, 2))
    kw2 = jnp.tile(k_norm_w, (1, 2))
    o3, l3 = [], []
    for g in range(N_GROUPS):
        o_g, l_g = _attn_fwd(proj, qw2, kw2, g)
        o3.append(o_g)
        l3.append(l_g)
    head_id = jnp.arange(ATTN_W) // HEAD_DIM
    bd = (head_id[:, None] == head_id[None, :]).astype(BF16)
    (dout, da, lse_delta, dcv, mt, yat, yct, dmo, dya, dyc, dproj,
     dgate, dbg, dlnw, dlnb, dcb, loss_p) = _tail(
        x2, tgt2, proj, o3, l3, wa, wc, wo, gate, b_gate[:, :D_MODEL], b_gate[:, D_MODEL:], convw_pad,
        conv_b, conv_ln_w, conv_ln_b, bd)

    dproj, dconvw8 = _conv_bwd(dcv, proj, convw_pad, dproj)
    dconvw = jnp.sum(dconvw8.reshape(CONV_HALO, 8, CONV_W), axis=1)
    dqw_g3, dkw_g3 = [], []
    for g in range(N_GROUPS):
        dproj, dqw_g, dkw_g = _attn_bwd(proj, da, lse_delta, qw2, kw2, dproj, g)
        dqw_g3.append(dqw_g)
        dkw_g3.append(dkw_g)
    dw_in_p = _mm_dw(ht, dproj).reshape(N_DEV, SHARD_W, D_MODEL)
    dwo_p = _mm_acc(mt, dmo, "mm_dwo", col_slots=False).reshape(N_DEV, D_MODEL // N_DEV, D_MODEL)
    dwa_p = _mm_acc(yat, dya, "mm_dwa", col_slots=True)
    dwc_p = _mm_acc(yct, dyc, "mm_dwc", col_slots=True)

    partials = [dw_in_p, dwa_p, dwc_p, dwo_p]
    me_arr = jnp.reshape(me, (1,)).astype(jnp.int32)
    from_sib = _exchange_sibling(partials, "exchange_sibling")
    presums = [_presum(p, f, me_arr, f"presum{i}") for i, (p, f) in enumerate(zip(partials, from_sib))]
    s_sems, r_sems, pre_thru, land_thru, token = _exchange_chips_start(presums, "exchange_chips_start")
    dh = _mm_dh(dproj, wt, token)
    gx, dsh, dsc, dnw = _norm_bwd(x2, dh, dout, norm_w, scale)
    small_p = _pack_partials(dsh, dsc, dgate, dnw, dbg, dqw_g3, dkw_g3, dcb, dlnw, dlnb, loss_p)
    small_all, dconvw_all = _all_gather([small_p, dconvw], "gather_small", vmem=True)

    small_w = (b_ada, norm_w, b_gate, q_norm_w, k_norm_w, conv_b, conv_ln_w, conv_ln_b)
    small_m = (m_b_ada, m_norm_w, m_b_gate, m_q_norm_w, m_k_norm_w, m_conv_b, m_conv_ln_w, m_conv_ln_b)
    small_v = (v_b_ada, v_norm_w, v_b_gate, v_q_norm_w, v_k_norm_w, v_conv_b, v_conv_ln_w, v_conv_ln_b)
    r_small, loss_row = _adamw_small(small_all, small_w, small_m, small_v)
    dcw_mine = lax.dynamic_slice_in_dim(dconvw_all[:, :CONV_K, :], me * HEAD_DIM, HEAD_DIM, axis=2)
    r_convw = _adamw([(dcw_mine, N_DEV)], conv_w[0], m_conv_w[0], v_conv_w[0], "adamw_conv_w", stacked=True)

    d_ada_all = small_all[:, 0:3, :].reshape(N_DEV, 3 * D_MODEL)
    d_ada_cols = lax.dynamic_slice_in_dim(d_ada_all, me * (3 * D_MODEL // N_DEV), 3 * D_MODEL // N_DEV, axis=1)
    g_wada = _ada_bwd(c_all, d_ada_cols)
    r_ada = _adamw(g_wada, w_ada[0], m_w_ada[0], v_w_ada[0], "adamw_w_ada", stacked=False)
    pres, lands = _exchange_chips_wait(s_sems, r_sems, pre_thru, land_thru, r_ada[1], "exchange_chips_wait")
    terms = [[(p, 1), (l, len(CHIP_K))] for p, l in zip(pres, lands)]
    r_win = [jnp.transpose(r) for r in _adamw(terms[0], w_in_t, m_w_in_t, v_w_in_t, "adamw_w_in", stacked=True)]
    r_wap = _adamw(terms[1], w_attn_proj[0], m_w_attn_proj[0], v_w_attn_proj[0], "adamw_w_attn_proj", stacked=True)
    r_wcp = _adamw(terms[2], w_conv_proj[0], m_w_conv_proj[0], v_w_conv_proj[0], "adamw_w_conv_proj", stacked=True)
    r_wout = _adamw(terms[3], w_out[0], m_w_out[0], v_w_out[0], "adamw_w_out", stacked=True)

    outs = [loss_row[0, 0], gx[None]]
    for k in range(4):
        b_ada_k, norm_w_k, b_gate_k, qn_k, kn_k, conv_b_k, ln_w_k, ln_b_k = r_small[k]
        outs += [r_ada[k][None], b_ada_k, norm_w_k, r_win[k][None], b_gate_k, qn_k, kn_k, r_wap[k][None],
                 r_convw[k][None], conv_b_k, ln_w_k, ln_b_k, r_wcp[k][None], r_wout[k][None]]
    return tuple(outs)
```

```python
import functools

import jax
import jax.numpy as jnp
from jax import lax
from jax.experimental import pallas as pl
from jax.experimental.pallas import tpu as pltpu

F32 = jnp.float32
BF16 = jnp.bfloat16
SDS = jax.ShapeDtypeStruct
MESH = pl.DeviceIdType.MESH

N_DEV = 8
D_MODEL = 1024
HEAD_DIM = 64
N_GROUPS = 3
DILATIONS = (1, 4, 16)
BAND = 128
BWD_UNROLL = 8
ATTN_W = 512
CONV_W = 512
CONV_K = 31
CONV_HALO = 32
IN_W = 8704
SHARD_W = IN_W // N_DEV
PAIR_W = 2 * SHARD_W
Q0, K0, V0, ZA0, U0, ZC0, G0 = 0, 1536, 3072, 4608, 5120, 6144, 6656
EPS = 1e-6
LANE = 128
VMEM_LIMIT = 56 * 1024 * 1024

ADAM_LR, ADAM_B1, ADAM_B2, ADAM_EPS, ADAM_WD, ADAM_STEP = 0.001, 0.9, 0.999, 1e-08, 0.01, 10

CONVW_FLAT = 2048


def _params(**kw):
    return pltpu.CompilerParams(vmem_limit_bytes=VMEM_LIMIT, **kw)


def _sigmoid(z):
    return 0.5 * jnp.tanh(0.5 * z) + 0.5


def _dot(a, b):
    return jnp.dot(a, b, preferred_element_type=F32)


def _dot_nt(a, b):
    return lax.dot_general(a, b, (((1,), (1,)), ((), ())), preferred_element_type=F32)


def _dot_tn(a, b):
    return lax.dot_general(a, b, (((0,), (0,)), ((), ())), preferred_element_type=F32)


def _peer(x, y, c, k):
    px = 1 - x if (k >> 2) & 1 else x
    py = 1 - y if (k >> 1) & 1 else y
    pc = 1 - c if k & 1 else c
    return (px, py, pc), 4 * px + 2 * py + pc


def _all_gather(arrays, name, vmem):
    n = len(arrays)
    space = pltpu.VMEM if vmem else pl.ANY

    def body(*refs):
        ins, outs = refs[:n], refs[n:2 * n]
        send_sems, recv_sems, local_sems = refs[2 * n:]
        x, y, c = lax.axis_index("x"), lax.axis_index("y"), lax.axis_index("c")
        me = 4 * x + 2 * y + c
        locals_ = [pltpu.make_async_copy(ins[a], outs[a].at[me], local_sems.at[a]) for a in range(n)]
        for cp in locals_:
            cp.start()
        sends = []
        for k in range(1, N_DEV):
            peer, _ = _peer(x, y, c, k)
            for a in range(n):
                cp = pltpu.make_async_remote_copy(
                    src_ref=ins[a], dst_ref=outs[a].at[me], send_sem=send_sems.at[a, k - 1],
                    recv_sem=recv_sems.at[a, k - 1], device_id=peer, device_id_type=MESH)
                cp.start()
                sends.append(cp)
        for k in range(1, N_DEV):
            peer, pidx = _peer(x, y, c, k)
            for a in range(n):
                pltpu.make_async_remote_copy(
                    src_ref=ins[a], dst_ref=outs[a].at[pidx], send_sem=send_sems.at[a, k - 1],
                    recv_sem=recv_sems.at[a, k - 1], device_id=peer, device_id_type=MESH).wait_recv()
        for cp in sends:
            cp.wait_send()
        for cp in locals_:
            cp.wait()

    return pl.pallas_call(
        body, name=name,
        out_shape=[SDS((N_DEV,) + a.shape, a.dtype) for a in arrays],
        in_specs=[pl.BlockSpec(memory_space=space)] * n,
        out_specs=[pl.BlockSpec(memory_space=space)] * n,
        scratch_shapes=[pltpu.SemaphoreType.DMA((n, N_DEV - 1)), pltpu.SemaphoreType.DMA((n, N_DEV - 1)),
                        pltpu.SemaphoreType.DMA((n,))],
        compiler_params=_params(),
    )(*arrays)


CHIP_K = (2, 4, 6)


def _all_gather_chips(arrays, name):
    n = len(arrays)
    k_y, k_x, k_d = CHIP_K

    def body(*refs):
        ins, outs = refs[:n], refs[n:2 * n]
        send_sems, recv_sems, local_sems = refs[2 * n:]
        x, y, c = lax.axis_index("x"), lax.axis_index("y"), lax.axis_index("c")
        me = 4 * x + 2 * y + c
        sib, sib_idx = _peer(x, y, c, 1)
        nbr_y, idx_y = _peer(x, y, c, k_y)
        nbr_x, idx_x = _peer(x, y, c, k_x)
        _, idx_d = _peer(x, y, c, k_d)

        def copy(a, slot, block, to, src=None):
            return pltpu.make_async_remote_copy(
                src_ref=outs[a].at[block] if src is None else src, dst_ref=outs[a].at[block],
                send_sem=send_sems.at[a, slot], recv_sem=recv_sems.at[a, slot], device_id=to, device_id_type=MESH)

        locals_ = [pltpu.make_async_copy(ins[a], outs[a].at[me], local_sems.at[a]) for a in range(n)]
        for cp in locals_:
            cp.start()
        for a in range(n):
            copy(a, 0, me, sib, src=ins[a]).start()
            copy(a, 1, me, nbr_y, src=ins[a]).start()
            copy(a, 2, me, nbr_x, src=ins[a]).start()

        def arrived(slot, block, frm, send_on_to=None):
            for a in range(n):
                copy(a, slot, block, frm).wait_recv()
                if send_on_to is not None:
                    copy(a, 3, block, send_on_to).start()
                copy(a, 3 + slot, block, sib).start()

        @pl.when(c == 0)
        def _():
            arrived(1, idx_y, nbr_y, send_on_to=nbr_x)
            arrived(2, idx_x, nbr_x)

        @pl.when(c == 1)
        def _():
            arrived(2, idx_x, nbr_x, send_on_to=nbr_y)
            arrived(1, idx_y, nbr_y)

        arrived(3, idx_d, nbr_x)
        for a in range(n):
            copy(a, 0, sib_idx, sib).wait_recv()
        for slot, k in ((4, k_y), (5, k_x), (6, k_d)):
            _, pidx = _peer(x, y, 1 - c, k)
            for a in range(n):
                copy(a, slot, pidx, sib).wait_recv()
        for slot in range(N_DEV - 1):
            for a in range(n):
                copy(a, slot, me, sib).wait_send()
        for cp in locals_:
            cp.wait()

    return pl.pallas_call(
        body, name=name,
        out_shape=[SDS((N_DEV,) + a.shape, a.dtype) for a in arrays],
        in_specs=[pl.BlockSpec(memory_space=pl.ANY)] * n,
        out_specs=[pl.BlockSpec(memory_space=pl.ANY)] * n,
        scratch_shapes=[pltpu.SemaphoreType.DMA((n, N_DEV - 1)), pltpu.SemaphoreType.DMA((n, N_DEV - 1)),
                        pltpu.SemaphoreType.DMA((n,))],
        compiler_params=_params(),
    )(*arrays)


def _exchange_sibling(arrays, name):
    n = len(arrays)
    ks = (0,) + CHIP_K

    def body(*refs):
        ins, outs = refs[:n], refs[n:2 * n]
        send_sems, recv_sems = refs[2 * n:]
        x, y, c = lax.axis_index("x"), lax.axis_index("y"), lax.axis_index("c")
        sib, sib_idx = _peer(x, y, c, 1)
        sends = []
        for i, k in enumerate(ks):
            _, tgt = _peer(x, y, 1 - c, k) if k else (None, sib_idx)
            for a in range(n):
                cp = pltpu.make_async_remote_copy(
                    src_ref=ins[a].at[tgt], dst_ref=outs[a].at[i], send_sem=send_sems.at[a, i],
                    recv_sem=recv_sems.at[a, i], device_id=sib, device_id_type=MESH)
                cp.start()
                sends.append(cp)
        for cp in sends:
            cp.wait_recv()
        for cp in sends:
            cp.wait_send()

    return pl.pallas_call(
        body, name=name,
        out_shape=[SDS((len(ks),) + a.shape[1:], a.dtype) for a in arrays],
        in_specs=[pl.BlockSpec(memory_space=pl.ANY)] * n,
        out_specs=[pl.BlockSpec(memory_space=pl.ANY)] * n,
        scratch_shapes=[pltpu.SemaphoreType.DMA((n, len(ks))), pltpu.SemaphoreType.DMA((n, len(ks)))],
        compiler_params=_params(),
    )(*arrays)


def _presum(mine, from_sib, me_arr, name):
    _, rows, cols = mine.shape
    tr = _row_tile(rows)
    ns = 1 + len(CHIP_K)

    def body(me_ref, a_ref, b_ref, o_ref):
        del me_ref
        o_ref[...] = (a_ref[...].astype(F32) + b_ref[...].astype(F32)).astype(o_ref.dtype)

    grid_spec = pltpu.PrefetchScalarGridSpec(
        num_scalar_prefetch=1, grid=(ns, rows // tr),
        in_specs=[pl.BlockSpec((1, tr, cols), lambda j, i, me: (jnp.bitwise_xor(me[0], 2 * j), i, 0)),
                  pl.BlockSpec((1, tr, cols), lambda j, i, me: (j, i, 0))],
        out_specs=pl.BlockSpec((1, tr, cols), lambda j, i, me: (j, i, 0)))
    return pl.pallas_call(body, name=name, grid_spec=grid_spec, out_shape=SDS((ns, rows, cols), mine.dtype),
                          compiler_params=_params())(me_arr, mine, from_sib)


HBM_SPEC = pl.BlockSpec(memory_space=pltpu.HBM)
SEM_SPEC = pl.BlockSpec(memory_space=pltpu.SEMAPHORE)
SIDE_EFFECT = pltpu.SideEffectType.DATAFLOW_SIDE_EFFECTING


def _chips_copies(pre_refs, land_refs, send_sems, recv_sems):
    x, y, c = lax.axis_index("x"), lax.axis_index("y"), lax.axis_index("c")
    copies = []
    for j, k in enumerate(CHIP_K):
        peer, _ = _peer(x, y, c, k)
        for a in range(len(pre_refs)):
            copies.append(pltpu.make_async_remote_copy(
                src_ref=pre_refs[a].at[1 + j], dst_ref=land_refs[a].at[j], send_sem=send_sems.at[a * len(CHIP_K) + j],
                recv_sem=recv_sems.at[a * len(CHIP_K) + j], device_id=peer, device_id_type=MESH))
    return copies


def _exchange_chips_start(presums, name):
    n = len(presums)

    def body(*refs):
        pre, land = refs[:n], refs[n:2 * n]
        send_sems, recv_sems = refs[2 * n], refs[2 * n + 1]
        token = refs[-1]
        for cp in _chips_copies(pre, land, send_sems, recv_sems):
            cp.start()
        token[...] = jnp.zeros_like(token)

    nk = len(CHIP_K)
    hbm = [pltpu.HBM(p.shape, p.dtype) for p in presums]
    hbm_land = [pltpu.HBM((nk,) + p.shape[1:], p.dtype) for p in presums]
    res = pl.pallas_call(
        body, name=name,
        out_shape=(pltpu.SemaphoreType.DMA((n * nk,)), pltpu.SemaphoreType.DMA((n * nk,)), *hbm, *hbm_land, SDS((8, LANE), F32)),
        in_specs=[HBM_SPEC] * (2 * n),
        out_specs=(SEM_SPEC, SEM_SPEC, *([HBM_SPEC] * (2 * n)), pl.BlockSpec(memory_space=pltpu.VMEM)),
        input_output_aliases={i: 2 + i for i in range(2 * n)},
        compiler_params=pltpu.CompilerParams(has_side_effects=SIDE_EFFECT),
    )(*[pltpu.with_memory_space_constraint(p, pltpu.HBM) for p in presums],
      *[pltpu.with_memory_space_constraint(lax.empty((nk,) + p.shape[1:], p.dtype), pltpu.HBM) for p in presums])
    return res[0], res[1], res[2:2 + n], res[2 + n:2 + 2 * n], res[-1]


def _exchange_chips_wait(send_sems, recv_sems, pre_thru, land_thru, after, name):
    n = len(pre_thru)

    def body(*refs):
        pre, land = refs[:n], refs[n:2 * n]
        s_sems, r_sems = refs[2 * n], refs[2 * n + 1]
        for cp in _chips_copies(pre, land, s_sems, r_sems):
            cp.wait_send()
            cp.wait_recv()

    hbm = [pltpu.HBM(p.shape, p.dtype) for p in (*pre_thru, *land_thru)]
    res = pl.pallas_call(
        body, name=name, out_shape=tuple(hbm),
        in_specs=[HBM_SPEC] * (2 * n) + [SEM_SPEC, SEM_SPEC, pl.BlockSpec(memory_space=pl.ANY)],
        out_specs=tuple([HBM_SPEC] * (2 * n)),
        input_output_aliases={i: i for i in range(2 * n)},
        compiler_params=pltpu.CompilerParams(has_side_effects=SIDE_EFFECT),
    )(*pre_thru, *land_thru, send_sems, recv_sems, after)
    return res[:n], res[n:]


def _exchange_chips(presums, name):
    n = len(presums)
    nk = len(CHIP_K)

    def body(*refs):
        pre, land = refs[:n], refs[n:2 * n]
        send_sems, recv_sems = refs[2 * n:]
        copies = _chips_copies(pre, land, send_sems, recv_sems)
        for cp in copies:
            cp.start()
        for cp in copies:
            cp.wait_recv()
        for cp in copies:
            cp.wait_send()

    return pl.pallas_call(
        body, name=name,
        out_shape=[SDS((nk,) + p.shape[1:], p.dtype) for p in presums],
        in_specs=[pl.BlockSpec(memory_space=pl.ANY)] * n,
        out_specs=[pl.BlockSpec(memory_space=pl.ANY)] * n,
        scratch_shapes=[pltpu.SemaphoreType.DMA((n * nk,)), pltpu.SemaphoreType.DMA((n * nk,))],
        compiler_params=_params(),
    )(*presums)


def _cast_bf16(w, name):
    def body(w_ref, o_ref):
        o_ref[...] = w_ref[...].astype(BF16)

    return pl.pallas_call(body, name=name, out_shape=SDS(w.shape, BF16), compiler_params=_params())(w)


def _cols_from_slots(wg, name):
    _, rows, cols = wg.shape

    def body(w_ref, o_ref):
        for j in range(N_DEV):
            o_ref[:, j * cols:(j + 1) * cols] = w_ref[j]

    return pl.pallas_call(body, name=name, out_shape=SDS((rows, N_DEV * cols), wg.dtype), compiler_params=_params())(wg)


def _ada_fwd(c_all, w_ada):
    def body(c_ref, w_ref, o_ref):
        cv = c_ref[...]
        sc = (cv * _sigmoid(cv)).astype(BF16)
        o_ref[...] = _dot(sc, w_ref[...].astype(BF16))

    return pl.pallas_call(body, name="ada_fwd", out_shape=SDS((N_DEV, w_ada.shape[1]), F32),
                          compiler_params=_params())(c_all, w_ada)


def _ada_bwd(c_all, d_ada_cols):
    def body(c_ref, d_ref, o_ref):
        cv = c_ref[...]
        sc = (cv * _sigmoid(cv)).astype(BF16)
        o_ref[...] = _dot_tn(sc, d_ref[...].astype(BF16))

    return pl.pallas_call(body, name="ada_bwd", out_shape=SDS((D_MODEL, d_ada_cols.shape[1]), F32),
                          compiler_params=_params())(c_all, d_ada_cols)


def _norm_fwd(x, norm_w, scale, shift):
    s = x.shape[0]
    tr = 512

    def body(x_ref, nw_ref, sc_ref, sh_ref, h_ref, ht_ref):
        xv = x_ref[...]
        r = lax.rsqrt(jnp.mean(xv * xv, axis=-1, keepdims=True) + EPS)
        h = (xv * r * nw_ref[...]) * (1.0 + sc_ref[...]) + sh_ref[...]
        h_ref[...] = h.astype(BF16)
        ht_ref[...] = h.T.astype(BF16)

    vec = pl.BlockSpec((1, D_MODEL), lambda i: (0, 0))
    return pl.pallas_call(
        body, name="norm_fwd", grid=(s // tr,),
        in_specs=[pl.BlockSpec((tr, D_MODEL), lambda i: (i, 0)), vec, vec, vec],
        out_specs=[pl.BlockSpec((tr, D_MODEL), lambda i: (i, 0)), pl.BlockSpec((D_MODEL, tr), lambda i: (0, i))],
        out_shape=[SDS((s, D_MODEL), BF16), SDS((D_MODEL, s), BF16)], compiler_params=_params(),
    )(x, norm_w, scale, shift)


def _mm_in(h, wt):
    s = h.shape[0]
    tm = 512

    def body(h_ref, w_ref, o_ref):
        o_ref[...] = _dot_nt(h_ref[...], w_ref[...])

    return pl.pallas_call(
        body, name="mm_in", grid=(IN_W // PAIR_W, s // tm),
        in_specs=[pl.BlockSpec((tm, D_MODEL), lambda p, m: (m, 0)),
                  pl.BlockSpec((PAIR_W, D_MODEL), lambda p, m: (p, 0))],
        out_specs=pl.BlockSpec((tm, PAIR_W), lambda p, m: (m, p)),
        out_shape=SDS((s, IN_W), F32), compiler_params=_params(),
    )(h, wt)


def _head_ones():
    a = lax.broadcasted_iota(jnp.int32, (LANE, LANE), 0) // HEAD_DIM
    b = lax.broadcasted_iota(jnp.int32, (LANE, LANE), 1) // HEAD_DIM
    return (a == b).astype(BF16)


def _head_sums(t, ones):
    return _dot(t.astype(BF16), ones)


def _band_bias(bias, transposed=False):
    qi = lax.broadcasted_iota(jnp.int32, (2 * BAND, 2 * BAND), 1 if transposed else 0) % BAND
    kj = lax.broadcasted_iota(jnp.int32, (2 * BAND, 2 * BAND), 0 if transposed else 1)
    dist = qi + BAND - kj
    valid = (dist >= 0) & (dist <= BAND)
    bias[1] = jnp.where(valid, 0.0, -1e30)
    bias[0] = jnp.where(valid & (kj >= BAND), 0.0, -1e30)


def _token_rows(j, d, chunk, per_r):
    return pl.ds(j // per_r + (j % per_r) * (chunk * d), chunk, stride=d)


def _deinterleave(src_ref, dst_ref, w_ref, ones, d, sub_len, chunk, scale, dst_off):
    per_r = sub_len // chunk

    def step(j, _):
        t = src_ref[_token_rows(j, d, chunk, per_r), :]
        if w_ref is not None:
            ms = _head_sums(t * t, ones) * (1.0 / HEAD_DIM)
            t = t * lax.rsqrt(ms + EPS) * (w_ref[...] * scale)
        dst_ref[pl.ds(pl.multiple_of(dst_off + j * chunk, BAND), chunk), :] = t.astype(dst_ref.dtype)
        return 0
    lax.fori_loop(0, d * per_r, step, 0, unroll=4)


def _attn_fwd(proj, qw2, kw2, g):
    s = proj.shape[0]
    d = DILATIONS[g]
    sub_len = s // d
    nb = sub_len // BAND
    chunk = min(sub_len, 256)

    def body(q_ref, k_ref, v_ref, qw_ref, kw_ref, o_ref, l_ref, qn_ref, kn_ref, vn_ref, qd, kd, vd, od, ld, bias):
        lo = lax.broadcasted_iota(jnp.int32, (1, LANE), 1) < HEAD_DIM
        ones = _head_ones()

        @pl.when(pl.program_id(0) == 0)
        def _():
            _band_bias(bias)

        kd[0:BAND, :] = jnp.zeros((BAND, LANE), BF16)
        vd[0:BAND, :] = jnp.zeros((BAND, LANE), BF16)
        _deinterleave(q_ref, qd, qw_ref, ones, d, sub_len, chunk, HEAD_DIM ** -0.5, 0)
        _deinterleave(k_ref, kd, kw_ref, ones, d, sub_len, chunk, 1.0, BAND)
        _deinterleave(v_ref, vd, None, ones, d, sub_len, chunk, 1.0, BAND)
        qn_ref[...] = qd[...]
        kn_ref[...] = kd[BAND:BAND + s, :]
        vn_ref[...] = vd[BAND:BAND + s, :]

        def block(t, _):
            base = pl.multiple_of(t * BAND, BAND)
            q = qd[pl.ds(base, BAND), :]
            k2 = kd[pl.ds(base, 2 * BAND), :]
            v2 = vd[pl.ds(base, 2 * BAND), :]
            zero = jnp.zeros_like(q)
            qs = jnp.concatenate([jnp.where(lo, q, zero), jnp.where(lo, zero, q)], axis=0)
            sc = _dot_nt(qs, k2) + bias[jnp.minimum(t % nb, 1)]
            m = jnp.max(sc, axis=-1, keepdims=True)
            p = jnp.exp(sc - m)
            den = jnp.sum(p, axis=-1, keepdims=True)
            u = _dot(p.astype(BF16), v2) * (1.0 / den)
            lse = m + jnp.log(den)
            od[pl.ds(base, BAND), :] = jnp.where(lo, u[:BAND], u[BAND:])
            ld[pl.ds(base, BAND), :] = jnp.where(lo, lse[:BAND], lse[BAND:])
            return 0
        lax.fori_loop(0, s // BAND, block, 0, unroll=16)

        per_r = sub_len // chunk

        def back(j, _):
            src = pl.ds(pl.multiple_of(j * chunk, chunk), chunk)
            dst = _token_rows(j, d, chunk, per_r)
            o_ref[dst, :] = od[src, :]
            l_ref[dst, :] = ld[src, :]
            return 0
        lax.fori_loop(0, d * per_r, back, 0, unroll=2)

    col = lambda off: pl.BlockSpec((s, LANE), lambda hp, off=off: (0, off // LANE + 4 * g + hp))
    vec = pl.BlockSpec((1, LANE), lambda hp: (0, 0))
    out = pl.BlockSpec((s, LANE), lambda hp: (0, hp))
    return pl.pallas_call(
        body, name=f"attn_fwd{g}", grid=(ATTN_W // LANE,),
        in_specs=[col(Q0), col(K0), col(V0), vec, vec], out_specs=[out] * 5,
        out_shape=[SDS((s, ATTN_W), F32)] * 2 + [SDS((s, ATTN_W), BF16)] * 3,
        scratch_shapes=[pltpu.VMEM((s, LANE), BF16), pltpu.VMEM((s + BAND, LANE), BF16), pltpu.VMEM((s + BAND, LANE), BF16),
                        pltpu.VMEM((s, LANE), F32), pltpu.VMEM((s, LANE), F32),
                        pltpu.VMEM((2, 2 * BAND, 2 * BAND), F32)],
        compiler_params=_params(),
    )(proj, proj, proj, qw2, kw2)


def _attn_bwd(proj, qn, kn, vn, da, lse_delta, qw2, kw2, dproj, g):
    s = proj.shape[0]
    d = DILATIONS[g]
    sub_len = s // d
    nb = sub_len // BAND
    chunk = min(sub_len, 256)

    def body(q_ref, k_ref, qn_ref, kn_ref, vn_ref, da_ref, ld_ref, qw_ref, kw_ref, dp_in, dp_out, dqw_ref, dkw_ref,
             kd, vd, kdt, dad, lst, dlt, dqt, dqd, dkd, dvd, st, stb, bias_t, wacc, sem):
        del dp_in
        hp = pl.program_id(0)
        lo = lax.broadcasted_iota(jnp.int32, (1, LANE), 1) < HEAD_DIM
        row_lo = lax.broadcasted_iota(jnp.int32, (LANE, 1), 0) < HEAD_DIM
        ones = _head_ones()
        per_r = sub_len // chunk
        cblk = chunk // BAND

        @pl.when(hp == 0)
        def _():
            _band_bias(bias_t, transposed=True)

        kd[0:BAND, :] = jnp.zeros((BAND, LANE), BF16)
        vd[0:BAND, :] = jnp.zeros((BAND, LANE), BF16)
        kdt[0] = jnp.zeros((LANE, BAND), BF16)
        kd[BAND:BAND + s, :] = kn_ref[...]
        vd[BAND:BAND + s, :] = vn_ref[...]

        def k_step(t, _):
            kdt[1 + t] = kn_ref[pl.ds(pl.multiple_of(t * BAND, BAND), BAND), :].astype(F32).T.astype(BF16)
            return 0
        lax.fori_loop(0, s // BAND, k_step, 0, unroll=4)
        _deinterleave(da_ref, dad, None, ones, d, sub_len, chunk, 1.0, 0)

        def rows_step(j, _):
            tok = _token_rows(j, d, chunk, per_r)
            tt = ld_ref[tok, :].T
            for u in range(cblk):
                cols = slice(u * BAND, (u + 1) * BAND)
                lst[j * cblk + u, 0:1, :] = tt[0:1, cols]
                lst[j * cblk + u, 1:2, :] = tt[HEAD_DIM:HEAD_DIM + 1, cols]
                dlt[j * cblk + u, 0:1, :] = tt[HEAD_DIM // 2:HEAD_DIM // 2 + 1, cols]
                dlt[j * cblk + u, 1:2, :] = tt[HEAD_DIM + HEAD_DIM // 2:HEAD_DIM + HEAD_DIM // 2 + 1, cols]
            return 0
        lax.fori_loop(0, d * per_r, rows_step, 0, unroll=4)

        def block(t, carry):
            ck, cv = carry
            base = pl.multiple_of(t * BAND, BAND)
            q = qn_ref[pl.ds(base, BAND), :]
            k2 = kd[pl.ds(base, 2 * BAND), :]
            v2 = vd[pl.ds(base, 2 * BAND), :]
            k2t = jnp.concatenate([kdt[t], kdt[t + 1]], axis=1)
            dav = dad[pl.ds(base, BAND), :]
            zero = jnp.zeros_like(q)
            qs = jnp.concatenate([jnp.where(lo, q, zero), jnp.where(lo, zero, q)], axis=0)
            das = jnp.concatenate([jnp.where(lo, dav, zero), jnp.where(lo, zero, dav)], axis=0)
            ls_row = jnp.concatenate([lst[t, 0:1, :], lst[t, 1:2, :]], axis=1)
            dl_row = jnp.concatenate([dlt[t, 0:1, :], dlt[t, 1:2, :]], axis=1)
            sc_t = _dot_nt(k2, qs) + bias_t[jnp.minimum(t % nb, 1)]
            p_t = jnp.exp(sc_t - ls_row)
            dp_t = _dot_nt(v2, das)
            ds_t = (p_t * (dp_t - dl_row)).astype(BF16)
            dv2 = _dot(p_t.astype(BF16), das)
            dk2 = _dot(ds_t, qs)
            dvd[pl.ds(base, BAND), :] = cv + dv2[:BAND]
            dkd[pl.ds(base, BAND), :] = ck + dk2[:BAND]
            dq_t = _dot(k2t, ds_t)
            dqt[t] = jnp.where(row_lo, dq_t[:, :BAND], dq_t[:, BAND:])
            return dk2[BAND:], dv2[BAND:]

        def blocks(i, carry):
            for u in range(BWD_UNROLL):
                carry = block(i * BWD_UNROLL + u, carry)
            return carry
        zeros = jnp.zeros((BAND, LANE), F32)
        ck, cv = lax.fori_loop(0, s // (BAND * BWD_UNROLL), blocks, (zeros, zeros))
        dkd[s:s + BAND, :] = ck
        dvd[s:s + BAND, :] = cv

        def dq_rows(t, _):
            dqd[pl.ds(pl.multiple_of(t * BAND, BAND), BAND), :] = dqt[t].T
            return 0
        lax.fori_loop(0, s // BAND, dq_rows, 0, unroll=4)

        def col_copy(slot, col0):
            return pltpu.make_async_copy(
                stb.at[slot], dp_out.at[:, pl.ds(pl.multiple_of(col0 + LANE * (4 * g + hp), LANE), LANE)], sem.at[slot])

        def store_cols(slot, col0):
            @pl.when(hp > 0)
            def _():
                col_copy(slot, col0).wait()
            stb[slot] = st[...].astype(BF16)
            col_copy(slot, col0).start()

        def norm_back(src_ref, dy_ref, dy_off, w_ref, scale, dw_ref, slot, col0):
            wacc[...] = jnp.zeros_like(wacc)

            def step(j, _):
                tok = _token_rows(j, d, chunk, per_r)
                t = src_ref[tok, :]
                dy = dy_ref[pl.ds(pl.multiple_of(dy_off + j * chunk, BAND), chunk), :]
                rr = lax.rsqrt(_head_sums(t * t, ones) * (1.0 / HEAD_DIM) + EPS)
                nrm = t * rr
                wacc[...] += jnp.sum((dy * nrm).reshape(chunk // 8, 8, LANE), axis=0)
                dn = dy * (w_ref[...] * scale)
                st[tok, :] = rr * (dn - nrm * (_head_sums(dn * nrm, ones) * (1.0 / HEAD_DIM)))
                return 0
            lax.fori_loop(0, d * per_r, step, 0, unroll=4)
            dw_ref[...] += jnp.broadcast_to(jnp.sum(wacc[...], axis=0, keepdims=True) * scale, dw_ref.shape)
            store_cols(slot, col0)

        @pl.when(hp == 0)
        def _():
            dqw_ref[...] = jnp.zeros_like(dqw_ref)
            dkw_ref[...] = jnp.zeros_like(dkw_ref)

        norm_back(q_ref, dqd, 0, qw_ref, HEAD_DIM ** -0.5, dqw_ref, 0, Q0)
        norm_back(k_ref, dkd, BAND, kw_ref, 1.0, dkw_ref, 1, K0)

        def v_back(j, _):
            src = pl.ds(pl.multiple_of(BAND + j * chunk, BAND), chunk)
            st[_token_rows(j, d, chunk, per_r), :] = dvd[src, :]
            return 0
        lax.fori_loop(0, d * per_r, v_back, 0, unroll=2)
        store_cols(2, V0)

        @pl.when(hp == ATTN_W // LANE - 1)
        def _():
            for slot, col0 in enumerate((Q0, K0, V0)):
                col_copy(slot, col0).wait()

    col = lambda off: pl.BlockSpec((s, LANE), lambda hp, off=off: (0, off // LANE + 4 * g + hp))
    mid = pl.BlockSpec((s, LANE), lambda hp: (0, hp))
    vec = pl.BlockSpec((1, LANE), lambda hp: (0, 0))
    acc = pl.BlockSpec((8, LANE), lambda hp: (0, 0))
    any_ = pl.BlockSpec(memory_space=pl.ANY)
    return pl.pallas_call(
        body, name=f"attn_bwd{g}", grid=(ATTN_W // LANE,),
        in_specs=[col(Q0), col(K0), mid, mid, mid, mid, mid, vec, vec, any_],
        out_specs=[any_, acc, acc],
        out_shape=[SDS(dproj.shape, dproj.dtype), SDS((8, LANE), F32), SDS((8, LANE), F32)],
        input_output_aliases={9: 0},
        scratch_shapes=[pltpu.VMEM((s + BAND, LANE), BF16), pltpu.VMEM((s + BAND, LANE), BF16),
                        pltpu.VMEM((s // BAND + 1, LANE, BAND), BF16), pltpu.VMEM((s, LANE), BF16),
                        pltpu.VMEM((s // BAND, 8, BAND), F32), pltpu.VMEM((s // BAND, 8, BAND), F32),
                        pltpu.VMEM((s // BAND, LANE, BAND), F32),
                        pltpu.VMEM((s, LANE), F32), pltpu.VMEM((s + BAND, LANE), F32), pltpu.VMEM((s + BAND, LANE), F32),
                        pltpu.VMEM((s, LANE), F32), pltpu.VMEM((3, s, LANE), BF16),
                        pltpu.VMEM((2, 2 * BAND, 2 * BAND), F32), pltpu.VMEM((8, LANE), F32),
                        pltpu.SemaphoreType.DMA((3,))],
        compiler_params=_params(),
    )(proj, proj, qn, kn, vn, da, lse_delta, qw2, kw2, dproj)


def _tap_views(ext_ref, sh_ref, offsets, tr, cols):
    for b in range(8):
        group = [j for j, o in enumerate(offsets) if o % 8 == b]
        if not group:
            continue
        first = min(offsets[j] for j in group)
        span = tr + max(offsets[j] for j in group) - first
        sh_ref[0:span, cols] = ext_ref[first:first + span, cols]
        for j in group:
            yield j, sh_ref[offsets[j] - first:offsets[j] - first + tr, cols]


def _silu_grad(z, sg):
    return sg * (1.0 + z * (1.0 - sg))


def _glu(u):
    a_h, b_h = u[:, :CONV_W], u[:, CONV_W:]
    sg = _sigmoid(b_h)
    return a_h, sg, a_h * sg


def _tail(x, tgt, proj, o3, l3, wa, wc, wo, gate, bga, bgc, convw, convb, lnw, lnb, bd):
    s = x.shape[0]
    tr = 256

    def body(x_ref, t_ref, za_ref, u_ref, uh_ref, zc_ref, g0_ref, g1_ref, g2_ref, g3_ref,
             o0_ref, o1_ref, o2_ref, l0_ref, l1_ref, l2_ref, wa_ref, wc_ref, wo_ref,
             gate_ref, bga_ref, bgc_ref, cw_ref, cb_ref, lnw_ref, lnb_ref, bd_ref,
             dout_ref, da_ref, ld_ref, dcv_ref, mt_ref, yat_ref, yct_ref, dmo_ref, dya_ref, dyc_ref, dp_ref,
             dgate_ref, dbg_ref, dlnw_ref, dlnb_ref, dcb_ref, loss_ref,
             ext, sh, st_za, st_zc, st_g, sems):
        i = pl.program_id(0)

        @pl.when(i == 0)
        def _():
            for r in (dgate_ref, dbg_ref, dlnw_ref, dlnb_ref, dcb_ref, loss_ref):
                r[...] = jnp.zeros_like(r)

        def acc_rows(ref, v):
            ref[...] += jnp.broadcast_to(jnp.sum(v, axis=0, keepdims=True), ref.shape)

        la, lb, lc = l0_ref[...], l1_ref[...], l2_ref[...]
        mx = jnp.maximum(jnp.maximum(la, lb), lc)
        ea, eb, ec = jnp.exp(la - mx), jnp.exp(lb - mx), jnp.exp(lc - mx)
        den = ea + eb + ec
        inv = 1.0 / den
        attn = (ea * inv) * o0_ref[...] + (eb * inv) * o1_ref[...] + (ec * inv) * o2_ref[...]
        lse = mx + jnp.log(den)

        za = za_ref[...]
        sga = _sigmoid(za)
        sa = za * sga
        ya_in = attn * sa
        y_attn = _dot(ya_in.astype(BF16), wa_ref[...])

        _, _, glu = _glu(u_ref[...])
        _, _, glu_h = _glu(uh_ref[...])
        ext[0:CONV_HALO, :] = jnp.where(i > 0, glu_h, 0.0)
        ext[CONV_HALO:CONV_HALO + tr, :] = glu
        cv_blocks = []
        for cb in range(CONV_W // LANE):
            cols = slice(cb * LANE, (cb + 1) * LANE)
            cv_c = jnp.broadcast_to(cb_ref[:, cols], (tr, LANE))
            for j, rows in _tap_views(ext, sh, [CONV_HALO - (CONV_K - 1) + j for j in range(CONV_K)], tr, cols):
                cv_c = cv_c + cw_ref[j:j + 1, cols] * rows
            cv_blocks.append(cv_c)
        cv = jnp.concatenate(cv_blocks, axis=1)
        mu = jnp.mean(cv, axis=-1, keepdims=True)
        xc = cv - mu
        rstd = lax.rsqrt(jnp.mean(xc * xc, axis=-1, keepdims=True) + EPS)
        nrm = xc * rstd
        ln = nrm * lnw_ref[...] + lnb_ref[...]
        sgl = _sigmoid(ln)
        cs = ln * sgl
        zc = zc_ref[...]
        sgc = _sigmoid(zc)
        scz = zc * sgc
        yc_in = cs * scz
        y_conv = _dot(yc_in.astype(BF16), wc_ref[...])

        ga = _sigmoid(jnp.concatenate([g0_ref[...], g1_ref[...]], axis=1) + bga_ref[...])
        gc = _sigmoid(jnp.concatenate([g2_ref[...], g3_ref[...]], axis=1) + bgc_ref[...])
        merged = ga * y_attn + gc * y_conv
        mo = _dot(merged.astype(BF16), wo_ref[...])
        gate_v = gate_ref[...]
        err = (x_ref[...] + gate_v * mo) - t_ref[...]
        loss_ref[...] += 0.5 * jnp.sum(jnp.mean(err * err, axis=-1, keepdims=True))
        d_out = err * (1.0 / D_MODEL)
        dout_ref[...] = d_out

        rows = pl.ds(pl.multiple_of(i * tr, tr), tr)
        cps = [pltpu.make_async_copy(st_za, dp_ref.at[rows, pl.ds(ZA0, ATTN_W)], sems.at[0]),
               pltpu.make_async_copy(st_zc, dp_ref.at[rows, pl.ds(ZC0, CONV_W)], sems.at[1]),
               pltpu.make_async_copy(st_g, dp_ref.at[rows, pl.ds(G0, 2 * D_MODEL)], sems.at[2])]

        @pl.when(i > 0)
        def _():
            for cp in cps:
                cp.wait()

        acc_rows(dgate_ref, d_out * mo)
        dmo_b = (d_out * gate_v).astype(BF16)
        dmo_ref[...] = dmo_b
        mt_ref[...] = merged.T.astype(BF16)
        d_merged = _dot_nt(dmo_b, wo_ref[...])
        d_ya = (d_merged * ga).astype(BF16)
        d_yc = (d_merged * gc).astype(BF16)
        dya_ref[...] = d_ya
        dyc_ref[...] = d_yc
        dga = d_merged * y_attn * (ga * (1.0 - ga))
        dgc = d_merged * y_conv * (gc * (1.0 - gc))
        dgs = jnp.concatenate([dga, dgc], axis=1)
        acc_rows(dbg_ref, dgs)
        st_g[...] = dgs.astype(BF16)

        yat_ref[...] = ya_in.T.astype(BF16)
        d_ya_in = _dot_nt(d_ya, wa_ref[...])
        d_attn = d_ya_in * sa
        da_ref[...] = d_attn
        st_za[...] = (d_ya_in * attn * _silu_grad(za, sga)).astype(BF16)
        prod = d_attn * attn
        hi = prod.astype(BF16)
        lo_ = (prod - hi.astype(F32)).astype(BF16)
        delta = _dot(hi, bd_ref[...]) + _dot(lo_, bd_ref[...])
        first_half = (lax.broadcasted_iota(jnp.int32, (1, ATTN_W), 1) % HEAD_DIM) < HEAD_DIM // 2
        ld_ref[...] = jnp.where(first_half, lse, delta)

        yct_ref[...] = yc_in.T.astype(BF16)
        d_yc_in = _dot_nt(d_yc, wc_ref[...])
        st_zc[...] = (d_yc_in * cs * _silu_grad(zc, sgc)).astype(BF16)
        d_ln = (d_yc_in * scz) * _silu_grad(ln, sgl)
        acc_rows(dlnw_ref, d_ln * nrm)
        acc_rows(dlnb_ref, d_ln)
        d_nrm = d_ln * lnw_ref[...]
        d_cv = rstd * (d_nrm - jnp.mean(d_nrm, axis=-1, keepdims=True)
                       - nrm * jnp.mean(d_nrm * nrm, axis=-1, keepdims=True))
        acc_rows(dcb_ref, d_cv)
        dcv_ref[...] = d_cv

        for cp in cps:
            cp.start()

        @pl.when(i == s // tr - 1)
        def _():
            for cp in cps:
                cp.wait()

    def rows(width, colblk=0):
        return pl.BlockSpec((tr, width), lambda i, colblk=colblk: (i, colblk))

    def const(shape):
        return pl.BlockSpec(shape, lambda i: (0,) * len(shape))

    halo = pl.BlockSpec((CONV_HALO, D_MODEL), lambda i: (jnp.maximum(i * (tr // CONV_HALO) - 1, 0), U0 // D_MODEL))
    in_specs = [rows(D_MODEL), rows(D_MODEL), rows(ATTN_W, ZA0 // ATTN_W), rows(D_MODEL, U0 // D_MODEL), halo,
                rows(CONV_W, ZC0 // CONV_W)]
    in_specs += [rows(512, G0 // 512 + j) for j in range(4)]
    in_specs += [rows(ATTN_W)] * 6
    in_specs += [const(wa.shape), const(wc.shape), const(wo.shape), const((1, D_MODEL)), const((1, D_MODEL)),
                 const((1, D_MODEL)), const(convw.shape), const((1, CONV_W)), const((1, CONV_W)), const((1, CONV_W)),
                 const(bd.shape)]
    tcol = lambda width: pl.BlockSpec((width, tr), lambda i: (0, i))
    out_specs = [rows(D_MODEL), rows(ATTN_W), rows(ATTN_W), rows(CONV_W),
                 tcol(D_MODEL), tcol(ATTN_W), tcol(CONV_W), rows(D_MODEL), rows(D_MODEL), rows(D_MODEL),
                 pl.BlockSpec(memory_space=pl.ANY),
                 const((8, D_MODEL)), const((8, 2 * D_MODEL)), const((8, CONV_W)), const((8, CONV_W)), const((8, CONV_W)),
                 const((8, LANE))]
    out_shape = [SDS((s, D_MODEL), F32), SDS((s, ATTN_W), F32), SDS((s, ATTN_W), F32),
                 SDS((s, CONV_W), F32),
                 SDS((D_MODEL, s), BF16), SDS((ATTN_W, s), BF16), SDS((CONV_W, s), BF16),
                 SDS((s, D_MODEL), BF16), SDS((s, D_MODEL), BF16), SDS((s, D_MODEL), BF16),
                 SDS((s, IN_W), BF16),
                 SDS((8, D_MODEL), F32), SDS((8, 2 * D_MODEL), F32), SDS((8, CONV_W), F32), SDS((8, CONV_W), F32),
                 SDS((8, CONV_W), F32), SDS((8, LANE), F32)]
    return pl.pallas_call(
        body, name="tail", grid=(s // tr,), in_specs=in_specs, out_specs=out_specs, out_shape=out_shape,
        scratch_shapes=[pltpu.VMEM((CONV_HALO + tr, CONV_W), F32), pltpu.VMEM((CONV_HALO + tr, CONV_W), F32),
                        pltpu.VMEM((tr, ATTN_W), BF16),
                        pltpu.VMEM((tr, CONV_W), BF16), pltpu.VMEM((tr, 2 * D_MODEL), BF16),
                        pltpu.SemaphoreType.DMA((3,))],
        compiler_params=_params(),
    )(x, tgt, proj, proj, proj, proj, proj, proj, proj, proj, *o3, *l3, wa, wc, wo, gate, bga, bgc,
      convw, convb, lnw, lnb, bd)


def _conv_bwd(dcv, proj, convw, dproj):
    s = dcv.shape[0]
    tr = 128
    nt = s // tr

    def body(dcv_ref, dcvn_ref, u_ref, uh_ref, cw_ref, dp_in, dp_out, dw_ref, extg, extd, sh):
        del dp_in
        i = pl.program_id(0)

        @pl.when(i == 0)
        def _():
            dw_ref[...] = jnp.zeros_like(dw_ref)

        _, _, glu = _glu(u_ref[...])
        _, _, glu_h = _glu(uh_ref[...])
        extg[0:CONV_HALO, :] = jnp.where(i > 0, glu_h, 0.0)
        extg[CONV_HALO:CONV_HALO + tr, :] = glu
        extd[0:tr, :] = dcv_ref[...]
        extd[tr:tr + CONV_HALO, :] = jnp.where(i < nt - 1, dcvn_ref[...], 0.0)
        for cb in range(CONV_W // LANE):
            cols = slice(cb * LANE, (cb + 1) * LANE)
            dglu = jnp.zeros((tr, LANE), F32)
            for j, rows in _tap_views(extd, sh, [CONV_K - 1 - j for j in range(CONV_K)], tr, cols):
                dglu = dglu + cw_ref[j:j + 1, cols] * rows
            dcv_c = dcv_ref[:, cols]
            for j, rows in _tap_views(extg, sh, [CONV_HALO - (CONV_K - 1) + j for j in range(CONV_K)], tr, cols):
                dw_ref[8 * j:8 * j + 8, cols] += jnp.sum((dcv_c * rows).reshape(tr // 8, 8, LANE), axis=0)
            a_h = u_ref[:, cols]
            sgb = _sigmoid(u_ref[:, CONV_W + cb * LANE:CONV_W + (cb + 1) * LANE])
            dp_out[:, cols] = (dglu * sgb).astype(BF16)
            dp_out[:, CONV_W + cb * LANE:CONV_W + (cb + 1) * LANE] = (dglu * a_h * (sgb * (1.0 - sgb))).astype(BF16)

    ucol = U0 // D_MODEL
    return pl.pallas_call(
        body, name="conv_bwd", grid=(nt,),
        in_specs=[pl.BlockSpec((tr, CONV_W), lambda i: (i, 0)),
                  pl.BlockSpec((CONV_HALO, CONV_W), lambda i: (jnp.minimum((i + 1) * (tr // CONV_HALO), s // CONV_HALO - 1), 0)),
                  pl.BlockSpec((tr, D_MODEL), lambda i: (i, ucol)),
                  pl.BlockSpec((CONV_HALO, D_MODEL), lambda i: (jnp.maximum(i * (tr // CONV_HALO) - 1, 0), ucol)),
                  pl.BlockSpec(convw.shape, lambda i: (0, 0)),
                  pl.BlockSpec(memory_space=pl.ANY)],
        out_specs=[pl.BlockSpec((tr, D_MODEL), lambda i: (i, ucol)), pl.BlockSpec((8 * CONV_HALO, CONV_W), lambda i: (0, 0))],
        out_shape=[SDS(dproj.shape, dproj.dtype), SDS((8 * CONV_HALO, CONV_W), F32)],
        input_output_aliases={5: 0},
        scratch_shapes=[pltpu.VMEM((CONV_HALO + tr, CONV_W), F32)] * 3,
        compiler_params=_params(),
    )(dcv, dcv, proj, proj, convw, dproj)


def _mm_acc(at, b, name, col_slots):
    m, s = at.shape
    n = b.shape[1]
    tk = 512
    nk = s // tk

    def body(a_ref, b_ref, o_ref, acc):
        k = pl.program_id(0)

        @pl.when(k == 0)
        def _():
            acc[...] = jnp.zeros_like(acc)

        acc[...] += _dot(a_ref[...], b_ref[...])

        @pl.when(k == nk - 1)
        def _():
            if col_slots:
                w = n // N_DEV
                for j in range(N_DEV):
                    o_ref[j] = acc[:, j * w:(j + 1) * w].astype(BF16)
            else:
                o_ref[...] = acc[...].astype(BF16)

    if col_slots:
        out_shape = SDS((N_DEV, m, n // N_DEV), BF16)
        out_spec = pl.BlockSpec((N_DEV, m, n // N_DEV), lambda k: (0, 0, 0))
    else:
        out_shape = SDS((m, n), BF16)
        out_spec = pl.BlockSpec((m, n), lambda k: (0, 0))
    return pl.pallas_call(
        body, name=name, grid=(nk,),
        in_specs=[pl.BlockSpec((m, tk), lambda k: (0, k)), pl.BlockSpec((tk, n), lambda k: (k, 0))],
        out_specs=out_spec, out_shape=out_shape, scratch_shapes=[pltpu.VMEM((m, n), F32)],
        compiler_params=_params(),
    )(at, b)


def _mm_dw(ht, dproj):
    s = ht.shape[1]
    tk = 512
    nk = s // tk

    def body(a_ref, b_ref, o_ref, acc):
        k = pl.program_id(1)

        @pl.when(k == 0)
        def _():
            acc[...] = jnp.zeros_like(acc)

        acc[...] += _dot(a_ref[...], b_ref[...])

        @pl.when(k == nk - 1)
        def _():
            o_ref[...] = acc[...].T.astype(BF16)

    return pl.pallas_call(
        body, name="mm_dw", grid=(IN_W // PAIR_W, nk),
        in_specs=[pl.BlockSpec((D_MODEL, tk), lambda p, k: (0, k)), pl.BlockSpec((tk, PAIR_W), lambda p, k: (k, p))],
        out_specs=pl.BlockSpec((PAIR_W, D_MODEL), lambda p, k: (p, 0)),
        out_shape=SDS((IN_W, D_MODEL), BF16), scratch_shapes=[pltpu.VMEM((D_MODEL, PAIR_W), F32)],
        compiler_params=_params(),
    )(ht, dproj)


def _mm_dh(dproj, wt, token):
    s = dproj.shape[0]
    tm = 1024

    def body(dp_ref, w_ref, tok_ref, o_ref):
        del tok_ref
        p = pl.program_id(1)
        part = _dot(dp_ref[...], w_ref[...])

        @pl.when(p == 0)
        def _():
            o_ref[...] = part

        @pl.when(p > 0)
        def _():
            o_ref[...] += part

    return pl.pallas_call(
        body, name="mm_dh", grid=(s // tm, IN_W // PAIR_W),
        in_specs=[pl.BlockSpec((tm, PAIR_W), lambda m, p: (m, p)),
                  pl.BlockSpec((PAIR_W, D_MODEL), lambda m, p: (p, 0)),
                  pl.BlockSpec(token.shape, lambda m, p: (0, 0))],
        out_specs=pl.BlockSpec((tm, D_MODEL), lambda m, p: (m, 0)),
        out_shape=SDS((s, D_MODEL), F32), compiler_params=_params(),
    )(dproj, wt, token)


def _norm_bwd(x, dh, dout, norm_w, scale):
    s = x.shape[0]
    tr = 512

    def body(x_ref, dh_ref, do_ref, nw_ref, sc_ref, gx_ref, dsh_ref, dsc_ref, dnw_ref):
        i = pl.program_id(0)

        @pl.when(i == 0)
        def _():
            for r in (dsh_ref, dsc_ref, dnw_ref):
                r[...] = jnp.zeros_like(r)

        def acc_rows(ref, v):
            ref[...] += jnp.broadcast_to(jnp.sum(v, axis=0, keepdims=True), ref.shape)

        xv = x_ref[...]
        dh_v = dh_ref[...]
        r = lax.rsqrt(jnp.mean(xv * xv, axis=-1, keepdims=True) + EPS)
        xn = xv * r
        one_sc = 1.0 + sc_ref[...]
        acc_rows(dsh_ref, dh_v)
        acc_rows(dsc_ref, dh_v * (xn * nw_ref[...]))
        acc_rows(dnw_ref, dh_v * xn * one_sc)
        dxn = dh_v * (nw_ref[...] * one_sc)
        gx_ref[...] = do_ref[...] + r * (dxn - xn * jnp.mean(dxn * xn, axis=-1, keepdims=True))

    blk = pl.BlockSpec((tr, D_MODEL), lambda i: (i, 0))
    vec = pl.BlockSpec((1, D_MODEL), lambda i: (0, 0))
    acc = pl.BlockSpec((8, D_MODEL), lambda i: (0, 0))
    return pl.pallas_call(
        body, name="norm_bwd", grid=(s // tr,), in_specs=[blk, blk, blk, vec, vec],
        out_specs=[blk, acc, acc, acc],
        out_shape=[SDS((s, D_MODEL), F32)] + [SDS((8, D_MODEL), F32)] * 3, compiler_params=_params(),
    )(x, dh, dout, norm_w, scale)


SMALL_ROWS = 8
QN_COL, KN_COL, CB_COL, LOSS_COL = 0, LANE, 2 * LANE, 2 * LANE + CONV_W


def _pack_partials(dsh, dsc, dgate, dnw, dbg, dqw3, dkw3, dcb, dlnw, dlnb, loss_p):
    n3 = len(dqw3)

    def body(*refs):
        dsh_r, dsc_r, dgate_r, dnw_r, dbg_r = refs[:5]
        dq_r, dk_r = refs[5:5 + n3], refs[5 + n3:5 + 2 * n3]
        dcb_r, dlnw_r, dlnb_r, loss_r, o_ref = refs[5 + 2 * n3:]

        def both_heads(rs):
            t = rs[0][0:1, :]
            for r in rs[1:]:
                t = t + r[0:1, :]
            return t + pltpu.roll(t, HEAD_DIM, axis=1)

        o_ref[0:1, :] = dsh_r[0:1, :]
        o_ref[1:2, :] = dsc_r[0:1, :]
        o_ref[2:3, :] = dgate_r[0:1, :]
        o_ref[3:4, :] = dnw_r[0:1, :]
        o_ref[4:5, :] = dbg_r[0:1, 0:D_MODEL]
        o_ref[5:6, :] = dbg_r[0:1, D_MODEL:]
        o_ref[6:7, QN_COL:QN_COL + LANE] = both_heads(dq_r)
        o_ref[6:7, KN_COL:KN_COL + LANE] = both_heads(dk_r)
        o_ref[6:7, CB_COL:CB_COL + CONV_W] = dcb_r[0:1, :]
        o_ref[6:7, LOSS_COL:LOSS_COL + LANE] = loss_r[0:1, :]
        o_ref[6:7, LOSS_COL + LANE:] = jnp.zeros((1, D_MODEL - LOSS_COL - LANE), F32)
        o_ref[7:8, 0:CONV_W] = dlnw_r[0:1, :]
        o_ref[7:8, CONV_W:] = dlnb_r[0:1, :]

    return pl.pallas_call(body, name="pack_partials", out_shape=SDS((SMALL_ROWS, D_MODEL), F32),
                          compiler_params=_params())(dsh, dsc, dgate, dnw, dbg, *dqw3, *dkw3, dcb, dlnw, dlnb, loss_p)


def _adamw_update(g, w, m, v):
    bc1 = 1.0 - ADAM_B1 ** ADAM_STEP
    bc2 = 1.0 - ADAM_B2 ** ADAM_STEP
    m_new = ADAM_B1 * m + (1.0 - ADAM_B1) * g
    v_new = ADAM_B2 * v + (1.0 - ADAM_B2) * (g * g)
    delta = -ADAM_LR * ((m_new / bc1) / (jnp.sqrt(v_new / bc2) + ADAM_EPS) + ADAM_WD * w)
    return delta, m_new, v_new


def _adamw_small(small_all, ws, ms, vs):
    n = len(ws)
    where = [(slice(0, 3), None), (slice(3, 4), None), (slice(4, 6), None), (6, QN_COL), (6, KN_COL), (6, CB_COL),
             (7, 0), (7, CONV_W)]

    def body(*refs):
        g_ref = refs[0]
        w_r, m_r, v_r = refs[1:1 + n], refs[1 + n:1 + 2 * n], refs[1 + 2 * n:1 + 3 * n]
        outs = refs[1 + 3 * n:]
        g_o, d_o, m_o, v_o, loss_o = outs[:n], outs[n:2 * n], outs[2 * n:3 * n], outs[3 * n:4 * n], outs[4 * n]
        gsum = g_ref[0]
        for dev in range(1, N_DEV):
            gsum = gsum + g_ref[dev]
        loss_o[...] = gsum[6:7, LOSS_COL:LOSS_COL + LANE]
        for i, (rows, col) in enumerate(where):
            width = w_r[i].shape[1]
            if col is None:
                g = jnp.concatenate([gsum[r:r + 1, :] for r in range(rows.start, rows.stop)], axis=1)
            else:
                g = gsum[rows:rows + 1, col:col + width]
            delta, m_new, v_new = _adamw_update(g, w_r[i][...], m_r[i][...], v_r[i][...])
            g_o[i][...] = g
            d_o[i][...] = delta
            m_o[i][...] = m_new
            v_o[i][...] = v_new

    shapes = [SDS(w.shape, F32) for w in ws]
    res = pl.pallas_call(body, name="adamw_small", out_shape=shapes * 4 + [SDS((1, LANE), F32)],
                         compiler_params=_params())(small_all, *ws, *ms, *vs)
    return [res[k * n:(k + 1) * n] for k in range(4)], res[4 * n]


def _row_tile(rows):
    if rows <= 128:
        return rows
    return 128 if rows % 128 == 0 else SHARD_W // 4


def _adamw(gsrc, w, m, v, name, stacked):
    rows, cols = w.shape
    tr = _row_tile(rows)
    n_src = len(gsrc) if stacked else 1

    def body(*refs):
        g_refs, (w_ref, m_ref, v_ref, go_ref, d_ref, mo_ref, vo_ref) = refs[:n_src], refs[n_src:]
        if stacked:
            g = None
            for g_ref, (_, slots) in zip(g_refs, gsrc):
                for j in range(slots):
                    t = g_ref[j].astype(F32)
                    g = t if g is None else g + t
        else:
            g = g_refs[0][...]
        delta, m_new, v_new = _adamw_update(g, w_ref[...], m_ref[...], v_ref[...])
        go_ref[...] = g
        d_ref[...] = delta
        mo_ref[...] = m_new
        vo_ref[...] = v_new

    blk = pl.BlockSpec((tr, cols), lambda i: (i, 0))
    if stacked:
        gspecs = [pl.BlockSpec((slots, tr, arr.shape[2]), lambda i: (0, i, 0)) for arr, slots in gsrc]
        gargs = [arr for arr, _ in gsrc]
    else:
        gspecs, gargs = [blk], [gsrc]
    in_specs = gspecs + [blk, blk, blk]
    args = gargs + [w, m, v]
    return pl.pallas_call(
        body, name=name, grid=(rows // tr,), in_specs=in_specs, out_specs=[blk] * 4,
        out_shape=[SDS((rows, cols), F32)] * 4, compiler_params=_params(),
    )(*args)


def kernel(x, c, w_ada, b_ada, norm_w, w_in, b_gate, q_norm_w, k_norm_w, w_attn_proj, conv_w, conv_b, conv_ln_w, conv_ln_b, w_conv_proj, w_out, loss_target, m_w_ada, m_b_ada, m_norm_w, m_w_in, m_b_gate, m_q_norm_w, m_k_norm_w, m_w_attn_proj, m_conv_w, m_conv_b, m_conv_ln_w, m_conv_ln_b, m_w_conv_proj, m_w_out, v_w_ada, v_b_ada, v_norm_w, v_w_in, v_b_gate, v_q_norm_w, v_k_norm_w, v_w_attn_proj, v_conv_w, v_conv_b, v_conv_ln_w, v_conv_ln_b, v_w_conv_proj, v_w_out):
    xi, yi, ci = lax.axis_index("x"), lax.axis_index("y"), lax.axis_index("c")
    me = 4 * xi + 2 * yi + ci
    x2, tgt2 = x[0], loss_target[0]
    w_in_t, m_w_in_t, v_w_in_t = (jnp.transpose(a[0]) for a in (w_in, m_w_in, v_w_in))
    s = x2.shape[0]

    cw_flat = jnp.pad(conv_w[0].reshape(1, -1), ((0, 0), (0, CONVW_FLAT - CONV_K * HEAD_DIM)))
    pre = jnp.concatenate([c, cw_flat], axis=1).reshape(8, -1)
    (pre_all,) = _all_gather([pre], "gather_c_convw", vmem=True)
    pre_all = pre_all.reshape(N_DEV, -1)
    c_all = pre_all[:, :D_MODEL]
    convw_full = pre_all[:, D_MODEL:D_MODEL + CONV_K * HEAD_DIM].reshape(N_DEV, CONV_K, HEAD_DIM)
    convw_full = jnp.transpose(convw_full, (1, 0, 2)).reshape(CONV_K, CONV_W)
    convw_pad = jnp.pad(convw_full, ((0, CONV_HALO - CONV_K), (0, 0)))

    ada_part = _ada_fwd(c_all, w_ada[0])
    (ada_all,) = _all_gather([ada_part], "gather_ada", vmem=True)
    ada = lax.dynamic_index_in_dim(ada_all, me, axis=1, keepdims=False).reshape(1, 3 * D_MODEL) + b_ada
    shift, scale, gate = ada[:, :D_MODEL], ada[:, D_MODEL:2 * D_MODEL], ada[:, 2 * D_MODEL:]

    wt_g, wa_g, wc_g, wo_g = _all_gather_chips(
        [_cast_bf16(w_in_t, "cast_win"), _cast_bf16(w_attn_proj[0], "cast_wa"), _cast_bf16(w_conv_proj[0], "cast_wc"),
         _cast_bf16(w_out[0], "cast_wo")], "gather_weights")
    wt = wt_g.reshape(IN_W, D_MODEL)
    wa = _cols_from_slots(wa_g, "cols_wa")
    wc = _cols_from_slots(wc_g, "cols_wc")
    wo = wo_g.reshape(D_MODEL, D_MODEL)

    h, ht = _norm_fwd(x2, norm_w, scale, shift)
    proj = _mm_in(h, wt)
    qw2 = jnp.tile(q_norm_w, (1, 2))
    kw2 = jnp.tile(k_norm_w, (1, 2))
    o3, l3, qkv3 = [], [], []
    for g in range(N_GROUPS):
        o_g, l_g, qn_g, kn_g, vn_g = _attn_fwd(proj, qw2, kw2, g)
        o3.append(o_g)
        l3.append(l_g)
        qkv3.append((qn_g, kn_g, vn_g))
    head_id = jnp.arange(ATTN_W) // HEAD_DIM
    bd = (head_id[:, None] == head_id[None, :]).astype(BF16)
    (dout, da, lse_delta, dcv, mt, yat, yct, dmo, dya, dyc, dproj,
     dgate, dbg, dlnw, dlnb, dcb, loss_p) = _tail(
        x2, tgt2, proj, o3, l3, wa, wc, wo, gate, b_gate[:, :D_MODEL], b_gate[:, D_MODEL:], convw_pad,
        conv_b, conv_ln_w, conv_ln_b, bd)

    dproj, dconvw8 = _conv_bwd(dcv, proj, convw_pad, dproj)
    dconvw = jnp.sum(dconvw8.reshape(CONV_HALO, 8, CONV_W), axis=1)
    dqw_g3, dkw_g3 = [], []
    for g in range(N_GROUPS):
        dproj, dqw_g, dkw_g = _attn_bwd(proj, *qkv3[g], da, lse_delta, qw2, kw2, dproj, g)
        dqw_g3.append(dqw_g)
        dkw_g3.append(dkw_g)
    dw_in_p = _mm_dw(ht, dproj).reshape(N_DEV, SHARD_W, D_MODEL)
    dwo_p = _mm_acc(mt, dmo, "mm_dwo", col_slots=False).reshape(N_DEV, D_MODEL // N_DEV, D_MODEL)
    dwa_p = _mm_acc(yat, dya, "mm_dwa", col_slots=True)
    dwc_p = _mm_acc(yct, dyc, "mm_dwc", col_slots=True)

    partials = [dw_in_p, dwa_p, dwc_p, dwo_p]
    me_arr = jnp.reshape(me, (1,)).astype(jnp.int32)
    from_sib = _exchange_sibling(partials, "exchange_sibling")
    presums = [_presum(p, f, me_arr, f"presum{i}") for i, (p, f) in enumerate(zip(partials, from_sib))]
    s_sems, r_sems, pre_thru, land_thru, token = _exchange_chips_start(presums, "exchange_chips_start")
    dh = _mm_dh(dproj, wt, token)
    gx, dsh, dsc, dnw = _norm_bwd(x2, dh, dout, norm_w, scale)
    small_p = _pack_partials(dsh, dsc, dgate, dnw, dbg, dqw_g3, dkw_g3, dcb, dlnw, dlnb, loss_p)
    small_all, dconvw_all = _all_gather([small_p, dconvw], "gather_small", vmem=True)

    small_w = (b_ada, norm_w, b_gate, q_norm_w, k_norm_w, conv_b, conv_ln_w, conv_ln_b)
    small_m = (m_b_ada, m_norm_w, m_b_gate, m_q_norm_w, m_k_norm_w, m_conv_b, m_conv_ln_w, m_conv_ln_b)
    small_v = (v_b_ada, v_norm_w, v_b_gate, v_q_norm_w, v_k_norm_w, v_conv_b, v_conv_ln_w, v_conv_ln_b)
    r_small, loss_row = _adamw_small(small_all, small_w, small_m, small_v)
    dcw_mine = lax.dynamic_slice_in_dim(dconvw_all[:, :CONV_K, :], me * HEAD_DIM, HEAD_DIM, axis=2)
    r_convw = _adamw([(dcw_mine, N_DEV)], conv_w[0], m_conv_w[0], v_conv_w[0], "adamw_conv_w", stacked=True)

    d_ada_all = small_all[:, 0:3, :].reshape(N_DEV, 3 * D_MODEL)
    d_ada_cols = lax.dynamic_slice_in_dim(d_ada_all, me * (3 * D_MODEL // N_DEV), 3 * D_MODEL // N_DEV, axis=1)
    g_wada = _ada_bwd(c_all, d_ada_cols)
    r_ada = _adamw(g_wada, w_ada[0], m_w_ada[0], v_w_ada[0], "adamw_w_ada", stacked=False)
    pres, lands = _exchange_chips_wait(s_sems, r_sems, pre_thru, land_thru, r_ada[1], "exchange_chips_wait")
    terms = [[(p, 1), (l, len(CHIP_K))] for p, l in zip(pres, lands)]
    r_win = [jnp.transpose(r) for r in _adamw(terms[0], w_in_t, m_w_in_t, v_w_in_t, "adamw_w_in", stacked=True)]
    r_wap = _adamw(terms[1], w_attn_proj[0], m_w_attn_proj[0], v_w_attn_proj[0], "adamw_w_attn_proj", stacked=True)
    r_wcp = _adamw(terms[2], w_conv_proj[0], m_w_conv_proj[0], v_w_conv_proj[0], "adamw_w_conv_proj", stacked=True)
    r_wout = _adamw(terms[3], w_out[0], m_w_out[0], v_w_out[0], "adamw_w_out", stacked=True)

    outs = [loss_row[0, 0], gx[None]]
    for k in range(4):
        b_ada_k, norm_w_k, b_gate_k, qn_k, kn_k, conv_b_k, ln_w_k, ln_b_k = r_small[k]
        outs += [r_ada[k][None], b_ada_k, norm_w_k, r_win[k][None], b_gate_k, qn_k, kn_k, r_wap[k][None],
                 r_convw[k][None], conv_b_k, ln_w_k, ln_b_k, r_wcp[k][None], r_wout[k][None]]
    return tuple(outs)
```

```python
import functools

import jax
import jax.numpy as jnp
from jax import lax
from jax.experimental import pallas as pl
from jax.experimental.pallas import tpu as pltpu

F32 = jnp.float32
BF16 = jnp.bfloat16
SDS = jax.ShapeDtypeStruct
MESH = pl.DeviceIdType.MESH

N_DEV = 8
D_MODEL = 1024
HEAD_DIM = 64
N_GROUPS = 3
DILATIONS = (1, 4, 16)
BAND = 128
BWD_UNROLL = 8
ATTN_W = 512
CONV_W = 512
CONV_K = 31
CONV_HALO = 32
IN_W = 8704
SHARD_W = IN_W // N_DEV
PAIR_W = 2 * SHARD_W
Q0, K0, V0, ZA0, U0, ZC0, G0 = 0, 1536, 3072, 4608, 5120, 6144, 6656
EPS = 1e-6
LANE = 128
VMEM_LIMIT = 56 * 1024 * 1024

ADAM_LR, ADAM_B1, ADAM_B2, ADAM_EPS, ADAM_WD, ADAM_STEP = 0.001, 0.9, 0.999, 1e-08, 0.01, 10

CONVW_FLAT = 2048


def _params(**kw):
    return pltpu.CompilerParams(vmem_limit_bytes=VMEM_LIMIT, **kw)


def _sigmoid(z):
    return 0.5 * jnp.tanh(0.5 * z) + 0.5


def _dot(a, b):
    return jnp.dot(a, b, preferred_element_type=F32)


def _dot_nt(a, b):
    return lax.dot_general(a, b, (((1,), (1,)), ((), ())), preferred_element_type=F32)


def _dot_tn(a, b):
    return lax.dot_general(a, b, (((0,), (0,)), ((), ())), preferred_element_type=F32)


def _peer(x, y, c, k):
    px = 1 - x if (k >> 2) & 1 else x
    py = 1 - y if (k >> 1) & 1 else y
    pc = 1 - c if k & 1 else c
    return (px, py, pc), 4 * px + 2 * py + pc


def _all_gather(arrays, name, vmem):
    n = len(arrays)
    space = pltpu.VMEM if vmem else pl.ANY

    def body(*refs):
        ins, outs = refs[:n], refs[n:2 * n]
        send_sems, recv_sems, local_sems = refs[2 * n:]
        x, y, c = lax.axis_index("x"), lax.axis_index("y"), lax.axis_index("c")
        me = 4 * x + 2 * y + c
        locals_ = [pltpu.make_async_copy(ins[a], outs[a].at[me], local_sems.at[a]) for a in range(n)]
        for cp in locals_:
            cp.start()
        sends = []
        for k in range(1, N_DEV):
            peer, _ = _peer(x, y, c, k)
            for a in range(n):
                cp = pltpu.make_async_remote_copy(
                    src_ref=ins[a], dst_ref=outs[a].at[me], send_sem=send_sems.at[a, k - 1],
                    recv_sem=recv_sems.at[a, k - 1], device_id=peer, device_id_type=MESH)
                cp.start()
                sends.append(cp)
        for k in range(1, N_DEV):
            peer, pidx = _peer(x, y, c, k)
            for a in range(n):
                pltpu.make_async_remote_copy(
                    src_ref=ins[a], dst_ref=outs[a].at[pidx], send_sem=send_sems.at[a, k - 1],
                    recv_sem=recv_sems.at[a, k - 1], device_id=peer, device_id_type=MESH).wait_recv()
        for cp in sends:
            cp.wait_send()
        for cp in locals_:
            cp.wait()

    return pl.pallas_call(
        body, name=name,
        out_shape=[SDS((N_DEV,) + a.shape, a.dtype) for a in arrays],
        in_specs=[pl.BlockSpec(memory_space=space)] * n,
        out_specs=[pl.BlockSpec(memory_space=space)] * n,
        scratch_shapes=[pltpu.SemaphoreType.DMA((n, N_DEV - 1)), pltpu.SemaphoreType.DMA((n, N_DEV - 1)),
                        pltpu.SemaphoreType.DMA((n,))],
        compiler_params=_params(),
    )(*arrays)


CHIP_K = (2, 4, 6)


def _all_gather_chips(arrays, name):
    n = len(arrays)
    k_y, k_x, k_d = CHIP_K

    def body(*refs):
        ins, outs = refs[:n], refs[n:2 * n]
        send_sems, recv_sems, local_sems = refs[2 * n:]
        x, y, c = lax.axis_index("x"), lax.axis_index("y"), lax.axis_index("c")
        me = 4 * x + 2 * y + c
        sib, sib_idx = _peer(x, y, c, 1)
        nbr_y, idx_y = _peer(x, y, c, k_y)
        nbr_x, idx_x = _peer(x, y, c, k_x)
        _, idx_d = _peer(x, y, c, k_d)

        def copy(a, slot, block, to, src=None):
            return pltpu.make_async_remote_copy(
                src_ref=outs[a].at[block] if src is None else src, dst_ref=outs[a].at[block],
                send_sem=send_sems.at[a, slot], recv_sem=recv_sems.at[a, slot], device_id=to, device_id_type=MESH)

        locals_ = [pltpu.make_async_copy(ins[a], outs[a].at[me], local_sems.at[a]) for a in range(n)]
        for cp in locals_:
            cp.start()
        for a in range(n):
            copy(a, 0, me, sib, src=ins[a]).start()
            copy(a, 1, me, nbr_y, src=ins[a]).start()
            copy(a, 2, me, nbr_x, src=ins[a]).start()

        def arrived(slot, block, frm, send_on_to=None):
            for a in range(n):
                copy(a, slot, block, frm).wait_recv()
                if send_on_to is not None:
                    copy(a, 3, block, send_on_to).start()
                copy(a, 3 + slot, block, sib).start()

        @pl.when(c == 0)
        def _():
            arrived(1, idx_y, nbr_y, send_on_to=nbr_x)
            arrived(2, idx_x, nbr_x)

        @pl.when(c == 1)
        def _():
            arrived(2, idx_x, nbr_x, send_on_to=nbr_y)
            arrived(1, idx_y, nbr_y)

        arrived(3, idx_d, nbr_x)
        for a in range(n):
            copy(a, 0, sib_idx, sib).wait_recv()
        for slot, k in ((4, k_y), (5, k_x), (6, k_d)):
            _, pidx = _peer(x, y, 1 - c, k)
            for a in range(n):
                copy(a, slot, pidx, sib).wait_recv()
        for slot in range(N_DEV - 1):
            for a in range(n):
                copy(a, slot, me, sib).wait_send()
        for cp in locals_:
            cp.wait()

    return pl.pallas_call(
        body, name=name,
        out_shape=[SDS((N_DEV,) + a.shape, a.dtype) for a in arrays],
        in_specs=[pl.BlockSpec(memory_space=pl.ANY)] * n,
        out_specs=[pl.BlockSpec(memory_space=pl.ANY)] * n,
        scratch_shapes=[pltpu.SemaphoreType.DMA((n, N_DEV - 1)), pltpu.SemaphoreType.DMA((n, N_DEV - 1)),
                        pltpu.SemaphoreType.DMA((n,))],
        compiler_params=_params(),
    )(*arrays)


def _exchange_sibling(arrays, name):
    n = len(arrays)
    ks = (0,) + CHIP_K

    def body(*refs):
        ins, outs = refs[:n], refs[n:2 * n]
        send_sems, recv_sems = refs[2 * n:]
        x, y, c = lax.axis_index("x"), lax.axis_index("y"), lax.axis_index("c")
        sib, sib_idx = _peer(x, y, c, 1)
        sends = []
        for i, k in enumerate(ks):
            _, tgt = _peer(x, y, 1 - c, k) if k else (None, sib_idx)
            for a in range(n):
                cp = pltpu.make_async_remote_copy(
                    src_ref=ins[a].at[tgt], dst_ref=outs[a].at[i], send_sem=send_sems.at[a, i],
                    recv_sem=recv_sems.at[a, i], device_id=sib, device_id_type=MESH)
                cp.start()
                sends.append(cp)
        for cp in sends:
            cp.wait_recv()
        for cp in sends:
            cp.wait_send()

    return pl.pallas_call(
        body, name=name,
        out_shape=[SDS((len(ks),) + a.shape[1:], a.dtype) for a in arrays],
        in_specs=[pl.BlockSpec(memory_space=pl.ANY)] * n,
        out_specs=[pl.BlockSpec(memory_space=pl.ANY)] * n,
        scratch_shapes=[pltpu.SemaphoreType.DMA((n, len(ks))), pltpu.SemaphoreType.DMA((n, len(ks)))],
        compiler_params=_params(),
    )(*arrays)


def _presum(mine, from_sib, me_arr, name):
    _, rows, cols = mine.shape
    tr = _row_tile(rows)
    ns = 1 + len(CHIP_K)

    def body(me_ref, a_ref, b_ref, o_ref):
        del me_ref
        o_ref[...] = (a_ref[...].astype(F32) + b_ref[...].astype(F32)).astype(o_ref.dtype)

    grid_spec = pltpu.PrefetchScalarGridSpec(
        num_scalar_prefetch=1, grid=(ns, rows // tr),
        in_specs=[pl.BlockSpec((1, tr, cols), lambda j, i, me: (jnp.bitwise_xor(me[0], 2 * j), i, 0)),
                  pl.BlockSpec((1, tr, cols), lambda j, i, me: (j, i, 0))],
        out_specs=pl.BlockSpec((1, tr, cols), lambda j, i, me: (j, i, 0)))
    return pl.pallas_call(body, name=name, grid_spec=grid_spec, out_shape=SDS((ns, rows, cols), mine.dtype),
                          compiler_params=_params())(me_arr, mine, from_sib)


HBM_SPEC = pl.BlockSpec(memory_space=pltpu.HBM)
SEM_SPEC = pl.BlockSpec(memory_space=pltpu.SEMAPHORE)
SIDE_EFFECT = pltpu.SideEffectType.DATAFLOW_SIDE_EFFECTING


def _chips_copies(pre_refs, land_refs, send_sems, recv_sems):
    x, y, c = lax.axis_index("x"), lax.axis_index("y"), lax.axis_index("c")
    copies = []
    for j, k in enumerate(CHIP_K):
        peer, _ = _peer(x, y, c, k)
        for a in range(len(pre_refs)):
            copies.append(pltpu.make_async_remote_copy(
                src_ref=pre_refs[a].at[1 + j], dst_ref=land_refs[a].at[j], send_sem=send_sems.at[a * len(CHIP_K) + j],
                recv_sem=recv_sems.at[a * len(CHIP_K) + j], device_id=peer, device_id_type=MESH))
    return copies


def _exchange_chips_start(presums, name):
    n = len(presums)

    def body(*refs):
        pre, land = refs[:n], refs[n:2 * n]
        send_sems, recv_sems = refs[2 * n], refs[2 * n + 1]
        token = refs[-1]
        for cp in _chips_copies(pre, land, send_sems, recv_sems):
            cp.start()
        token[...] = jnp.zeros_like(token)

    nk = len(CHIP_K)
    hbm = [pltpu.HBM(p.shape, p.dtype) for p in presums]
    hbm_land = [pltpu.HBM((nk,) + p.shape[1:], p.dtype) for p in presums]
    res = pl.pallas_call(
        body, name=name,
        out_shape=(pltpu.SemaphoreType.DMA((n * nk,)), pltpu.SemaphoreType.DMA((n * nk,)), *hbm, *hbm_land, SDS((8, LANE), F32)),
        in_specs=[HBM_SPEC] * (2 * n),
        out_specs=(SEM_SPEC, SEM_SPEC, *([HBM_SPEC] * (2 * n)), pl.BlockSpec(memory_space=pltpu.VMEM)),
        input_output_aliases={i: 2 + i for i in range(2 * n)},
        compiler_params=pltpu.CompilerParams(has_side_effects=SIDE_EFFECT),
    )(*[pltpu.with_memory_space_constraint(p, pltpu.HBM) for p in presums],
      *[pltpu.with_memory_space_constraint(lax.empty((nk,) + p.shape[1:], p.dtype), pltpu.HBM) for p in presums])
    return res[0], res[1], res[2:2 + n], res[2 + n:2 + 2 * n], res[-1]


def _exchange_chips_wait(send_sems, recv_sems, pre_thru, land_thru, after, name):
    n = len(pre_thru)

    def body(*refs):
        pre, land = refs[:n], refs[n:2 * n]
        s_sems, r_sems = refs[2 * n], refs[2 * n + 1]
        for cp in _chips_copies(pre, land, s_sems, r_sems):
            cp.wait_send()
            cp.wait_recv()

    hbm = [pltpu.HBM(p.shape, p.dtype) for p in (*pre_thru, *land_thru)]
    res = pl.pallas_call(
        body, name=name, out_shape=tuple(hbm),
        in_specs=[HBM_SPEC] * (2 * n) + [SEM_SPEC, SEM_SPEC, pl.BlockSpec(memory_space=pl.ANY)],
        out_specs=tuple([HBM_SPEC] * (2 * n)),
        input_output_aliases={i: i for i in range(2 * n)},
        compiler_params=pltpu.CompilerParams(has_side_effects=SIDE_EFFECT),
    )(*pre_thru, *land_thru, send_sems, recv_sems, after)
    return res[:n], res[n:]


def _exchange_chips(presums, name):
    n = len(presums)
    nk = len(CHIP_K)

    def body(*refs):
        pre, land = refs[:n], refs[n:2 * n]
        send_sems, recv_sems = refs[2 * n:]
        copies = _chips_copies(pre, land, send_sems, recv_sems)
        for cp in copies:
            cp.start()
        for cp in copies:
            cp.wait_recv()
        for cp in copies:
            cp.wait_send()

    return pl.pallas_call(
        body, name=name,
        out_shape=[SDS((nk,) + p.shape[1:], p.dtype) for p in presums],
        in_specs=[pl.BlockSpec(memory_space=pl.ANY)] * n,
        out_specs=[pl.BlockSpec(memory_space=pl.ANY)] * n,
        scratch_shapes=[pltpu.SemaphoreType.DMA((n * nk,)), pltpu.SemaphoreType.DMA((n * nk,))],
        compiler_params=_params(),
    )(*presums)


def _cast_bf16(w, name):
    def body(w_ref, o_ref):
        o_ref[...] = w_ref[...].astype(BF16)

    return pl.pallas_call(body, name=name, out_shape=SDS(w.shape, BF16), compiler_params=_params())(w)


def _cols_from_slots(wg, name):
    _, rows, cols = wg.shape

    def body(w_ref, o_ref):
        for j in range(N_DEV):
            o_ref[:, j * cols:(j + 1) * cols] = w_ref[j]

    return pl.pallas_call(body, name=name, out_shape=SDS((rows, N_DEV * cols), wg.dtype), compiler_params=_params())(wg)


def _ada_fwd(c_all, w_ada):
    def body(c_ref, w_ref, o_ref):
        cv = c_ref[...]
        sc = (cv * _sigmoid(cv)).astype(BF16)
        o_ref[...] = _dot(sc, w_ref[...].astype(BF16))

    return pl.pallas_call(body, name="ada_fwd", out_shape=SDS((N_DEV, w_ada.shape[1]), F32),
                          compiler_params=_params())(c_all, w_ada)


def _ada_bwd(c_all, d_ada_cols):
    def body(c_ref, d_ref, o_ref):
        cv = c_ref[...]
        sc = (cv * _sigmoid(cv)).astype(BF16)
        o_ref[...] = _dot_tn(sc, d_ref[...].astype(BF16))

    return pl.pallas_call(body, name="ada_bwd", out_shape=SDS((D_MODEL, d_ada_cols.shape[1]), F32),
                          compiler_params=_params())(c_all, d_ada_cols)


def _norm_fwd(x, norm_w, scale, shift):
    s = x.shape[0]
    tr = 512

    def body(x_ref, nw_ref, sc_ref, sh_ref, h_ref, ht_ref):
        xv = x_ref[...]
        r = lax.rsqrt(jnp.mean(xv * xv, axis=-1, keepdims=True) + EPS)
        h = (xv * r * nw_ref[...]) * (1.0 + sc_ref[...]) + sh_ref[...]
        h_ref[...] = h.astype(BF16)
        ht_ref[...] = h.T.astype(BF16)

    vec = pl.BlockSpec((1, D_MODEL), lambda i: (0, 0))
    return pl.pallas_call(
        body, name="norm_fwd", grid=(s // tr,),
        in_specs=[pl.BlockSpec((tr, D_MODEL), lambda i: (i, 0)), vec, vec, vec],
        out_specs=[pl.BlockSpec((tr, D_MODEL), lambda i: (i, 0)), pl.BlockSpec((D_MODEL, tr), lambda i: (0, i))],
        out_shape=[SDS((s, D_MODEL), BF16), SDS((D_MODEL, s), BF16)], compiler_params=_params(),
    )(x, norm_w, scale, shift)


def _mm_in(h, wt):
    s = h.shape[0]
    tm = 512

    def body(h_ref, w_ref, o_ref):
        o_ref[...] = _dot_nt(h_ref[...], w_ref[...])

    return pl.pallas_call(
        body, name="mm_in", grid=(IN_W // PAIR_W, s // tm),
        in_specs=[pl.BlockSpec((tm, D_MODEL), lambda p, m: (m, 0)),
                  pl.BlockSpec((PAIR_W, D_MODEL), lambda p, m: (p, 0))],
        out_specs=pl.BlockSpec((tm, PAIR_W), lambda p, m: (m, p)),
        out_shape=SDS((s, IN_W), F32), compiler_params=_params(),
    )(h, wt)


def _head_ones():
    a = lax.broadcasted_iota(jnp.int32, (LANE, LANE), 0) // HEAD_DIM
    b = lax.broadcasted_iota(jnp.int32, (LANE, LANE), 1) // HEAD_DIM
    return (a == b).astype(BF16)


def _head_sums(t, ones):
    return _dot(t.astype(BF16), ones)


def _band_bias(bias, transposed=False):
    qi = lax.broadcasted_iota(jnp.int32, (2 * BAND, 2 * BAND), 1 if transposed else 0) % BAND
    kj = lax.broadcasted_iota(jnp.int32, (2 * BAND, 2 * BAND), 0 if transposed else 1)
    dist = qi + BAND - kj
    valid = (dist >= 0) & (dist <= BAND)
    bias[1] = jnp.where(valid, 0.0, -1e30)
    bias[0] = jnp.where(valid & (kj >= BAND), 0.0, -1e30)


def _token_rows(j, d, chunk, per_r):
    return pl.ds(j // per_r + (j % per_r) * (chunk * d), chunk, stride=d)


def _deinterleave(src_ref, dst_ref, w_ref, ones, d, sub_len, chunk, scale, dst_off):
    per_r = sub_len // chunk

    def step(j, _):
        t = src_ref[_token_rows(j, d, chunk, per_r), :]
        if w_ref is not None:
            ms = _head_sums(t * t, ones) * (1.0 / HEAD_DIM)
            t = t * lax.rsqrt(ms + EPS) * (w_ref[...] * scale)
        dst_ref[pl.ds(pl.multiple_of(dst_off + j * chunk, BAND), chunk), :] = t.astype(dst_ref.dtype)
        return 0
    lax.fori_loop(0, d * per_r, step, 0, unroll=4)


N_PAIRS = ATTN_W // LANE


def _attn_fwd(proj, qw2, kw2):
    s = proj.shape[0]

    def group_body(g, step, q_ref, k_ref, v_ref, qw_ref, kw_ref, o_ref, l_ref, qn_ref, kn_ref, vn_ref,
                   qd, kd, vd, od, ld, bias):
        d = DILATIONS[g]
        sub_len = s // d
        nb = sub_len // BAND
        chunk = min(sub_len, 256)
        lo = lax.broadcasted_iota(jnp.int32, (1, LANE), 1) < HEAD_DIM
        ones = _head_ones()

        @pl.when(step == 0)
        def _():
            _band_bias(bias)

        kd[0:BAND, :] = jnp.zeros((BAND, LANE), BF16)
        vd[0:BAND, :] = jnp.zeros((BAND, LANE), BF16)
        _deinterleave(q_ref, qd, qw_ref, ones, d, sub_len, chunk, HEAD_DIM ** -0.5, 0)
        _deinterleave(k_ref, kd, kw_ref, ones, d, sub_len, chunk, 1.0, BAND)
        _deinterleave(v_ref, vd, None, ones, d, sub_len, chunk, 1.0, BAND)
        qn_ref[...] = qd[...]
        kn_ref[...] = kd[BAND:BAND + s, :]
        vn_ref[...] = vd[BAND:BAND + s, :]

        def block(t, _):
            base = pl.multiple_of(t * BAND, BAND)
            q = qd[pl.ds(base, BAND), :]
            k2 = kd[pl.ds(base, 2 * BAND), :]
            v2 = vd[pl.ds(base, 2 * BAND), :]
            zero = jnp.zeros_like(q)
            qs = jnp.concatenate([jnp.where(lo, q, zero), jnp.where(lo, zero, q)], axis=0)
            sc = _dot_nt(qs, k2) + bias[jnp.minimum(t % nb, 1)]
            m = jnp.max(sc, axis=-1, keepdims=True)
            p = jnp.exp(sc - m)
            den = jnp.sum(p, axis=-1, keepdims=True)
            u = _dot(p.astype(BF16), v2) * (1.0 / den)
            lse = m + jnp.log(den)
            od[pl.ds(base, BAND), :] = jnp.where(lo, u[:BAND], u[BAND:])
            ld[pl.ds(base, BAND), :] = jnp.where(lo, lse[:BAND], lse[BAND:])
            return 0
        lax.fori_loop(0, s // BAND, block, 0, unroll=16)

        per_r = sub_len // chunk

        def back(j, _):
            src = pl.ds(pl.multiple_of(j * chunk, chunk), chunk)
            dst = _token_rows(j, d, chunk, per_r)
            o_ref[dst, :] = od[src, :]
            l_ref[dst, :] = ld[src, :]
            return 0
        lax.fori_loop(0, d * per_r, back, 0, unroll=2)

    def body(*refs):
        step = pl.program_id(0)
        for g in range(N_GROUPS):
            pl.when(step // N_PAIRS == g)(functools.partial(group_body, g, step, *refs))

    col = lambda off: pl.BlockSpec((s, LANE), lambda i, off=off: (0, off // LANE + i))
    vec = pl.BlockSpec((1, LANE), lambda i: (0, 0))
    out = pl.BlockSpec((s, LANE), lambda i: (0, i))
    width = N_GROUPS * ATTN_W
    return pl.pallas_call(
        body, name="attn_fwd", grid=(N_GROUPS * N_PAIRS,),
        in_specs=[col(Q0), col(K0), col(V0), vec, vec], out_specs=[out] * 5,
        out_shape=[SDS((s, width), F32)] * 2 + [SDS((s, width), BF16)] * 3,
        scratch_shapes=[pltpu.VMEM((s, LANE), BF16), pltpu.VMEM((s + BAND, LANE), BF16), pltpu.VMEM((s + BAND, LANE), BF16),
                        pltpu.VMEM((s, LANE), F32), pltpu.VMEM((s, LANE), F32),
                        pltpu.VMEM((2, 2 * BAND, 2 * BAND), F32)],
        compiler_params=_params(),
    )(proj, proj, proj, qw2, kw2)


def _attn_bwd(proj, qn, kn, vn, da, lse_delta, qw2, kw2, dproj):
    s = proj.shape[0]
    n_steps = N_GROUPS * N_PAIRS

    def group_body(g, hp, q_ref, k_ref, qn_ref, kn_ref, vn_ref, da_ref, ld_ref, qw_ref, kw_ref, dp_in, dp_out,
                   dqw_ref, dkw_ref, kd, vd, kdt, dad, lst, dlt, dqt, dqd, dkd, dvd, st, stb, bias_t, wacc, sem):
        del dp_in
        d = DILATIONS[g]
        sub_len = s // d
        nb = sub_len // BAND
        chunk = min(sub_len, 256)
        lo = lax.broadcasted_iota(jnp.int32, (1, LANE), 1) < HEAD_DIM
        row_lo = lax.broadcasted_iota(jnp.int32, (LANE, 1), 0) < HEAD_DIM
        ones = _head_ones()
        per_r = sub_len // chunk
        cblk = chunk // BAND

        @pl.when(hp == 0)
        def _():
            _band_bias(bias_t, transposed=True)

        kd[0:BAND, :] = jnp.zeros((BAND, LANE), BF16)
        vd[0:BAND, :] = jnp.zeros((BAND, LANE), BF16)
        kdt[0] = jnp.zeros((LANE, BAND), BF16)
        kd[BAND:BAND + s, :] = kn_ref[...]
        vd[BAND:BAND + s, :] = vn_ref[...]

        def k_step(t, _):
            kdt[1 + t] = kn_ref[pl.ds(pl.multiple_of(t * BAND, BAND), BAND), :].astype(F32).T.astype(BF16)
            return 0
        lax.fori_loop(0, s // BAND, k_step, 0, unroll=4)
        _deinterleave(da_ref, dad, None, ones, d, sub_len, chunk, 1.0, 0)

        def rows_step(j, _):
            tok = _token_rows(j, d, chunk, per_r)
            tt = ld_ref[tok, :].T
            for u in range(cblk):
                cols = slice(u * BAND, (u + 1) * BAND)
                lst[j * cblk + u, 0:1, :] = tt[0:1, cols]
                lst[j * cblk + u, 1:2, :] = tt[HEAD_DIM:HEAD_DIM + 1, cols]
                dlt[j * cblk + u, 0:1, :] = tt[HEAD_DIM // 2:HEAD_DIM // 2 + 1, cols]
                dlt[j * cblk + u, 1:2, :] = tt[HEAD_DIM + HEAD_DIM // 2:HEAD_DIM + HEAD_DIM // 2 + 1, cols]
            return 0
        lax.fori_loop(0, d * per_r, rows_step, 0, unroll=4)

        def block(t, carry):
            ck, cv = carry
            base = pl.multiple_of(t * BAND, BAND)
            q = qn_ref[pl.ds(base, BAND), :]
            k2 = kd[pl.ds(base, 2 * BAND), :]
            v2 = vd[pl.ds(base, 2 * BAND), :]
            k2t = jnp.concatenate([kdt[t], kdt[t + 1]], axis=1)
            dav = dad[pl.ds(base, BAND), :]
            zero = jnp.zeros_like(q)
            qs = jnp.concatenate([jnp.where(lo, q, zero), jnp.where(lo, zero, q)], axis=0)
            das = jnp.concatenate([jnp.where(lo, dav, zero), jnp.where(lo, zero, dav)], axis=0)
            ls_row = jnp.concatenate([lst[t, 0:1, :], lst[t, 1:2, :]], axis=1)
            dl_row = jnp.concatenate([dlt[t, 0:1, :], dlt[t, 1:2, :]], axis=1)
            sc_t = _dot_nt(k2, qs) + bias_t[jnp.minimum(t % nb, 1)]
            p_t = jnp.exp(sc_t - ls_row)
            dp_t = _dot_nt(v2, das)
            ds_t = (p_t * (dp_t - dl_row)).astype(BF16)
            dv2 = _dot(p_t.astype(BF16), das)
            dk2 = _dot(ds_t, qs)
            dvd[pl.ds(base, BAND), :] = cv + dv2[:BAND]
            dkd[pl.ds(base, BAND), :] = ck + dk2[:BAND]
            dq_t = _dot(k2t, ds_t)
            dqt[t] = jnp.where(row_lo, dq_t[:, :BAND], dq_t[:, BAND:])
            return dk2[BAND:], dv2[BAND:]

        def blocks(i, carry):
            for u in range(BWD_UNROLL):
                carry = block(i * BWD_UNROLL + u, carry)
            return carry
        zeros = jnp.zeros((BAND, LANE), F32)
        ck, cv = lax.fori_loop(0, s // (BAND * BWD_UNROLL), blocks, (zeros, zeros))
        dkd[s:s + BAND, :] = ck
        dvd[s:s + BAND, :] = cv

        def dq_rows(t, _):
            dqd[pl.ds(pl.multiple_of(t * BAND, BAND), BAND), :] = dqt[t].T
            return 0
        lax.fori_loop(0, s // BAND, dq_rows, 0, unroll=4)

        def col_copy(slot, col0):
            return pltpu.make_async_copy(
                stb.at[slot], dp_out.at[:, pl.ds(pl.multiple_of(col0 + LANE * hp, LANE), LANE)], sem.at[slot])

        def store_cols(slot, col0):
            @pl.when(hp > 0)
            def _():
                col_copy(slot, col0).wait()
            stb[slot] = st[...].astype(BF16)
            col_copy(slot, col0).start()

        def norm_back(src_ref, dy_ref, dy_off, w_ref, scale, dw_ref, slot, col0):
            wacc[...] = jnp.zeros_like(wacc)

            def step(j, _):
                tok = _token_rows(j, d, chunk, per_r)
                t = src_ref[tok, :]
                dy = dy_ref[pl.ds(pl.multiple_of(dy_off + j * chunk, BAND), chunk), :]
                rr = lax.rsqrt(_head_sums(t * t, ones) * (1.0 / HEAD_DIM) + EPS)
                nrm = t * rr
                wacc[...] += jnp.sum((dy * nrm).reshape(chunk // 8, 8, LANE), axis=0)
                dn = dy * (w_ref[...] * scale)
                st[tok, :] = rr * (dn - nrm * (_head_sums(dn * nrm, ones) * (1.0 / HEAD_DIM)))
                return 0
            lax.fori_loop(0, d * per_r, step, 0, unroll=4)
            dw_ref[...] += jnp.broadcast_to(jnp.sum(wacc[...], axis=0, keepdims=True) * scale, dw_ref.shape)
            store_cols(slot, col0)

        @pl.when(hp == 0)
        def _():
            dqw_ref[...] = jnp.zeros_like(dqw_ref)
            dkw_ref[...] = jnp.zeros_like(dkw_ref)

        norm_back(q_ref, dqd, 0, qw_ref, HEAD_DIM ** -0.5, dqw_ref, 0, Q0)
        norm_back(k_ref, dkd, BAND, kw_ref, 1.0, dkw_ref, 1, K0)

        def v_back(j, _):
            src = pl.ds(pl.multiple_of(BAND + j * chunk, BAND), chunk)
            st[_token_rows(j, d, chunk, per_r), :] = dvd[src, :]
            return 0
        lax.fori_loop(0, d * per_r, v_back, 0, unroll=2)
        store_cols(2, V0)

        @pl.when(hp == n_steps - 1)
        def _():
            for slot, col0 in enumerate((Q0, K0, V0)):
                col_copy(slot, col0).wait()

    def body(*refs):
        step = pl.program_id(0)
        for g in range(N_GROUPS):
            pl.when(step // N_PAIRS == g)(functools.partial(group_body, g, step, *refs))

    col = lambda off: pl.BlockSpec((s, LANE), lambda i, off=off: (0, off // LANE + i))
    mid = pl.BlockSpec((s, LANE), lambda i: (0, i))
    slot4 = pl.BlockSpec((s, LANE), lambda i: (0, i % N_PAIRS))
    vec = pl.BlockSpec((1, LANE), lambda i: (0, 0))
    acc = pl.BlockSpec((8, LANE), lambda i: (0, 0))
    any_ = pl.BlockSpec(memory_space=pl.ANY)
    return pl.pallas_call(
        body, name="attn_bwd", grid=(n_steps,),
        in_specs=[col(Q0), col(K0), mid, mid, mid, slot4, slot4, vec, vec, any_],
        out_specs=[any_, acc, acc],
        out_shape=[SDS(dproj.shape, dproj.dtype), SDS((8, LANE), F32), SDS((8, LANE), F32)],
        input_output_aliases={9: 0},
        scratch_shapes=[pltpu.VMEM((s + BAND, LANE), BF16), pltpu.VMEM((s + BAND, LANE), BF16),
                        pltpu.VMEM((s // BAND + 1, LANE, BAND), BF16), pltpu.VMEM((s, LANE), BF16),
                        pltpu.VMEM((s // BAND, 8, BAND), F32), pltpu.VMEM((s // BAND, 8, BAND), F32),
                        pltpu.VMEM((s // BAND, LANE, BAND), F32),
                        pltpu.VMEM((s, LANE), F32), pltpu.VMEM((s + BAND, LANE), F32), pltpu.VMEM((s + BAND, LANE), F32),
                        pltpu.VMEM((s, LANE), F32), pltpu.VMEM((3, s, LANE), BF16),
                        pltpu.VMEM((2, 2 * BAND, 2 * BAND), F32), pltpu.VMEM((8, LANE), F32),
                        pltpu.SemaphoreType.DMA((3,))],
        compiler_params=_params(),
    )(proj, proj, qn, kn, vn, da, lse_delta, qw2, kw2, dproj)


def _tap_views(ext_ref, sh_ref, offsets, tr, cols):
    for b in range(8):
        group = [j for j, o in enumerate(offsets) if o % 8 == b]
        if not group:
            continue
        first = min(offsets[j] for j in group)
        span = tr + max(offsets[j] for j in group) - first
        sh_ref[0:span, cols] = ext_ref[first:first + span, cols]
        for j in group:
            yield j, sh_ref[offsets[j] - first:offsets[j] - first + tr, cols]


def _silu_grad(z, sg):
    return sg * (1.0 + z * (1.0 - sg))


def _glu(u):
    a_h, b_h = u[:, :CONV_W], u[:, CONV_W:]
    sg = _sigmoid(b_h)
    return a_h, sg, a_h * sg


def _tail(x, tgt, proj, o3, l3, wa, wc, wo, gate, bga, bgc, convw, convb, lnw, lnb, bd):
    s = x.shape[0]
    tr = 256

    def body(x_ref, t_ref, za_ref, u_ref, uh_ref, zc_ref, g0_ref, g1_ref, g2_ref, g3_ref,
             o0_ref, o1_ref, o2_ref, l0_ref, l1_ref, l2_ref, wa_ref, wc_ref, wo_ref,
             gate_ref, bga_ref, bgc_ref, cw_ref, cb_ref, lnw_ref, lnb_ref, bd_ref,
             dout_ref, da_ref, ld_ref, dcv_ref, mt_ref, yat_ref, yct_ref, dmo_ref, dya_ref, dyc_ref, dp_ref,
             dgate_ref, dbg_ref, dlnw_ref, dlnb_ref, dcb_ref, loss_ref,
             ext, sh, st_za, st_zc, st_g, sems):
        i = pl.program_id(0)

        @pl.when(i == 0)
        def _():
            for r in (dgate_ref, dbg_ref, dlnw_ref, dlnb_ref, dcb_ref, loss_ref):
                r[...] = jnp.zeros_like(r)

        def acc_rows(ref, v):
            ref[...] += jnp.broadcast_to(jnp.sum(v, axis=0, keepdims=True), ref.shape)

        la, lb, lc = l0_ref[...], l1_ref[...], l2_ref[...]
        mx = jnp.maximum(jnp.maximum(la, lb), lc)
        ea, eb, ec = jnp.exp(la - mx), jnp.exp(lb - mx), jnp.exp(lc - mx)
        den = ea + eb + ec
        inv = 1.0 / den
        attn = (ea * inv) * o0_ref[...] + (eb * inv) * o1_ref[...] + (ec * inv) * o2_ref[...]
        lse = mx + jnp.log(den)

        za = za_ref[...]
        sga = _sigmoid(za)
        sa = za * sga
        ya_in = attn * sa
        y_attn = _dot(ya_in.astype(BF16), wa_ref[...])

        _, _, glu = _glu(u_ref[...])
        _, _, glu_h = _glu(uh_ref[...])
        ext[0:CONV_HALO, :] = jnp.where(i > 0, glu_h, 0.0)
        ext[CONV_HALO:CONV_HALO + tr, :] = glu
        cv_blocks = []
        for cb in range(CONV_W // LANE):
            cols = slice(cb * LANE, (cb + 1) * LANE)
            cv_c = jnp.broadcast_to(cb_ref[:, cols], (tr, LANE))
            for j, rows in _tap_views(ext, sh, [CONV_HALO - (CONV_K - 1) + j for j in range(CONV_K)], tr, cols):
                cv_c = cv_c + cw_ref[j:j + 1, cols] * rows
            cv_blocks.append(cv_c)
        cv = jnp.concatenate(cv_blocks, axis=1)
        mu = jnp.mean(cv, axis=-1, keepdims=True)
        xc = cv - mu
        rstd = lax.rsqrt(jnp.mean(xc * xc, axis=-1, keepdims=True) + EPS)
        nrm = xc * rstd
        ln = nrm * lnw_ref[...] + lnb_ref[...]
        sgl = _sigmoid(ln)
        cs = ln * sgl
        zc = zc_ref[...]
        sgc = _sigmoid(zc)
        scz = zc * sgc
        yc_in = cs * scz
        y_conv = _dot(yc_in.astype(BF16), wc_ref[...])

        ga = _sigmoid(jnp.concatenate([g0_ref[...], g1_ref[...]], axis=1) + bga_ref[...])
        gc = _sigmoid(jnp.concatenate([g2_ref[...], g3_ref[...]], axis=1) + bgc_ref[...])
        merged = ga * y_attn + gc * y_conv
        mo = _dot(merged.astype(BF16), wo_ref[...])
        gate_v = gate_ref[...]
        err = (x_ref[...] + gate_v * mo) - t_ref[...]
        loss_ref[...] += 0.5 * jnp.sum(jnp.mean(err * err, axis=-1, keepdims=True))
        d_out = err * (1.0 / D_MODEL)
        dout_ref[...] = d_out

        rows = pl.ds(pl.multiple_of(i * tr, tr), tr)
        cps = [pltpu.make_async_copy(st_za, dp_ref.at[rows, pl.ds(ZA0, ATTN_W)], sems.at[0]),
               pltpu.make_async_copy(st_zc, dp_ref.at[rows, pl.ds(ZC0, CONV_W)], sems.at[1]),
               pltpu.make_async_copy(st_g, dp_ref.at[rows, pl.ds(G0, 2 * D_MODEL)], sems.at[2])]

        @pl.when(i > 0)
        def _():
            for cp in cps:
                cp.wait()

        acc_rows(dgate_ref, d_out * mo)
        dmo_b = (d_out * gate_v).astype(BF16)
        dmo_ref[...] = dmo_b
        mt_ref[...] = merged.T.astype(BF16)
        d_merged = _dot_nt(dmo_b, wo_ref[...])
        d_ya = (d_merged * ga).astype(BF16)
        d_yc = (d_merged * gc).astype(BF16)
        dya_ref[...] = d_ya
        dyc_ref[...] = d_yc
        dga = d_merged * y_attn * (ga * (1.0 - ga))
        dgc = d_merged * y_conv * (gc * (1.0 - gc))
        dgs = jnp.concatenate([dga, dgc], axis=1)
        acc_rows(dbg_ref, dgs)
        st_g[...] = dgs.astype(BF16)

        yat_ref[...] = ya_in.T.astype(BF16)
        d_ya_in = _dot_nt(d_ya, wa_ref[...])
        d_attn = d_ya_in * sa
        da_ref[...] = d_attn
        st_za[...] = (d_ya_in * attn * _silu_grad(za, sga)).astype(BF16)
        prod = d_attn * attn
        hi = prod.astype(BF16)
        lo_ = (prod - hi.astype(F32)).astype(BF16)
        delta = _dot(hi, bd_ref[...]) + _dot(lo_, bd_ref[...])
        first_half = (lax.broadcasted_iota(jnp.int32, (1, ATTN_W), 1) % HEAD_DIM) < HEAD_DIM // 2
        ld_ref[...] = jnp.where(first_half, lse, delta)

        yct_ref[...] = yc_in.T.astype(BF16)
        d_yc_in = _dot_nt(d_yc, wc_ref[...])
        st_zc[...] = (d_yc_in * cs * _silu_grad(zc, sgc)).astype(BF16)
        d_ln = (d_yc_in * scz) * _silu_grad(ln, sgl)
        acc_rows(dlnw_ref, d_ln * nrm)
        acc_rows(dlnb_ref, d_ln)
        d_nrm = d_ln * lnw_ref[...]
        d_cv = rstd * (d_nrm - jnp.mean(d_nrm, axis=-1, keepdims=True)
                       - nrm * jnp.mean(d_nrm * nrm, axis=-1, keepdims=True))
        acc_rows(dcb_ref, d_cv)
        dcv_ref[...] = d_cv

        for cp in cps:
            cp.start()

        @pl.when(i == s // tr - 1)
        def _():
            for cp in cps:
                cp.wait()

    def rows(width, colblk=0):
        return pl.BlockSpec((tr, width), lambda i, colblk=colblk: (i, colblk))

    def const(shape):
        return pl.BlockSpec(shape, lambda i: (0,) * len(shape))

    halo = pl.BlockSpec((CONV_HALO, D_MODEL), lambda i: (jnp.maximum(i * (tr // CONV_HALO) - 1, 0), U0 // D_MODEL))
    in_specs = [rows(D_MODEL), rows(D_MODEL), rows(ATTN_W, ZA0 // ATTN_W), rows(D_MODEL, U0 // D_MODEL), halo,
                rows(CONV_W, ZC0 // CONV_W)]
    in_specs += [rows(512, G0 // 512 + j) for j in range(4)]
    in_specs += [rows(ATTN_W, g) for g in range(N_GROUPS)] * 2
    in_specs += [const(wa.shape), const(wc.shape), const(wo.shape), const((1, D_MODEL)), const((1, D_MODEL)),
                 const((1, D_MODEL)), const(convw.shape), const((1, CONV_W)), const((1, CONV_W)), const((1, CONV_W)),
                 const(bd.shape)]
    tcol = lambda width: pl.BlockSpec((width, tr), lambda i: (0, i))
    out_specs = [rows(D_MODEL), rows(ATTN_W), rows(ATTN_W), rows(CONV_W),
                 tcol(D_MODEL), tcol(ATTN_W), tcol(CONV_W), rows(D_MODEL), rows(D_MODEL), rows(D_MODEL),
                 pl.BlockSpec(memory_space=pl.ANY),
                 const((8, D_MODEL)), const((8, 2 * D_MODEL)), const((8, CONV_W)), const((8, CONV_W)), const((8, CONV_W)),
                 const((8, LANE))]
    out_shape = [SDS((s, D_MODEL), F32), SDS((s, ATTN_W), F32), SDS((s, ATTN_W), F32),
                 SDS((s, CONV_W), F32),
                 SDS((D_MODEL, s), BF16), SDS((ATTN_W, s), BF16), SDS((CONV_W, s), BF16),
                 SDS((s, D_MODEL), BF16), SDS((s, D_MODEL), BF16), SDS((s, D_MODEL), BF16),
                 SDS((s, IN_W), BF16),
                 SDS((8, D_MODEL), F32), SDS((8, 2 * D_MODEL), F32), SDS((8, CONV_W), F32), SDS((8, CONV_W), F32),
                 SDS((8, CONV_W), F32), SDS((8, LANE), F32)]
    return pl.pallas_call(
        body, name="tail", grid=(s // tr,), in_specs=in_specs, out_specs=out_specs, out_shape=out_shape,
        scratch_shapes=[pltpu.VMEM((CONV_HALO + tr, CONV_W), F32), pltpu.VMEM((CONV_HALO + tr, CONV_W), F32),
                        pltpu.VMEM((tr, ATTN_W), BF16),
                        pltpu.VMEM((tr, CONV_W), BF16), pltpu.VMEM((tr, 2 * D_MODEL), BF16),
                        pltpu.SemaphoreType.DMA((3,))],
        compiler_params=_params(),
    )(x, tgt, proj, proj, proj, proj, proj, proj, proj, proj, *o3, *l3, wa, wc, wo, gate, bga, bgc,
      convw, convb, lnw, lnb, bd)


def _conv_bwd(dcv, proj, convw, dproj):
    s = dcv.shape[0]
    tr = 128
    nt = s // tr

    def body(dcv_ref, dcvn_ref, u_ref, uh_ref, cw_ref, dp_in, dp_out, dw_ref, extg, extd, sh):
        del dp_in
        i = pl.program_id(0)

        @pl.when(i == 0)
        def _():
            dw_ref[...] = jnp.zeros_like(dw_ref)

        _, _, glu = _glu(u_ref[...])
        _, _, glu_h = _glu(uh_ref[...])
        extg[0:CONV_HALO, :] = jnp.where(i > 0, glu_h, 0.0)
        extg[CONV_HALO:CONV_HALO + tr, :] = glu
        extd[0:tr, :] = dcv_ref[...]
        extd[tr:tr + CONV_HALO, :] = jnp.where(i < nt - 1, dcvn_ref[...], 0.0)
        for cb in range(CONV_W // LANE):
            cols = slice(cb * LANE, (cb + 1) * LANE)
            dglu = jnp.zeros((tr, LANE), F32)
            for j, rows in _tap_views(extd, sh, [CONV_K - 1 - j for j in range(CONV_K)], tr, cols):
                dglu = dglu + cw_ref[j:j + 1, cols] * rows
            dcv_c = dcv_ref[:, cols]
            for j, rows in _tap_views(extg, sh, [CONV_HALO - (CONV_K - 1) + j for j in range(CONV_K)], tr, cols):
                dw_ref[8 * j:8 * j + 8, cols] += jnp.sum((dcv_c * rows).reshape(tr // 8, 8, LANE), axis=0)
            a_h = u_ref[:, cols]
            sgb = _sigmoid(u_ref[:, CONV_W + cb * LANE:CONV_W + (cb + 1) * LANE])
            dp_out[:, cols] = (dglu * sgb).astype(BF16)
            dp_out[:, CONV_W + cb * LANE:CONV_W + (cb + 1) * LANE] = (dglu * a_h * (sgb * (1.0 - sgb))).astype(BF16)

    ucol = U0 // D_MODEL
    return pl.pallas_call(
        body, name="conv_bwd", grid=(nt,),
        in_specs=[pl.BlockSpec((tr, CONV_W), lambda i: (i, 0)),
                  pl.BlockSpec((CONV_HALO, CONV_W), lambda i: (jnp.minimum((i + 1) * (tr // CONV_HALO), s // CONV_HALO - 1), 0)),
                  pl.BlockSpec((tr, D_MODEL), lambda i: (i, ucol)),
                  pl.BlockSpec((CONV_HALO, D_MODEL), lambda i: (jnp.maximum(i * (tr // CONV_HALO) - 1, 0), ucol)),
                  pl.BlockSpec(convw.shape, lambda i: (0, 0)),
                  pl.BlockSpec(memory_space=pl.ANY)],
        out_specs=[pl.BlockSpec((tr, D_MODEL), lambda i: (i, ucol)), pl.BlockSpec((8 * CONV_HALO, CONV_W), lambda i: (0, 0))],
        out_shape=[SDS(dproj.shape, dproj.dtype), SDS((8 * CONV_HALO, CONV_W), F32)],
        input_output_aliases={5: 0},
        scratch_shapes=[pltpu.VMEM((CONV_HALO + tr, CONV_W), F32)] * 3,
        compiler_params=_params(),
    )(dcv, dcv, proj, proj, convw, dproj)


def _mm_acc(at, b, name, col_slots):
    m, s = at.shape
    n = b.shape[1]
    tk = 512
    nk = s // tk

    def body(a_ref, b_ref, o_ref, acc):
        k = pl.program_id(0)

        @pl.when(k == 0)
        def _():
            acc[...] = jnp.zeros_like(acc)

        acc[...] += _dot(a_ref[...], b_ref[...])

        @pl.when(k == nk - 1)
        def _():
            if col_slots:
                w = n // N_DEV
                for j in range(N_DEV):
                    o_ref[j] = acc[:, j * w:(j + 1) * w].astype(BF16)
            else:
                o_ref[...] = acc[...].astype(BF16)

    if col_slots:
        out_shape = SDS((N_DEV, m, n // N_DEV), BF16)
        out_spec = pl.BlockSpec((N_DEV, m, n // N_DEV), lambda k: (0, 0, 0))
    else:
        out_shape = SDS((m, n), BF16)
        out_spec = pl.BlockSpec((m, n), lambda k: (0, 0))
    return pl.pallas_call(
        body, name=name, grid=(nk,),
        in_specs=[pl.BlockSpec((m, tk), lambda k: (0, k)), pl.BlockSpec((tk, n), lambda k: (k, 0))],
        out_specs=out_spec, out_shape=out_shape, scratch_shapes=[pltpu.VMEM((m, n), F32)],
        compiler_params=_params(),
    )(at, b)


def _mm_dw(ht, dproj):
    s = ht.shape[1]
    tk = 512
    nk = s // tk

    def body(a_ref, b_ref, o_ref, acc):
        k = pl.program_id(1)

        @pl.when(k == 0)
        def _():
            acc[...] = jnp.zeros_like(acc)

        acc[...] += _dot(a_ref[...], b_ref[...])

        @pl.when(k == nk - 1)
        def _():
            o_ref[...] = acc[...].T.astype(BF16)

    return pl.pallas_call(
        body, name="mm_dw", grid=(IN_W // PAIR_W, nk),
        in_specs=[pl.BlockSpec((D_MODEL, tk), lambda p, k: (0, k)), pl.BlockSpec((tk, PAIR_W), lambda p, k: (k, p))],
        out_specs=pl.BlockSpec((PAIR_W, D_MODEL), lambda p, k: (p, 0)),
        out_shape=SDS((IN_W, D_MODEL), BF16), scratch_shapes=[pltpu.VMEM((D_MODEL, PAIR_W), F32)],
        compiler_params=_params(),
    )(ht, dproj)


def _mm_dh(dproj, wt, token):
    s = dproj.shape[0]
    tm = 1024

    def body(dp_ref, w_ref, tok_ref, o_ref):
        del tok_ref
        p = pl.program_id(1)
        part = _dot(dp_ref[...], w_ref[...])

        @pl.when(p == 0)
        def _():
            o_ref[...] = part

        @pl.when(p > 0)
        def _():
            o_ref[...] += part

    return pl.pallas_call(
        body, name="mm_dh", grid=(s // tm, IN_W // PAIR_W),
        in_specs=[pl.BlockSpec((tm, PAIR_W), lambda m, p: (m, p)),
                  pl.BlockSpec((PAIR_W, D_MODEL), lambda m, p: (p, 0)),
                  pl.BlockSpec(token.shape, lambda m, p: (0, 0))],
        out_specs=pl.BlockSpec((tm, D_MODEL), lambda m, p: (m, 0)),
        out_shape=SDS((s, D_MODEL), F32), compiler_params=_params(),
    )(dproj, wt, token)


def _norm_bwd(x, dh, dout, norm_w, scale):
    s = x.shape[0]
    tr = 512

    def body(x_ref, dh_ref, do_ref, nw_ref, sc_ref, gx_ref, dsh_ref, dsc_ref, dnw_ref):
        i = pl.program_id(0)

        @pl.when(i == 0)
        def _():
            for r in (dsh_ref, dsc_ref, dnw_ref):
                r[...] = jnp.zeros_like(r)

        def acc_rows(ref, v):
            ref[...] += jnp.broadcast_to(jnp.sum(v, axis=0, keepdims=True), ref.shape)

        xv = x_ref[...]
        dh_v = dh_ref[...]
        r = lax.rsqrt(jnp.mean(xv * xv, axis=-1, keepdims=True) + EPS)
        xn = xv * r
        one_sc = 1.0 + sc_ref[...]
        acc_rows(dsh_ref, dh_v)
        acc_rows(dsc_ref, dh_v * (xn * nw_ref[...]))
        acc_rows(dnw_ref, dh_v * xn * one_sc)
        dxn = dh_v * (nw_ref[...] * one_sc)
        gx_ref[...] = do_ref[...] + r * (dxn - xn * jnp.mean(dxn * xn, axis=-1, keepdims=True))

    blk = pl.BlockSpec((tr, D_MODEL), lambda i: (i, 0))
    vec = pl.BlockSpec((1, D_MODEL), lambda i: (0, 0))
    acc = pl.BlockSpec((8, D_MODEL), lambda i: (0, 0))
    return pl.pallas_call(
        body, name="norm_bwd", grid=(s // tr,), in_specs=[blk, blk, blk, vec, vec],
        out_specs=[blk, acc, acc, acc],
        out_shape=[SDS((s, D_MODEL), F32)] + [SDS((8, D_MODEL), F32)] * 3, compiler_params=_params(),
    )(x, dh, dout, norm_w, scale)


SMALL_ROWS = 8
QN_COL, KN_COL, CB_COL, LOSS_COL = 0, LANE, 2 * LANE, 2 * LANE + CONV_W


def _pack_partials(dsh, dsc, dgate, dnw, dbg, dqw3, dkw3, dcb, dlnw, dlnb, loss_p):
    n3 = len(dqw3)

    def body(*refs):
        dsh_r, dsc_r, dgate_r, dnw_r, dbg_r = refs[:5]
        dq_r, dk_r = refs[5:5 + n3], refs[5 + n3:5 + 2 * n3]
        dcb_r, dlnw_r, dlnb_r, loss_r, o_ref = refs[5 + 2 * n3:]

        def both_heads(rs):
            t = rs[0][0:1, :]
            for r in rs[1:]:
                t = t + r[0:1, :]
            return t + pltpu.roll(t, HEAD_DIM, axis=1)

        o_ref[0:1, :] = dsh_r[0:1, :]
        o_ref[1:2, :] = dsc_r[0:1, :]
        o_ref[2:3, :] = dgate_r[0:1, :]
        o_ref[3:4, :] = dnw_r[0:1, :]
        o_ref[4:5, :] = dbg_r[0:1, 0:D_MODEL]
        o_ref[5:6, :] = dbg_r[0:1, D_MODEL:]
        o_ref[6:7, QN_COL:QN_COL + LANE] = both_heads(dq_r)
        o_ref[6:7, KN_COL:KN_COL + LANE] = both_heads(dk_r)
        o_ref[6:7, CB_COL:CB_COL + CONV_W] = dcb_r[0:1, :]
        o_ref[6:7, LOSS_COL:LOSS_COL + LANE] = loss_r[0:1, :]
        o_ref[6:7, LOSS_COL + LANE:] = jnp.zeros((1, D_MODEL - LOSS_COL - LANE), F32)
        o_ref[7:8, 0:CONV_W] = dlnw_r[0:1, :]
        o_ref[7:8, CONV_W:] = dlnb_r[0:1, :]

    return pl.pallas_call(body, name="pack_partials", out_shape=SDS((SMALL_ROWS, D_MODEL), F32),
                          compiler_params=_params())(dsh, dsc, dgate, dnw, dbg, *dqw3, *dkw3, dcb, dlnw, dlnb, loss_p)


def _adamw_update(g, w, m, v):
    bc1 = 1.0 - ADAM_B1 ** ADAM_STEP
    bc2 = 1.0 - ADAM_B2 ** ADAM_STEP
    m_new = ADAM_B1 * m + (1.0 - ADAM_B1) * g
    v_new = ADAM_B2 * v + (1.0 - ADAM_B2) * (g * g)
    delta = -ADAM_LR * ((m_new / bc1) / (jnp.sqrt(v_new / bc2) + ADAM_EPS) + ADAM_WD * w)
    return delta, m_new, v_new


def _adamw_small(small_all, ws, ms, vs):
    n = len(ws)
    where = [(slice(0, 3), None), (slice(3, 4), None), (slice(4, 6), None), (6, QN_COL), (6, KN_COL), (6, CB_COL),
             (7, 0), (7, CONV_W)]

    def body(*refs):
        g_ref = refs[0]
        w_r, m_r, v_r = refs[1:1 + n], refs[1 + n:1 + 2 * n], refs[1 + 2 * n:1 + 3 * n]
        outs = refs[1 + 3 * n:]
        g_o, d_o, m_o, v_o, loss_o = outs[:n], outs[n:2 * n], outs[2 * n:3 * n], outs[3 * n:4 * n], outs[4 * n]
        gsum = g_ref[0]
        for dev in range(1, N_DEV):
            gsum = gsum + g_ref[dev]
        loss_o[...] = gsum[6:7, LOSS_COL:LOSS_COL + LANE]
        for i, (rows, col) in enumerate(where):
            width = w_r[i].shape[1]
            if col is None:
                g = jnp.concatenate([gsum[r:r + 1, :] for r in range(rows.start, rows.stop)], axis=1)
            else:
                g = gsum[rows:rows + 1, col:col + width]
            delta, m_new, v_new = _adamw_update(g, w_r[i][...], m_r[i][...], v_r[i][...])
            g_o[i][...] = g
            d_o[i][...] = delta
            m_o[i][...] = m_new
            v_o[i][...] = v_new

    shapes = [SDS(w.shape, F32) for w in ws]
    res = pl.pallas_call(body, name="adamw_small", out_shape=shapes * 4 + [SDS((1, LANE), F32)],
                         compiler_params=_params())(small_all, *ws, *ms, *vs)
    return [res[k * n:(k + 1) * n] for k in range(4)], res[4 * n]


def _row_tile(rows):
    if rows <= 128:
        return rows
    return 128 if rows % 128 == 0 else SHARD_W // 4


def _adamw(gsrc, w, m, v, name, stacked):
    rows, cols = w.shape
    tr = _row_tile(rows)
    n_src = len(gsrc) if stacked else 1

    def body(*refs):
        g_refs, (w_ref, m_ref, v_ref, go_ref, d_ref, mo_ref, vo_ref) = refs[:n_src], refs[n_src:]
        if stacked:
            g = None
            for g_ref, (_, slots) in zip(g_refs, gsrc):
                for j in range(slots):
                    t = g_ref[j].astype(F32)
                    g = t if g is None else g + t
        else:
            g = g_refs[0][...]
        delta, m_new, v_new = _adamw_update(g, w_ref[...], m_ref[...], v_ref[...])
        go_ref[...] = g
        d_ref[...] = delta
        mo_ref[...] = m_new
        vo_ref[...] = v_new

    blk = pl.BlockSpec((tr, cols), lambda i: (i, 0))
    if stacked:
        gspecs = [pl.BlockSpec((slots, tr, arr.shape[2]), lambda i: (0, i, 0)) for arr, slots in gsrc]
        gargs = [arr for arr, _ in gsrc]
    else:
        gspecs, gargs = [blk], [gsrc]
    in_specs = gspecs + [blk, blk, blk]
    args = gargs + [w, m, v]
    return pl.pallas_call(
        body, name=name, grid=(rows // tr,), in_specs=in_specs, out_specs=[blk] * 4,
        out_shape=[SDS((rows, cols), F32)] * 4, compiler_params=_params(),
    )(*args)


def kernel(x, c, w_ada, b_ada, norm_w, w_in, b_gate, q_norm_w, k_norm_w, w_attn_proj, conv_w, conv_b, conv_ln_w, conv_ln_b, w_conv_proj, w_out, loss_target, m_w_ada, m_b_ada, m_norm_w, m_w_in, m_b_gate, m_q_norm_w, m_k_norm_w, m_w_attn_proj, m_conv_w, m_conv_b, m_conv_ln_w, m_conv_ln_b, m_w_conv_proj, m_w_out, v_w_ada, v_b_ada, v_norm_w, v_w_in, v_b_gate, v_q_norm_w, v_k_norm_w, v_w_attn_proj, v_conv_w, v_conv_b, v_conv_ln_w, v_conv_ln_b, v_w_conv_proj, v_w_out):
    xi, yi, ci = lax.axis_index("x"), lax.axis_index("y"), lax.axis_index("c")
    me = 4 * xi + 2 * yi + ci
    x2, tgt2 = x[0], loss_target[0]
    w_in_t, m_w_in_t, v_w_in_t = (jnp.transpose(a[0]) for a in (w_in, m_w_in, v_w_in))
    s = x2.shape[0]

    cw_flat = jnp.pad(conv_w[0].reshape(1, -1), ((0, 0), (0, CONVW_FLAT - CONV_K * HEAD_DIM)))
    pre = jnp.concatenate([c, cw_flat], axis=1).reshape(8, -1)
    (pre_all,) = _all_gather([pre], "gather_c_convw", vmem=True)
    pre_all = pre_all.reshape(N_DEV, -1)
    c_all = pre_all[:, :D_MODEL]
    convw_full = pre_all[:, D_MODEL:D_MODEL + CONV_K * HEAD_DIM].reshape(N_DEV, CONV_K, HEAD_DIM)
    convw_full = jnp.transpose(convw_full, (1, 0, 2)).reshape(CONV_K, CONV_W)
    convw_pad = jnp.pad(convw_full, ((0, CONV_HALO - CONV_K), (0, 0)))

    ada_part = _ada_fwd(c_all, w_ada[0])
    (ada_all,) = _all_gather([ada_part], "gather_ada", vmem=True)
    ada = lax.dynamic_index_in_dim(ada_all, me, axis=1, keepdims=False).reshape(1, 3 * D_MODEL) + b_ada
    shift, scale, gate = ada[:, :D_MODEL], ada[:, D_MODEL:2 * D_MODEL], ada[:, 2 * D_MODEL:]

    wt_g, wa_g, wc_g, wo_g = _all_gather_chips(
        [_cast_bf16(w_in_t, "cast_win"), _cast_bf16(w_attn_proj[0], "cast_wa"), _cast_bf16(w_conv_proj[0], "cast_wc"),
         _cast_bf16(w_out[0], "cast_wo")], "gather_weights")
    wt = wt_g.reshape(IN_W, D_MODEL)
    wa = _cols_from_slots(wa_g, "cols_wa")
    wc = _cols_from_slots(wc_g, "cols_wc")
    wo = wo_g.reshape(D_MODEL, D_MODEL)

    h, ht = _norm_fwd(x2, norm_w, scale, shift)
    proj = _mm_in(h, wt)
    qw2 = jnp.tile(q_norm_w, (1, 2))
    kw2 = jnp.tile(k_norm_w, (1, 2))
    o_all, l_all, qn, kn, vn = _attn_fwd(proj, qw2, kw2)
    o3, l3 = [o_all] * N_GROUPS, [l_all] * N_GROUPS
    head_id = jnp.arange(ATTN_W) // HEAD_DIM
    bd = (head_id[:, None] == head_id[None, :]).astype(BF16)
    (dout, da, lse_delta, dcv, mt, yat, yct, dmo, dya, dyc, dproj,
     dgate, dbg, dlnw, dlnb, dcb, loss_p) = _tail(
        x2, tgt2, proj, o3, l3, wa, wc, wo, gate, b_gate[:, :D_MODEL], b_gate[:, D_MODEL:], convw_pad,
        conv_b, conv_ln_w, conv_ln_b, bd)

    dproj, dconvw8 = _conv_bwd(dcv, proj, convw_pad, dproj)
    dconvw = jnp.sum(dconvw8.reshape(CONV_HALO, 8, CONV_W), axis=1)
    dproj, dqw_all, dkw_all = _attn_bwd(proj, qn, kn, vn, da, lse_delta, qw2, kw2, dproj)
    dqw_g3, dkw_g3 = [dqw_all], [dkw_all]
    dw_in_p = _mm_dw(ht, dproj).reshape(N_DEV, SHARD_W, D_MODEL)
    dwo_p = _mm_acc(mt, dmo, "mm_dwo", col_slots=False).reshape(N_DEV, D_MODEL // N_DEV, D_MODEL)
    dwa_p = _mm_acc(yat, dya, "mm_dwa", col_slots=True)
    dwc_p = _mm_acc(yct, dyc, "mm_dwc", col_slots=True)

    partials = [dw_in_p, dwa_p, dwc_p, dwo_p]
    me_arr = jnp.reshape(me, (1,)).astype(jnp.int32)
    from_sib = _exchange_sibling(partials, "exchange_sibling")
    presums = [_presum(p, f, me_arr, f"presum{i}") for i, (p, f) in enumerate(zip(partials, from_sib))]
    s_sems, r_sems, pre_thru, land_thru, token = _exchange_chips_start(presums, "exchange_chips_start")
    dh = _mm_dh(dproj, wt, token)
    gx, dsh, dsc, dnw = _norm_bwd(x2, dh, dout, norm_w, scale)
    small_p = _pack_partials(dsh, dsc, dgate, dnw, dbg, dqw_g3, dkw_g3, dcb, dlnw, dlnb, loss_p)
    small_all, dconvw_all = _all_gather([small_p, dconvw], "gather_small", vmem=True)

    small_w = (b_ada, norm_w, b_gate, q_norm_w, k_norm_w, conv_b, conv_ln_w, conv_ln_b)
    small_m = (m_b_ada, m_norm_w, m_b_gate, m_q_norm_w, m_k_norm_w, m_conv_b, m_conv_ln_w, m_conv_ln_b)
    small_v = (v_b_ada, v_norm_w, v_b_gate, v_q_norm_w, v_k_norm_w, v_conv_b, v_conv_ln_w, v_conv_ln_b)
    r_small, loss_row = _adamw_small(small_all, small_w, small_m, small_v)
    dcw_mine = lax.dynamic_slice_in_dim(dconvw_all[:, :CONV_K, :], me * HEAD_DIM, HEAD_DIM, axis=2)
    r_convw = _adamw([(dcw_mine, N_DEV)], conv_w[0], m_conv_w[0], v_conv_w[0], "adamw_conv_w", stacked=True)

    d_ada_all = small_all[:, 0:3, :].reshape(N_DEV, 3 * D_MODEL)
    d_ada_cols = lax.dynamic_slice_in_dim(d_ada_all, me * (3 * D_MODEL // N_DEV), 3 * D_MODEL // N_DEV, axis=1)
    g_wada = _ada_bwd(c_all, d_ada_cols)
    r_ada = _adamw(g_wada, w_ada[0], m_w_ada[0], v_w_ada[0], "adamw_w_ada", stacked=False)
    pres, lands = _exchange_chips_wait(s_sems, r_sems, pre_thru, land_thru, r_ada[1], "exchange_chips_wait")
    terms = [[(p, 1), (l, len(CHIP_K))] for p, l in zip(pres, lands)]
    r_win = [jnp.transpose(r) for r in _adamw(terms[0], w_in_t, m_w_in_t, v_w_in_t, "adamw_w_in", stacked=True)]
    r_wap = _adamw(terms[1], w_attn_proj[0], m_w_attn_proj[0], v_w_attn_proj[0], "adamw_w_attn_proj", stacked=True)
    r_wcp = _adamw(terms[2], w_conv_proj[0], m_w_conv_proj[0], v_w_conv_proj[0], "adamw_w_conv_proj", stacked=True)
    r_wout = _adamw(terms[3], w_out[0], m_w_out[0], v_w_out[0], "adamw_w_out", stacked=True)

    outs = [loss_row[0, 0], gx[None]]
    for k in range(4):
        b_ada_k, norm_w_k, b_gate_k, qn_k, kn_k, conv_b_k, ln_w_k, ln_b_k = r_small[k]
        outs += [r_ada[k][None], b_ada_k, norm_w_k, r_win[k][None], b_gate_k, qn_k, kn_k, r_wap[k][None],
                 r_convw[k][None], conv_b_k, ln_w_k, ln_b_k, r_wcp[k][None], r_wout[k][None]]
    return tuple(outs)
```

```python
import functools

import jax
import jax.numpy as jnp
from jax import lax
from jax.experimental import pallas as pl
from jax.experimental.pallas import tpu as pltpu

F32 = jnp.float32
BF16 = jnp.bfloat16
SDS = jax.ShapeDtypeStruct
MESH = pl.DeviceIdType.MESH

N_DEV = 8
D_MODEL = 1024
HEAD_DIM = 64
N_GROUPS = 3
DILATIONS = (1, 4, 16)
BAND = 128
BWD_UNROLL = 8
ATTN_W = 512
CONV_W = 512
CONV_K = 31
CONV_HALO = 32
IN_W = 8704
SHARD_W = IN_W // N_DEV
PAIR_W = 2 * SHARD_W
Q0, K0, V0, ZA0, U0, ZC0, G0 = 0, 1536, 3072, 4608, 5120, 6144, 6656
EPS = 1e-6
LANE = 128
VMEM_LIMIT = 56 * 1024 * 1024

ADAM_LR, ADAM_B1, ADAM_B2, ADAM_EPS, ADAM_WD, ADAM_STEP = 0.001, 0.9, 0.999, 1e-08, 0.01, 10

CONVW_FLAT = 2048


def _params(**kw):
    return pltpu.CompilerParams(vmem_limit_bytes=VMEM_LIMIT, **kw)


def _sigmoid(z):
    return 0.5 * jnp.tanh(0.5 * z) + 0.5


def _dot(a, b):
    return jnp.dot(a, b, preferred_element_type=F32)


def _dot_nt(a, b):
    return lax.dot_general(a, b, (((1,), (1,)), ((), ())), preferred_element_type=F32)


def _dot_tn(a, b):
    return lax.dot_general(a, b, (((0,), (0,)), ((), ())), preferred_element_type=F32)


def _peer(x, y, c, k):
    px = 1 - x if (k >> 2) & 1 else x
    py = 1 - y if (k >> 1) & 1 else y
    pc = 1 - c if k & 1 else c
    return (px, py, pc), 4 * px + 2 * py + pc


def _all_gather(arrays, name, vmem):
    n = len(arrays)
    space = pltpu.VMEM if vmem else pl.ANY

    def body(*refs):
        ins, outs = refs[:n], refs[n:2 * n]
        send_sems, recv_sems, local_sems = refs[2 * n:]
        x, y, c = lax.axis_index("x"), lax.axis_index("y"), lax.axis_index("c")
        me = 4 * x + 2 * y + c
        locals_ = [pltpu.make_async_copy(ins[a], outs[a].at[me], local_sems.at[a]) for a in range(n)]
        for cp in locals_:
            cp.start()
        sends = []
        for k in range(1, N_DEV):
            peer, _ = _peer(x, y, c, k)
            for a in range(n):
                cp = pltpu.make_async_remote_copy(
                    src_ref=ins[a], dst_ref=outs[a].at[me], send_sem=send_sems.at[a, k - 1],
                    recv_sem=recv_sems.at[a, k - 1], device_id=peer, device_id_type=MESH)
                cp.start()
                sends.append(cp)
        for k in range(1, N_DEV):
            peer, pidx = _peer(x, y, c, k)
            for a in range(n):
                pltpu.make_async_remote_copy(
                    src_ref=ins[a], dst_ref=outs[a].at[pidx], send_sem=send_sems.at[a, k - 1],
                    recv_sem=recv_sems.at[a, k - 1], device_id=peer, device_id_type=MESH).wait_recv()
        for cp in sends:
            cp.wait_send()
        for cp in locals_:
            cp.wait()

    return pl.pallas_call(
        body, name=name,
        out_shape=[SDS((N_DEV,) + a.shape, a.dtype) for a in arrays],
        in_specs=[pl.BlockSpec(memory_space=space)] * n,
        out_specs=[pl.BlockSpec(memory_space=space)] * n,
        scratch_shapes=[pltpu.SemaphoreType.DMA((n, N_DEV - 1)), pltpu.SemaphoreType.DMA((n, N_DEV - 1)),
                        pltpu.SemaphoreType.DMA((n,))],
        compiler_params=_params(),
    )(*arrays)


CHIP_K = (2, 4, 6)


def _all_gather_chips(arrays, name):
    n = len(arrays)
    k_y, k_x, k_d = CHIP_K

    def body(*refs):
        ins, outs = refs[:n], refs[n:2 * n]
        send_sems, recv_sems, local_sems = refs[2 * n:]
        x, y, c = lax.axis_index("x"), lax.axis_index("y"), lax.axis_index("c")
        me = 4 * x + 2 * y + c
        sib, sib_idx = _peer(x, y, c, 1)
        nbr_y, idx_y = _peer(x, y, c, k_y)
        nbr_x, idx_x = _peer(x, y, c, k_x)
        _, idx_d = _peer(x, y, c, k_d)

        def copy(a, slot, block, to, src=None):
            return pltpu.make_async_remote_copy(
                src_ref=outs[a].at[block] if src is None else src, dst_ref=outs[a].at[block],
                send_sem=send_sems.at[a, slot], recv_sem=recv_sems.at[a, slot], device_id=to, device_id_type=MESH)

        locals_ = [pltpu.make_async_copy(ins[a], outs[a].at[me], local_sems.at[a]) for a in range(n)]
        for cp in locals_:
            cp.start()
        for a in range(n):
            copy(a, 0, me, sib, src=ins[a]).start()
            copy(a, 1, me, nbr_y, src=ins[a]).start()
            copy(a, 2, me, nbr_x, src=ins[a]).start()

        def arrived(slot, block, frm, send_on_to=None):
            for a in range(n):
                copy(a, slot, block, frm).wait_recv()
                if send_on_to is not None:
                    copy(a, 3, block, send_on_to).start()
                copy(a, 3 + slot, block, sib).start()

        @pl.when(c == 0)
        def _():
            arrived(1, idx_y, nbr_y, send_on_to=nbr_x)
            arrived(2, idx_x, nbr_x)

        @pl.when(c == 1)
        def _():
            arrived(2, idx_x, nbr_x, send_on_to=nbr_y)
            arrived(1, idx_y, nbr_y)

        arrived(3, idx_d, nbr_x)
        for a in range(n):
            copy(a, 0, sib_idx, sib).wait_recv()
        for slot, k in ((4, k_y), (5, k_x), (6, k_d)):
            _, pidx = _peer(x, y, 1 - c, k)
            for a in range(n):
                copy(a, slot, pidx, sib).wait_recv()
        for slot in range(N_DEV - 1):
            for a in range(n):
                copy(a, slot, me, sib).wait_send()
        for cp in locals_:
            cp.wait()

    return pl.pallas_call(
        body, name=name,
        out_shape=[SDS((N_DEV,) + a.shape, a.dtype) for a in arrays],
        in_specs=[pl.BlockSpec(memory_space=pl.ANY)] * n,
        out_specs=[pl.BlockSpec(memory_space=pl.ANY)] * n,
        scratch_shapes=[pltpu.SemaphoreType.DMA((n, N_DEV - 1)), pltpu.SemaphoreType.DMA((n, N_DEV - 1)),
                        pltpu.SemaphoreType.DMA((n,))],
        compiler_params=_params(),
    )(*arrays)


def _exchange_sibling(arrays, name):
    n = len(arrays)
    ks = (0,) + CHIP_K

    def body(*refs):
        ins, outs = refs[:n], refs[n:2 * n]
        send_sems, recv_sems = refs[2 * n:]
        x, y, c = lax.axis_index("x"), lax.axis_index("y"), lax.axis_index("c")
        sib, sib_idx = _peer(x, y, c, 1)
        sends = []
        for i, k in enumerate(ks):
            _, tgt = _peer(x, y, 1 - c, k) if k else (None, sib_idx)
            for a in range(n):
                cp = pltpu.make_async_remote_copy(
                    src_ref=ins[a].at[tgt], dst_ref=outs[a].at[i], send_sem=send_sems.at[a, i],
                    recv_sem=recv_sems.at[a, i], device_id=sib, device_id_type=MESH)
                cp.start()
                sends.append(cp)
        for cp in sends:
            cp.wait_recv()
        for cp in sends:
            cp.wait_send()

    return pl.pallas_call(
        body, name=name,
        out_shape=[SDS((len(ks),) + a.shape[1:], a.dtype) for a in arrays],
        in_specs=[pl.BlockSpec(memory_space=pl.ANY)] * n,
        out_specs=[pl.BlockSpec(memory_space=pl.ANY)] * n,
        scratch_shapes=[pltpu.SemaphoreType.DMA((n, len(ks))), pltpu.SemaphoreType.DMA((n, len(ks)))],
        compiler_params=_params(),
    )(*arrays)


def _presum(mine, from_sib, me_arr, name):
    _, rows, cols = mine.shape
    tr = _row_tile(rows)
    ns = 1 + len(CHIP_K)

    def body(me_ref, a_ref, b_ref, o_ref):
        del me_ref
        o_ref[...] = (a_ref[...].astype(F32) + b_ref[...].astype(F32)).astype(o_ref.dtype)

    grid_spec = pltpu.PrefetchScalarGridSpec(
        num_scalar_prefetch=1, grid=(ns, rows // tr),
        in_specs=[pl.BlockSpec((1, tr, cols), lambda j, i, me: (jnp.bitwise_xor(me[0], 2 * j), i, 0)),
                  pl.BlockSpec((1, tr, cols), lambda j, i, me: (j, i, 0))],
        out_specs=pl.BlockSpec((1, tr, cols), lambda j, i, me: (j, i, 0)))
    return pl.pallas_call(body, name=name, grid_spec=grid_spec, out_shape=SDS((ns, rows, cols), mine.dtype),
                          compiler_params=_params())(me_arr, mine, from_sib)


HBM_SPEC = pl.BlockSpec(memory_space=pltpu.HBM)
SEM_SPEC = pl.BlockSpec(memory_space=pltpu.SEMAPHORE)
SIDE_EFFECT = pltpu.SideEffectType.DATAFLOW_SIDE_EFFECTING


def _chips_copies(pre_refs, land_refs, send_sems, recv_sems):
    x, y, c = lax.axis_index("x"), lax.axis_index("y"), lax.axis_index("c")
    copies = []
    for j, k in enumerate(CHIP_K):
        peer, _ = _peer(x, y, c, k)
        for a in range(len(pre_refs)):
            copies.append(pltpu.make_async_remote_copy(
                src_ref=pre_refs[a].at[1 + j], dst_ref=land_refs[a].at[j], send_sem=send_sems.at[a * len(CHIP_K) + j],
                recv_sem=recv_sems.at[a * len(CHIP_K) + j], device_id=peer, device_id_type=MESH))
    return copies


def _exchange_chips_start(presums, name):
    n = len(presums)

    def body(*refs):
        pre, land = refs[:n], refs[n:2 * n]
        send_sems, recv_sems = refs[2 * n], refs[2 * n + 1]
        token = refs[-1]
        for cp in _chips_copies(pre, land, send_sems, recv_sems):
            cp.start()
        token[...] = jnp.zeros_like(token)

    nk = len(CHIP_K)
    hbm = [pltpu.HBM(p.shape, p.dtype) for p in presums]
    hbm_land = [pltpu.HBM((nk,) + p.shape[1:], p.dtype) for p in presums]
    res = pl.pallas_call(
        body, name=name,
        out_shape=(pltpu.SemaphoreType.DMA((n * nk,)), pltpu.SemaphoreType.DMA((n * nk,)), *hbm, *hbm_land, SDS((8, LANE), F32)),
        in_specs=[HBM_SPEC] * (2 * n),
        out_specs=(SEM_SPEC, SEM_SPEC, *([HBM_SPEC] * (2 * n)), pl.BlockSpec(memory_space=pltpu.VMEM)),
        input_output_aliases={i: 2 + i for i in range(2 * n)},
        compiler_params=pltpu.CompilerParams(has_side_effects=SIDE_EFFECT),
    )(*[pltpu.with_memory_space_constraint(p, pltpu.HBM) for p in presums],
      *[pltpu.with_memory_space_constraint(lax.empty((nk,) + p.shape[1:], p.dtype), pltpu.HBM) for p in presums])
    return res[0], res[1], res[2:2 + n], res[2 + n:2 + 2 * n], res[-1]


def _exchange_chips_wait(send_sems, recv_sems, pre_thru, land_thru, after, name):
    n = len(pre_thru)

    def body(*refs):
        pre, land = refs[:n], refs[n:2 * n]
        s_sems, r_sems = refs[2 * n], refs[2 * n + 1]
        for cp in _chips_copies(pre, land, s_sems, r_sems):
            cp.wait_send()
            cp.wait_recv()

    hbm = [pltpu.HBM(p.shape, p.dtype) for p in (*pre_thru, *land_thru)]
    res = pl.pallas_call(
        body, name=name, out_shape=tuple(hbm),
        in_specs=[HBM_SPEC] * (2 * n) + [SEM_SPEC, SEM_SPEC, pl.BlockSpec(memory_space=pl.ANY)],
        out_specs=tuple([HBM_SPEC] * (2 * n)),
        input_output_aliases={i: i for i in range(2 * n)},
        compiler_params=pltpu.CompilerParams(has_side_effects=SIDE_EFFECT),
    )(*pre_thru, *land_thru, send_sems, recv_sems, after)
    return res[:n], res[n:]


def _exchange_chips(presums, name):
    n = len(presums)
    nk = len(CHIP_K)

    def body(*refs):
        pre, land = refs[:n], refs[n:2 * n]
        send_sems, recv_sems = refs[2 * n:]
        copies = _chips_copies(pre, land, send_sems, recv_sems)
        for cp in copies:
            cp.start()
        for cp in copies:
            cp.wait_recv()
        for cp in copies:
            cp.wait_send()

    return pl.pallas_call(
        body, name=name,
        out_shape=[SDS((nk,) + p.shape[1:], p.dtype) for p in presums],
        in_specs=[pl.BlockSpec(memory_space=pl.ANY)] * n,
        out_specs=[pl.BlockSpec(memory_space=pl.ANY)] * n,
        scratch_shapes=[pltpu.SemaphoreType.DMA((n * nk,)), pltpu.SemaphoreType.DMA((n * nk,))],
        compiler_params=_params(),
    )(*presums)


def _cast_bf16(w, name):
    def body(w_ref, o_ref):
        o_ref[...] = w_ref[...].astype(BF16)

    return pl.pallas_call(body, name=name, out_shape=SDS(w.shape, BF16), compiler_params=_params())(w)


def _cols_from_slots(wg, name):
    _, rows, cols = wg.shape

    def body(w_ref, o_ref):
        for j in range(N_DEV):
            o_ref[:, j * cols:(j + 1) * cols] = w_ref[j]

    return pl.pallas_call(body, name=name, out_shape=SDS((rows, N_DEV * cols), wg.dtype), compiler_params=_params())(wg)


def _ada_fwd(c_all, w_ada):
    def body(c_ref, w_ref, o_ref):
        cv = c_ref[...]
        sc = (cv * _sigmoid(cv)).astype(BF16)
        o_ref[...] = _dot(sc, w_ref[...].astype(BF16))

    return pl.pallas_call(body, name="ada_fwd", out_shape=SDS((N_DEV, w_ada.shape[1]), F32),
                          compiler_params=_params())(c_all, w_ada)


def _ada_bwd(c_all, d_ada_cols):
    def body(c_ref, d_ref, o_ref):
        cv = c_ref[...]
        sc = (cv * _sigmoid(cv)).astype(BF16)
        o_ref[...] = _dot_tn(sc, d_ref[...].astype(BF16))

    return pl.pallas_call(body, name="ada_bwd", out_shape=SDS((D_MODEL, d_ada_cols.shape[1]), F32),
                          compiler_params=_params())(c_all, d_ada_cols)


def _norm_fwd(x, norm_w, scale, shift):
    s = x.shape[0]
    tr = 512

    def body(x_ref, nw_ref, sc_ref, sh_ref, h_ref, ht_ref):
        xv = x_ref[...]
        r = lax.rsqrt(jnp.mean(xv * xv, axis=-1, keepdims=True) + EPS)
        h = (xv * r * nw_ref[...]) * (1.0 + sc_ref[...]) + sh_ref[...]
        h_ref[...] = h.astype(BF16)
        ht_ref[...] = h.T.astype(BF16)

    vec = pl.BlockSpec((1, D_MODEL), lambda i: (0, 0))
    return pl.pallas_call(
        body, name="norm_fwd", grid=(s // tr,),
        in_specs=[pl.BlockSpec((tr, D_MODEL), lambda i: (i, 0)), vec, vec, vec],
        out_specs=[pl.BlockSpec((tr, D_MODEL), lambda i: (i, 0)), pl.BlockSpec((D_MODEL, tr), lambda i: (0, i))],
        out_shape=[SDS((s, D_MODEL), BF16), SDS((D_MODEL, s), BF16)], compiler_params=_params(),
    )(x, norm_w, scale, shift)


def _mm_in(h, wt):
    s = h.shape[0]
    tm = 512

    def body(h_ref, w_ref, o_ref):
        o_ref[...] = _dot_nt(h_ref[...], w_ref[...])

    return pl.pallas_call(
        body, name="mm_in", grid=(IN_W // PAIR_W, s // tm),
        in_specs=[pl.BlockSpec((tm, D_MODEL), lambda p, m: (m, 0)),
                  pl.BlockSpec((PAIR_W, D_MODEL), lambda p, m: (p, 0))],
        out_specs=pl.BlockSpec((tm, PAIR_W), lambda p, m: (m, p)),
        out_shape=SDS((s, IN_W), F32), compiler_params=_params(),
    )(h, wt)


def _head_ones():
    a = lax.broadcasted_iota(jnp.int32, (LANE, LANE), 0) // HEAD_DIM
    b = lax.broadcasted_iota(jnp.int32, (LANE, LANE), 1) // HEAD_DIM
    return (a == b).astype(BF16)


def _head_sums(t, ones):
    return _dot(t.astype(BF16), ones)


def _band_bias(bias, transposed=False):
    qi = lax.broadcasted_iota(jnp.int32, (2 * BAND, 2 * BAND), 1 if transposed else 0) % BAND
    kj = lax.broadcasted_iota(jnp.int32, (2 * BAND, 2 * BAND), 0 if transposed else 1)
    dist = qi + BAND - kj
    valid = (dist >= 0) & (dist <= BAND)
    bias[1] = jnp.where(valid, 0.0, -1e30)
    bias[0] = jnp.where(valid & (kj >= BAND), 0.0, -1e30)


def _token_rows(j, d, chunk, per_r):
    return pl.ds(j // per_r + (j % per_r) * (chunk * d), chunk, stride=d)


def _deinterleave(src_ref, dst_ref, w_ref, ones, d, sub_len, chunk, scale, dst_off):
    per_r = sub_len // chunk

    def step(j, _):
        t = src_ref[_token_rows(j, d, chunk, per_r), :]
        if w_ref is not None:
            ms = _head_sums(t * t, ones) * (1.0 / HEAD_DIM)
            t = t * lax.rsqrt(ms + EPS) * (w_ref[...] * scale)
        dst_ref[pl.ds(pl.multiple_of(dst_off + j * chunk, BAND), chunk), :] = t.astype(dst_ref.dtype)
        return 0
    lax.fori_loop(0, d * per_r, step, 0, unroll=4)


N_PAIRS = ATTN_W // LANE


def _attn_fwd(proj, qw2, kw2):
    s = proj.shape[0]

    def group_body(g, step, q_ref, k_ref, v_ref, qw_ref, kw_ref, o_ref, l_ref, qn_ref, kn_ref, vn_ref,
                   qd, kd, vd, od, ld, bias):
        d = DILATIONS[g]
        sub_len = s // d
        nb = sub_len // BAND
        chunk = min(sub_len, 256)
        lo = lax.broadcasted_iota(jnp.int32, (1, LANE), 1) < HEAD_DIM
        ones = _head_ones()

        @pl.when(step == 0)
        def _():
            _band_bias(bias)

        kd[0:BAND, :] = jnp.zeros((BAND, LANE), BF16)
        vd[0:BAND, :] = jnp.zeros((BAND, LANE), BF16)
        _deinterleave(q_ref, qd, qw_ref, ones, d, sub_len, chunk, HEAD_DIM ** -0.5, 0)
        _deinterleave(k_ref, kd, kw_ref, ones, d, sub_len, chunk, 1.0, BAND)
        _deinterleave(v_ref, vd, None, ones, d, sub_len, chunk, 1.0, BAND)
        qn_ref[...] = qd[...]
        kn_ref[...] = kd[BAND:BAND + s, :]
        vn_ref[...] = vd[BAND:BAND + s, :]

        def block(t, _):
            base = pl.multiple_of(t * BAND, BAND)
            q = qd[pl.ds(base, BAND), :]
            k2 = kd[pl.ds(base, 2 * BAND), :]
            v2 = vd[pl.ds(base, 2 * BAND), :]
            zero = jnp.zeros_like(q)
            qs = jnp.concatenate([jnp.where(lo, q, zero), jnp.where(lo, zero, q)], axis=0)
            sc = _dot_nt(qs, k2) + bias[jnp.minimum(t % nb, 1)]
            m = jnp.max(sc, axis=-1, keepdims=True)
            p = jnp.exp(sc - m)
            den = jnp.sum(p, axis=-1, keepdims=True)
            u = _dot(p.astype(BF16), v2) * (1.0 / den)
            lse = m + jnp.log(den)
            od[pl.ds(base, BAND), :] = jnp.where(lo, u[:BAND], u[BAND:])
            ld[pl.ds(base, BAND), :] = jnp.where(lo, lse[:BAND], lse[BAND:])
            return 0
        lax.fori_loop(0, s // BAND, block, 0, unroll=16)

        per_r = sub_len // chunk

        def back(j, _):
            src = pl.ds(pl.multiple_of(j * chunk, chunk), chunk)
            dst = _token_rows(j, d, chunk, per_r)
            o_ref[dst, :] = od[src, :]
            l_ref[dst, :] = ld[src, :]
            return 0
        lax.fori_loop(0, d * per_r, back, 0, unroll=2)

    def body(*refs):
        step = pl.program_id(0)
        for g in range(N_GROUPS):
            pl.when(step // N_PAIRS == g)(functools.partial(group_body, g, step, *refs))

    col = lambda off: pl.BlockSpec((s, LANE), lambda i, off=off: (0, off // LANE + i))
    vec = pl.BlockSpec((1, LANE), lambda i: (0, 0))
    out = pl.BlockSpec((s, LANE), lambda i: (0, i))
    width = N_GROUPS * ATTN_W
    return pl.pallas_call(
        body, name="attn_fwd", grid=(N_GROUPS * N_PAIRS,),
        in_specs=[col(Q0), col(K0), col(V0), vec, vec], out_specs=[out] * 5,
        out_shape=[SDS((s, width), F32)] * 2 + [SDS((s, width), BF16)] * 3,
        scratch_shapes=[pltpu.VMEM((s, LANE), BF16), pltpu.VMEM((s + BAND, LANE), BF16), pltpu.VMEM((s + BAND, LANE), BF16),
                        pltpu.VMEM((s, LANE), F32), pltpu.VMEM((s, LANE), F32),
                        pltpu.VMEM((2, 2 * BAND, 2 * BAND), F32)],
        compiler_params=_params(),
    )(proj, proj, proj, qw2, kw2)


def _attn_bwd(proj, qn, kn, vn, da, lse_delta, qw2, kw2, dproj):
    s = proj.shape[0]
    n_steps = N_GROUPS * N_PAIRS

    def group_body(g, hp, q_ref, k_ref, qn_ref, kn_ref, vn_ref, da_ref, ld_ref, qw_ref, kw_ref, dp_in, dp_out,
                   dqw_ref, dkw_ref, kd, vd, kdt, dad, lst, dlt, dqt, dqd, dkd, dvd, st, stb, bias_t, wacc, sem):
        del dp_in
        d = DILATIONS[g]
        sub_len = s // d
        nb = sub_len // BAND
        chunk = min(sub_len, 256)
        lo = lax.broadcasted_iota(jnp.int32, (1, LANE), 1) < HEAD_DIM
        row_lo = lax.broadcasted_iota(jnp.int32, (LANE, 1), 0) < HEAD_DIM
        ones = _head_ones()
        per_r = sub_len // chunk
        cblk = chunk // BAND

        @pl.when(hp == 0)
        def _():
            _band_bias(bias_t, transposed=True)

        kd[0:BAND, :] = jnp.zeros((BAND, LANE), BF16)
        vd[0:BAND, :] = jnp.zeros((BAND, LANE), BF16)
        kdt[0] = jnp.zeros((LANE, BAND), BF16)
        kd[BAND:BAND + s, :] = kn_ref[...]
        vd[BAND:BAND + s, :] = vn_ref[...]

        def k_step(t, _):
            kdt[1 + t] = kn_ref[pl.ds(pl.multiple_of(t * BAND, BAND), BAND), :].astype(F32).T.astype(BF16)
            return 0
        lax.fori_loop(0, s // BAND, k_step, 0, unroll=4)
        _deinterleave(da_ref, dad, None, ones, d, sub_len, chunk, 1.0, 0)

        def rows_step(j, _):
            tok = _token_rows(j, d, chunk, per_r)
            tt = ld_ref[tok, :].T
            for u in range(cblk):
                cols = slice(u * BAND, (u + 1) * BAND)
                lst[j * cblk + u, 0:1, :] = tt[0:1, cols]
                lst[j * cblk + u, 1:2, :] = tt[HEAD_DIM:HEAD_DIM + 1, cols]
                dlt[j * cblk + u, 0:1, :] = tt[HEAD_DIM // 2:HEAD_DIM // 2 + 1, cols]
                dlt[j * cblk + u, 1:2, :] = tt[HEAD_DIM + HEAD_DIM // 2:HEAD_DIM + HEAD_DIM // 2 + 1, cols]
            return 0
        lax.fori_loop(0, d * per_r, rows_step, 0, unroll=4)

        def block(t, carry):
            ck, cv = carry
            base = pl.multiple_of(t * BAND, BAND)
            q = qn_ref[pl.ds(base, BAND), :]
            k2 = kd[pl.ds(base, 2 * BAND), :]
            v2 = vd[pl.ds(base, 2 * BAND), :]
            k2t = jnp.concatenate([kdt[t], kdt[t + 1]], axis=1)
            dav = dad[pl.ds(base, BAND), :]
            zero = jnp.zeros_like(q)
            qs = jnp.concatenate([jnp.where(lo, q, zero), jnp.where(lo, zero, q)], axis=0)
            das = jnp.concatenate([jnp.where(lo, dav, zero), jnp.where(lo, zero, dav)], axis=0)
            ls_row = jnp.concatenate([lst[t, 0:1, :], lst[t, 1:2, :]], axis=1)
            dl_row = jnp.concatenate([dlt[t, 0:1, :], dlt[t, 1:2, :]], axis=1)
            sc_t = _dot_nt(k2, qs) + bias_t[jnp.minimum(t % nb, 1)]
            p_t = jnp.exp(sc_t - ls_row)
            dp_t = _dot_nt(v2, das)
            ds_t = (p_t * (dp_t - dl_row)).astype(BF16)
            dv2 = _dot(p_t.astype(BF16), das)
            dk2 = _dot(ds_t, qs)
            dvd[pl.ds(base, BAND), :] = cv + dv2[:BAND]
            dkd[pl.ds(base, BAND), :] = ck + dk2[:BAND]
            dq_t = _dot(k2t, ds_t)
            dqt[t] = jnp.where(row_lo, dq_t[:, :BAND], dq_t[:, BAND:])
            return dk2[BAND:], dv2[BAND:]

        def blocks(i, carry):
            for u in range(BWD_UNROLL):
                carry = block(i * BWD_UNROLL + u, carry)
            return carry
        zeros = jnp.zeros((BAND, LANE), F32)
        ck, cv = lax.fori_loop(0, s // (BAND * BWD_UNROLL), blocks, (zeros, zeros))
        dkd[s:s + BAND, :] = ck
        dvd[s:s + BAND, :] = cv

        def dq_rows(t, _):
            dqd[pl.ds(pl.multiple_of(t * BAND, BAND), BAND), :] = dqt[t].T
            return 0
        lax.fori_loop(0, s // BAND, dq_rows, 0, unroll=4)

        def col_copy(slot, col0):
            return pltpu.make_async_copy(
                stb.at[slot], dp_out.at[:, pl.ds(pl.multiple_of(col0 + LANE * hp, LANE), LANE)], sem.at[slot])

        def store_cols(slot, col0):
            @pl.when(hp > 0)
            def _():
                col_copy(slot, col0).wait()
            stb[slot] = st[...].astype(BF16)
            col_copy(slot, col0).start()

        def norm_back(src_ref, dy_ref, dy_off, w_ref, scale, dw_ref, slot, col0):
            wacc[...] = jnp.zeros_like(wacc)

            def step(j, _):
                tok = _token_rows(j, d, chunk, per_r)
                t = src_ref[tok, :]
                dy = dy_ref[pl.ds(pl.multiple_of(dy_off + j * chunk, BAND), chunk), :]
                rr = lax.rsqrt(_head_sums(t * t, ones) * (1.0 / HEAD_DIM) + EPS)
                nrm = t * rr
                wacc[...] += jnp.sum((dy * nrm).reshape(chunk // 8, 8, LANE), axis=0)
                dn = dy * (w_ref[...] * scale)
                st[tok, :] = rr * (dn - nrm * (_head_sums(dn * nrm, ones) * (1.0 / HEAD_DIM)))
                return 0
            lax.fori_loop(0, d * per_r, step, 0, unroll=4)
            dw_ref[...] += jnp.broadcast_to(jnp.sum(wacc[...], axis=0, keepdims=True) * scale, dw_ref.shape)
            store_cols(slot, col0)

        @pl.when(hp == 0)
        def _():
            dqw_ref[...] = jnp.zeros_like(dqw_ref)
            dkw_ref[...] = jnp.zeros_like(dkw_ref)

        norm_back(q_ref, dqd, 0, qw_ref, HEAD_DIM ** -0.5, dqw_ref, 0, Q0)
        norm_back(k_ref, dkd, BAND, kw_ref, 1.0, dkw_ref, 1, K0)

        def v_back(j, _):
            src = pl.ds(pl.multiple_of(BAND + j * chunk, BAND), chunk)
            st[_token_rows(j, d, chunk, per_r), :] = dvd[src, :]
            return 0
        lax.fori_loop(0, d * per_r, v_back, 0, unroll=2)
        store_cols(2, V0)

        @pl.when(hp == n_steps - 1)
        def _():
            for slot, col0 in enumerate((Q0, K0, V0)):
                col_copy(slot, col0).wait()

    def body(*refs):
        step = pl.program_id(0)
        for g in range(N_GROUPS):
            pl.when(step // N_PAIRS == g)(functools.partial(group_body, g, step, *refs))

    col = lambda off: pl.BlockSpec((s, LANE), lambda i, off=off: (0, off // LANE + i))
    mid = pl.BlockSpec((s, LANE), lambda i: (0, i))
    slot4 = pl.BlockSpec((s, LANE), lambda i: (0, i % N_PAIRS))
    vec = pl.BlockSpec((1, LANE), lambda i: (0, 0))
    acc = pl.BlockSpec((8, LANE), lambda i: (0, 0))
    any_ = pl.BlockSpec(memory_space=pl.ANY)
    return pl.pallas_call(
        body, name="attn_bwd", grid=(n_steps,),
        in_specs=[col(Q0), col(K0), mid, mid, mid, slot4, slot4, vec, vec, any_],
        out_specs=[any_, acc, acc],
        out_shape=[SDS(dproj.shape, dproj.dtype), SDS((8, LANE), F32), SDS((8, LANE), F32)],
        input_output_aliases={9: 0},
        scratch_shapes=[pltpu.VMEM((s + BAND, LANE), BF16), pltpu.VMEM((s + BAND, LANE), BF16),
                        pltpu.VMEM((s // BAND + 1, LANE, BAND), BF16), pltpu.VMEM((s, LANE), BF16),
                        pltpu.VMEM((s // BAND, 8, BAND), F32), pltpu.VMEM((s // BAND, 8, BAND), F32),
                        pltpu.VMEM((s // BAND, LANE, BAND), F32),
                        pltpu.VMEM((s, LANE), F32), pltpu.VMEM((s + BAND, LANE), F32), pltpu.VMEM((s + BAND, LANE), F32),
                        pltpu.VMEM((s, LANE), F32), pltpu.VMEM((3, s, LANE), BF16),
                        pltpu.VMEM((2, 2 * BAND, 2 * BAND), F32), pltpu.VMEM((8, LANE), F32),
                        pltpu.SemaphoreType.DMA((3,))],
        compiler_params=_params(),
    )(proj, proj, qn, kn, vn, da, lse_delta, qw2, kw2, dproj)


def _tap_views(ext_ref, sh_ref, offsets, tr, cols):
    for b in range(8):
        group = [j for j, o in enumerate(offsets) if o % 8 == b]
        if not group:
            continue
        first = min(offsets[j] for j in group)
        span = tr + max(offsets[j] for j in group) - first
        sh_ref[0:span, cols] = ext_ref[first:first + span, cols]
        for j in group:
            yield j, sh_ref[offsets[j] - first:offsets[j] - first + tr, cols]


def _silu_grad(z, sg):
    return sg * (1.0 + z * (1.0 - sg))


def _glu(u):
    a_h, b_h = u[:, :CONV_W], u[:, CONV_W:]
    sg = _sigmoid(b_h)
    return a_h, sg, a_h * sg


def _tail(x, tgt, proj, o3, l3, wa, wc, wo, gate, bga, bgc, convw, convb, lnw, lnb, bd):
    s = x.shape[0]
    tr = 256

    def body(x_ref, t_ref, za_ref, u_ref, uh_ref, zc_ref, g0_ref, g1_ref, g2_ref, g3_ref,
             o0_ref, o1_ref, o2_ref, l0_ref, l1_ref, l2_ref, wa_ref, wc_ref, wo_ref,
             gate_ref, bga_ref, bgc_ref, cw_ref, cb_ref, lnw_ref, lnb_ref, bd_ref,
             dout_ref, da_ref, ld_ref, dcv_ref, mt_ref, yat_ref, yct_ref, dmo_ref, dya_ref, dyc_ref, dp_ref,
             dgate_ref, dbg_ref, dlnw_ref, dlnb_ref, dcb_ref, loss_ref,
             ext, sh, st_za, st_zc, st_g, sems):
        i = pl.program_id(0)

        @pl.when(i == 0)
        def _():
            for r in (dgate_ref, dbg_ref, dlnw_ref, dlnb_ref, dcb_ref, loss_ref):
                r[...] = jnp.zeros_like(r)

        def acc_rows(ref, v):
            ref[...] += jnp.broadcast_to(jnp.sum(v, axis=0, keepdims=True), ref.shape)

        la, lb, lc = l0_ref[...], l1_ref[...], l2_ref[...]
        mx = jnp.maximum(jnp.maximum(la, lb), lc)
        ea, eb, ec = jnp.exp(la - mx), jnp.exp(lb - mx), jnp.exp(lc - mx)
        den = ea + eb + ec
        inv = 1.0 / den
        attn = (ea * inv) * o0_ref[...] + (eb * inv) * o1_ref[...] + (ec * inv) * o2_ref[...]
        lse = mx + jnp.log(den)

        za = za_ref[...]
        sga = _sigmoid(za)
        sa = za * sga
        ya_in = attn * sa
        y_attn = _dot(ya_in.astype(BF16), wa_ref[...])

        _, _, glu = _glu(u_ref[...])
        _, _, glu_h = _glu(uh_ref[...])
        ext[0:CONV_HALO, :] = jnp.where(i > 0, glu_h, 0.0)
        ext[CONV_HALO:CONV_HALO + tr, :] = glu
        cv_blocks = []
        for cb in range(CONV_W // LANE):
            cols = slice(cb * LANE, (cb + 1) * LANE)
            cv_c = jnp.broadcast_to(cb_ref[:, cols], (tr, LANE))
            for j, rows in _tap_views(ext, sh, [CONV_HALO - (CONV_K - 1) + j for j in range(CONV_K)], tr, cols):
                cv_c = cv_c + cw_ref[j:j + 1, cols] * rows
            cv_blocks.append(cv_c)
        cv = jnp.concatenate(cv_blocks, axis=1)
        mu = jnp.mean(cv, axis=-1, keepdims=True)
        xc = cv - mu
        rstd = lax.rsqrt(jnp.mean(xc * xc, axis=-1, keepdims=True) + EPS)
        nrm = xc * rstd
        ln = nrm * lnw_ref[...] + lnb_ref[...]
        sgl = _sigmoid(ln)
        cs = ln * sgl
        zc = zc_ref[...]
        sgc = _sigmoid(zc)
        scz = zc * sgc
        yc_in = cs * scz
        y_conv = _dot(yc_in.astype(BF16), wc_ref[...])

        ga = _sigmoid(jnp.concatenate([g0_ref[...], g1_ref[...]], axis=1) + bga_ref[...])
        gc = _sigmoid(jnp.concatenate([g2_ref[...], g3_ref[...]], axis=1) + bgc_ref[...])
        merged = ga * y_attn + gc * y_conv
        mo = _dot(merged.astype(BF16), wo_ref[...])
        gate_v = gate_ref[...]
        err = (x_ref[...] + gate_v * mo) - t_ref[...]
        loss_ref[...] += 0.5 * jnp.sum(jnp.mean(err * err, axis=-1, keepdims=True))
        d_out = err * (1.0 / D_MODEL)
        dout_ref[...] = d_out

        rows = pl.ds(pl.multiple_of(i * tr, tr), tr)
        cps = [pltpu.make_async_copy(st_za, dp_ref.at[rows, pl.ds(ZA0, ATTN_W)], sems.at[0]),
               pltpu.make_async_copy(st_zc, dp_ref.at[rows, pl.ds(ZC0, CONV_W)], sems.at[1]),
               pltpu.make_async_copy(st_g, dp_ref.at[rows, pl.ds(G0, 2 * D_MODEL)], sems.at[2])]

        @pl.when(i > 0)
        def _():
            for cp in cps:
                cp.wait()

        acc_rows(dgate_ref, d_out * mo)
        dmo_b = (d_out * gate_v).astype(BF16)
        dmo_ref[...] = dmo_b
        mt_ref[...] = merged.T.astype(BF16)
        d_merged = _dot_nt(dmo_b, wo_ref[...])
        d_ya = (d_merged * ga).astype(BF16)
        d_yc = (d_merged * gc).astype(BF16)
        dya_ref[...] = d_ya
        dyc_ref[...] = d_yc
        dga = d_merged * y_attn * (ga * (1.0 - ga))
        dgc = d_merged * y_conv * (gc * (1.0 - gc))
        dgs = jnp.concatenate([dga, dgc], axis=1)
        acc_rows(dbg_ref, dgs)
        st_g[...] = dgs.astype(BF16)

        yat_ref[...] = ya_in.T.astype(BF16)
        d_ya_in = _dot_nt(d_ya, wa_ref[...])
        d_attn = d_ya_in * sa
        da_ref[...] = d_attn
        st_za[...] = (d_ya_in * attn * _silu_grad(za, sga)).astype(BF16)
        prod = d_attn * attn
        hi = prod.astype(BF16)
        lo_ = (prod - hi.astype(F32)).astype(BF16)
        delta = _dot(hi, bd_ref[...]) + _dot(lo_, bd_ref[...])
        first_half = (lax.broadcasted_iota(jnp.int32, (1, ATTN_W), 1) % HEAD_DIM) < HEAD_DIM // 2
        ld_ref[...] = jnp.where(first_half, lse, delta)

        yct_ref[...] = yc_in.T.astype(BF16)
        d_yc_in = _dot_nt(d_yc, wc_ref[...])
        st_zc[...] = (d_yc_in * cs * _silu_grad(zc, sgc)).astype(BF16)
        d_ln = (d_yc_in * scz) * _silu_grad(ln, sgl)
        acc_rows(dlnw_ref, d_ln * nrm)
        acc_rows(dlnb_ref, d_ln)
        d_nrm = d_ln * lnw_ref[...]
        d_cv = rstd * (d_nrm - jnp.mean(d_nrm, axis=-1, keepdims=True)
                       - nrm * jnp.mean(d_nrm * nrm, axis=-1, keepdims=True))
        acc_rows(dcb_ref, d_cv)
        dcv_ref[...] = d_cv

        for cp in cps:
            cp.start()

        @pl.when(i == s // tr - 1)
        def _():
            for cp in cps:
                cp.wait()

    def rows(width, colblk=0):
        return pl.BlockSpec((tr, width), lambda i, colblk=colblk: (i, colblk))

    def const(shape):
        return pl.BlockSpec(shape, lambda i: (0,) * len(shape))

    halo = pl.BlockSpec((CONV_HALO, D_MODEL), lambda i: (jnp.maximum(i * (tr // CONV_HALO) - 1, 0), U0 // D_MODEL))
    in_specs = [rows(D_MODEL), rows(D_MODEL), rows(ATTN_W, ZA0 // ATTN_W), rows(D_MODEL, U0 // D_MODEL), halo,
                rows(CONV_W, ZC0 // CONV_W)]
    in_specs += [rows(512, G0 // 512 + j) for j in range(4)]
    in_specs += [rows(ATTN_W, g) for g in range(N_GROUPS)] * 2
    in_specs += [const(wa.shape), const(wc.shape), const(wo.shape), const((1, D_MODEL)), const((1, D_MODEL)),
                 const((1, D_MODEL)), const(convw.shape), const((1, CONV_W)), const((1, CONV_W)), const((1, CONV_W)),
                 const(bd.shape)]
    tcol = lambda width: pl.BlockSpec((width, tr), lambda i: (0, i))
    out_specs = [rows(D_MODEL), rows(ATTN_W), rows(ATTN_W), rows(CONV_W),
                 tcol(D_MODEL), tcol(ATTN_W), tcol(CONV_W), rows(D_MODEL), rows(D_MODEL), rows(D_MODEL),
                 pl.BlockSpec(memory_space=pl.ANY),
                 const((8, D_MODEL)), const((8, 2 * D_MODEL)), const((8, CONV_W)), const((8, CONV_W)), const((8, CONV_W)),
                 const((8, LANE))]
    out_shape = [SDS((s, D_MODEL), F32), SDS((s, ATTN_W), F32), SDS((s, ATTN_W), F32),
                 SDS((s, CONV_W), F32),
                 SDS((D_MODEL, s), BF16), SDS((ATTN_W, s), BF16), SDS((CONV_W, s), BF16),
                 SDS((s, D_MODEL), BF16), SDS((s, D_MODEL), BF16), SDS((s, D_MODEL), BF16),
                 SDS((s, IN_W), BF16),
                 SDS((8, D_MODEL), F32), SDS((8, 2 * D_MODEL), F32), SDS((8, CONV_W), F32), SDS((8, CONV_W), F32),
                 SDS((8, CONV_W), F32), SDS((8, LANE), F32)]
    return pl.pallas_call(
        body, name="tail", grid=(s // tr,), in_specs=in_specs, out_specs=out_specs, out_shape=out_shape,
        scratch_shapes=[pltpu.VMEM((CONV_HALO + tr, CONV_W), F32), pltpu.VMEM((CONV_HALO + tr, CONV_W), F32),
                        pltpu.VMEM((tr, ATTN_W), BF16),
                        pltpu.VMEM((tr, CONV_W), BF16), pltpu.VMEM((tr, 2 * D_MODEL), BF16),
                        pltpu.SemaphoreType.DMA((3,))],
        compiler_params=_params(),
    )(x, tgt, proj, proj, proj, proj, proj, proj, proj, proj, *o3, *l3, wa, wc, wo, gate, bga, bgc,
      convw, convb, lnw, lnb, bd)


def _conv_bwd(dcv, proj, convw, dproj):
    s = dcv.shape[0]
    tr = 128
    nt = s // tr

    def body(dcv_ref, dcvn_ref, u_ref, uh_ref, cw_ref, dp_in, dp_out, dw_ref, extg, extd, sh):
        del dp_in
        i = pl.program_id(0)

        @pl.when(i == 0)
        def _():
            dw_ref[...] = jnp.zeros_like(dw_ref)

        _, _, glu = _glu(u_ref[...])
        _, _, glu_h = _glu(uh_ref[...])
        extg[0:CONV_HALO, :] = jnp.where(i > 0, glu_h, 0.0)
        extg[CONV_HALO:CONV_HALO + tr, :] = glu
        extd[0:tr, :] = dcv_ref[...]
        extd[tr:tr + CONV_HALO, :] = jnp.where(i < nt - 1, dcvn_ref[...], 0.0)
        for cb in range(CONV_W // LANE):
            cols = slice(cb * LANE, (cb + 1) * LANE)
            dglu = jnp.zeros((tr, LANE), F32)
            for j, rows in _tap_views(extd, sh, [CONV_K - 1 - j for j in range(CONV_K)], tr, cols):
                dglu = dglu + cw_ref[j:j + 1, cols] * rows
            dcv_c = dcv_ref[:, cols]
            for j, rows in _tap_views(extg, sh, [CONV_HALO - (CONV_K - 1) + j for j in range(CONV_K)], tr, cols):
                dw_ref[8 * j:8 * j + 8, cols] += jnp.sum((dcv_c * rows).reshape(tr // 8, 8, LANE), axis=0)
            a_h = u_ref[:, cols]
            sgb = _sigmoid(u_ref[:, CONV_W + cb * LANE:CONV_W + (cb + 1) * LANE])
            dp_out[:, cols] = (dglu * sgb).astype(BF16)
            dp_out[:, CONV_W + cb * LANE:CONV_W + (cb + 1) * LANE] = (dglu * a_h * (sgb * (1.0 - sgb))).astype(BF16)

    ucol = U0 // D_MODEL
    return pl.pallas_call(
        body, name="conv_bwd", grid=(nt,),
        in_specs=[pl.BlockSpec((tr, CONV_W), lambda i: (i, 0)),
                  pl.BlockSpec((CONV_HALO, CONV_W), lambda i: (jnp.minimum((i + 1) * (tr // CONV_HALO), s // CONV_HALO - 1), 0)),
                  pl.BlockSpec((tr, D_MODEL), lambda i: (i, ucol)),
                  pl.BlockSpec((CONV_HALO, D_MODEL), lambda i: (jnp.maximum(i * (tr // CONV_HALO) - 1, 0), ucol)),
                  pl.BlockSpec(convw.shape, lambda i: (0, 0)),
                  pl.BlockSpec(memory_space=pl.ANY)],
        out_specs=[pl.BlockSpec((tr, D_MODEL), lambda i: (i, ucol)), pl.BlockSpec((8 * CONV_HALO, CONV_W), lambda i: (0, 0))],
        out_shape=[SDS(dproj.shape, dproj.dtype), SDS((8 * CONV_HALO, CONV_W), F32)],
        input_output_aliases={5: 0},
        scratch_shapes=[pltpu.VMEM((CONV_HALO + tr, CONV_W), F32)] * 3,
        compiler_params=_params(),
    )(dcv, dcv, proj, proj, convw, dproj)


def _mm_acc(at, b, name, col_slots):
    m, s = at.shape
    n = b.shape[1]
    tk = 512
    nk = s // tk

    def body(a_ref, b_ref, o_ref, acc):
        k = pl.program_id(0)

        @pl.when(k == 0)
        def _():
            acc[...] = jnp.zeros_like(acc)

        acc[...] += _dot(a_ref[...], b_ref[...])

        @pl.when(k == nk - 1)
        def _():
            if col_slots:
                w = n // N_DEV
                for j in range(N_DEV):
                    o_ref[j] = acc[:, j * w:(j + 1) * w].astype(BF16)
            else:
                o_ref[...] = acc[...].astype(BF16)

    if col_slots:
        out_shape = SDS((N_DEV, m, n // N_DEV), BF16)
        out_spec = pl.BlockSpec((N_DEV, m, n // N_DEV), lambda k: (0, 0, 0))
    else:
        out_shape = SDS((m, n), BF16)
        out_spec = pl.BlockSpec((m, n), lambda k: (0, 0))
    return pl.pallas_call(
        body, name=name, grid=(nk,),
        in_specs=[pl.BlockSpec((m, tk), lambda k: (0, k)), pl.BlockSpec((tk, n), lambda k: (k, 0))],
        out_specs=out_spec, out_shape=out_shape, scratch_shapes=[pltpu.VMEM((m, n), F32)],
        compiler_params=_params(),
    )(at, b)


def _mm_dw(ht, dproj):
    s = ht.shape[1]
    tk = 512
    nk = s // tk

    def body(a_ref, b_ref, o_ref, acc):
        k = pl.program_id(1)

        @pl.when(k == 0)
        def _():
            acc[...] = jnp.zeros_like(acc)

        acc[...] += _dot(a_ref[...], b_ref[...])

        @pl.when(k == nk - 1)
        def _():
            o_ref[...] = acc[...].T.astype(BF16)

    return pl.pallas_call(
        body, name="mm_dw", grid=(IN_W // PAIR_W, nk),
        in_specs=[pl.BlockSpec((D_MODEL, tk), lambda p, k: (0, k)), pl.BlockSpec((tk, PAIR_W), lambda p, k: (k, p))],
        out_specs=pl.BlockSpec((PAIR_W, D_MODEL), lambda p, k: (p, 0)),
        out_shape=SDS((IN_W, D_MODEL), BF16), scratch_shapes=[pltpu.VMEM((D_MODEL, PAIR_W), F32)],
        compiler_params=_params(),
    )(ht, dproj)


def _mm_dh_norm_bwd(dproj, wt, x, dout, norm_w, scale, token):
    s = dproj.shape[0]
    tm = 1024
    n_p = IN_W // PAIR_W

    def body(dp_ref, w_ref, x_ref, do_ref, nw_ref, sc_ref, tok_ref, gx_ref, dsh_ref, dsc_ref, dnw_ref, dh_acc):
        del tok_ref
        m, p = pl.program_id(0), pl.program_id(1)
        part = _dot(dp_ref[...], w_ref[...])

        @pl.when(p == 0)
        def _():
            dh_acc[...] = part

        @pl.when(p > 0)
        def _():
            dh_acc[...] += part

        @pl.when((m == 0) & (p == 0))
        def _():
            for r in (dsh_ref, dsc_ref, dnw_ref):
                r[...] = jnp.zeros_like(r)

        @pl.when(p == n_p - 1)
        def _():
            def acc_rows(ref, v):
                ref[...] += jnp.broadcast_to(jnp.sum(v, axis=0, keepdims=True), ref.shape)

            xv = x_ref[...]
            dh_v = dh_acc[...]
            r = lax.rsqrt(jnp.mean(xv * xv, axis=-1, keepdims=True) + EPS)
            xn = xv * r
            one_sc = 1.0 + sc_ref[...]
            acc_rows(dsh_ref, dh_v)
            acc_rows(dsc_ref, dh_v * (xn * nw_ref[...]))
            acc_rows(dnw_ref, dh_v * xn * one_sc)
            dxn = dh_v * (nw_ref[...] * one_sc)
            gx_ref[...] = do_ref[...] + r * (dxn - xn * jnp.mean(dxn * xn, axis=-1, keepdims=True))

    rows = pl.BlockSpec((tm, D_MODEL), lambda m, p: (m, 0))
    vec = pl.BlockSpec((1, D_MODEL), lambda m, p: (0, 0))
    acc = pl.BlockSpec((8, D_MODEL), lambda m, p: (0, 0))
    return pl.pallas_call(
        body, name="mm_dh_norm_bwd", grid=(s // tm, n_p),
        in_specs=[pl.BlockSpec((tm, PAIR_W), lambda m, p: (m, p)),
                  pl.BlockSpec((PAIR_W, D_MODEL), lambda m, p: (p, 0)),
                  rows, rows, vec, vec, pl.BlockSpec(token.shape, lambda m, p: (0, 0))],
        out_specs=[rows, acc, acc, acc],
        out_shape=[SDS((s, D_MODEL), F32)] + [SDS((8, D_MODEL), F32)] * 3,
        scratch_shapes=[pltpu.VMEM((tm, D_MODEL), F32)], compiler_params=_params(),
    )(dproj, wt, x, dout, norm_w, scale, token)


SMALL_ROWS = 8
QN_COL, KN_COL, CB_COL, LOSS_COL = 0, LANE, 2 * LANE, 2 * LANE + CONV_W


def _pack_partials(dsh, dsc, dgate, dnw, dbg, dqw3, dkw3, dcb, dlnw, dlnb, loss_p):
    n3 = len(dqw3)

    def body(*refs):
        dsh_r, dsc_r, dgate_r, dnw_r, dbg_r = refs[:5]
        dq_r, dk_r = refs[5:5 + n3], refs[5 + n3:5 + 2 * n3]
        dcb_r, dlnw_r, dlnb_r, loss_r, o_ref = refs[5 + 2 * n3:]

        def both_heads(rs):
            t = rs[0][0:1, :]
            for r in rs[1:]:
                t = t + r[0:1, :]
            return t + pltpu.roll(t, HEAD_DIM, axis=1)

        o_ref[0:1, :] = dsh_r[0:1, :]
        o_ref[1:2, :] = dsc_r[0:1, :]
        o_ref[2:3, :] = dgate_r[0:1, :]
        o_ref[3:4, :] = dnw_r[0:1, :]
        o_ref[4:5, :] = dbg_r[0:1, 0:D_MODEL]
        o_ref[5:6, :] = dbg_r[0:1, D_MODEL:]
        o_ref[6:7, QN_COL:QN_COL + LANE] = both_heads(dq_r)
        o_ref[6:7, KN_COL:KN_COL + LANE] = both_heads(dk_r)
        o_ref[6:7, CB_COL:CB_COL + CONV_W] = dcb_r[0:1, :]
        o_ref[6:7, LOSS_COL:LOSS_COL + LANE] = loss_r[0:1, :]
        o_ref[6:7, LOSS_COL + LANE:] = jnp.zeros((1, D_MODEL - LOSS_COL - LANE), F32)
        o_ref[7:8, 0:CONV_W] = dlnw_r[0:1, :]
        o_ref[7:8, CONV_W:] = dlnb_r[0:1, :]

    return pl.pallas_call(body, name="pack_partials", out_shape=SDS((SMALL_ROWS, D_MODEL), F32),
                          compiler_params=_params())(dsh, dsc, dgate, dnw, dbg, *dqw3, *dkw3, dcb, dlnw, dlnb, loss_p)


def _adamw_update(g, w, m, v):
    bc1 = 1.0 - ADAM_B1 ** ADAM_STEP
    bc2 = 1.0 - ADAM_B2 ** ADAM_STEP
    m_new = ADAM_B1 * m + (1.0 - ADAM_B1) * g
    v_new = ADAM_B2 * v + (1.0 - ADAM_B2) * (g * g)
    delta = -ADAM_LR * ((m_new / bc1) / (jnp.sqrt(v_new / bc2) + ADAM_EPS) + ADAM_WD * w)
    return delta, m_new, v_new


def _adamw_small(small_all, ws, ms, vs):
    n = len(ws)
    where = [(slice(0, 3), None), (slice(3, 4), None), (slice(4, 6), None), (6, QN_COL), (6, KN_COL), (6, CB_COL),
             (7, 0), (7, CONV_W)]

    def body(*refs):
        g_ref = refs[0]
        w_r, m_r, v_r = refs[1:1 + n], refs[1 + n:1 + 2 * n], refs[1 + 2 * n:1 + 3 * n]
        outs = refs[1 + 3 * n:]
        g_o, d_o, m_o, v_o, loss_o = outs[:n], outs[n:2 * n], outs[2 * n:3 * n], outs[3 * n:4 * n], outs[4 * n]
        gsum = g_ref[0]
        for dev in range(1, N_DEV):
            gsum = gsum + g_ref[dev]
        loss_o[...] = gsum[6:7, LOSS_COL:LOSS_COL + LANE]
        for i, (rows, col) in enumerate(where):
            width = w_r[i].shape[1]
            if col is None:
                g = jnp.concatenate([gsum[r:r + 1, :] for r in range(rows.start, rows.stop)], axis=1)
            else:
                g = gsum[rows:rows + 1, col:col + width]
            delta, m_new, v_new = _adamw_update(g, w_r[i][...], m_r[i][...], v_r[i][...])
            g_o[i][...] = g
            d_o[i][...] = delta
            m_o[i][...] = m_new
            v_o[i][...] = v_new

    shapes = [SDS(w.shape, F32) for w in ws]
    res = pl.pallas_call(body, name="adamw_small", out_shape=shapes * 4 + [SDS((1, LANE), F32)],
                         compiler_params=_params())(small_all, *ws, *ms, *vs)
    return [res[k * n:(k + 1) * n] for k in range(4)], res[4 * n]


def _row_tile(rows):
    if rows <= 128:
        return rows
    return 128 if rows % 128 == 0 else SHARD_W // 4


def _adamw(gsrc, w, m, v, name, stacked):
    rows, cols = w.shape
    tr = _row_tile(rows)
    n_src = len(gsrc) if stacked else 1

    def body(*refs):
        g_refs, (w_ref, m_ref, v_ref, go_ref, d_ref, mo_ref, vo_ref) = refs[:n_src], refs[n_src:]
        if stacked:
            g = None
            for g_ref, (_, slots) in zip(g_refs, gsrc):
                for j in range(slots):
                    t = g_ref[j].astype(F32)
                    g = t if g is None else g + t
        else:
            g = g_refs[0][...]
        delta, m_new, v_new = _adamw_update(g, w_ref[...], m_ref[...], v_ref[...])
        go_ref[...] = g
        d_ref[...] = delta
        mo_ref[...] = m_new
        vo_ref[...] = v_new

    blk = pl.BlockSpec((tr, cols), lambda i: (i, 0))
    if stacked:
        gspecs = [pl.BlockSpec((slots, tr, arr.shape[2]), lambda i: (0, i, 0)) for arr, slots in gsrc]
        gargs = [arr for arr, _ in gsrc]
    else:
        gspecs, gargs = [blk], [gsrc]
    in_specs = gspecs + [blk, blk, blk]
    args = gargs + [w, m, v]
    return pl.pallas_call(
        body, name=name, grid=(rows // tr,), in_specs=in_specs, out_specs=[blk] * 4,
        out_shape=[SDS((rows, cols), F32)] * 4, compiler_params=_params(),
    )(*args)


def kernel(x, c, w_ada, b_ada, norm_w, w_in, b_gate, q_norm_w, k_norm_w, w_attn_proj, conv_w, conv_b, conv_ln_w, conv_ln_b, w_conv_proj, w_out, loss_target, m_w_ada, m_b_ada, m_norm_w, m_w_in, m_b_gate, m_q_norm_w, m_k_norm_w, m_w_attn_proj, m_conv_w, m_conv_b, m_conv_ln_w, m_conv_ln_b, m_w_conv_proj, m_w_out, v_w_ada, v_b_ada, v_norm_w, v_w_in, v_b_gate, v_q_norm_w, v_k_norm_w, v_w_attn_proj, v_conv_w, v_conv_b, v_conv_ln_w, v_conv_ln_b, v_w_conv_proj, v_w_out):
    xi, yi, ci = lax.axis_index("x"), lax.axis_index("y"), lax.axis_index("c")
    me = 4 * xi + 2 * yi + ci
    x2, tgt2 = x[0], loss_target[0]
    w_in_t, m_w_in_t, v_w_in_t = (jnp.transpose(a[0]) for a in (w_in, m_w_in, v_w_in))
    s = x2.shape[0]

    cw_flat = jnp.pad(conv_w[0].reshape(1, -1), ((0, 0), (0, CONVW_FLAT - CONV_K * HEAD_DIM)))
    pre = jnp.concatenate([c, cw_flat], axis=1).reshape(8, -1)
    (pre_all,) = _all_gather([pre], "gather_c_convw", vmem=True)
    pre_all = pre_all.reshape(N_DEV, -1)
    c_all = pre_all[:, :D_MODEL]
    convw_full = pre_all[:, D_MODEL:D_MODEL + CONV_K * HEAD_DIM].reshape(N_DEV, CONV_K, HEAD_DIM)
    convw_full = jnp.transpose(convw_full, (1, 0, 2)).reshape(CONV_K, CONV_W)
    convw_pad = jnp.pad(convw_full, ((0, CONV_HALO - CONV_K), (0, 0)))

    ada_part = _ada_fwd(c_all, w_ada[0])
    (ada_all,) = _all_gather([ada_part], "gather_ada", vmem=True)
    ada = lax.dynamic_index_in_dim(ada_all, me, axis=1, keepdims=False).reshape(1, 3 * D_MODEL) + b_ada
    shift, scale, gate = ada[:, :D_MODEL], ada[:, D_MODEL:2 * D_MODEL], ada[:, 2 * D_MODEL:]

    wt_g, wa_g, wc_g, wo_g = _all_gather_chips(
        [_cast_bf16(w_in_t, "cast_win"), _cast_bf16(w_attn_proj[0], "cast_wa"), _cast_bf16(w_conv_proj[0], "cast_wc"),
         _cast_bf16(w_out[0], "cast_wo")], "gather_weights")
    wt = wt_g.reshape(IN_W, D_MODEL)
    wa = _cols_from_slots(wa_g, "cols_wa")
    wc = _cols_from_slots(wc_g, "cols_wc")
    wo = wo_g.reshape(D_MODEL, D_MODEL)

    h, ht = _norm_fwd(x2, norm_w, scale, shift)
    proj = _mm_in(h, wt)
    qw2 = jnp.tile(q_norm_w, (1, 2))
    kw2 = jnp.tile(k_norm_w, (1, 2))
    o_all, l_all, qn, kn, vn = _attn_fwd(proj, qw2, kw2)
    o3, l3 = [o_all] * N_GROUPS, [l_all] * N_GROUPS
    head_id = jnp.arange(ATTN_W) // HEAD_DIM
    bd = (head_id[:, None] == head_id[None, :]).astype(BF16)
    (dout, da, lse_delta, dcv, mt, yat, yct, dmo, dya, dyc, dproj,
     dgate, dbg, dlnw, dlnb, dcb, loss_p) = _tail(
        x2, tgt2, proj, o3, l3, wa, wc, wo, gate, b_gate[:, :D_MODEL], b_gate[:, D_MODEL:], convw_pad,
        conv_b, conv_ln_w, conv_ln_b, bd)

    dproj, dconvw8 = _conv_bwd(dcv, proj, convw_pad, dproj)
    dconvw = jnp.sum(dconvw8.reshape(CONV_HALO, 8, CONV_W), axis=1)
    dproj, dqw_all, dkw_all = _attn_bwd(proj, qn, kn, vn, da, lse_delta, qw2, kw2, dproj)
    dqw_g3, dkw_g3 = [dqw_all], [dkw_all]
    dw_in_p = _mm_dw(ht, dproj).reshape(N_DEV, SHARD_W, D_MODEL)
    dwo_p = _mm_acc(mt, dmo, "mm_dwo", col_slots=False).reshape(N_DEV, D_MODEL // N_DEV, D_MODEL)
    dwa_p = _mm_acc(yat, dya, "mm_dwa", col_slots=True)
    dwc_p = _mm_acc(yct, dyc, "mm_dwc", col_slots=True)

    partials = [dw_in_p, dwa_p, dwc_p, dwo_p]
    me_arr = jnp.reshape(me, (1,)).astype(jnp.int32)
    from_sib = _exchange_sibling(partials, "exchange_sibling")
    presums = [_presum(p, f, me_arr, f"presum{i}") for i, (p, f) in enumerate(zip(partials, from_sib))]
    s_sems, r_sems, pre_thru, land_thru, token = _exchange_chips_start(presums, "exchange_chips_start")
    gx, dsh, dsc, dnw = _mm_dh_norm_bwd(dproj, wt, x2, dout, norm_w, scale, token)
    small_p = _pack_partials(dsh, dsc, dgate, dnw, dbg, dqw_g3, dkw_g3, dcb, dlnw, dlnb, loss_p)
    small_all, dconvw_all = _all_gather([small_p, dconvw], "gather_small", vmem=True)

    small_w = (b_ada, norm_w, b_gate, q_norm_w, k_norm_w, conv_b, conv_ln_w, conv_ln_b)
    small_m = (m_b_ada, m_norm_w, m_b_gate, m_q_norm_w, m_k_norm_w, m_conv_b, m_conv_ln_w, m_conv_ln_b)
    small_v = (v_b_ada, v_norm_w, v_b_gate, v_q_norm_w, v_k_norm_w, v_conv_b, v_conv_ln_w, v_conv_ln_b)
    r_small, loss_row = _adamw_small(small_all, small_w, small_m, small_v)
    dcw_mine = lax.dynamic_slice_in_dim(dconvw_all[:, :CONV_K, :], me * HEAD_DIM, HEAD_DIM, axis=2)
    r_convw = _adamw([(dcw_mine, N_DEV)], conv_w[0], m_conv_w[0], v_conv_w[0], "adamw_conv_w", stacked=True)

    d_ada_all = small_all[:, 0:3, :].reshape(N_DEV, 3 * D_MODEL)
    d_ada_cols = lax.dynamic_slice_in_dim(d_ada_all, me * (3 * D_MODEL // N_DEV), 3 * D_MODEL // N_DEV, axis=1)
    g_wada = _ada_bwd(c_all, d_ada_cols)
    r_ada = _adamw(g_wada, w_ada[0], m_w_ada[0], v_w_ada[0], "adamw_w_ada", stacked=False)
    pres, lands = _exchange_chips_wait(s_sems, r_sems, pre_thru, land_thru, r_ada[1], "exchange_chips_wait")
    terms = [[(p, 1), (l, len(CHIP_K))] for p, l in zip(pres, lands)]
    r_win = [jnp.transpose(r) for r in _adamw(terms[0], w_in_t, m_w_in_t, v_w_in_t, "adamw_w_in", stacked=True)]
    r_wap = _adamw(terms[1], w_attn_proj[0], m_w_attn_proj[0], v_w_attn_proj[0], "adamw_w_attn_proj", stacked=True)
    r_wcp = _adamw(terms[2], w_conv_proj[0], m_w_conv_proj[0], v_w_conv_proj[0], "adamw_w_conv_proj", stacked=True)
    r_wout = _adamw(terms[3], w_out[0], m_w_out[0], v_w_out[0], "adamw_w_out", stacked=True)

    outs = [loss_row[0, 0], gx[None]]
    for k in range(4):
        b_ada_k, norm_w_k, b_gate_k, qn_k, kn_k, conv_b_k, ln_w_k, ln_b_k = r_small[k]
        outs += [r_ada[k][None], b_ada_k, norm_w_k, r_win[k][None], b_gate_k, qn_k, kn_k, r_wap[k][None],
                 r_convw[k][None], conv_b_k, ln_w_k, ln_b_k, r_wcp[k][None], r_wout[k][None]]
    return tuple(outs)
```

```python
import functools

import jax
import jax.numpy as jnp
from jax import lax
from jax.experimental import pallas as pl
from jax.experimental.pallas import tpu as pltpu

F32 = jnp.float32
BF16 = jnp.bfloat16
SDS = jax.ShapeDtypeStruct
MESH = pl.DeviceIdType.MESH

N_DEV = 8
D_MODEL = 1024
HEAD_DIM = 64
N_GROUPS = 3
DILATIONS = (1, 4, 16)
BAND = 128
BWD_UNROLL = 8
ATTN_W = 512
CONV_W = 512
CONV_K = 31
CONV_HALO = 32
IN_W = 8704
SHARD_W = IN_W // N_DEV
PAIR_W = 2 * SHARD_W
Q0, K0, V0, ZA0, U0, ZC0, G0 = 0, 1536, 3072, 4608, 5120, 6144, 6656
EPS = 1e-6
LANE = 128
VMEM_LIMIT = 56 * 1024 * 1024

ADAM_LR, ADAM_B1, ADAM_B2, ADAM_EPS, ADAM_WD, ADAM_STEP = 0.001, 0.9, 0.999, 1e-08, 0.01, 10

CONVW_FLAT = 2048


def _params(**kw):
    return pltpu.CompilerParams(vmem_limit_bytes=VMEM_LIMIT, **kw)


def _sigmoid(z):
    return 0.5 * jnp.tanh(0.5 * z) + 0.5


def _dot(a, b):
    return jnp.dot(a, b, preferred_element_type=F32)


def _dot_nt(a, b):
    return lax.dot_general(a, b, (((1,), (1,)), ((), ())), preferred_element_type=F32)


def _dot_tn(a, b):
    return lax.dot_general(a, b, (((0,), (0,)), ((), ())), preferred_element_type=F32)


def _peer(x, y, c, k):
    px = 1 - x if (k >> 2) & 1 else x
    py = 1 - y if (k >> 1) & 1 else y
    pc = 1 - c if k & 1 else c
    return (px, py, pc), 4 * px + 2 * py + pc


def _all_gather(arrays, name, vmem):
    n = len(arrays)
    space = pltpu.VMEM if vmem else pl.ANY

    def body(*refs):
        ins, outs = refs[:n], refs[n:2 * n]
        send_sems, recv_sems, local_sems = refs[2 * n:]
        x, y, c = lax.axis_index("x"), lax.axis_index("y"), lax.axis_index("c")
        me = 4 * x + 2 * y + c
        locals_ = [pltpu.make_async_copy(ins[a], outs[a].at[me], local_sems.at[a]) for a in range(n)]
        for cp in locals_:
            cp.start()
        sends = []
        for k in range(1, N_DEV):
            peer, _ = _peer(x, y, c, k)
            for a in range(n):
                cp = pltpu.make_async_remote_copy(
                    src_ref=ins[a], dst_ref=outs[a].at[me], send_sem=send_sems.at[a, k - 1],
                    recv_sem=recv_sems.at[a, k - 1], device_id=peer, device_id_type=MESH)
                cp.start()
                sends.append(cp)
        for k in range(1, N_DEV):
            peer, pidx = _peer(x, y, c, k)
            for a in range(n):
                pltpu.make_async_remote_copy(
                    src_ref=ins[a], dst_ref=outs[a].at[pidx], send_sem=send_sems.at[a, k - 1],
                    recv_sem=recv_sems.at[a, k - 1], device_id=peer, device_id_type=MESH).wait_recv()
        for cp in sends:
            cp.wait_send()
        for cp in locals_:
            cp.wait()

    return pl.pallas_call(
        body, name=name,
        out_shape=[SDS((N_DEV,) + a.shape, a.dtype) for a in arrays],
        in_specs=[pl.BlockSpec(memory_space=space)] * n,
        out_specs=[pl.BlockSpec(memory_space=space)] * n,
        scratch_shapes=[pltpu.SemaphoreType.DMA((n, N_DEV - 1)), pltpu.SemaphoreType.DMA((n, N_DEV - 1)),
                        pltpu.SemaphoreType.DMA((n,))],
        compiler_params=_params(),
    )(*arrays)


CHIP_K = (2, 4, 6)


def _all_gather_chips(arrays, name):
    n = len(arrays)
    k_y, k_x, k_d = CHIP_K

    def body(*refs):
        ins, outs = refs[:n], refs[n:2 * n]
        send_sems, recv_sems, local_sems = refs[2 * n:]
        x, y, c = lax.axis_index("x"), lax.axis_index("y"), lax.axis_index("c")
        me = 4 * x + 2 * y + c
        sib, sib_idx = _peer(x, y, c, 1)
        nbr_y, idx_y = _peer(x, y, c, k_y)
        nbr_x, idx_x = _peer(x, y, c, k_x)
        _, idx_d = _peer(x, y, c, k_d)

        def copy(a, slot, block, to, src=None):
            return pltpu.make_async_remote_copy(
                src_ref=outs[a].at[block] if src is None else src, dst_ref=outs[a].at[block],
                send_sem=send_sems.at[a, slot], recv_sem=recv_sems.at[a, slot], device_id=to, device_id_type=MESH)

        locals_ = [pltpu.make_async_copy(ins[a], outs[a].at[me], local_sems.at[a]) for a in range(n)]
        for cp in locals_:
            cp.start()
        for a in range(n):
            copy(a, 0, me, sib, src=ins[a]).start()
            copy(a, 1, me, nbr_y, src=ins[a]).start()
            copy(a, 2, me, nbr_x, src=ins[a]).start()

        def arrived(slot, block, frm, send_on_to=None):
            for a in range(n):
                copy(a, slot, block, frm).wait_recv()
                if send_on_to is not None:
                    copy(a, 3, block, send_on_to).start()
                copy(a, 3 + slot, block, sib).start()

        @pl.when(c == 0)
        def _():
            arrived(1, idx_y, nbr_y, send_on_to=nbr_x)
            arrived(2, idx_x, nbr_x)

        @pl.when(c == 1)
        def _():
            arrived(2, idx_x, nbr_x, send_on_to=nbr_y)
            arrived(1, idx_y, nbr_y)

        arrived(3, idx_d, nbr_x)
        for a in range(n):
            copy(a, 0, sib_idx, sib).wait_recv()
        for slot, k in ((4, k_y), (5, k_x), (6, k_d)):
            _, pidx = _peer(x, y, 1 - c, k)
            for a in range(n):
                copy(a, slot, pidx, sib).wait_recv()
        for slot in range(N_DEV - 1):
            for a in range(n):
                copy(a, slot, me, sib).wait_send()
        for cp in locals_:
            cp.wait()

    return pl.pallas_call(
        body, name=name,
        out_shape=[SDS((N_DEV,) + a.shape, a.dtype) for a in arrays],
        in_specs=[pl.BlockSpec(memory_space=pl.ANY)] * n,
        out_specs=[pl.BlockSpec(memory_space=pl.ANY)] * n,
        scratch_shapes=[pltpu.SemaphoreType.DMA((n, N_DEV - 1)), pltpu.SemaphoreType.DMA((n, N_DEV - 1)),
                        pltpu.SemaphoreType.DMA((n,))],
        compiler_params=_params(),
    )(*arrays)


def _gather_mm_in(h, arrays, name):
    n = len(arrays)
    s = h.shape[0]
    tm = 512
    nm = s // tm
    k_y, k_x, k_d = CHIP_K

    def body(*refs):
        h_ref, ins, proj_ref, outs = refs[0], refs[1:1 + n], refs[1 + n], refs[2 + n:2 + 2 * n]
        wbuf, obuf, send_sems, recv_sems, local_sems, wsem, osem = refs[2 + 2 * n:]
        x, y, c = lax.axis_index("x"), lax.axis_index("y"), lax.axis_index("c")
        me = 4 * x + 2 * y + c
        sib, sib_idx = _peer(x, y, c, 1)
        nbr_y, idx_y = _peer(x, y, c, k_y)
        nbr_x, idx_x = _peer(x, y, c, k_x)
        _, idx_d = _peer(x, y, c, k_d)

        def copy(a, slot, block, to, src=None):
            return pltpu.make_async_remote_copy(
                src_ref=outs[a].at[block] if src is None else src, dst_ref=outs[a].at[block],
                send_sem=send_sems.at[a, slot], recv_sem=recv_sems.at[a, slot], device_id=to, device_id_type=MESH)

        locals_ = [pltpu.make_async_copy(ins[a], outs[a].at[me], local_sems.at[a]) for a in range(n)]
        for cp in locals_:
            cp.start()
        for a in range(n):
            copy(a, 0, me, sib, src=ins[a]).start()
            copy(a, 1, me, nbr_y, src=ins[a]).start()
            copy(a, 2, me, nbr_x, src=ins[a]).start()

        def arrived(slot, block, frm, send_on_to=None):
            for a in range(n):
                copy(a, slot, block, frm).wait_recv()
                if send_on_to is not None:
                    copy(a, 3, block, send_on_to).start()
                copy(a, 3 + slot, block, sib).start()

        def from_sibling(slot, k):
            _, pidx = _peer(x, y, 1 - c, k)
            for a in range(n):
                copy(a, slot, pidx, sib).wait_recv()

        def out_copy(slot, m, pair):
            return pltpu.make_async_copy(
                obuf.at[slot], proj_ref.at[pl.ds(pl.multiple_of(m * tm, tm), tm),
                                           pl.ds(pl.multiple_of(pair * PAIR_W, LANE), PAIR_W)], osem.at[slot])

        def project(pair, first):
            loads = [pltpu.make_async_copy(outs[0].at[2 * pair + i], wbuf.at[i * SHARD_W:(i + 1) * SHARD_W], wsem.at[i])
                     for i in range(2)]
            for cp in loads:
                cp.start()
            for cp in loads:
                cp.wait()

            def step(m, _):
                slot = m % 2
                acc = _dot_nt(h_ref[pl.ds(pl.multiple_of(m * tm, tm), tm), :], wbuf[...])
                if first:
                    pl.when(m >= 2)(lambda: out_copy(slot, m, pair).wait())
                else:
                    out_copy(slot, m, pair).wait()
                obuf[slot] = acc
                out_copy(slot, m, pair).start()
                return 0
            lax.fori_loop(0, nm, step, 0)

        for a in range(n):
            copy(a, 0, sib_idx, sib).wait_recv()
        for cp in locals_:
            cp.wait()
        project(2 * x + y, first=True)

        @pl.when(c == 0)
        def _():
            arrived(1, idx_y, nbr_y, send_on_to=nbr_x)
            arrived(2, idx_x, nbr_x)

        @pl.when(c == 1)
        def _():
            arrived(2, idx_x, nbr_x, send_on_to=nbr_y)
            arrived(1, idx_y, nbr_y)

        from_sibling(4, k_y)
        project(2 * x + (1 - y), first=False)
        arrived(3, idx_d, nbr_x)
        from_sibling(5, k_x)
        project(2 * (1 - x) + y, first=False)
        from_sibling(6, k_d)
        project(2 * (1 - x) + (1 - y), first=False)
        for slot in range(2):
            out_copy(slot, 0, 0).wait()
        for slot in range(N_DEV - 1):
            for a in range(n):
                copy(a, slot, me, sib).wait_send()

    any_ = pl.BlockSpec(memory_space=pl.ANY)
    res = pl.pallas_call(
        body, name=name,
        out_shape=[SDS((s, IN_W), F32)] + [SDS((N_DEV,) + a.shape, a.dtype) for a in arrays],
        in_specs=[pl.BlockSpec(memory_space=pltpu.VMEM)] + [any_] * n,
        out_specs=[any_] * (1 + n),
        scratch_shapes=[pltpu.VMEM((PAIR_W, D_MODEL), BF16), pltpu.VMEM((2, tm, PAIR_W), F32),
                        pltpu.SemaphoreType.DMA((n, N_DEV - 1)), pltpu.SemaphoreType.DMA((n, N_DEV - 1)),
                        pltpu.SemaphoreType.DMA((n,)), pltpu.SemaphoreType.DMA((2,)), pltpu.SemaphoreType.DMA((2,))],
        compiler_params=_params(),
    )(h, *arrays)
    return res[0], res[1:]


def _exchange_sibling(arrays, name):
    n = len(arrays)
    ks = (0,) + CHIP_K

    def body(*refs):
        ins, outs = refs[:n], refs[n:2 * n]
        send_sems, recv_sems = refs[2 * n:]
        x, y, c = lax.axis_index("x"), lax.axis_index("y"), lax.axis_index("c")
        sib, sib_idx = _peer(x, y, c, 1)
        sends = []
        for i, k in enumerate(ks):
            _, tgt = _peer(x, y, 1 - c, k) if k else (None, sib_idx)
            for a in range(n):
                cp = pltpu.make_async_remote_copy(
                    src_ref=ins[a].at[tgt], dst_ref=outs[a].at[i], send_sem=send_sems.at[a, i],
                    recv_sem=recv_sems.at[a, i], device_id=sib, device_id_type=MESH)
                cp.start()
                sends.append(cp)
        for cp in sends:
            cp.wait_recv()
        for cp in sends:
            cp.wait_send()

    return pl.pallas_call(
        body, name=name,
        out_shape=[SDS((len(ks),) + a.shape[1:], a.dtype) for a in arrays],
        in_specs=[pl.BlockSpec(memory_space=pl.ANY)] * n,
        out_specs=[pl.BlockSpec(memory_space=pl.ANY)] * n,
        scratch_shapes=[pltpu.SemaphoreType.DMA((n, len(ks))), pltpu.SemaphoreType.DMA((n, len(ks)))],
        compiler_params=_params(),
    )(*arrays)


def _presum(mine, from_sib, me_arr, name):
    _, rows, cols = mine.shape
    tr = _row_tile(rows)
    ns = 1 + len(CHIP_K)

    def body(me_ref, a_ref, b_ref, o_ref):
        del me_ref
        o_ref[...] = (a_ref[...].astype(F32) + b_ref[...].astype(F32)).astype(o_ref.dtype)

    grid_spec = pltpu.PrefetchScalarGridSpec(
        num_scalar_prefetch=1, grid=(ns, rows // tr),
        in_specs=[pl.BlockSpec((1, tr, cols), lambda j, i, me: (jnp.bitwise_xor(me[0], 2 * j), i, 0)),
                  pl.BlockSpec((1, tr, cols), lambda j, i, me: (j, i, 0))],
        out_specs=pl.BlockSpec((1, tr, cols), lambda j, i, me: (j, i, 0)))
    return pl.pallas_call(body, name=name, grid_spec=grid_spec, out_shape=SDS((ns, rows, cols), mine.dtype),
                          compiler_params=_params())(me_arr, mine, from_sib)


HBM_SPEC = pl.BlockSpec(memory_space=pltpu.HBM)
SEM_SPEC = pl.BlockSpec(memory_space=pltpu.SEMAPHORE)
SIDE_EFFECT = pltpu.SideEffectType.DATAFLOW_SIDE_EFFECTING


def _chips_copies(pre_refs, land_refs, send_sems, recv_sems):
    x, y, c = lax.axis_index("x"), lax.axis_index("y"), lax.axis_index("c")
    copies = []
    for j, k in enumerate(CHIP_K):
        peer, _ = _peer(x, y, c, k)
        for a in range(len(pre_refs)):
            copies.append(pltpu.make_async_remote_copy(
                src_ref=pre_refs[a].at[1 + j], dst_ref=land_refs[a].at[j], send_sem=send_sems.at[a * len(CHIP_K) + j],
                recv_sem=recv_sems.at[a * len(CHIP_K) + j], device_id=peer, device_id_type=MESH))
    return copies


def _exchange_chips_start(presums, name):
    n = len(presums)

    def body(*refs):
        pre, land = refs[:n], refs[n:2 * n]
        send_sems, recv_sems = refs[2 * n], refs[2 * n + 1]
        token = refs[-1]
        for cp in _chips_copies(pre, land, send_sems, recv_sems):
            cp.start()
        token[...] = jnp.zeros_like(token)

    nk = len(CHIP_K)
    hbm = [pltpu.HBM(p.shape, p.dtype) for p in presums]
    hbm_land = [pltpu.HBM((nk,) + p.shape[1:], p.dtype) for p in presums]
    res = pl.pallas_call(
        body, name=name,
        out_shape=(pltpu.SemaphoreType.DMA((n * nk,)), pltpu.SemaphoreType.DMA((n * nk,)), *hbm, *hbm_land, SDS((8, LANE), F32)),
        in_specs=[HBM_SPEC] * (2 * n),
        out_specs=(SEM_SPEC, SEM_SPEC, *([HBM_SPEC] * (2 * n)), pl.BlockSpec(memory_space=pltpu.VMEM)),
        input_output_aliases={i: 2 + i for i in range(2 * n)},
        compiler_params=pltpu.CompilerParams(has_side_effects=SIDE_EFFECT),
    )(*[pltpu.with_memory_space_constraint(p, pltpu.HBM) for p in presums],
      *[pltpu.with_memory_space_constraint(lax.empty((nk,) + p.shape[1:], p.dtype), pltpu.HBM) for p in presums])
    return res[0], res[1], res[2:2 + n], res[2 + n:2 + 2 * n], res[-1]


def _exchange_chips_wait(send_sems, recv_sems, pre_thru, land_thru, after, name):
    n = len(pre_thru)

    def body(*refs):
        pre, land = refs[:n], refs[n:2 * n]
        s_sems, r_sems = refs[2 * n], refs[2 * n + 1]
        for cp in _chips_copies(pre, land, s_sems, r_sems):
            cp.wait_send()
            cp.wait_recv()

    hbm = [pltpu.HBM(p.shape, p.dtype) for p in (*pre_thru, *land_thru)]
    res = pl.pallas_call(
        body, name=name, out_shape=tuple(hbm),
        in_specs=[HBM_SPEC] * (2 * n) + [SEM_SPEC, SEM_SPEC, pl.BlockSpec(memory_space=pl.ANY)],
        out_specs=tuple([HBM_SPEC] * (2 * n)),
        input_output_aliases={i: i for i in range(2 * n)},
        compiler_params=pltpu.CompilerParams(has_side_effects=SIDE_EFFECT),
    )(*pre_thru, *land_thru, send_sems, recv_sems, after)
    return res[:n], res[n:]


def _exchange_chips(presums, name):
    n = len(presums)
    nk = len(CHIP_K)

    def body(*refs):
        pre, land = refs[:n], refs[n:2 * n]
        send_sems, recv_sems = refs[2 * n:]
        copies = _chips_copies(pre, land, send_sems, recv_sems)
        for cp in copies:
            cp.start()
        for cp in copies:
            cp.wait_recv()
        for cp in copies:
            cp.wait_send()

    return pl.pallas_call(
        body, name=name,
        out_shape=[SDS((nk,) + p.shape[1:], p.dtype) for p in presums],
        in_specs=[pl.BlockSpec(memory_space=pl.ANY)] * n,
        out_specs=[pl.BlockSpec(memory_space=pl.ANY)] * n,
        scratch_shapes=[pltpu.SemaphoreType.DMA((n * nk,)), pltpu.SemaphoreType.DMA((n * nk,))],
        compiler_params=_params(),
    )(*presums)


def _cast_bf16(w, name):
    def body(w_ref, o_ref):
        o_ref[...] = w_ref[...].astype(BF16)

    return pl.pallas_call(body, name=name, out_shape=SDS(w.shape, BF16), compiler_params=_params())(w)


def _cols_from_slots(wg, name):
    _, rows, cols = wg.shape

    def body(w_ref, o_ref):
        for j in range(N_DEV):
            o_ref[:, j * cols:(j + 1) * cols] = w_ref[j]

    return pl.pallas_call(body, name=name, out_shape=SDS((rows, N_DEV * cols), wg.dtype), compiler_params=_params())(wg)


def _ada_fwd(c_all, w_ada):
    def body(c_ref, w_ref, o_ref):
        cv = c_ref[...]
        sc = (cv * _sigmoid(cv)).astype(BF16)
        o_ref[...] = _dot(sc, w_ref[...].astype(BF16))

    return pl.pallas_call(body, name="ada_fwd", out_shape=SDS((N_DEV, w_ada.shape[1]), F32),
                          compiler_params=_params())(c_all, w_ada)


def _ada_bwd(c_all, d_ada_cols):
    def body(c_ref, d_ref, o_ref):
        cv = c_ref[...]
        sc = (cv * _sigmoid(cv)).astype(BF16)
        o_ref[...] = _dot_tn(sc, d_ref[...].astype(BF16))

    return pl.pallas_call(body, name="ada_bwd", out_shape=SDS((D_MODEL, d_ada_cols.shape[1]), F32),
                          compiler_params=_params())(c_all, d_ada_cols)


def _norm_fwd(x, norm_w, scale, shift):
    s = x.shape[0]
    tr = 512

    def body(x_ref, nw_ref, sc_ref, sh_ref, h_ref, ht_ref):
        xv = x_ref[...]
        r = lax.rsqrt(jnp.mean(xv * xv, axis=-1, keepdims=True) + EPS)
        h = (xv * r * nw_ref[...]) * (1.0 + sc_ref[...]) + sh_ref[...]
        h_ref[...] = h.astype(BF16)
        ht_ref[...] = h.T.astype(BF16)

    vec = pl.BlockSpec((1, D_MODEL), lambda i: (0, 0))
    return pl.pallas_call(
        body, name="norm_fwd", grid=(s // tr,),
        in_specs=[pl.BlockSpec((tr, D_MODEL), lambda i: (i, 0)), vec, vec, vec],
        out_specs=[pl.BlockSpec((tr, D_MODEL), lambda i: (i, 0)), pl.BlockSpec((D_MODEL, tr), lambda i: (0, i))],
        out_shape=[SDS((s, D_MODEL), BF16), SDS((D_MODEL, s), BF16)], compiler_params=_params(),
    )(x, norm_w, scale, shift)


def _mm_in(h, wt):
    s = h.shape[0]
    tm = 512

    def body(h_ref, w_ref, o_ref):
        o_ref[...] = _dot_nt(h_ref[...], w_ref[...])

    return pl.pallas_call(
        body, name="mm_in", grid=(IN_W // PAIR_W, s // tm),
        in_specs=[pl.BlockSpec((tm, D_MODEL), lambda p, m: (m, 0)),
                  pl.BlockSpec((PAIR_W, D_MODEL), lambda p, m: (p, 0))],
        out_specs=pl.BlockSpec((tm, PAIR_W), lambda p, m: (m, p)),
        out_shape=SDS((s, IN_W), F32), compiler_params=_params(),
    )(h, wt)


def _head_ones():
    a = lax.broadcasted_iota(jnp.int32, (LANE, LANE), 0) // HEAD_DIM
    b = lax.broadcasted_iota(jnp.int32, (LANE, LANE), 1) // HEAD_DIM
    return (a == b).astype(BF16)


def _head_sums(t, ones):
    return _dot(t.astype(BF16), ones)


def _band_bias(bias, transposed=False):
    qi = lax.broadcasted_iota(jnp.int32, (2 * BAND, 2 * BAND), 1 if transposed else 0) % BAND
    kj = lax.broadcasted_iota(jnp.int32, (2 * BAND, 2 * BAND), 0 if transposed else 1)
    dist = qi + BAND - kj
    valid = (dist >= 0) & (dist <= BAND)
    bias[1] = jnp.where(valid, 0.0, -1e30)
    bias[0] = jnp.where(valid & (kj >= BAND), 0.0, -1e30)


def _token_rows(j, d, chunk, per_r):
    return pl.ds(j // per_r + (j % per_r) * (chunk * d), chunk, stride=d)


def _deinterleave(src_ref, dst_ref, w_ref, ones, d, sub_len, chunk, scale, dst_off):
    per_r = sub_len // chunk

    def step(j, _):
        t = src_ref[_token_rows(j, d, chunk, per_r), :]
        if w_ref is not None:
            ms = _head_sums(t * t, ones) * (1.0 / HEAD_DIM)
            t = t * lax.rsqrt(ms + EPS) * (w_ref[...] * scale)
        dst_ref[pl.ds(pl.multiple_of(dst_off + j * chunk, BAND), chunk), :] = t.astype(dst_ref.dtype)
        return 0
    lax.fori_loop(0, d * per_r, step, 0, unroll=4)


N_PAIRS = ATTN_W // LANE


def _attn_fwd(proj, qw2, kw2):
    s = proj.shape[0]

    def group_body(g, step, q_ref, k_ref, v_ref, qw_ref, kw_ref, o_ref, l_ref, qn_ref, kn_ref, vn_ref,
                   qd, kd, vd, od, ld, bias):
        d = DILATIONS[g]
        sub_len = s // d
        nb = sub_len // BAND
        chunk = min(sub_len, 256)
        lo = lax.broadcasted_iota(jnp.int32, (1, LANE), 1) < HEAD_DIM
        ones = _head_ones()

        @pl.when(step == 0)
        def _():
            _band_bias(bias)

        kd[0:BAND, :] = jnp.zeros((BAND, LANE), BF16)
        vd[0:BAND, :] = jnp.zeros((BAND, LANE), BF16)
        _deinterleave(q_ref, qd, qw_ref, ones, d, sub_len, chunk, HEAD_DIM ** -0.5, 0)
        _deinterleave(k_ref, kd, kw_ref, ones, d, sub_len, chunk, 1.0, BAND)
        _deinterleave(v_ref, vd, None, ones, d, sub_len, chunk, 1.0, BAND)
        qn_ref[...] = qd[...]
        kn_ref[...] = kd[BAND:BAND + s, :]
        vn_ref[...] = vd[BAND:BAND + s, :]

        def block(t, _):
            base = pl.multiple_of(t * BAND, BAND)
            q = qd[pl.ds(base, BAND), :]
            k2 = kd[pl.ds(base, 2 * BAND), :]
            v2 = vd[pl.ds(base, 2 * BAND), :]
            zero = jnp.zeros_like(q)
            qs = jnp.concatenate([jnp.where(lo, q, zero), jnp.where(lo, zero, q)], axis=0)
            sc = _dot_nt(qs, k2) + bias[jnp.minimum(t % nb, 1)]
            m = jnp.max(sc, axis=-1, keepdims=True)
            p = jnp.exp(sc - m)
            den = jnp.sum(p, axis=-1, keepdims=True)
            u = _dot(p.astype(BF16), v2) * (1.0 / den)
            lse = m + jnp.log(den)
            od[pl.ds(base, BAND), :] = jnp.where(lo, u[:BAND], u[BAND:])
            ld[pl.ds(base, BAND), :] = jnp.where(lo, lse[:BAND], lse[BAND:])
            return 0
        lax.fori_loop(0, s // BAND, block, 0, unroll=16)

        per_r = sub_len // chunk

        def back(j, _):
            src = pl.ds(pl.multiple_of(j * chunk, chunk), chunk)
            dst = _token_rows(j, d, chunk, per_r)
            o_ref[dst, :] = od[src, :]
            l_ref[dst, :] = ld[src, :]
            return 0
        lax.fori_loop(0, d * per_r, back, 0, unroll=2)

    def body(*refs):
        step = pl.program_id(0)
        for g in range(N_GROUPS):
            pl.when(step // N_PAIRS == g)(functools.partial(group_body, g, step, *refs))

    col = lambda off: pl.BlockSpec((s, LANE), lambda i, off=off: (0, off // LANE + i))
    vec = pl.BlockSpec((1, LANE), lambda i: (0, 0))
    out = pl.BlockSpec((s, LANE), lambda i: (0, i))
    width = N_GROUPS * ATTN_W
    return pl.pallas_call(
        body, name="attn_fwd", grid=(N_GROUPS * N_PAIRS,),
        in_specs=[col(Q0), col(K0), col(V0), vec, vec], out_specs=[out] * 5,
        out_shape=[SDS((s, width), F32)] * 2 + [SDS((s, width), BF16)] * 3,
        scratch_shapes=[pltpu.VMEM((s, LANE), BF16), pltpu.VMEM((s + BAND, LANE), BF16), pltpu.VMEM((s + BAND, LANE), BF16),
                        pltpu.VMEM((s, LANE), F32), pltpu.VMEM((s, LANE), F32),
                        pltpu.VMEM((2, 2 * BAND, 2 * BAND), F32)],
        compiler_params=_params(),
    )(proj, proj, proj, qw2, kw2)


def _attn_bwd(proj, qn, kn, vn, da, lse_delta, qw2, kw2, dproj):
    s = proj.shape[0]
    n_steps = N_GROUPS * N_PAIRS

    def group_body(g, hp, q_ref, k_ref, qn_ref, kn_ref, vn_ref, da_ref, ld_ref, qw_ref, kw_ref, dp_in, dp_out,
                   dqw_ref, dkw_ref, kd, vd, kdt, dad, lst, dlt, dqt, dqd, dkd, dvd, st, stb, bias_t, wacc, sem):
        del dp_in
        d = DILATIONS[g]
        sub_len = s // d
        nb = sub_len // BAND
        chunk = min(sub_len, 256)
        lo = lax.broadcasted_iota(jnp.int32, (1, LANE), 1) < HEAD_DIM
        row_lo = lax.broadcasted_iota(jnp.int32, (LANE, 1), 0) < HEAD_DIM
        ones = _head_ones()
        per_r = sub_len // chunk
        cblk = chunk // BAND

        @pl.when(hp == 0)
        def _():
            _band_bias(bias_t, transposed=True)

        kd[0:BAND, :] = jnp.zeros((BAND, LANE), BF16)
        vd[0:BAND, :] = jnp.zeros((BAND, LANE), BF16)
        kdt[0] = jnp.zeros((LANE, BAND), BF16)
        kd[BAND:BAND + s, :] = kn_ref[...]
        vd[BAND:BAND + s, :] = vn_ref[...]

        def k_step(t, _):
            kdt[1 + t] = kn_ref[pl.ds(pl.multiple_of(t * BAND, BAND), BAND), :].astype(F32).T.astype(BF16)
            return 0
        lax.fori_loop(0, s // BAND, k_step, 0, unroll=4)
        _deinterleave(da_ref, dad, None, ones, d, sub_len, chunk, 1.0, 0)

        def rows_step(j, _):
            tok = _token_rows(j, d, chunk, per_r)
            tt = ld_ref[tok, :].T
            for u in range(cblk):
                cols = slice(u * BAND, (u + 1) * BAND)
                lst[j * cblk + u, 0:1, :] = tt[0:1, cols]
                lst[j * cblk + u, 1:2, :] = tt[HEAD_DIM:HEAD_DIM + 1, cols]
                dlt[j * cblk + u, 0:1, :] = tt[HEAD_DIM // 2:HEAD_DIM // 2 + 1, cols]
                dlt[j * cblk + u, 1:2, :] = tt[HEAD_DIM + HEAD_DIM // 2:HEAD_DIM + HEAD_DIM // 2 + 1, cols]
            return 0
        lax.fori_loop(0, d * per_r, rows_step, 0, unroll=4)

        def block(t, carry):
            ck, cv = carry
            base = pl.multiple_of(t * BAND, BAND)
            q = qn_ref[pl.ds(base, BAND), :]
            k2 = kd[pl.ds(base, 2 * BAND), :]
            v2 = vd[pl.ds(base, 2 * BAND), :]
            k2t = jnp.concatenate([kdt[t], kdt[t + 1]], axis=1)
            dav = dad[pl.ds(base, BAND), :]
            zero = jnp.zeros_like(q)
            qs = jnp.concatenate([jnp.where(lo, q, zero), jnp.where(lo, zero, q)], axis=0)
            das = jnp.concatenate([jnp.where(lo, dav, zero), jnp.where(lo, zero, dav)], axis=0)
            ls_row = jnp.concatenate([lst[t, 0:1, :], lst[t, 1:2, :]], axis=1)
            dl_row = jnp.concatenate([dlt[t, 0:1, :], dlt[t, 1:2, :]], axis=1)
            sc_t = _dot_nt(k2, qs) + bias_t[jnp.minimum(t % nb, 1)]
            p_t = jnp.exp(sc_t - ls_row)
            dp_t = _dot_nt(v2, das)
            ds_t = (p_t * (dp_t - dl_row)).astype(BF16)
            dv2 = _dot(p_t.astype(BF16), das)
            dk2 = _dot(ds_t, qs)
            dvd[pl.ds(base, BAND), :] = cv + dv2[:BAND]
            dkd[pl.ds(base, BAND), :] = ck + dk2[:BAND]
            dq_t = _dot(k2t, ds_t)
            dqt[t] = jnp.where(row_lo, dq_t[:, :BAND], dq_t[:, BAND:])
            return dk2[BAND:], dv2[BAND:]

        def blocks(i, carry):
            for u in range(BWD_UNROLL):
                carry = block(i * BWD_UNROLL + u, carry)
            return carry
        zeros = jnp.zeros((BAND, LANE), F32)
        ck, cv = lax.fori_loop(0, s // (BAND * BWD_UNROLL), blocks, (zeros, zeros))
        dkd[s:s + BAND, :] = ck
        dvd[s:s + BAND, :] = cv

        def dq_rows(t, _):
            dqd[pl.ds(pl.multiple_of(t * BAND, BAND), BAND), :] = dqt[t].T
            return 0
        lax.fori_loop(0, s // BAND, dq_rows, 0, unroll=4)

        def col_copy(slot, col0):
            return pltpu.make_async_copy(
                stb.at[slot], dp_out.at[:, pl.ds(pl.multiple_of(col0 + LANE * hp, LANE), LANE)], sem.at[slot])

        def store_cols(slot, col0):
            @pl.when(hp > 0)
            def _():
                col_copy(slot, col0).wait()
            stb[slot] = st[...].astype(BF16)
            col_copy(slot, col0).start()

        def norm_back(src_ref, dy_ref, dy_off, w_ref, scale, dw_ref, slot, col0):
            wacc[...] = jnp.zeros_like(wacc)

            def step(j, _):
                tok = _token_rows(j, d, chunk, per_r)
                t = src_ref[tok, :]
                dy = dy_ref[pl.ds(pl.multiple_of(dy_off + j * chunk, BAND), chunk), :]
                rr = lax.rsqrt(_head_sums(t * t, ones) * (1.0 / HEAD_DIM) + EPS)
                nrm = t * rr
                wacc[...] += jnp.sum((dy * nrm).reshape(chunk // 8, 8, LANE), axis=0)
                dn = dy * (w_ref[...] * scale)
                st[tok, :] = rr * (dn - nrm * (_head_sums(dn * nrm, ones) * (1.0 / HEAD_DIM)))
                return 0
            lax.fori_loop(0, d * per_r, step, 0, unroll=4)
            dw_ref[...] += jnp.broadcast_to(jnp.sum(wacc[...], axis=0, keepdims=True) * scale, dw_ref.shape)
            store_cols(slot, col0)

        @pl.when(hp == 0)
        def _():
            dqw_ref[...] = jnp.zeros_like(dqw_ref)
            dkw_ref[...] = jnp.zeros_like(dkw_ref)

        norm_back(q_ref, dqd, 0, qw_ref, HEAD_DIM ** -0.5, dqw_ref, 0, Q0)
        norm_back(k_ref, dkd, BAND, kw_ref, 1.0, dkw_ref, 1, K0)

        def v_back(j, _):
            src = pl.ds(pl.multiple_of(BAND + j * chunk, BAND), chunk)
            st[_token_rows(j, d, chunk, per_r), :] = dvd[src, :]
            return 0
        lax.fori_loop(0, d * per_r, v_back, 0, unroll=2)
        store_cols(2, V0)

        @pl.when(hp == n_steps - 1)
        def _():
            for slot, col0 in enumerate((Q0, K0, V0)):
                col_copy(slot, col0).wait()

    def body(*refs):
        step = pl.program_id(0)
        for g in range(N_GROUPS):
            pl.when(step // N_PAIRS == g)(functools.partial(group_body, g, step, *refs))

    col = lambda off: pl.BlockSpec((s, LANE), lambda i, off=off: (0, off // LANE + i))
    mid = pl.BlockSpec((s, LANE), lambda i: (0, i))
    slot4 = pl.BlockSpec((s, LANE), lambda i: (0, i % N_PAIRS))
    vec = pl.BlockSpec((1, LANE), lambda i: (0, 0))
    acc = pl.BlockSpec((8, LANE), lambda i: (0, 0))
    any_ = pl.BlockSpec(memory_space=pl.ANY)
    return pl.pallas_call(
        body, name="attn_bwd", grid=(n_steps,),
        in_specs=[col(Q0), col(K0), mid, mid, mid, slot4, slot4, vec, vec, any_],
        out_specs=[any_, acc, acc],
        out_shape=[SDS(dproj.shape, dproj.dtype), SDS((8, LANE), F32), SDS((8, LANE), F32)],
        input_output_aliases={9: 0},
        scratch_shapes=[pltpu.VMEM((s + BAND, LANE), BF16), pltpu.VMEM((s + BAND, LANE), BF16),
                        pltpu.VMEM((s // BAND + 1, LANE, BAND), BF16), pltpu.VMEM((s, LANE), BF16),
                        pltpu.VMEM((s // BAND, 8, BAND), F32), pltpu.VMEM((s // BAND, 8, BAND), F32),
                        pltpu.VMEM((s // BAND, LANE, BAND), F32),
                        pltpu.VMEM((s, LANE), F32), pltpu.VMEM((s + BAND, LANE), F32), pltpu.VMEM((s + BAND, LANE), F32),
                        pltpu.VMEM((s, LANE), F32), pltpu.VMEM((3, s, LANE), BF16),
                        pltpu.VMEM((2, 2 * BAND, 2 * BAND), F32), pltpu.VMEM((8, LANE), F32),
                        pltpu.SemaphoreType.DMA((3,))],
        compiler_params=_params(),
    )(proj, proj, qn, kn, vn, da, lse_delta, qw2, kw2, dproj)


def _tap_views(ext_ref, sh_ref, offsets, tr, cols):
    for b in range(8):
        group = [j for j, o in enumerate(offsets) if o % 8 == b]
        if not group:
            continue
        first = min(offsets[j] for j in group)
        span = tr + max(offsets[j] for j in group) - first
        sh_ref[0:span, cols] = ext_ref[first:first + span, cols]
        for j in group:
            yield j, sh_ref[offsets[j] - first:offsets[j] - first + tr, cols]


def _silu_grad(z, sg):
    return sg * (1.0 + z * (1.0 - sg))


def _glu(u):
    a_h, b_h = u[:, :CONV_W], u[:, CONV_W:]
    sg = _sigmoid(b_h)
    return a_h, sg, a_h * sg


def _tail(x, tgt, proj, o3, l3, wa, wc, wo, gate, bga, bgc, convw, convb, lnw, lnb, bd):
    s = x.shape[0]
    tr = 256

    def body(x_ref, t_ref, za_ref, u_ref, uh_ref, zc_ref, g0_ref, g1_ref, g2_ref, g3_ref,
             o0_ref, o1_ref, o2_ref, l0_ref, l1_ref, l2_ref, wa_ref, wc_ref, wo_ref,
             gate_ref, bga_ref, bgc_ref, cw_ref, cb_ref, lnw_ref, lnb_ref, bd_ref,
             dout_ref, da_ref, ld_ref, dcv_ref, mt_ref, yat_ref, yct_ref, dmo_ref, dya_ref, dyc_ref, dp_ref,
             dgate_ref, dbg_ref, dlnw_ref, dlnb_ref, dcb_ref, loss_ref,
             ext, sh, st_za, st_zc, st_g, sems):
        i = pl.program_id(0)

        @pl.when(i == 0)
        def _():
            for r in (dgate_ref, dbg_ref, dlnw_ref, dlnb_ref, dcb_ref, loss_ref):
                r[...] = jnp.zeros_like(r)

        def acc_rows(ref, v):
            ref[...] += jnp.broadcast_to(jnp.sum(v, axis=0, keepdims=True), ref.shape)

        la, lb, lc = l0_ref[...], l1_ref[...], l2_ref[...]
        mx = jnp.maximum(jnp.maximum(la, lb), lc)
        ea, eb, ec = jnp.exp(la - mx), jnp.exp(lb - mx), jnp.exp(lc - mx)
        den = ea + eb + ec
        inv = 1.0 / den
        attn = (ea * inv) * o0_ref[...] + (eb * inv) * o1_ref[...] + (ec * inv) * o2_ref[...]
        lse = mx + jnp.log(den)

        za = za_ref[...]
        sga = _sigmoid(za)
        sa = za * sga
        ya_in = attn * sa
        y_attn = _dot(ya_in.astype(BF16), wa_ref[...])

        _, _, glu = _glu(u_ref[...])
        _, _, glu_h = _glu(uh_ref[...])
        ext[0:CONV_HALO, :] = jnp.where(i > 0, glu_h, 0.0)
        ext[CONV_HALO:CONV_HALO + tr, :] = glu
        cv_blocks = []
        for cb in range(CONV_W // LANE):
            cols = slice(cb * LANE, (cb + 1) * LANE)
            cv_c = jnp.broadcast_to(cb_ref[:, cols], (tr, LANE))
            for j, rows in _tap_views(ext, sh, [CONV_HALO - (CONV_K - 1) + j for j in range(CONV_K)], tr, cols):
                cv_c = cv_c + cw_ref[j:j + 1, cols] * rows
            cv_blocks.append(cv_c)
        cv = jnp.concatenate(cv_blocks, axis=1)
        mu = jnp.mean(cv, axis=-1, keepdims=True)
        xc = cv - mu
        rstd = lax.rsqrt(jnp.mean(xc * xc, axis=-1, keepdims=True) + EPS)
        nrm = xc * rstd
        ln = nrm * lnw_ref[...] + lnb_ref[...]
        sgl = _sigmoid(ln)
        cs = ln * sgl
        zc = zc_ref[...]
        sgc = _sigmoid(zc)
        scz = zc * sgc
        yc_in = cs * scz
        y_conv = _dot(yc_in.astype(BF16), wc_ref[...])

        ga = _sigmoid(jnp.concatenate([g0_ref[...], g1_ref[...]], axis=1) + bga_ref[...])
        gc = _sigmoid(jnp.concatenate([g2_ref[...], g3_ref[...]], axis=1) + bgc_ref[...])
        merged = ga * y_attn + gc * y_conv
        mo = _dot(merged.astype(BF16), wo_ref[...])
        gate_v = gate_ref[...]
        err = (x_ref[...] + gate_v * mo) - t_ref[...]
        loss_ref[...] += 0.5 * jnp.sum(jnp.mean(err * err, axis=-1, keepdims=True))
        d_out = err * (1.0 / D_MODEL)
        dout_ref[...] = d_out

        rows = pl.ds(pl.multiple_of(i * tr, tr), tr)
        cps = [pltpu.make_async_copy(st_za, dp_ref.at[rows, pl.ds(ZA0, ATTN_W)], sems.at[0]),
               pltpu.make_async_copy(st_zc, dp_ref.at[rows, pl.ds(ZC0, CONV_W)], sems.at[1]),
               pltpu.make_async_copy(st_g, dp_ref.at[rows, pl.ds(G0, 2 * D_MODEL)], sems.at[2])]

        @pl.when(i > 0)
        def _():
            for cp in cps:
                cp.wait()

        acc_rows(dgate_ref, d_out * mo)
        dmo_b = (d_out * gate_v).astype(BF16)
        dmo_ref[...] = dmo_b
        mt_ref[...] = merged.T.astype(BF16)
        d_merged = _dot_nt(dmo_b, wo_ref[...])
        d_ya = (d_merged * ga).astype(BF16)
        d_yc = (d_merged * gc).astype(BF16)
        dya_ref[...] = d_ya
        dyc_ref[...] = d_yc
        dga = d_merged * y_attn * (ga * (1.0 - ga))
        dgc = d_merged * y_conv * (gc * (1.0 - gc))
        dgs = jnp.concatenate([dga, dgc], axis=1)
        acc_rows(dbg_ref, dgs)
        st_g[...] = dgs.astype(BF16)

        yat_ref[...] = ya_in.T.astype(BF16)
        d_ya_in = _dot_nt(d_ya, wa_ref[...])
        d_attn = d_ya_in * sa
        da_ref[...] = d_attn
        st_za[...] = (d_ya_in * attn * _silu_grad(za, sga)).astype(BF16)
        prod = d_attn * attn
        hi = prod.astype(BF16)
        lo_ = (prod - hi.astype(F32)).astype(BF16)
        delta = _dot(hi, bd_ref[...]) + _dot(lo_, bd_ref[...])
        first_half = (lax.broadcasted_iota(jnp.int32, (1, ATTN_W), 1) % HEAD_DIM) < HEAD_DIM // 2
        ld_ref[...] = jnp.where(first_half, lse, delta)

        yct_ref[...] = yc_in.T.astype(BF16)
        d_yc_in = _dot_nt(d_yc, wc_ref[...])
        st_zc[...] = (d_yc_in * cs * _silu_grad(zc, sgc)).astype(BF16)
        d_ln = (d_yc_in * scz) * _silu_grad(ln, sgl)
        acc_rows(dlnw_ref, d_ln * nrm)
        acc_rows(dlnb_ref, d_ln)
        d_nrm = d_ln * lnw_ref[...]
        d_cv = rstd * (d_nrm - jnp.mean(d_nrm, axis=-1, keepdims=True)
                       - nrm * jnp.mean(d_nrm * nrm, axis=-1, keepdims=True))
        acc_rows(dcb_ref, d_cv)
        dcv_ref[...] = d_cv

        for cp in cps:
            cp.start()

        @pl.when(i == s // tr - 1)
        def _():
            for cp in cps:
                cp.wait()

    def rows(width, colblk=0):
        return pl.BlockSpec((tr, width), lambda i, colblk=colblk: (i, colblk))

    def const(shape):
        return pl.BlockSpec(shape, lambda i: (0,) * len(shape))

    halo = pl.BlockSpec((CONV_HALO, D_MODEL), lambda i: (jnp.maximum(i * (tr // CONV_HALO) - 1, 0), U0 // D_MODEL))
    in_specs = [rows(D_MODEL), rows(D_MODEL), rows(ATTN_W, ZA0 // ATTN_W), rows(D_MODEL, U0 // D_MODEL), halo,
                rows(CONV_W, ZC0 // CONV_W)]
    in_specs += [rows(512, G0 // 512 + j) for j in range(4)]
    in_specs += [rows(ATTN_W, g) for g in range(N_GROUPS)] * 2
    in_specs += [const(wa.shape), const(wc.shape), const(wo.shape), const((1, D_MODEL)), const((1, D_MODEL)),
                 const((1, D_MODEL)), const(convw.shape), const((1, CONV_W)), const((1, CONV_W)), const((1, CONV_W)),
                 const(bd.shape)]
    tcol = lambda width: pl.BlockSpec((width, tr), lambda i: (0, i))
    out_specs = [rows(D_MODEL), rows(ATTN_W), rows(ATTN_W), rows(CONV_W),
                 tcol(D_MODEL), tcol(ATTN_W), tcol(CONV_W), rows(D_MODEL), rows(D_MODEL), rows(D_MODEL),
                 pl.BlockSpec(memory_space=pl.ANY),
                 const((8, D_MODEL)), const((8, 2 * D_MODEL)), const((8, CONV_W)), const((8, CONV_W)), const((8, CONV_W)),
                 const((8, LANE))]
    out_shape = [SDS((s, D_MODEL), F32), SDS((s, ATTN_W), F32), SDS((s, ATTN_W), F32),
                 SDS((s, CONV_W), F32),
                 SDS((D_MODEL, s), BF16), SDS((ATTN_W, s), BF16), SDS((CONV_W, s), BF16),
                 SDS((s, D_MODEL), BF16), SDS((s, D_MODEL), BF16), SDS((s, D_MODEL), BF16),
                 SDS((s, IN_W), BF16),
                 SDS((8, D_MODEL), F32), SDS((8, 2 * D_MODEL), F32), SDS((8, CONV_W), F32), SDS((8, CONV_W), F32),
                 SDS((8, CONV_W), F32), SDS((8, LANE), F32)]
    return pl.pallas_call(
        body, name="tail", grid=(s // tr,), in_specs=in_specs, out_specs=out_specs, out_shape=out_shape,
        scratch_shapes=[pltpu.VMEM((CONV_HALO + tr, CONV_W), F32), pltpu.VMEM((CONV_HALO + tr, CONV_W), F32),
                        pltpu.VMEM((tr, ATTN_W), BF16),
                        pltpu.VMEM((tr, CONV_W), BF16), pltpu.VMEM((tr, 2 * D_MODEL), BF16),
                        pltpu.SemaphoreType.DMA((3,))],
        compiler_params=_params(),
    )(x, tgt, proj, proj, proj, proj, proj, proj, proj, proj, *o3, *l3, wa, wc, wo, gate, bga, bgc,
      convw, convb, lnw, lnb, bd)


def _conv_bwd(dcv, proj, convw, dproj):
    s = dcv.shape[0]
    tr = 128
    nt = s // tr

    def body(dcv_ref, dcvn_ref, u_ref, uh_ref, cw_ref, dp_in, dp_out, dw_ref, extg, extd, sh):
        del dp_in
        i = pl.program_id(0)

        @pl.when(i == 0)
        def _():
            dw_ref[...] = jnp.zeros_like(dw_ref)

        _, _, glu = _glu(u_ref[...])
        _, _, glu_h = _glu(uh_ref[...])
        extg[0:CONV_HALO, :] = jnp.where(i > 0, glu_h, 0.0)
        extg[CONV_HALO:CONV_HALO + tr, :] = glu
        extd[0:tr, :] = dcv_ref[...]
        extd[tr:tr + CONV_HALO, :] = jnp.where(i < nt - 1, dcvn_ref[...], 0.0)
        for cb in range(CONV_W // LANE):
            cols = slice(cb * LANE, (cb + 1) * LANE)
            dglu = jnp.zeros((tr, LANE), F32)
            for j, rows in _tap_views(extd, sh, [CONV_K - 1 - j for j in range(CONV_K)], tr, cols):
                dglu = dglu + cw_ref[j:j + 1, cols] * rows
            dcv_c = dcv_ref[:, cols]
            for j, rows in _tap_views(extg, sh, [CONV_HALO - (CONV_K - 1) + j for j in range(CONV_K)], tr, cols):
                dw_ref[8 * j:8 * j + 8, cols] += jnp.sum((dcv_c * rows).reshape(tr // 8, 8, LANE), axis=0)
            a_h = u_ref[:, cols]
            sgb = _sigmoid(u_ref[:, CONV_W + cb * LANE:CONV_W + (cb + 1) * LANE])
            dp_out[:, cols] = (dglu * sgb).astype(BF16)
            dp_out[:, CONV_W + cb * LANE:CONV_W + (cb + 1) * LANE] = (dglu * a_h * (sgb * (1.0 - sgb))).astype(BF16)

    ucol = U0 // D_MODEL
    return pl.pallas_call(
        body, name="conv_bwd", grid=(nt,),
        in_specs=[pl.BlockSpec((tr, CONV_W), lambda i: (i, 0)),
                  pl.BlockSpec((CONV_HALO, CONV_W), lambda i: (jnp.minimum((i + 1) * (tr // CONV_HALO), s // CONV_HALO - 1), 0)),
                  pl.BlockSpec((tr, D_MODEL), lambda i: (i, ucol)),
                  pl.BlockSpec((CONV_HALO, D_MODEL), lambda i: (jnp.maximum(i * (tr // CONV_HALO) - 1, 0), ucol)),
                  pl.BlockSpec(convw.shape, lambda i: (0, 0)),
                  pl.BlockSpec(memory_space=pl.ANY)],
        out_specs=[pl.BlockSpec((tr, D_MODEL), lambda i: (i, ucol)), pl.BlockSpec((8 * CONV_HALO, CONV_W), lambda i: (0, 0))],
        out_shape=[SDS(dproj.shape, dproj.dtype), SDS((8 * CONV_HALO, CONV_W), F32)],
        input_output_aliases={5: 0},
        scratch_shapes=[pltpu.VMEM((CONV_HALO + tr, CONV_W), F32)] * 3,
        compiler_params=_params(),
    )(dcv, dcv, proj, proj, convw, dproj)


def _mm_acc(at, b, name, col_slots):
    m, s = at.shape
    n = b.shape[1]
    tk = 512
    nk = s // tk

    def body(a_ref, b_ref, o_ref, acc):
        k = pl.program_id(0)

        @pl.when(k == 0)
        def _():
            acc[...] = jnp.zeros_like(acc)

        acc[...] += _dot(a_ref[...], b_ref[...])

        @pl.when(k == nk - 1)
        def _():
            if col_slots:
                w = n // N_DEV
                for j in range(N_DEV):
                    o_ref[j] = acc[:, j * w:(j + 1) * w].astype(BF16)
            else:
                o_ref[...] = acc[...].astype(BF16)

    if col_slots:
        out_shape = SDS((N_DEV, m, n // N_DEV), BF16)
        out_spec = pl.BlockSpec((N_DEV, m, n // N_DEV), lambda k: (0, 0, 0))
    else:
        out_shape = SDS((m, n), BF16)
        out_spec = pl.BlockSpec((m, n), lambda k: (0, 0))
    return pl.pallas_call(
        body, name=name, grid=(nk,),
        in_specs=[pl.BlockSpec((m, tk), lambda k: (0, k)), pl.BlockSpec((tk, n), lambda k: (k, 0))],
        out_specs=out_spec, out_shape=out_shape, scratch_shapes=[pltpu.VMEM((m, n), F32)],
        compiler_params=_params(),
    )(at, b)


def _mm_dw(ht, dproj):
    s = ht.shape[1]
    tk = 512
    nk = s // tk

    def body(a_ref, b_ref, o_ref, acc):
        k = pl.program_id(1)

        @pl.when(k == 0)
        def _():
            acc[...] = jnp.zeros_like(acc)

        acc[...] += _dot(a_ref[...], b_ref[...])

        @pl.when(k == nk - 1)
        def _():
            o_ref[...] = acc[...].T.astype(BF16)

    return pl.pallas_call(
        body, name="mm_dw", grid=(IN_W // PAIR_W, nk),
        in_specs=[pl.BlockSpec((D_MODEL, tk), lambda p, k: (0, k)), pl.BlockSpec((tk, PAIR_W), lambda p, k: (k, p))],
        out_specs=pl.BlockSpec((PAIR_W, D_MODEL), lambda p, k: (p, 0)),
        out_shape=SDS((IN_W, D_MODEL), BF16), scratch_shapes=[pltpu.VMEM((D_MODEL, PAIR_W), F32)],
        compiler_params=_params(),
    )(ht, dproj)


def _mm_dh_norm_bwd(dproj, wt, x, dout, norm_w, scale, token):
    s = dproj.shape[0]
    tm = 1024
    n_p = IN_W // PAIR_W

    def body(dp_ref, w_ref, x_ref, do_ref, nw_ref, sc_ref, tok_ref, gx_ref, dsh_ref, dsc_ref, dnw_ref, dh_acc):
        del tok_ref
        m, p = pl.program_id(0), pl.program_id(1)
        part = _dot(dp_ref[...], w_ref[...])

        @pl.when(p == 0)
        def _():
            dh_acc[...] = part

        @pl.when(p > 0)
        def _():
            dh_acc[...] += part

        @pl.when((m == 0) & (p == 0))
        def _():
            for r in (dsh_ref, dsc_ref, dnw_ref):
                r[...] = jnp.zeros_like(r)

        @pl.when(p == n_p - 1)
        def _():
            def acc_rows(ref, v):
                ref[...] += jnp.broadcast_to(jnp.sum(v, axis=0, keepdims=True), ref.shape)

            xv = x_ref[...]
            dh_v = dh_acc[...]
            r = lax.rsqrt(jnp.mean(xv * xv, axis=-1, keepdims=True) + EPS)
            xn = xv * r
            one_sc = 1.0 + sc_ref[...]
            acc_rows(dsh_ref, dh_v)
            acc_rows(dsc_ref, dh_v * (xn * nw_ref[...]))
            acc_rows(dnw_ref, dh_v * xn * one_sc)
            dxn = dh_v * (nw_ref[...] * one_sc)
            gx_ref[...] = do_ref[...] + r * (dxn - xn * jnp.mean(dxn * xn, axis=-1, keepdims=True))

    rows = pl.BlockSpec((tm, D_MODEL), lambda m, p: (m, 0))
    vec = pl.BlockSpec((1, D_MODEL), lambda m, p: (0, 0))
    acc = pl.BlockSpec((8, D_MODEL), lambda m, p: (0, 0))
    return pl.pallas_call(
        body, name="mm_dh_norm_bwd", grid=(s // tm, n_p),
        in_specs=[pl.BlockSpec((tm, PAIR_W), lambda m, p: (m, p)),
                  pl.BlockSpec((PAIR_W, D_MODEL), lambda m, p: (p, 0)),
                  rows, rows, vec, vec, pl.BlockSpec(token.shape, lambda m, p: (0, 0))],
        out_specs=[rows, acc, acc, acc],
        out_shape=[SDS((s, D_MODEL), F32)] + [SDS((8, D_MODEL), F32)] * 3,
        scratch_shapes=[pltpu.VMEM((tm, D_MODEL), F32)], compiler_params=_params(),
    )(dproj, wt, x, dout, norm_w, scale, token)


SMALL_ROWS = 8
QN_COL, KN_COL, CB_COL, LOSS_COL = 0, LANE, 2 * LANE, 2 * LANE + CONV_W


def _pack_partials(dsh, dsc, dgate, dnw, dbg, dqw3, dkw3, dcb, dlnw, dlnb, loss_p):
    n3 = len(dqw3)

    def body(*refs):
        dsh_r, dsc_r, dgate_r, dnw_r, dbg_r = refs[:5]
        dq_r, dk_r = refs[5:5 + n3], refs[5 + n3:5 + 2 * n3]
        dcb_r, dlnw_r, dlnb_r, loss_r, o_ref = refs[5 + 2 * n3:]

        def both_heads(rs):
            t = rs[0][0:1, :]
            for r in rs[1:]:
                t = t + r[0:1, :]
            return t + pltpu.roll(t, HEAD_DIM, axis=1)

        o_ref[0:1, :] = dsh_r[0:1, :]
        o_ref[1:2, :] = dsc_r[0:1, :]
        o_ref[2:3, :] = dgate_r[0:1, :]
        o_ref[3:4, :] = dnw_r[0:1, :]
        o_ref[4:5, :] = dbg_r[0:1, 0:D_MODEL]
        o_ref[5:6, :] = dbg_r[0:1, D_MODEL:]
        o_ref[6:7, QN_COL:QN_COL + LANE] = both_heads(dq_r)
        o_ref[6:7, KN_COL:KN_COL + LANE] = both_heads(dk_r)
        o_ref[6:7, CB_COL:CB_COL + CONV_W] = dcb_r[0:1, :]
        o_ref[6:7, LOSS_COL:LOSS_COL + LANE] = loss_r[0:1, :]
        o_ref[6:7, LOSS_COL + LANE:] = jnp.zeros((1, D_MODEL - LOSS_COL - LANE), F32)
        o_ref[7:8, 0:CONV_W] = dlnw_r[0:1, :]
        o_ref[7:8, CONV_W:] = dlnb_r[0:1, :]

    return pl.pallas_call(body, name="pack_partials", out_shape=SDS((SMALL_ROWS, D_MODEL), F32),
                          compiler_params=_params())(dsh, dsc, dgate, dnw, dbg, *dqw3, *dkw3, dcb, dlnw, dlnb, loss_p)


def _adamw_update(g, w, m, v):
    bc1 = 1.0 - ADAM_B1 ** ADAM_STEP
    bc2 = 1.0 - ADAM_B2 ** ADAM_STEP
    m_new = ADAM_B1 * m + (1.0 - ADAM_B1) * g
    v_new = ADAM_B2 * v + (1.0 - ADAM_B2) * (g * g)
    delta = -ADAM_LR * ((m_new / bc1) / (jnp.sqrt(v_new / bc2) + ADAM_EPS) + ADAM_WD * w)
    return delta, m_new, v_new


def _adamw_small(small_all, ws, ms, vs):
    n = len(ws)
    where = [(slice(0, 3), None), (slice(3, 4), None), (slice(4, 6), None), (6, QN_COL), (6, KN_COL), (6, CB_COL),
             (7, 0), (7, CONV_W)]

    def body(*refs):
        g_ref = refs[0]
        w_r, m_r, v_r = refs[1:1 + n], refs[1 + n:1 + 2 * n], refs[1 + 2 * n:1 + 3 * n]
        outs = refs[1 + 3 * n:]
        g_o, d_o, m_o, v_o, loss_o = outs[:n], outs[n:2 * n], outs[2 * n:3 * n], outs[3 * n:4 * n], outs[4 * n]
        gsum = g_ref[0]
        for dev in range(1, N_DEV):
            gsum = gsum + g_ref[dev]
        loss_o[...] = gsum[6:7, LOSS_COL:LOSS_COL + LANE]
        for i, (rows, col) in enumerate(where):
            width = w_r[i].shape[1]
            if col is None:
                g = jnp.concatenate([gsum[r:r + 1, :] for r in range(rows.start, rows.stop)], axis=1)
            else:
                g = gsum[rows:rows + 1, col:col + width]
            delta, m_new, v_new = _adamw_update(g, w_r[i][...], m_r[i][...], v_r[i][...])
            g_o[i][...] = g
            d_o[i][...] = delta
            m_o[i][...] = m_new
            v_o[i][...] = v_new

    shapes = [SDS(w.shape, F32) for w in ws]
    res = pl.pallas_call(body, name="adamw_small", out_shape=shapes * 4 + [SDS((1, LANE), F32)],
                         compiler_params=_params())(small_all, *ws, *ms, *vs)
    return [res[k * n:(k + 1) * n] for k in range(4)], res[4 * n]


def _row_tile(rows):
    if rows <= 128:
        return rows
    return 128 if rows % 128 == 0 else SHARD_W // 4


def _adamw(gsrc, w, m, v, name, stacked):
    rows, cols = w.shape
    tr = _row_tile(rows)
    n_src = len(gsrc) if stacked else 1

    def body(*refs):
        g_refs, (w_ref, m_ref, v_ref, go_ref, d_ref, mo_ref, vo_ref) = refs[:n_src], refs[n_src:]
        if stacked:
            g = None
            for g_ref, (_, slots) in zip(g_refs, gsrc):
                for j in range(slots):
                    t = g_ref[j].astype(F32)
                    g = t if g is None else g + t
        else:
            g = g_refs[0][...]
        delta, m_new, v_new = _adamw_update(g, w_ref[...], m_ref[...], v_ref[...])
        go_ref[...] = g
        d_ref[...] = delta
        mo_ref[...] = m_new
        vo_ref[...] = v_new

    blk = pl.BlockSpec((tr, cols), lambda i: (i, 0))
    if stacked:
        gspecs = [pl.BlockSpec((slots, tr, arr.shape[2]), lambda i: (0, i, 0)) for arr, slots in gsrc]
        gargs = [arr for arr, _ in gsrc]
    else:
        gspecs, gargs = [blk], [gsrc]
    in_specs = gspecs + [blk, blk, blk]
    args = gargs + [w, m, v]
    return pl.pallas_call(
        body, name=name, grid=(rows // tr,), in_specs=in_specs, out_specs=[blk] * 4,
        out_shape=[SDS((rows, cols), F32)] * 4, compiler_params=_params(),
    )(*args)


def kernel(x, c, w_ada, b_ada, norm_w, w_in, b_gate, q_norm_w, k_norm_w, w_attn_proj, conv_w, conv_b, conv_ln_w, conv_ln_b, w_conv_proj, w_out, loss_target, m_w_ada, m_b_ada, m_norm_w, m_w_in, m_b_gate, m_q_norm_w, m_k_norm_w, m_w_attn_proj, m_conv_w, m_conv_b, m_conv_ln_w, m_conv_ln_b, m_w_conv_proj, m_w_out, v_w_ada, v_b_ada, v_norm_w, v_w_in, v_b_gate, v_q_norm_w, v_k_norm_w, v_w_attn_proj, v_conv_w, v_conv_b, v_conv_ln_w, v_conv_ln_b, v_w_conv_proj, v_w_out):
    xi, yi, ci = lax.axis_index("x"), lax.axis_index("y"), lax.axis_index("c")
    me = 4 * xi + 2 * yi + ci
    x2, tgt2 = x[0], loss_target[0]
    w_in_t, m_w_in_t, v_w_in_t = (jnp.transpose(a[0]) for a in (w_in, m_w_in, v_w_in))
    s = x2.shape[0]

    cw_flat = jnp.pad(conv_w[0].reshape(1, -1), ((0, 0), (0, CONVW_FLAT - CONV_K * HEAD_DIM)))
    pre = jnp.concatenate([c, cw_flat], axis=1).reshape(8, -1)
    (pre_all,) = _all_gather([pre], "gather_c_convw", vmem=True)
    pre_all = pre_all.reshape(N_DEV, -1)
    c_all = pre_all[:, :D_MODEL]
    convw_full = pre_all[:, D_MODEL:D_MODEL + CONV_K * HEAD_DIM].reshape(N_DEV, CONV_K, HEAD_DIM)
    convw_full = jnp.transpose(convw_full, (1, 0, 2)).reshape(CONV_K, CONV_W)
    convw_pad = jnp.pad(convw_full, ((0, CONV_HALO - CONV_K), (0, 0)))

    ada_part = _ada_fwd(c_all, w_ada[0])
    (ada_all,) = _all_gather([ada_part], "gather_ada", vmem=True)
    ada = lax.dynamic_index_in_dim(ada_all, me, axis=1, keepdims=False).reshape(1, 3 * D_MODEL) + b_ada
    shift, scale, gate = ada[:, :D_MODEL], ada[:, D_MODEL:2 * D_MODEL], ada[:, 2 * D_MODEL:]

    h, ht = _norm_fwd(x2, norm_w, scale, shift)
    proj, (wt_g, wa_g, wc_g, wo_g) = _gather_mm_in(
        h, [_cast_bf16(w_in_t, "cast_win"), _cast_bf16(w_attn_proj[0], "cast_wa"), _cast_bf16(w_conv_proj[0], "cast_wc"),
            _cast_bf16(w_out[0], "cast_wo")], "gather_mm_in")
    wt = wt_g.reshape(IN_W, D_MODEL)
    wa = _cols_from_slots(wa_g, "cols_wa")
    wc = _cols_from_slots(wc_g, "cols_wc")
    wo = wo_g.reshape(D_MODEL, D_MODEL)
    qw2 = jnp.tile(q_norm_w, (1, 2))
    kw2 = jnp.tile(k_norm_w, (1, 2))
    o_all, l_all, qn, kn, vn = _attn_fwd(proj, qw2, kw2)
    o3, l3 = [o_all] * N_GROUPS, [l_all] * N_GROUPS
    head_id = jnp.arange(ATTN_W) // HEAD_DIM
    bd = (head_id[:, None] == head_id[None, :]).astype(BF16)
    (dout, da, lse_delta, dcv, mt, yat, yct, dmo, dya, dyc, dproj,
     dgate, dbg, dlnw, dlnb, dcb, loss_p) = _tail(
        x2, tgt2, proj, o3, l3, wa, wc, wo, gate, b_gate[:, :D_MODEL], b_gate[:, D_MODEL:], convw_pad,
        conv_b, conv_ln_w, conv_ln_b, bd)

    dproj, dconvw8 = _conv_bwd(dcv, proj, convw_pad, dproj)
    dconvw = jnp.sum(dconvw8.reshape(CONV_HALO, 8, CONV_W), axis=1)
    dproj, dqw_all, dkw_all = _attn_bwd(proj, qn, kn, vn, da, lse_delta, qw2, kw2, dproj)
    dqw_g3, dkw_g3 = [dqw_all], [dkw_all]
    dw_in_p = _mm_dw(ht, dproj).reshape(N_DEV, SHARD_W, D_MODEL)
    dwo_p = _mm_acc(mt, dmo, "mm_dwo", col_slots=False).reshape(N_DEV, D_MODEL // N_DEV, D_MODEL)
    dwa_p = _mm_acc(yat, dya, "mm_dwa", col_slots=True)
    dwc_p = _mm_acc(yct, dyc, "mm_dwc", col_slots=True)

    partials = [dw_in_p, dwa_p, dwc_p, dwo_p]
    me_arr = jnp.reshape(me, (1,)).astype(jnp.int32)
    from_sib = _exchange_sibling(partials, "exchange_sibling")
    presums = [_presum(p, f, me_arr, f"presum{i}") for i, (p, f) in enumerate(zip(partials, from_sib))]
    s_sems, r_sems, pre_thru, land_thru, token = _exchange_chips_start(presums, "exchange_chips_start")
    gx, dsh, dsc, dnw = _mm_dh_norm_bwd(dproj, wt, x2, dout, norm_w, scale, token)
    small_p = _pack_partials(dsh, dsc, dgate, dnw, dbg, dqw_g3, dkw_g3, dcb, dlnw, dlnb, loss_p)
    small_all, dconvw_all = _all_gather([small_p, dconvw], "gather_small", vmem=True)

    small_w = (b_ada, norm_w, b_gate, q_norm_w, k_norm_w, conv_b, conv_ln_w, conv_ln_b)
    small_m = (m_b_ada, m_norm_w, m_b_gate, m_q_norm_w, m_k_norm_w, m_conv_b, m_conv_ln_w, m_conv_ln_b)
    small_v = (v_b_ada, v_norm_w, v_b_gate, v_q_norm_w, v_k_norm_w, v_conv_b, v_conv_ln_w, v_conv_ln_b)
    r_small, loss_row = _adamw_small(small_all, small_w, small_m, small_v)
    dcw_mine = lax.dynamic_slice_in_dim(dconvw_all[:, :CONV_K, :], me * HEAD_DIM, HEAD_DIM, axis=2)
    r_convw = _adamw([(dcw_mine, N_DEV)], conv_w[0], m_conv_w[0], v_conv_w[0], "adamw_conv_w", stacked=True)

    d_ada_all = small_all[:, 0:3, :].reshape(N_DEV, 3 * D_MODEL)
    d_ada_cols = lax.dynamic_slice_in_dim(d_ada_all, me * (3 * D_MODEL // N_DEV), 3 * D_MODEL // N_DEV, axis=1)
    g_wada = _ada_bwd(c_all, d_ada_cols)
    r_ada = _adamw(g_wada, w_ada[0], m_w_ada[0], v_w_ada[0], "adamw_w_ada", stacked=False)
    pres, lands = _exchange_chips_wait(s_sems, r_sems, pre_thru, land_thru, r_ada[1], "exchange_chips_wait")
    terms = [[(p, 1), (l, len(CHIP_K))] for p, l in zip(pres, lands)]
    r_win = [jnp.transpose(r) for r in _adamw(terms[0], w_in_t, m_w_in_t, v_w_in_t, "adamw_w_in", stacked=True)]
    r_wap = _adamw(terms[1], w_attn_proj[0], m_w_attn_proj[0], v_w_attn_proj[0], "adamw_w_attn_proj", stacked=True)
    r_wcp = _adamw(terms[2], w_conv_proj[0], m_w_conv_proj[0], v_w_conv_proj[0], "adamw_w_conv_proj", stacked=True)
    r_wout = _adamw(terms[3], w_out[0], m_w_out[0], v_w_out[0], "adamw_w_out", stacked=True)

    outs = [loss_row[0, 0], gx[None]]
    for k in range(4):
        b_ada_k, norm_w_k, b_gate_k, qn_k, kn_k, conv_b_k, ln_w_k, ln_b_k = r_small[k]
        outs += [r_ada[k][None], b_ada_k, norm_w_k, r_win[k][None], b_gate_k, qn_k, kn_k, r_wap[k][None],
                 r_convw[k][None], conv_b_k, ln_w_k, ln_b_k, r_wcp[k][None], r_wout[k][None]]
    return tuple(outs)
```

```python
import functools

import jax
import jax.numpy as jnp
from jax import lax
from jax.experimental import pallas as pl
from jax.experimental.pallas import tpu as pltpu

F32 = jnp.float32
BF16 = jnp.bfloat16
SDS = jax.ShapeDtypeStruct
MESH = pl.DeviceIdType.MESH

N_DEV = 8
D_MODEL = 1024
HEAD_DIM = 64
N_GROUPS = 3
DILATIONS = (1, 4, 16)
BAND = 128
BWD_UNROLL = 8
ATTN_W = 512
CONV_W = 512
CONV_K = 31
CONV_HALO = 32
IN_W = 8704
SHARD_W = IN_W // N_DEV
PAIR_W = 2 * SHARD_W
Q0, K0, V0, ZA0, U0, ZC0, G0 = 0, 1536, 3072, 4608, 5120, 6144, 6656
EPS = 1e-6
LANE = 128
VMEM_LIMIT = 56 * 1024 * 1024

ADAM_LR, ADAM_B1, ADAM_B2, ADAM_EPS, ADAM_WD, ADAM_STEP = 0.001, 0.9, 0.999, 1e-08, 0.01, 10

CONVW_FLAT = 2048


def _params(**kw):
    return pltpu.CompilerParams(vmem_limit_bytes=VMEM_LIMIT, **kw)


def _sigmoid(z):
    return 0.5 * jnp.tanh(0.5 * z) + 0.5


def _dot(a, b):
    return jnp.dot(a, b, preferred_element_type=F32)


def _dot_nt(a, b):
    return lax.dot_general(a, b, (((1,), (1,)), ((), ())), preferred_element_type=F32)


def _dot_tn(a, b):
    return lax.dot_general(a, b, (((0,), (0,)), ((), ())), preferred_element_type=F32)


def _peer(x, y, c, k):
    px = 1 - x if (k >> 2) & 1 else x
    py = 1 - y if (k >> 1) & 1 else y
    pc = 1 - c if k & 1 else c
    return (px, py, pc), 4 * px + 2 * py + pc


def _all_gather(arrays, name, vmem):
    n = len(arrays)
    space = pltpu.VMEM if vmem else pl.ANY

    def body(*refs):
        ins, outs = refs[:n], refs[n:2 * n]
        send_sems, recv_sems, local_sems = refs[2 * n:]
        x, y, c = lax.axis_index("x"), lax.axis_index("y"), lax.axis_index("c")
        me = 4 * x + 2 * y + c
        locals_ = [pltpu.make_async_copy(ins[a], outs[a].at[me], local_sems.at[a]) for a in range(n)]
        for cp in locals_:
            cp.start()
        sends = []
        for k in range(1, N_DEV):
            peer, _ = _peer(x, y, c, k)
            for a in range(n):
                cp = pltpu.make_async_remote_copy(
                    src_ref=ins[a], dst_ref=outs[a].at[me], send_sem=send_sems.at[a, k - 1],
                    recv_sem=recv_sems.at[a, k - 1], device_id=peer, device_id_type=MESH)
                cp.start()
                sends.append(cp)
        for k in range(1, N_DEV):
            peer, pidx = _peer(x, y, c, k)
            for a in range(n):
                pltpu.make_async_remote_copy(
                    src_ref=ins[a], dst_ref=outs[a].at[pidx], send_sem=send_sems.at[a, k - 1],
                    recv_sem=recv_sems.at[a, k - 1], device_id=peer, device_id_type=MESH).wait_recv()
        for cp in sends:
            cp.wait_send()
        for cp in locals_:
            cp.wait()

    return pl.pallas_call(
        body, name=name,
        out_shape=[SDS((N_DEV,) + a.shape, a.dtype) for a in arrays],
        in_specs=[pl.BlockSpec(memory_space=space)] * n,
        out_specs=[pl.BlockSpec(memory_space=space)] * n,
        scratch_shapes=[pltpu.SemaphoreType.DMA((n, N_DEV - 1)), pltpu.SemaphoreType.DMA((n, N_DEV - 1)),
                        pltpu.SemaphoreType.DMA((n,))],
        compiler_params=_params(),
    )(*arrays)


CHIP_K = (2, 4, 6)


def _all_gather_chips(arrays, name):
    n = len(arrays)
    k_y, k_x, k_d = CHIP_K

    def body(*refs):
        ins, outs = refs[:n], refs[n:2 * n]
        send_sems, recv_sems, local_sems = refs[2 * n:]
        x, y, c = lax.axis_index("x"), lax.axis_index("y"), lax.axis_index("c")
        me = 4 * x + 2 * y + c
        sib, sib_idx = _peer(x, y, c, 1)
        nbr_y, idx_y = _peer(x, y, c, k_y)
        nbr_x, idx_x = _peer(x, y, c, k_x)
        _, idx_d = _peer(x, y, c, k_d)

        def copy(a, slot, block, to, src=None):
            return pltpu.make_async_remote_copy(
                src_ref=outs[a].at[block] if src is None else src, dst_ref=outs[a].at[block],
                send_sem=send_sems.at[a, slot], recv_sem=recv_sems.at[a, slot], device_id=to, device_id_type=MESH)

        locals_ = [pltpu.make_async_copy(ins[a], outs[a].at[me], local_sems.at[a]) for a in range(n)]
        for cp in locals_:
            cp.start()
        for a in range(n):
            copy(a, 0, me, sib, src=ins[a]).start()
            copy(a, 1, me, nbr_y, src=ins[a]).start()
            copy(a, 2, me, nbr_x, src=ins[a]).start()

        def arrived(slot, block, frm, send_on_to=None):
            for a in range(n):
                copy(a, slot, block, frm).wait_recv()
                if send_on_to is not None:
                    copy(a, 3, block, send_on_to).start()
                copy(a, 3 + slot, block, sib).start()

        @pl.when(c == 0)
        def _():
            arrived(1, idx_y, nbr_y, send_on_to=nbr_x)
            arrived(2, idx_x, nbr_x)

        @pl.when(c == 1)
        def _():
            arrived(2, idx_x, nbr_x, send_on_to=nbr_y)
            arrived(1, idx_y, nbr_y)

        arrived(3, idx_d, nbr_x)
        for a in range(n):
            copy(a, 0, sib_idx, sib).wait_recv()
        for slot, k in ((4, k_y), (5, k_x), (6, k_d)):
            _, pidx = _peer(x, y, 1 - c, k)
            for a in range(n):
                copy(a, slot, pidx, sib).wait_recv()
        for slot in range(N_DEV - 1):
            for a in range(n):
                copy(a, slot, me, sib).wait_send()
        for cp in locals_:
            cp.wait()

    return pl.pallas_call(
        body, name=name,
        out_shape=[SDS((N_DEV,) + a.shape, a.dtype) for a in arrays],
        in_specs=[pl.BlockSpec(memory_space=pl.ANY)] * n,
        out_specs=[pl.BlockSpec(memory_space=pl.ANY)] * n,
        scratch_shapes=[pltpu.SemaphoreType.DMA((n, N_DEV - 1)), pltpu.SemaphoreType.DMA((n, N_DEV - 1)),
                        pltpu.SemaphoreType.DMA((n,))],
        compiler_params=_params(),
    )(*arrays)


def _gather_mm_in(h, arrays, name):
    n = len(arrays)
    s = h.shape[0]
    tm = 512
    nm = s // tm
    k_y, k_x, k_d = CHIP_K

    def body(*refs):
        h_ref, ins, proj_ref, outs = refs[0], refs[1:1 + n], refs[1 + n], refs[2 + n:2 + 2 * n]
        wbuf, obuf, send_sems, recv_sems, local_sems, wsem, osem = refs[2 + 2 * n:]
        x, y, c = lax.axis_index("x"), lax.axis_index("y"), lax.axis_index("c")
        me = 4 * x + 2 * y + c
        sib, sib_idx = _peer(x, y, c, 1)
        nbr_y, idx_y = _peer(x, y, c, k_y)
        nbr_x, idx_x = _peer(x, y, c, k_x)
        _, idx_d = _peer(x, y, c, k_d)

        def copy(a, slot, block, to, src=None):
            return pltpu.make_async_remote_copy(
                src_ref=outs[a].at[block] if src is None else src, dst_ref=outs[a].at[block],
                send_sem=send_sems.at[a, slot], recv_sem=recv_sems.at[a, slot], device_id=to, device_id_type=MESH)

        locals_ = [pltpu.make_async_copy(ins[a], outs[a].at[me], local_sems.at[a]) for a in range(n)]
        for cp in locals_:
            cp.start()
        for a in range(n):
            copy(a, 0, me, sib, src=ins[a]).start()
            copy(a, 1, me, nbr_y, src=ins[a]).start()
            copy(a, 2, me, nbr_x, src=ins[a]).start()

        def arrived(slot, block, frm, send_on_to=None):
            for a in range(n):
                copy(a, slot, block, frm).wait_recv()
                if send_on_to is not None:
                    copy(a, 3, block, send_on_to).start()
                copy(a, 3 + slot, block, sib).start()

        def from_sibling(slot, k):
            _, pidx = _peer(x, y, 1 - c, k)
            for a in range(n):
                copy(a, slot, pidx, sib).wait_recv()

        def out_copy(slot, m, pair):
            return pltpu.make_async_copy(
                obuf.at[slot], proj_ref.at[pl.ds(pl.multiple_of(m * tm, tm), tm),
                                           pl.ds(pl.multiple_of(pair * PAIR_W, LANE), PAIR_W)], osem.at[slot])

        def project(pair, first):
            loads = [pltpu.make_async_copy(outs[0].at[2 * pair + i], wbuf.at[i * SHARD_W:(i + 1) * SHARD_W], wsem.at[i])
                     for i in range(2)]
            for cp in loads:
                cp.start(priority=1)
            for cp in loads:
                cp.wait()

            def step(m, _):
                slot = m % 2
                acc = _dot_nt(h_ref[pl.ds(pl.multiple_of(m * tm, tm), tm), :], wbuf[...])
                if first:
                    pl.when(m >= 2)(lambda: out_copy(slot, m, pair).wait())
                else:
                    out_copy(slot, m, pair).wait()
                obuf[slot] = acc
                out_copy(slot, m, pair).start(priority=1)
                return 0
            lax.fori_loop(0, nm, step, 0)

        for a in range(n):
            copy(a, 0, sib_idx, sib).wait_recv()
        for cp in locals_:
            cp.wait()
        project(2 * x + y, first=True)

        @pl.when(c == 0)
        def _():
            arrived(1, idx_y, nbr_y, send_on_to=nbr_x)
            arrived(2, idx_x, nbr_x)

        @pl.when(c == 1)
        def _():
            arrived(2, idx_x, nbr_x, send_on_to=nbr_y)
            arrived(1, idx_y, nbr_y)

        from_sibling(4, k_y)
        project(2 * x + (1 - y), first=False)
        arrived(3, idx_d, nbr_x)
        from_sibling(5, k_x)
        project(2 * (1 - x) + y, first=False)
        from_sibling(6, k_d)
        project(2 * (1 - x) + (1 - y), first=False)
        for slot in range(2):
            out_copy(slot, 0, 0).wait()
        for slot in range(N_DEV - 1):
            for a in range(n):
                copy(a, slot, me, sib).wait_send()

    any_ = pl.BlockSpec(memory_space=pl.ANY)
    res = pl.pallas_call(
        body, name=name,
        out_shape=[SDS((s, IN_W), F32)] + [SDS((N_DEV,) + a.shape, a.dtype) for a in arrays],
        in_specs=[pl.BlockSpec(memory_space=pltpu.VMEM)] + [any_] * n,
        out_specs=[any_] * (1 + n),
        scratch_shapes=[pltpu.VMEM((PAIR_W, D_MODEL), BF16), pltpu.VMEM((2, tm, PAIR_W), F32),
                        pltpu.SemaphoreType.DMA((n, N_DEV - 1)), pltpu.SemaphoreType.DMA((n, N_DEV - 1)),
                        pltpu.SemaphoreType.DMA((n,)), pltpu.SemaphoreType.DMA((2,)), pltpu.SemaphoreType.DMA((2,))],
        compiler_params=_params(),
    )(h, *arrays)
    return res[0], res[1:]


def _exchange_sibling(arrays, name):
    n = len(arrays)
    ks = (0,) + CHIP_K

    def body(*refs):
        ins, outs = refs[:n], refs[n:2 * n]
        send_sems, recv_sems = refs[2 * n:]
        x, y, c = lax.axis_index("x"), lax.axis_index("y"), lax.axis_index("c")
        sib, sib_idx = _peer(x, y, c, 1)
        sends = []
        for i, k in enumerate(ks):
            _, tgt = _peer(x, y, 1 - c, k) if k else (None, sib_idx)
            for a in range(n):
                cp = pltpu.make_async_remote_copy(
                    src_ref=ins[a].at[tgt], dst_ref=outs[a].at[i], send_sem=send_sems.at[a, i],
                    recv_sem=recv_sems.at[a, i], device_id=sib, device_id_type=MESH)
                cp.start()
                sends.append(cp)
        for cp in sends:
            cp.wait_recv()
        for cp in sends:
            cp.wait_send()

    return pl.pallas_call(
        body, name=name,
        out_shape=[SDS((len(ks),) + a.shape[1:], a.dtype) for a in arrays],
        in_specs=[pl.BlockSpec(memory_space=pl.ANY)] * n,
        out_specs=[pl.BlockSpec(memory_space=pl.ANY)] * n,
        scratch_shapes=[pltpu.SemaphoreType.DMA((n, len(ks))), pltpu.SemaphoreType.DMA((n, len(ks)))],
        compiler_params=_params(),
    )(*arrays)


def _presum(mine, from_sib, me_arr, name):
    _, rows, cols = mine.shape
    tr = _row_tile(rows)
    ns = 1 + len(CHIP_K)

    def body(me_ref, a_ref, b_ref, o_ref):
        del me_ref
        o_ref[...] = (a_ref[...].astype(F32) + b_ref[...].astype(F32)).astype(o_ref.dtype)

    grid_spec = pltpu.PrefetchScalarGridSpec(
        num_scalar_prefetch=1, grid=(ns, rows // tr),
        in_specs=[pl.BlockSpec((1, tr, cols), lambda j, i, me: (jnp.bitwise_xor(me[0], 2 * j), i, 0)),
                  pl.BlockSpec((1, tr, cols), lambda j, i, me: (j, i, 0))],
        out_specs=pl.BlockSpec((1, tr, cols), lambda j, i, me: (j, i, 0)))
    return pl.pallas_call(body, name=name, grid_spec=grid_spec, out_shape=SDS((ns, rows, cols), mine.dtype),
                          compiler_params=_params())(me_arr, mine, from_sib)


HBM_SPEC = pl.BlockSpec(memory_space=pltpu.HBM)
SEM_SPEC = pl.BlockSpec(memory_space=pltpu.SEMAPHORE)
SIDE_EFFECT = pltpu.SideEffectType.DATAFLOW_SIDE_EFFECTING


def _chips_copies(pre_refs, land_refs, send_sems, recv_sems):
    x, y, c = lax.axis_index("x"), lax.axis_index("y"), lax.axis_index("c")
    copies = []
    for j, k in enumerate(CHIP_K):
        peer, _ = _peer(x, y, c, k)
        for a in range(len(pre_refs)):
            copies.append(pltpu.make_async_remote_copy(
                src_ref=pre_refs[a].at[1 + j], dst_ref=land_refs[a].at[j], send_sem=send_sems.at[a * len(CHIP_K) + j],
                recv_sem=recv_sems.at[a * len(CHIP_K) + j], device_id=peer, device_id_type=MESH))
    return copies


def _exchange_chips_start(presums, name):
    n = len(presums)

    def body(*refs):
        pre, land = refs[:n], refs[n:2 * n]
        send_sems, recv_sems = refs[2 * n], refs[2 * n + 1]
        token = refs[-1]
        for cp in _chips_copies(pre, land, send_sems, recv_sems):
            cp.start()
        token[...] = jnp.zeros_like(token)

    nk = len(CHIP_K)
    hbm = [pltpu.HBM(p.shape, p.dtype) for p in presums]
    hbm_land = [pltpu.HBM((nk,) + p.shape[1:], p.dtype) for p in presums]
    res = pl.pallas_call(
        body, name=name,
        out_shape=(pltpu.SemaphoreType.DMA((n * nk,)), pltpu.SemaphoreType.DMA((n * nk,)), *hbm, *hbm_land, SDS((8, LANE), F32)),
        in_specs=[HBM_SPEC] * (2 * n),
        out_specs=(SEM_SPEC, SEM_SPEC, *([HBM_SPEC] * (2 * n)), pl.BlockSpec(memory_space=pltpu.VMEM)),
        input_output_aliases={i: 2 + i for i in range(2 * n)},
        compiler_params=pltpu.CompilerParams(has_side_effects=SIDE_EFFECT),
    )(*[pltpu.with_memory_space_constraint(p, pltpu.HBM) for p in presums],
      *[pltpu.with_memory_space_constraint(lax.empty((nk,) + p.shape[1:], p.dtype), pltpu.HBM) for p in presums])
    return res[0], res[1], res[2:2 + n], res[2 + n:2 + 2 * n], res[-1]


def _exchange_chips_wait(send_sems, recv_sems, pre_thru, land_thru, after, name):
    n = len(pre_thru)

    def body(*refs):
        pre, land = refs[:n], refs[n:2 * n]
        s_sems, r_sems = refs[2 * n], refs[2 * n + 1]
        for cp in _chips_copies(pre, land, s_sems, r_sems):
            cp.wait_send()
            cp.wait_recv()

    hbm = [pltpu.HBM(p.shape, p.dtype) for p in (*pre_thru, *land_thru)]
    res = pl.pallas_call(
        body, name=name, out_shape=tuple(hbm),
        in_specs=[HBM_SPEC] * (2 * n) + [SEM_SPEC, SEM_SPEC, pl.BlockSpec(memory_space=pl.ANY)],
        out_specs=tuple([HBM_SPEC] * (2 * n)),
        input_output_aliases={i: i for i in range(2 * n)},
        compiler_params=pltpu.CompilerParams(has_side_effects=SIDE_EFFECT),
    )(*pre_thru, *land_thru, send_sems, recv_sems, after)
    return res[:n], res[n:]


def _exchange_chips(presums, name):
    n = len(presums)
    nk = len(CHIP_K)

    def body(*refs):
        pre, land = refs[:n], refs[n:2 * n]
        send_sems, recv_sems = refs[2 * n:]
        copies = _chips_copies(pre, land, send_sems, recv_sems)
        for cp in copies:
            cp.start()
        for cp in copies:
            cp.wait_recv()
        for cp in copies:
            cp.wait_send()

    return pl.pallas_call(
        body, name=name,
        out_shape=[SDS((nk,) + p.shape[1:], p.dtype) for p in presums],
        in_specs=[pl.BlockSpec(memory_space=pl.ANY)] * n,
        out_specs=[pl.BlockSpec(memory_space=pl.ANY)] * n,
        scratch_shapes=[pltpu.SemaphoreType.DMA((n * nk,)), pltpu.SemaphoreType.DMA((n * nk,))],
        compiler_params=_params(),
    )(*presums)


def _cast_bf16(w, name):
    def body(w_ref, o_ref):
        o_ref[...] = w_ref[...].astype(BF16)

    return pl.pallas_call(body, name=name, out_shape=SDS(w.shape, BF16), compiler_params=_params())(w)


def _cols_from_slots(wg, name):
    _, rows, cols = wg.shape

    def body(w_ref, o_ref):
        for j in range(N_DEV):
            o_ref[:, j * cols:(j + 1) * cols] = w_ref[j]

    return pl.pallas_call(body, name=name, out_shape=SDS((rows, N_DEV * cols), wg.dtype), compiler_params=_params())(wg)


def _ada_fwd(c_all, w_ada):
    def body(c_ref, w_ref, o_ref):
        cv = c_ref[...]
        sc = (cv * _sigmoid(cv)).astype(BF16)
        o_ref[...] = _dot(sc, w_ref[...].astype(BF16))

    return pl.pallas_call(body, name="ada_fwd", out_shape=SDS((N_DEV, w_ada.shape[1]), F32),
                          compiler_params=_params())(c_all, w_ada)


def _ada_bwd(c_all, d_ada_cols):
    def body(c_ref, d_ref, o_ref):
        cv = c_ref[...]
        sc = (cv * _sigmoid(cv)).astype(BF16)
        o_ref[...] = _dot_tn(sc, d_ref[...].astype(BF16))

    return pl.pallas_call(body, name="ada_bwd", out_shape=SDS((D_MODEL, d_ada_cols.shape[1]), F32),
                          compiler_params=_params())(c_all, d_ada_cols)


def _norm_fwd(x, norm_w, scale, shift):
    s = x.shape[0]
    tr = 512

    def body(x_ref, nw_ref, sc_ref, sh_ref, h_ref, ht_ref):
        xv = x_ref[...]
        r = lax.rsqrt(jnp.mean(xv * xv, axis=-1, keepdims=True) + EPS)
        h = (xv * r * nw_ref[...]) * (1.0 + sc_ref[...]) + sh_ref[...]
        h_ref[...] = h.astype(BF16)
        ht_ref[...] = h.T.astype(BF16)

    vec = pl.BlockSpec((1, D_MODEL), lambda i: (0, 0))
    return pl.pallas_call(
        body, name="norm_fwd", grid=(s // tr,),
        in_specs=[pl.BlockSpec((tr, D_MODEL), lambda i: (i, 0)), vec, vec, vec],
        out_specs=[pl.BlockSpec((tr, D_MODEL), lambda i: (i, 0)), pl.BlockSpec((D_MODEL, tr), lambda i: (0, i))],
        out_shape=[SDS((s, D_MODEL), BF16), SDS((D_MODEL, s), BF16)], compiler_params=_params(),
    )(x, norm_w, scale, shift)


def _mm_in(h, wt):
    s = h.shape[0]
    tm = 512

    def body(h_ref, w_ref, o_ref):
        o_ref[...] = _dot_nt(h_ref[...], w_ref[...])

    return pl.pallas_call(
        body, name="mm_in", grid=(IN_W // PAIR_W, s // tm),
        in_specs=[pl.BlockSpec((tm, D_MODEL), lambda p, m: (m, 0)),
                  pl.BlockSpec((PAIR_W, D_MODEL), lambda p, m: (p, 0))],
        out_specs=pl.BlockSpec((tm, PAIR_W), lambda p, m: (m, p)),
        out_shape=SDS((s, IN_W), F32), compiler_params=_params(),
    )(h, wt)


def _head_ones():
    a = lax.broadcasted_iota(jnp.int32, (LANE, LANE), 0) // HEAD_DIM
    b = lax.broadcasted_iota(jnp.int32, (LANE, LANE), 1) // HEAD_DIM
    return (a == b).astype(BF16)


def _head_sums(t, ones):
    return _dot(t.astype(BF16), ones)


def _band_bias(bias, transposed=False):
    qi = lax.broadcasted_iota(jnp.int32, (2 * BAND, 2 * BAND), 1 if transposed else 0) % BAND
    kj = lax.broadcasted_iota(jnp.int32, (2 * BAND, 2 * BAND), 0 if transposed else 1)
    dist = qi + BAND - kj
    valid = (dist >= 0) & (dist <= BAND)
    bias[1] = jnp.where(valid, 0.0, -1e30)
    bias[0] = jnp.where(valid & (kj >= BAND), 0.0, -1e30)


def _token_rows(j, d, chunk, per_r):
    return pl.ds(j // per_r + (j % per_r) * (chunk * d), chunk, stride=d)


def _deinterleave(src_ref, dst_ref, w_ref, ones, d, sub_len, chunk, scale, dst_off):
    per_r = sub_len // chunk

    def step(j, _):
        t = src_ref[_token_rows(j, d, chunk, per_r), :]
        if w_ref is not None:
            ms = _head_sums(t * t, ones) * (1.0 / HEAD_DIM)
            t = t * lax.rsqrt(ms + EPS) * (w_ref[...] * scale)
        dst_ref[pl.ds(pl.multiple_of(dst_off + j * chunk, BAND), chunk), :] = t.astype(dst_ref.dtype)
        return 0
    lax.fori_loop(0, d * per_r, step, 0, unroll=4)


N_PAIRS = ATTN_W // LANE


def _attn_fwd(proj, qw2, kw2):
    s = proj.shape[0]

    def group_body(g, step, q_ref, k_ref, v_ref, qw_ref, kw_ref, o_ref, l_ref, qn_ref, kn_ref, vn_ref,
                   qd, kd, vd, od, ld, bias):
        d = DILATIONS[g]
        sub_len = s // d
        nb = sub_len // BAND
        chunk = min(sub_len, 256)
        lo = lax.broadcasted_iota(jnp.int32, (1, LANE), 1) < HEAD_DIM
        ones = _head_ones()

        @pl.when(step == 0)
        def _():
            _band_bias(bias)

        kd[0:BAND, :] = jnp.zeros((BAND, LANE), BF16)
        vd[0:BAND, :] = jnp.zeros((BAND, LANE), BF16)
        _deinterleave(q_ref, qd, qw_ref, ones, d, sub_len, chunk, HEAD_DIM ** -0.5, 0)
        _deinterleave(k_ref, kd, kw_ref, ones, d, sub_len, chunk, 1.0, BAND)
        _deinterleave(v_ref, vd, None, ones, d, sub_len, chunk, 1.0, BAND)
        qn_ref[...] = qd[...]
        kn_ref[...] = kd[BAND:BAND + s, :]
        vn_ref[...] = vd[BAND:BAND + s, :]

        def block(t, _):
            base = pl.multiple_of(t * BAND, BAND)
            q = qd[pl.ds(base, BAND), :]
            k2 = kd[pl.ds(base, 2 * BAND), :]
            v2 = vd[pl.ds(base, 2 * BAND), :]
            zero = jnp.zeros_like(q)
            qs = jnp.concatenate([jnp.where(lo, q, zero), jnp.where(lo, zero, q)], axis=0)
            sc = _dot_nt(qs, k2) + bias[jnp.minimum(t % nb, 1)]
            m = jnp.max(sc, axis=-1, keepdims=True)
            p = jnp.exp(sc - m)
            den = jnp.sum(p, axis=-1, keepdims=True)
            u = _dot(p.astype(BF16), v2) * (1.0 / den)
            lse = m + jnp.log(den)
            od[pl.ds(base, BAND), :] = jnp.where(lo, u[:BAND], u[BAND:])
            ld[pl.ds(base, BAND), :] = jnp.where(lo, lse[:BAND], lse[BAND:])
            return 0
        lax.fori_loop(0, s // BAND, block, 0, unroll=16)

        per_r = sub_len // chunk

        def back(j, _):
            src = pl.ds(pl.multiple_of(j * chunk, chunk), chunk)
            dst = _token_rows(j, d, chunk, per_r)
            o_ref[dst, :] = od[src, :]
            l_ref[dst, :] = ld[src, :]
            return 0
        lax.fori_loop(0, d * per_r, back, 0, unroll=2)

    def body(*refs):
        step = pl.program_id(0)
        for g in range(N_GROUPS):
            pl.when(step // N_PAIRS == g)(functools.partial(group_body, g, step, *refs))

    col = lambda off: pl.BlockSpec((s, LANE), lambda i, off=off: (0, off // LANE + i))
    vec = pl.BlockSpec((1, LANE), lambda i: (0, 0))
    out = pl.BlockSpec((s, LANE), lambda i: (0, i))
    width = N_GROUPS * ATTN_W
    return pl.pallas_call(
        body, name="attn_fwd", grid=(N_GROUPS * N_PAIRS,),
        in_specs=[col(Q0), col(K0), col(V0), vec, vec], out_specs=[out] * 5,
        out_shape=[SDS((s, width), F32)] * 2 + [SDS((s, width), BF16)] * 3,
        scratch_shapes=[pltpu.VMEM((s, LANE), BF16), pltpu.VMEM((s + BAND, LANE), BF16), pltpu.VMEM((s + BAND, LANE), BF16),
                        pltpu.VMEM((s, LANE), F32), pltpu.VMEM((s, LANE), F32),
                        pltpu.VMEM((2, 2 * BAND, 2 * BAND), F32)],
        compiler_params=_params(),
    )(proj, proj, proj, qw2, kw2)


def _attn_bwd(proj, qn, kn, vn, da, lse_delta, qw2, kw2, dproj):
    s = proj.shape[0]
    n_steps = N_GROUPS * N_PAIRS

    def group_body(g, hp, q_ref, k_ref, qn_ref, kn_ref, vn_ref, da_ref, ld_ref, qw_ref, kw_ref, dp_in, dp_out,
                   dqw_ref, dkw_ref, kd, vd, kdt, dad, lst, dlt, dqt, dqd, dkd, dvd, st, stb, bias_t, wacc, sem):
        del dp_in
        d = DILATIONS[g]
        sub_len = s // d
        nb = sub_len // BAND
        chunk = min(sub_len, 256)
        lo = lax.broadcasted_iota(jnp.int32, (1, LANE), 1) < HEAD_DIM
        row_lo = lax.broadcasted_iota(jnp.int32, (LANE, 1), 0) < HEAD_DIM
        ones = _head_ones()
        per_r = sub_len // chunk
        cblk = chunk // BAND

        @pl.when(hp == 0)
        def _():
            _band_bias(bias_t, transposed=True)

        kd[0:BAND, :] = jnp.zeros((BAND, LANE), BF16)
        vd[0:BAND, :] = jnp.zeros((BAND, LANE), BF16)
        kdt[0] = jnp.zeros((LANE, BAND), BF16)
        kd[BAND:BAND + s, :] = kn_ref[...]
        vd[BAND:BAND + s, :] = vn_ref[...]

        def k_step(t, _):
            kdt[1 + t] = kn_ref[pl.ds(pl.multiple_of(t * BAND, BAND), BAND), :].astype(F32).T.astype(BF16)
            return 0
        lax.fori_loop(0, s // BAND, k_step, 0, unroll=4)
        _deinterleave(da_ref, dad, None, ones, d, sub_len, chunk, 1.0, 0)

        def rows_step(j, _):
            tok = _token_rows(j, d, chunk, per_r)
            tt = ld_ref[tok, :].T
            for u in range(cblk):
                cols = slice(u * BAND, (u + 1) * BAND)
                lst[j * cblk + u, 0:1, :] = tt[0:1, cols]
                lst[j * cblk + u, 1:2, :] = tt[HEAD_DIM:HEAD_DIM + 1, cols]
                dlt[j * cblk + u, 0:1, :] = tt[HEAD_DIM // 2:HEAD_DIM // 2 + 1, cols]
                dlt[j * cblk + u, 1:2, :] = tt[HEAD_DIM + HEAD_DIM // 2:HEAD_DIM + HEAD_DIM // 2 + 1, cols]
            return 0
        lax.fori_loop(0, d * per_r, rows_step, 0, unroll=4)

        def block(t, carry):
            ck, cv = carry
            base = pl.multiple_of(t * BAND, BAND)
            q = qn_ref[pl.ds(base, BAND), :]
            k2 = kd[pl.ds(base, 2 * BAND), :]
            v2 = vd[pl.ds(base, 2 * BAND), :]
            k2t = jnp.concatenate([kdt[t], kdt[t + 1]], axis=1)
            dav = dad[pl.ds(base, BAND), :]
            zero = jnp.zeros_like(q)
            qs = jnp.concatenate([jnp.where(lo, q, zero), jnp.where(lo, zero, q)], axis=0)
            das = jnp.concatenate([jnp.where(lo, dav, zero), jnp.where(lo, zero, dav)], axis=0)
            ls_row = jnp.concatenate([lst[t, 0:1, :], lst[t, 1:2, :]], axis=1)
            dl_row = jnp.concatenate([dlt[t, 0:1, :], dlt[t, 1:2, :]], axis=1)
            sc_t = _dot_nt(k2, qs) + bias_t[jnp.minimum(t % nb, 1)]
            p_t = jnp.exp(sc_t - ls_row)
            dp_t = _dot_nt(v2, das)
            ds_t = (p_t * (dp_t - dl_row)).astype(BF16)
            dv2 = _dot(p_t.astype(BF16), das)
            dk2 = _dot(ds_t, qs)
            dvd[pl.ds(base, BAND), :] = cv + dv2[:BAND]
            dkd[pl.ds(base, BAND), :] = ck + dk2[:BAND]
            dq_t = _dot(k2t, ds_t)
            dqt[t] = jnp.where(row_lo, dq_t[:, :BAND], dq_t[:, BAND:])
            return dk2[BAND:], dv2[BAND:]

        def blocks(i, carry):
            for u in range(BWD_UNROLL):
                carry = block(i * BWD_UNROLL + u, carry)
            return carry
        zeros = jnp.zeros((BAND, LANE), F32)
        ck, cv = lax.fori_loop(0, s // (BAND * BWD_UNROLL), blocks, (zeros, zeros))
        dkd[s:s + BAND, :] = ck
        dvd[s:s + BAND, :] = cv

        def dq_rows(t, _):
            dqd[pl.ds(pl.multiple_of(t * BAND, BAND), BAND), :] = dqt[t].T
            return 0
        lax.fori_loop(0, s // BAND, dq_rows, 0, unroll=4)

        def col_copy(slot, col0):
            return pltpu.make_async_copy(
                stb.at[slot], dp_out.at[:, pl.ds(pl.multiple_of(col0 + LANE * hp, LANE), LANE)], sem.at[slot])

        def store_cols(slot, col0):
            @pl.when(hp > 0)
            def _():
                col_copy(slot, col0).wait()
            stb[slot] = st[...].astype(BF16)
            col_copy(slot, col0).start()

        def norm_back(src_ref, dy_ref, dy_off, w_ref, scale, dw_ref, slot, col0):
            wacc[...] = jnp.zeros_like(wacc)

            def step(j, _):
                tok = _token_rows(j, d, chunk, per_r)
                t = src_ref[tok, :]
                dy = dy_ref[pl.ds(pl.multiple_of(dy_off + j * chunk, BAND), chunk), :]
                rr = lax.rsqrt(_head_sums(t * t, ones) * (1.0 / HEAD_DIM) + EPS)
                nrm = t * rr
                wacc[...] += jnp.sum((dy * nrm).reshape(chunk // 8, 8, LANE), axis=0)
                dn = dy * (w_ref[...] * scale)
                st[tok, :] = rr * (dn - nrm * (_head_sums(dn * nrm, ones) * (1.0 / HEAD_DIM)))
                return 0
            lax.fori_loop(0, d * per_r, step, 0, unroll=4)
            dw_ref[...] += jnp.broadcast_to(jnp.sum(wacc[...], axis=0, keepdims=True) * scale, dw_ref.shape)
            store_cols(slot, col0)

        @pl.when(hp == 0)
        def _():
            dqw_ref[...] = jnp.zeros_like(dqw_ref)
            dkw_ref[...] = jnp.zeros_like(dkw_ref)

        norm_back(q_ref, dqd, 0, qw_ref, HEAD_DIM ** -0.5, dqw_ref, 0, Q0)
        norm_back(k_ref, dkd, BAND, kw_ref, 1.0, dkw_ref, 1, K0)

        def v_back(j, _):
            src = pl.ds(pl.multiple_of(BAND + j * chunk, BAND), chunk)
            st[_token_rows(j, d, chunk, per_r), :] = dvd[src, :]
            return 0
        lax.fori_loop(0, d * per_r, v_back, 0, unroll=2)
        store_cols(2, V0)

        @pl.when(hp == n_steps - 1)
        def _():
            for slot, col0 in enumerate((Q0, K0, V0)):
                col_copy(slot, col0).wait()

    def body(*refs):
        step = pl.program_id(0)
        for g in range(N_GROUPS):
            pl.when(step // N_PAIRS == g)(functools.partial(group_body, g, step, *refs))

    col = lambda off: pl.BlockSpec((s, LANE), lambda i, off=off: (0, off // LANE + i))
    mid = pl.BlockSpec((s, LANE), lambda i: (0, i))
    slot4 = pl.BlockSpec((s, LANE), lambda i: (0, i % N_PAIRS))
    vec = pl.BlockSpec((1, LANE), lambda i: (0, 0))
    acc = pl.BlockSpec((8, LANE), lambda i: (0, 0))
    any_ = pl.BlockSpec(memory_space=pl.ANY)
    return pl.pallas_call(
        body, name="attn_bwd", grid=(n_steps,),
        in_specs=[col(Q0), col(K0), mid, mid, mid, slot4, slot4, vec, vec, any_],
        out_specs=[any_, acc, acc],
        out_shape=[SDS(dproj.shape, dproj.dtype), SDS((8, LANE), F32), SDS((8, LANE), F32)],
        input_output_aliases={9: 0},
        scratch_shapes=[pltpu.VMEM((s + BAND, LANE), BF16), pltpu.VMEM((s + BAND, LANE), BF16),
                        pltpu.VMEM((s // BAND + 1, LANE, BAND), BF16), pltpu.VMEM((s, LANE), BF16),
                        pltpu.VMEM((s // BAND, 8, BAND), F32), pltpu.VMEM((s // BAND, 8, BAND), F32),
                        pltpu.VMEM((s // BAND, LANE, BAND), F32),
                        pltpu.VMEM((s, LANE), F32), pltpu.VMEM((s + BAND, LANE), F32), pltpu.VMEM((s + BAND, LANE), F32),
                        pltpu.VMEM((s, LANE), F32), pltpu.VMEM((3, s, LANE), BF16),
                        pltpu.VMEM((2, 2 * BAND, 2 * BAND), F32), pltpu.VMEM((8, LANE), F32),
                        pltpu.SemaphoreType.DMA((3,))],
        compiler_params=_params(),
    )(proj, proj, qn, kn, vn, da, lse_delta, qw2, kw2, dproj)


def _tap_views(ext_ref, sh_ref, offsets, tr, cols):
    for b in range(8):
        group = [j for j, o in enumerate(offsets) if o % 8 == b]
        if not group:
            continue
        first = min(offsets[j] for j in group)
        span = tr + max(offsets[j] for j in group) - first
        sh_ref[0:span, cols] = ext_ref[first:first + span, cols]
        for j in group:
            yield j, sh_ref[offsets[j] - first:offsets[j] - first + tr, cols]


def _silu_grad(z, sg):
    return sg * (1.0 + z * (1.0 - sg))


def _glu(u):
    a_h, b_h = u[:, :CONV_W], u[:, CONV_W:]
    sg = _sigmoid(b_h)
    return a_h, sg, a_h * sg


def _tail(x, tgt, proj, o3, l3, wa, wc, wo, gate, bga, bgc, convw, convb, lnw, lnb, bd):
    s = x.shape[0]
    tr = 256

    def body(x_ref, t_ref, za_ref, u_ref, uh_ref, zc_ref, g0_ref, g1_ref, g2_ref, g3_ref,
             o0_ref, o1_ref, o2_ref, l0_ref, l1_ref, l2_ref, wa_ref, wc_ref, wo_ref,
             gate_ref, bga_ref, bgc_ref, cw_ref, cb_ref, lnw_ref, lnb_ref, bd_ref,
             dout_ref, da_ref, ld_ref, dcv_ref, mt_ref, yat_ref, yct_ref, dmo_ref, dya_ref, dyc_ref, dp_ref,
             dgate_ref, dbg_ref, dlnw_ref, dlnb_ref, dcb_ref, loss_ref,
             ext, sh, st_za, st_zc, st_g, sems):
        i = pl.program_id(0)

        @pl.when(i == 0)
        def _():
            for r in (dgate_ref, dbg_ref, dlnw_ref, dlnb_ref, dcb_ref, loss_ref):
                r[...] = jnp.zeros_like(r)

        def acc_rows(ref, v):
            ref[...] += jnp.broadcast_to(jnp.sum(v, axis=0, keepdims=True), ref.shape)

        la, lb, lc = l0_ref[...], l1_ref[...], l2_ref[...]
        mx = jnp.maximum(jnp.maximum(la, lb), lc)
        ea, eb, ec = jnp.exp(la - mx), jnp.exp(lb - mx), jnp.exp(lc - mx)
        den = ea + eb + ec
        inv = 1.0 / den
        attn = (ea * inv) * o0_ref[...] + (eb * inv) * o1_ref[...] + (ec * inv) * o2_ref[...]
        lse = mx + jnp.log(den)

        za = za_ref[...]
        sga = _sigmoid(za)
        sa = za * sga
        ya_in = attn * sa
        y_attn = _dot(ya_in.astype(BF16), wa_ref[...])

        _, _, glu = _glu(u_ref[...])
        _, _, glu_h = _glu(uh_ref[...])
        ext[0:CONV_HALO, :] = jnp.where(i > 0, glu_h, 0.0)
        ext[CONV_HALO:CONV_HALO + tr, :] = glu
        cv_blocks = []
        for cb in range(CONV_W // LANE):
            cols = slice(cb * LANE, (cb + 1) * LANE)
            cv_c = jnp.broadcast_to(cb_ref[:, cols], (tr, LANE))
            for j, rows in _tap_views(ext, sh, [CONV_HALO - (CONV_K - 1) + j for j in range(CONV_K)], tr, cols):
                cv_c = cv_c + cw_ref[j:j + 1, cols] * rows
            cv_blocks.append(cv_c)
        cv = jnp.concatenate(cv_blocks, axis=1)
        mu = jnp.mean(cv, axis=-1, keepdims=True)
        xc = cv - mu
        rstd = lax.rsqrt(jnp.mean(xc * xc, axis=-1, keepdims=True) + EPS)
        nrm = xc * rstd
        ln = nrm * lnw_ref[...] + lnb_ref[...]
        sgl = _sigmoid(ln)
        cs = ln * sgl
        zc = zc_ref[...]
        sgc = _sigmoid(zc)
        scz = zc * sgc
        yc_in = cs * scz
        y_conv = _dot(yc_in.astype(BF16), wc_ref[...])

        ga = _sigmoid(jnp.concatenate([g0_ref[...], g1_ref[...]], axis=1) + bga_ref[...])
        gc = _sigmoid(jnp.concatenate([g2_ref[...], g3_ref[...]], axis=1) + bgc_ref[...])
        merged = ga * y_attn + gc * y_conv
        mo = _dot(merged.astype(BF16), wo_ref[...])
        gate_v = gate_ref[...]
        err = (x_ref[...] + gate_v * mo) - t_ref[...]
        loss_ref[...] += 0.5 * jnp.sum(jnp.mean(err * err, axis=-1, keepdims=True))
        d_out = err * (1.0 / D_MODEL)
        dout_ref[...] = d_out

        rows = pl.ds(pl.multiple_of(i * tr, tr), tr)
        cps = [pltpu.make_async_copy(st_za, dp_ref.at[rows, pl.ds(ZA0, ATTN_W)], sems.at[0]),
               pltpu.make_async_copy(st_zc, dp_ref.at[rows, pl.ds(ZC0, CONV_W)], sems.at[1]),
               pltpu.make_async_copy(st_g, dp_ref.at[rows, pl.ds(G0, 2 * D_MODEL)], sems.at[2])]

        @pl.when(i > 0)
        def _():
            for cp in cps:
                cp.wait()

        acc_rows(dgate_ref, d_out * mo)
        dmo_b = (d_out * gate_v).astype(BF16)
        dmo_ref[...] = dmo_b
        mt_ref[...] = merged.T.astype(BF16)
        d_merged = _dot_nt(dmo_b, wo_ref[...])
        d_ya = (d_merged * ga).astype(BF16)
        d_yc = (d_merged * gc).astype(BF16)
        dya_ref[...] = d_ya
        dyc_ref[...] = d_yc
        dga = d_merged * y_attn * (ga * (1.0 - ga))
        dgc = d_merged * y_conv * (gc * (1.0 - gc))
        dgs = jnp.concatenate([dga, dgc], axis=1)
        acc_rows(dbg_ref, dgs)
        st_g[...] = dgs.astype(BF16)

        yat_ref[...] = ya_in.T.astype(BF16)
        d_ya_in = _dot_nt(d_ya, wa_ref[...])
        d_attn = d_ya_in * sa
        da_ref[...] = d_attn
        st_za[...] = (d_ya_in * attn * _silu_grad(za, sga)).astype(BF16)
        prod = d_attn * attn
        hi = prod.astype(BF16)
        lo_ = (prod - hi.astype(F32)).astype(BF16)
        delta = _dot(hi, bd_ref[...]) + _dot(lo_, bd_ref[...])
        first_half = (lax.broadcasted_iota(jnp.int32, (1, ATTN_W), 1) % HEAD_DIM) < HEAD_DIM // 2
        ld_ref[...] = jnp.where(first_half, lse, delta)

        yct_ref[...] = yc_in.T.astype(BF16)
        d_yc_in = _dot_nt(d_yc, wc_ref[...])
        st_zc[...] = (d_yc_in * cs * _silu_grad(zc, sgc)).astype(BF16)
        d_ln = (d_yc_in * scz) * _silu_grad(ln, sgl)
        acc_rows(dlnw_ref, d_ln * nrm)
        acc_rows(dlnb_ref, d_ln)
        d_nrm = d_ln * lnw_ref[...]
        d_cv = rstd * (d_nrm - jnp.mean(d_nrm, axis=-1, keepdims=True)
                       - nrm * jnp.mean(d_nrm * nrm, axis=-1, keepdims=True))
        acc_rows(dcb_ref, d_cv)
        dcv_ref[...] = d_cv

        for cp in cps:
            cp.start()

        @pl.when(i == s // tr - 1)
        def _():
            for cp in cps:
                cp.wait()

    def rows(width, colblk=0):
        return pl.BlockSpec((tr, width), lambda i, colblk=colblk: (i, colblk))

    def const(shape):
        return pl.BlockSpec(shape, lambda i: (0,) * len(shape))

    halo = pl.BlockSpec((CONV_HALO, D_MODEL), lambda i: (jnp.maximum(i * (tr // CONV_HALO) - 1, 0), U0 // D_MODEL))
    in_specs = [rows(D_MODEL), rows(D_MODEL), rows(ATTN_W, ZA0 // ATTN_W), rows(D_MODEL, U0 // D_MODEL), halo,
                rows(CONV_W, ZC0 // CONV_W)]
    in_specs += [rows(512, G0 // 512 + j) for j in range(4)]
    in_specs += [rows(ATTN_W, g) for g in range(N_GROUPS)] * 2
    in_specs += [const(wa.shape), const(wc.shape), const(wo.shape), const((1, D_MODEL)), const((1, D_MODEL)),
                 const((1, D_MODEL)), const(convw.shape), const((1, CONV_W)), const((1, CONV_W)), const((1, CONV_W)),
                 const(bd.shape)]
    tcol = lambda width: pl.BlockSpec((width, tr), lambda i: (0, i))
    out_specs = [rows(D_MODEL), rows(ATTN_W), rows(ATTN_W), rows(CONV_W),
                 tcol(D_MODEL), tcol(ATTN_W), tcol(CONV_W), rows(D_MODEL), rows(D_MODEL), rows(D_MODEL),
                 pl.BlockSpec(memory_space=pl.ANY),
                 const((8, D_MODEL)), const((8, 2 * D_MODEL)), const((8, CONV_W)), const((8, CONV_W)), const((8, CONV_W)),
                 const((8, LANE))]
    out_shape = [SDS((s, D_MODEL), F32), SDS((s, ATTN_W), F32), SDS((s, ATTN_W), F32),
                 SDS((s, CONV_W), F32),
                 SDS((D_MODEL, s), BF16), SDS((ATTN_W, s), BF16), SDS((CONV_W, s), BF16),
                 SDS((s, D_MODEL), BF16), SDS((s, D_MODEL), BF16), SDS((s, D_MODEL), BF16),
                 SDS((s, IN_W), BF16),
                 SDS((8, D_MODEL), F32), SDS((8, 2 * D_MODEL), F32), SDS((8, CONV_W), F32), SDS((8, CONV_W), F32),
                 SDS((8, CONV_W), F32), SDS((8, LANE), F32)]
    return pl.pallas_call(
        body, name="tail", grid=(s // tr,), in_specs=in_specs, out_specs=out_specs, out_shape=out_shape,
        scratch_shapes=[pltpu.VMEM((CONV_HALO + tr, CONV_W), F32), pltpu.VMEM((CONV_HALO + tr, CONV_W), F32),
                        pltpu.VMEM((tr, ATTN_W), BF16),
                        pltpu.VMEM((tr, CONV_W), BF16), pltpu.VMEM((tr, 2 * D_MODEL), BF16),
                        pltpu.SemaphoreType.DMA((3,))],
        compiler_params=_params(),
    )(x, tgt, proj, proj, proj, proj, proj, proj, proj, proj, *o3, *l3, wa, wc, wo, gate, bga, bgc,
      convw, convb, lnw, lnb, bd)


def _conv_bwd(dcv, proj, convw, dproj):
    s = dcv.shape[0]
    tr = 128
    nt = s // tr

    def body(dcv_ref, dcvn_ref, u_ref, uh_ref, cw_ref, dp_in, dp_out, dw_ref, extg, extd, sh):
        del dp_in
        i = pl.program_id(0)

        @pl.when(i == 0)
        def _():
            dw_ref[...] = jnp.zeros_like(dw_ref)

        _, _, glu = _glu(u_ref[...])
        _, _, glu_h = _glu(uh_ref[...])
        extg[0:CONV_HALO, :] = jnp.where(i > 0, glu_h, 0.0)
        extg[CONV_HALO:CONV_HALO + tr, :] = glu
        extd[0:tr, :] = dcv_ref[...]
        extd[tr:tr + CONV_HALO, :] = jnp.where(i < nt - 1, dcvn_ref[...], 0.0)
        for cb in range(CONV_W // LANE):
            cols = slice(cb * LANE, (cb + 1) * LANE)
            dglu = jnp.zeros((tr, LANE), F32)
            for j, rows in _tap_views(extd, sh, [CONV_K - 1 - j for j in range(CONV_K)], tr, cols):
                dglu = dglu + cw_ref[j:j + 1, cols] * rows
            dcv_c = dcv_ref[:, cols]
            for j, rows in _tap_views(extg, sh, [CONV_HALO - (CONV_K - 1) + j for j in range(CONV_K)], tr, cols):
                dw_ref[8 * j:8 * j + 8, cols] += jnp.sum((dcv_c * rows).reshape(tr // 8, 8, LANE), axis=0)
            a_h = u_ref[:, cols]
            sgb = _sigmoid(u_ref[:, CONV_W + cb * LANE:CONV_W + (cb + 1) * LANE])
            dp_out[:, cols] = (dglu * sgb).astype(BF16)
            dp_out[:, CONV_W + cb * LANE:CONV_W + (cb + 1) * LANE] = (dglu * a_h * (sgb * (1.0 - sgb))).astype(BF16)

    ucol = U0 // D_MODEL
    return pl.pallas_call(
        body, name="conv_bwd", grid=(nt,),
        in_specs=[pl.BlockSpec((tr, CONV_W), lambda i: (i, 0)),
                  pl.BlockSpec((CONV_HALO, CONV_W), lambda i: (jnp.minimum((i + 1) * (tr // CONV_HALO), s // CONV_HALO - 1), 0)),
                  pl.BlockSpec((tr, D_MODEL), lambda i: (i, ucol)),
                  pl.BlockSpec((CONV_HALO, D_MODEL), lambda i: (jnp.maximum(i * (tr // CONV_HALO) - 1, 0), ucol)),
                  pl.BlockSpec(convw.shape, lambda i: (0, 0)),
                  pl.BlockSpec(memory_space=pl.ANY)],
        out_specs=[pl.BlockSpec((tr, D_MODEL), lambda i: (i, ucol)), pl.BlockSpec((8 * CONV_HALO, CONV_W), lambda i: (0, 0))],
        out_shape=[SDS(dproj.shape, dproj.dtype), SDS((8 * CONV_HALO, CONV_W), F32)],
        input_output_aliases={5: 0},
        scratch_shapes=[pltpu.VMEM((CONV_HALO + tr, CONV_W), F32)] * 3,
        compiler_params=_params(),
    )(dcv, dcv, proj, proj, convw, dproj)


def _mm_acc(at, b, name, col_slots):
    m, s = at.shape
    n = b.shape[1]
    tk = 512
    nk = s // tk

    def body(a_ref, b_ref, o_ref, acc):
        k = pl.program_id(0)

        @pl.when(k == 0)
        def _():
            acc[...] = jnp.zeros_like(acc)

        acc[...] += _dot(a_ref[...], b_ref[...])

        @pl.when(k == nk - 1)
        def _():
            if col_slots:
                w = n // N_DEV
                for j in range(N_DEV):
                    o_ref[j] = acc[:, j * w:(j + 1) * w].astype(BF16)
            else:
                o_ref[...] = acc[...].astype(BF16)

    if col_slots:
        out_shape = SDS((N_DEV, m, n // N_DEV), BF16)
        out_spec = pl.BlockSpec((N_DEV, m, n // N_DEV), lambda k: (0, 0, 0))
    else:
        out_shape = SDS((m, n), BF16)
        out_spec = pl.BlockSpec((m, n), lambda k: (0, 0))
    return pl.pallas_call(
        body, name=name, grid=(nk,),
        in_specs=[pl.BlockSpec((m, tk), lambda k: (0, k)), pl.BlockSpec((tk, n), lambda k: (k, 0))],
        out_specs=out_spec, out_shape=out_shape, scratch_shapes=[pltpu.VMEM((m, n), F32)],
        compiler_params=_params(),
    )(at, b)


def _mm_dw(ht, dproj):
    s = ht.shape[1]
    tk = 512
    nk = s // tk

    def body(a_ref, b_ref, o_ref, acc):
        k = pl.program_id(1)

        @pl.when(k == 0)
        def _():
            acc[...] = jnp.zeros_like(acc)

        acc[...] += _dot(a_ref[...], b_ref[...])

        @pl.when(k == nk - 1)
        def _():
            o_ref[...] = acc[...].T.astype(BF16)

    return pl.pallas_call(
        body, name="mm_dw", grid=(IN_W // PAIR_W, nk),
        in_specs=[pl.BlockSpec((D_MODEL, tk), lambda p, k: (0, k)), pl.BlockSpec((tk, PAIR_W), lambda p, k: (k, p))],
        out_specs=pl.BlockSpec((PAIR_W, D_MODEL), lambda p, k: (p, 0)),
        out_shape=SDS((IN_W, D_MODEL), BF16), scratch_shapes=[pltpu.VMEM((D_MODEL, PAIR_W), F32)],
        compiler_params=_params(),
    )(ht, dproj)


def _mm_dh_norm_bwd(dproj, wt, x, dout, norm_w, scale, token):
    s = dproj.shape[0]
    tm = 1024
    n_p = IN_W // PAIR_W

    def body(dp_ref, w_ref, x_ref, do_ref, nw_ref, sc_ref, tok_ref, gx_ref, dsh_ref, dsc_ref, dnw_ref, dh_acc):
        del tok_ref
        m, p = pl.program_id(0), pl.program_id(1)
        part = _dot(dp_ref[...], w_ref[...])

        @pl.when(p == 0)
        def _():
            dh_acc[...] = part

        @pl.when(p > 0)
        def _():
            dh_acc[...] += part

        @pl.when((m == 0) & (p == 0))
        def _():
            for r in (dsh_ref, dsc_ref, dnw_ref):
                r[...] = jnp.zeros_like(r)

        @pl.when(p == n_p - 1)
        def _():
            def acc_rows(ref, v):
                ref[...] += jnp.broadcast_to(jnp.sum(v, axis=0, keepdims=True), ref.shape)

            xv = x_ref[...]
            dh_v = dh_acc[...]
            r = lax.rsqrt(jnp.mean(xv * xv, axis=-1, keepdims=True) + EPS)
            xn = xv * r
            one_sc = 1.0 + sc_ref[...]
            acc_rows(dsh_ref, dh_v)
            acc_rows(dsc_ref, dh_v * (xn * nw_ref[...]))
            acc_rows(dnw_ref, dh_v * xn * one_sc)
            dxn = dh_v * (nw_ref[...] * one_sc)
            gx_ref[...] = do_ref[...] + r * (dxn - xn * jnp.mean(dxn * xn, axis=-1, keepdims=True))

    rows = pl.BlockSpec((tm, D_MODEL), lambda m, p: (m, 0))
    vec = pl.BlockSpec((1, D_MODEL), lambda m, p: (0, 0))
    acc = pl.BlockSpec((8, D_MODEL), lambda m, p: (0, 0))
    return pl.pallas_call(
        body, name="mm_dh_norm_bwd", grid=(s // tm, n_p),
        in_specs=[pl.BlockSpec((tm, PAIR_W), lambda m, p: (m, p)),
                  pl.BlockSpec((PAIR_W, D_MODEL), lambda m, p: (p, 0)),
                  rows, rows, vec, vec, pl.BlockSpec(token.shape, lambda m, p: (0, 0))],
        out_specs=[rows, acc, acc, acc],
        out_shape=[SDS((s, D_MODEL), F32)] + [SDS((8, D_MODEL), F32)] * 3,
        scratch_shapes=[pltpu.VMEM((tm, D_MODEL), F32)], compiler_params=_params(),
    )(dproj, wt, x, dout, norm_w, scale, token)


SMALL_ROWS = 8
QN_COL, KN_COL, CB_COL, LOSS_COL = 0, LANE, 2 * LANE, 2 * LANE + CONV_W


def _pack_partials(dsh, dsc, dgate, dnw, dbg, dqw3, dkw3, dcb, dlnw, dlnb, loss_p):
    n3 = len(dqw3)

    def body(*refs):
        dsh_r, dsc_r, dgate_r, dnw_r, dbg_r = refs[:5]
        dq_r, dk_r = refs[5:5 + n3], refs[5 + n3:5 + 2 * n3]
        dcb_r, dlnw_r, dlnb_r, loss_r, o_ref = refs[5 + 2 * n3:]

        def both_heads(rs):
            t = rs[0][0:1, :]
            for r in rs[1:]:
                t = t + r[0:1, :]
            return t + pltpu.roll(t, HEAD_DIM, axis=1)

        o_ref[0:1, :] = dsh_r[0:1, :]
        o_ref[1:2, :] = dsc_r[0:1, :]
        o_ref[2:3, :] = dgate_r[0:1, :]
        o_ref[3:4, :] = dnw_r[0:1, :]
        o_ref[4:5, :] = dbg_r[0:1, 0:D_MODEL]
        o_ref[5:6, :] = dbg_r[0:1, D_MODEL:]
        o_ref[6:7, QN_COL:QN_COL + LANE] = both_heads(dq_r)
        o_ref[6:7, KN_COL:KN_COL + LANE] = both_heads(dk_r)
        o_ref[6:7, CB_COL:CB_COL + CONV_W] = dcb_r[0:1, :]
        o_ref[6:7, LOSS_COL:LOSS_COL + LANE] = loss_r[0:1, :]
        o_ref[6:7, LOSS_COL + LANE:] = jnp.zeros((1, D_MODEL - LOSS_COL - LANE), F32)
        o_ref[7:8, 0:CONV_W] = dlnw_r[0:1, :]
        o_ref[7:8, CONV_W:] = dlnb_r[0:1, :]

    return pl.pallas_call(body, name="pack_partials", out_shape=SDS((SMALL_ROWS, D_MODEL), F32),
                          compiler_params=_params())(dsh, dsc, dgate, dnw, dbg, *dqw3, *dkw3, dcb, dlnw, dlnb, loss_p)


def _adamw_update(g, w, m, v):
    bc1 = 1.0 - ADAM_B1 ** ADAM_STEP
    bc2 = 1.0 - ADAM_B2 ** ADAM_STEP
    m_new = ADAM_B1 * m + (1.0 - ADAM_B1) * g
    v_new = ADAM_B2 * v + (1.0 - ADAM_B2) * (g * g)
    delta = -ADAM_LR * ((m_new / bc1) / (jnp.sqrt(v_new / bc2) + ADAM_EPS) + ADAM_WD * w)
    return delta, m_new, v_new


def _adamw_small(small_all, ws, ms, vs):
    n = len(ws)
    where = [(slice(0, 3), None), (slice(3, 4), None), (slice(4, 6), None), (6, QN_COL), (6, KN_COL), (6, CB_COL),
             (7, 0), (7, CONV_W)]

    def body(*refs):
        g_ref = refs[0]
        w_r, m_r, v_r = refs[1:1 + n], refs[1 + n:1 + 2 * n], refs[1 + 2 * n:1 + 3 * n]
        outs = refs[1 + 3 * n:]
        g_o, d_o, m_o, v_o, loss_o = outs[:n], outs[n:2 * n], outs[2 * n:3 * n], outs[3 * n:4 * n], outs[4 * n]
        gsum = g_ref[0]
        for dev in range(1, N_DEV):
            gsum = gsum + g_ref[dev]
        loss_o[...] = gsum[6:7, LOSS_COL:LOSS_COL + LANE]
        for i, (rows, col) in enumerate(where):
            width = w_r[i].shape[1]
            if col is None:
                g = jnp.concatenate([gsum[r:r + 1, :] for r in range(rows.start, rows.stop)], axis=1)
            else:
                g = gsum[rows:rows + 1, col:col + width]
            delta, m_new, v_new = _adamw_update(g, w_r[i][...], m_r[i][...], v_r[i][...])
            g_o[i][...] = g
            d_o[i][...] = delta
            m_o[i][...] = m_new
            v_o[i][...] = v_new

    shapes = [SDS(w.shape, F32) for w in ws]
    res = pl.pallas_call(body, name="adamw_small", out_shape=shapes * 4 + [SDS((1, LANE), F32)],
                         compiler_params=_params())(small_all, *ws, *ms, *vs)
    return [res[k * n:(k + 1) * n] for k in range(4)], res[4 * n]


def _row_tile(rows):
    if rows <= 128:
        return rows
    return 128 if rows % 128 == 0 else SHARD_W // 4


def _adamw(gsrc, w, m, v, name, stacked):
    rows, cols = w.shape
    tr = _row_tile(rows)
    n_src = len(gsrc) if stacked else 1

    def body(*refs):
        g_refs, (w_ref, m_ref, v_ref, go_ref, d_ref, mo_ref, vo_ref) = refs[:n_src], refs[n_src:]
        if stacked:
            g = None
            for g_ref, (_, slots) in zip(g_refs, gsrc):
                for j in range(slots):
                    t = g_ref[j].astype(F32)
                    g = t if g is None else g + t
        else:
            g = g_refs[0][...]
        delta, m_new, v_new = _adamw_update(g, w_ref[...], m_ref[...], v_ref[...])
        go_ref[...] = g
        d_ref[...] = delta
        mo_ref[...] = m_new
        vo_ref[...] = v_new

    blk = pl.BlockSpec((tr, cols), lambda i: (i, 0))
    if stacked:
        gspecs = [pl.BlockSpec((slots, tr, arr.shape[2]), lambda i: (0, i, 0)) for arr, slots in gsrc]
        gargs = [arr for arr, _ in gsrc]
    else:
        gspecs, gargs = [blk], [gsrc]
    in_specs = gspecs + [blk, blk, blk]
    args = gargs + [w, m, v]
    return pl.pallas_call(
        body, name=name, grid=(rows // tr,), in_specs=in_specs, out_specs=[blk] * 4,
        out_shape=[SDS((rows, cols), F32)] * 4, compiler_params=_params(),
    )(*args)


def kernel(x, c, w_ada, b_ada, norm_w, w_in, b_gate, q_norm_w, k_norm_w, w_attn_proj, conv_w, conv_b, conv_ln_w, conv_ln_b, w_conv_proj, w_out, loss_target, m_w_ada, m_b_ada, m_norm_w, m_w_in, m_b_gate, m_q_norm_w, m_k_norm_w, m_w_attn_proj, m_conv_w, m_conv_b, m_conv_ln_w, m_conv_ln_b, m_w_conv_proj, m_w_out, v_w_ada, v_b_ada, v_norm_w, v_w_in, v_b_gate, v_q_norm_w, v_k_norm_w, v_w_attn_proj, v_conv_w, v_conv_b, v_conv_ln_w, v_conv_ln_b, v_w_conv_proj, v_w_out):
    xi, yi, ci = lax.axis_index("x"), lax.axis_index("y"), lax.axis_index("c")
    me = 4 * xi + 2 * yi + ci
    x2, tgt2 = x[0], loss_target[0]
    w_in_t, m_w_in_t, v_w_in_t = (jnp.transpose(a[0]) for a in (w_in, m_w_in, v_w_in))
    s = x2.shape[0]

    cw_flat = jnp.pad(conv_w[0].reshape(1, -1), ((0, 0), (0, CONVW_FLAT - CONV_K * HEAD_DIM)))
    pre = jnp.concatenate([c, cw_flat], axis=1).reshape(8, -1)
    (pre_all,) = _all_gather([pre], "gather_c_convw", vmem=True)
    pre_all = pre_all.reshape(N_DEV, -1)
    c_all = pre_all[:, :D_MODEL]
    convw_full = pre_all[:, D_MODEL:D_MODEL + CONV_K * HEAD_DIM].reshape(N_DEV, CONV_K, HEAD_DIM)
    convw_full = jnp.transpose(convw_full, (1, 0, 2)).reshape(CONV_K, CONV_W)
    convw_pad = jnp.pad(convw_full, ((0, CONV_HALO - CONV_K), (0, 0)))

    ada_part = _ada_fwd(c_all, w_ada[0])
    (ada_all,) = _all_gather([ada_part], "gather_ada", vmem=True)
    ada = lax.dynamic_index_in_dim(ada_all, me, axis=1, keepdims=False).reshape(1, 3 * D_MODEL) + b_ada
    shift, scale, gate = ada[:, :D_MODEL], ada[:, D_MODEL:2 * D_MODEL], ada[:, 2 * D_MODEL:]

    h, ht = _norm_fwd(x2, norm_w, scale, shift)
    proj, (wt_g, wa_g, wc_g, wo_g) = _gather_mm_in(
        h, [_cast_bf16(w_in_t, "cast_win"), _cast_bf16(w_attn_proj[0], "cast_wa"), _cast_bf16(w_conv_proj[0], "cast_wc"),
            _cast_bf16(w_out[0], "cast_wo")], "gather_mm_in")
    wt = wt_g.reshape(IN_W, D_MODEL)
    wa = _cols_from_slots(wa_g, "cols_wa")
    wc = _cols_from_slots(wc_g, "cols_wc")
    wo = wo_g.reshape(D_MODEL, D_MODEL)
    qw2 = jnp.tile(q_norm_w, (1, 2))
    kw2 = jnp.tile(k_norm_w, (1, 2))
    o_all, l_all, qn, kn, vn = _attn_fwd(proj, qw2, kw2)
    o3, l3 = [o_all] * N_GROUPS, [l_all] * N_GROUPS
    head_id = jnp.arange(ATTN_W) // HEAD_DIM
    bd = (head_id[:, None] == head_id[None, :]).astype(BF16)
    (dout, da, lse_delta, dcv, mt, yat, yct, dmo, dya, dyc, dproj,
     dgate, dbg, dlnw, dlnb, dcb, loss_p) = _tail(
        x2, tgt2, proj, o3, l3, wa, wc, wo, gate, b_gate[:, :D_MODEL], b_gate[:, D_MODEL:], convw_pad,
        conv_b, conv_ln_w, conv_ln_b, bd)

    dproj, dconvw8 = _conv_bwd(dcv, proj, convw_pad, dproj)
    dconvw = jnp.sum(dconvw8.reshape(CONV_HALO, 8, CONV_W), axis=1)
    dproj, dqw_all, dkw_all = _attn_bwd(proj, qn, kn, vn, da, lse_delta, qw2, kw2, dproj)
    dqw_g3, dkw_g3 = [dqw_all], [dkw_all]
    dw_in_p = _mm_dw(ht, dproj).reshape(N_DEV, SHARD_W, D_MODEL)
    dwo_p = _mm_acc(mt, dmo, "mm_dwo", col_slots=False).reshape(N_DEV, D_MODEL // N_DEV, D_MODEL)
    dwa_p = _mm_acc(yat, dya, "mm_dwa", col_slots=True)
    dwc_p = _mm_acc(yct, dyc, "mm_dwc", col_slots=True)

    partials = [dw_in_p, dwa_p, dwc_p, dwo_p]
    me_arr = jnp.reshape(me, (1,)).astype(jnp.int32)
    from_sib = _exchange_sibling(partials, "exchange_sibling")
    presums = [_presum(p, f, me_arr, f"presum{i}") for i, (p, f) in enumerate(zip(partials, from_sib))]
    s_sems, r_sems, pre_thru, land_thru, token = _exchange_chips_start(presums, "exchange_chips_start")
    gx, dsh, dsc, dnw = _mm_dh_norm_bwd(dproj, wt, x2, dout, norm_w, scale, token)
    small_p = _pack_partials(dsh, dsc, dgate, dnw, dbg, dqw_g3, dkw_g3, dcb, dlnw, dlnb, loss_p)
    small_all, dconvw_all = _all_gather([small_p, dconvw], "gather_small", vmem=True)

    small_w = (b_ada, norm_w, b_gate, q_norm_w, k_norm_w, conv_b, conv_ln_w, conv_ln_b)
    small_m = (m_b_ada, m_norm_w, m_b_gate, m_q_norm_w, m_k_norm_w, m_conv_b, m_conv_ln_w, m_conv_ln_b)
    small_v = (v_b_ada, v_norm_w, v_b_gate, v_q_norm_w, v_k_norm_w, v_conv_b, v_conv_ln_w, v_conv_ln_b)
    r_small, loss_row = _adamw_small(small_all, small_w, small_m, small_v)
    dcw_mine = lax.dynamic_slice_in_dim(dconvw_all[:, :CONV_K, :], me * HEAD_DIM, HEAD_DIM, axis=2)
    r_convw = _adamw([(dcw_mine, N_DEV)], conv_w[0], m_conv_w[0], v_conv_w[0], "adamw_conv_w", stacked=True)

    d_ada_all = small_all[:, 0:3, :].reshape(N_DEV, 3 * D_MODEL)
    d_ada_cols = lax.dynamic_slice_in_dim(d_ada_all, me * (3 * D_MODEL // N_DEV), 3 * D_MODEL // N_DEV, axis=1)
    g_wada = _ada_bwd(c_all, d_ada_cols)
    r_ada = _adamw(g_wada, w_ada[0], m_w_ada[0], v_w_ada[0], "adamw_w_ada", stacked=False)
    pres, lands = _exchange_chips_wait(s_sems, r_sems, pre_thru, land_thru, r_ada[1], "exchange_chips_wait")
    terms = [[(p, 1), (l, len(CHIP_K))] for p, l in zip(pres, lands)]
    r_win = [jnp.transpose(r) for r in _adamw(terms[0], w_in_t, m_w_in_t, v_w_in_t, "adamw_w_in", stacked=True)]
    r_wap = _adamw(terms[1], w_attn_proj[0], m_w_attn_proj[0], v_w_attn_proj[0], "adamw_w_attn_proj", stacked=True)
    r_wcp = _adamw(terms[2], w_conv_proj[0], m_w_conv_proj[0], v_w_conv_proj[0], "adamw_w_conv_proj", stacked=True)
    r_wout = _adamw(terms[3], w_out[0], m_w_out[0], v_w_out[0], "adamw_w_out", stacked=True)

    outs = [loss_row[0, 0], gx[None]]
    for k in range(4):
        b_ada_k, norm_w_k, b_gate_k, qn_k, kn_k, conv_b_k, ln_w_k, ln_b_k = r_small[k]
        outs += [r_ada[k][None], b_ada_k, norm_w_k, r_win[k][None], b_gate_k, qn_k, kn_k, r_wap[k][None],
                 r_convw[k][None], conv_b_k, ln_w_k, ln_b_k, r_wcp[k][None], r_wout[k][None]]
    return tuple(outs)
```

```python
import functools

import jax
import jax.numpy as jnp
from jax import lax
from jax.experimental import pallas as pl
from jax.experimental.pallas import tpu as pltpu

F32 = jnp.float32
BF16 = jnp.bfloat16
SDS = jax.ShapeDtypeStruct
MESH = pl.DeviceIdType.MESH

N_DEV = 8
D_MODEL = 1024
HEAD_DIM = 64
N_GROUPS = 3
DILATIONS = (1, 4, 16)
BAND = 128
BWD_UNROLL = 16
ATTN_W = 512
CONV_W = 512
CONV_K = 31
CONV_HALO = 32
IN_W = 8704
SHARD_W = IN_W // N_DEV
PAIR_W = 2 * SHARD_W
Q0, K0, V0, ZA0, U0, ZC0, G0 = 0, 1536, 3072, 4608, 5120, 6144, 6656
EPS = 1e-6
LANE = 128
VMEM_LIMIT = 56 * 1024 * 1024

ADAM_LR, ADAM_B1, ADAM_B2, ADAM_EPS, ADAM_WD, ADAM_STEP = 0.001, 0.9, 0.999, 1e-08, 0.01, 10

CONVW_FLAT = 2048


def _params(**kw):
    return pltpu.CompilerParams(vmem_limit_bytes=VMEM_LIMIT, **kw)


def _sigmoid(z):
    return 0.5 * jnp.tanh(0.5 * z) + 0.5


def _dot(a, b):
    return jnp.dot(a, b, preferred_element_type=F32)


def _dot_nt(a, b):
    return lax.dot_general(a, b, (((1,), (1,)), ((), ())), preferred_element_type=F32)


def _dot_tn(a, b):
    return lax.dot_general(a, b, (((0,), (0,)), ((), ())), preferred_element_type=F32)


def _peer(x, y, c, k):
    px = 1 - x if (k >> 2) & 1 else x
    py = 1 - y if (k >> 1) & 1 else y
    pc = 1 - c if k & 1 else c
    return (px, py, pc), 4 * px + 2 * py + pc


def _all_gather(arrays, name, vmem):
    n = len(arrays)
    space = pltpu.VMEM if vmem else pl.ANY

    def body(*refs):
        ins, outs = refs[:n], refs[n:2 * n]
        send_sems, recv_sems, local_sems = refs[2 * n:]
        x, y, c = lax.axis_index("x"), lax.axis_index("y"), lax.axis_index("c")
        me = 4 * x + 2 * y + c
        locals_ = [pltpu.make_async_copy(ins[a], outs[a].at[me], local_sems.at[a]) for a in range(n)]
        for cp in locals_:
            cp.start()
        sends = []
        for k in range(1, N_DEV):
            peer, _ = _peer(x, y, c, k)
            for a in range(n):
                cp = pltpu.make_async_remote_copy(
                    src_ref=ins[a], dst_ref=outs[a].at[me], send_sem=send_sems.at[a, k - 1],
                    recv_sem=recv_sems.at[a, k - 1], device_id=peer, device_id_type=MESH)
                cp.start()
                sends.append(cp)
        for k in range(1, N_DEV):
            peer, pidx = _peer(x, y, c, k)
            for a in range(n):
                pltpu.make_async_remote_copy(
                    src_ref=ins[a], dst_ref=outs[a].at[pidx], send_sem=send_sems.at[a, k - 1],
                    recv_sem=recv_sems.at[a, k - 1], device_id=peer, device_id_type=MESH).wait_recv()
        for cp in sends:
            cp.wait_send()
        for cp in locals_:
            cp.wait()

    return pl.pallas_call(
        body, name=name,
        out_shape=[SDS((N_DEV,) + a.shape, a.dtype) for a in arrays],
        in_specs=[pl.BlockSpec(memory_space=space)] * n,
        out_specs=[pl.BlockSpec(memory_space=space)] * n,
        scratch_shapes=[pltpu.SemaphoreType.DMA((n, N_DEV - 1)), pltpu.SemaphoreType.DMA((n, N_DEV - 1)),
                        pltpu.SemaphoreType.DMA((n,))],
        compiler_params=_params(),
    )(*arrays)


CHIP_K = (2, 4, 6)


def _all_gather_chips(arrays, name):
    n = len(arrays)
    k_y, k_x, k_d = CHIP_K

    def body(*refs):
        ins, outs = refs[:n], refs[n:2 * n]
        send_sems, recv_sems, local_sems = refs[2 * n:]
        x, y, c = lax.axis_index("x"), lax.axis_index("y"), lax.axis_index("c")
        me = 4 * x + 2 * y + c
        sib, sib_idx = _peer(x, y, c, 1)
        nbr_y, idx_y = _peer(x, y, c, k_y)
        nbr_x, idx_x = _peer(x, y, c, k_x)
        _, idx_d = _peer(x, y, c, k_d)

        def copy(a, slot, block, to, src=None):
            return pltpu.make_async_remote_copy(
                src_ref=outs[a].at[block] if src is None else src, dst_ref=outs[a].at[block],
                send_sem=send_sems.at[a, slot], recv_sem=recv_sems.at[a, slot], device_id=to, device_id_type=MESH)

        locals_ = [pltpu.make_async_copy(ins[a], outs[a].at[me], local_sems.at[a]) for a in range(n)]
        for cp in locals_:
            cp.start()
        for a in range(n):
            copy(a, 0, me, sib, src=ins[a]).start()
            copy(a, 1, me, nbr_y, src=ins[a]).start()
            copy(a, 2, me, nbr_x, src=ins[a]).start()

        def arrived(slot, block, frm, send_on_to=None):
            for a in range(n):
                copy(a, slot, block, frm).wait_recv()
                if send_on_to is not None:
                    copy(a, 3, block, send_on_to).start()
                copy(a, 3 + slot, block, sib).start()

        @pl.when(c == 0)
        def _():
            arrived(1, idx_y, nbr_y, send_on_to=nbr_x)
            arrived(2, idx_x, nbr_x)

        @pl.when(c == 1)
        def _():
            arrived(2, idx_x, nbr_x, send_on_to=nbr_y)
            arrived(1, idx_y, nbr_y)

        arrived(3, idx_d, nbr_x)
        for a in range(n):
            copy(a, 0, sib_idx, sib).wait_recv()
        for slot, k in ((4, k_y), (5, k_x), (6, k_d)):
            _, pidx = _peer(x, y, 1 - c, k)
            for a in range(n):
                copy(a, slot, pidx, sib).wait_recv()
        for slot in range(N_DEV - 1):
            for a in range(n):
                copy(a, slot, me, sib).wait_send()
        for cp in locals_:
            cp.wait()

    return pl.pallas_call(
        body, name=name,
        out_shape=[SDS((N_DEV,) + a.shape, a.dtype) for a in arrays],
        in_specs=[pl.BlockSpec(memory_space=pl.ANY)] * n,
        out_specs=[pl.BlockSpec(memory_space=pl.ANY)] * n,
        scratch_shapes=[pltpu.SemaphoreType.DMA((n, N_DEV - 1)), pltpu.SemaphoreType.DMA((n, N_DEV - 1)),
                        pltpu.SemaphoreType.DMA((n,))],
        compiler_params=_params(),
    )(*arrays)


def _exchange_sibling(arrays, name):
    n = len(arrays)
    ks = (0,) + CHIP_K

    def body(*refs):
        ins, outs = refs[:n], refs[n:2 * n]
        send_sems, recv_sems = refs[2 * n:]
        x, y, c = lax.axis_index("x"), lax.axis_index("y"), lax.axis_index("c")
        sib, sib_idx = _peer(x, y, c, 1)
        sends = []
        for i, k in enumerate(ks):
            _, tgt = _peer(x, y, 1 - c, k) if k else (None, sib_idx)
            for a in range(n):
                cp = pltpu.make_async_remote_copy(
                    src_ref=ins[a].at[tgt], dst_ref=outs[a].at[i], send_sem=send_sems.at[a, i],
                    recv_sem=recv_sems.at[a, i], device_id=sib, device_id_type=MESH)
                cp.start()
                sends.append(cp)
        for cp in sends:
            cp.wait_recv()
        for cp in sends:
            cp.wait_send()

    return pl.pallas_call(
        body, name=name,
        out_shape=[SDS((len(ks),) + a.shape[1:], a.dtype) for a in arrays],
        in_specs=[pl.BlockSpec(memory_space=pl.ANY)] * n,
        out_specs=[pl.BlockSpec(memory_space=pl.ANY)] * n,
        scratch_shapes=[pltpu.SemaphoreType.DMA((n, len(ks))), pltpu.SemaphoreType.DMA((n, len(ks)))],
        compiler_params=_params(),
    )(*arrays)


def _presum(mine, from_sib, me_arr, name):
    _, rows, cols = mine.shape
    tr = _row_tile(rows)
    ns = 1 + len(CHIP_K)

    def body(me_ref, a_ref, b_ref, o_ref):
        del me_ref
        o_ref[...] = (a_ref[...].astype(F32) + b_ref[...].astype(F32)).astype(o_ref.dtype)

    grid_spec = pltpu.PrefetchScalarGridSpec(
        num_scalar_prefetch=1, grid=(ns, rows // tr),
        in_specs=[pl.BlockSpec((1, tr, cols), lambda j, i, me: (jnp.bitwise_xor(me[0], 2 * j), i, 0)),
                  pl.BlockSpec((1, tr, cols), lambda j, i, me: (j, i, 0))],
        out_specs=pl.BlockSpec((1, tr, cols), lambda j, i, me: (j, i, 0)))
    return pl.pallas_call(body, name=name, grid_spec=grid_spec, out_shape=SDS((ns, rows, cols), mine.dtype),
                          compiler_params=_params())(me_arr, mine, from_sib)


HBM_SPEC = pl.BlockSpec(memory_space=pltpu.HBM)
SEM_SPEC = pl.BlockSpec(memory_space=pltpu.SEMAPHORE)
SIDE_EFFECT = pltpu.SideEffectType.DATAFLOW_SIDE_EFFECTING


def _chips_copies(pre_refs, land_refs, send_sems, recv_sems):
    x, y, c = lax.axis_index("x"), lax.axis_index("y"), lax.axis_index("c")
    copies = []
    for j, k in enumerate(CHIP_K):
        peer, _ = _peer(x, y, c, k)
        for a in range(len(pre_refs)):
            copies.append(pltpu.make_async_remote_copy(
                src_ref=pre_refs[a].at[1 + j], dst_ref=land_refs[a].at[j], send_sem=send_sems.at[a * len(CHIP_K) + j],
                recv_sem=recv_sems.at[a * len(CHIP_K) + j], device_id=peer, device_id_type=MESH))
    return copies


def _exchange_chips_start(presums, name):
    n = len(presums)

    def body(*refs):
        pre, land = refs[:n], refs[n:2 * n]
        send_sems, recv_sems = refs[2 * n], refs[2 * n + 1]
        token = refs[-1]
        for cp in _chips_copies(pre, land, send_sems, recv_sems):
            cp.start()
        token[...] = jnp.zeros_like(token)

    nk = len(CHIP_K)
    hbm = [pltpu.HBM(p.shape, p.dtype) for p in presums]
    hbm_land = [pltpu.HBM((nk,) + p.shape[1:], p.dtype) for p in presums]
    res = pl.pallas_call(
        body, name=name,
        out_shape=(pltpu.SemaphoreType.DMA((n * nk,)), pltpu.SemaphoreType.DMA((n * nk,)), *hbm, *hbm_land, SDS((8, LANE), F32)),
        in_specs=[HBM_SPEC] * (2 * n),
        out_specs=(SEM_SPEC, SEM_SPEC, *([HBM_SPEC] * (2 * n)), pl.BlockSpec(memory_space=pltpu.VMEM)),
        input_output_aliases={i: 2 + i for i in range(2 * n)},
        compiler_params=pltpu.CompilerParams(has_side_effects=SIDE_EFFECT),
    )(*[pltpu.with_memory_space_constraint(p, pltpu.HBM) for p in presums],
      *[pltpu.with_memory_space_constraint(lax.empty((nk,) + p.shape[1:], p.dtype), pltpu.HBM) for p in presums])
    return res[0], res[1], res[2:2 + n], res[2 + n:2 + 2 * n], res[-1]


def _exchange_chips_wait(send_sems, recv_sems, pre_thru, land_thru, after, name):
    n = len(pre_thru)

    def body(*refs):
        pre, land = refs[:n], refs[n:2 * n]
        s_sems, r_sems = refs[2 * n], refs[2 * n + 1]
        for cp in _chips_copies(pre, land, s_sems, r_sems):
            cp.wait_send()
            cp.wait_recv()

    hbm = [pltpu.HBM(p.shape, p.dtype) for p in (*pre_thru, *land_thru)]
    res = pl.pallas_call(
        body, name=name, out_shape=tuple(hbm),
        in_specs=[HBM_SPEC] * (2 * n) + [SEM_SPEC, SEM_SPEC, pl.BlockSpec(memory_space=pl.ANY)],
        out_specs=tuple([HBM_SPEC] * (2 * n)),
        input_output_aliases={i: i for i in range(2 * n)},
        compiler_params=pltpu.CompilerParams(has_side_effects=SIDE_EFFECT),
    )(*pre_thru, *land_thru, send_sems, recv_sems, after)
    return res[:n], res[n:]


def _exchange_chips(presums, name):
    n = len(presums)
    nk = len(CHIP_K)

    def body(*refs):
        pre, land = refs[:n], refs[n:2 * n]
        send_sems, recv_sems = refs[2 * n:]
        copies = _chips_copies(pre, land, send_sems, recv_sems)
        for cp in copies:
            cp.start()
        for cp in copies:
            cp.wait_recv()
        for cp in copies:
            cp.wait_send()

    return pl.pallas_call(
        body, name=name,
        out_shape=[SDS((nk,) + p.shape[1:], p.dtype) for p in presums],
        in_specs=[pl.BlockSpec(memory_space=pl.ANY)] * n,
        out_specs=[pl.BlockSpec(memory_space=pl.ANY)] * n,
        scratch_shapes=[pltpu.SemaphoreType.DMA((n * nk,)), pltpu.SemaphoreType.DMA((n * nk,))],
        compiler_params=_params(),
    )(*presums)


def _cast_bf16(w, name):
    def body(w_ref, o_ref):
        o_ref[...] = w_ref[...].astype(BF16)

    return pl.pallas_call(body, name=name, out_shape=SDS(w.shape, BF16), compiler_params=_params())(w)


def _cols_from_slots(wg, name):
    _, rows, cols = wg.shape

    def body(w_ref, o_ref):
        for j in range(N_DEV):
            o_ref[:, j * cols:(j + 1) * cols] = w_ref[j]

    return pl.pallas_call(body, name=name, out_shape=SDS((rows, N_DEV * cols), wg.dtype), compiler_params=_params())(wg)


def _ada_fwd(c_all, w_ada):
    def body(c_ref, w_ref, o_ref):
        cv = c_ref[...]
        sc = (cv * _sigmoid(cv)).astype(BF16)
        o_ref[...] = _dot(sc, w_ref[...].astype(BF16))

    return pl.pallas_call(body, name="ada_fwd", out_shape=SDS((N_DEV, w_ada.shape[1]), F32),
                          compiler_params=_params())(c_all, w_ada)


def _ada_bwd(c_all, d_ada_cols):
    def body(c_ref, d_ref, o_ref):
        cv = c_ref[...]
        sc = (cv * _sigmoid(cv)).astype(BF16)
        o_ref[...] = _dot_tn(sc, d_ref[...].astype(BF16))

    return pl.pallas_call(body, name="ada_bwd", out_shape=SDS((D_MODEL, d_ada_cols.shape[1]), F32),
                          compiler_params=_params())(c_all, d_ada_cols)


def _norm_fwd(x, norm_w, scale, shift):
    s = x.shape[0]
    tr = 512

    def body(x_ref, nw_ref, sc_ref, sh_ref, h_ref, ht_ref):
        xv = x_ref[...]
        r = lax.rsqrt(jnp.mean(xv * xv, axis=-1, keepdims=True) + EPS)
        h = (xv * r * nw_ref[...]) * (1.0 + sc_ref[...]) + sh_ref[...]
        h_ref[...] = h.astype(BF16)
        ht_ref[...] = h.T.astype(BF16)

    vec = pl.BlockSpec((1, D_MODEL), lambda i: (0, 0))
    return pl.pallas_call(
        body, name="norm_fwd", grid=(s // tr,),
        in_specs=[pl.BlockSpec((tr, D_MODEL), lambda i: (i, 0)), vec, vec, vec],
        out_specs=[pl.BlockSpec((tr, D_MODEL), lambda i: (i, 0)), pl.BlockSpec((D_MODEL, tr), lambda i: (0, i))],
        out_shape=[SDS((s, D_MODEL), BF16), SDS((D_MODEL, s), BF16)], compiler_params=_params(),
    )(x, norm_w, scale, shift)


def _mm_in(h, wt):
    s = h.shape[0]
    tm = 512

    def body(h_ref, w_ref, o_ref):
        o_ref[...] = _dot_nt(h_ref[...], w_ref[...])

    return pl.pallas_call(
        body, name="mm_in", grid=(IN_W // PAIR_W, s // tm),
        in_specs=[pl.BlockSpec((tm, D_MODEL), lambda p, m: (m, 0)),
                  pl.BlockSpec((PAIR_W, D_MODEL), lambda p, m: (p, 0))],
        out_specs=pl.BlockSpec((tm, PAIR_W), lambda p, m: (m, p)),
        out_shape=SDS((s, IN_W), F32), compiler_params=_params(),
    )(h, wt)


def _head_ones():
    a = lax.broadcasted_iota(jnp.int32, (LANE, LANE), 0) // HEAD_DIM
    b = lax.broadcasted_iota(jnp.int32, (LANE, LANE), 1) // HEAD_DIM
    return (a == b).astype(BF16)


def _head_sums(t, ones):
    return _dot(t.astype(BF16), ones)


def _band_bias(bias, transposed=False):
    qi = lax.broadcasted_iota(jnp.int32, (2 * BAND, 2 * BAND), 1 if transposed else 0) % BAND
    kj = lax.broadcasted_iota(jnp.int32, (2 * BAND, 2 * BAND), 0 if transposed else 1)
    dist = qi + BAND - kj
    valid = (dist >= 0) & (dist <= BAND)
    bias[1] = jnp.where(valid, 0.0, -1e30)
    bias[0] = jnp.where(valid & (kj >= BAND), 0.0, -1e30)


def _token_rows(j, d, chunk, per_r):
    return pl.ds(j // per_r + (j % per_r) * (chunk * d), chunk, stride=d)


def _deinterleave(src_ref, dst_ref, w_ref, ones, d, sub_len, chunk, scale, dst_off):
    per_r = sub_len // chunk

    def step(j, _):
        t = src_ref[_token_rows(j, d, chunk, per_r), :]
        if w_ref is not None:
            ms = _head_sums(t * t, ones) * (1.0 / HEAD_DIM)
            t = t * lax.rsqrt(ms + EPS) * (w_ref[...] * scale)
        dst_ref[pl.ds(pl.multiple_of(dst_off + j * chunk, BAND), chunk), :] = t.astype(dst_ref.dtype)
        return 0
    lax.fori_loop(0, d * per_r, step, 0, unroll=4)


N_PAIRS = ATTN_W // LANE


def _attn_fwd(proj, qw2, kw2):
    s = proj.shape[0]

    def group_body(g, step, q_ref, k_ref, v_ref, qw_ref, kw_ref, o_ref, l_ref, qn_ref, kn_ref, vn_ref,
                   qd, kd, vd, od, ld, bias):
        d = DILATIONS[g]
        sub_len = s // d
        nb = sub_len // BAND
        chunk = min(sub_len, 256)
        lo = lax.broadcasted_iota(jnp.int32, (1, LANE), 1) < HEAD_DIM
        ones = _head_ones()

        @pl.when(step == 0)
        def _():
            _band_bias(bias)

        kd[0:BAND, :] = jnp.zeros((BAND, LANE), BF16)
        vd[0:BAND, :] = jnp.zeros((BAND, LANE), BF16)
        _deinterleave(q_ref, qd, qw_ref, ones, d, sub_len, chunk, HEAD_DIM ** -0.5, 0)
        _deinterleave(k_ref, kd, kw_ref, ones, d, sub_len, chunk, 1.0, BAND)
        _deinterleave(v_ref, vd, None, ones, d, sub_len, chunk, 1.0, BAND)
        qn_ref[...] = qd[...]
        kn_ref[...] = kd[BAND:BAND + s, :]
        vn_ref[...] = vd[BAND:BAND + s, :]

        def block(t, _):
            base = pl.multiple_of(t * BAND, BAND)
            q = qd[pl.ds(base, BAND), :]
            k2 = kd[pl.ds(base, 2 * BAND), :]
            v2 = vd[pl.ds(base, 2 * BAND), :]
            zero = jnp.zeros_like(q)
            qs = jnp.concatenate([jnp.where(lo, q, zero), jnp.where(lo, zero, q)], axis=0)
            sc = _dot_nt(qs, k2) + bias[jnp.minimum(t % nb, 1)]
            m = jnp.max(sc, axis=-1, keepdims=True)
            p = jnp.exp(sc - m)
            den = jnp.sum(p, axis=-1, keepdims=True)
            u = _dot(p.astype(BF16), v2) * (1.0 / den)
            lse = m + jnp.log(den)
            od[pl.ds(base, BAND), :] = jnp.where(lo, u[:BAND], u[BAND:])
            ld[pl.ds(base, BAND), :] = jnp.where(lo, lse[:BAND], lse[BAND:])
            return 0
        lax.fori_loop(0, s // BAND, block, 0, unroll=16)

        per_r = sub_len // chunk

        def back(j, _):
            src = pl.ds(pl.multiple_of(j * chunk, chunk), chunk)
            dst = _token_rows(j, d, chunk, per_r)
            o_ref[dst, :] = od[src, :]
            l_ref[dst, :] = ld[src, :]
            return 0
        lax.fori_loop(0, d * per_r, back, 0, unroll=2)

    def body(*refs):
        step = pl.program_id(0)
        for g in range(N_GROUPS):
            pl.when(step // N_PAIRS == g)(functools.partial(group_body, g, step, *refs))

    col = lambda off: pl.BlockSpec((s, LANE), lambda i, off=off: (0, off // LANE + i))
    vec = pl.BlockSpec((1, LANE), lambda i: (0, 0))
    out = pl.BlockSpec((s, LANE), lambda i: (0, i))
    width = N_GROUPS * ATTN_W
    return pl.pallas_call(
        body, name="attn_fwd", grid=(N_GROUPS * N_PAIRS,),
        in_specs=[col(Q0), col(K0), col(V0), vec, vec], out_specs=[out] * 5,
        out_shape=[SDS((s, width), F32)] * 2 + [SDS((s, width), BF16)] * 3,
        scratch_shapes=[pltpu.VMEM((s, LANE), BF16), pltpu.VMEM((s + BAND, LANE), BF16), pltpu.VMEM((s + BAND, LANE), BF16),
                        pltpu.VMEM((s, LANE), F32), pltpu.VMEM((s, LANE), F32),
                        pltpu.VMEM((2, 2 * BAND, 2 * BAND), F32)],
        compiler_params=_params(),
    )(proj, proj, proj, qw2, kw2)


def _attn_bwd(proj, qn, kn, vn, da, lse_delta, qw2, kw2, dproj):
    s = proj.shape[0]
    n_steps = N_GROUPS * N_PAIRS

    def group_body(g, hp, q_ref, k_ref, qn_ref, kn_ref, vn_ref, da_ref, ld_ref, qw_ref, kw_ref, dp_in, dp_out,
                   dqw_ref, dkw_ref, kd, vd, kdt, dad, lst, dlt, dqt, dqd, dkd, dvd, st, stb, bias_t, wacc, sem):
        del dp_in
        d = DILATIONS[g]
        sub_len = s // d
        nb = sub_len // BAND
        chunk = min(sub_len, 256)
        lo = lax.broadcasted_iota(jnp.int32, (1, LANE), 1) < HEAD_DIM
        row_lo = lax.broadcasted_iota(jnp.int32, (LANE, 1), 0) < HEAD_DIM
        ones = _head_ones()
        per_r = sub_len // chunk
        cblk = chunk // BAND

        @pl.when(hp == 0)
        def _():
            _band_bias(bias_t, transposed=True)

        kd[0:BAND, :] = jnp.zeros((BAND, LANE), BF16)
        vd[0:BAND, :] = jnp.zeros((BAND, LANE), BF16)
        kdt[0] = jnp.zeros((LANE, BAND), BF16)
        kd[BAND:BAND + s, :] = kn_ref[...]
        vd[BAND:BAND + s, :] = vn_ref[...]

        def k_step(t, _):
            kdt[1 + t] = kn_ref[pl.ds(pl.multiple_of(t * BAND, BAND), BAND), :].astype(F32).T.astype(BF16)
            return 0
        lax.fori_loop(0, s // BAND, k_step, 0, unroll=4)
        _deinterleave(da_ref, dad, None, ones, d, sub_len, chunk, 1.0, 0)

        def rows_step(j, _):
            tok = _token_rows(j, d, chunk, per_r)
            tt = ld_ref[tok, :].T
            for u in range(cblk):
                cols = slice(u * BAND, (u + 1) * BAND)
                lst[j * cblk + u, 0:1, :] = tt[0:1, cols]
                lst[j * cblk + u, 1:2, :] = tt[HEAD_DIM:HEAD_DIM + 1, cols]
                dlt[j * cblk + u, 0:1, :] = tt[HEAD_DIM // 2:HEAD_DIM // 2 + 1, cols]
                dlt[j * cblk + u, 1:2, :] = tt[HEAD_DIM + HEAD_DIM // 2:HEAD_DIM + HEAD_DIM // 2 + 1, cols]
            return 0
        lax.fori_loop(0, d * per_r, rows_step, 0, unroll=4)

        def block(t, carry):
            ck, cv = carry
            base = pl.multiple_of(t * BAND, BAND)
            q = qn_ref[pl.ds(base, BAND), :]
            k2 = kd[pl.ds(base, 2 * BAND), :]
            v2 = vd[pl.ds(base, 2 * BAND), :]
            k2t = jnp.concatenate([kdt[t], kdt[t + 1]], axis=1)
            dav = dad[pl.ds(base, BAND), :]
            zero = jnp.zeros_like(q)
            qs = jnp.concatenate([jnp.where(lo, q, zero), jnp.where(lo, zero, q)], axis=0)
            das = jnp.concatenate([jnp.where(lo, dav, zero), jnp.where(lo, zero, dav)], axis=0)
            ls_row = jnp.concatenate([lst[t, 0:1, :], lst[t, 1:2, :]], axis=1)
            dl_row = jnp.concatenate([dlt[t, 0:1, :], dlt[t, 1:2, :]], axis=1)
            sc_t = _dot_nt(k2, qs) + bias_t[jnp.minimum(t % nb, 1)]
            p_t = jnp.exp(sc_t - ls_row)
            dp_t = _dot_nt(v2, das)
            ds_t = (p_t * (dp_t - dl_row)).astype(BF16)
            dv2 = _dot(p_t.astype(BF16), das)
            dk2 = _dot(ds_t, qs)
            dvd[pl.ds(base, BAND), :] = cv + dv2[:BAND]
            dkd[pl.ds(base, BAND), :] = ck + dk2[:BAND]
            dq_t = _dot(k2t, ds_t)
            dqt[t] = jnp.where(row_lo, dq_t[:, :BAND], dq_t[:, BAND:])
            return dk2[BAND:], dv2[BAND:]

        def blocks(i, carry):
            for u in range(BWD_UNROLL):
                carry = block(i * BWD_UNROLL + u, carry)
            return carry
        zeros = jnp.zeros((BAND, LANE), F32)
        ck, cv = lax.fori_loop(0, s // (BAND * BWD_UNROLL), blocks, (zeros, zeros))
        dkd[s:s + BAND, :] = ck
        dvd[s:s + BAND, :] = cv

        def dq_rows(t, _):
            dqd[pl.ds(pl.multiple_of(t * BAND, BAND), BAND), :] = dqt[t].T
            return 0
        lax.fori_loop(0, s // BAND, dq_rows, 0, unroll=4)

        def col_copy(slot, col0):
            return pltpu.make_async_copy(
                stb.at[slot], dp_out.at[:, pl.ds(pl.multiple_of(col0 + LANE * hp, LANE), LANE)], sem.at[slot])

        def store_cols(slot, col0):
            @pl.when(hp > 0)
            def _():
                col_copy(slot, col0).wait()
            stb[slot] = st[...].astype(BF16)
            col_copy(slot, col0).start()

        def norm_back(src_ref, dy_ref, dy_off, w_ref, scale, dw_ref, slot, col0):
            wacc[...] = jnp.zeros_like(wacc)

            def step(j, _):
                tok = _token_rows(j, d, chunk, per_r)
                t = src_ref[tok, :]
                dy = dy_ref[pl.ds(pl.multiple_of(dy_off + j * chunk, BAND), chunk), :]
                rr = lax.rsqrt(_head_sums(t * t, ones) * (1.0 / HEAD_DIM) + EPS)
                nrm = t * rr
                wacc[...] += jnp.sum((dy * nrm).reshape(chunk // 8, 8, LANE), axis=0)
                dn = dy * (w_ref[...] * scale)
                st[tok, :] = rr * (dn - nrm * (_head_sums(dn * nrm, ones) * (1.0 / HEAD_DIM)))
                return 0
            lax.fori_loop(0, d * per_r, step, 0, unroll=4)
            dw_ref[...] += jnp.broadcast_to(jnp.sum(wacc[...], axis=0, keepdims=True) * scale, dw_ref.shape)
            store_cols(slot, col0)

        @pl.when(hp == 0)
        def _():
            dqw_ref[...] = jnp.zeros_like(dqw_ref)
            dkw_ref[...] = jnp.zeros_like(dkw_ref)

        norm_back(q_ref, dqd, 0, qw_ref, HEAD_DIM ** -0.5, dqw_ref, 0, Q0)
        norm_back(k_ref, dkd, BAND, kw_ref, 1.0, dkw_ref, 1, K0)

        def v_back(j, _):
            src = pl.ds(pl.multiple_of(BAND + j * chunk, BAND), chunk)
            st[_token_rows(j, d, chunk, per_r), :] = dvd[src, :]
            return 0
        lax.fori_loop(0, d * per_r, v_back, 0, unroll=2)
        store_cols(2, V0)

        @pl.when(hp == n_steps - 1)
        def _():
            for slot, col0 in enumerate((Q0, K0, V0)):
                col_copy(slot, col0).wait()

    def body(*refs):
        step = pl.program_id(0)
        for g in range(N_GROUPS):
            pl.when(step // N_PAIRS == g)(functools.partial(group_body, g, step, *refs))

    col = lambda off: pl.BlockSpec((s, LANE), lambda i, off=off: (0, off // LANE + i))
    mid = pl.BlockSpec((s, LANE), lambda i: (0, i))
    slot4 = pl.BlockSpec((s, LANE), lambda i: (0, i % N_PAIRS))
    vec = pl.BlockSpec((1, LANE), lambda i: (0, 0))
    acc = pl.BlockSpec((8, LANE), lambda i: (0, 0))
    any_ = pl.BlockSpec(memory_space=pl.ANY)
    return pl.pallas_call(
        body, name="attn_bwd", grid=(n_steps,),
        in_specs=[col(Q0), col(K0), mid, mid, mid, slot4, slot4, vec, vec, any_],
        out_specs=[any_, acc, acc],
        out_shape=[SDS(dproj.shape, dproj.dtype), SDS((8, LANE), F32), SDS((8, LANE), F32)],
        input_output_aliases={9: 0},
        scratch_shapes=[pltpu.VMEM((s + BAND, LANE), BF16), pltpu.VMEM((s + BAND, LANE), BF16),
                        pltpu.VMEM((s // BAND + 1, LANE, BAND), BF16), pltpu.VMEM((s, LANE), BF16),
                        pltpu.VMEM((s // BAND, 8, BAND), F32), pltpu.VMEM((s // BAND, 8, BAND), F32),
                        pltpu.VMEM((s // BAND, LANE, BAND), F32),
                        pltpu.VMEM((s, LANE), F32), pltpu.VMEM((s + BAND, LANE), F32), pltpu.VMEM((s + BAND, LANE), F32),
                        pltpu.VMEM((s, LANE), F32), pltpu.VMEM((3, s, LANE), BF16),
                        pltpu.VMEM((2, 2 * BAND, 2 * BAND), F32), pltpu.VMEM((8, LANE), F32),
                        pltpu.SemaphoreType.DMA((3,))],
        compiler_params=_params(),
    )(proj, proj, qn, kn, vn, da, lse_delta, qw2, kw2, dproj)


def _tap_views(ext_ref, sh_ref, offsets, tr, cols):
    for b in range(8):
        group = [j for j, o in enumerate(offsets) if o % 8 == b]
        if not group:
            continue
        first = min(offsets[j] for j in group)
        span = tr + max(offsets[j] for j in group) - first
        sh_ref[0:span, cols] = ext_ref[first:first + span, cols]
        for j in group:
            yield j, sh_ref[offsets[j] - first:offsets[j] - first + tr, cols]


def _silu_grad(z, sg):
    return sg * (1.0 + z * (1.0 - sg))


def _glu(u):
    a_h, b_h = u[:, :CONV_W], u[:, CONV_W:]
    sg = _sigmoid(b_h)
    return a_h, sg, a_h * sg


def _tail(x, tgt, proj, o3, l3, wa, wc, wo, gate, bga, bgc, convw, convb, lnw, lnb, bd):
    s = x.shape[0]
    tr = 256

    def body(x_ref, t_ref, za_ref, u_ref, uh_ref, zc_ref, g0_ref, g1_ref, g2_ref, g3_ref,
             o0_ref, o1_ref, o2_ref, l0_ref, l1_ref, l2_ref, wa_ref, wc_ref, wo_ref,
             gate_ref, bga_ref, bgc_ref, cw_ref, cb_ref, lnw_ref, lnb_ref, bd_ref,
             dout_ref, da_ref, ld_ref, dcv_ref, mt_ref, yat_ref, yct_ref, dmo_ref, dya_ref, dyc_ref, dp_ref,
             dgate_ref, dbg_ref, dlnw_ref, dlnb_ref, dcb_ref, loss_ref,
             ext, sh, st_za, st_zc, st_g, sems):
        i = pl.program_id(0)

        @pl.when(i == 0)
        def _():
            for r in (dgate_ref, dbg_ref, dlnw_ref, dlnb_ref, dcb_ref, loss_ref):
                r[...] = jnp.zeros_like(r)

        def acc_rows(ref, v):
            ref[...] += jnp.broadcast_to(jnp.sum(v, axis=0, keepdims=True), ref.shape)

        la, lb, lc = l0_ref[...], l1_ref[...], l2_ref[...]
        mx = jnp.maximum(jnp.maximum(la, lb), lc)
        ea, eb, ec = jnp.exp(la - mx), jnp.exp(lb - mx), jnp.exp(lc - mx)
        den = ea + eb + ec
        inv = 1.0 / den
        attn = (ea * inv) * o0_ref[...] + (eb * inv) * o1_ref[...] + (ec * inv) * o2_ref[...]
        lse = mx + jnp.log(den)

        za = za_ref[...]
        sga = _sigmoid(za)
        sa = za * sga
        ya_in = attn * sa
        y_attn = _dot(ya_in.astype(BF16), wa_ref[...])

        _, _, glu = _glu(u_ref[...])
        _, _, glu_h = _glu(uh_ref[...])
        ext[0:CONV_HALO, :] = jnp.where(i > 0, glu_h, 0.0)
        ext[CONV_HALO:CONV_HALO + tr, :] = glu
        cv_blocks = []
        for cb in range(CONV_W // LANE):
            cols = slice(cb * LANE, (cb + 1) * LANE)
            cv_c = jnp.broadcast_to(cb_ref[:, cols], (tr, LANE))
            for j, rows in _tap_views(ext, sh, [CONV_HALO - (CONV_K - 1) + j for j in range(CONV_K)], tr, cols):
                cv_c = cv_c + cw_ref[j:j + 1, cols] * rows
            cv_blocks.append(cv_c)
        cv = jnp.concatenate(cv_blocks, axis=1)
        mu = jnp.mean(cv, axis=-1, keepdims=True)
        xc = cv - mu
        rstd = lax.rsqrt(jnp.mean(xc * xc, axis=-1, keepdims=True) + EPS)
        nrm = xc * rstd
        ln = nrm * lnw_ref[...] + lnb_ref[...]
        sgl = _sigmoid(ln)
        cs = ln * sgl
        zc = zc_ref[...]
        sgc = _sigmoid(zc)
        scz = zc * sgc
        yc_in = cs * scz
        y_conv = _dot(yc_in.astype(BF16), wc_ref[...])

        ga = _sigmoid(jnp.concatenate([g0_ref[...], g1_ref[...]], axis=1) + bga_ref[...])
        gc = _sigmoid(jnp.concatenate([g2_ref[...], g3_ref[...]], axis=1) + bgc_ref[...])
        merged = ga * y_attn + gc * y_conv
        mo = _dot(merged.astype(BF16), wo_ref[...])
        gate_v = gate_ref[...]
        err = (x_ref[...] + gate_v * mo) - t_ref[...]
        loss_ref[...] += 0.5 * jnp.sum(jnp.mean(err * err, axis=-1, keepdims=True))
        d_out = err * (1.0 / D_MODEL)
        dout_ref[...] = d_out

        rows = pl.ds(pl.multiple_of(i * tr, tr), tr)
        cps = [pltpu.make_async_copy(st_za, dp_ref.at[rows, pl.ds(ZA0, ATTN_W)], sems.at[0]),
               pltpu.make_async_copy(st_zc, dp_ref.at[rows, pl.ds(ZC0, CONV_W)], sems.at[1]),
               pltpu.make_async_copy(st_g, dp_ref.at[rows, pl.ds(G0, 2 * D_MODEL)], sems.at[2])]

        @pl.when(i > 0)
        def _():
            for cp in cps:
                cp.wait()

        acc_rows(dgate_ref, d_out * mo)
        dmo_b = (d_out * gate_v).astype(BF16)
        dmo_ref[...] = dmo_b
        mt_ref[...] = merged.T.astype(BF16)
        d_merged = _dot_nt(dmo_b, wo_ref[...])
        d_ya = (d_merged * ga).astype(BF16)
        d_yc = (d_merged * gc).astype(BF16)
        dya_ref[...] = d_ya
        dyc_ref[...] = d_yc
        dga = d_merged * y_attn * (ga * (1.0 - ga))
        dgc = d_merged * y_conv * (gc * (1.0 - gc))
        dgs = jnp.concatenate([dga, dgc], axis=1)
        acc_rows(dbg_ref, dgs)
        st_g[...] = dgs.astype(BF16)

        yat_ref[...] = ya_in.T.astype(BF16)
        d_ya_in = _dot_nt(d_ya, wa_ref[...])
        d_attn = d_ya_in * sa
        da_ref[...] = d_attn
        st_za[...] = (d_ya_in * attn * _silu_grad(za, sga)).astype(BF16)
        prod = d_attn * attn
        hi = prod.astype(BF16)
        lo_ = (prod - hi.astype(F32)).astype(BF16)
        delta = _dot(hi, bd_ref[...]) + _dot(lo_, bd_ref[...])
        first_half = (lax.broadcasted_iota(jnp.int32, (1, ATTN_W), 1) % HEAD_DIM) < HEAD_DIM // 2
        ld_ref[...] = jnp.where(first_half, lse, delta)

        yct_ref[...] = yc_in.T.astype(BF16)
        d_yc_in = _dot_nt(d_yc, wc_ref[...])
        st_zc[...] = (d_yc_in * cs * _silu_grad(zc, sgc)).astype(BF16)
        d_ln = (d_yc_in * scz) * _silu_grad(ln, sgl)
        acc_rows(dlnw_ref, d_ln * nrm)
        acc_rows(dlnb_ref, d_ln)
        d_nrm = d_ln * lnw_ref[...]
        d_cv = rstd * (d_nrm - jnp.mean(d_nrm, axis=-1, keepdims=True)
                       - nrm * jnp.mean(d_nrm * nrm, axis=-1, keepdims=True))
        acc_rows(dcb_ref, d_cv)
        dcv_ref[...] = d_cv

        for cp in cps:
            cp.start()

        @pl.when(i == s // tr - 1)
        def _():
            for cp in cps:
                cp.wait()

    def rows(width, colblk=0):
        return pl.BlockSpec((tr, width), lambda i, colblk=colblk: (i, colblk))

    def const(shape):
        return pl.BlockSpec(shape, lambda i: (0,) * len(shape))

    halo = pl.BlockSpec((CONV_HALO, D_MODEL), lambda i: (jnp.maximum(i * (tr // CONV_HALO) - 1, 0), U0 // D_MODEL))
    in_specs = [rows(D_MODEL), rows(D_MODEL), rows(ATTN_W, ZA0 // ATTN_W), rows(D_MODEL, U0 // D_MODEL), halo,
                rows(CONV_W, ZC0 // CONV_W)]
    in_specs += [rows(512, G0 // 512 + j) for j in range(4)]
    in_specs += [rows(ATTN_W, g) for g in range(N_GROUPS)] * 2
    in_specs += [const(wa.shape), const(wc.shape), const(wo.shape), const((1, D_MODEL)), const((1, D_MODEL)),
                 const((1, D_MODEL)), const(convw.shape), const((1, CONV_W)), const((1, CONV_W)), const((1, CONV_W)),
                 const(bd.shape)]
    tcol = lambda width: pl.BlockSpec((width, tr), lambda i: (0, i))
    out_specs = [rows(D_MODEL), rows(ATTN_W), rows(ATTN_W), rows(CONV_W),
                 tcol(D_MODEL), tcol(ATTN_W), tcol(CONV_W), rows(D_MODEL), rows(D_MODEL), rows(D_MODEL),
                 pl.BlockSpec(memory_space=pl.ANY),
                 const((8, D_MODEL)), const((8, 2 * D_MODEL)), const((8, CONV_W)), const((8, CONV_W)), const((8, CONV_W)),
                 const((8, LANE))]
    out_shape = [SDS((s, D_MODEL), F32), SDS((s, ATTN_W), F32), SDS((s, ATTN_W), F32),
                 SDS((s, CONV_W), F32),
                 SDS((D_MODEL, s), BF16), SDS((ATTN_W, s), BF16), SDS((CONV_W, s), BF16),
                 SDS((s, D_MODEL), BF16), SDS((s, D_MODEL), BF16), SDS((s, D_MODEL), BF16),
                 SDS((s, IN_W), BF16),
                 SDS((8, D_MODEL), F32), SDS((8, 2 * D_MODEL), F32), SDS((8, CONV_W), F32), SDS((8, CONV_W), F32),
                 SDS((8, CONV_W), F32), SDS((8, LANE), F32)]
    return pl.pallas_call(
        body, name="tail", grid=(s // tr,), in_specs=in_specs, out_specs=out_specs, out_shape=out_shape,
        scratch_shapes=[pltpu.VMEM((CONV_HALO + tr, CONV_W), F32), pltpu.VMEM((CONV_HALO + tr, CONV_W), F32),
                        pltpu.VMEM((tr, ATTN_W), BF16),
                        pltpu.VMEM((tr, CONV_W), BF16), pltpu.VMEM((tr, 2 * D_MODEL), BF16),
                        pltpu.SemaphoreType.DMA((3,))],
        compiler_params=_params(),
    )(x, tgt, proj, proj, proj, proj, proj, proj, proj, proj, *o3, *l3, wa, wc, wo, gate, bga, bgc,
      convw, convb, lnw, lnb, bd)


def _conv_bwd(dcv, proj, convw, dproj):
    s = dcv.shape[0]
    tr = 128
    nt = s // tr

    def body(dcv_ref, dcvn_ref, u_ref, uh_ref, cw_ref, dp_in, dp_out, dw_ref, extg, extd, sh):
        del dp_in
        i = pl.program_id(0)

        @pl.when(i == 0)
        def _():
            dw_ref[...] = jnp.zeros_like(dw_ref)

        _, _, glu = _glu(u_ref[...])
        _, _, glu_h = _glu(uh_ref[...])
        extg[0:CONV_HALO, :] = jnp.where(i > 0, glu_h, 0.0)
        extg[CONV_HALO:CONV_HALO + tr, :] = glu
        extd[0:tr, :] = dcv_ref[...]
        extd[tr:tr + CONV_HALO, :] = jnp.where(i < nt - 1, dcvn_ref[...], 0.0)
        for cb in range(CONV_W // LANE):
            cols = slice(cb * LANE, (cb + 1) * LANE)
            dglu = jnp.zeros((tr, LANE), F32)
            for j, rows in _tap_views(extd, sh, [CONV_K - 1 - j for j in range(CONV_K)], tr, cols):
                dglu = dglu + cw_ref[j:j + 1, cols] * rows
            dcv_c = dcv_ref[:, cols]
            for j, rows in _tap_views(extg, sh, [CONV_HALO - (CONV_K - 1) + j for j in range(CONV_K)], tr, cols):
                dw_ref[8 * j:8 * j + 8, cols] += jnp.sum((dcv_c * rows).reshape(tr // 8, 8, LANE), axis=0)
            a_h = u_ref[:, cols]
            sgb = _sigmoid(u_ref[:, CONV_W + cb * LANE:CONV_W + (cb + 1) * LANE])
            dp_out[:, cols] = (dglu * sgb).astype(BF16)
            dp_out[:, CONV_W + cb * LANE:CONV_W + (cb + 1) * LANE] = (dglu * a_h * (sgb * (1.0 - sgb))).astype(BF16)

    ucol = U0 // D_MODEL
    return pl.pallas_call(
        body, name="conv_bwd", grid=(nt,),
        in_specs=[pl.BlockSpec((tr, CONV_W), lambda i: (i, 0)),
                  pl.BlockSpec((CONV_HALO, CONV_W), lambda i: (jnp.minimum((i + 1) * (tr // CONV_HALO), s // CONV_HALO - 1), 0)),
                  pl.BlockSpec((tr, D_MODEL), lambda i: (i, ucol)),
                  pl.BlockSpec((CONV_HALO, D_MODEL), lambda i: (jnp.maximum(i * (tr // CONV_HALO) - 1, 0), ucol)),
                  pl.BlockSpec(convw.shape, lambda i: (0, 0)),
                  pl.BlockSpec(memory_space=pl.ANY)],
        out_specs=[pl.BlockSpec((tr, D_MODEL), lambda i: (i, ucol)), pl.BlockSpec((8 * CONV_HALO, CONV_W), lambda i: (0, 0))],
        out_shape=[SDS(dproj.shape, dproj.dtype), SDS((8 * CONV_HALO, CONV_W), F32)],
        input_output_aliases={5: 0},
        scratch_shapes=[pltpu.VMEM((CONV_HALO + tr, CONV_W), F32)] * 3,
        compiler_params=_params(),
    )(dcv, dcv, proj, proj, convw, dproj)


def _mm_acc(at, b, name, col_slots):
    m, s = at.shape
    n = b.shape[1]
    tk = 1024
    nk = s // tk

    def body(a_ref, b_ref, o_ref, acc):
        k = pl.program_id(0)

        @pl.when(k == 0)
        def _():
            acc[...] = jnp.zeros_like(acc)

        acc[...] += _dot(a_ref[...], b_ref[...])

        @pl.when(k == nk - 1)
        def _():
            if col_slots:
                w = n // N_DEV
                for j in range(N_DEV):
                    o_ref[j] = acc[:, j * w:(j + 1) * w].astype(BF16)
            else:
                o_ref[...] = acc[...].astype(BF16)

    if col_slots:
        out_shape = SDS((N_DEV, m, n // N_DEV), BF16)
        out_spec = pl.BlockSpec((N_DEV, m, n // N_DEV), lambda k: (0, 0, 0))
    else:
        out_shape = SDS((m, n), BF16)
        out_spec = pl.BlockSpec((m, n), lambda k: (0, 0))
    return pl.pallas_call(
        body, name=name, grid=(nk,),
        in_specs=[pl.BlockSpec((m, tk), lambda k: (0, k)), pl.BlockSpec((tk, n), lambda k: (k, 0))],
        out_specs=out_spec, out_shape=out_shape, scratch_shapes=[pltpu.VMEM((m, n), F32)],
        compiler_params=_params(),
    )(at, b)


def _mm_dw(ht, dproj):
    s = ht.shape[1]
    tk = 1024
    nk = s // tk

    def body(a_ref, b_ref, o_ref, acc):
        k = pl.program_id(1)

        @pl.when(k == 0)
        def _():
            acc[...] = jnp.zeros_like(acc)

        acc[...] += _dot(a_ref[...], b_ref[...])

        @pl.when(k == nk - 1)
        def _():
            o_ref[...] = acc[...].T.astype(BF16)

    return pl.pallas_call(
        body, name="mm_dw", grid=(IN_W // PAIR_W, nk),
        in_specs=[pl.BlockSpec((D_MODEL, tk), lambda p, k: (0, k)), pl.BlockSpec((tk, PAIR_W), lambda p, k: (k, p))],
        out_specs=pl.BlockSpec((PAIR_W, D_MODEL), lambda p, k: (p, 0)),
        out_shape=SDS((IN_W, D_MODEL), BF16), scratch_shapes=[pltpu.VMEM((D_MODEL, PAIR_W), F32)],
        compiler_params=_params(),
    )(ht, dproj)


def _mm_dh_norm_bwd(dproj, wt, x, dout, norm_w, scale, token):
    s = dproj.shape[0]
    tm = 1024
    n_p = IN_W // PAIR_W

    def body(dp_ref, w_ref, x_ref, do_ref, nw_ref, sc_ref, tok_ref, gx_ref, dsh_ref, dsc_ref, dnw_ref, dh_acc):
        del tok_ref
        m, p = pl.program_id(0), pl.program_id(1)
        part = _dot(dp_ref[...], w_ref[...])

        @pl.when(p == 0)
        def _():
            dh_acc[...] = part

        @pl.when(p > 0)
        def _():
            dh_acc[...] += part

        @pl.when((m == 0) & (p == 0))
        def _():
            for r in (dsh_ref, dsc_ref, dnw_ref):
                r[...] = jnp.zeros_like(r)

        @pl.when(p == n_p - 1)
        def _():
            def acc_rows(ref, v):
                ref[...] += jnp.broadcast_to(jnp.sum(v, axis=0, keepdims=True), ref.shape)

            xv = x_ref[...]
            dh_v = dh_acc[...]
            r = lax.rsqrt(jnp.mean(xv * xv, axis=-1, keepdims=True) + EPS)
            xn = xv * r
            one_sc = 1.0 + sc_ref[...]
            acc_rows(dsh_ref, dh_v)
            acc_rows(dsc_ref, dh_v * (xn * nw_ref[...]))
            acc_rows(dnw_ref, dh_v * xn * one_sc)
            dxn = dh_v * (nw_ref[...] * one_sc)
            gx_ref[...] = do_ref[...] + r * (dxn - xn * jnp.mean(dxn * xn, axis=-1, keepdims=True))

    rows = pl.BlockSpec((tm, D_MODEL), lambda m, p: (m, 0))
    vec = pl.BlockSpec((1, D_MODEL), lambda m, p: (0, 0))
    acc = pl.BlockSpec((8, D_MODEL), lambda m, p: (0, 0))
    return pl.pallas_call(
        body, name="mm_dh_norm_bwd", grid=(s // tm, n_p),
        in_specs=[pl.BlockSpec((tm, PAIR_W), lambda m, p: (m, p)),
                  pl.BlockSpec((PAIR_W, D_MODEL), lambda m, p: (p, 0)),
                  rows, rows, vec, vec, pl.BlockSpec(token.shape, lambda m, p: (0, 0))],
        out_specs=[rows, acc, acc, acc],
        out_shape=[SDS((s, D_MODEL), F32)] + [SDS((8, D_MODEL), F32)] * 3,
        scratch_shapes=[pltpu.VMEM((tm, D_MODEL), F32)], compiler_params=_params(),
    )(dproj, wt, x, dout, norm_w, scale, token)


SMALL_ROWS = 8
QN_COL, KN_COL, CB_COL, LOSS_COL = 0, LANE, 2 * LANE, 2 * LANE + CONV_W


def _pack_partials(dsh, dsc, dgate, dnw, dbg, dqw3, dkw3, dcb, dlnw, dlnb, loss_p):
    n3 = len(dqw3)

    def body(*refs):
        dsh_r, dsc_r, dgate_r, dnw_r, dbg_r = refs[:5]
        dq_r, dk_r = refs[5:5 + n3], refs[5 + n3:5 + 2 * n3]
        dcb_r, dlnw_r, dlnb_r, loss_r, o_ref = refs[5 + 2 * n3:]

        def both_heads(rs):
            t = rs[0][0:1, :]
            for r in rs[1:]:
                t = t + r[0:1, :]
            return t + pltpu.roll(t, HEAD_DIM, axis=1)

        o_ref[0:1, :] = dsh_r[0:1, :]
        o_ref[1:2, :] = dsc_r[0:1, :]
        o_ref[2:3, :] = dgate_r[0:1, :]
        o_ref[3:4, :] = dnw_r[0:1, :]
        o_ref[4:5, :] = dbg_r[0:1, 0:D_MODEL]
        o_ref[5:6, :] = dbg_r[0:1, D_MODEL:]
        o_ref[6:7, QN_COL:QN_COL + LANE] = both_heads(dq_r)
        o_ref[6:7, KN_COL:KN_COL + LANE] = both_heads(dk_r)
        o_ref[6:7, CB_COL:CB_COL + CONV_W] = dcb_r[0:1, :]
        o_ref[6:7, LOSS_COL:LOSS_COL + LANE] = loss_r[0:1, :]
        o_ref[6:7, LOSS_COL + LANE:] = jnp.zeros((1, D_MODEL - LOSS_COL - LANE), F32)
        o_ref[7:8, 0:CONV_W] = dlnw_r[0:1, :]
        o_ref[7:8, CONV_W:] = dlnb_r[0:1, :]

    return pl.pallas_call(body, name="pack_partials", out_shape=SDS((SMALL_ROWS, D_MODEL), F32),
                          compiler_params=_params())(dsh, dsc, dgate, dnw, dbg, *dqw3, *dkw3, dcb, dlnw, dlnb, loss_p)


def _adamw_update(g, w, m, v):
    bc1 = 1.0 - ADAM_B1 ** ADAM_STEP
    bc2 = 1.0 - ADAM_B2 ** ADAM_STEP
    m_new = ADAM_B1 * m + (1.0 - ADAM_B1) * g
    v_new = ADAM_B2 * v + (1.0 - ADAM_B2) * (g * g)
    delta = -ADAM_LR * ((m_new / bc1) / (jnp.sqrt(v_new / bc2) + ADAM_EPS) + ADAM_WD * w)
    return delta, m_new, v_new


def _adamw_small(small_all, ws, ms, vs):
    n = len(ws)
    where = [(slice(0, 3), None), (slice(3, 4), None), (slice(4, 6), None), (6, QN_COL), (6, KN_COL), (6, CB_COL),
             (7, 0), (7, CONV_W)]

    def body(*refs):
        g_ref = refs[0]
        w_r, m_r, v_r = refs[1:1 + n], refs[1 + n:1 + 2 * n], refs[1 + 2 * n:1 + 3 * n]
        outs = refs[1 + 3 * n:]
        g_o, d_o, m_o, v_o, loss_o = outs[:n], outs[n:2 * n], outs[2 * n:3 * n], outs[3 * n:4 * n], outs[4 * n]
        gsum = g_ref[0]
        for dev in range(1, N_DEV):
            gsum = gsum + g_ref[dev]
        loss_o[...] = gsum[6:7, LOSS_COL:LOSS_COL + LANE]
        for i, (rows, col) in enumerate(where):
            width = w_r[i].shape[1]
            if col is None:
                g = jnp.concatenate([gsum[r:r + 1, :] for r in range(rows.start, rows.stop)], axis=1)
            else:
                g = gsum[rows:rows + 1, col:col + width]
            delta, m_new, v_new = _adamw_update(g, w_r[i][...], m_r[i][...], v_r[i][...])
            g_o[i][...] = g
            d_o[i][...] = delta
            m_o[i][...] = m_new
            v_o[i][...] = v_new

    shapes = [SDS(w.shape, F32) for w in ws]
    res = pl.pallas_call(body, name="adamw_small", out_shape=shapes * 4 + [SDS((1, LANE), F32)],
                         compiler_params=_params())(small_all, *ws, *ms, *vs)
    return [res[k * n:(k + 1) * n] for k in range(4)], res[4 * n]


def _row_tile(rows):
    if rows <= 128:
        return rows
    if rows % 256 == 0:
        return 256
    return 128 if rows % 128 == 0 else SHARD_W // 4


def _adamw(gsrc, w, m, v, name, stacked):
    rows, cols = w.shape
    tr = _row_tile(rows)
    n_src = len(gsrc) if stacked else 1

    def body(*refs):
        g_refs, (w_ref, m_ref, v_ref, go_ref, d_ref, mo_ref, vo_ref) = refs[:n_src], refs[n_src:]
        if stacked:
            g = None
            for g_ref, (_, slots) in zip(g_refs, gsrc):
                for j in range(slots):
                    t = g_ref[j].astype(F32)
                    g = t if g is None else g + t
        else:
            g = g_refs[0][...]
        delta, m_new, v_new = _adamw_update(g, w_ref[...], m_ref[...], v_ref[...])
        go_ref[...] = g
        d_ref[...] = delta
        mo_ref[...] = m_new
        vo_ref[...] = v_new

    blk = pl.BlockSpec((tr, cols), lambda i: (i, 0))
    if stacked:
        gspecs = [pl.BlockSpec((slots, tr, arr.shape[2]), lambda i: (0, i, 0)) for arr, slots in gsrc]
        gargs = [arr for arr, _ in gsrc]
    else:
        gspecs, gargs = [blk], [gsrc]
    in_specs = gspecs + [blk, blk, blk]
    args = gargs + [w, m, v]
    return pl.pallas_call(
        body, name=name, grid=(rows // tr,), in_specs=in_specs, out_specs=[blk] * 4,
        out_shape=[SDS((rows, cols), F32)] * 4, compiler_params=_params(),
    )(*args)


def kernel(x, c, w_ada, b_ada, norm_w, w_in, b_gate, q_norm_w, k_norm_w, w_attn_proj, conv_w, conv_b, conv_ln_w, conv_ln_b, w_conv_proj, w_out, loss_target, m_w_ada, m_b_ada, m_norm_w, m_w_in, m_b_gate, m_q_norm_w, m_k_norm_w, m_w_attn_proj, m_conv_w, m_conv_b, m_conv_ln_w, m_conv_ln_b, m_w_conv_proj, m_w_out, v_w_ada, v_b_ada, v_norm_w, v_w_in, v_b_gate, v_q_norm_w, v_k_norm_w, v_w_attn_proj, v_conv_w, v_conv_b, v_conv_ln_w, v_conv_ln_b, v_w_conv_proj, v_w_out):
    xi, yi, ci = lax.axis_index("x"), lax.axis_index("y"), lax.axis_index("c")
    me = 4 * xi + 2 * yi + ci
    x2, tgt2 = x[0], loss_target[0]
    w_in_t, m_w_in_t, v_w_in_t = (jnp.transpose(a[0]) for a in (w_in, m_w_in, v_w_in))
    s = x2.shape[0]

    cw_flat = jnp.pad(conv_w[0].reshape(1, -1), ((0, 0), (0, CONVW_FLAT - CONV_K * HEAD_DIM)))
    pre = jnp.concatenate([c, cw_flat], axis=1).reshape(8, -1)
    (pre_all,) = _all_gather([pre], "gather_c_convw", vmem=True)
    pre_all = pre_all.reshape(N_DEV, -1)
    c_all = pre_all[:, :D_MODEL]
    convw_full = pre_all[:, D_MODEL:D_MODEL + CONV_K * HEAD_DIM].reshape(N_DEV, CONV_K, HEAD_DIM)
    convw_full = jnp.transpose(convw_full, (1, 0, 2)).reshape(CONV_K, CONV_W)
    convw_pad = jnp.pad(convw_full, ((0, CONV_HALO - CONV_K), (0, 0)))

    ada_part = _ada_fwd(c_all, w_ada[0])
    (ada_all,) = _all_gather([ada_part], "gather_ada", vmem=True)
    ada = lax.dynamic_index_in_dim(ada_all, me, axis=1, keepdims=False).reshape(1, 3 * D_MODEL) + b_ada
    shift, scale, gate = ada[:, :D_MODEL], ada[:, D_MODEL:2 * D_MODEL], ada[:, 2 * D_MODEL:]

    wt_g, wa_g, wc_g, wo_g = _all_gather_chips(
        [_cast_bf16(w_in_t, "cast_win"), _cast_bf16(w_attn_proj[0], "cast_wa"), _cast_bf16(w_conv_proj[0], "cast_wc"),
         _cast_bf16(w_out[0], "cast_wo")], "gather_weights")
    wt = wt_g.reshape(IN_W, D_MODEL)
    wa = _cols_from_slots(wa_g, "cols_wa")
    wc = _cols_from_slots(wc_g, "cols_wc")
    wo = wo_g.reshape(D_MODEL, D_MODEL)

    h, ht = _norm_fwd(x2, norm_w, scale, shift)
    proj = _mm_in(h, wt)
    qw2 = jnp.tile(q_norm_w, (1, 2))
    kw2 = jnp.tile(k_norm_w, (1, 2))
    o_all, l_all, qn, kn, vn = _attn_fwd(proj, qw2, kw2)
    o3, l3 = [o_all] * N_GROUPS, [l_all] * N_GROUPS
    head_id = jnp.arange(ATTN_W) // HEAD_DIM
    bd = (head_id[:, None] == head_id[None, :]).astype(BF16)
    (dout, da, lse_delta, dcv, mt, yat, yct, dmo, dya, dyc, dproj,
     dgate, dbg, dlnw, dlnb, dcb, loss_p) = _tail(
        x2, tgt2, proj, o3, l3, wa, wc, wo, gate, b_gate[:, :D_MODEL], b_gate[:, D_MODEL:], convw_pad,
        conv_b, conv_ln_w, conv_ln_b, bd)

    dproj, dconvw8 = _conv_bwd(dcv, proj, convw_pad, dproj)
    dconvw = jnp.sum(dconvw8.reshape(CONV_HALO, 8, CONV_W), axis=1)
    dproj, dqw_all, dkw_all = _attn_bwd(proj, qn, kn, vn, da, lse_delta, qw2, kw2, dproj)
    dqw_g3, dkw_g3 = [dqw_all], [dkw_all]
    dw_in_p = _mm_dw(ht, dproj).reshape(N_DEV, SHARD_W, D_MODEL)
    dwo_p = _mm_acc(mt, dmo, "mm_dwo", col_slots=False).reshape(N_DEV, D_MODEL // N_DEV, D_MODEL)
    dwa_p = _mm_acc(yat, dya, "mm_dwa", col_slots=True)
    dwc_p = _mm_acc(yct, dyc, "mm_dwc", col_slots=True)

    partials = [dw_in_p, dwa_p, dwc_p, dwo_p]
    me_arr = jnp.reshape(me, (1,)).astype(jnp.int32)
    from_sib = _exchange_sibling(partials, "exchange_sibling")
    presums = [_presum(p, f, me_arr, f"presum{i}") for i, (p, f) in enumerate(zip(partials, from_sib))]
    s_sems, r_sems, pre_thru, land_thru, token = _exchange_chips_start(presums, "exchange_chips_start")
    gx, dsh, dsc, dnw = _mm_dh_norm_bwd(dproj, wt, x2, dout, norm_w, scale, token)
    small_p = _pack_partials(dsh, dsc, dgate, dnw, dbg, dqw_g3, dkw_g3, dcb, dlnw, dlnb, loss_p)
    small_all, dconvw_all = _all_gather([small_p, dconvw], "gather_small", vmem=True)

    small_w = (b_ada, norm_w, b_gate, q_norm_w, k_norm_w, conv_b, conv_ln_w, conv_ln_b)
    small_m = (m_b_ada, m_norm_w, m_b_gate, m_q_norm_w, m_k_norm_w, m_conv_b, m_conv_ln_w, m_conv_ln_b)
    small_v = (v_b_ada, v_norm_w, v_b_gate, v_q_norm_w, v_k_norm_w, v_conv_b, v_conv_ln_w, v_conv_ln_b)
    r_small, loss_row = _adamw_small(small_all, small_w, small_m, small_v)
    dcw_mine = lax.dynamic_slice_in_dim(dconvw_all[:, :CONV_K, :], me * HEAD_DIM, HEAD_DIM, axis=2)
    r_convw = _adamw([(dcw_mine, N_DEV)], conv_w[0], m_conv_w[0], v_conv_w[0], "adamw_conv_w", stacked=True)

    d_ada_all = small_all[:, 0:3, :].reshape(N_DEV, 3 * D_MODEL)
    d_ada_cols = lax.dynamic_slice_in_dim(d_ada_all, me * (3 * D_MODEL // N_DEV), 3 * D_MODEL // N_DEV, axis=1)
    g_wada = _ada_bwd(c_all, d_ada_cols)
    r_ada = _adamw(g_wada, w_ada[0], m_w_ada[0], v_w_ada[0], "adamw_w_ada", stacked=False)
    pres, lands = _exchange_chips_wait(s_sems, r_sems, pre_thru, land_thru, r_ada[1], "exchange_chips_wait")
    terms = [[(p, 1), (l, len(CHIP_K))] for p, l in zip(pres, lands)]
    r_win = [jnp.transpose(r) for r in _adamw(terms[0], w_in_t, m_w_in_t, v_w_in_t, "adamw_w_in", stacked=True)]
    r_wap = _adamw(terms[1], w_attn_proj[0], m_w_attn_proj[0], v_w_attn_proj[0], "adamw_w_attn_proj", stacked=True)
    r_wcp = _adamw(terms[2], w_conv_proj[0], m_w_conv_proj[0], v_w_conv_proj[0], "adamw_w_conv_proj", stacked=True)
    r_wout = _adamw(terms[3], w_out[0], m_w_out[0], v_w_out[0], "adamw_w_out", stacked=True)

    outs = [loss_row[0, 0], gx[None]]
    for k in range(4):
        b_ada_k, norm_w_k, b_gate_k, qn_k, kn_k, conv_b_k, ln_w_k, ln_b_k = r_small[k]
        outs += [r_ada[k][None], b_ada_k, norm_w_k, r_win[k][None], b_gate_k, qn_k, kn_k, r_wap[k][None],
                 r_convw[k][None], conv_b_k, ln_w_k, ln_b_k, r_wcp[k][None], r_wout[k][None]]
    return tuple(outs)
```

```python
import functools

import jax
import jax.numpy as jnp
from jax import lax
from jax.experimental import pallas as pl
from jax.experimental.pallas import tpu as pltpu

F32 = jnp.float32
BF16 = jnp.bfloat16
SDS = jax.ShapeDtypeStruct
MESH = pl.DeviceIdType.MESH

N_DEV = 8
D_MODEL = 1024
HEAD_DIM = 64
N_GROUPS = 3
DILATIONS = (1, 4, 16)
BAND = 128
BWD_UNROLL = 8
ATTN_W = 512
CONV_W = 512
CONV_K = 31
CONV_HALO = 32
IN_W = 8704
SHARD_W = IN_W // N_DEV
PAIR_W = 2 * SHARD_W
Q0, K0, V0, ZA0, U0, ZC0, G0 = 0, 1536, 3072, 4608, 5120, 6144, 6656
EPS = 1e-6
LANE = 128
VMEM_LIMIT = 56 * 1024 * 1024

ADAM_LR, ADAM_B1, ADAM_B2, ADAM_EPS, ADAM_WD, ADAM_STEP = 0.001, 0.9, 0.999, 1e-08, 0.01, 10

CONVW_FLAT = 2048


def _params(**kw):
    return pltpu.CompilerParams(vmem_limit_bytes=VMEM_LIMIT, **kw)


def _sigmoid(z):
    return 0.5 * jnp.tanh(0.5 * z) + 0.5


def _dot(a, b):
    return jnp.dot(a, b, preferred_element_type=F32)


def _dot_nt(a, b):
    return lax.dot_general(a, b, (((1,), (1,)), ((), ())), preferred_element_type=F32)


def _dot_tn(a, b):
    return lax.dot_general(a, b, (((0,), (0,)), ((), ())), preferred_element_type=F32)


def _peer(x, y, c, k):
    px = 1 - x if (k >> 2) & 1 else x
    py = 1 - y if (k >> 1) & 1 else y
    pc = 1 - c if k & 1 else c
    return (px, py, pc), 4 * px + 2 * py + pc


def _all_gather(arrays, name, vmem):
    n = len(arrays)
    space = pltpu.VMEM if vmem else pl.ANY

    def body(*refs):
        ins, outs = refs[:n], refs[n:2 * n]
        send_sems, recv_sems, local_sems = refs[2 * n:]
        x, y, c = lax.axis_index("x"), lax.axis_index("y"), lax.axis_index("c")
        me = 4 * x + 2 * y + c
        locals_ = [pltpu.make_async_copy(ins[a], outs[a].at[me], local_sems.at[a]) for a in range(n)]
        for cp in locals_:
            cp.start()
        sends = []
        for k in range(1, N_DEV):
            peer, _ = _peer(x, y, c, k)
            for a in range(n):
                cp = pltpu.make_async_remote_copy(
                    src_ref=ins[a], dst_ref=outs[a].at[me], send_sem=send_sems.at[a, k - 1],
                    recv_sem=recv_sems.at[a, k - 1], device_id=peer, device_id_type=MESH)
                cp.start()
                sends.append(cp)
        for k in range(1, N_DEV):
            peer, pidx = _peer(x, y, c, k)
            for a in range(n):
                pltpu.make_async_remote_copy(
                    src_ref=ins[a], dst_ref=outs[a].at[pidx], send_sem=send_sems.at[a, k - 1],
                    recv_sem=recv_sems.at[a, k - 1], device_id=peer, device_id_type=MESH).wait_recv()
        for cp in sends:
            cp.wait_send()
        for cp in locals_:
            cp.wait()

    return pl.pallas_call(
        body, name=name,
        out_shape=[SDS((N_DEV,) + a.shape, a.dtype) for a in arrays],
        in_specs=[pl.BlockSpec(memory_space=space)] * n,
        out_specs=[pl.BlockSpec(memory_space=space)] * n,
        scratch_shapes=[pltpu.SemaphoreType.DMA((n, N_DEV - 1)), pltpu.SemaphoreType.DMA((n, N_DEV - 1)),
                        pltpu.SemaphoreType.DMA((n,))],
        compiler_params=_params(),
    )(*arrays)


CHIP_K = (2, 4, 6)


def _all_gather_chips(arrays, name):
    n = len(arrays)
    k_y, k_x, k_d = CHIP_K

    def body(*refs):
        ins, outs = refs[:n], refs[n:2 * n]
        send_sems, recv_sems, local_sems = refs[2 * n:]
        x, y, c = lax.axis_index("x"), lax.axis_index("y"), lax.axis_index("c")
        me = 4 * x + 2 * y + c
        sib, sib_idx = _peer(x, y, c, 1)
        nbr_y, idx_y = _peer(x, y, c, k_y)
        nbr_x, idx_x = _peer(x, y, c, k_x)
        _, idx_d = _peer(x, y, c, k_d)

        def copy(a, slot, block, to, src=None):
            return pltpu.make_async_remote_copy(
                src_ref=outs[a].at[block] if src is None else src, dst_ref=outs[a].at[block],
                send_sem=send_sems.at[a, slot], recv_sem=recv_sems.at[a, slot], device_id=to, device_id_type=MESH)

        locals_ = [pltpu.make_async_copy(ins[a], outs[a].at[me], local_sems.at[a]) for a in range(n)]
        for cp in locals_:
            cp.start()
        for a in range(n):
            copy(a, 0, me, sib, src=ins[a]).start()
            copy(a, 1, me, nbr_y, src=ins[a]).start()
            copy(a, 2, me, nbr_x, src=ins[a]).start()

        def arrived(slot, block, frm, send_on_to=None):
            for a in range(n):
                copy(a, slot, block, frm).wait_recv()
                if send_on_to is not None:
                    copy(a, 3, block, send_on_to).start()
                copy(a, 3 + slot, block, sib).start()

        @pl.when(c == 0)
        def _():
            arrived(1, idx_y, nbr_y, send_on_to=nbr_x)
            arrived(2, idx_x, nbr_x)

        @pl.when(c == 1)
        def _():
            arrived(2, idx_x, nbr_x, send_on_to=nbr_y)
            arrived(1, idx_y, nbr_y)

        arrived(3, idx_d, nbr_x)
        for a in range(n):
            copy(a, 0, sib_idx, sib).wait_recv()
        for slot, k in ((4, k_y), (5, k_x), (6, k_d)):
            _, pidx = _peer(x, y, 1 - c, k)
            for a in range(n):
                copy(a, slot, pidx, sib).wait_recv()
        for slot in range(N_DEV - 1):
            for a in range(n):
                copy(a, slot, me, sib).wait_send()
        for cp in locals_:
            cp.wait()

    return pl.pallas_call(
        body, name=name,
        out_shape=[SDS((N_DEV,) + a.shape, a.dtype) for a in arrays],
        in_specs=[pl.BlockSpec(memory_space=pl.ANY)] * n,
        out_specs=[pl.BlockSpec(memory_space=pl.ANY)] * n,
        scratch_shapes=[pltpu.SemaphoreType.DMA((n, N_DEV - 1)), pltpu.SemaphoreType.DMA((n, N_DEV - 1)),
                        pltpu.SemaphoreType.DMA((n,))],
        compiler_params=_params(),
    )(*arrays)


def _exchange_sibling(arrays, name):
    n = len(arrays)
    ks = (0,) + CHIP_K

    def body(*refs):
        ins, outs = refs[:n], refs[n:2 * n]
        send_sems, recv_sems = refs[2 * n:]
        x, y, c = lax.axis_index("x"), lax.axis_index("y"), lax.axis_index("c")
        sib, sib_idx = _peer(x, y, c, 1)
        sends = []
        for i, k in enumerate(ks):
            _, tgt = _peer(x, y, 1 - c, k) if k else (None, sib_idx)
            for a in range(n):
                cp = pltpu.make_async_remote_copy(
                    src_ref=ins[a].at[tgt], dst_ref=outs[a].at[i], send_sem=send_sems.at[a, i],
                    recv_sem=recv_sems.at[a, i], device_id=sib, device_id_type=MESH)
                cp.start()
                sends.append(cp)
        for cp in sends:
            cp.wait_recv()
        for cp in sends:
            cp.wait_send()

    return pl.pallas_call(
        body, name=name,
        out_shape=[SDS((len(ks),) + a.shape[1:], a.dtype) for a in arrays],
        in_specs=[pl.BlockSpec(memory_space=pl.ANY)] * n,
        out_specs=[pl.BlockSpec(memory_space=pl.ANY)] * n,
        scratch_shapes=[pltpu.SemaphoreType.DMA((n, len(ks))), pltpu.SemaphoreType.DMA((n, len(ks)))],
        compiler_params=_params(),
    )(*arrays)


def _presum(mine, from_sib, me_arr, name):
    _, rows, cols = mine.shape
    tr = _row_tile(rows)
    ns = 1 + len(CHIP_K)

    def body(me_ref, a_ref, b_ref, o_ref):
        del me_ref
        o_ref[...] = (a_ref[...].astype(F32) + b_ref[...].astype(F32)).astype(o_ref.dtype)

    grid_spec = pltpu.PrefetchScalarGridSpec(
        num_scalar_prefetch=1, grid=(ns, rows // tr),
        in_specs=[pl.BlockSpec((1, tr, cols), lambda j, i, me: (jnp.bitwise_xor(me[0], 2 * j), i, 0)),
                  pl.BlockSpec((1, tr, cols), lambda j, i, me: (j, i, 0))],
        out_specs=pl.BlockSpec((1, tr, cols), lambda j, i, me: (j, i, 0)))
    return pl.pallas_call(body, name=name, grid_spec=grid_spec, out_shape=SDS((ns, rows, cols), mine.dtype),
                          compiler_params=_params())(me_arr, mine, from_sib)


HBM_SPEC = pl.BlockSpec(memory_space=pltpu.HBM)
SEM_SPEC = pl.BlockSpec(memory_space=pltpu.SEMAPHORE)
SIDE_EFFECT = pltpu.SideEffectType.DATAFLOW_SIDE_EFFECTING


def _chips_copies(pre_refs, land_refs, send_sems, recv_sems):
    x, y, c = lax.axis_index("x"), lax.axis_index("y"), lax.axis_index("c")
    copies = []
    for j, k in enumerate(CHIP_K):
        peer, _ = _peer(x, y, c, k)
        for a in range(len(pre_refs)):
            copies.append(pltpu.make_async_remote_copy(
                src_ref=pre_refs[a].at[1 + j], dst_ref=land_refs[a].at[j], send_sem=send_sems.at[a * len(CHIP_K) + j],
                recv_sem=recv_sems.at[a * len(CHIP_K) + j], device_id=peer, device_id_type=MESH))
    return copies


def _exchange_chips_start(presums, name):
    n = len(presums)

    def body(*refs):
        pre, land = refs[:n], refs[n:2 * n]
        send_sems, recv_sems = refs[2 * n], refs[2 * n + 1]
        token = refs[-1]
        for cp in _chips_copies(pre, land, send_sems, recv_sems):
            cp.start()
        token[...] = jnp.zeros_like(token)

    nk = len(CHIP_K)
    hbm = [pltpu.HBM(p.shape, p.dtype) for p in presums]
    hbm_land = [pltpu.HBM((nk,) + p.shape[1:], p.dtype) for p in presums]
    res = pl.pallas_call(
        body, name=name,
        out_shape=(pltpu.SemaphoreType.DMA((n * nk,)), pltpu.SemaphoreType.DMA((n * nk,)), *hbm, *hbm_land, SDS((8, LANE), F32)),
        in_specs=[HBM_SPEC] * (2 * n),
        out_specs=(SEM_SPEC, SEM_SPEC, *([HBM_SPEC] * (2 * n)), pl.BlockSpec(memory_space=pltpu.VMEM)),
        input_output_aliases={i: 2 + i for i in range(2 * n)},
        compiler_params=pltpu.CompilerParams(has_side_effects=SIDE_EFFECT),
    )(*[pltpu.with_memory_space_constraint(p, pltpu.HBM) for p in presums],
      *[pltpu.with_memory_space_constraint(lax.empty((nk,) + p.shape[1:], p.dtype), pltpu.HBM) for p in presums])
    return res[0], res[1], res[2:2 + n], res[2 + n:2 + 2 * n], res[-1]


def _exchange_chips_wait(send_sems, recv_sems, pre_thru, land_thru, after, name):
    n = len(pre_thru)

    def body(*refs):
        pre, land = refs[:n], refs[n:2 * n]
        s_sems, r_sems = refs[2 * n], refs[2 * n + 1]
        for cp in _chips_copies(pre, land, s_sems, r_sems):
            cp.wait_send()
            cp.wait_recv()

    hbm = [pltpu.HBM(p.shape, p.dtype) for p in (*pre_thru, *land_thru)]
    res = pl.pallas_call(
        body, name=name, out_shape=tuple(hbm),
        in_specs=[HBM_SPEC] * (2 * n) + [SEM_SPEC, SEM_SPEC, pl.BlockSpec(memory_space=pl.ANY)],
        out_specs=tuple([HBM_SPEC] * (2 * n)),
        input_output_aliases={i: i for i in range(2 * n)},
        compiler_params=pltpu.CompilerParams(has_side_effects=SIDE_EFFECT),
    )(*pre_thru, *land_thru, send_sems, recv_sems, after)
    return res[:n], res[n:]


def _exchange_chips(presums, name):
    n = len(presums)
    nk = len(CHIP_K)

    def body(*refs):
        pre, land = refs[:n], refs[n:2 * n]
        send_sems, recv_sems = refs[2 * n:]
        copies = _chips_copies(pre, land, send_sems, recv_sems)
        for cp in copies:
            cp.start()
        for cp in copies:
            cp.wait_recv()
        for cp in copies:
            cp.wait_send()

    return pl.pallas_call(
        body, name=name,
        out_shape=[SDS((nk,) + p.shape[1:], p.dtype) for p in presums],
        in_specs=[pl.BlockSpec(memory_space=pl.ANY)] * n,
        out_specs=[pl.BlockSpec(memory_space=pl.ANY)] * n,
        scratch_shapes=[pltpu.SemaphoreType.DMA((n * nk,)), pltpu.SemaphoreType.DMA((n * nk,))],
        compiler_params=_params(),
    )(*presums)


def _cast_bf16(w, name):
    def body(w_ref, o_ref):
        o_ref[...] = w_ref[...].astype(BF16)

    return pl.pallas_call(body, name=name, out_shape=SDS(w.shape, BF16), compiler_params=_params())(w)


def _cols_from_slots(wg, name):
    _, rows, cols = wg.shape

    def body(w_ref, o_ref):
        for j in range(N_DEV):
            o_ref[:, j * cols:(j + 1) * cols] = w_ref[j]

    return pl.pallas_call(body, name=name, out_shape=SDS((rows, N_DEV * cols), wg.dtype), compiler_params=_params())(wg)


def _ada_fwd(c_all, w_ada):
    def body(c_ref, w_ref, o_ref):
        cv = c_ref[...]
        sc = (cv * _sigmoid(cv)).astype(BF16)
        o_ref[...] = _dot(sc, w_ref[...].astype(BF16))

    return pl.pallas_call(body, name="ada_fwd", out_shape=SDS((N_DEV, w_ada.shape[1]), F32),
                          compiler_params=_params())(c_all, w_ada)


def _ada_bwd(c_all, d_ada_cols):
    def body(c_ref, d_ref, o_ref):
        cv = c_ref[...]
        sc = (cv * _sigmoid(cv)).astype(BF16)
        o_ref[...] = _dot_tn(sc, d_ref[...].astype(BF16))

    return pl.pallas_call(body, name="ada_bwd", out_shape=SDS((D_MODEL, d_ada_cols.shape[1]), F32),
                          compiler_params=_params())(c_all, d_ada_cols)


def _norm_fwd(x, norm_w, scale, shift):
    s = x.shape[0]
    tr = 1024

    def body(x_ref, nw_ref, sc_ref, sh_ref, h_ref, ht_ref):
        xv = x_ref[...]
        r = lax.rsqrt(jnp.mean(xv * xv, axis=-1, keepdims=True) + EPS)
        h = (xv * r * nw_ref[...]) * (1.0 + sc_ref[...]) + sh_ref[...]
        h_ref[...] = h.astype(BF16)
        ht_ref[...] = h.T.astype(BF16)

    vec = pl.BlockSpec((1, D_MODEL), lambda i: (0, 0))
    return pl.pallas_call(
        body, name="norm_fwd", grid=(s // tr,),
        in_specs=[pl.BlockSpec((tr, D_MODEL), lambda i: (i, 0)), vec, vec, vec],
        out_specs=[pl.BlockSpec((tr, D_MODEL), lambda i: (i, 0)), pl.BlockSpec((D_MODEL, tr), lambda i: (0, i))],
        out_shape=[SDS((s, D_MODEL), BF16), SDS((D_MODEL, s), BF16)], compiler_params=_params(),
    )(x, norm_w, scale, shift)


def _mm_in(h, wt):
    s = h.shape[0]
    tm = 512

    def body(h_ref, w_ref, o_ref):
        o_ref[...] = _dot_nt(h_ref[...], w_ref[...])

    return pl.pallas_call(
        body, name="mm_in", grid=(IN_W // PAIR_W, s // tm),
        in_specs=[pl.BlockSpec((tm, D_MODEL), lambda p, m: (m, 0)),
                  pl.BlockSpec((PAIR_W, D_MODEL), lambda p, m: (p, 0))],
        out_specs=pl.BlockSpec((tm, PAIR_W), lambda p, m: (m, p)),
        out_shape=SDS((s, IN_W), F32), compiler_params=_params(),
    )(h, wt)


def _head_ones():
    a = lax.broadcasted_iota(jnp.int32, (LANE, LANE), 0) // HEAD_DIM
    b = lax.broadcasted_iota(jnp.int32, (LANE, LANE), 1) // HEAD_DIM
    return (a == b).astype(BF16)


def _head_sums(t, ones):
    return _dot(t.astype(BF16), ones)


def _band_bias(bias, transposed=False):
    qi = lax.broadcasted_iota(jnp.int32, (2 * BAND, 2 * BAND), 1 if transposed else 0) % BAND
    kj = lax.broadcasted_iota(jnp.int32, (2 * BAND, 2 * BAND), 0 if transposed else 1)
    dist = qi + BAND - kj
    valid = (dist >= 0) & (dist <= BAND)
    bias[1] = jnp.where(valid, 0.0, -1e30)
    bias[0] = jnp.where(valid & (kj >= BAND), 0.0, -1e30)


def _token_rows(j, d, chunk, per_r):
    return pl.ds(j // per_r + (j % per_r) * (chunk * d), chunk, stride=d)


def _deinterleave(src_ref, dst_ref, w_ref, ones, d, sub_len, chunk, scale, dst_off):
    per_r = sub_len // chunk

    def step(j, _):
        t = src_ref[_token_rows(j, d, chunk, per_r), :]
        if w_ref is not None:
            ms = _head_sums(t * t, ones) * (1.0 / HEAD_DIM)
            t = t * lax.rsqrt(ms + EPS) * (w_ref[...] * scale)
        dst_ref[pl.ds(pl.multiple_of(dst_off + j * chunk, BAND), chunk), :] = t.astype(dst_ref.dtype)
        return 0
    lax.fori_loop(0, d * per_r, step, 0, unroll=4)


N_PAIRS = ATTN_W // LANE


def _attn_fwd(proj, qw2, kw2):
    s = proj.shape[0]

    def group_body(g, step, q_ref, k_ref, v_ref, qw_ref, kw_ref, o_ref, l_ref, qn_ref, kn_ref, vn_ref,
                   qd, kd, vd, od, ld, bias):
        d = DILATIONS[g]
        sub_len = s // d
        nb = sub_len // BAND
        chunk = min(sub_len, 256)
        lo = lax.broadcasted_iota(jnp.int32, (1, LANE), 1) < HEAD_DIM
        ones = _head_ones()

        @pl.when(step == 0)
        def _():
            _band_bias(bias)

        kd[0:BAND, :] = jnp.zeros((BAND, LANE), BF16)
        vd[0:BAND, :] = jnp.zeros((BAND, LANE), BF16)
        _deinterleave(q_ref, qd, qw_ref, ones, d, sub_len, chunk, HEAD_DIM ** -0.5, 0)
        _deinterleave(k_ref, kd, kw_ref, ones, d, sub_len, chunk, 1.0, BAND)
        _deinterleave(v_ref, vd, None, ones, d, sub_len, chunk, 1.0, BAND)
        qn_ref[...] = qd[...]
        kn_ref[...] = kd[BAND:BAND + s, :]
        vn_ref[...] = vd[BAND:BAND + s, :]

        def block(t, _):
            base = pl.multiple_of(t * BAND, BAND)
            q = qd[pl.ds(base, BAND), :]
            k2 = kd[pl.ds(base, 2 * BAND), :]
            v2 = vd[pl.ds(base, 2 * BAND), :]
            zero = jnp.zeros_like(q)
            qs = jnp.concatenate([jnp.where(lo, q, zero), jnp.where(lo, zero, q)], axis=0)
            sc = _dot_nt(qs, k2) + bias[jnp.minimum(t % nb, 1)]
            m = jnp.max(sc, axis=-1, keepdims=True)
            p = jnp.exp(sc - m)
            den = jnp.sum(p, axis=-1, keepdims=True)
            u = _dot(p.astype(BF16), v2) * (1.0 / den)
            lse = m + jnp.log(den)
            od[pl.ds(base, BAND), :] = jnp.where(lo, u[:BAND], u[BAND:])
            ld[pl.ds(base, BAND), :] = jnp.where(lo, lse[:BAND], lse[BAND:])
            return 0
        lax.fori_loop(0, s // BAND, block, 0, unroll=16)

        per_r = sub_len // chunk

        def back(j, _):
            src = pl.ds(pl.multiple_of(j * chunk, chunk), chunk)
            dst = _token_rows(j, d, chunk, per_r)
            o_ref[dst, :] = od[src, :]
            l_ref[dst, :] = ld[src, :]
            return 0
        lax.fori_loop(0, d * per_r, back, 0, unroll=2)

    def body(*refs):
        step = pl.program_id(0)
        for g in range(N_GROUPS):
            pl.when(step // N_PAIRS == g)(functools.partial(group_body, g, step, *refs))

    col = lambda off: pl.BlockSpec((s, LANE), lambda i, off=off: (0, off // LANE + i))
    vec = pl.BlockSpec((1, LANE), lambda i: (0, 0))
    out = pl.BlockSpec((s, LANE), lambda i: (0, i))
    width = N_GROUPS * ATTN_W
    return pl.pallas_call(
        body, name="attn_fwd", grid=(N_GROUPS * N_PAIRS,),
        in_specs=[col(Q0), col(K0), col(V0), vec, vec], out_specs=[out] * 5,
        out_shape=[SDS((s, width), F32)] * 2 + [SDS((s, width), BF16)] * 3,
        scratch_shapes=[pltpu.VMEM((s, LANE), BF16), pltpu.VMEM((s + BAND, LANE), BF16), pltpu.VMEM((s + BAND, LANE), BF16),
                        pltpu.VMEM((s, LANE), F32), pltpu.VMEM((s, LANE), F32),
                        pltpu.VMEM((2, 2 * BAND, 2 * BAND), F32)],
        compiler_params=_params(),
    )(proj, proj, proj, qw2, kw2)


def _attn_bwd(proj, qn, kn, vn, da, lse_delta, qw2, kw2, dproj):
    s = proj.shape[0]
    n_steps = N_GROUPS * N_PAIRS

    def group_body(g, hp, q_ref, k_ref, qn_ref, kn_ref, vn_ref, da_ref, ld_ref, qw_ref, kw_ref, dp_in, dp_out,
                   dqw_ref, dkw_ref, kd, vd, kdt, dad, lst, dlt, dqt, dqd, dkd, dvd, st, st_k, stb, bias_t, wacc, sem):
        del dp_in
        d = DILATIONS[g]
        sub_len = s // d
        nb = sub_len // BAND
        chunk = min(sub_len, 256)
        lo = lax.broadcasted_iota(jnp.int32, (1, LANE), 1) < HEAD_DIM
        row_lo = lax.broadcasted_iota(jnp.int32, (LANE, 1), 0) < HEAD_DIM
        ones = _head_ones()
        per_r = sub_len // chunk
        cblk = chunk // BAND

        @pl.when(hp == 0)
        def _():
            _band_bias(bias_t, transposed=True)

        kd[0:BAND, :] = jnp.zeros((BAND, LANE), BF16)
        vd[0:BAND, :] = jnp.zeros((BAND, LANE), BF16)
        kdt[0] = jnp.zeros((LANE, BAND), BF16)
        kd[BAND:BAND + s, :] = kn_ref[...]
        vd[BAND:BAND + s, :] = vn_ref[...]

        def k_step(t, _):
            kdt[1 + t] = kn_ref[pl.ds(pl.multiple_of(t * BAND, BAND), BAND), :].astype(F32).T.astype(BF16)
            return 0
        lax.fori_loop(0, s // BAND, k_step, 0, unroll=4)
        _deinterleave(da_ref, dad, None, ones, d, sub_len, chunk, 1.0, 0)

        def rows_step(j, _):
            tok = _token_rows(j, d, chunk, per_r)
            tt = ld_ref[tok, :].T
            for u in range(cblk):
                cols = slice(u * BAND, (u + 1) * BAND)
                lst[j * cblk + u, 0:1, :] = tt[0:1, cols]
                lst[j * cblk + u, 1:2, :] = tt[HEAD_DIM:HEAD_DIM + 1, cols]
                dlt[j * cblk + u, 0:1, :] = tt[HEAD_DIM // 2:HEAD_DIM // 2 + 1, cols]
                dlt[j * cblk + u, 1:2, :] = tt[HEAD_DIM + HEAD_DIM // 2:HEAD_DIM + HEAD_DIM // 2 + 1, cols]
            return 0
        lax.fori_loop(0, d * per_r, rows_step, 0, unroll=4)

        def block(t, carry):
            ck, cv = carry
            base = pl.multiple_of(t * BAND, BAND)
            q = qn_ref[pl.ds(base, BAND), :]
            k2 = kd[pl.ds(base, 2 * BAND), :]
            v2 = vd[pl.ds(base, 2 * BAND), :]
            k2t = jnp.concatenate([kdt[t], kdt[t + 1]], axis=1)
            dav = dad[pl.ds(base, BAND), :]
            zero = jnp.zeros_like(q)
            qs = jnp.concatenate([jnp.where(lo, q, zero), jnp.where(lo, zero, q)], axis=0)
            das = jnp.concatenate([jnp.where(lo, dav, zero), jnp.where(lo, zero, dav)], axis=0)
            ls_row = jnp.concatenate([lst[t, 0:1, :], lst[t, 1:2, :]], axis=1)
            dl_row = jnp.concatenate([dlt[t, 0:1, :], dlt[t, 1:2, :]], axis=1)
            sc_t = _dot_nt(k2, qs) + bias_t[jnp.minimum(t % nb, 1)]
            p_t = jnp.exp(sc_t - ls_row)
            dp_t = _dot_nt(v2, das)
            ds_t = (p_t * (dp_t - dl_row)).astype(BF16)
            dv2 = _dot(p_t.astype(BF16), das)
            dk2 = _dot(ds_t, qs)
            dvd[pl.ds(base, BAND), :] = cv + dv2[:BAND]
            dkd[pl.ds(base, BAND), :] = ck + dk2[:BAND]
            dq_t = _dot(k2t, ds_t)
            dqt[t] = jnp.where(row_lo, dq_t[:, :BAND], dq_t[:, BAND:])
            return dk2[BAND:], dv2[BAND:]

        def blocks(i, carry):
            for u in range(BWD_UNROLL):
                carry = block(i * BWD_UNROLL + u, carry)
            return carry
        zeros = jnp.zeros((BAND, LANE), F32)
        ck, cv = lax.fori_loop(0, s // (BAND * BWD_UNROLL), blocks, (zeros, zeros))
        dkd[s:s + BAND, :] = ck
        dvd[s:s + BAND, :] = cv

        def dq_rows(t, _):
            dqd[pl.ds(pl.multiple_of(t * BAND, BAND), BAND), :] = dqt[t].T
            return 0
        lax.fori_loop(0, s // BAND, dq_rows, 0, unroll=4)

        def col_copy(slot, col0):
            return pltpu.make_async_copy(
                stb.at[slot], dp_out.at[:, pl.ds(pl.multiple_of(col0 + LANE * hp, LANE), LANE)], sem.at[slot])

        def store_cols(slot, col0, src):
            @pl.when(hp > 0)
            def _():
                col_copy(slot, col0).wait()
            stb[slot] = src[...].astype(BF16)
            col_copy(slot, col0).start()

        sides = ((q_ref, dqd, 0, qw_ref, HEAD_DIM ** -0.5, st), (k_ref, dkd, BAND, kw_ref, 1.0, st_k))
        wacc[...] = jnp.zeros_like(wacc)

        def norm_step(j, _):
            tok = _token_rows(j, d, chunk, per_r)
            for i, (src_ref, dy_ref, dy_off, w_ref, scale, dst) in enumerate(sides):
                t = src_ref[tok, :]
                dy = dy_ref[pl.ds(pl.multiple_of(dy_off + j * chunk, BAND), chunk), :]
                rr = lax.rsqrt(_head_sums(t * t, ones) * (1.0 / HEAD_DIM) + EPS)
                nrm = t * rr
                wacc[i] += jnp.sum((dy * nrm).reshape(chunk // 8, 8, LANE), axis=0)
                dn = dy * (w_ref[...] * scale)
                dst[tok, :] = rr * (dn - nrm * (_head_sums(dn * nrm, ones) * (1.0 / HEAD_DIM)))
            return 0
        lax.fori_loop(0, d * per_r, norm_step, 0, unroll=4)

        @pl.when(hp == 0)
        def _():
            dqw_ref[...] = jnp.zeros_like(dqw_ref)
            dkw_ref[...] = jnp.zeros_like(dkw_ref)

        for i, dw_ref in enumerate((dqw_ref, dkw_ref)):
            dw_ref[...] += jnp.broadcast_to(jnp.sum(wacc[i], axis=0, keepdims=True) * sides[i][4], dw_ref.shape)
        store_cols(0, Q0, st)
        store_cols(1, K0, st_k)

        def v_back(j, _):
            src = pl.ds(pl.multiple_of(BAND + j * chunk, BAND), chunk)
            st[_token_rows(j, d, chunk, per_r), :] = dvd[src, :]
            return 0
        lax.fori_loop(0, d * per_r, v_back, 0, unroll=2)
        store_cols(2, V0, st)

        @pl.when(hp == n_steps - 1)
        def _():
            for slot, col0 in enumerate((Q0, K0, V0)):
                col_copy(slot, col0).wait()

    def body(*refs):
        step = pl.program_id(0)
        for g in range(N_GROUPS):
            pl.when(step // N_PAIRS == g)(functools.partial(group_body, g, step, *refs))

    col = lambda off: pl.BlockSpec((s, LANE), lambda i, off=off: (0, off // LANE + i))
    mid = pl.BlockSpec((s, LANE), lambda i: (0, i))
    slot4 = pl.BlockSpec((s, LANE), lambda i: (0, i % N_PAIRS))
    vec = pl.BlockSpec((1, LANE), lambda i: (0, 0))
    acc = pl.BlockSpec((8, LANE), lambda i: (0, 0))
    any_ = pl.BlockSpec(memory_space=pl.ANY)
    return pl.pallas_call(
        body, name="attn_bwd", grid=(n_steps,),
        in_specs=[col(Q0), col(K0), mid, mid, mid, slot4, slot4, vec, vec, any_],
        out_specs=[any_, acc, acc],
        out_shape=[SDS(dproj.shape, dproj.dtype), SDS((8, LANE), F32), SDS((8, LANE), F32)],
        input_output_aliases={9: 0},
        scratch_shapes=[pltpu.VMEM((s + BAND, LANE), BF16), pltpu.VMEM((s + BAND, LANE), BF16),
                        pltpu.VMEM((s // BAND + 1, LANE, BAND), BF16), pltpu.VMEM((s, LANE), BF16),
                        pltpu.VMEM((s // BAND, 8, BAND), F32), pltpu.VMEM((s // BAND, 8, BAND), F32),
                        pltpu.VMEM((s // BAND, LANE, BAND), F32),
                        pltpu.VMEM((s, LANE), F32), pltpu.VMEM((s + BAND, LANE), F32), pltpu.VMEM((s + BAND, LANE), F32),
                        pltpu.VMEM((s, LANE), F32), pltpu.VMEM((s, LANE), F32), pltpu.VMEM((3, s, LANE), BF16),
                        pltpu.VMEM((2, 2 * BAND, 2 * BAND), F32), pltpu.VMEM((2, 8, LANE), F32),
                        pltpu.SemaphoreType.DMA((3,))],
        compiler_params=_params(),
    )(proj, proj, qn, kn, vn, da, lse_delta, qw2, kw2, dproj)


def _tap_views(ext_ref, sh_ref, offsets, tr, cols):
    for b in range(8):
        group = [j for j, o in enumerate(offsets) if o % 8 == b]
        if not group:
            continue
        first = min(offsets[j] for j in group)
        span = tr + max(offsets[j] for j in group) - first
        sh_ref[0:span, cols] = ext_ref[first:first + span, cols]
        for j in group:
            yield j, sh_ref[offsets[j] - first:offsets[j] - first + tr, cols]


def _silu_grad(z, sg):
    return sg * (1.0 + z * (1.0 - sg))


def _glu(u):
    a_h, b_h = u[:, :CONV_W], u[:, CONV_W:]
    sg = _sigmoid(b_h)
    return a_h, sg, a_h * sg


def _tail(x, tgt, proj, o3, l3, wa, wc, wo, gate, bga, bgc, convw, convb, lnw, lnb, bd):
    s = x.shape[0]
    tr = 256

    def body(x_ref, t_ref, za_ref, u_ref, uh_ref, zc_ref, g0_ref, g1_ref, g2_ref, g3_ref,
             o0_ref, o1_ref, o2_ref, l0_ref, l1_ref, l2_ref, wa_ref, wc_ref, wo_ref,
             gate_ref, bga_ref, bgc_ref, cw_ref, cb_ref, lnw_ref, lnb_ref, bd_ref,
             dout_ref, da_ref, ld_ref, dcv_ref, mt_ref, yat_ref, yct_ref, dmo_ref, dya_ref, dyc_ref, dp_ref,
             dgate_ref, dbg_ref, dlnw_ref, dlnb_ref, dcb_ref, loss_ref,
             ext, sh, st_za, st_zc, st_g, sems):
        i = pl.program_id(0)

        @pl.when(i == 0)
        def _():
            for r in (dgate_ref, dbg_ref, dlnw_ref, dlnb_ref, dcb_ref, loss_ref):
                r[...] = jnp.zeros_like(r)

        def acc_rows(ref, v):
            ref[...] += jnp.broadcast_to(jnp.sum(v, axis=0, keepdims=True), ref.shape)

        la, lb, lc = l0_ref[...], l1_ref[...], l2_ref[...]
        mx = jnp.maximum(jnp.maximum(la, lb), lc)
        ea, eb, ec = jnp.exp(la - mx), jnp.exp(lb - mx), jnp.exp(lc - mx)
        den = ea + eb + ec
        inv = 1.0 / den
        attn = (ea * inv) * o0_ref[...] + (eb * inv) * o1_ref[...] + (ec * inv) * o2_ref[...]
        lse = mx + jnp.log(den)

        za = za_ref[...]
        sga = _sigmoid(za)
        sa = za * sga
        ya_in = attn * sa
        y_attn = _dot(ya_in.astype(BF16), wa_ref[...])

        _, _, glu = _glu(u_ref[...])
        _, _, glu_h = _glu(uh_ref[...])
        ext[0:CONV_HALO, :] = jnp.where(i > 0, glu_h, 0.0)
        ext[CONV_HALO:CONV_HALO + tr, :] = glu
        cv_blocks = []
        for cb in range(CONV_W // LANE):
            cols = slice(cb * LANE, (cb + 1) * LANE)
            cv_c = jnp.broadcast_to(cb_ref[:, cols], (tr, LANE))
            for j, rows in _tap_views(ext, sh, [CONV_HALO - (CONV_K - 1) + j for j in range(CONV_K)], tr, cols):
                cv_c = cv_c + cw_ref[j:j + 1, cols] * rows
            cv_blocks.append(cv_c)
        cv = jnp.concatenate(cv_blocks, axis=1)
        mu = jnp.mean(cv, axis=-1, keepdims=True)
        xc = cv - mu
        rstd = lax.rsqrt(jnp.mean(xc * xc, axis=-1, keepdims=True) + EPS)
        nrm = xc * rstd
        ln = nrm * lnw_ref[...] + lnb_ref[...]
        sgl = _sigmoid(ln)
        cs = ln * sgl
        zc = zc_ref[...]
        sgc = _sigmoid(zc)
        scz = zc * sgc
        yc_in = cs * scz
        y_conv = _dot(yc_in.astype(BF16), wc_ref[...])

        ga = _sigmoid(jnp.concatenate([g0_ref[...], g1_ref[...]], axis=1) + bga_ref[...])
        gc = _sigmoid(jnp.concatenate([g2_ref[...], g3_ref[...]], axis=1) + bgc_ref[...])
        merged = ga * y_attn + gc * y_conv
        mo = _dot(merged.astype(BF16), wo_ref[...])
        gate_v = gate_ref[...]
        err = (x_ref[...] + gate_v * mo) - t_ref[...]
        loss_ref[...] += 0.5 * jnp.sum(jnp.mean(err * err, axis=-1, keepdims=True))
        d_out = err * (1.0 / D_MODEL)
        dout_ref[...] = d_out

        rows = pl.ds(pl.multiple_of(i * tr, tr), tr)
        cps = [pltpu.make_async_copy(st_za, dp_ref.at[rows, pl.ds(ZA0, ATTN_W)], sems.at[0]),
               pltpu.make_async_copy(st_zc, dp_ref.at[rows, pl.ds(ZC0, CONV_W)], sems.at[1]),
               pltpu.make_async_copy(st_g, dp_ref.at[rows, pl.ds(G0, 2 * D_MODEL)], sems.at[2])]

        @pl.when(i > 0)
        def _():
            for cp in cps:
                cp.wait()

        acc_rows(dgate_ref, d_out * mo)
        dmo_b = (d_out * gate_v).astype(BF16)
        dmo_ref[...] = dmo_b
        mt_ref[...] = merged.T.astype(BF16)
        d_merged = _dot_nt(dmo_b, wo_ref[...])
        d_ya = (d_merged * ga).astype(BF16)
        d_yc = (d_merged * gc).astype(BF16)
        dya_ref[...] = d_ya
        dyc_ref[...] = d_yc
        dga = d_merged * y_attn * (ga * (1.0 - ga))
        dgc = d_merged * y_conv * (gc * (1.0 - gc))
        dgs = jnp.concatenate([dga, dgc], axis=1)
        acc_rows(dbg_ref, dgs)
        st_g[...] = dgs.astype(BF16)

        yat_ref[...] = ya_in.T.astype(BF16)
        d_ya_in = _dot_nt(d_ya, wa_ref[...])
        d_attn = d_ya_in * sa
        da_ref[...] = d_attn
        st_za[...] = (d_ya_in * attn * _silu_grad(za, sga)).astype(BF16)
        prod = d_attn * attn
        hi = prod.astype(BF16)
        lo_ = (prod - hi.astype(F32)).astype(BF16)
        delta = _dot(hi, bd_ref[...]) + _dot(lo_, bd_ref[...])
        first_half = (lax.broadcasted_iota(jnp.int32, (1, ATTN_W), 1) % HEAD_DIM) < HEAD_DIM // 2
        ld_ref[...] = jnp.where(first_half, lse, delta)

        yct_ref[...] = yc_in.T.astype(BF16)
        d_yc_in = _dot_nt(d_yc, wc_ref[...])
        st_zc[...] = (d_yc_in * cs * _silu_grad(zc, sgc)).astype(BF16)
        d_ln = (d_yc_in * scz) * _silu_grad(ln, sgl)
        acc_rows(dlnw_ref, d_ln * nrm)
        acc_rows(dlnb_ref, d_ln)
        d_nrm = d_ln * lnw_ref[...]
        d_cv = rstd * (d_nrm - jnp.mean(d_nrm, axis=-1, keepdims=True)
                       - nrm * jnp.mean(d_nrm * nrm, axis=-1, keepdims=True))
        acc_rows(dcb_ref, d_cv)
        dcv_ref[...] = d_cv

        for cp in cps:
            cp.start()

        @pl.when(i == s // tr - 1)
        def _():
            for cp in cps:
                cp.wait()

    def rows(width, colblk=0):
        return pl.BlockSpec((tr, width), lambda i, colblk=colblk: (i, colblk))

    def const(shape):
        return pl.BlockSpec(shape, lambda i: (0,) * len(shape))

    halo = pl.BlockSpec((CONV_HALO, D_MODEL), lambda i: (jnp.maximum(i * (tr // CONV_HALO) - 1, 0), U0 // D_MODEL))
    in_specs = [rows(D_MODEL), rows(D_MODEL), rows(ATTN_W, ZA0 // ATTN_W), rows(D_MODEL, U0 // D_MODEL), halo,
                rows(CONV_W, ZC0 // CONV_W)]
    in_specs += [rows(512, G0 // 512 + j) for j in range(4)]
    in_specs += [rows(ATTN_W, g) for g in range(N_GROUPS)] * 2
    in_specs += [const(wa.shape), const(wc.shape), const(wo.shape), const((1, D_MODEL)), const((1, D_MODEL)),
                 const((1, D_MODEL)), const(convw.shape), const((1, CONV_W)), const((1, CONV_W)), const((1, CONV_W)),
                 const(bd.shape)]
    tcol = lambda width: pl.BlockSpec((width, tr), lambda i: (0, i))
    out_specs = [rows(D_MODEL), rows(ATTN_W), rows(ATTN_W), rows(CONV_W),
                 tcol(D_MODEL), tcol(ATTN_W), tcol(CONV_W), rows(D_MODEL), rows(D_MODEL), rows(D_MODEL),
                 pl.BlockSpec(memory_space=pl.ANY),
                 const((8, D_MODEL)), const((8, 2 * D_MODEL)), const((8, CONV_W)), const((8, CONV_W)), const((8, CONV_W)),
                 const((8, LANE))]
    out_shape = [SDS((s, D_MODEL), F32), SDS((s, ATTN_W), F32), SDS((s, ATTN_W), F32),
                 SDS((s, CONV_W), F32),
                 SDS((D_MODEL, s), BF16), SDS((ATTN_W, s), BF16), SDS((CONV_W, s), BF16),
                 SDS((s, D_MODEL), BF16), SDS((s, D_MODEL), BF16), SDS((s, D_MODEL), BF16),
                 SDS((s, IN_W), BF16),
                 SDS((8, D_MODEL), F32), SDS((8, 2 * D_MODEL), F32), SDS((8, CONV_W), F32), SDS((8, CONV_W), F32),
                 SDS((8, CONV_W), F32), SDS((8, LANE), F32)]
    return pl.pallas_call(
        body, name="tail", grid=(s // tr,), in_specs=in_specs, out_specs=out_specs, out_shape=out_shape,
        scratch_shapes=[pltpu.VMEM((CONV_HALO + tr, CONV_W), F32), pltpu.VMEM((CONV_HALO + tr, CONV_W), F32),
                        pltpu.VMEM((tr, ATTN_W), BF16),
                        pltpu.VMEM((tr, CONV_W), BF16), pltpu.VMEM((tr, 2 * D_MODEL), BF16),
                        pltpu.SemaphoreType.DMA((3,))],
        compiler_params=_params(),
    )(x, tgt, proj, proj, proj, proj, proj, proj, proj, proj, *o3, *l3, wa, wc, wo, gate, bga, bgc,
      convw, convb, lnw, lnb, bd)


def _conv_bwd(dcv, proj, convw, dproj):
    s = dcv.shape[0]
    tr = 128
    nt = s // tr

    def body(dcv_ref, dcvn_ref, u_ref, uh_ref, cw_ref, dp_in, dp_out, dw_ref, extg, extd, sh):
        del dp_in
        i = pl.program_id(0)

        @pl.when(i == 0)
        def _():
            dw_ref[...] = jnp.zeros_like(dw_ref)

        _, _, glu = _glu(u_ref[...])
        _, _, glu_h = _glu(uh_ref[...])
        extg[0:CONV_HALO, :] = jnp.where(i > 0, glu_h, 0.0)
        extg[CONV_HALO:CONV_HALO + tr, :] = glu
        extd[0:tr, :] = dcv_ref[...]
        extd[tr:tr + CONV_HALO, :] = jnp.where(i < nt - 1, dcvn_ref[...], 0.0)
        for cb in range(CONV_W // LANE):
            cols = slice(cb * LANE, (cb + 1) * LANE)
            dglu = jnp.zeros((tr, LANE), F32)
            for j, rows in _tap_views(extd, sh, [CONV_K - 1 - j for j in range(CONV_K)], tr, cols):
                dglu = dglu + cw_ref[j:j + 1, cols] * rows
            dcv_c = dcv_ref[:, cols]
            for j, rows in _tap_views(extg, sh, [CONV_HALO - (CONV_K - 1) + j for j in range(CONV_K)], tr, cols):
                dw_ref[8 * j:8 * j + 8, cols] += jnp.sum((dcv_c * rows).reshape(tr // 8, 8, LANE), axis=0)
            a_h = u_ref[:, cols]
            sgb = _sigmoid(u_ref[:, CONV_W + cb * LANE:CONV_W + (cb + 1) * LANE])
            dp_out[:, cols] = (dglu * sgb).astype(BF16)
            dp_out[:, CONV_W + cb * LANE:CONV_W + (cb + 1) * LANE] = (dglu * a_h * (sgb * (1.0 - sgb))).astype(BF16)

    ucol = U0 // D_MODEL
    return pl.pallas_call(
        body, name="conv_bwd", grid=(nt,),
        in_specs=[pl.BlockSpec((tr, CONV_W), lambda i: (i, 0)),
                  pl.BlockSpec((CONV_HALO, CONV_W), lambda i: (jnp.minimum((i + 1) * (tr // CONV_HALO), s // CONV_HALO - 1), 0)),
                  pl.BlockSpec((tr, D_MODEL), lambda i: (i, ucol)),
                  pl.BlockSpec((CONV_HALO, D_MODEL), lambda i: (jnp.maximum(i * (tr // CONV_HALO) - 1, 0), ucol)),
                  pl.BlockSpec(convw.shape, lambda i: (0, 0)),
                  pl.BlockSpec(memory_space=pl.ANY)],
        out_specs=[pl.BlockSpec((tr, D_MODEL), lambda i: (i, ucol)), pl.BlockSpec((8 * CONV_HALO, CONV_W), lambda i: (0, 0))],
        out_shape=[SDS(dproj.shape, dproj.dtype), SDS((8 * CONV_HALO, CONV_W), F32)],
        input_output_aliases={5: 0},
        scratch_shapes=[pltpu.VMEM((CONV_HALO + tr, CONV_W), F32)] * 3,
        compiler_params=_params(),
    )(dcv, dcv, proj, proj, convw, dproj)


def _mm_acc(at, b, name, col_slots):
    m, s = at.shape
    n = b.shape[1]
    tk = 1024
    nk = s // tk

    def body(a_ref, b_ref, o_ref, acc):
        k = pl.program_id(0)

        @pl.when(k == 0)
        def _():
            acc[...] = jnp.zeros_like(acc)

        acc[...] += _dot(a_ref[...], b_ref[...])

        @pl.when(k == nk - 1)
        def _():
            if col_slots:
                w = n // N_DEV
                for j in range(N_DEV):
                    o_ref[j] = acc[:, j * w:(j + 1) * w].astype(BF16)
            else:
                o_ref[...] = acc[...].astype(BF16)

    if col_slots:
        out_shape = SDS((N_DEV, m, n // N_DEV), BF16)
        out_spec = pl.BlockSpec((N_DEV, m, n // N_DEV), lambda k: (0, 0, 0))
    else:
        out_shape = SDS((m, n), BF16)
        out_spec = pl.BlockSpec((m, n), lambda k: (0, 0))
    return pl.pallas_call(
        body, name=name, grid=(nk,),
        in_specs=[pl.BlockSpec((m, tk), lambda k: (0, k)), pl.BlockSpec((tk, n), lambda k: (k, 0))],
        out_specs=out_spec, out_shape=out_shape, scratch_shapes=[pltpu.VMEM((m, n), F32)],
        compiler_params=_params(),
    )(at, b)


def _mm_dw(ht, dproj):
    s = ht.shape[1]
    tk = 1024
    nk = s // tk

    def body(a_ref, b_ref, o_ref, acc):
        k = pl.program_id(1)

        @pl.when(k == 0)
        def _():
            acc[...] = jnp.zeros_like(acc)

        acc[...] += _dot(a_ref[...], b_ref[...])

        @pl.when(k == nk - 1)
        def _():
            o_ref[...] = acc[...].T.astype(BF16)

    return pl.pallas_call(
        body, name="mm_dw", grid=(IN_W // PAIR_W, nk),
        in_specs=[pl.BlockSpec((D_MODEL, tk), lambda p, k: (0, k)), pl.BlockSpec((tk, PAIR_W), lambda p, k: (k, p))],
        out_specs=pl.BlockSpec((PAIR_W, D_MODEL), lambda p, k: (p, 0)),
        out_shape=SDS((IN_W, D_MODEL), BF16), scratch_shapes=[pltpu.VMEM((D_MODEL, PAIR_W), F32)],
        compiler_params=_params(),
    )(ht, dproj)


def _mm_dh_norm_bwd(dproj, wt, x, dout, norm_w, scale, token):
    s = dproj.shape[0]
    tm = 1024
    n_p = IN_W // PAIR_W

    def body(dp_ref, w_ref, x_ref, do_ref, nw_ref, sc_ref, tok_ref, gx_ref, dsh_ref, dsc_ref, dnw_ref, dh_acc):
        del tok_ref
        m, p = pl.program_id(0), pl.program_id(1)
        part = _dot(dp_ref[...], w_ref[...])

        @pl.when(p == 0)
        def _():
            dh_acc[...] = part

        @pl.when(p > 0)
        def _():
            dh_acc[...] += part

        @pl.when((m == 0) & (p == 0))
        def _():
            for r in (dsh_ref, dsc_ref, dnw_ref):
                r[...] = jnp.zeros_like(r)

        @pl.when(p == n_p - 1)
        def _():
            def acc_rows(ref, v):
                ref[...] += jnp.broadcast_to(jnp.sum(v, axis=0, keepdims=True), ref.shape)

            xv = x_ref[...]
            dh_v = dh_acc[...]
            r = lax.rsqrt(jnp.mean(xv * xv, axis=-1, keepdims=True) + EPS)
            xn = xv * r
            one_sc = 1.0 + sc_ref[...]
            acc_rows(dsh_ref, dh_v)
            acc_rows(dsc_ref, dh_v * (xn * nw_ref[...]))
            acc_rows(dnw_ref, dh_v * xn * one_sc)
            dxn = dh_v * (nw_ref[...] * one_sc)
            gx_ref[...] = do_ref[...] + r * (dxn - xn * jnp.mean(dxn * xn, axis=-1, keepdims=True))

    rows = pl.BlockSpec((tm, D_MODEL), lambda m, p: (m, 0))
    vec = pl.BlockSpec((1, D_MODEL), lambda m, p: (0, 0))
    acc = pl.BlockSpec((8, D_MODEL), lambda m, p: (0, 0))
    return pl.pallas_call(
        body, name="mm_dh_norm_bwd", grid=(s // tm, n_p),
        in_specs=[pl.BlockSpec((tm, PAIR_W), lambda m, p: (m, p)),
                  pl.BlockSpec((PAIR_W, D_MODEL), lambda m, p: (p, 0)),
                  rows, rows, vec, vec, pl.BlockSpec(token.shape, lambda m, p: (0, 0))],
        out_specs=[rows, acc, acc, acc],
        out_shape=[SDS((s, D_MODEL), F32)] + [SDS((8, D_MODEL), F32)] * 3,
        scratch_shapes=[pltpu.VMEM((tm, D_MODEL), F32)], compiler_params=_params(),
    )(dproj, wt, x, dout, norm_w, scale, token)


SMALL_ROWS = 8
QN_COL, KN_COL, CB_COL, LOSS_COL = 0, LANE, 2 * LANE, 2 * LANE + CONV_W


def _pack_partials(dsh, dsc, dgate, dnw, dbg, dqw3, dkw3, dcb, dlnw, dlnb, loss_p):
    n3 = len(dqw3)

    def body(*refs):
        dsh_r, dsc_r, dgate_r, dnw_r, dbg_r = refs[:5]
        dq_r, dk_r = refs[5:5 + n3], refs[5 + n3:5 + 2 * n3]
        dcb_r, dlnw_r, dlnb_r, loss_r, o_ref = refs[5 + 2 * n3:]

        def both_heads(rs):
            t = rs[0][0:1, :]
            for r in rs[1:]:
                t = t + r[0:1, :]
            return t + pltpu.roll(t, HEAD_DIM, axis=1)

        o_ref[0:1, :] = dsh_r[0:1, :]
        o_ref[1:2, :] = dsc_r[0:1, :]
        o_ref[2:3, :] = dgate_r[0:1, :]
        o_ref[3:4, :] = dnw_r[0:1, :]
        o_ref[4:5, :] = dbg_r[0:1, 0:D_MODEL]
        o_ref[5:6, :] = dbg_r[0:1, D_MODEL:]
        o_ref[6:7, QN_COL:QN_COL + LANE] = both_heads(dq_r)
        o_ref[6:7, KN_COL:KN_COL + LANE] = both_heads(dk_r)
        o_ref[6:7, CB_COL:CB_COL + CONV_W] = dcb_r[0:1, :]
        o_ref[6:7, LOSS_COL:LOSS_COL + LANE] = loss_r[0:1, :]
        o_ref[6:7, LOSS_COL + LANE:] = jnp.zeros((1, D_MODEL - LOSS_COL - LANE), F32)
        o_ref[7:8, 0:CONV_W] = dlnw_r[0:1, :]
        o_ref[7:8, CONV_W:] = dlnb_r[0:1, :]

    return pl.pallas_call(body, name="pack_partials", out_shape=SDS((SMALL_ROWS, D_MODEL), F32),
                          compiler_params=_params())(dsh, dsc, dgate, dnw, dbg, *dqw3, *dkw3, dcb, dlnw, dlnb, loss_p)


def _adamw_update(g, w, m, v):
    bc1 = 1.0 - ADAM_B1 ** ADAM_STEP
    bc2 = 1.0 - ADAM_B2 ** ADAM_STEP
    m_new = ADAM_B1 * m + (1.0 - ADAM_B1) * g
    v_new = ADAM_B2 * v + (1.0 - ADAM_B2) * (g * g)
    delta = -ADAM_LR * ((m_new / bc1) / (jnp.sqrt(v_new / bc2) + ADAM_EPS) + ADAM_WD * w)
    return delta, m_new, v_new


def _adamw_small(small_all, ws, ms, vs):
    n = len(ws)
    where = [(slice(0, 3), None), (slice(3, 4), None), (slice(4, 6), None), (6, QN_COL), (6, KN_COL), (6, CB_COL),
             (7, 0), (7, CONV_W)]

    def body(*refs):
        g_ref = refs[0]
        w_r, m_r, v_r = refs[1:1 + n], refs[1 + n:1 + 2 * n], refs[1 + 2 * n:1 + 3 * n]
        outs = refs[1 + 3 * n:]
        g_o, d_o, m_o, v_o, loss_o = outs[:n], outs[n:2 * n], outs[2 * n:3 * n], outs[3 * n:4 * n], outs[4 * n]
        gsum = g_ref[0]
        for dev in range(1, N_DEV):
            gsum = gsum + g_ref[dev]
        loss_o[...] = gsum[6:7, LOSS_COL:LOSS_COL + LANE]
        for i, (rows, col) in enumerate(where):
            width = w_r[i].shape[1]
            if col is None:
                g = jnp.concatenate([gsum[r:r + 1, :] for r in range(rows.start, rows.stop)], axis=1)
            else:
                g = gsum[rows:rows + 1, col:col + width]
            delta, m_new, v_new = _adamw_update(g, w_r[i][...], m_r[i][...], v_r[i][...])
            g_o[i][...] = g
            d_o[i][...] = delta
            m_o[i][...] = m_new
            v_o[i][...] = v_new

    shapes = [SDS(w.shape, F32) for w in ws]
    res = pl.pallas_call(body, name="adamw_small", out_shape=shapes * 4 + [SDS((1, LANE), F32)],
                         compiler_params=_params())(small_all, *ws, *ms, *vs)
    return [res[k * n:(k + 1) * n] for k in range(4)], res[4 * n]


def _row_tile(rows):
    if rows <= 128:
        return rows
    if rows % 256 == 0:
        return 256
    return 128 if rows % 128 == 0 else SHARD_W // 4


def _adamw(gsrc, w, m, v, name, stacked):
    rows, cols = w.shape
    tr = _row_tile(rows)
    n_src = len(gsrc) if stacked else 1

    def body(*refs):
        g_refs, (w_ref, m_ref, v_ref, go_ref, d_ref, mo_ref, vo_ref) = refs[:n_src], refs[n_src:]
        if stacked:
            g = None
            for g_ref, (_, slots) in zip(g_refs, gsrc):
                for j in range(slots):
                    t = g_ref[j].astype(F32)
                    g = t if g is None else g + t
        else:
            g = g_refs[0][...]
        delta, m_new, v_new = _adamw_update(g, w_ref[...], m_ref[...], v_ref[...])
        go_ref[...] = g
        d_ref[...] = delta
        mo_ref[...] = m_new
        vo_ref[...] = v_new

    blk = pl.BlockSpec((tr, cols), lambda i: (i, 0))
    if stacked:
        gspecs = [pl.BlockSpec((slots, tr, arr.shape[2]), lambda i: (0, i, 0)) for arr, slots in gsrc]
        gargs = [arr for arr, _ in gsrc]
    else:
        gspecs, gargs = [blk], [gsrc]
    in_specs = gspecs + [blk, blk, blk]
    args = gargs + [w, m, v]
    return pl.pallas_call(
        body, name=name, grid=(rows // tr,), in_specs=in_specs, out_specs=[blk] * 4,
        out_shape=[SDS((rows, cols), F32)] * 4, compiler_params=_params(),
    )(*args)


def kernel(x, c, w_ada, b_ada, norm_w, w_in, b_gate, q_norm_w, k_norm_w, w_attn_proj, conv_w, conv_b, conv_ln_w, conv_ln_b, w_conv_proj, w_out, loss_target, m_w_ada, m_b_ada, m_norm_w, m_w_in, m_b_gate, m_q_norm_w, m_k_norm_w, m_w_attn_proj, m_conv_w, m_conv_b, m_conv_ln_w, m_conv_ln_b, m_w_conv_proj, m_w_out, v_w_ada, v_b_ada, v_norm_w, v_w_in, v_b_gate, v_q_norm_w, v_k_norm_w, v_w_attn_proj, v_conv_w, v_conv_b, v_conv_ln_w, v_conv_ln_b, v_w_conv_proj, v_w_out):
    xi, yi, ci = lax.axis_index("x"), lax.axis_index("y"), lax.axis_index("c")
    me = 4 * xi + 2 * yi + ci
    x2, tgt2 = x[0], loss_target[0]
    w_in_t, m_w_in_t, v_w_in_t = (jnp.transpose(a[0]) for a in (w_in, m_w_in, v_w_in))
    s = x2.shape[0]

    cw_flat = jnp.pad(conv_w[0].reshape(1, -1), ((0, 0), (0, CONVW_FLAT - CONV_K * HEAD_DIM)))
    pre = jnp.concatenate([c, cw_flat], axis=1).reshape(8, -1)
    (pre_all,) = _all_gather([pre], "gather_c_convw", vmem=True)
    pre_all = pre_all.reshape(N_DEV, -1)
    c_all = pre_all[:, :D_MODEL]
    convw_full = pre_all[:, D_MODEL:D_MODEL + CONV_K * HEAD_DIM].reshape(N_DEV, CONV_K, HEAD_DIM)
    convw_full = jnp.transpose(convw_full, (1, 0, 2)).reshape(CONV_K, CONV_W)
    convw_pad = jnp.pad(convw_full, ((0, CONV_HALO - CONV_K), (0, 0)))

    ada_part = _ada_fwd(c_all, w_ada[0])
    (ada_all,) = _all_gather([ada_part], "gather_ada", vmem=True)
    ada = lax.dynamic_index_in_dim(ada_all, me, axis=1, keepdims=False).reshape(1, 3 * D_MODEL) + b_ada
    shift, scale, gate = ada[:, :D_MODEL], ada[:, D_MODEL:2 * D_MODEL], ada[:, 2 * D_MODEL:]

    wt_g, wa_g, wc_g, wo_g = _all_gather_chips(
        [_cast_bf16(w_in_t, "cast_win"), _cast_bf16(w_attn_proj[0], "cast_wa"), _cast_bf16(w_conv_proj[0], "cast_wc"),
         _cast_bf16(w_out[0], "cast_wo")], "gather_weights")
    wt = wt_g.reshape(IN_W, D_MODEL)
    wa = _cols_from_slots(wa_g, "cols_wa")
    wc = _cols_from_slots(wc_g, "cols_wc")
    wo = wo_g.reshape(D_MODEL, D_MODEL)

    h, ht = _norm_fwd(x2, norm_w, scale, shift)
    proj = _mm_in(h, wt)
    qw2 = jnp.tile(q_norm_w, (1, 2))
    kw2 = jnp.tile(k_norm_w, (1, 2))
    o_all, l_all, qn, kn, vn = _attn_fwd(proj, qw2, kw2)
    o3, l3 = [o_all] * N_GROUPS, [l_all] * N_GROUPS
    head_id = jnp.arange(ATTN_W) // HEAD_DIM
    bd = (head_id[:, None] == head_id[None, :]).astype(BF16)
    (dout, da, lse_delta, dcv, mt, yat, yct, dmo, dya, dyc, dproj,
     dgate, dbg, dlnw, dlnb, dcb, loss_p) = _tail(
        x2, tgt2, proj, o3, l3, wa, wc, wo, gate, b_gate[:, :D_MODEL], b_gate[:, D_MODEL:], convw_pad,
        conv_b, conv_ln_w, conv_ln_b, bd)

    dproj, dconvw8 = _conv_bwd(dcv, proj, convw_pad, dproj)
    dconvw = jnp.sum(dconvw8.reshape(CONV_HALO, 8, CONV_W), axis=1)
    dproj, dqw_all, dkw_all = _attn_bwd(proj, qn, kn, vn, da, lse_delta, qw2, kw2, dproj)
    dqw_g3, dkw_g3 = [dqw_all], [dkw_all]
    dw_in_p = _mm_dw(ht, dproj).reshape(N_DEV, SHARD_W, D_MODEL)
    dwo_p = _mm_acc(mt, dmo, "mm_dwo", col_slots=False).reshape(N_DEV, D_MODEL // N_DEV, D_MODEL)
    dwa_p = _mm_acc(yat, dya, "mm_dwa", col_slots=True)
    dwc_p = _mm_acc(yct, dyc, "mm_dwc", col_slots=True)

    partials = [dw_in_p, dwa_p, dwc_p, dwo_p]
    me_arr = jnp.reshape(me, (1,)).astype(jnp.int32)
    from_sib = _exchange_sibling(partials, "exchange_sibling")
    presums = [_presum(p, f, me_arr, f"presum{i}") for i, (p, f) in enumerate(zip(partials, from_sib))]
    s_sems, r_sems, pre_thru, land_thru, token = _exchange_chips_start(presums, "exchange_chips_start")
    gx, dsh, dsc, dnw = _mm_dh_norm_bwd(dproj, wt, x2, dout, norm_w, scale, token)
    small_p = _pack_partials(dsh, dsc, dgate, dnw, dbg, dqw_g3, dkw_g3, dcb, dlnw, dlnb, loss_p)
    small_all, dconvw_all = _all_gather([small_p, dconvw], "gather_small", vmem=True)

    small_w = (b_ada, norm_w, b_gate, q_norm_w, k_norm_w, conv_b, conv_ln_w, conv_ln_b)
    small_m = (m_b_ada, m_norm_w, m_b_gate, m_q_norm_w, m_k_norm_w, m_conv_b, m_conv_ln_w, m_conv_ln_b)
    small_v = (v_b_ada, v_norm_w, v_b_gate, v_q_norm_w, v_k_norm_w, v_conv_b, v_conv_ln_w, v_conv_ln_b)
    r_small, loss_row = _adamw_small(small_all, small_w, small_m, small_v)
    dcw_mine = lax.dynamic_slice_in_dim(dconvw_all[:, :CONV_K, :], me * HEAD_DIM, HEAD_DIM, axis=2)
    r_convw = _adamw([(dcw_mine, N_DEV)], conv_w[0], m_conv_w[0], v_conv_w[0], "adamw_conv_w", stacked=True)

    d_ada_all = small_all[:, 0:3, :].reshape(N_DEV, 3 * D_MODEL)
    d_ada_cols = lax.dynamic_slice_in_dim(d_ada_all, me * (3 * D_MODEL // N_DEV), 3 * D_MODEL // N_DEV, axis=1)
    g_wada = _ada_bwd(c_all, d_ada_cols)
    r_ada = _adamw(g_wada, w_ada[0], m_w_ada[0], v_w_ada[0], "adamw_w_ada", stacked=False)
    pres, lands = _exchange_chips_wait(s_sems, r_sems, pre_thru, land_thru, r_ada[1], "exchange_chips_wait")
    terms = [[(p, 1), (l, len(CHIP_K))] for p, l in zip(pres, lands)]
    r_win = [jnp.transpose(r) for r in _adamw(terms[0], w_in_t, m_w_in_t, v_w_in_t, "adamw_w_in", stacked=True)]
    r_wap = _adamw(terms[1], w_attn_proj[0], m_w_attn_proj[0], v_w_attn_proj[0], "adamw_w_attn_proj", stacked=True)
    r_wcp = _adamw(terms[2], w_conv_proj[0], m_w_conv_proj[0], v_w_conv_proj[0], "adamw_w_conv_proj", stacked=True)
    r_wout = _adamw(terms[3], w_out[0], m_w_out[0], v_w_out[0], "adamw_w_out", stacked=True)

    outs = [loss_row[0, 0], gx[None]]
    for k in range(4):
        b_ada_k, norm_w_k, b_gate_k, qn_k, kn_k, conv_b_k, ln_w_k, ln_b_k = r_small[k]
        outs += [r_ada[k][None], b_ada_k, norm_w_k, r_win[k][None], b_gate_k, qn_k, kn_k, r_wap[k][None],
                 r_convw[k][None], conv_b_k, ln_w_k, ln_b_k, r_wcp[k][None], r_wout[k][None]]
    return tuple(outs)
```

```python
import functools

import jax
import jax.numpy as jnp
from jax import lax
from jax.experimental import pallas as pl
from jax.experimental.pallas import tpu as pltpu

F32 = jnp.float32
BF16 = jnp.bfloat16
SDS = jax.ShapeDtypeStruct
MESH = pl.DeviceIdType.MESH

N_DEV = 8
D_MODEL = 1024
HEAD_DIM = 64
N_GROUPS = 3
DILATIONS = (1, 4, 16)
BAND = 128
BWD_UNROLL = 8
ATTN_W = 512
CONV_W = 512
CONV_K = 31
CONV_HALO = 32
IN_W = 8704
SHARD_W = IN_W // N_DEV
PAIR_W = 2 * SHARD_W
Q0, K0, V0, ZA0, U0, ZC0, G0 = 0, 1536, 3072, 4608, 5120, 6144, 6656
EPS = 1e-6
LANE = 128
VMEM_LIMIT = 56 * 1024 * 1024

ADAM_LR, ADAM_B1, ADAM_B2, ADAM_EPS, ADAM_WD, ADAM_STEP = 0.001, 0.9, 0.999, 1e-08, 0.01, 10

CONVW_FLAT = 2048


def _params(**kw):
    return pltpu.CompilerParams(vmem_limit_bytes=VMEM_LIMIT, **kw)


def _sigmoid(z):
    return 0.5 * jnp.tanh(0.5 * z) + 0.5


def _dot(a, b):
    return jnp.dot(a, b, preferred_element_type=F32)


def _dot_nt(a, b):
    return lax.dot_general(a, b, (((1,), (1,)), ((), ())), preferred_element_type=F32)


def _dot_tn(a, b):
    return lax.dot_general(a, b, (((0,), (0,)), ((), ())), preferred_element_type=F32)


def _peer(x, y, c, k):
    px = 1 - x if (k >> 2) & 1 else x
    py = 1 - y if (k >> 1) & 1 else y
    pc = 1 - c if k & 1 else c
    return (px, py, pc), 4 * px + 2 * py + pc


def _all_gather(arrays, name, vmem):
    n = len(arrays)
    space = pltpu.VMEM if vmem else pl.ANY

    def body(*refs):
        ins, outs = refs[:n], refs[n:2 * n]
        send_sems, recv_sems, local_sems = refs[2 * n:]
        x, y, c = lax.axis_index("x"), lax.axis_index("y"), lax.axis_index("c")
        me = 4 * x + 2 * y + c
        locals_ = [pltpu.make_async_copy(ins[a], outs[a].at[me], local_sems.at[a]) for a in range(n)]
        for cp in locals_:
            cp.start()
        sends = []
        for k in range(1, N_DEV):
            peer, _ = _peer(x, y, c, k)
            for a in range(n):
                cp = pltpu.make_async_remote_copy(
                    src_ref=ins[a], dst_ref=outs[a].at[me], send_sem=send_sems.at[a, k - 1],
                    recv_sem=recv_sems.at[a, k - 1], device_id=peer, device_id_type=MESH)
                cp.start()
                sends.append(cp)
        for k in range(1, N_DEV):
            peer, pidx = _peer(x, y, c, k)
            for a in range(n):
                pltpu.make_async_remote_copy(
                    src_ref=ins[a], dst_ref=outs[a].at[pidx], send_sem=send_sems.at[a, k - 1],
                    recv_sem=recv_sems.at[a, k - 1], device_id=peer, device_id_type=MESH).wait_recv()
        for cp in sends:
            cp.wait_send()
        for cp in locals_:
            cp.wait()

    return pl.pallas_call(
        body, name=name,
        out_shape=[SDS((N_DEV,) + a.shape, a.dtype) for a in arrays],
        in_specs=[pl.BlockSpec(memory_space=space)] * n,
        out_specs=[pl.BlockSpec(memory_space=space)] * n,
        scratch_shapes=[pltpu.SemaphoreType.DMA((n, N_DEV - 1)), pltpu.SemaphoreType.DMA((n, N_DEV - 1)),
                        pltpu.SemaphoreType.DMA((n,))],
        compiler_params=_params(),
    )(*arrays)


CHIP_K = (2, 4, 6)


def _all_gather_chips(arrays, name):
    n = len(arrays)
    k_y, k_x, k_d = CHIP_K

    def body(*refs):
        ins, outs = refs[:n], refs[n:2 * n]
        send_sems, recv_sems, local_sems = refs[2 * n:]
        x, y, c = lax.axis_index("x"), lax.axis_index("y"), lax.axis_index("c")
        me = 4 * x + 2 * y + c
        sib, sib_idx = _peer(x, y, c, 1)
        nbr_y, idx_y = _peer(x, y, c, k_y)
        nbr_x, idx_x = _peer(x, y, c, k_x)
        _, idx_d = _peer(x, y, c, k_d)

        def copy(a, slot, block, to, src=None):
            return pltpu.make_async_remote_copy(
                src_ref=outs[a].at[block] if src is None else src, dst_ref=outs[a].at[block],
                send_sem=send_sems.at[a, slot], recv_sem=recv_sems.at[a, slot], device_id=to, device_id_type=MESH)

        locals_ = [pltpu.make_async_copy(ins[a], outs[a].at[me], local_sems.at[a]) for a in range(n)]
        for cp in locals_:
            cp.start()
        for a in range(n):
            copy(a, 0, me, sib, src=ins[a]).start()
            copy(a, 1, me, nbr_y, src=ins[a]).start()
            copy(a, 2, me, nbr_x, src=ins[a]).start()

        def arrived(slot, block, frm, send_on_to=None):
            for a in range(n):
                copy(a, slot, block, frm).wait_recv()
                if send_on_to is not None:
                    copy(a, 3, block, send_on_to).start()
                copy(a, 3 + slot, block, sib).start()

        @pl.when(c == 0)
        def _():
            arrived(1, idx_y, nbr_y, send_on_to=nbr_x)
            arrived(2, idx_x, nbr_x)

        @pl.when(c == 1)
        def _():
            arrived(2, idx_x, nbr_x, send_on_to=nbr_y)
            arrived(1, idx_y, nbr_y)

        arrived(3, idx_d, nbr_x)
        for a in range(n):
            copy(a, 0, sib_idx, sib).wait_recv()
        for slot, k in ((4, k_y), (5, k_x), (6, k_d)):
            _, pidx = _peer(x, y, 1 - c, k)
            for a in range(n):
                copy(a, slot, pidx, sib).wait_recv()
        for slot in range(N_DEV - 1):
            for a in range(n):
                copy(a, slot, me, sib).wait_send()
        for cp in locals_:
            cp.wait()

    return pl.pallas_call(
        body, name=name,
        out_shape=[SDS((N_DEV,) + a.shape, a.dtype) for a in arrays],
        in_specs=[pl.BlockSpec(memory_space=pl.ANY)] * n,
        out_specs=[pl.BlockSpec(memory_space=pl.ANY)] * n,
        scratch_shapes=[pltpu.SemaphoreType.DMA((n, N_DEV - 1)), pltpu.SemaphoreType.DMA((n, N_DEV - 1)),
                        pltpu.SemaphoreType.DMA((n,))],
        compiler_params=_params(),
    )(*arrays)


def _exchange_sibling(arrays, name):
    n = len(arrays)
    ks = (0,) + CHIP_K

    def body(*refs):
        ins, outs = refs[:n], refs[n:2 * n]
        send_sems, recv_sems = refs[2 * n:]
        x, y, c = lax.axis_index("x"), lax.axis_index("y"), lax.axis_index("c")
        sib, sib_idx = _peer(x, y, c, 1)
        sends = []
        for i, k in enumerate(ks):
            _, tgt = _peer(x, y, 1 - c, k) if k else (None, sib_idx)
            for a in range(n):
                cp = pltpu.make_async_remote_copy(
                    src_ref=ins[a].at[tgt], dst_ref=outs[a].at[i], send_sem=send_sems.at[a, i],
                    recv_sem=recv_sems.at[a, i], device_id=sib, device_id_type=MESH)
                cp.start()
                sends.append(cp)
        for cp in sends:
            cp.wait_recv()
        for cp in sends:
            cp.wait_send()

    return pl.pallas_call(
        body, name=name,
        out_shape=[SDS((len(ks),) + a.shape[1:], a.dtype) for a in arrays],
        in_specs=[pl.BlockSpec(memory_space=pl.ANY)] * n,
        out_specs=[pl.BlockSpec(memory_space=pl.ANY)] * n,
        scratch_shapes=[pltpu.SemaphoreType.DMA((n, len(ks))), pltpu.SemaphoreType.DMA((n, len(ks)))],
        compiler_params=_params(),
    )(*arrays)


def _presum(mine, from_sib, me_arr, name):
    _, rows, cols = mine.shape
    tr = _row_tile(rows)
    ns = 1 + len(CHIP_K)

    def body(me_ref, a_ref, b_ref, o_ref):
        del me_ref
        o_ref[...] = (a_ref[...].astype(F32) + b_ref[...].astype(F32)).astype(o_ref.dtype)

    grid_spec = pltpu.PrefetchScalarGridSpec(
        num_scalar_prefetch=1, grid=(ns, rows // tr),
        in_specs=[pl.BlockSpec((1, tr, cols), lambda j, i, me: (jnp.bitwise_xor(me[0], 2 * j), i, 0)),
                  pl.BlockSpec((1, tr, cols), lambda j, i, me: (j, i, 0))],
        out_specs=pl.BlockSpec((1, tr, cols), lambda j, i, me: (j, i, 0)))
    return pl.pallas_call(body, name=name, grid_spec=grid_spec, out_shape=SDS((ns, rows, cols), mine.dtype),
                          compiler_params=_params())(me_arr, mine, from_sib)


HBM_SPEC = pl.BlockSpec(memory_space=pltpu.HBM)
SEM_SPEC = pl.BlockSpec(memory_space=pltpu.SEMAPHORE)
SIDE_EFFECT = pltpu.SideEffectType.DATAFLOW_SIDE_EFFECTING


def _chips_copies(pre_refs, land_refs, send_sems, recv_sems):
    x, y, c = lax.axis_index("x"), lax.axis_index("y"), lax.axis_index("c")
    copies = []
    for j, k in enumerate(CHIP_K):
        peer, _ = _peer(x, y, c, k)
        for a in range(len(pre_refs)):
            copies.append(pltpu.make_async_remote_copy(
                src_ref=pre_refs[a].at[1 + j], dst_ref=land_refs[a].at[j], send_sem=send_sems.at[a * len(CHIP_K) + j],
                recv_sem=recv_sems.at[a * len(CHIP_K) + j], device_id=peer, device_id_type=MESH))
    return copies


def _exchange_chips_start(presums, name):
    n = len(presums)

    def body(*refs):
        pre, land = refs[:n], refs[n:2 * n]
        send_sems, recv_sems = refs[2 * n], refs[2 * n + 1]
        token = refs[-1]
        for cp in _chips_copies(pre, land, send_sems, recv_sems):
            cp.start()
        token[...] = jnp.zeros_like(token)

    nk = len(CHIP_K)
    hbm = [pltpu.HBM(p.shape, p.dtype) for p in presums]
    hbm_land = [pltpu.HBM((nk,) + p.shape[1:], p.dtype) for p in presums]
    res = pl.pallas_call(
        body, name=name,
        out_shape=(pltpu.SemaphoreType.DMA((n * nk,)), pltpu.SemaphoreType.DMA((n * nk,)), *hbm, *hbm_land, SDS((8, LANE), F32)),
        in_specs=[HBM_SPEC] * (2 * n),
        out_specs=(SEM_SPEC, SEM_SPEC, *([HBM_SPEC] * (2 * n)), pl.BlockSpec(memory_space=pltpu.VMEM)),
        input_output_aliases={i: 2 + i for i in range(2 * n)},
        compiler_params=pltpu.CompilerParams(has_side_effects=SIDE_EFFECT),
    )(*[pltpu.with_memory_space_constraint(p, pltpu.HBM) for p in presums],
      *[pltpu.with_memory_space_constraint(lax.empty((nk,) + p.shape[1:], p.dtype), pltpu.HBM) for p in presums])
    return res[0], res[1], res[2:2 + n], res[2 + n:2 + 2 * n], res[-1]


def _exchange_chips_wait(send_sems, recv_sems, pre_thru, land_thru, after, name):
    n = len(pre_thru)

    def body(*refs):
        pre, land = refs[:n], refs[n:2 * n]
        s_sems, r_sems = refs[2 * n], refs[2 * n + 1]
        for cp in _chips_copies(pre, land, s_sems, r_sems):
            cp.wait_send()
            cp.wait_recv()

    hbm = [pltpu.HBM(p.shape, p.dtype) for p in (*pre_thru, *land_thru)]
    res = pl.pallas_call(
        body, name=name, out_shape=tuple(hbm),
        in_specs=[HBM_SPEC] * (2 * n) + [SEM_SPEC, SEM_SPEC, pl.BlockSpec(memory_space=pl.ANY)],
        out_specs=tuple([HBM_SPEC] * (2 * n)),
        input_output_aliases={i: i for i in range(2 * n)},
        compiler_params=pltpu.CompilerParams(has_side_effects=SIDE_EFFECT),
    )(*pre_thru, *land_thru, send_sems, recv_sems, after)
    return res[:n], res[n:]


def _exchange_chips(presums, name):
    n = len(presums)
    nk = len(CHIP_K)

    def body(*refs):
        pre, land = refs[:n], refs[n:2 * n]
        send_sems, recv_sems = refs[2 * n:]
        copies = _chips_copies(pre, land, send_sems, recv_sems)
        for cp in copies:
            cp.start()
        for cp in copies:
            cp.wait_recv()
        for cp in copies:
            cp.wait_send()

    return pl.pallas_call(
        body, name=name,
        out_shape=[SDS((nk,) + p.shape[1:], p.dtype) for p in presums],
        in_specs=[pl.BlockSpec(memory_space=pl.ANY)] * n,
        out_specs=[pl.BlockSpec(memory_space=pl.ANY)] * n,
        scratch_shapes=[pltpu.SemaphoreType.DMA((n * nk,)), pltpu.SemaphoreType.DMA((n * nk,))],
        compiler_params=_params(),
    )(*presums)


def _cast_bf16(w, name):
    def body(w_ref, o_ref):
        o_ref[...] = w_ref[...].astype(BF16)

    return pl.pallas_call(body, name=name, out_shape=SDS(w.shape, BF16), compiler_params=_params())(w)


def _cols_from_slots(wg, name):
    _, rows, cols = wg.shape

    def body(w_ref, o_ref):
        for j in range(N_DEV):
            o_ref[:, j * cols:(j + 1) * cols] = w_ref[j]

    return pl.pallas_call(body, name=name, out_shape=SDS((rows, N_DEV * cols), wg.dtype), compiler_params=_params())(wg)


def _ada_fwd(c_all, w_ada):
    def body(c_ref, w_ref, o_ref):
        cv = c_ref[...]
        sc = (cv * _sigmoid(cv)).astype(BF16)
        o_ref[...] = _dot(sc, w_ref[...].astype(BF16))

    return pl.pallas_call(body, name="ada_fwd", out_shape=SDS((N_DEV, w_ada.shape[1]), F32),
                          compiler_params=_params())(c_all, w_ada)


def _ada_bwd(c_all, d_ada_cols):
    def body(c_ref, d_ref, o_ref):
        cv = c_ref[...]
        sc = (cv * _sigmoid(cv)).astype(BF16)
        o_ref[...] = _dot_tn(sc, d_ref[...].astype(BF16))

    return pl.pallas_call(body, name="ada_bwd", out_shape=SDS((D_MODEL, d_ada_cols.shape[1]), F32),
                          compiler_params=_params())(c_all, d_ada_cols)


def _norm_fwd(x, norm_w, scale, shift):
    s = x.shape[0]
    tr = 1024

    def body(x_ref, nw_ref, sc_ref, sh_ref, h_ref, ht_ref):
        xv = x_ref[...]
        r = lax.rsqrt(jnp.mean(xv * xv, axis=-1, keepdims=True) + EPS)
        h = (xv * r * nw_ref[...]) * (1.0 + sc_ref[...]) + sh_ref[...]
        h_ref[...] = h.astype(BF16)
        ht_ref[...] = h.T.astype(BF16)

    vec = pl.BlockSpec((1, D_MODEL), lambda i: (0, 0))
    return pl.pallas_call(
        body, name="norm_fwd", grid=(s // tr,),
        in_specs=[pl.BlockSpec((tr, D_MODEL), lambda i: (i, 0)), vec, vec, vec],
        out_specs=[pl.BlockSpec((tr, D_MODEL), lambda i: (i, 0)), pl.BlockSpec((D_MODEL, tr), lambda i: (0, i))],
        out_shape=[SDS((s, D_MODEL), BF16), SDS((D_MODEL, s), BF16)], compiler_params=_params(),
    )(x, norm_w, scale, shift)


def _mm_in(h, wt):
    s = h.shape[0]
    tm = 512

    def body(h_ref, w_ref, o_ref):
        o_ref[...] = _dot_nt(h_ref[...], w_ref[...])

    return pl.pallas_call(
        body, name="mm_in", grid=(IN_W // PAIR_W, s // tm),
        in_specs=[pl.BlockSpec((tm, D_MODEL), lambda p, m: (m, 0)),
                  pl.BlockSpec((PAIR_W, D_MODEL), lambda p, m: (p, 0))],
        out_specs=pl.BlockSpec((tm, PAIR_W), lambda p, m: (m, p)),
        out_shape=SDS((s, IN_W), F32), compiler_params=_params(),
    )(h, wt)


def _head_ones():
    a = lax.broadcasted_iota(jnp.int32, (LANE, LANE), 0) // HEAD_DIM
    b = lax.broadcasted_iota(jnp.int32, (LANE, LANE), 1) // HEAD_DIM
    return (a == b).astype(BF16)


def _head_sums(t, ones):
    return _dot(t.astype(BF16), ones)


def _band_bias(bias, transposed=False):
    qi = lax.broadcasted_iota(jnp.int32, (2 * BAND, 2 * BAND), 1 if transposed else 0) % BAND
    kj = lax.broadcasted_iota(jnp.int32, (2 * BAND, 2 * BAND), 0 if transposed else 1)
    dist = qi + BAND - kj
    valid = (dist >= 0) & (dist <= BAND)
    bias[1] = jnp.where(valid, 0.0, -1e30)
    bias[0] = jnp.where(valid & (kj >= BAND), 0.0, -1e30)


def _token_rows(j, d, chunk, per_r):
    return pl.ds(j // per_r + (j % per_r) * (chunk * d), chunk, stride=d)


def _deinterleave_many(jobs, ones, d, sub_len, chunk, unroll):
    per_r = sub_len // chunk

    def step(j, _):
        tok = _token_rows(j, d, chunk, per_r)
        for src_ref, dst_ref, w_ref, scale, dst_off in jobs:
            t = src_ref[tok, :]
            if w_ref is not None:
                ms = _head_sums(t * t, ones) * (1.0 / HEAD_DIM)
                t = t * lax.rsqrt(ms + EPS) * (w_ref[...] * scale)
            dst_ref[pl.ds(pl.multiple_of(dst_off + j * chunk, BAND), chunk), :] = t.astype(dst_ref.dtype)
        return 0
    lax.fori_loop(0, d * per_r, step, 0, unroll=unroll)


def _deinterleave(src_ref, dst_ref, w_ref, ones, d, sub_len, chunk, scale, dst_off):
    _deinterleave_many([(src_ref, dst_ref, w_ref, scale, dst_off)], ones, d, sub_len, chunk, 4)


N_PAIRS = ATTN_W // LANE


def _attn_fwd(proj, qw2, kw2):
    s = proj.shape[0]

    def group_body(g, step, q_ref, k_ref, v_ref, qw_ref, kw_ref, o_ref, l_ref, qn_ref, kn_ref, vn_ref,
                   qd, kd, vd, od, ld, bias):
        d = DILATIONS[g]
        sub_len = s // d
        nb = sub_len // BAND
        chunk = min(sub_len, 256)
        lo = lax.broadcasted_iota(jnp.int32, (1, LANE), 1) < HEAD_DIM
        ones = _head_ones()

        @pl.when(step == 0)
        def _():
            _band_bias(bias)

        kd[0:BAND, :] = jnp.zeros((BAND, LANE), BF16)
        vd[0:BAND, :] = jnp.zeros((BAND, LANE), BF16)
        _deinterleave_many([(q_ref, qd, qw_ref, HEAD_DIM ** -0.5, 0), (k_ref, kd, kw_ref, 1.0, BAND),
                            (v_ref, vd, None, 1.0, BAND)], ones, d, sub_len, chunk, 4)
        qn_ref[...] = qd[...]
        kn_ref[...] = kd[BAND:BAND + s, :]
        vn_ref[...] = vd[BAND:BAND + s, :]

        def block(t, _):
            base = pl.multiple_of(t * BAND, BAND)
            q = qd[pl.ds(base, BAND), :]
            k2 = kd[pl.ds(base, 2 * BAND), :]
            v2 = vd[pl.ds(base, 2 * BAND), :]
            zero = jnp.zeros_like(q)
            qs = jnp.concatenate([jnp.where(lo, q, zero), jnp.where(lo, zero, q)], axis=0)
            sc = _dot_nt(qs, k2) + bias[jnp.minimum(t % nb, 1)]
            m = jnp.max(sc, axis=-1, keepdims=True)
            p = jnp.exp(sc - m)
            den = jnp.sum(p, axis=-1, keepdims=True)
            u = _dot(p.astype(BF16), v2) * (1.0 / den)
            lse = m + jnp.log(den)
            od[pl.ds(base, BAND), :] = jnp.where(lo, u[:BAND], u[BAND:])
            ld[pl.ds(base, BAND), :] = jnp.where(lo, lse[:BAND], lse[BAND:])
            return 0
        lax.fori_loop(0, s // BAND, block, 0, unroll=16)

        per_r = sub_len // chunk

        def back(j, _):
            src = pl.ds(pl.multiple_of(j * chunk, chunk), chunk)
            dst = _token_rows(j, d, chunk, per_r)
            o_ref[dst, :] = od[src, :]
            l_ref[dst, :] = ld[src, :]
            return 0
        lax.fori_loop(0, d * per_r, back, 0, unroll=2)

    def body(*refs):
        step = pl.program_id(0)
        for g in range(N_GROUPS):
            pl.when(step // N_PAIRS == g)(functools.partial(group_body, g, step, *refs))

    col = lambda off: pl.BlockSpec((s, LANE), lambda i, off=off: (0, off // LANE + i))
    vec = pl.BlockSpec((1, LANE), lambda i: (0, 0))
    out = pl.BlockSpec((s, LANE), lambda i: (0, i))
    width = N_GROUPS * ATTN_W
    return pl.pallas_call(
        body, name="attn_fwd", grid=(N_GROUPS * N_PAIRS,),
        in_specs=[col(Q0), col(K0), col(V0), vec, vec], out_specs=[out] * 5,
        out_shape=[SDS((s, width), F32)] * 2 + [SDS((s, width), BF16)] * 3,
        scratch_shapes=[pltpu.VMEM((s, LANE), BF16), pltpu.VMEM((s + BAND, LANE), BF16), pltpu.VMEM((s + BAND, LANE), BF16),
                        pltpu.VMEM((s, LANE), F32), pltpu.VMEM((s, LANE), F32),
                        pltpu.VMEM((2, 2 * BAND, 2 * BAND), F32)],
        compiler_params=_params(),
    )(proj, proj, proj, qw2, kw2)


def _attn_bwd(proj, qn, kn, vn, da, lse_delta, qw2, kw2, dproj):
    s = proj.shape[0]
    n_steps = N_GROUPS * N_PAIRS

    def group_body(g, hp, q_ref, k_ref, qn_ref, kn_ref, vn_ref, da_ref, ld_ref, qw_ref, kw_ref, dp_in, dp_out,
                   dqw_ref, dkw_ref, kd, vd, kdt, dad, lst, dlt, dqt, dqd, dkd, dvd, st, st_k, stb, bias_t, wacc, sem):
        del dp_in
        d = DILATIONS[g]
        sub_len = s // d
        nb = sub_len // BAND
        chunk = min(sub_len, 256)
        lo = lax.broadcasted_iota(jnp.int32, (1, LANE), 1) < HEAD_DIM
        row_lo = lax.broadcasted_iota(jnp.int32, (LANE, 1), 0) < HEAD_DIM
        ones = _head_ones()
        per_r = sub_len // chunk
        cblk = chunk // BAND

        @pl.when(hp == 0)
        def _():
            _band_bias(bias_t, transposed=True)

        kd[0:BAND, :] = jnp.zeros((BAND, LANE), BF16)
        vd[0:BAND, :] = jnp.zeros((BAND, LANE), BF16)
        kdt[0] = jnp.zeros((LANE, BAND), BF16)
        kd[BAND:BAND + s, :] = kn_ref[...]
        vd[BAND:BAND + s, :] = vn_ref[...]

        def k_step(t, _):
            kdt[1 + t] = kn_ref[pl.ds(pl.multiple_of(t * BAND, BAND), BAND), :].astype(F32).T.astype(BF16)
            return 0
        lax.fori_loop(0, s // BAND, k_step, 0, unroll=4)
        _deinterleave(da_ref, dad, None, ones, d, sub_len, chunk, 1.0, 0)

        def rows_step(j, _):
            tok = _token_rows(j, d, chunk, per_r)
            tt = ld_ref[tok, :].T
            for u in range(cblk):
                cols = slice(u * BAND, (u + 1) * BAND)
                lst[j * cblk + u, 0:1, :] = tt[0:1, cols]
                lst[j * cblk + u, 1:2, :] = tt[HEAD_DIM:HEAD_DIM + 1, cols]
                dlt[j * cblk + u, 0:1, :] = tt[HEAD_DIM // 2:HEAD_DIM // 2 + 1, cols]
                dlt[j * cblk + u, 1:2, :] = tt[HEAD_DIM + HEAD_DIM // 2:HEAD_DIM + HEAD_DIM // 2 + 1, cols]
            return 0
        lax.fori_loop(0, d * per_r, rows_step, 0, unroll=4)

        def block(t, carry):
            ck, cv = carry
            base = pl.multiple_of(t * BAND, BAND)
            q = qn_ref[pl.ds(base, BAND), :]
            k2 = kd[pl.ds(base, 2 * BAND), :]
            v2 = vd[pl.ds(base, 2 * BAND), :]
            k2t = jnp.concatenate([kdt[t], kdt[t + 1]], axis=1)
            dav = dad[pl.ds(base, BAND), :]
            zero = jnp.zeros_like(q)
            qs = jnp.concatenate([jnp.where(lo, q, zero), jnp.where(lo, zero, q)], axis=0)
            das = jnp.concatenate([jnp.where(lo, dav, zero), jnp.where(lo, zero, dav)], axis=0)
            ls_row = jnp.concatenate([lst[t, 0:1, :], lst[t, 1:2, :]], axis=1)
            dl_row = jnp.concatenate([dlt[t, 0:1, :], dlt[t, 1:2, :]], axis=1)
            sc_t = _dot_nt(k2, qs) + bias_t[jnp.minimum(t % nb, 1)]
            p_t = jnp.exp(sc_t - ls_row)
            dp_t = _dot_nt(v2, das)
            ds_t = (p_t * (dp_t - dl_row)).astype(BF16)
            dv2 = _dot(p_t.astype(BF16), das)
            dk2 = _dot(ds_t, qs)
            dvd[pl.ds(base, BAND), :] = cv + dv2[:BAND]
            dkd[pl.ds(base, BAND), :] = ck + dk2[:BAND]
            dq_t = _dot(k2t, ds_t)
            dqt[t] = jnp.where(row_lo, dq_t[:, :BAND], dq_t[:, BAND:])
            return dk2[BAND:], dv2[BAND:]

        def blocks(i, carry):
            for u in range(BWD_UNROLL):
                carry = block(i * BWD_UNROLL + u, carry)
            return carry
        zeros = jnp.zeros((BAND, LANE), F32)
        ck, cv = lax.fori_loop(0, s // (BAND * BWD_UNROLL), blocks, (zeros, zeros))
        dkd[s:s + BAND, :] = ck
        dvd[s:s + BAND, :] = cv

        def dq_rows(t, _):
            dqd[pl.ds(pl.multiple_of(t * BAND, BAND), BAND), :] = dqt[t].T
            return 0
        lax.fori_loop(0, s // BAND, dq_rows, 0, unroll=4)

        def col_copy(slot, col0):
            return pltpu.make_async_copy(
                stb.at[slot], dp_out.at[:, pl.ds(pl.multiple_of(col0 + LANE * hp, LANE), LANE)], sem.at[slot])

        def store_cols(slot, col0, src):
            @pl.when(hp > 0)
            def _():
                col_copy(slot, col0).wait()
            stb[slot] = src[...].astype(BF16)
            col_copy(slot, col0).start()

        sides = ((q_ref, dqd, 0, qw_ref, HEAD_DIM ** -0.5, st), (k_ref, dkd, BAND, kw_ref, 1.0, st_k))
        wacc[...] = jnp.zeros_like(wacc)

        def norm_step(j, _):
            tok = _token_rows(j, d, chunk, per_r)
            for i, (src_ref, dy_ref, dy_off, w_ref, scale, dst) in enumerate(sides):
                t = src_ref[tok, :]
                dy = dy_ref[pl.ds(pl.multiple_of(dy_off + j * chunk, BAND), chunk), :]
                rr = lax.rsqrt(_head_sums(t * t, ones) * (1.0 / HEAD_DIM) + EPS)
                nrm = t * rr
                wacc[i] += jnp.sum((dy * nrm).reshape(chunk // 8, 8, LANE), axis=0)
                dn = dy * (w_ref[...] * scale)
                dst[tok, :] = rr * (dn - nrm * (_head_sums(dn * nrm, ones) * (1.0 / HEAD_DIM)))
            return 0
        lax.fori_loop(0, d * per_r, norm_step, 0, unroll=4)

        @pl.when(hp == 0)
        def _():
            dqw_ref[...] = jnp.zeros_like(dqw_ref)
            dkw_ref[...] = jnp.zeros_like(dkw_ref)

        for i, dw_ref in enumerate((dqw_ref, dkw_ref)):
            dw_ref[...] += jnp.broadcast_to(jnp.sum(wacc[i], axis=0, keepdims=True) * sides[i][4], dw_ref.shape)
        store_cols(0, Q0, st)
        store_cols(1, K0, st_k)

        def v_back(j, _):
            src = pl.ds(pl.multiple_of(BAND + j * chunk, BAND), chunk)
            st[_token_rows(j, d, chunk, per_r), :] = dvd[src, :]
            return 0
        lax.fori_loop(0, d * per_r, v_back, 0, unroll=2)
        store_cols(2, V0, st)

        @pl.when(hp == n_steps - 1)
        def _():
            for slot, col0 in enumerate((Q0, K0, V0)):
                col_copy(slot, col0).wait()

    def body(*refs):
        step = pl.program_id(0)
        for g in range(N_GROUPS):
            pl.when(step // N_PAIRS == g)(functools.partial(group_body, g, step, *refs))

    col = lambda off: pl.BlockSpec((s, LANE), lambda i, off=off: (0, off // LANE + i))
    mid = pl.BlockSpec((s, LANE), lambda i: (0, i))
    slot4 = pl.BlockSpec((s, LANE), lambda i: (0, i % N_PAIRS))
    vec = pl.BlockSpec((1, LANE), lambda i: (0, 0))
    acc = pl.BlockSpec((8, LANE), lambda i: (0, 0))
    any_ = pl.BlockSpec(memory_space=pl.ANY)
    return pl.pallas_call(
        body, name="attn_bwd", grid=(n_steps,),
        in_specs=[col(Q0), col(K0), mid, mid, mid, slot4, slot4, vec, vec, any_],
        out_specs=[any_, acc, acc],
        out_shape=[SDS(dproj.shape, dproj.dtype), SDS((8, LANE), F32), SDS((8, LANE), F32)],
        input_output_aliases={9: 0},
        scratch_shapes=[pltpu.VMEM((s + BAND, LANE), BF16), pltpu.VMEM((s + BAND, LANE), BF16),
                        pltpu.VMEM((s // BAND + 1, LANE, BAND), BF16), pltpu.VMEM((s, LANE), BF16),
                        pltpu.VMEM((s // BAND, 8, BAND), F32), pltpu.VMEM((s // BAND, 8, BAND), F32),
                        pltpu.VMEM((s // BAND, LANE, BAND), F32),
                        pltpu.VMEM((s, LANE), F32), pltpu.VMEM((s + BAND, LANE), F32), pltpu.VMEM((s + BAND, LANE), F32),
                        pltpu.VMEM((s, LANE), F32), pltpu.VMEM((s, LANE), F32), pltpu.VMEM((3, s, LANE), BF16),
                        pltpu.VMEM((2, 2 * BAND, 2 * BAND), F32), pltpu.VMEM((2, 8, LANE), F32),
                        pltpu.SemaphoreType.DMA((3,))],
        compiler_params=_params(),
    )(proj, proj, qn, kn, vn, da, lse_delta, qw2, kw2, dproj)


def _tap_views(ext_ref, sh_ref, offsets, tr, cols):
    for b in range(8):
        group = [j for j, o in enumerate(offsets) if o % 8 == b]
        if not group:
            continue
        first = min(offsets[j] for j in group)
        span = tr + max(offsets[j] for j in group) - first
        sh_ref[0:span, cols] = ext_ref[first:first + span, cols]
        for j in group:
            yield j, sh_ref[offsets[j] - first:offsets[j] - first + tr, cols]


def _silu_grad(z, sg):
    return sg * (1.0 + z * (1.0 - sg))


def _glu(u):
    a_h, b_h = u[:, :CONV_W], u[:, CONV_W:]
    sg = _sigmoid(b_h)
    return a_h, sg, a_h * sg


def _tail(x, tgt, proj, o3, l3, wa, wc, wo, gate, bga, bgc, convw, convb, lnw, lnb, bd):
    s = x.shape[0]
    tr = 256

    def body(x_ref, t_ref, za_ref, u_ref, uh_ref, zc_ref, g0_ref, g1_ref, g2_ref, g3_ref,
             o0_ref, o1_ref, o2_ref, l0_ref, l1_ref, l2_ref, wa_ref, wc_ref, wo_ref,
             gate_ref, bga_ref, bgc_ref, cw_ref, cb_ref, lnw_ref, lnb_ref, bd_ref,
             dout_ref, da_ref, ld_ref, dcv_ref, mt_ref, yat_ref, yct_ref, dmo_ref, dya_ref, dyc_ref, dp_ref,
             dgate_ref, dbg_ref, dlnw_ref, dlnb_ref, dcb_ref, loss_ref,
             ext, sh, st_za, st_zc, st_g, sems):
        i = pl.program_id(0)

        @pl.when(i == 0)
        def _():
            for r in (dgate_ref, dbg_ref, dlnw_ref, dlnb_ref, dcb_ref, loss_ref):
                r[...] = jnp.zeros_like(r)

        def acc_rows(ref, v):
            ref[...] += jnp.broadcast_to(jnp.sum(v, axis=0, keepdims=True), ref.shape)

        la, lb, lc = l0_ref[...], l1_ref[...], l2_ref[...]
        mx = jnp.maximum(jnp.maximum(la, lb), lc)
        ea, eb, ec = jnp.exp(la - mx), jnp.exp(lb - mx), jnp.exp(lc - mx)
        den = ea + eb + ec
        inv = 1.0 / den
        attn = (ea * inv) * o0_ref[...] + (eb * inv) * o1_ref[...] + (ec * inv) * o2_ref[...]
        lse = mx + jnp.log(den)

        za = za_ref[...]
        sga = _sigmoid(za)
        sa = za * sga
        ya_in = attn * sa
        y_attn = _dot(ya_in.astype(BF16), wa_ref[...])

        _, _, glu = _glu(u_ref[...])
        _, _, glu_h = _glu(uh_ref[...])
        ext[0:CONV_HALO, :] = jnp.where(i > 0, glu_h, 0.0)
        ext[CONV_HALO:CONV_HALO + tr, :] = glu
        cv_blocks = []
        for cb in range(CONV_W // LANE):
            cols = slice(cb * LANE, (cb + 1) * LANE)
            cv_c = jnp.broadcast_to(cb_ref[:, cols], (tr, LANE))
            for j, rows in _tap_views(ext, sh, [CONV_HALO - (CONV_K - 1) + j for j in range(CONV_K)], tr, cols):
                cv_c = cv_c + cw_ref[j:j + 1, cols] * rows
            cv_blocks.append(cv_c)
        cv = jnp.concatenate(cv_blocks, axis=1)
        mu = jnp.mean(cv, axis=-1, keepdims=True)
        xc = cv - mu
        rstd = lax.rsqrt(jnp.mean(xc * xc, axis=-1, keepdims=True) + EPS)
        nrm = xc * rstd
        ln = nrm * lnw_ref[...] + lnb_ref[...]
        sgl = _sigmoid(ln)
        cs = ln * sgl
        zc = zc_ref[...]
        sgc = _sigmoid(zc)
        scz = zc * sgc
        yc_in = cs * scz
        y_conv = _dot(yc_in.astype(BF16), wc_ref[...])

        ga = _sigmoid(jnp.concatenate([g0_ref[...], g1_ref[...]], axis=1) + bga_ref[...])
        gc = _sigmoid(jnp.concatenate([g2_ref[...], g3_ref[...]], axis=1) + bgc_ref[...])
        merged = ga * y_attn + gc * y_conv
        mo = _dot(merged.astype(BF16), wo_ref[...])
        gate_v = gate_ref[...]
        err = (x_ref[...] + gate_v * mo) - t_ref[...]
        loss_ref[...] += 0.5 * jnp.sum(jnp.mean(err * err, axis=-1, keepdims=True))
        d_out = err * (1.0 / D_MODEL)
        dout_ref[...] = d_out

        rows = pl.ds(pl.multiple_of(i * tr, tr), tr)
        cps = [pltpu.make_async_copy(st_za, dp_ref.at[rows, pl.ds(ZA0, ATTN_W)], sems.at[0]),
               pltpu.make_async_copy(st_zc, dp_ref.at[rows, pl.ds(ZC0, CONV_W)], sems.at[1]),
               pltpu.make_async_copy(st_g, dp_ref.at[rows, pl.ds(G0, 2 * D_MODEL)], sems.at[2])]

        @pl.when(i > 0)
        def _():
            for cp in cps:
                cp.wait()

        acc_rows(dgate_ref, d_out * mo)
        dmo_b = (d_out * gate_v).astype(BF16)
        dmo_ref[...] = dmo_b
        mt_ref[...] = merged.T.astype(BF16)
        d_merged = _dot_nt(dmo_b, wo_ref[...])
        d_ya = (d_merged * ga).astype(BF16)
        d_yc = (d_merged * gc).astype(BF16)
        dya_ref[...] = d_ya
        dyc_ref[...] = d_yc
        dga = d_merged * y_attn * (ga * (1.0 - ga))
        dgc = d_merged * y_conv * (gc * (1.0 - gc))
        dgs = jnp.concatenate([dga, dgc], axis=1)
        acc_rows(dbg_ref, dgs)
        st_g[...] = dgs.astype(BF16)

        yat_ref[...] = ya_in.T.astype(BF16)
        d_ya_in = _dot_nt(d_ya, wa_ref[...])
        d_attn = d_ya_in * sa
        da_ref[...] = d_attn
        st_za[...] = (d_ya_in * attn * _silu_grad(za, sga)).astype(BF16)
        prod = d_attn * attn
        hi = prod.astype(BF16)
        lo_ = (prod - hi.astype(F32)).astype(BF16)
        delta = _dot(hi, bd_ref[...]) + _dot(lo_, bd_ref[...])
        first_half = (lax.broadcasted_iota(jnp.int32, (1, ATTN_W), 1) % HEAD_DIM) < HEAD_DIM // 2
        ld_ref[...] = jnp.where(first_half, lse, delta)

        yct_ref[...] = yc_in.T.astype(BF16)
        d_yc_in = _dot_nt(d_yc, wc_ref[...])
        st_zc[...] = (d_yc_in * cs * _silu_grad(zc, sgc)).astype(BF16)
        d_ln = (d_yc_in * scz) * _silu_grad(ln, sgl)
        acc_rows(dlnw_ref, d_ln * nrm)
        acc_rows(dlnb_ref, d_ln)
        d_nrm = d_ln * lnw_ref[...]
        d_cv = rstd * (d_nrm - jnp.mean(d_nrm, axis=-1, keepdims=True)
                       - nrm * jnp.mean(d_nrm * nrm, axis=-1, keepdims=True))
        acc_rows(dcb_ref, d_cv)
        dcv_ref[...] = d_cv

        for cp in cps:
            cp.start()

        @pl.when(i == s // tr - 1)
        def _():
            for cp in cps:
                cp.wait()

    def rows(width, colblk=0):
        return pl.BlockSpec((tr, width), lambda i, colblk=colblk: (i, colblk))

    def const(shape):
        return pl.BlockSpec(shape, lambda i: (0,) * len(shape))

    halo = pl.BlockSpec((CONV_HALO, D_MODEL), lambda i: (jnp.maximum(i * (tr // CONV_HALO) - 1, 0), U0 // D_MODEL))
    in_specs = [rows(D_MODEL), rows(D_MODEL), rows(ATTN_W, ZA0 // ATTN_W), rows(D_MODEL, U0 // D_MODEL), halo,
                rows(CONV_W, ZC0 // CONV_W)]
    in_specs += [rows(512, G0 // 512 + j) for j in range(4)]
    in_specs += [rows(ATTN_W, g) for g in range(N_GROUPS)] * 2
    in_specs += [const(wa.shape), const(wc.shape), const(wo.shape), const((1, D_MODEL)), const((1, D_MODEL)),
                 const((1, D_MODEL)), const(convw.shape), const((1, CONV_W)), const((1, CONV_W)), const((1, CONV_W)),
                 const(bd.shape)]
    tcol = lambda width: pl.BlockSpec((width, tr), lambda i: (0, i))
    out_specs = [rows(D_MODEL), rows(ATTN_W), rows(ATTN_W), rows(CONV_W),
                 tcol(D_MODEL), tcol(ATTN_W), tcol(CONV_W), rows(D_MODEL), rows(D_MODEL), rows(D_MODEL),
                 pl.BlockSpec(memory_space=pl.ANY),
                 const((8, D_MODEL)), const((8, 2 * D_MODEL)), const((8, CONV_W)), const((8, CONV_W)), const((8, CONV_W)),
                 const((8, LANE))]
    out_shape = [SDS((s, D_MODEL), F32), SDS((s, ATTN_W), F32), SDS((s, ATTN_W), F32),
                 SDS((s, CONV_W), F32),
                 SDS((D_MODEL, s), BF16), SDS((ATTN_W, s), BF16), SDS((CONV_W, s), BF16),
                 SDS((s, D_MODEL), BF16), SDS((s, D_MODEL), BF16), SDS((s, D_MODEL), BF16),
                 SDS((s, IN_W), BF16),
                 SDS((8, D_MODEL), F32), SDS((8, 2 * D_MODEL), F32), SDS((8, CONV_W), F32), SDS((8, CONV_W), F32),
                 SDS((8, CONV_W), F32), SDS((8, LANE), F32)]
    return pl.pallas_call(
        body, name="tail", grid=(s // tr,), in_specs=in_specs, out_specs=out_specs, out_shape=out_shape,
        scratch_shapes=[pltpu.VMEM((CONV_HALO + tr, CONV_W), F32), pltpu.VMEM((CONV_HALO + tr, CONV_W), F32),
                        pltpu.VMEM((tr, ATTN_W), BF16),
                        pltpu.VMEM((tr, CONV_W), BF16), pltpu.VMEM((tr, 2 * D_MODEL), BF16),
                        pltpu.SemaphoreType.DMA((3,))],
        compiler_params=_params(),
    )(x, tgt, proj, proj, proj, proj, proj, proj, proj, proj, *o3, *l3, wa, wc, wo, gate, bga, bgc,
      convw, convb, lnw, lnb, bd)


def _conv_bwd(dcv, proj, convw, dproj):
    s = dcv.shape[0]
    tr = 128
    nt = s // tr

    def body(dcv_ref, dcvn_ref, u_ref, uh_ref, cw_ref, dp_in, dp_out, dw_ref, extg, extd, sh):
        del dp_in
        i = pl.program_id(0)

        @pl.when(i == 0)
        def _():
            dw_ref[...] = jnp.zeros_like(dw_ref)

        _, _, glu = _glu(u_ref[...])
        _, _, glu_h = _glu(uh_ref[...])
        extg[0:CONV_HALO, :] = jnp.where(i > 0, glu_h, 0.0)
        extg[CONV_HALO:CONV_HALO + tr, :] = glu
        extd[0:tr, :] = dcv_ref[...]
        extd[tr:tr + CONV_HALO, :] = jnp.where(i < nt - 1, dcvn_ref[...], 0.0)
        for cb in range(CONV_W // LANE):
            cols = slice(cb * LANE, (cb + 1) * LANE)
            dglu = jnp.zeros((tr, LANE), F32)
            for j, rows in _tap_views(extd, sh, [CONV_K - 1 - j for j in range(CONV_K)], tr, cols):
                dglu = dglu + cw_ref[j:j + 1, cols] * rows
            dcv_c = dcv_ref[:, cols]
            for j, rows in _tap_views(extg, sh, [CONV_HALO - (CONV_K - 1) + j for j in range(CONV_K)], tr, cols):
                dw_ref[8 * j:8 * j + 8, cols] += jnp.sum((dcv_c * rows).reshape(tr // 8, 8, LANE), axis=0)
            a_h = u_ref[:, cols]
            sgb = _sigmoid(u_ref[:, CONV_W + cb * LANE:CONV_W + (cb + 1) * LANE])
            dp_out[:, cols] = (dglu * sgb).astype(BF16)
            dp_out[:, CONV_W + cb * LANE:CONV_W + (cb + 1) * LANE] = (dglu * a_h * (sgb * (1.0 - sgb))).astype(BF16)

    ucol = U0 // D_MODEL
    return pl.pallas_call(
        body, name="conv_bwd", grid=(nt,),
        in_specs=[pl.BlockSpec((tr, CONV_W), lambda i: (i, 0)),
                  pl.BlockSpec((CONV_HALO, CONV_W), lambda i: (jnp.minimum((i + 1) * (tr // CONV_HALO), s // CONV_HALO - 1), 0)),
                  pl.BlockSpec((tr, D_MODEL), lambda i: (i, ucol)),
                  pl.BlockSpec((CONV_HALO, D_MODEL), lambda i: (jnp.maximum(i * (tr // CONV_HALO) - 1, 0), ucol)),
                  pl.BlockSpec(convw.shape, lambda i: (0, 0)),
                  pl.BlockSpec(memory_space=pl.ANY)],
        out_specs=[pl.BlockSpec((tr, D_MODEL), lambda i: (i, ucol)), pl.BlockSpec((8 * CONV_HALO, CONV_W), lambda i: (0, 0))],
        out_shape=[SDS(dproj.shape, dproj.dtype), SDS((8 * CONV_HALO, CONV_W), F32)],
        input_output_aliases={5: 0},
        scratch_shapes=[pltpu.VMEM((CONV_HALO + tr, CONV_W), F32)] * 3,
        compiler_params=_params(),
    )(dcv, dcv, proj, proj, convw, dproj)


def _mm_acc(at, b, name, col_slots):
    m, s = at.shape
    n = b.shape[1]
    tk = 1024
    nk = s // tk

    def body(a_ref, b_ref, o_ref, acc):
        k = pl.program_id(0)

        @pl.when(k == 0)
        def _():
            acc[...] = jnp.zeros_like(acc)

        acc[...] += _dot(a_ref[...], b_ref[...])

        @pl.when(k == nk - 1)
        def _():
            if col_slots:
                w = n // N_DEV
                for j in range(N_DEV):
                    o_ref[j] = acc[:, j * w:(j + 1) * w].astype(BF16)
            else:
                o_ref[...] = acc[...].astype(BF16)

    if col_slots:
        out_shape = SDS((N_DEV, m, n // N_DEV), BF16)
        out_spec = pl.BlockSpec((N_DEV, m, n // N_DEV), lambda k: (0, 0, 0))
    else:
        out_shape = SDS((m, n), BF16)
        out_spec = pl.BlockSpec((m, n), lambda k: (0, 0))
    return pl.pallas_call(
        body, name=name, grid=(nk,),
        in_specs=[pl.BlockSpec((m, tk), lambda k: (0, k)), pl.BlockSpec((tk, n), lambda k: (k, 0))],
        out_specs=out_spec, out_shape=out_shape, scratch_shapes=[pltpu.VMEM((m, n), F32)],
        compiler_params=_params(),
    )(at, b)


def _mm_dw(ht, dproj):
    s = ht.shape[1]
    tk = 1024
    nk = s // tk

    def body(a_ref, b_ref, o_ref, acc):
        k = pl.program_id(1)

        @pl.when(k == 0)
        def _():
            acc[...] = jnp.zeros_like(acc)

        acc[...] += _dot(a_ref[...], b_ref[...])

        @pl.when(k == nk - 1)
        def _():
            o_ref[...] = acc[...].T.astype(BF16)

    return pl.pallas_call(
        body, name="mm_dw", grid=(IN_W // PAIR_W, nk),
        in_specs=[pl.BlockSpec((D_MODEL, tk), lambda p, k: (0, k)), pl.BlockSpec((tk, PAIR_W), lambda p, k: (k, p))],
        out_specs=pl.BlockSpec((PAIR_W, D_MODEL), lambda p, k: (p, 0)),
        out_shape=SDS((IN_W, D_MODEL), BF16), scratch_shapes=[pltpu.VMEM((D_MODEL, PAIR_W), F32)],
        compiler_params=_params(),
    )(ht, dproj)


def _mm_dh_norm_bwd(dproj, wt, x, dout, norm_w, scale, token):
    s = dproj.shape[0]
    tm = 1024
    n_p = IN_W // PAIR_W

    def body(dp_ref, w_ref, x_ref, do_ref, nw_ref, sc_ref, tok_ref, gx_ref, dsh_ref, dsc_ref, dnw_ref, dh_acc):
        del tok_ref
        m, p = pl.program_id(0), pl.program_id(1)
        part = _dot(dp_ref[...], w_ref[...])

        @pl.when(p == 0)
        def _():
            dh_acc[...] = part

        @pl.when(p > 0)
        def _():
            dh_acc[...] += part

        @pl.when((m == 0) & (p == 0))
        def _():
            for r in (dsh_ref, dsc_ref, dnw_ref):
                r[...] = jnp.zeros_like(r)

        @pl.when(p == n_p - 1)
        def _():
            def acc_rows(ref, v):
                ref[...] += jnp.broadcast_to(jnp.sum(v, axis=0, keepdims=True), ref.shape)

            xv = x_ref[...]
            dh_v = dh_acc[...]
            r = lax.rsqrt(jnp.mean(xv * xv, axis=-1, keepdims=True) + EPS)
            xn = xv * r
            one_sc = 1.0 + sc_ref[...]
            acc_rows(dsh_ref, dh_v)
            acc_rows(dsc_ref, dh_v * (xn * nw_ref[...]))
            acc_rows(dnw_ref, dh_v * xn * one_sc)
            dxn = dh_v * (nw_ref[...] * one_sc)
            gx_ref[...] = do_ref[...] + r * (dxn - xn * jnp.mean(dxn * xn, axis=-1, keepdims=True))

    rows = pl.BlockSpec((tm, D_MODEL), lambda m, p: (m, 0))
    vec = pl.BlockSpec((1, D_MODEL), lambda m, p: (0, 0))
    acc = pl.BlockSpec((8, D_MODEL), lambda m, p: (0, 0))
    return pl.pallas_call(
        body, name="mm_dh_norm_bwd", grid=(s // tm, n_p),
        in_specs=[pl.BlockSpec((tm, PAIR_W), lambda m, p: (m, p)),
                  pl.BlockSpec((PAIR_W, D_MODEL), lambda m, p: (p, 0)),
                  rows, rows, vec, vec, pl.BlockSpec(token.shape, lambda m, p: (0, 0))],
        out_specs=[rows, acc, acc, acc],
        out_shape=[SDS((s, D_MODEL), F32)] + [SDS((8, D_MODEL), F32)] * 3,
        scratch_shapes=[pltpu.VMEM((tm, D_MODEL), F32)], compiler_params=_params(),
    )(dproj, wt, x, dout, norm_w, scale, token)


SMALL_ROWS = 8
QN_COL, KN_COL, CB_COL, LOSS_COL = 0, LANE, 2 * LANE, 2 * LANE + CONV_W


def _pack_partials(dsh, dsc, dgate, dnw, dbg, dqw3, dkw3, dcb, dlnw, dlnb, loss_p):
    n3 = len(dqw3)

    def body(*refs):
        dsh_r, dsc_r, dgate_r, dnw_r, dbg_r = refs[:5]
        dq_r, dk_r = refs[5:5 + n3], refs[5 + n3:5 + 2 * n3]
        dcb_r, dlnw_r, dlnb_r, loss_r, o_ref = refs[5 + 2 * n3:]

        def both_heads(rs):
            t = rs[0][0:1, :]
            for r in rs[1:]:
                t = t + r[0:1, :]
            return t + pltpu.roll(t, HEAD_DIM, axis=1)

        o_ref[0:1, :] = dsh_r[0:1, :]
        o_ref[1:2, :] = dsc_r[0:1, :]
        o_ref[2:3, :] = dgate_r[0:1, :]
        o_ref[3:4, :] = dnw_r[0:1, :]
        o_ref[4:5, :] = dbg_r[0:1, 0:D_MODEL]
        o_ref[5:6, :] = dbg_r[0:1, D_MODEL:]
        o_ref[6:7, QN_COL:QN_COL + LANE] = both_heads(dq_r)
        o_ref[6:7, KN_COL:KN_COL + LANE] = both_heads(dk_r)
        o_ref[6:7, CB_COL:CB_COL + CONV_W] = dcb_r[0:1, :]
        o_ref[6:7, LOSS_COL:LOSS_COL + LANE] = loss_r[0:1, :]
        o_ref[6:7, LOSS_COL + LANE:] = jnp.zeros((1, D_MODEL - LOSS_COL - LANE), F32)
        o_ref[7:8, 0:CONV_W] = dlnw_r[0:1, :]
        o_ref[7:8, CONV_W:] = dlnb_r[0:1, :]

    return pl.pallas_call(body, name="pack_partials", out_shape=SDS((SMALL_ROWS, D_MODEL), F32),
                          compiler_params=_params())(dsh, dsc, dgate, dnw, dbg, *dqw3, *dkw3, dcb, dlnw, dlnb, loss_p)


def _adamw_update(g, w, m, v):
    bc1 = 1.0 - ADAM_B1 ** ADAM_STEP
    bc2 = 1.0 - ADAM_B2 ** ADAM_STEP
    m_new = ADAM_B1 * m + (1.0 - ADAM_B1) * g
    v_new = ADAM_B2 * v + (1.0 - ADAM_B2) * (g * g)
    delta = -ADAM_LR * ((m_new / bc1) / (jnp.sqrt(v_new / bc2) + ADAM_EPS) + ADAM_WD * w)
    return delta, m_new, v_new


def _adamw_small(small_all, ws, ms, vs):
    n = len(ws)
    where = [(slice(0, 3), None), (slice(3, 4), None), (slice(4, 6), None), (6, QN_COL), (6, KN_COL), (6, CB_COL),
             (7, 0), (7, CONV_W)]

    def body(*refs):
        g_ref = refs[0]
        w_r, m_r, v_r = refs[1:1 + n], refs[1 + n:1 + 2 * n], refs[1 + 2 * n:1 + 3 * n]
        outs = refs[1 + 3 * n:]
        g_o, d_o, m_o, v_o, loss_o = outs[:n], outs[n:2 * n], outs[2 * n:3 * n], outs[3 * n:4 * n], outs[4 * n]
        gsum = g_ref[0]
        for dev in range(1, N_DEV):
            gsum = gsum + g_ref[dev]
        loss_o[...] = gsum[6:7, LOSS_COL:LOSS_COL + LANE]
        for i, (rows, col) in enumerate(where):
            width = w_r[i].shape[1]
            if col is None:
                g = jnp.concatenate([gsum[r:r + 1, :] for r in range(rows.start, rows.stop)], axis=1)
            else:
                g = gsum[rows:rows + 1, col:col + width]
            delta, m_new, v_new = _adamw_update(g, w_r[i][...], m_r[i][...], v_r[i][...])
            g_o[i][...] = g
            d_o[i][...] = delta
            m_o[i][...] = m_new
            v_o[i][...] = v_new

    shapes = [SDS(w.shape, F32) for w in ws]
    res = pl.pallas_call(body, name="adamw_small", out_shape=shapes * 4 + [SDS((1, LANE), F32)],
                         compiler_params=_params())(small_all, *ws, *ms, *vs)
    return [res[k * n:(k + 1) * n] for k in range(4)], res[4 * n]


def _row_tile(rows):
    if rows <= 128:
        return rows
    if rows % 256 == 0:
        return 256
    return 128 if rows % 128 == 0 else SHARD_W // 4


def _adamw(gsrc, w, m, v, name, stacked):
    rows, cols = w.shape
    tr = _row_tile(rows)
    n_src = len(gsrc) if stacked else 1

    def body(*refs):
        g_refs, (w_ref, m_ref, v_ref, go_ref, d_ref, mo_ref, vo_ref) = refs[:n_src], refs[n_src:]
        if stacked:
            g = None
            for g_ref, (_, slots) in zip(g_refs, gsrc):
                for j in range(slots):
                    t = g_ref[j].astype(F32)
                    g = t if g is None else g + t
        else:
            g = g_refs[0][...]
        delta, m_new, v_new = _adamw_update(g, w_ref[...], m_ref[...], v_ref[...])
        go_ref[...] = g
        d_ref[...] = delta
        mo_ref[...] = m_new
        vo_ref[...] = v_new

    blk = pl.BlockSpec((tr, cols), lambda i: (i, 0))
    if stacked:
        gspecs = [pl.BlockSpec((slots, tr, arr.shape[2]), lambda i: (0, i, 0)) for arr, slots in gsrc]
        gargs = [arr for arr, _ in gsrc]
    else:
        gspecs, gargs = [blk], [gsrc]
    in_specs = gspecs + [blk, blk, blk]
    args = gargs + [w, m, v]
    return pl.pallas_call(
        body, name=name, grid=(rows // tr,), in_specs=in_specs, out_specs=[blk] * 4,
        out_shape=[SDS((rows, cols), F32)] * 4, compiler_params=_params(),
    )(*args)


def kernel(x, c, w_ada, b_ada, norm_w, w_in, b_gate, q_norm_w, k_norm_w, w_attn_proj, conv_w, conv_b, conv_ln_w, conv_ln_b, w_conv_proj, w_out, loss_target, m_w_ada, m_b_ada, m_norm_w, m_w_in, m_b_gate, m_q_norm_w, m_k_norm_w, m_w_attn_proj, m_conv_w, m_conv_b, m_conv_ln_w, m_conv_ln_b, m_w_conv_proj, m_w_out, v_w_ada, v_b_ada, v_norm_w, v_w_in, v_b_gate, v_q_norm_w, v_k_norm_w, v_w_attn_proj, v_conv_w, v_conv_b, v_conv_ln_w, v_conv_ln_b, v_w_conv_proj, v_w_out):
    xi, yi, ci = lax.axis_index("x"), lax.axis_index("y"), lax.axis_index("c")
    me = 4 * xi + 2 * yi + ci
    x2, tgt2 = x[0], loss_target[0]
    w_in_t, m_w_in_t, v_w_in_t = (jnp.transpose(a[0]) for a in (w_in, m_w_in, v_w_in))
    s = x2.shape[0]

    cw_flat = jnp.pad(conv_w[0].reshape(1, -1), ((0, 0), (0, CONVW_FLAT - CONV_K * HEAD_DIM)))
    pre = jnp.concatenate([c, cw_flat], axis=1).reshape(8, -1)
    (pre_all,) = _all_gather([pre], "gather_c_convw", vmem=True)
    pre_all = pre_all.reshape(N_DEV, -1)
    c_all = pre_all[:, :D_MODEL]
    convw_full = pre_all[:, D_MODEL:D_MODEL + CONV_K * HEAD_DIM].reshape(N_DEV, CONV_K, HEAD_DIM)
    convw_full = jnp.transpose(convw_full, (1, 0, 2)).reshape(CONV_K, CONV_W)
    convw_pad = jnp.pad(convw_full, ((0, CONV_HALO - CONV_K), (0, 0)))

    ada_part = _ada_fwd(c_all, w_ada[0])
    (ada_all,) = _all_gather([ada_part], "gather_ada", vmem=True)
    ada = lax.dynamic_index_in_dim(ada_all, me, axis=1, keepdims=False).reshape(1, 3 * D_MODEL) + b_ada
    shift, scale, gate = ada[:, :D_MODEL], ada[:, D_MODEL:2 * D_MODEL], ada[:, 2 * D_MODEL:]

    wt_g, wa_g, wc_g, wo_g = _all_gather_chips(
        [_cast_bf16(w_in_t, "cast_win"), _cast_bf16(w_attn_proj[0], "cast_wa"), _cast_bf16(w_conv_proj[0], "cast_wc"),
         _cast_bf16(w_out[0], "cast_wo")], "gather_weights")
    wt = wt_g.reshape(IN_W, D_MODEL)
    wa = _cols_from_slots(wa_g, "cols_wa")
    wc = _cols_from_slots(wc_g, "cols_wc")
    wo = wo_g.reshape(D_MODEL, D_MODEL)

    h, ht = _norm_fwd(x2, norm_w, scale, shift)
    proj = _mm_in(h, wt)
    qw2 = jnp.tile(q_norm_w, (1, 2))
    kw2 = jnp.tile(k_norm_w, (1, 2))
    o_all, l_all, qn, kn, vn = _attn_fwd(proj, qw2, kw2)
    o3, l3 = [o_all] * N_GROUPS, [l_all] * N_GROUPS
    head_id = jnp.arange(ATTN_W) // HEAD_DIM
    bd = (head_id[:, None] == head_id[None, :]).astype(BF16)
    (dout, da, lse_delta, dcv, mt, yat, yct, dmo, dya, dyc, dproj,
     dgate, dbg, dlnw, dlnb, dcb, loss_p) = _tail(
        x2, tgt2, proj, o3, l3, wa, wc, wo, gate, b_gate[:, :D_MODEL], b_gate[:, D_MODEL:], convw_pad,
        conv_b, conv_ln_w, conv_ln_b, bd)

    dproj, dconvw8 = _conv_bwd(dcv, proj, convw_pad, dproj)
    dconvw = jnp.sum(dconvw8.reshape(CONV_HALO, 8, CONV_W), axis=1)
    dproj, dqw_all, dkw_all = _attn_bwd(proj, qn, kn, vn, da, lse_delta, qw2, kw2, dproj)
    dqw_g3, dkw_g3 = [dqw_all], [dkw_all]
    dw_in_p = _mm_dw(ht, dproj).reshape(N_DEV, SHARD_W, D_MODEL)
    dwo_p = _mm_acc(mt, dmo, "mm_dwo", col_slots=False).reshape(N_DEV, D_MODEL // N_DEV, D_MODEL)
    dwa_p = _mm_acc(yat, dya, "mm_dwa", col_slots=True)
    dwc_p = _mm_acc(yct, dyc, "mm_dwc", col_slots=True)

    partials = [dw_in_p, dwa_p, dwc_p, dwo_p]
    me_arr = jnp.reshape(me, (1,)).astype(jnp.int32)
    from_sib = _exchange_sibling(partials, "exchange_sibling")
    presums = [_presum(p, f, me_arr, f"presum{i}") for i, (p, f) in enumerate(zip(partials, from_sib))]
    s_sems, r_sems, pre_thru, land_thru, token = _exchange_chips_start(presums, "exchange_chips_start")
    gx, dsh, dsc, dnw = _mm_dh_norm_bwd(dproj, wt, x2, dout, norm_w, scale, token)
    small_p = _pack_partials(dsh, dsc, dgate, dnw, dbg, dqw_g3, dkw_g3, dcb, dlnw, dlnb, loss_p)
    small_all, dconvw_all = _all_gather([small_p, dconvw], "gather_small", vmem=True)

    small_w = (b_ada, norm_w, b_gate, q_norm_w, k_norm_w, conv_b, conv_ln_w, conv_ln_b)
    small_m = (m_b_ada, m_norm_w, m_b_gate, m_q_norm_w, m_k_norm_w, m_conv_b, m_conv_ln_w, m_conv_ln_b)
    small_v = (v_b_ada, v_norm_w, v_b_gate, v_q_norm_w, v_k_norm_w, v_conv_b, v_conv_ln_w, v_conv_ln_b)
    r_small, loss_row = _adamw_small(small_all, small_w, small_m, small_v)
    dcw_mine = lax.dynamic_slice_in_dim(dconvw_all[:, :CONV_K, :], me * HEAD_DIM, HEAD_DIM, axis=2)
    r_convw = _adamw([(dcw_mine, N_DEV)], conv_w[0], m_conv_w[0], v_conv_w[0], "adamw_conv_w", stacked=True)

    d_ada_all = small_all[:, 0:3, :].reshape(N_DEV, 3 * D_MODEL)
    d_ada_cols = lax.dynamic_slice_in_dim(d_ada_all, me * (3 * D_MODEL // N_DEV), 3 * D_MODEL // N_DEV, axis=1)
    g_wada = _ada_bwd(c_all, d_ada_cols)
    r_ada = _adamw(g_wada, w_ada[0], m_w_ada[0], v_w_ada[0], "adamw_w_ada", stacked=False)
    pres, lands = _exchange_chips_wait(s_sems, r_sems, pre_thru, land_thru, r_ada[1], "exchange_chips_wait")
    terms = [[(p, 1), (l, len(CHIP_K))] for p, l in zip(pres, lands)]
    r_win = [jnp.transpose(r) for r in _adamw(terms[0], w_in_t, m_w_in_t, v_w_in_t, "adamw_w_in", stacked=True)]
    r_wap = _adamw(terms[1], w_attn_proj[0], m_w_attn_proj[0], v_w_attn_proj[0], "adamw_w_attn_proj", stacked=True)
    r_wcp = _adamw(terms[2], w_conv_proj[0], m_w_conv_proj[0], v_w_conv_proj[0], "adamw_w_conv_proj", stacked=True)
    r_wout = _adamw(terms[3], w_out[0], m_w_out[0], v_w_out[0], "adamw_w_out", stacked=True)

    outs = [loss_row[0, 0], gx[None]]
    for k in range(4):
        b_ada_k, norm_w_k, b_gate_k, qn_k, kn_k, conv_b_k, ln_w_k, ln_b_k = r_small[k]
        outs += [r_ada[k][None], b_ada_k, norm_w_k, r_win[k][None], b_gate_k, qn_k, kn_k, r_wap[k][None],
                 r_convw[k][None], conv_b_k, ln_w_k, ln_b_k, r_wcp[k][None], r_wout[k][None]]
    return tuple(outs)
```

```python
import functools

import jax
import jax.numpy as jnp
from jax import lax
from jax.experimental import pallas as pl
from jax.experimental.pallas import tpu as pltpu

F32 = jnp.float32
BF16 = jnp.bfloat16
SDS = jax.ShapeDtypeStruct
MESH = pl.DeviceIdType.MESH

N_DEV = 8
D_MODEL = 1024
HEAD_DIM = 64
N_GROUPS = 3
DILATIONS = (1, 4, 16)
BAND = 128
BWD_UNROLL = 8
ATTN_W = 512
CONV_W = 512
CONV_K = 31
CONV_HALO = 32
IN_W = 8704
SHARD_W = IN_W // N_DEV
PAIR_W = 2 * SHARD_W
Q0, K0, V0, ZA0, U0, ZC0, G0 = 0, 1536, 3072, 4608, 5120, 6144, 6656
EPS = 1e-6
LANE = 128
VMEM_LIMIT = 56 * 1024 * 1024

ADAM_LR, ADAM_B1, ADAM_B2, ADAM_EPS, ADAM_WD, ADAM_STEP = 0.001, 0.9, 0.999, 1e-08, 0.01, 10

CONVW_FLAT = 2048


def _params(**kw):
    return pltpu.CompilerParams(vmem_limit_bytes=VMEM_LIMIT, **kw)


def _sigmoid(z):
    return 0.5 * jnp.tanh(0.5 * z) + 0.5


def _dot(a, b):
    return jnp.dot(a, b, preferred_element_type=F32)


def _dot_nt(a, b):
    return lax.dot_general(a, b, (((1,), (1,)), ((), ())), preferred_element_type=F32)


def _dot_tn(a, b):
    return lax.dot_general(a, b, (((0,), (0,)), ((), ())), preferred_element_type=F32)


def _peer(x, y, c, k):
    px = 1 - x if (k >> 2) & 1 else x
    py = 1 - y if (k >> 1) & 1 else y
    pc = 1 - c if k & 1 else c
    return (px, py, pc), 4 * px + 2 * py + pc


def _all_gather(arrays, name, vmem):
    n = len(arrays)
    space = pltpu.VMEM if vmem else pl.ANY

    def body(*refs):
        ins, outs = refs[:n], refs[n:2 * n]
        send_sems, recv_sems, local_sems = refs[2 * n:]
        x, y, c = lax.axis_index("x"), lax.axis_index("y"), lax.axis_index("c")
        me = 4 * x + 2 * y + c
        locals_ = [pltpu.make_async_copy(ins[a], outs[a].at[me], local_sems.at[a]) for a in range(n)]
        for cp in locals_:
            cp.start()
        sends = []
        for k in range(1, N_DEV):
            peer, _ = _peer(x, y, c, k)
            for a in range(n):
                cp = pltpu.make_async_remote_copy(
                    src_ref=ins[a], dst_ref=outs[a].at[me], send_sem=send_sems.at[a, k - 1],
                    recv_sem=recv_sems.at[a, k - 1], device_id=peer, device_id_type=MESH)
                cp.start()
                sends.append(cp)
        for k in range(1, N_DEV):
            peer, pidx = _peer(x, y, c, k)
            for a in range(n):
                pltpu.make_async_remote_copy(
                    src_ref=ins[a], dst_ref=outs[a].at[pidx], send_sem=send_sems.at[a, k - 1],
                    recv_sem=recv_sems.at[a, k - 1], device_id=peer, device_id_type=MESH).wait_recv()
        for cp in sends:
            cp.wait_send()
        for cp in locals_:
            cp.wait()

    return pl.pallas_call(
        body, name=name,
        out_shape=[SDS((N_DEV,) + a.shape, a.dtype) for a in arrays],
        in_specs=[pl.BlockSpec(memory_space=space)] * n,
        out_specs=[pl.BlockSpec(memory_space=space)] * n,
        scratch_shapes=[pltpu.SemaphoreType.DMA((n, N_DEV - 1)), pltpu.SemaphoreType.DMA((n, N_DEV - 1)),
                        pltpu.SemaphoreType.DMA((n,))],
        compiler_params=_params(),
    )(*arrays)


CHIP_K = (2, 4, 6)


def _all_gather_chips(arrays, name):
    n = len(arrays)
    k_y, k_x, k_d = CHIP_K

    def body(*refs):
        ins, outs = refs[:n], refs[n:2 * n]
        send_sems, recv_sems, local_sems = refs[2 * n:]
        x, y, c = lax.axis_index("x"), lax.axis_index("y"), lax.axis_index("c")
        me = 4 * x + 2 * y + c
        sib, sib_idx = _peer(x, y, c, 1)
        nbr_y, idx_y = _peer(x, y, c, k_y)
        nbr_x, idx_x = _peer(x, y, c, k_x)
        _, idx_d = _peer(x, y, c, k_d)

        def copy(a, slot, block, to, src=None):
            return pltpu.make_async_remote_copy(
                src_ref=outs[a].at[block] if src is None else src, dst_ref=outs[a].at[block],
                send_sem=send_sems.at[a, slot], recv_sem=recv_sems.at[a, slot], device_id=to, device_id_type=MESH)

        locals_ = [pltpu.make_async_copy(ins[a], outs[a].at[me], local_sems.at[a]) for a in range(n)]
        for cp in locals_:
            cp.start()
        for a in range(n):
            copy(a, 0, me, sib, src=ins[a]).start()
            copy(a, 1, me, nbr_y, src=ins[a]).start()
            copy(a, 2, me, nbr_x, src=ins[a]).start()

        def arrived(slot, block, frm, send_on_to=None):
            for a in range(n):
                copy(a, slot, block, frm).wait_recv()
                if send_on_to is not None:
                    copy(a, 3, block, send_on_to).start()
                copy(a, 3 + slot, block, sib).start()

        @pl.when(c == 0)
        def _():
            arrived(1, idx_y, nbr_y, send_on_to=nbr_x)
            arrived(2, idx_x, nbr_x)

        @pl.when(c == 1)
        def _():
            arrived(2, idx_x, nbr_x, send_on_to=nbr_y)
            arrived(1, idx_y, nbr_y)

        arrived(3, idx_d, nbr_x)
        for a in range(n):
            copy(a, 0, sib_idx, sib).wait_recv()
        for slot, k in ((4, k_y), (5, k_x), (6, k_d)):
            _, pidx = _peer(x, y, 1 - c, k)
            for a in range(n):
                copy(a, slot, pidx, sib).wait_recv()
        for slot in range(N_DEV - 1):
            for a in range(n):
                copy(a, slot, me, sib).wait_send()
        for cp in locals_:
            cp.wait()

    return pl.pallas_call(
        body, name=name,
        out_shape=[SDS((N_DEV,) + a.shape, a.dtype) for a in arrays],
        in_specs=[pl.BlockSpec(memory_space=pl.ANY)] * n,
        out_specs=[pl.BlockSpec(memory_space=pl.ANY)] * n,
        scratch_shapes=[pltpu.SemaphoreType.DMA((n, N_DEV - 1)), pltpu.SemaphoreType.DMA((n, N_DEV - 1)),
                        pltpu.SemaphoreType.DMA((n,))],
        compiler_params=_params(),
    )(*arrays)


def _exchange_sibling(arrays, name):
    n = len(arrays)
    ks = (0,) + CHIP_K

    def body(*refs):
        ins, outs = refs[:n], refs[n:2 * n]
        send_sems, recv_sems = refs[2 * n:]
        x, y, c = lax.axis_index("x"), lax.axis_index("y"), lax.axis_index("c")
        sib, sib_idx = _peer(x, y, c, 1)
        sends = []
        for i, k in enumerate(ks):
            _, tgt = _peer(x, y, 1 - c, k) if k else (None, sib_idx)
            for a in range(n):
                cp = pltpu.make_async_remote_copy(
                    src_ref=ins[a].at[tgt], dst_ref=outs[a].at[i], send_sem=send_sems.at[a, i],
                    recv_sem=recv_sems.at[a, i], device_id=sib, device_id_type=MESH)
                cp.start()
                sends.append(cp)
        for cp in sends:
            cp.wait_recv()
        for cp in sends:
            cp.wait_send()

    return pl.pallas_call(
        body, name=name,
        out_shape=[SDS((len(ks),) + a.shape[1:], a.dtype) for a in arrays],
        in_specs=[pl.BlockSpec(memory_space=pl.ANY)] * n,
        out_specs=[pl.BlockSpec(memory_space=pl.ANY)] * n,
        scratch_shapes=[pltpu.SemaphoreType.DMA((n, len(ks))), pltpu.SemaphoreType.DMA((n, len(ks)))],
        compiler_params=_params(),
    )(*arrays)


def _presum(mine, from_sib, me_arr, name):
    _, rows, cols = mine.shape
    tr = _row_tile(rows)
    ns = 1 + len(CHIP_K)

    def body(me_ref, a_ref, b_ref, o_ref):
        del me_ref
        o_ref[...] = (a_ref[...].astype(F32) + b_ref[...].astype(F32)).astype(o_ref.dtype)

    grid_spec = pltpu.PrefetchScalarGridSpec(
        num_scalar_prefetch=1, grid=(ns, rows // tr),
        in_specs=[pl.BlockSpec((1, tr, cols), lambda j, i, me: (jnp.bitwise_xor(me[0], 2 * j), i, 0)),
                  pl.BlockSpec((1, tr, cols), lambda j, i, me: (j, i, 0))],
        out_specs=pl.BlockSpec((1, tr, cols), lambda j, i, me: (j, i, 0)))
    return pl.pallas_call(body, name=name, grid_spec=grid_spec, out_shape=SDS((ns, rows, cols), mine.dtype),
                          compiler_params=_params())(me_arr, mine, from_sib)


HBM_SPEC = pl.BlockSpec(memory_space=pltpu.HBM)
SEM_SPEC = pl.BlockSpec(memory_space=pltpu.SEMAPHORE)
SIDE_EFFECT = pltpu.SideEffectType.DATAFLOW_SIDE_EFFECTING


def _chips_copies(pre_refs, land_refs, send_sems, recv_sems):
    x, y, c = lax.axis_index("x"), lax.axis_index("y"), lax.axis_index("c")
    copies = []
    for j, k in enumerate(CHIP_K):
        peer, _ = _peer(x, y, c, k)
        for a in range(len(pre_refs)):
            copies.append(pltpu.make_async_remote_copy(
                src_ref=pre_refs[a].at[1 + j], dst_ref=land_refs[a].at[j], send_sem=send_sems.at[a * len(CHIP_K) + j],
                recv_sem=recv_sems.at[a * len(CHIP_K) + j], device_id=peer, device_id_type=MESH))
    return copies


def _exchange_chips_start(presums, name):
    n = len(presums)

    def body(*refs):
        pre, land = refs[:n], refs[n:2 * n]
        send_sems, recv_sems = refs[2 * n], refs[2 * n + 1]
        token = refs[-1]
        for cp in _chips_copies(pre, land, send_sems, recv_sems):
            cp.start()
        token[...] = jnp.zeros_like(token)

    nk = len(CHIP_K)
    hbm = [pltpu.HBM(p.shape, p.dtype) for p in presums]
    hbm_land = [pltpu.HBM((nk,) + p.shape[1:], p.dtype) for p in presums]
    res = pl.pallas_call(
        body, name=name,
        out_shape=(pltpu.SemaphoreType.DMA((n * nk,)), pltpu.SemaphoreType.DMA((n * nk,)), *hbm, *hbm_land, SDS((8, LANE), F32)),
        in_specs=[HBM_SPEC] * (2 * n),
        out_specs=(SEM_SPEC, SEM_SPEC, *([HBM_SPEC] * (2 * n)), pl.BlockSpec(memory_space=pltpu.VMEM)),
        input_output_aliases={i: 2 + i for i in range(2 * n)},
        compiler_params=pltpu.CompilerParams(has_side_effects=SIDE_EFFECT),
    )(*[pltpu.with_memory_space_constraint(p, pltpu.HBM) for p in presums],
      *[pltpu.with_memory_space_constraint(lax.empty((nk,) + p.shape[1:], p.dtype), pltpu.HBM) for p in presums])
    return res[0], res[1], res[2:2 + n], res[2 + n:2 + 2 * n], res[-1]


def _exchange_chips_wait(send_sems, recv_sems, pre_thru, land_thru, after, name):
    n = len(pre_thru)

    def body(*refs):
        pre, land = refs[:n], refs[n:2 * n]
        s_sems, r_sems = refs[2 * n], refs[2 * n + 1]
        for cp in _chips_copies(pre, land, s_sems, r_sems):
            cp.wait_send()
            cp.wait_recv()

    hbm = [pltpu.HBM(p.shape, p.dtype) for p in (*pre_thru, *land_thru)]
    res = pl.pallas_call(
        body, name=name, out_shape=tuple(hbm),
        in_specs=[HBM_SPEC] * (2 * n) + [SEM_SPEC, SEM_SPEC, pl.BlockSpec(memory_space=pl.ANY)],
        out_specs=tuple([HBM_SPEC] * (2 * n)),
        input_output_aliases={i: i for i in range(2 * n)},
        compiler_params=pltpu.CompilerParams(has_side_effects=SIDE_EFFECT),
    )(*pre_thru, *land_thru, send_sems, recv_sems, after)
    return res[:n], res[n:]


def _exchange_chips(presums, name):
    n = len(presums)
    nk = len(CHIP_K)

    def body(*refs):
        pre, land = refs[:n], refs[n:2 * n]
        send_sems, recv_sems = refs[2 * n:]
        copies = _chips_copies(pre, land, send_sems, recv_sems)
        for cp in copies:
            cp.start()
        for cp in copies:
            cp.wait_recv()
        for cp in copies:
            cp.wait_send()

    return pl.pallas_call(
        body, name=name,
        out_shape=[SDS((nk,) + p.shape[1:], p.dtype) for p in presums],
        in_specs=[pl.BlockSpec(memory_space=pl.ANY)] * n,
        out_specs=[pl.BlockSpec(memory_space=pl.ANY)] * n,
        scratch_shapes=[pltpu.SemaphoreType.DMA((n * nk,)), pltpu.SemaphoreType.DMA((n * nk,))],
        compiler_params=_params(),
    )(*presums)


def _cast_bf16(w, name):
    def body(w_ref, o_ref):
        o_ref[...] = w_ref[...].astype(BF16)

    return pl.pallas_call(body, name=name, out_shape=SDS(w.shape, BF16), compiler_params=_params())(w)


def _cols_from_slots(wg, name):
    _, rows, cols = wg.shape

    def body(w_ref, o_ref):
        for j in range(N_DEV):
            o_ref[:, j * cols:(j + 1) * cols] = w_ref[j]

    return pl.pallas_call(body, name=name, out_shape=SDS((rows, N_DEV * cols), wg.dtype), compiler_params=_params())(wg)


def _ada_fwd(c_all, w_ada):
    def body(c_ref, w_ref, o_ref):
        cv = c_ref[...]
        sc = (cv * _sigmoid(cv)).astype(BF16)
        o_ref[...] = _dot(sc, w_ref[...].astype(BF16))

    return pl.pallas_call(body, name="ada_fwd", out_shape=SDS((N_DEV, w_ada.shape[1]), F32),
                          compiler_params=_params())(c_all, w_ada)


def _ada_bwd(c_all, d_ada_cols):
    def body(c_ref, d_ref, o_ref):
        cv = c_ref[...]
        sc = (cv * _sigmoid(cv)).astype(BF16)
        o_ref[...] = _dot_tn(sc, d_ref[...].astype(BF16))

    return pl.pallas_call(body, name="ada_bwd", out_shape=SDS((D_MODEL, d_ada_cols.shape[1]), F32),
                          compiler_params=_params())(c_all, d_ada_cols)


def _norm_fwd(x, norm_w, scale, shift):
    s = x.shape[0]
    tr = 1024

    def body(x_ref, nw_ref, sc_ref, sh_ref, h_ref, ht_ref):
        xv = x_ref[...]
        r = lax.rsqrt(jnp.mean(xv * xv, axis=-1, keepdims=True) + EPS)
        h = (xv * r * nw_ref[...]) * (1.0 + sc_ref[...]) + sh_ref[...]
        h_ref[...] = h.astype(BF16)
        ht_ref[...] = h.T.astype(BF16)

    vec = pl.BlockSpec((1, D_MODEL), lambda i: (0, 0))
    return pl.pallas_call(
        body, name="norm_fwd", grid=(s // tr,),
        in_specs=[pl.BlockSpec((tr, D_MODEL), lambda i: (i, 0)), vec, vec, vec],
        out_specs=[pl.BlockSpec((tr, D_MODEL), lambda i: (i, 0)), pl.BlockSpec((D_MODEL, tr), lambda i: (0, i))],
        out_shape=[SDS((s, D_MODEL), BF16), SDS((D_MODEL, s), BF16)], compiler_params=_params(),
    )(x, norm_w, scale, shift)


def _mm_in(h, wt):
    s = h.shape[0]
    tm = 512

    def body(h_ref, w_ref, o_ref):
        o_ref[...] = _dot_nt(h_ref[...], w_ref[...])

    return pl.pallas_call(
        body, name="mm_in", grid=(IN_W // PAIR_W, s // tm),
        in_specs=[pl.BlockSpec((tm, D_MODEL), lambda p, m: (m, 0)),
                  pl.BlockSpec((PAIR_W, D_MODEL), lambda p, m: (p, 0))],
        out_specs=pl.BlockSpec((tm, PAIR_W), lambda p, m: (m, p)),
        out_shape=SDS((s, IN_W), F32), compiler_params=_params(),
    )(h, wt)


def _head_ones():
    a = lax.broadcasted_iota(jnp.int32, (LANE, LANE), 0) // HEAD_DIM
    b = lax.broadcasted_iota(jnp.int32, (LANE, LANE), 1) // HEAD_DIM
    return (a == b).astype(BF16)


def _head_sums(t, ones):
    return _dot(t.astype(BF16), ones)


def _band_bias(bias, transposed=False):
    qi = lax.broadcasted_iota(jnp.int32, (2 * BAND, 2 * BAND), 1 if transposed else 0) % BAND
    kj = lax.broadcasted_iota(jnp.int32, (2 * BAND, 2 * BAND), 0 if transposed else 1)
    dist = qi + BAND - kj
    valid = (dist >= 0) & (dist <= BAND)
    bias[1] = jnp.where(valid, 0.0, -1e30)
    bias[0] = jnp.where(valid & (kj >= BAND), 0.0, -1e30)


def _token_rows(j, d, chunk, per_r):
    return pl.ds(j // per_r + (j % per_r) * (chunk * d), chunk, stride=d)


def _deinterleave_many(jobs, ones, d, sub_len, chunk, unroll):
    per_r = sub_len // chunk

    def step(j, _):
        tok = _token_rows(j, d, chunk, per_r)
        for src_ref, dst_ref, w_ref, scale, dst_off in jobs:
            t = src_ref[tok, :]
            if w_ref is not None:
                ms = _head_sums(t * t, ones) * (1.0 / HEAD_DIM)
                t = t * lax.rsqrt(ms + EPS) * (w_ref[...] * scale)
            dst_ref[pl.ds(pl.multiple_of(dst_off + j * chunk, BAND), chunk), :] = t.astype(dst_ref.dtype)
        return 0
    lax.fori_loop(0, d * per_r, step, 0, unroll=unroll)


def _deinterleave(src_ref, dst_ref, w_ref, ones, d, sub_len, chunk, scale, dst_off):
    _deinterleave_many([(src_ref, dst_ref, w_ref, scale, dst_off)], ones, d, sub_len, chunk, 4)


N_PAIRS = ATTN_W // LANE


def _attn_fwd(proj, qw2, kw2):
    s = proj.shape[0]

    def group_body(g, step, q_ref, k_ref, v_ref, qw_ref, kw_ref, o_ref, l_ref, qd, kd, vd, od, ld, bias):
        d = DILATIONS[g]
        sub_len = s // d
        nb = sub_len // BAND
        chunk = min(sub_len, 256)
        lo = lax.broadcasted_iota(jnp.int32, (1, LANE), 1) < HEAD_DIM
        ones = _head_ones()

        @pl.when(step == 0)
        def _():
            _band_bias(bias)

        kd[0:BAND, :] = jnp.zeros((BAND, LANE), BF16)
        vd[0:BAND, :] = jnp.zeros((BAND, LANE), BF16)
        _deinterleave_many([(q_ref, qd, qw_ref, HEAD_DIM ** -0.5, 0), (k_ref, kd, kw_ref, 1.0, BAND),
                            (v_ref, vd, None, 1.0, BAND)], ones, d, sub_len, chunk, 4)

        def block(t, _):
            base = pl.multiple_of(t * BAND, BAND)
            q = qd[pl.ds(base, BAND), :]
            k2 = kd[pl.ds(base, 2 * BAND), :]
            v2 = vd[pl.ds(base, 2 * BAND), :]
            zero = jnp.zeros_like(q)
            qs = jnp.concatenate([jnp.where(lo, q, zero), jnp.where(lo, zero, q)], axis=0)
            sc = _dot_nt(qs, k2) + bias[jnp.minimum(t % nb, 1)]
            m = jnp.max(sc, axis=-1, keepdims=True)
            p = jnp.exp(sc - m)
            den = jnp.sum(p, axis=-1, keepdims=True)
            u = _dot(p.astype(BF16), v2) * (1.0 / den)
            lse = m + jnp.log(den)
            od[pl.ds(base, BAND), :] = jnp.where(lo, u[:BAND], u[BAND:])
            ld[pl.ds(base, BAND), :] = jnp.where(lo, lse[:BAND], lse[BAND:])
            return 0
        lax.fori_loop(0, s // BAND, block, 0, unroll=16)

        per_r = sub_len // chunk

        def back(j, _):
            src = pl.ds(pl.multiple_of(j * chunk, chunk), chunk)
            dst = _token_rows(j, d, chunk, per_r)
            o_ref[dst, :] = od[src, :]
            l_ref[dst, :] = ld[src, :]
            return 0
        lax.fori_loop(0, d * per_r, back, 0, unroll=2)

    def body(*refs):
        step = pl.program_id(0)
        for g in range(N_GROUPS):
            pl.when(step // N_PAIRS == g)(functools.partial(group_body, g, step, *refs))

    col = lambda off: pl.BlockSpec((s, LANE), lambda i, off=off: (0, off // LANE + i))
    vec = pl.BlockSpec((1, LANE), lambda i: (0, 0))
    out = pl.BlockSpec((s, LANE), lambda i: (0, i))
    width = N_GROUPS * ATTN_W
    return pl.pallas_call(
        body, name="attn_fwd", grid=(N_GROUPS * N_PAIRS,),
        in_specs=[col(Q0), col(K0), col(V0), vec, vec],
        out_specs=[out, out, out, pl.BlockSpec((s + BAND, LANE), lambda i: (0, i)),
                   pl.BlockSpec((s + BAND, LANE), lambda i: (0, i))],
        out_shape=[SDS((s, width), F32)] * 2 + [SDS((s, width), BF16)] + [SDS((s + BAND, width), BF16)] * 2,
        scratch_shapes=[pltpu.VMEM((s, LANE), F32), pltpu.VMEM((s, LANE), F32),
                        pltpu.VMEM((2, 2 * BAND, 2 * BAND), F32)],
        compiler_params=_params(),
    )(proj, proj, proj, qw2, kw2)


def _attn_bwd(proj, qn, kn, vn, da, lse_delta, qw2, kw2, dproj):
    s = proj.shape[0]
    n_steps = N_GROUPS * N_PAIRS

    def group_body(g, hp, q_ref, k_ref, qn_ref, kd, vd, da_ref, ld_ref, qw_ref, kw_ref, dp_in, dp_out,
                   dqw_ref, dkw_ref, kdt, dad, lst, dlt, dqt, dqd, dkd, dvd, st, st_k, stb, bias_t, wacc, sem):
        del dp_in
        d = DILATIONS[g]
        sub_len = s // d
        nb = sub_len // BAND
        chunk = min(sub_len, 256)
        lo = lax.broadcasted_iota(jnp.int32, (1, LANE), 1) < HEAD_DIM
        row_lo = lax.broadcasted_iota(jnp.int32, (LANE, 1), 0) < HEAD_DIM
        ones = _head_ones()
        per_r = sub_len // chunk
        cblk = chunk // BAND

        @pl.when(hp == 0)
        def _():
            _band_bias(bias_t, transposed=True)

        kdt[0] = jnp.zeros((LANE, BAND), BF16)

        def k_step(t, _):
            kdt[1 + t] = kd[pl.ds(pl.multiple_of(BAND + t * BAND, BAND), BAND), :].astype(F32).T.astype(BF16)
            return 0
        lax.fori_loop(0, s // BAND, k_step, 0, unroll=4)
        _deinterleave(da_ref, dad, None, ones, d, sub_len, chunk, 1.0, 0)

        def rows_step(j, _):
            tok = _token_rows(j, d, chunk, per_r)
            tt = ld_ref[tok, :].T
            for u in range(cblk):
                cols = slice(u * BAND, (u + 1) * BAND)
                lst[j * cblk + u, 0:1, :] = tt[0:1, cols]
                lst[j * cblk + u, 1:2, :] = tt[HEAD_DIM:HEAD_DIM + 1, cols]
                dlt[j * cblk + u, 0:1, :] = tt[HEAD_DIM // 2:HEAD_DIM // 2 + 1, cols]
                dlt[j * cblk + u, 1:2, :] = tt[HEAD_DIM + HEAD_DIM // 2:HEAD_DIM + HEAD_DIM // 2 + 1, cols]
            return 0
        lax.fori_loop(0, d * per_r, rows_step, 0, unroll=4)

        def block(t, carry):
            ck, cv = carry
            base = pl.multiple_of(t * BAND, BAND)
            q = qn_ref[pl.ds(base, BAND), :]
            k2 = kd[pl.ds(base, 2 * BAND), :]
            v2 = vd[pl.ds(base, 2 * BAND), :]
            k2t = jnp.concatenate([kdt[t], kdt[t + 1]], axis=1)
            dav = dad[pl.ds(base, BAND), :]
            zero = jnp.zeros_like(q)
            qs = jnp.concatenate([jnp.where(lo, q, zero), jnp.where(lo, zero, q)], axis=0)
            das = jnp.concatenate([jnp.where(lo, dav, zero), jnp.where(lo, zero, dav)], axis=0)
            ls_row = jnp.concatenate([lst[t, 0:1, :], lst[t, 1:2, :]], axis=1)
            dl_row = jnp.concatenate([dlt[t, 0:1, :], dlt[t, 1:2, :]], axis=1)
            sc_t = _dot_nt(k2, qs) + bias_t[jnp.minimum(t % nb, 1)]
            p_t = jnp.exp(sc_t - ls_row)
            dp_t = _dot_nt(v2, das)
            ds_t = (p_t * (dp_t - dl_row)).astype(BF16)
            dv2 = _dot(p_t.astype(BF16), das)
            dk2 = _dot(ds_t, qs)
            dvd[pl.ds(base, BAND), :] = cv + dv2[:BAND]
            dkd[pl.ds(base, BAND), :] = ck + dk2[:BAND]
            dq_t = _dot(k2t, ds_t)
            dqt[t] = jnp.where(row_lo, dq_t[:, :BAND], dq_t[:, BAND:])
            return dk2[BAND:], dv2[BAND:]

        def blocks(i, carry):
            for u in range(BWD_UNROLL):
                carry = block(i * BWD_UNROLL + u, carry)
            return carry
        zeros = jnp.zeros((BAND, LANE), F32)
        ck, cv = lax.fori_loop(0, s // (BAND * BWD_UNROLL), blocks, (zeros, zeros))
        dkd[s:s + BAND, :] = ck
        dvd[s:s + BAND, :] = cv

        def dq_rows(t, _):
            dqd[pl.ds(pl.multiple_of(t * BAND, BAND), BAND), :] = dqt[t].T
            return 0
        lax.fori_loop(0, s // BAND, dq_rows, 0, unroll=4)

        def col_copy(slot, col0):
            return pltpu.make_async_copy(
                stb.at[slot], dp_out.at[:, pl.ds(pl.multiple_of(col0 + LANE * hp, LANE), LANE)], sem.at[slot])

        def store_cols(slot, col0, src):
            @pl.when(hp > 0)
            def _():
                col_copy(slot, col0).wait()
            stb[slot] = src[...].astype(BF16)
            col_copy(slot, col0).start()

        sides = ((q_ref, dqd, 0, qw_ref, HEAD_DIM ** -0.5, st), (k_ref, dkd, BAND, kw_ref, 1.0, st_k))
        wacc[...] = jnp.zeros_like(wacc)

        def norm_step(j, _):
            tok = _token_rows(j, d, chunk, per_r)
            for i, (src_ref, dy_ref, dy_off, w_ref, scale, dst) in enumerate(sides):
                t = src_ref[tok, :]
                dy = dy_ref[pl.ds(pl.multiple_of(dy_off + j * chunk, BAND), chunk), :]
                rr = lax.rsqrt(_head_sums(t * t, ones) * (1.0 / HEAD_DIM) + EPS)
                nrm = t * rr
                wacc[i] += jnp.sum((dy * nrm).reshape(chunk // 8, 8, LANE), axis=0)
                dn = dy * (w_ref[...] * scale)
                dst[tok, :] = rr * (dn - nrm * (_head_sums(dn * nrm, ones) * (1.0 / HEAD_DIM)))
            return 0
        lax.fori_loop(0, d * per_r, norm_step, 0, unroll=4)

        @pl.when(hp == 0)
        def _():
            dqw_ref[...] = jnp.zeros_like(dqw_ref)
            dkw_ref[...] = jnp.zeros_like(dkw_ref)

        for i, dw_ref in enumerate((dqw_ref, dkw_ref)):
            dw_ref[...] += jnp.broadcast_to(jnp.sum(wacc[i], axis=0, keepdims=True) * sides[i][4], dw_ref.shape)
        store_cols(0, Q0, st)
        store_cols(1, K0, st_k)

        def v_back(j, _):
            src = pl.ds(pl.multiple_of(BAND + j * chunk, BAND), chunk)
            st[_token_rows(j, d, chunk, per_r), :] = dvd[src, :]
            return 0
        lax.fori_loop(0, d * per_r, v_back, 0, unroll=2)
        store_cols(2, V0, st)

        @pl.when(hp == n_steps - 1)
        def _():
            for slot, col0 in enumerate((Q0, K0, V0)):
                col_copy(slot, col0).wait()

    def body(*refs):
        step = pl.program_id(0)
        for g in range(N_GROUPS):
            pl.when(step // N_PAIRS == g)(functools.partial(group_body, g, step, *refs))

    col = lambda off: pl.BlockSpec((s, LANE), lambda i, off=off: (0, off // LANE + i))
    mid = pl.BlockSpec((s, LANE), lambda i: (0, i))
    padded = pl.BlockSpec((s + BAND, LANE), lambda i: (0, i))
    slot4 = pl.BlockSpec((s, LANE), lambda i: (0, i % N_PAIRS))
    vec = pl.BlockSpec((1, LANE), lambda i: (0, 0))
    acc = pl.BlockSpec((8, LANE), lambda i: (0, 0))
    any_ = pl.BlockSpec(memory_space=pl.ANY)
    return pl.pallas_call(
        body, name="attn_bwd", grid=(n_steps,),
        in_specs=[col(Q0), col(K0), mid, padded, padded, slot4, slot4, vec, vec, any_],
        out_specs=[any_, acc, acc],
        out_shape=[SDS(dproj.shape, dproj.dtype), SDS((8, LANE), F32), SDS((8, LANE), F32)],
        input_output_aliases={9: 0},
        scratch_shapes=[pltpu.VMEM((s // BAND + 1, LANE, BAND), BF16), pltpu.VMEM((s, LANE), BF16),
                        pltpu.VMEM((s // BAND, 8, BAND), F32), pltpu.VMEM((s // BAND, 8, BAND), F32),
                        pltpu.VMEM((s // BAND, LANE, BAND), F32),
                        pltpu.VMEM((s, LANE), F32), pltpu.VMEM((s + BAND, LANE), F32), pltpu.VMEM((s + BAND, LANE), F32),
                        pltpu.VMEM((s, LANE), F32), pltpu.VMEM((s, LANE), F32), pltpu.VMEM((3, s, LANE), BF16),
                        pltpu.VMEM((2, 2 * BAND, 2 * BAND), F32), pltpu.VMEM((2, 8, LANE), F32),
                        pltpu.SemaphoreType.DMA((3,))],
        compiler_params=_params(),
    )(proj, proj, qn, kn, vn, da, lse_delta, qw2, kw2, dproj)


def _tap_views(ext_ref, sh_ref, offsets, tr, cols):
    for b in range(8):
        group = [j for j, o in enumerate(offsets) if o % 8 == b]
        if not group:
            continue
        first = min(offsets[j] for j in group)
        span = tr + max(offsets[j] for j in group) - first
        sh_ref[0:span, cols] = ext_ref[first:first + span, cols]
        for j in group:
            yield j, sh_ref[offsets[j] - first:offsets[j] - first + tr, cols]


def _silu_grad(z, sg):
    return sg * (1.0 + z * (1.0 - sg))


def _glu(u):
    a_h, b_h = u[:, :CONV_W], u[:, CONV_W:]
    sg = _sigmoid(b_h)
    return a_h, sg, a_h * sg


def _tail(x, tgt, proj, o3, l3, wa, wc, wo, gate, bga, bgc, convw, convb, lnw, lnb, bd):
    s = x.shape[0]
    tr = 256

    def body(x_ref, t_ref, za_ref, u_ref, uh_ref, zc_ref, g0_ref, g1_ref, g2_ref, g3_ref,
             o0_ref, o1_ref, o2_ref, l0_ref, l1_ref, l2_ref, wa_ref, wc_ref, wo_ref,
             gate_ref, bga_ref, bgc_ref, cw_ref, cb_ref, lnw_ref, lnb_ref, bd_ref,
             dout_ref, da_ref, ld_ref, dcv_ref, mt_ref, yat_ref, yct_ref, dmo_ref, dya_ref, dyc_ref, dp_ref,
             dgate_ref, dbg_ref, dlnw_ref, dlnb_ref, dcb_ref, loss_ref,
             ext, sh, st_za, st_zc, st_g, sems):
        i = pl.program_id(0)

        @pl.when(i == 0)
        def _():
            for r in (dgate_ref, dbg_ref, dlnw_ref, dlnb_ref, dcb_ref, loss_ref):
                r[...] = jnp.zeros_like(r)

        def acc_rows(ref, v):
            ref[...] += jnp.broadcast_to(jnp.sum(v, axis=0, keepdims=True), ref.shape)

        la, lb, lc = l0_ref[...], l1_ref[...], l2_ref[...]
        mx = jnp.maximum(jnp.maximum(la, lb), lc)
        ea, eb, ec = jnp.exp(la - mx), jnp.exp(lb - mx), jnp.exp(lc - mx)
        den = ea + eb + ec
        inv = 1.0 / den
        attn = (ea * inv) * o0_ref[...] + (eb * inv) * o1_ref[...] + (ec * inv) * o2_ref[...]
        lse = mx + jnp.log(den)

        za = za_ref[...]
        sga = _sigmoid(za)
        sa = za * sga
        ya_in = attn * sa
        y_attn = _dot(ya_in.astype(BF16), wa_ref[...])

        _, _, glu = _glu(u_ref[...])
        _, _, glu_h = _glu(uh_ref[...])
        ext[0:CONV_HALO, :] = jnp.where(i > 0, glu_h, 0.0)
        ext[CONV_HALO:CONV_HALO + tr, :] = glu
        cv_blocks = []
        for cb in range(CONV_W // LANE):
            cols = slice(cb * LANE, (cb + 1) * LANE)
            cv_c = jnp.broadcast_to(cb_ref[:, cols], (tr, LANE))
            for j, rows in _tap_views(ext, sh, [CONV_HALO - (CONV_K - 1) + j for j in range(CONV_K)], tr, cols):
                cv_c = cv_c + cw_ref[j:j + 1, cols] * rows
            cv_blocks.append(cv_c)
        cv = jnp.concatenate(cv_blocks, axis=1)
        mu = jnp.mean(cv, axis=-1, keepdims=True)
        xc = cv - mu
        rstd = lax.rsqrt(jnp.mean(xc * xc, axis=-1, keepdims=True) + EPS)
        nrm = xc * rstd
        ln = nrm * lnw_ref[...] + lnb_ref[...]
        sgl = _sigmoid(ln)
        cs = ln * sgl
        zc = zc_ref[...]
        sgc = _sigmoid(zc)
        scz = zc * sgc
        yc_in = cs * scz
        y_conv = _dot(yc_in.astype(BF16), wc_ref[...])

        ga = _sigmoid(jnp.concatenate([g0_ref[...], g1_ref[...]], axis=1) + bga_ref[...])
        gc = _sigmoid(jnp.concatenate([g2_ref[...], g3_ref[...]], axis=1) + bgc_ref[...])
        merged = ga * y_attn + gc * y_conv
        mo = _dot(merged.astype(BF16), wo_ref[...])
        gate_v = gate_ref[...]
        err = (x_ref[...] + gate_v * mo) - t_ref[...]
        loss_ref[...] += 0.5 * jnp.sum(jnp.mean(err * err, axis=-1, keepdims=True))
        d_out = err * (1.0 / D_MODEL)
        dout_ref[...] = d_out

        rows = pl.ds(pl.multiple_of(i * tr, tr), tr)
        cps = [pltpu.make_async_copy(st_za, dp_ref.at[rows, pl.ds(ZA0, ATTN_W)], sems.at[0]),
               pltpu.make_async_copy(st_zc, dp_ref.at[rows, pl.ds(ZC0, CONV_W)], sems.at[1]),
               pltpu.make_async_copy(st_g, dp_ref.at[rows, pl.ds(G0, 2 * D_MODEL)], sems.at[2])]

        @pl.when(i > 0)
        def _():
            for cp in cps:
                cp.wait()

        acc_rows(dgate_ref, d_out * mo)
        dmo_b = (d_out * gate_v).astype(BF16)
        dmo_ref[...] = dmo_b
        mt_ref[...] = merged.T.astype(BF16)
        d_merged = _dot_nt(dmo_b, wo_ref[...])
        d_ya = (d_merged * ga).astype(BF16)
        d_yc = (d_merged * gc).astype(BF16)
        dya_ref[...] = d_ya
        dyc_ref[...] = d_yc
        dga = d_merged * y_attn * (ga * (1.0 - ga))
        dgc = d_merged * y_conv * (gc * (1.0 - gc))
        dgs = jnp.concatenate([dga, dgc], axis=1)
        acc_rows(dbg_ref, dgs)
        st_g[...] = dgs.astype(BF16)

        yat_ref[...] = ya_in.T.astype(BF16)
        d_ya_in = _dot_nt(d_ya, wa_ref[...])
        d_attn = d_ya_in * sa
        da_ref[...] = d_attn
        st_za[...] = (d_ya_in * attn * _silu_grad(za, sga)).astype(BF16)
        prod = d_attn * attn
        hi = prod.astype(BF16)
        lo_ = (prod - hi.astype(F32)).astype(BF16)
        delta = _dot(hi, bd_ref[...]) + _dot(lo_, bd_ref[...])
        first_half = (lax.broadcasted_iota(jnp.int32, (1, ATTN_W), 1) % HEAD_DIM) < HEAD_DIM // 2
        ld_ref[...] = jnp.where(first_half, lse, delta)

        yct_ref[...] = yc_in.T.astype(BF16)
        d_yc_in = _dot_nt(d_yc, wc_ref[...])
        st_zc[...] = (d_yc_in * cs * _silu_grad(zc, sgc)).astype(BF16)
        d_ln = (d_yc_in * scz) * _silu_grad(ln, sgl)
        acc_rows(dlnw_ref, d_ln * nrm)
        acc_rows(dlnb_ref, d_ln)
        d_nrm = d_ln * lnw_ref[...]
        d_cv = rstd * (d_nrm - jnp.mean(d_nrm, axis=-1, keepdims=True)
                       - nrm * jnp.mean(d_nrm * nrm, axis=-1, keepdims=True))
        acc_rows(dcb_ref, d_cv)
        dcv_ref[...] = d_cv

        for cp in cps:
            cp.start()

        @pl.when(i == s // tr - 1)
        def _():
            for cp in cps:
                cp.wait()

    def rows(width, colblk=0):
        return pl.BlockSpec((tr, width), lambda i, colblk=colblk: (i, colblk))

    def const(shape):
        return pl.BlockSpec(shape, lambda i: (0,) * len(shape))

    halo = pl.BlockSpec((CONV_HALO, D_MODEL), lambda i: (jnp.maximum(i * (tr // CONV_HALO) - 1, 0), U0 // D_MODEL))
    in_specs = [rows(D_MODEL), rows(D_MODEL), rows(ATTN_W, ZA0 // ATTN_W), rows(D_MODEL, U0 // D_MODEL), halo,
                rows(CONV_W, ZC0 // CONV_W)]
    in_specs += [rows(512, G0 // 512 + j) for j in range(4)]
    in_specs += [rows(ATTN_W, g) for g in range(N_GROUPS)] * 2
    in_specs += [const(wa.shape), const(wc.shape), const(wo.shape), const((1, D_MODEL)), const((1, D_MODEL)),
                 const((1, D_MODEL)), const(convw.shape), const((1, CONV_W)), const((1, CONV_W)), const((1, CONV_W)),
                 const(bd.shape)]
    tcol = lambda width: pl.BlockSpec((width, tr), lambda i: (0, i))
    out_specs = [rows(D_MODEL), rows(ATTN_W), rows(ATTN_W), rows(CONV_W),
                 tcol(D_MODEL), tcol(ATTN_W), tcol(CONV_W), rows(D_MODEL), rows(D_MODEL), rows(D_MODEL),
                 pl.BlockSpec(memory_space=pl.ANY),
                 const((8, D_MODEL)), const((8, 2 * D_MODEL)), const((8, CONV_W)), const((8, CONV_W)), const((8, CONV_W)),
                 const((8, LANE))]
    out_shape = [SDS((s, D_MODEL), F32), SDS((s, ATTN_W), F32), SDS((s, ATTN_W), F32),
                 SDS((s, CONV_W), F32),
                 SDS((D_MODEL, s), BF16), SDS((ATTN_W, s), BF16), SDS((CONV_W, s), BF16),
                 SDS((s, D_MODEL), BF16), SDS((s, D_MODEL), BF16), SDS((s, D_MODEL), BF16),
                 SDS((s, IN_W), BF16),
                 SDS((8, D_MODEL), F32), SDS((8, 2 * D_MODEL), F32), SDS((8, CONV_W), F32), SDS((8, CONV_W), F32),
                 SDS((8, CONV_W), F32), SDS((8, LANE), F32)]
    return pl.pallas_call(
        body, name="tail", grid=(s // tr,), in_specs=in_specs, out_specs=out_specs, out_shape=out_shape,
        scratch_shapes=[pltpu.VMEM((CONV_HALO + tr, CONV_W), F32), pltpu.VMEM((CONV_HALO + tr, CONV_W), F32),
                        pltpu.VMEM((tr, ATTN_W), BF16),
                        pltpu.VMEM((tr, CONV_W), BF16), pltpu.VMEM((tr, 2 * D_MODEL), BF16),
                        pltpu.SemaphoreType.DMA((3,))],
        compiler_params=_params(),
    )(x, tgt, proj, proj, proj, proj, proj, proj, proj, proj, *o3, *l3, wa, wc, wo, gate, bga, bgc,
      convw, convb, lnw, lnb, bd)


def _conv_bwd(dcv, proj, convw, dproj):
    s = dcv.shape[0]
    tr = 128
    nt = s // tr

    def body(dcv_ref, dcvn_ref, u_ref, uh_ref, cw_ref, dp_in, dp_out, dw_ref, extg, extd, sh):
        del dp_in
        i = pl.program_id(0)

        @pl.when(i == 0)
        def _():
            dw_ref[...] = jnp.zeros_like(dw_ref)

        _, _, glu = _glu(u_ref[...])
        _, _, glu_h = _glu(uh_ref[...])
        extg[0:CONV_HALO, :] = jnp.where(i > 0, glu_h, 0.0)
        extg[CONV_HALO:CONV_HALO + tr, :] = glu
        extd[0:tr, :] = dcv_ref[...]
        extd[tr:tr + CONV_HALO, :] = jnp.where(i < nt - 1, dcvn_ref[...], 0.0)
        for cb in range(CONV_W // LANE):
            cols = slice(cb * LANE, (cb + 1) * LANE)
            dglu = jnp.zeros((tr, LANE), F32)
            for j, rows in _tap_views(extd, sh, [CONV_K - 1 - j for j in range(CONV_K)], tr, cols):
                dglu = dglu + cw_ref[j:j + 1, cols] * rows
            dcv_c = dcv_ref[:, cols]
            for j, rows in _tap_views(extg, sh, [CONV_HALO - (CONV_K - 1) + j for j in range(CONV_K)], tr, cols):
                dw_ref[8 * j:8 * j + 8, cols] += jnp.sum((dcv_c * rows).reshape(tr // 8, 8, LANE), axis=0)
            a_h = u_ref[:, cols]
            sgb = _sigmoid(u_ref[:, CONV_W + cb * LANE:CONV_W + (cb + 1) * LANE])
            dp_out[:, cols] = (dglu * sgb).astype(BF16)
            dp_out[:, CONV_W + cb * LANE:CONV_W + (cb + 1) * LANE] = (dglu * a_h * (sgb * (1.0 - sgb))).astype(BF16)

    ucol = U0 // D_MODEL
    return pl.pallas_call(
        body, name="conv_bwd", grid=(nt,),
        in_specs=[pl.BlockSpec((tr, CONV_W), lambda i: (i, 0)),
                  pl.BlockSpec((CONV_HALO, CONV_W), lambda i: (jnp.minimum((i + 1) * (tr // CONV_HALO), s // CONV_HALO - 1), 0)),
                  pl.BlockSpec((tr, D_MODEL), lambda i: (i, ucol)),
                  pl.BlockSpec((CONV_HALO, D_MODEL), lambda i: (jnp.maximum(i * (tr // CONV_HALO) - 1, 0), ucol)),
                  pl.BlockSpec(convw.shape, lambda i: (0, 0)),
                  pl.BlockSpec(memory_space=pl.ANY)],
        out_specs=[pl.BlockSpec((tr, D_MODEL), lambda i: (i, ucol)), pl.BlockSpec((8 * CONV_HALO, CONV_W), lambda i: (0, 0))],
        out_shape=[SDS(dproj.shape, dproj.dtype), SDS((8 * CONV_HALO, CONV_W), F32)],
        input_output_aliases={5: 0},
        scratch_shapes=[pltpu.VMEM((CONV_HALO + tr, CONV_W), F32)] * 3,
        compiler_params=_params(),
    )(dcv, dcv, proj, proj, convw, dproj)


def _mm_acc(at, b, name, col_slots):
    m, s = at.shape
    n = b.shape[1]
    tk = 1024
    nk = s // tk

    def body(a_ref, b_ref, o_ref, acc):
        k = pl.program_id(0)

        @pl.when(k == 0)
        def _():
            acc[...] = jnp.zeros_like(acc)

        acc[...] += _dot(a_ref[...], b_ref[...])

        @pl.when(k == nk - 1)
        def _():
            if col_slots:
                w = n // N_DEV
                for j in range(N_DEV):
                    o_ref[j] = acc[:, j * w:(j + 1) * w].astype(BF16)
            else:
                o_ref[...] = acc[...].astype(BF16)

    if col_slots:
        out_shape = SDS((N_DEV, m, n // N_DEV), BF16)
        out_spec = pl.BlockSpec((N_DEV, m, n // N_DEV), lambda k: (0, 0, 0))
    else:
        out_shape = SDS((m, n), BF16)
        out_spec = pl.BlockSpec((m, n), lambda k: (0, 0))
    return pl.pallas_call(
        body, name=name, grid=(nk,),
        in_specs=[pl.BlockSpec((m, tk), lambda k: (0, k)), pl.BlockSpec((tk, n), lambda k: (k, 0))],
        out_specs=out_spec, out_shape=out_shape, scratch_shapes=[pltpu.VMEM((m, n), F32)],
        compiler_params=_params(),
    )(at, b)


def _mm_dw(ht, dproj):
    s = ht.shape[1]
    tk = 1024
    nk = s // tk

    def body(a_ref, b_ref, o_ref, acc):
        k = pl.program_id(1)

        @pl.when(k == 0)
        def _():
            acc[...] = jnp.zeros_like(acc)

        acc[...] += _dot(a_ref[...], b_ref[...])

        @pl.when(k == nk - 1)
        def _():
            o_ref[...] = acc[...].T.astype(BF16)

    return pl.pallas_call(
        body, name="mm_dw", grid=(IN_W // PAIR_W, nk),
        in_specs=[pl.BlockSpec((D_MODEL, tk), lambda p, k: (0, k)), pl.BlockSpec((tk, PAIR_W), lambda p, k: (k, p))],
        out_specs=pl.BlockSpec((PAIR_W, D_MODEL), lambda p, k: (p, 0)),
        out_shape=SDS((IN_W, D_MODEL), BF16), scratch_shapes=[pltpu.VMEM((D_MODEL, PAIR_W), F32)],
        compiler_params=_params(),
    )(ht, dproj)


def _mm_dh_norm_bwd(dproj, wt, x, dout, norm_w, scale, token):
    s = dproj.shape[0]
    tm = 1024
    n_p = IN_W // PAIR_W

    def body(dp_ref, w_ref, x_ref, do_ref, nw_ref, sc_ref, tok_ref, gx_ref, dsh_ref, dsc_ref, dnw_ref, dh_acc):
        del tok_ref
        m, p = pl.program_id(0), pl.program_id(1)
        part = _dot(dp_ref[...], w_ref[...])

        @pl.when(p == 0)
        def _():
            dh_acc[...] = part

        @pl.when(p > 0)
        def _():
            dh_acc[...] += part

        @pl.when((m == 0) & (p == 0))
        def _():
            for r in (dsh_ref, dsc_ref, dnw_ref):
                r[...] = jnp.zeros_like(r)

        @pl.when(p == n_p - 1)
        def _():
            def acc_rows(ref, v):
                ref[...] += jnp.broadcast_to(jnp.sum(v, axis=0, keepdims=True), ref.shape)

            xv = x_ref[...]
            dh_v = dh_acc[...]
            r = lax.rsqrt(jnp.mean(xv * xv, axis=-1, keepdims=True) + EPS)
            xn = xv * r
            one_sc = 1.0 + sc_ref[...]
            acc_rows(dsh_ref, dh_v)
            acc_rows(dsc_ref, dh_v * (xn * nw_ref[...]))
            acc_rows(dnw_ref, dh_v * xn * one_sc)
            dxn = dh_v * (nw_ref[...] * one_sc)
            gx_ref[...] = do_ref[...] + r * (dxn - xn * jnp.mean(dxn * xn, axis=-1, keepdims=True))

    rows = pl.BlockSpec((tm, D_MODEL), lambda m, p: (m, 0))
    vec = pl.BlockSpec((1, D_MODEL), lambda m, p: (0, 0))
    acc = pl.BlockSpec((8, D_MODEL), lambda m, p: (0, 0))
    return pl.pallas_call(
        body, name="mm_dh_norm_bwd", grid=(s // tm, n_p),
        in_specs=[pl.BlockSpec((tm, PAIR_W), lambda m, p: (m, p)),
                  pl.BlockSpec((PAIR_W, D_MODEL), lambda m, p: (p, 0)),
                  rows, rows, vec, vec, pl.BlockSpec(token.shape, lambda m, p: (0, 0))],
        out_specs=[rows, acc, acc, acc],
        out_shape=[SDS((s, D_MODEL), F32)] + [SDS((8, D_MODEL), F32)] * 3,
        scratch_shapes=[pltpu.VMEM((tm, D_MODEL), F32)], compiler_params=_params(),
    )(dproj, wt, x, dout, norm_w, scale, token)


SMALL_ROWS = 8
QN_COL, KN_COL, CB_COL, LOSS_COL = 0, LANE, 2 * LANE, 2 * LANE + CONV_W


def _pack_partials(dsh, dsc, dgate, dnw, dbg, dqw3, dkw3, dcb, dlnw, dlnb, loss_p):
    n3 = len(dqw3)

    def body(*refs):
        dsh_r, dsc_r, dgate_r, dnw_r, dbg_r = refs[:5]
        dq_r, dk_r = refs[5:5 + n3], refs[5 + n3:5 + 2 * n3]
        dcb_r, dlnw_r, dlnb_r, loss_r, o_ref = refs[5 + 2 * n3:]

        def both_heads(rs):
            t = rs[0][0:1, :]
            for r in rs[1:]:
                t = t + r[0:1, :]
            return t + pltpu.roll(t, HEAD_DIM, axis=1)

        o_ref[0:1, :] = dsh_r[0:1, :]
        o_ref[1:2, :] = dsc_r[0:1, :]
        o_ref[2:3, :] = dgate_r[0:1, :]
        o_ref[3:4, :] = dnw_r[0:1, :]
        o_ref[4:5, :] = dbg_r[0:1, 0:D_MODEL]
        o_ref[5:6, :] = dbg_r[0:1, D_MODEL:]
        o_ref[6:7, QN_COL:QN_COL + LANE] = both_heads(dq_r)
        o_ref[6:7, KN_COL:KN_COL + LANE] = both_heads(dk_r)
        o_ref[6:7, CB_COL:CB_COL + CONV_W] = dcb_r[0:1, :]
        o_ref[6:7, LOSS_COL:LOSS_COL + LANE] = loss_r[0:1, :]
        o_ref[6:7, LOSS_COL + LANE:] = jnp.zeros((1, D_MODEL - LOSS_COL - LANE), F32)
        o_ref[7:8, 0:CONV_W] = dlnw_r[0:1, :]
        o_ref[7:8, CONV_W:] = dlnb_r[0:1, :]

    return pl.pallas_call(body, name="pack_partials", out_shape=SDS((SMALL_ROWS, D_MODEL), F32),
                          compiler_params=_params())(dsh, dsc, dgate, dnw, dbg, *dqw3, *dkw3, dcb, dlnw, dlnb, loss_p)


def _adamw_update(g, w, m, v):
    bc1 = 1.0 - ADAM_B1 ** ADAM_STEP
    bc2 = 1.0 - ADAM_B2 ** ADAM_STEP
    m_new = ADAM_B1 * m + (1.0 - ADAM_B1) * g
    v_new = ADAM_B2 * v + (1.0 - ADAM_B2) * (g * g)
    delta = -ADAM_LR * ((m_new / bc1) / (jnp.sqrt(v_new / bc2) + ADAM_EPS) + ADAM_WD * w)
    return delta, m_new, v_new


def _adamw_small(small_all, ws, ms, vs):
    n = len(ws)
    where = [(slice(0, 3), None), (slice(3, 4), None), (slice(4, 6), None), (6, QN_COL), (6, KN_COL), (6, CB_COL),
             (7, 0), (7, CONV_W)]

    def body(*refs):
        g_ref = refs[0]
        w_r, m_r, v_r = refs[1:1 + n], refs[1 + n:1 + 2 * n], refs[1 + 2 * n:1 + 3 * n]
        outs = refs[1 + 3 * n:]
        g_o, d_o, m_o, v_o, loss_o = outs[:n], outs[n:2 * n], outs[2 * n:3 * n], outs[3 * n:4 * n], outs[4 * n]
        gsum = g_ref[0]
        for dev in range(1, N_DEV):
            gsum = gsum + g_ref[dev]
        loss_o[...] = gsum[6:7, LOSS_COL:LOSS_COL + LANE]
        for i, (rows, col) in enumerate(where):
            width = w_r[i].shape[1]
            if col is None:
                g = jnp.concatenate([gsum[r:r + 1, :] for r in range(rows.start, rows.stop)], axis=1)
            else:
                g = gsum[rows:rows + 1, col:col + width]
            delta, m_new, v_new = _adamw_update(g, w_r[i][...], m_r[i][...], v_r[i][...])
            g_o[i][...] = g
            d_o[i][...] = delta
            m_o[i][...] = m_new
            v_o[i][...] = v_new

    shapes = [SDS(w.shape, F32) for w in ws]
    res = pl.pallas_call(body, name="adamw_small", out_shape=shapes * 4 + [SDS((1, LANE), F32)],
                         compiler_params=_params())(small_all, *ws, *ms, *vs)
    return [res[k * n:(k + 1) * n] for k in range(4)], res[4 * n]


def _row_tile(rows):
    if rows <= 128:
        return rows
    if rows % 256 == 0:
        return 256
    return 128 if rows % 128 == 0 else SHARD_W // 4


def _adamw(gsrc, w, m, v, name, stacked):
    rows, cols = w.shape
    tr = _row_tile(rows)
    n_src = len(gsrc) if stacked else 1

    def body(*refs):
        g_refs, (w_ref, m_ref, v_ref, go_ref, d_ref, mo_ref, vo_ref) = refs[:n_src], refs[n_src:]
        if stacked:
            g = None
            for g_ref, (_, slots) in zip(g_refs, gsrc):
                for j in range(slots):
                    t = g_ref[j].astype(F32)
                    g = t if g is None else g + t
        else:
            g = g_refs[0][...]
        delta, m_new, v_new = _adamw_update(g, w_ref[...], m_ref[...], v_ref[...])
        go_ref[...] = g
        d_ref[...] = delta
        mo_ref[...] = m_new
        vo_ref[...] = v_new

    blk = pl.BlockSpec((tr, cols), lambda i: (i, 0))
    if stacked:
        gspecs = [pl.BlockSpec((slots, tr, arr.shape[2]), lambda i: (0, i, 0)) for arr, slots in gsrc]
        gargs = [arr for arr, _ in gsrc]
    else:
        gspecs, gargs = [blk], [gsrc]
    in_specs = gspecs + [blk, blk, blk]
    args = gargs + [w, m, v]
    return pl.pallas_call(
        body, name=name, grid=(rows // tr,), in_specs=in_specs, out_specs=[blk] * 4,
        out_shape=[SDS((rows, cols), F32)] * 4, compiler_params=_params(),
    )(*args)


def kernel(x, c, w_ada, b_ada, norm_w, w_in, b_gate, q_norm_w, k_norm_w, w_attn_proj, conv_w, conv_b, conv_ln_w, conv_ln_b, w_conv_proj, w_out, loss_target, m_w_ada, m_b_ada, m_norm_w, m_w_in, m_b_gate, m_q_norm_w, m_k_norm_w, m_w_attn_proj, m_conv_w, m_conv_b, m_conv_ln_w, m_conv_ln_b, m_w_conv_proj, m_w_out, v_w_ada, v_b_ada, v_norm_w, v_w_in, v_b_gate, v_q_norm_w, v_k_norm_w, v_w_attn_proj, v_conv_w, v_conv_b, v_conv_ln_w, v_conv_ln_b, v_w_conv_proj, v_w_out):
    xi, yi, ci = lax.axis_index("x"), lax.axis_index("y"), lax.axis_index("c")
    me = 4 * xi + 2 * yi + ci
    x2, tgt2 = x[0], loss_target[0]
    w_in_t, m_w_in_t, v_w_in_t = (jnp.transpose(a[0]) for a in (w_in, m_w_in, v_w_in))
    s = x2.shape[0]

    cw_flat = jnp.pad(conv_w[0].reshape(1, -1), ((0, 0), (0, CONVW_FLAT - CONV_K * HEAD_DIM)))
    pre = jnp.concatenate([c, cw_flat], axis=1).reshape(8, -1)
    (pre_all,) = _all_gather([pre], "gather_c_convw", vmem=True)
    pre_all = pre_all.reshape(N_DEV, -1)
    c_all = pre_all[:, :D_MODEL]
    convw_full = pre_all[:, D_MODEL:D_MODEL + CONV_K * HEAD_DIM].reshape(N_DEV, CONV_K, HEAD_DIM)
    convw_full = jnp.transpose(convw_full, (1, 0, 2)).reshape(CONV_K, CONV_W)
    convw_pad = jnp.pad(convw_full, ((0, CONV_HALO - CONV_K), (0, 0)))

    ada_part = _ada_fwd(c_all, w_ada[0])
    (ada_all,) = _all_gather([ada_part], "gather_ada", vmem=True)
    ada = lax.dynamic_index_in_dim(ada_all, me, axis=1, keepdims=False).reshape(1, 3 * D_MODEL) + b_ada
    shift, scale, gate = ada[:, :D_MODEL], ada[:, D_MODEL:2 * D_MODEL], ada[:, 2 * D_MODEL:]

    wt_g, wa_g, wc_g, wo_g = _all_gather_chips(
        [_cast_bf16(w_in_t, "cast_win"), _cast_bf16(w_attn_proj[0], "cast_wa"), _cast_bf16(w_conv_proj[0], "cast_wc"),
         _cast_bf16(w_out[0], "cast_wo")], "gather_weights")
    wt = wt_g.reshape(IN_W, D_MODEL)
    wa = _cols_from_slots(wa_g, "cols_wa")
    wc = _cols_from_slots(wc_g, "cols_wc")
    wo = wo_g.reshape(D_MODEL, D_MODEL)

    h, ht = _norm_fwd(x2, norm_w, scale, shift)
    proj = _mm_in(h, wt)
    qw2 = jnp.tile(q_norm_w, (1, 2))
    kw2 = jnp.tile(k_norm_w, (1, 2))
    o_all, l_all, qn, kn, vn = _attn_fwd(proj, qw2, kw2)
    o3, l3 = [o_all] * N_GROUPS, [l_all] * N_GROUPS
    head_id = jnp.arange(ATTN_W) // HEAD_DIM
    bd = (head_id[:, None] == head_id[None, :]).astype(BF16)
    (dout, da, lse_delta, dcv, mt, yat, yct, dmo, dya, dyc, dproj,
     dgate, dbg, dlnw, dlnb, dcb, loss_p) = _tail(
        x2, tgt2, proj, o3, l3, wa, wc, wo, gate, b_gate[:, :D_MODEL], b_gate[:, D_MODEL:], convw_pad,
        conv_b, conv_ln_w, conv_ln_b, bd)

    dproj, dconvw8 = _conv_bwd(dcv, proj, convw_pad, dproj)
    dconvw = jnp.sum(dconvw8.reshape(CONV_HALO, 8, CONV_W), axis=1)
    dproj, dqw_all, dkw_all = _attn_bwd(proj, qn, kn, vn, da, lse_delta, qw2, kw2, dproj)
    dqw_g3, dkw_g3 = [dqw_all], [dkw_all]
    dw_in_p = _mm_dw(ht, dproj).reshape(N_DEV, SHARD_W, D_MODEL)
    dwo_p = _mm_acc(mt, dmo, "mm_dwo", col_slots=False).reshape(N_DEV, D_MODEL // N_DEV, D_MODEL)
    dwa_p = _mm_acc(yat, dya, "mm_dwa", col_slots=True)
    dwc_p = _mm_acc(yct, dyc, "mm_dwc", col_slots=True)

    partials = [dw_in_p, dwa_p, dwc_p, dwo_p]
    me_arr = jnp.reshape(me, (1,)).astype(jnp.int32)
    from_sib = _exchange_sibling(partials, "exchange_sibling")
    presums = [_presum(p, f, me_arr, f"presum{i}") for i, (p, f) in enumerate(zip(partials, from_sib))]
    s_sems, r_sems, pre_thru, land_thru, token = _exchange_chips_start(presums, "exchange_chips_start")
    gx, dsh, dsc, dnw = _mm_dh_norm_bwd(dproj, wt, x2, dout, norm_w, scale, token)
    small_p = _pack_partials(dsh, dsc, dgate, dnw, dbg, dqw_g3, dkw_g3, dcb, dlnw, dlnb, loss_p)
    small_all, dconvw_all = _all_gather([small_p, dconvw], "gather_small", vmem=True)

    small_w = (b_ada, norm_w, b_gate, q_norm_w, k_norm_w, conv_b, conv_ln_w, conv_ln_b)
    small_m = (m_b_ada, m_norm_w, m_b_gate, m_q_norm_w, m_k_norm_w, m_conv_b, m_conv_ln_w, m_conv_ln_b)
    small_v = (v_b_ada, v_norm_w, v_b_gate, v_q_norm_w, v_k_norm_w, v_conv_b, v_conv_ln_w, v_conv_ln_b)
    r_small, loss_row = _adamw_small(small_all, small_w, small_m, small_v)
    dcw_mine = lax.dynamic_slice_in_dim(dconvw_all[:, :CONV_K, :], me * HEAD_DIM, HEAD_DIM, axis=2)
    r_convw = _adamw([(dcw_mine, N_DEV)], conv_w[0], m_conv_w[0], v_conv_w[0], "adamw_conv_w", stacked=True)

    d_ada_all = small_all[:, 0:3, :].reshape(N_DEV, 3 * D_MODEL)
    d_ada_cols = lax.dynamic_slice_in_dim(d_ada_all, me * (3 * D_MODEL // N_DEV), 3 * D_MODEL // N_DEV, axis=1)
    g_wada = _ada_bwd(c_all, d_ada_cols)
    r_ada = _adamw(g_wada, w_ada[0], m_w_ada[0], v_w_ada[0], "adamw_w_ada", stacked=False)
    pres, lands = _exchange_chips_wait(s_sems, r_sems, pre_thru, land_thru, r_ada[1], "exchange_chips_wait")
    terms = [[(p, 1), (l, len(CHIP_K))] for p, l in zip(pres, lands)]
    r_win = [jnp.transpose(r) for r in _adamw(terms[0], w_in_t, m_w_in_t, v_w_in_t, "adamw_w_in", stacked=True)]
    r_wap = _adamw(terms[1], w_attn_proj[0], m_w_attn_proj[0], v_w_attn_proj[0], "adamw_w_attn_proj", stacked=True)
    r_wcp = _adamw(terms[2], w_conv_proj[0], m_w_conv_proj[0], v_w_conv_proj[0], "adamw_w_conv_proj", stacked=True)
    r_wout = _adamw(terms[3], w_out[0], m_w_out[0], v_w_out[0], "adamw_w_out", stacked=True)

    outs = [loss_row[0, 0], gx[None]]
    for k in range(4):
        b_ada_k, norm_w_k, b_gate_k, qn_k, kn_k, conv_b_k, ln_w_k, ln_b_k = r_small[k]
        outs += [r_ada[k][None], b_ada_k, norm_w_k, r_win[k][None], b_gate_k, qn_k, kn_k, r_wap[k][None],
                 r_convw[k][None], conv_b_k, ln_w_k, ln_b_k, r_wcp[k][None], r_wout[k][None]]
    return tuple(outs)
```

```python
import functools

import jax
import jax.numpy as jnp
from jax import lax
from jax.experimental import pallas as pl
from jax.experimental.pallas import tpu as pltpu

F32 = jnp.float32
BF16 = jnp.bfloat16
SDS = jax.ShapeDtypeStruct
MESH = pl.DeviceIdType.MESH

N_DEV = 8
D_MODEL = 1024
HEAD_DIM = 64
N_GROUPS = 3
DILATIONS = (1, 4, 16)
BAND = 128
BWD_UNROLL = 8
ATTN_W = 512
CONV_W = 512
CONV_K = 31
CONV_HALO = 32
IN_W = 8704
SHARD_W = IN_W // N_DEV
PAIR_W = 2 * SHARD_W
Q0, K0, V0, ZA0, U0, ZC0, G0 = 0, 1536, 3072, 4608, 5120, 6144, 6656
EPS = 1e-6
LANE = 128
VMEM_LIMIT = 56 * 1024 * 1024

ADAM_LR, ADAM_B1, ADAM_B2, ADAM_EPS, ADAM_WD, ADAM_STEP = 0.001, 0.9, 0.999, 1e-08, 0.01, 10

CONVW_FLAT = 2048


def _params(**kw):
    return pltpu.CompilerParams(vmem_limit_bytes=VMEM_LIMIT, **kw)


def _sigmoid(z):
    return 0.5 * jnp.tanh(0.5 * z) + 0.5


def _dot(a, b):
    return jnp.dot(a, b, preferred_element_type=F32)


def _dot_nt(a, b):
    return lax.dot_general(a, b, (((1,), (1,)), ((), ())), preferred_element_type=F32)


def _dot_tn(a, b):
    return lax.dot_general(a, b, (((0,), (0,)), ((), ())), preferred_element_type=F32)


def _peer(x, y, c, k):
    px = 1 - x if (k >> 2) & 1 else x
    py = 1 - y if (k >> 1) & 1 else y
    pc = 1 - c if k & 1 else c
    return (px, py, pc), 4 * px + 2 * py + pc


def _all_gather(arrays, name, vmem):
    n = len(arrays)
    space = pltpu.VMEM if vmem else pl.ANY

    def body(*refs):
        ins, outs = refs[:n], refs[n:2 * n]
        send_sems, recv_sems, local_sems = refs[2 * n:]
        x, y, c = lax.axis_index("x"), lax.axis_index("y"), lax.axis_index("c")
        me = 4 * x + 2 * y + c
        locals_ = [pltpu.make_async_copy(ins[a], outs[a].at[me], local_sems.at[a]) for a in range(n)]
        for cp in locals_:
            cp.start()
        sends = []
        for k in range(1, N_DEV):
            peer, _ = _peer(x, y, c, k)
            for a in range(n):
                cp = pltpu.make_async_remote_copy(
                    src_ref=ins[a], dst_ref=outs[a].at[me], send_sem=send_sems.at[a, k - 1],
                    recv_sem=recv_sems.at[a, k - 1], device_id=peer, device_id_type=MESH)
                cp.start()
                sends.append(cp)
        for k in range(1, N_DEV):
            peer, pidx = _peer(x, y, c, k)
            for a in range(n):
                pltpu.make_async_remote_copy(
                    src_ref=ins[a], dst_ref=outs[a].at[pidx], send_sem=send_sems.at[a, k - 1],
                    recv_sem=recv_sems.at[a, k - 1], device_id=peer, device_id_type=MESH).wait_recv()
        for cp in sends:
            cp.wait_send()
        for cp in locals_:
            cp.wait()

    return pl.pallas_call(
        body, name=name,
        out_shape=[SDS((N_DEV,) + a.shape, a.dtype) for a in arrays],
        in_specs=[pl.BlockSpec(memory_space=space)] * n,
        out_specs=[pl.BlockSpec(memory_space=space)] * n,
        scratch_shapes=[pltpu.SemaphoreType.DMA((n, N_DEV - 1)), pltpu.SemaphoreType.DMA((n, N_DEV - 1)),
                        pltpu.SemaphoreType.DMA((n,))],
        compiler_params=_params(),
    )(*arrays)


CHIP_K = (2, 4, 6)


def _all_gather_chips(arrays, name):
    n = len(arrays)
    k_y, k_x, k_d = CHIP_K

    def body(*refs):
        ins, outs = refs[:n], refs[n:2 * n]
        send_sems, recv_sems, local_sems = refs[2 * n:]
        x, y, c = lax.axis_index("x"), lax.axis_index("y"), lax.axis_index("c")
        me = 4 * x + 2 * y + c
        sib, sib_idx = _peer(x, y, c, 1)
        nbr_y, idx_y = _peer(x, y, c, k_y)
        nbr_x, idx_x = _peer(x, y, c, k_x)
        _, idx_d = _peer(x, y, c, k_d)

        def copy(a, slot, block, to, src=None):
            return pltpu.make_async_remote_copy(
                src_ref=outs[a].at[block] if src is None else src, dst_ref=outs[a].at[block],
                send_sem=send_sems.at[a, slot], recv_sem=recv_sems.at[a, slot], device_id=to, device_id_type=MESH)

        locals_ = [pltpu.make_async_copy(ins[a], outs[a].at[me], local_sems.at[a]) for a in range(n)]
        for cp in locals_:
            cp.start()
        for a in range(n):
            copy(a, 0, me, sib, src=ins[a]).start()
            copy(a, 1, me, nbr_y, src=ins[a]).start()
            copy(a, 2, me, nbr_x, src=ins[a]).start()

        def arrived(slot, block, frm, send_on_to=None):
            for a in range(n):
                copy(a, slot, block, frm).wait_recv()
                if send_on_to is not None:
                    copy(a, 3, block, send_on_to).start()
                copy(a, 3 + slot, block, sib).start()

        @pl.when(c == 0)
        def _():
            arrived(1, idx_y, nbr_y, send_on_to=nbr_x)
            arrived(2, idx_x, nbr_x)

        @pl.when(c == 1)
        def _():
            arrived(2, idx_x, nbr_x, send_on_to=nbr_y)
            arrived(1, idx_y, nbr_y)

        arrived(3, idx_d, nbr_x)
        for a in range(n):
            copy(a, 0, sib_idx, sib).wait_recv()
        for slot, k in ((4, k_y), (5, k_x), (6, k_d)):
            _, pidx = _peer(x, y, 1 - c, k)
            for a in range(n):
                copy(a, slot, pidx, sib).wait_recv()
        for slot in range(N_DEV - 1):
            for a in range(n):
                copy(a, slot, me, sib).wait_send()
        for cp in locals_:
            cp.wait()

    return pl.pallas_call(
        body, name=name,
        out_shape=[SDS((N_DEV,) + a.shape, a.dtype) for a in arrays],
        in_specs=[pl.BlockSpec(memory_space=pl.ANY)] * n,
        out_specs=[pl.BlockSpec(memory_space=pl.ANY)] * n,
        scratch_shapes=[pltpu.SemaphoreType.DMA((n, N_DEV - 1)), pltpu.SemaphoreType.DMA((n, N_DEV - 1)),
                        pltpu.SemaphoreType.DMA((n,))],
        compiler_params=_params(),
    )(*arrays)


def _exchange_sibling(arrays, name):
    n = len(arrays)
    ks = (0,) + CHIP_K

    def body(*refs):
        ins, outs = refs[:n], refs[n:2 * n]
        send_sems, recv_sems = refs[2 * n:]
        x, y, c = lax.axis_index("x"), lax.axis_index("y"), lax.axis_index("c")
        sib, sib_idx = _peer(x, y, c, 1)
        sends = []
        for i, k in enumerate(ks):
            _, tgt = _peer(x, y, 1 - c, k) if k else (None, sib_idx)
            for a in range(n):
                cp = pltpu.make_async_remote_copy(
                    src_ref=ins[a].at[tgt], dst_ref=outs[a].at[i], send_sem=send_sems.at[a, i],
                    recv_sem=recv_sems.at[a, i], device_id=sib, device_id_type=MESH)
                cp.start()
                sends.append(cp)
        for cp in sends:
            cp.wait_recv()
        for cp in sends:
            cp.wait_send()

    return pl.pallas_call(
        body, name=name,
        out_shape=[SDS((len(ks),) + a.shape[1:], a.dtype) for a in arrays],
        in_specs=[pl.BlockSpec(memory_space=pl.ANY)] * n,
        out_specs=[pl.BlockSpec(memory_space=pl.ANY)] * n,
        scratch_shapes=[pltpu.SemaphoreType.DMA((n, len(ks))), pltpu.SemaphoreType.DMA((n, len(ks)))],
        compiler_params=_params(),
    )(*arrays)


def _presum(mine, from_sib, me_arr, name):
    _, rows, cols = mine.shape
    tr = _row_tile(rows)
    ns = 1 + len(CHIP_K)

    def body(me_ref, a_ref, b_ref, o_ref):
        del me_ref
        o_ref[...] = (a_ref[...].astype(F32) + b_ref[...].astype(F32)).astype(o_ref.dtype)

    grid_spec = pltpu.PrefetchScalarGridSpec(
        num_scalar_prefetch=1, grid=(ns, rows // tr),
        in_specs=[pl.BlockSpec((1, tr, cols), lambda j, i, me: (jnp.bitwise_xor(me[0], 2 * j), i, 0)),
                  pl.BlockSpec((1, tr, cols), lambda j, i, me: (j, i, 0))],
        out_specs=pl.BlockSpec((1, tr, cols), lambda j, i, me: (j, i, 0)))
    return pl.pallas_call(body, name=name, grid_spec=grid_spec, out_shape=SDS((ns, rows, cols), mine.dtype),
                          compiler_params=_params())(me_arr, mine, from_sib)


HBM_SPEC = pl.BlockSpec(memory_space=pltpu.HBM)
SEM_SPEC = pl.BlockSpec(memory_space=pltpu.SEMAPHORE)
SIDE_EFFECT = pltpu.SideEffectType.DATAFLOW_SIDE_EFFECTING


def _chips_copies(pre_refs, land_refs, send_sems, recv_sems):
    x, y, c = lax.axis_index("x"), lax.axis_index("y"), lax.axis_index("c")
    copies = []
    for j, k in enumerate(CHIP_K):
        peer, _ = _peer(x, y, c, k)
        for a in range(len(pre_refs)):
            copies.append(pltpu.make_async_remote_copy(
                src_ref=pre_refs[a].at[1 + j], dst_ref=land_refs[a].at[j], send_sem=send_sems.at[a * len(CHIP_K) + j],
                recv_sem=recv_sems.at[a * len(CHIP_K) + j], device_id=peer, device_id_type=MESH))
    return copies


def _exchange_chips_start(presums, name):
    n = len(presums)

    def body(*refs):
        pre, land = refs[:n], refs[n:2 * n]
        send_sems, recv_sems = refs[2 * n], refs[2 * n + 1]
        token = refs[-1]
        for cp in _chips_copies(pre, land, send_sems, recv_sems):
            cp.start()
        token[...] = jnp.zeros_like(token)

    nk = len(CHIP_K)
    hbm = [pltpu.HBM(p.shape, p.dtype) for p in presums]
    hbm_land = [pltpu.HBM((nk,) + p.shape[1:], p.dtype) for p in presums]
    res = pl.pallas_call(
        body, name=name,
        out_shape=(pltpu.SemaphoreType.DMA((n * nk,)), pltpu.SemaphoreType.DMA((n * nk,)), *hbm, *hbm_land, SDS((8, LANE), F32)),
        in_specs=[HBM_SPEC] * (2 * n),
        out_specs=(SEM_SPEC, SEM_SPEC, *([HBM_SPEC] * (2 * n)), pl.BlockSpec(memory_space=pltpu.VMEM)),
        input_output_aliases={i: 2 + i for i in range(2 * n)},
        compiler_params=pltpu.CompilerParams(has_side_effects=SIDE_EFFECT),
    )(*[pltpu.with_memory_space_constraint(p, pltpu.HBM) for p in presums],
      *[pltpu.with_memory_space_constraint(lax.empty((nk,) + p.shape[1:], p.dtype), pltpu.HBM) for p in presums])
    return res[0], res[1], res[2:2 + n], res[2 + n:2 + 2 * n], res[-1]


def _exchange_chips_wait(send_sems, recv_sems, pre_thru, land_thru, after, name):
    n = len(pre_thru)

    def body(*refs):
        pre, land = refs[:n], refs[n:2 * n]
        s_sems, r_sems = refs[2 * n], refs[2 * n + 1]
        for cp in _chips_copies(pre, land, s_sems, r_sems):
            cp.wait_send()
            cp.wait_recv()

    hbm = [pltpu.HBM(p.shape, p.dtype) for p in (*pre_thru, *land_thru)]
    res = pl.pallas_call(
        body, name=name, out_shape=tuple(hbm),
        in_specs=[HBM_SPEC] * (2 * n) + [SEM_SPEC, SEM_SPEC, pl.BlockSpec(memory_space=pl.ANY)],
        out_specs=tuple([HBM_SPEC] * (2 * n)),
        input_output_aliases={i: i for i in range(2 * n)},
        compiler_params=pltpu.CompilerParams(has_side_effects=SIDE_EFFECT),
    )(*pre_thru, *land_thru, send_sems, recv_sems, after)
    return res[:n], res[n:]


def _cast_bf16(w, name):
    def body(w_ref, o_ref):
        o_ref[...] = w_ref[...].astype(BF16)

    return pl.pallas_call(body, name=name, out_shape=SDS(w.shape, BF16), compiler_params=_params())(w)


def _cols_from_slots(wg, name):
    _, rows, cols = wg.shape

    def body(w_ref, o_ref):
        for j in range(N_DEV):
            o_ref[:, j * cols:(j + 1) * cols] = w_ref[j]

    return pl.pallas_call(body, name=name, out_shape=SDS((rows, N_DEV * cols), wg.dtype), compiler_params=_params())(wg)


def _ada_fwd(c_all, w_ada):
    def body(c_ref, w_ref, o_ref):
        cv = c_ref[...]
        sc = (cv * _sigmoid(cv)).astype(BF16)
        o_ref[...] = _dot(sc, w_ref[...].astype(BF16))

    return pl.pallas_call(body, name="ada_fwd", out_shape=SDS((N_DEV, w_ada.shape[1]), F32),
                          compiler_params=_params())(c_all, w_ada)


def _ada_bwd(c_all, d_ada_cols):
    def body(c_ref, d_ref, o_ref):
        cv = c_ref[...]
        sc = (cv * _sigmoid(cv)).astype(BF16)
        o_ref[...] = _dot_tn(sc, d_ref[...].astype(BF16))

    return pl.pallas_call(body, name="ada_bwd", out_shape=SDS((D_MODEL, d_ada_cols.shape[1]), F32),
                          compiler_params=_params())(c_all, d_ada_cols)


def _norm_fwd(x, norm_w, scale, shift):
    s = x.shape[0]
    tr = 1024

    def body(x_ref, nw_ref, sc_ref, sh_ref, h_ref, ht_ref):
        xv = x_ref[...]
        r = lax.rsqrt(jnp.mean(xv * xv, axis=-1, keepdims=True) + EPS)
        h = (xv * r * nw_ref[...]) * (1.0 + sc_ref[...]) + sh_ref[...]
        h_ref[...] = h.astype(BF16)
        ht_ref[...] = h.T.astype(BF16)

    vec = pl.BlockSpec((1, D_MODEL), lambda i: (0, 0))
    return pl.pallas_call(
        body, name="norm_fwd", grid=(s // tr,),
        in_specs=[pl.BlockSpec((tr, D_MODEL), lambda i: (i, 0)), vec, vec, vec],
        out_specs=[pl.BlockSpec((tr, D_MODEL), lambda i: (i, 0)), pl.BlockSpec((D_MODEL, tr), lambda i: (0, i))],
        out_shape=[SDS((s, D_MODEL), BF16), SDS((D_MODEL, s), BF16)], compiler_params=_params(),
    )(x, norm_w, scale, shift)


def _mm_in(h, wt):
    s = h.shape[0]
    tm = 1024

    def body(h_ref, w_ref, o_ref):
        o_ref[...] = _dot_nt(h_ref[...], w_ref[...])

    return pl.pallas_call(
        body, name="mm_in", grid=(IN_W // PAIR_W, s // tm),
        in_specs=[pl.BlockSpec((tm, D_MODEL), lambda p, m: (m, 0)),
                  pl.BlockSpec((PAIR_W, D_MODEL), lambda p, m: (p, 0))],
        out_specs=pl.BlockSpec((tm, PAIR_W), lambda p, m: (m, p)),
        out_shape=SDS((s, IN_W), F32), compiler_params=_params(),
    )(h, wt)


def _head_ones():
    a = lax.broadcasted_iota(jnp.int32, (LANE, LANE), 0) // HEAD_DIM
    b = lax.broadcasted_iota(jnp.int32, (LANE, LANE), 1) // HEAD_DIM
    return (a == b).astype(BF16)


def _head_sums(t, ones):
    return _dot(t.astype(BF16), ones)


def _band_bias(bias, transposed=False):
    qi = lax.broadcasted_iota(jnp.int32, (2 * BAND, 2 * BAND), 1 if transposed else 0) % BAND
    kj = lax.broadcasted_iota(jnp.int32, (2 * BAND, 2 * BAND), 0 if transposed else 1)
    dist = qi + BAND - kj
    valid = (dist >= 0) & (dist <= BAND)
    bias[1] = jnp.where(valid, 0.0, -1e30)
    bias[0] = jnp.where(valid & (kj >= BAND), 0.0, -1e30)


def _token_rows(j, d, chunk, per_r):
    return pl.ds(j // per_r + (j % per_r) * (chunk * d), chunk, stride=d)


def _deinterleave_many(jobs, ones, d, sub_len, chunk, unroll):
    per_r = sub_len // chunk

    def step(j, _):
        tok = _token_rows(j, d, chunk, per_r)
        for src_ref, dst_ref, w_ref, scale, dst_off in jobs:
            t = src_ref[tok, :]
            if w_ref is not None:
                ms = _head_sums(t * t, ones) * (1.0 / HEAD_DIM)
                t = t * lax.rsqrt(ms + EPS) * (w_ref[...] * scale)
            dst_ref[pl.ds(pl.multiple_of(dst_off + j * chunk, BAND), chunk), :] = t.astype(dst_ref.dtype)
        return 0
    lax.fori_loop(0, d * per_r, step, 0, unroll=unroll)


def _deinterleave(src_ref, dst_ref, w_ref, ones, d, sub_len, chunk, scale, dst_off):
    _deinterleave_many([(src_ref, dst_ref, w_ref, scale, dst_off)], ones, d, sub_len, chunk, 4)


N_PAIRS = ATTN_W // LANE


def _attn_fwd(proj, qw2, kw2):
    s = proj.shape[0]

    def group_body(g, step, q_ref, k_ref, v_ref, qw_ref, kw_ref, o_ref, l_ref, qd, kd, vd, od, ld, bias):
        d = DILATIONS[g]
        sub_len = s // d
        nb = sub_len // BAND
        chunk = min(sub_len, 256)
        lo = lax.broadcasted_iota(jnp.int32, (1, LANE), 1) < HEAD_DIM
        ones = _head_ones()

        @pl.when(step == 0)
        def _():
            _band_bias(bias)

        kd[0:BAND, :] = jnp.zeros((BAND, LANE), BF16)
        vd[0:BAND, :] = jnp.zeros((BAND, LANE), BF16)
        _deinterleave_many([(q_ref, qd, qw_ref, HEAD_DIM ** -0.5, 0), (k_ref, kd, kw_ref, 1.0, BAND),
                            (v_ref, vd, None, 1.0, BAND)], ones, d, sub_len, chunk, 4)

        def block(t, _):
            base = pl.multiple_of(t * BAND, BAND)
            q = qd[pl.ds(base, BAND), :]
            k2 = kd[pl.ds(base, 2 * BAND), :]
            v2 = vd[pl.ds(base, 2 * BAND), :]
            zero = jnp.zeros_like(q)
            qs = jnp.concatenate([jnp.where(lo, q, zero), jnp.where(lo, zero, q)], axis=0)
            sc = _dot_nt(qs, k2) + bias[jnp.minimum(t % nb, 1)]
            m = jnp.max(sc, axis=-1, keepdims=True)
            p = jnp.exp(sc - m)
            den = jnp.sum(p, axis=-1, keepdims=True)
            u = _dot(p.astype(BF16), v2) * (1.0 / den)
            lse = m + jnp.log(den)
            od[pl.ds(base, BAND), :] = jnp.where(lo, u[:BAND], u[BAND:])
            ld[pl.ds(base, BAND), :] = jnp.where(lo, lse[:BAND], lse[BAND:])
            return 0
        lax.fori_loop(0, s // BAND, block, 0, unroll=16)

        per_r = sub_len // chunk

        def back(j, _):
            src = pl.ds(pl.multiple_of(j * chunk, chunk), chunk)
            dst = _token_rows(j, d, chunk, per_r)
            o_ref[dst, :] = od[src, :]
            l_ref[dst, :] = ld[src, :]
            return 0
        lax.fori_loop(0, d * per_r, back, 0, unroll=2)

    def body(*refs):
        step = pl.program_id(0)
        for g in range(N_GROUPS):
            pl.when(step // N_PAIRS == g)(functools.partial(group_body, g, step, *refs))

    col = lambda off: pl.BlockSpec((s, LANE), lambda i, off=off: (0, off // LANE + i))
    vec = pl.BlockSpec((1, LANE), lambda i: (0, 0))
    out = pl.BlockSpec((s, LANE), lambda i: (0, i))
    width = N_GROUPS * ATTN_W
    return pl.pallas_call(
        body, name="attn_fwd", grid=(N_GROUPS * N_PAIRS,),
        in_specs=[col(Q0), col(K0), col(V0), vec, vec],
        out_specs=[out, out, out, pl.BlockSpec((s + BAND, LANE), lambda i: (0, i)),
                   pl.BlockSpec((s + BAND, LANE), lambda i: (0, i))],
        out_shape=[SDS((s, width), F32)] * 2 + [SDS((s, width), BF16)] + [SDS((s + BAND, width), BF16)] * 2,
        scratch_shapes=[pltpu.VMEM((s, LANE), F32), pltpu.VMEM((s, LANE), F32),
                        pltpu.VMEM((2, 2 * BAND, 2 * BAND), F32)],
        compiler_params=_params(),
    )(proj, proj, proj, qw2, kw2)


def _attn_bwd(proj, qn, kn, vn, da, lse_delta, qw2, kw2, dproj):
    s = proj.shape[0]
    n_steps = N_GROUPS * N_PAIRS

    def group_body(g, hp, q_ref, k_ref, qn_ref, kd, vd, da_ref, ld_ref, qw_ref, kw_ref, dp_in, dp_out,
                   dqw_ref, dkw_ref, kdt, dad, lst, dlt, dqt, dqd, dkd, dvd, st, st_k, stb, bias_t, wacc, sem):
        del dp_in
        d = DILATIONS[g]
        sub_len = s // d
        nb = sub_len // BAND
        chunk = min(sub_len, 256)
        lo = lax.broadcasted_iota(jnp.int32, (1, LANE), 1) < HEAD_DIM
        row_lo = lax.broadcasted_iota(jnp.int32, (LANE, 1), 0) < HEAD_DIM
        ones = _head_ones()
        per_r = sub_len // chunk
        cblk = chunk // BAND

        @pl.when(hp == 0)
        def _():
            _band_bias(bias_t, transposed=True)

        kdt[0] = jnp.zeros((LANE, BAND), BF16)

        def k_step(t, _):
            kdt[1 + t] = kd[pl.ds(pl.multiple_of(BAND + t * BAND, BAND), BAND), :].astype(F32).T.astype(BF16)
            return 0
        lax.fori_loop(0, s // BAND, k_step, 0, unroll=4)
        _deinterleave(da_ref, dad, None, ones, d, sub_len, chunk, 1.0, 0)

        def rows_step(j, _):
            tok = _token_rows(j, d, chunk, per_r)
            tt = ld_ref[tok, :].T
            for u in range(cblk):
                cols = slice(u * BAND, (u + 1) * BAND)
                lst[j * cblk + u, 0:1, :] = tt[0:1, cols]
                lst[j * cblk + u, 1:2, :] = tt[HEAD_DIM:HEAD_DIM + 1, cols]
                dlt[j * cblk + u, 0:1, :] = tt[HEAD_DIM // 2:HEAD_DIM // 2 + 1, cols]
                dlt[j * cblk + u, 1:2, :] = tt[HEAD_DIM + HEAD_DIM // 2:HEAD_DIM + HEAD_DIM // 2 + 1, cols]
            return 0
        lax.fori_loop(0, d * per_r, rows_step, 0, unroll=4)

        def block(t, carry):
            ck, cv = carry
            base = pl.multiple_of(t * BAND, BAND)
            q = qn_ref[pl.ds(base, BAND), :]
            k2 = kd[pl.ds(base, 2 * BAND), :]
            v2 = vd[pl.ds(base, 2 * BAND), :]
            k2t = jnp.concatenate([kdt[t], kdt[t + 1]], axis=1)
            dav = dad[pl.ds(base, BAND), :]
            zero = jnp.zeros_like(q)
            qs = jnp.concatenate([jnp.where(lo, q, zero), jnp.where(lo, zero, q)], axis=0)
            das = jnp.concatenate([jnp.where(lo, dav, zero), jnp.where(lo, zero, dav)], axis=0)
            ls_row = jnp.concatenate([lst[t, 0:1, :], lst[t, 1:2, :]], axis=1)
            dl_row = jnp.concatenate([dlt[t, 0:1, :], dlt[t, 1:2, :]], axis=1)
            sc_t = _dot_nt(k2, qs) + bias_t[jnp.minimum(t % nb, 1)]
            p_t = jnp.exp(sc_t - ls_row)
            dp_t = _dot_nt(v2, das)
            ds_t = (p_t * (dp_t - dl_row)).astype(BF16)
            dv2 = _dot(p_t.astype(BF16), das)
            dk2 = _dot(ds_t, qs)
            dvd[pl.ds(base, BAND), :] = cv + dv2[:BAND]
            dkd[pl.ds(base, BAND), :] = ck + dk2[:BAND]
            dq_t = _dot(k2t, ds_t)
            dqt[t] = jnp.where(row_lo, dq_t[:, :BAND], dq_t[:, BAND:])
            return dk2[BAND:], dv2[BAND:]

        def blocks(i, carry):
            for u in range(BWD_UNROLL):
                carry = block(i * BWD_UNROLL + u, carry)
            return carry
        zeros = jnp.zeros((BAND, LANE), F32)
        ck, cv = lax.fori_loop(0, s // (BAND * BWD_UNROLL), blocks, (zeros, zeros))
        dkd[s:s + BAND, :] = ck
        dvd[s:s + BAND, :] = cv

        def dq_rows(t, _):
            dqd[pl.ds(pl.multiple_of(t * BAND, BAND), BAND), :] = dqt[t].T
            return 0
        lax.fori_loop(0, s // BAND, dq_rows, 0, unroll=4)

        def col_copy(slot, col0):
            return pltpu.make_async_copy(
                stb.at[slot], dp_out.at[:, pl.ds(pl.multiple_of(col0 + LANE * hp, LANE), LANE)], sem.at[slot])

        def store_cols(slot, col0, src):
            @pl.when(hp > 0)
            def _():
                col_copy(slot, col0).wait()
            stb[slot] = src[...].astype(BF16)
            col_copy(slot, col0).start()

        sides = ((q_ref, dqd, 0, qw_ref, HEAD_DIM ** -0.5, st), (k_ref, dkd, BAND, kw_ref, 1.0, st_k))
        wacc[...] = jnp.zeros_like(wacc)

        def norm_step(j, _):
            tok = _token_rows(j, d, chunk, per_r)
            for i, (src_ref, dy_ref, dy_off, w_ref, scale, dst) in enumerate(sides):
                t = src_ref[tok, :]
                dy = dy_ref[pl.ds(pl.multiple_of(dy_off + j * chunk, BAND), chunk), :]
                rr = lax.rsqrt(_head_sums(t * t, ones) * (1.0 / HEAD_DIM) + EPS)
                nrm = t * rr
                wacc[i] += jnp.sum((dy * nrm).reshape(chunk // 8, 8, LANE), axis=0)
                dn = dy * (w_ref[...] * scale)
                dst[tok, :] = rr * (dn - nrm * (_head_sums(dn * nrm, ones) * (1.0 / HEAD_DIM)))
            return 0
        lax.fori_loop(0, d * per_r, norm_step, 0, unroll=4)

        @pl.when(hp == 0)
        def _():
            dqw_ref[...] = jnp.zeros_like(dqw_ref)
            dkw_ref[...] = jnp.zeros_like(dkw_ref)

        for i, dw_ref in enumerate((dqw_ref, dkw_ref)):
            dw_ref[...] += jnp.broadcast_to(jnp.sum(wacc[i], axis=0, keepdims=True) * sides[i][4], dw_ref.shape)
        store_cols(0, Q0, st)
        store_cols(1, K0, st_k)

        def v_back(j, _):
            src = pl.ds(pl.multiple_of(BAND + j * chunk, BAND), chunk)
            st[_token_rows(j, d, chunk, per_r), :] = dvd[src, :]
            return 0
        lax.fori_loop(0, d * per_r, v_back, 0, unroll=2)
        store_cols(2, V0, st)

        @pl.when(hp == n_steps - 1)
        def _():
            for slot, col0 in enumerate((Q0, K0, V0)):
                col_copy(slot, col0).wait()

    def body(*refs):
        step = pl.program_id(0)
        for g in range(N_GROUPS):
            pl.when(step // N_PAIRS == g)(functools.partial(group_body, g, step, *refs))

    col = lambda off: pl.BlockSpec((s, LANE), lambda i, off=off: (0, off // LANE + i))
    mid = pl.BlockSpec((s, LANE), lambda i: (0, i))
    padded = pl.BlockSpec((s + BAND, LANE), lambda i: (0, i))
    slot4 = pl.BlockSpec((s, LANE), lambda i: (0, i % N_PAIRS))
    vec = pl.BlockSpec((1, LANE), lambda i: (0, 0))
    acc = pl.BlockSpec((8, LANE), lambda i: (0, 0))
    any_ = pl.BlockSpec(memory_space=pl.ANY)
    return pl.pallas_call(
        body, name="attn_bwd", grid=(n_steps,),
        in_specs=[col(Q0), col(K0), mid, padded, padded, slot4, slot4, vec, vec, any_],
        out_specs=[any_, acc, acc],
        out_shape=[SDS(dproj.shape, dproj.dtype), SDS((8, LANE), F32), SDS((8, LANE), F32)],
        input_output_aliases={9: 0},
        scratch_shapes=[pltpu.VMEM((s // BAND + 1, LANE, BAND), BF16), pltpu.VMEM((s, LANE), BF16),
                        pltpu.VMEM((s // BAND, 8, BAND), F32), pltpu.VMEM((s // BAND, 8, BAND), F32),
                        pltpu.VMEM((s // BAND, LANE, BAND), F32),
                        pltpu.VMEM((s, LANE), F32), pltpu.VMEM((s + BAND, LANE), F32), pltpu.VMEM((s + BAND, LANE), F32),
                        pltpu.VMEM((s, LANE), F32), pltpu.VMEM((s, LANE), F32), pltpu.VMEM((3, s, LANE), BF16),
                        pltpu.VMEM((2, 2 * BAND, 2 * BAND), F32), pltpu.VMEM((2, 8, LANE), F32),
                        pltpu.SemaphoreType.DMA((3,))],
        compiler_params=_params(),
    )(proj, proj, qn, kn, vn, da, lse_delta, qw2, kw2, dproj)


def _tap_views(ext_ref, sh_ref, offsets, tr, cols):
    for b in range(8):
        group = [j for j, o in enumerate(offsets) if o % 8 == b]
        if not group:
            continue
        first = min(offsets[j] for j in group)
        span = tr + max(offsets[j] for j in group) - first
        sh_ref[0:span, cols] = ext_ref[first:first + span, cols]
        for j in group:
            yield j, sh_ref[offsets[j] - first:offsets[j] - first + tr, cols]


def _silu_grad(z, sg):
    return sg * (1.0 + z * (1.0 - sg))


def _glu(u):
    a_h, b_h = u[:, :CONV_W], u[:, CONV_W:]
    sg = _sigmoid(b_h)
    return a_h, sg, a_h * sg


def _tail(x, tgt, proj, o3, l3, wa, wc, wo, gate, bga, bgc, convw, convb, lnw, lnb, bd):
    s = x.shape[0]
    tr = 256

    def body(x_ref, t_ref, za_ref, u_ref, uh_ref, zc_ref, g0_ref, g1_ref, g2_ref, g3_ref,
             o0_ref, o1_ref, o2_ref, l0_ref, l1_ref, l2_ref, wa_ref, wc_ref, wo_ref,
             gate_ref, bga_ref, bgc_ref, cw_ref, cb_ref, lnw_ref, lnb_ref, bd_ref,
             dout_ref, da_ref, ld_ref, dcv_ref, mt_ref, yat_ref, yct_ref, dmo_ref, dya_ref, dyc_ref, dp_ref,
             dgate_ref, dbg_ref, dlnw_ref, dlnb_ref, dcb_ref, loss_ref,
             ext, sh, st_za, st_zc, st_g, sems):
        i = pl.program_id(0)

        @pl.when(i == 0)
        def _():
            for r in (dgate_ref, dbg_ref, dlnw_ref, dlnb_ref, dcb_ref, loss_ref):
                r[...] = jnp.zeros_like(r)

        def acc_rows(ref, v):
            ref[...] += jnp.broadcast_to(jnp.sum(v, axis=0, keepdims=True), ref.shape)

        la, lb, lc = l0_ref[...], l1_ref[...], l2_ref[...]
        mx = jnp.maximum(jnp.maximum(la, lb), lc)
        ea, eb, ec = jnp.exp(la - mx), jnp.exp(lb - mx), jnp.exp(lc - mx)
        den = ea + eb + ec
        inv = 1.0 / den
        attn = (ea * inv) * o0_ref[...] + (eb * inv) * o1_ref[...] + (ec * inv) * o2_ref[...]
        lse = mx + jnp.log(den)

        za = za_ref[...]
        sga = _sigmoid(za)
        sa = za * sga
        ya_in = attn * sa
        y_attn = _dot(ya_in.astype(BF16), wa_ref[...])

        _, _, glu = _glu(u_ref[...])
        _, _, glu_h = _glu(uh_ref[...])
        ext[0:CONV_HALO, :] = jnp.where(i > 0, glu_h, 0.0)
        ext[CONV_HALO:CONV_HALO + tr, :] = glu
        cv_blocks = []
        for cb in range(CONV_W // LANE):
            cols = slice(cb * LANE, (cb + 1) * LANE)
            cv_c = jnp.broadcast_to(cb_ref[:, cols], (tr, LANE))
            for j, rows in _tap_views(ext, sh, [CONV_HALO - (CONV_K - 1) + j for j in range(CONV_K)], tr, cols):
                cv_c = cv_c + cw_ref[j:j + 1, cols] * rows
            cv_blocks.append(cv_c)
        cv = jnp.concatenate(cv_blocks, axis=1)
        mu = jnp.mean(cv, axis=-1, keepdims=True)
        xc = cv - mu
        rstd = lax.rsqrt(jnp.mean(xc * xc, axis=-1, keepdims=True) + EPS)
        nrm = xc * rstd
        ln = nrm * lnw_ref[...] + lnb_ref[...]
        sgl = _sigmoid(ln)
        cs = ln * sgl
        zc = zc_ref[...]
        sgc = _sigmoid(zc)
        scz = zc * sgc
        yc_in = cs * scz
        y_conv = _dot(yc_in.astype(BF16), wc_ref[...])

        ga = _sigmoid(jnp.concatenate([g0_ref[...], g1_ref[...]], axis=1) + bga_ref[...])
        gc = _sigmoid(jnp.concatenate([g2_ref[...], g3_ref[...]], axis=1) + bgc_ref[...])
        merged = ga * y_attn + gc * y_conv
        mo = _dot(merged.astype(BF16), wo_ref[...])
        gate_v = gate_ref[...]
        err = (x_ref[...] + gate_v * mo) - t_ref[...]
        loss_ref[...] += 0.5 * jnp.sum(jnp.mean(err * err, axis=-1, keepdims=True))
        d_out = err * (1.0 / D_MODEL)
        dout_ref[...] = d_out

        rows = pl.ds(pl.multiple_of(i * tr, tr), tr)
        cps = [pltpu.make_async_copy(st_za, dp_ref.at[rows, pl.ds(ZA0, ATTN_W)], sems.at[0]),
               pltpu.make_async_copy(st_zc, dp_ref.at[rows, pl.ds(ZC0, CONV_W)], sems.at[1]),
               pltpu.make_async_copy(st_g, dp_ref.at[rows, pl.ds(G0, 2 * D_MODEL)], sems.at[2])]

        @pl.when(i > 0)
        def _():
            for cp in cps:
                cp.wait()

        acc_rows(dgate_ref, d_out * mo)
        dmo_b = (d_out * gate_v).astype(BF16)
        dmo_ref[...] = dmo_b
        mt_ref[...] = merged.T.astype(BF16)
        d_merged = _dot_nt(dmo_b, wo_ref[...])
        d_ya = (d_merged * ga).astype(BF16)
        d_yc = (d_merged * gc).astype(BF16)
        dya_ref[...] = d_ya
        dyc_ref[...] = d_yc
        dga = d_merged * y_attn * (ga * (1.0 - ga))
        dgc = d_merged * y_conv * (gc * (1.0 - gc))
        dgs = jnp.concatenate([dga, dgc], axis=1)
        acc_rows(dbg_ref, dgs)
        st_g[...] = dgs.astype(BF16)

        yat_ref[...] = ya_in.T.astype(BF16)
        d_ya_in = _dot_nt(d_ya, wa_ref[...])
        d_attn = d_ya_in * sa
        da_ref[...] = d_attn
        st_za[...] = (d_ya_in * attn * _silu_grad(za, sga)).astype(BF16)
        prod = d_attn * attn
        hi = prod.astype(BF16)
        lo_ = (prod - hi.astype(F32)).astype(BF16)
        delta = _dot(hi, bd_ref[...]) + _dot(lo_, bd_ref[...])
        first_half = (lax.broadcasted_iota(jnp.int32, (1, ATTN_W), 1) % HEAD_DIM) < HEAD_DIM // 2
        ld_ref[...] = jnp.where(first_half, lse, delta)

        yct_ref[...] = yc_in.T.astype(BF16)
        d_yc_in = _dot_nt(d_yc, wc_ref[...])
        st_zc[...] = (d_yc_in * cs * _silu_grad(zc, sgc)).astype(BF16)
        d_ln = (d_yc_in * scz) * _silu_grad(ln, sgl)
        acc_rows(dlnw_ref, d_ln * nrm)
        acc_rows(dlnb_ref, d_ln)
        d_nrm = d_ln * lnw_ref[...]
        d_cv = rstd * (d_nrm - jnp.mean(d_nrm, axis=-1, keepdims=True)
                       - nrm * jnp.mean(d_nrm * nrm, axis=-1, keepdims=True))
        acc_rows(dcb_ref, d_cv)
        dcv_ref[...] = d_cv

        for cp in cps:
            cp.start()

        @pl.when(i == s // tr - 1)
        def _():
            for cp in cps:
                cp.wait()

    def rows(width, colblk=0):
        return pl.BlockSpec((tr, width), lambda i, colblk=colblk: (i, colblk))

    def const(shape):
        return pl.BlockSpec(shape, lambda i: (0,) * len(shape))

    halo = pl.BlockSpec((CONV_HALO, D_MODEL), lambda i: (jnp.maximum(i * (tr // CONV_HALO) - 1, 0), U0 // D_MODEL))
    in_specs = [rows(D_MODEL), rows(D_MODEL), rows(ATTN_W, ZA0 // ATTN_W), rows(D_MODEL, U0 // D_MODEL), halo,
                rows(CONV_W, ZC0 // CONV_W)]
    in_specs += [rows(512, G0 // 512 + j) for j in range(4)]
    in_specs += [rows(ATTN_W, g) for g in range(N_GROUPS)] * 2
    in_specs += [const(wa.shape), const(wc.shape), const(wo.shape), const((1, D_MODEL)), const((1, D_MODEL)),
                 const((1, D_MODEL)), const(convw.shape), const((1, CONV_W)), const((1, CONV_W)), const((1, CONV_W)),
                 const(bd.shape)]
    tcol = lambda width: pl.BlockSpec((width, tr), lambda i: (0, i))
    out_specs = [rows(D_MODEL), rows(ATTN_W), rows(ATTN_W), rows(CONV_W),
                 tcol(D_MODEL), tcol(ATTN_W), tcol(CONV_W), rows(D_MODEL), rows(D_MODEL), rows(D_MODEL),
                 pl.BlockSpec(memory_space=pl.ANY),
                 const((8, D_MODEL)), const((8, 2 * D_MODEL)), const((8, CONV_W)), const((8, CONV_W)), const((8, CONV_W)),
                 const((8, LANE))]
    out_shape = [SDS((s, D_MODEL), F32), SDS((s, ATTN_W), F32), SDS((s, ATTN_W), F32),
                 SDS((s, CONV_W), F32),
                 SDS((D_MODEL, s), BF16), SDS((ATTN_W, s), BF16), SDS((CONV_W, s), BF16),
                 SDS((s, D_MODEL), BF16), SDS((s, D_MODEL), BF16), SDS((s, D_MODEL), BF16),
                 SDS((s, IN_W), BF16),
                 SDS((8, D_MODEL), F32), SDS((8, 2 * D_MODEL), F32), SDS((8, CONV_W), F32), SDS((8, CONV_W), F32),
                 SDS((8, CONV_W), F32), SDS((8, LANE), F32)]
    return pl.pallas_call(
        body, name="tail", grid=(s // tr,), in_specs=in_specs, out_specs=out_specs, out_shape=out_shape,
        scratch_shapes=[pltpu.VMEM((CONV_HALO + tr, CONV_W), F32), pltpu.VMEM((CONV_HALO + tr, CONV_W), F32),
                        pltpu.VMEM((tr, ATTN_W), BF16),
                        pltpu.VMEM((tr, CONV_W), BF16), pltpu.VMEM((tr, 2 * D_MODEL), BF16),
                        pltpu.SemaphoreType.DMA((3,))],
        compiler_params=_params(),
    )(x, tgt, proj, proj, proj, proj, proj, proj, proj, proj, *o3, *l3, wa, wc, wo, gate, bga, bgc,
      convw, convb, lnw, lnb, bd)


def _conv_bwd(dcv, proj, convw, dproj):
    s = dcv.shape[0]
    tr = 128
    nt = s // tr

    def body(dcv_ref, dcvn_ref, u_ref, uh_ref, cw_ref, dp_in, dp_out, dw_ref, extg, extd, sh):
        del dp_in
        i = pl.program_id(0)

        @pl.when(i == 0)
        def _():
            dw_ref[...] = jnp.zeros_like(dw_ref)

        _, _, glu = _glu(u_ref[...])
        _, _, glu_h = _glu(uh_ref[...])
        extg[0:CONV_HALO, :] = jnp.where(i > 0, glu_h, 0.0)
        extg[CONV_HALO:CONV_HALO + tr, :] = glu
        extd[0:tr, :] = dcv_ref[...]
        extd[tr:tr + CONV_HALO, :] = jnp.where(i < nt - 1, dcvn_ref[...], 0.0)
        for cb in range(CONV_W // LANE):
            cols = slice(cb * LANE, (cb + 1) * LANE)
            dglu = jnp.zeros((tr, LANE), F32)
            for j, rows in _tap_views(extd, sh, [CONV_K - 1 - j for j in range(CONV_K)], tr, cols):
                dglu = dglu + cw_ref[j:j + 1, cols] * rows
            dcv_c = dcv_ref[:, cols]
            for j, rows in _tap_views(extg, sh, [CONV_HALO - (CONV_K - 1) + j for j in range(CONV_K)], tr, cols):
                dw_ref[8 * j:8 * j + 8, cols] += jnp.sum((dcv_c * rows).reshape(tr // 8, 8, LANE), axis=0)
            a_h = u_ref[:, cols]
            sgb = _sigmoid(u_ref[:, CONV_W + cb * LANE:CONV_W + (cb + 1) * LANE])
            dp_out[:, cols] = (dglu * sgb).astype(BF16)
            dp_out[:, CONV_W + cb * LANE:CONV_W + (cb + 1) * LANE] = (dglu * a_h * (sgb * (1.0 - sgb))).astype(BF16)

    ucol = U0 // D_MODEL
    return pl.pallas_call(
        body, name="conv_bwd", grid=(nt,),
        in_specs=[pl.BlockSpec((tr, CONV_W), lambda i: (i, 0)),
                  pl.BlockSpec((CONV_HALO, CONV_W), lambda i: (jnp.minimum((i + 1) * (tr // CONV_HALO), s // CONV_HALO - 1), 0)),
                  pl.BlockSpec((tr, D_MODEL), lambda i: (i, ucol)),
                  pl.BlockSpec((CONV_HALO, D_MODEL), lambda i: (jnp.maximum(i * (tr // CONV_HALO) - 1, 0), ucol)),
                  pl.BlockSpec(convw.shape, lambda i: (0, 0)),
                  pl.BlockSpec(memory_space=pl.ANY)],
        out_specs=[pl.BlockSpec((tr, D_MODEL), lambda i: (i, ucol)), pl.BlockSpec((8 * CONV_HALO, CONV_W), lambda i: (0, 0))],
        out_shape=[SDS(dproj.shape, dproj.dtype), SDS((8 * CONV_HALO, CONV_W), F32)],
        input_output_aliases={5: 0},
        scratch_shapes=[pltpu.VMEM((CONV_HALO + tr, CONV_W), F32)] * 3,
        compiler_params=_params(),
    )(dcv, dcv, proj, proj, convw, dproj)


def _mm_acc(at, b, name, col_slots):
    m, s = at.shape
    n = b.shape[1]
    tk = 1024
    nk = s // tk

    def body(a_ref, b_ref, o_ref, acc):
        k = pl.program_id(0)

        @pl.when(k == 0)
        def _():
            acc[...] = jnp.zeros_like(acc)

        acc[...] += _dot(a_ref[...], b_ref[...])

        @pl.when(k == nk - 1)
        def _():
            if col_slots:
                w = n // N_DEV
                for j in range(N_DEV):
                    o_ref[j] = acc[:, j * w:(j + 1) * w].astype(BF16)
            else:
                o_ref[...] = acc[...].astype(BF16)

    if col_slots:
        out_shape = SDS((N_DEV, m, n // N_DEV), BF16)
        out_spec = pl.BlockSpec((N_DEV, m, n // N_DEV), lambda k: (0, 0, 0))
    else:
        out_shape = SDS((m, n), BF16)
        out_spec = pl.BlockSpec((m, n), lambda k: (0, 0))
    return pl.pallas_call(
        body, name=name, grid=(nk,),
        in_specs=[pl.BlockSpec((m, tk), lambda k: (0, k)), pl.BlockSpec((tk, n), lambda k: (k, 0))],
        out_specs=out_spec, out_shape=out_shape, scratch_shapes=[pltpu.VMEM((m, n), F32)],
        compiler_params=_params(),
    )(at, b)


def _mm_dw(ht, dproj):
    s = ht.shape[1]
    tk = 1024
    nk = s // tk

    def body(a_ref, b_ref, o_ref, acc):
        k = pl.program_id(1)

        @pl.when(k == 0)
        def _():
            acc[...] = jnp.zeros_like(acc)

        acc[...] += _dot(a_ref[...], b_ref[...])

        @pl.when(k == nk - 1)
        def _():
            o_ref[...] = acc[...].T.astype(BF16)

    return pl.pallas_call(
        body, name="mm_dw", grid=(IN_W // PAIR_W, nk),
        in_specs=[pl.BlockSpec((D_MODEL, tk), lambda p, k: (0, k)), pl.BlockSpec((tk, PAIR_W), lambda p, k: (k, p))],
        out_specs=pl.BlockSpec((PAIR_W, D_MODEL), lambda p, k: (p, 0)),
        out_shape=SDS((IN_W, D_MODEL), BF16), scratch_shapes=[pltpu.VMEM((D_MODEL, PAIR_W), F32)],
        compiler_params=_params(),
    )(ht, dproj)


def _mm_dh_norm_bwd(dproj, wt, x, dout, norm_w, scale, token):
    s = dproj.shape[0]
    tm = 1024
    n_p = IN_W // PAIR_W

    def body(dp_ref, w_ref, x_ref, do_ref, nw_ref, sc_ref, tok_ref, gx_ref, dsh_ref, dsc_ref, dnw_ref, dh_acc):
        del tok_ref
        m, p = pl.program_id(0), pl.program_id(1)
        part = _dot(dp_ref[...], w_ref[...])

        @pl.when(p == 0)
        def _():
            dh_acc[...] = part

        @pl.when(p > 0)
        def _():
            dh_acc[...] += part

        @pl.when((m == 0) & (p == 0))
        def _():
            for r in (dsh_ref, dsc_ref, dnw_ref):
                r[...] = jnp.zeros_like(r)

        @pl.when(p == n_p - 1)
        def _():
            def acc_rows(ref, v):
                ref[...] += jnp.broadcast_to(jnp.sum(v, axis=0, keepdims=True), ref.shape)

            xv = x_ref[...]
            dh_v = dh_acc[...]
            r = lax.rsqrt(jnp.mean(xv * xv, axis=-1, keepdims=True) + EPS)
            xn = xv * r
            one_sc = 1.0 + sc_ref[...]
            acc_rows(dsh_ref, dh_v)
            acc_rows(dsc_ref, dh_v * (xn * nw_ref[...]))
            acc_rows(dnw_ref, dh_v * xn * one_sc)
            dxn = dh_v * (nw_ref[...] * one_sc)
            gx_ref[...] = do_ref[...] + r * (dxn - xn * jnp.mean(dxn * xn, axis=-1, keepdims=True))

    rows = pl.BlockSpec((tm, D_MODEL), lambda m, p: (m, 0))
    vec = pl.BlockSpec((1, D_MODEL), lambda m, p: (0, 0))
    acc = pl.BlockSpec((8, D_MODEL), lambda m, p: (0, 0))
    return pl.pallas_call(
        body, name="mm_dh_norm_bwd", grid=(s // tm, n_p),
        in_specs=[pl.BlockSpec((tm, PAIR_W), lambda m, p: (m, p)),
                  pl.BlockSpec((PAIR_W, D_MODEL), lambda m, p: (p, 0)),
                  rows, rows, vec, vec, pl.BlockSpec(token.shape, lambda m, p: (0, 0))],
        out_specs=[rows, acc, acc, acc],
        out_shape=[SDS((s, D_MODEL), F32)] + [SDS((8, D_MODEL), F32)] * 3,
        scratch_shapes=[pltpu.VMEM((tm, D_MODEL), F32)], compiler_params=_params(),
    )(dproj, wt, x, dout, norm_w, scale, token)


SMALL_ROWS = 8
QN_COL, KN_COL, CB_COL, LOSS_COL = 0, LANE, 2 * LANE, 2 * LANE + CONV_W


def _pack_partials(dsh, dsc, dgate, dnw, dbg, dqw3, dkw3, dcb, dlnw, dlnb, loss_p):
    n3 = len(dqw3)

    def body(*refs):
        dsh_r, dsc_r, dgate_r, dnw_r, dbg_r = refs[:5]
        dq_r, dk_r = refs[5:5 + n3], refs[5 + n3:5 + 2 * n3]
        dcb_r, dlnw_r, dlnb_r, loss_r, o_ref = refs[5 + 2 * n3:]

        def both_heads(rs):
            t = rs[0][0:1, :]
            for r in rs[1:]:
                t = t + r[0:1, :]
            return t + pltpu.roll(t, HEAD_DIM, axis=1)

        o_ref[0:1, :] = dsh_r[0:1, :]
        o_ref[1:2, :] = dsc_r[0:1, :]
        o_ref[2:3, :] = dgate_r[0:1, :]
        o_ref[3:4, :] = dnw_r[0:1, :]
        o_ref[4:5, :] = dbg_r[0:1, 0:D_MODEL]
        o_ref[5:6, :] = dbg_r[0:1, D_MODEL:]
        o_ref[6:7, QN_COL:QN_COL + LANE] = both_heads(dq_r)
        o_ref[6:7, KN_COL:KN_COL + LANE] = both_heads(dk_r)
        o_ref[6:7, CB_COL:CB_COL + CONV_W] = dcb_r[0:1, :]
        o_ref[6:7, LOSS_COL:LOSS_COL + LANE] = loss_r[0:1, :]
        o_ref[6:7, LOSS_COL + LANE:] = jnp.zeros((1, D_MODEL - LOSS_COL - LANE), F32)
        o_ref[7:8, 0:CONV_W] = dlnw_r[0:1, :]
        o_ref[7:8, CONV_W:] = dlnb_r[0:1, :]

    return pl.pallas_call(body, name="pack_partials", out_shape=SDS((SMALL_ROWS, D_MODEL), F32),
                          compiler_params=_params())(dsh, dsc, dgate, dnw, dbg, *dqw3, *dkw3, dcb, dlnw, dlnb, loss_p)


def _adamw_update(g, w, m, v):
    bc1 = 1.0 - ADAM_B1 ** ADAM_STEP
    bc2 = 1.0 - ADAM_B2 ** ADAM_STEP
    m_new = ADAM_B1 * m + (1.0 - ADAM_B1) * g
    v_new = ADAM_B2 * v + (1.0 - ADAM_B2) * (g * g)
    delta = -ADAM_LR * ((m_new / bc1) / (jnp.sqrt(v_new / bc2) + ADAM_EPS) + ADAM_WD * w)
    return delta, m_new, v_new


def _adamw_small(small_all, ws, ms, vs):
    n = len(ws)
    where = [(slice(0, 3), None), (slice(3, 4), None), (slice(4, 6), None), (6, QN_COL), (6, KN_COL), (6, CB_COL),
             (7, 0), (7, CONV_W)]

    def body(*refs):
        g_ref = refs[0]
        w_r, m_r, v_r = refs[1:1 + n], refs[1 + n:1 + 2 * n], refs[1 + 2 * n:1 + 3 * n]
        outs = refs[1 + 3 * n:]
        g_o, d_o, m_o, v_o, loss_o = outs[:n], outs[n:2 * n], outs[2 * n:3 * n], outs[3 * n:4 * n], outs[4 * n]
        gsum = g_ref[0]
        for dev in range(1, N_DEV):
            gsum = gsum + g_ref[dev]
        loss_o[...] = gsum[6:7, LOSS_COL:LOSS_COL + LANE]
        for i, (rows, col) in enumerate(where):
            width = w_r[i].shape[1]
            if col is None:
                g = jnp.concatenate([gsum[r:r + 1, :] for r in range(rows.start, rows.stop)], axis=1)
            else:
                g = gsum[rows:rows + 1, col:col + width]
            delta, m_new, v_new = _adamw_update(g, w_r[i][...], m_r[i][...], v_r[i][...])
            g_o[i][...] = g
            d_o[i][...] = delta
            m_o[i][...] = m_new
            v_o[i][...] = v_new

    shapes = [SDS(w.shape, F32) for w in ws]
    res = pl.pallas_call(body, name="adamw_small", out_shape=shapes * 4 + [SDS((1, LANE), F32)],
                         compiler_params=_params())(small_all, *ws, *ms, *vs)
    return [res[k * n:(k + 1) * n] for k in range(4)], res[4 * n]


def _row_tile(rows):
    if rows <= 128:
        return rows
    if rows % 256 == 0:
        return 256
    return 128 if rows % 128 == 0 else SHARD_W // 4


def _adamw(gsrc, w, m, v, name, stacked):
    rows, cols = w.shape
    tr = _row_tile(rows)
    n_src = len(gsrc) if stacked else 1

    def body(*refs):
        g_refs, (w_ref, m_ref, v_ref, go_ref, d_ref, mo_ref, vo_ref) = refs[:n_src], refs[n_src:]
        if stacked:
            g = None
            for g_ref, (_, slots) in zip(g_refs, gsrc):
                for j in range(slots):
                    t = g_ref[j].astype(F32)
                    g = t if g is None else g + t
        else:
            g = g_refs[0][...]
        delta, m_new, v_new = _adamw_update(g, w_ref[...], m_ref[...], v_ref[...])
        go_ref[...] = g
        d_ref[...] = delta
        mo_ref[...] = m_new
        vo_ref[...] = v_new

    blk = pl.BlockSpec((tr, cols), lambda i: (i, 0))
    if stacked:
        gspecs = [pl.BlockSpec((slots, tr, arr.shape[2]), lambda i: (0, i, 0)) for arr, slots in gsrc]
        gargs = [arr for arr, _ in gsrc]
    else:
        gspecs, gargs = [blk], [gsrc]
    in_specs = gspecs + [blk, blk, blk]
    args = gargs + [w, m, v]
    return pl.pallas_call(
        body, name=name, grid=(rows // tr,), in_specs=in_specs, out_specs=[blk] * 4,
        out_shape=[SDS((rows, cols), F32)] * 4, compiler_params=_params(),
    )(*args)


def kernel(x, c, w_ada, b_ada, norm_w, w_in, b_gate, q_norm_w, k_norm_w, w_attn_proj, conv_w, conv_b, conv_ln_w, conv_ln_b, w_conv_proj, w_out, loss_target, m_w_ada, m_b_ada, m_norm_w, m_w_in, m_b_gate, m_q_norm_w, m_k_norm_w, m_w_attn_proj, m_conv_w, m_conv_b, m_conv_ln_w, m_conv_ln_b, m_w_conv_proj, m_w_out, v_w_ada, v_b_ada, v_norm_w, v_w_in, v_b_gate, v_q_norm_w, v_k_norm_w, v_w_attn_proj, v_conv_w, v_conv_b, v_conv_ln_w, v_conv_ln_b, v_w_conv_proj, v_w_out):
    xi, yi, ci = lax.axis_index("x"), lax.axis_index("y"), lax.axis_index("c")
    me = 4 * xi + 2 * yi + ci
    x2, tgt2 = x[0], loss_target[0]
    w_in_t, m_w_in_t, v_w_in_t = (jnp.transpose(a[0]) for a in (w_in, m_w_in, v_w_in))
    s = x2.shape[0]

    cw_flat = jnp.pad(conv_w[0].reshape(1, -1), ((0, 0), (0, CONVW_FLAT - CONV_K * HEAD_DIM)))
    pre = jnp.concatenate([c, cw_flat], axis=1).reshape(8, -1)
    (pre_all,) = _all_gather([pre], "gather_c_convw", vmem=True)
    pre_all = pre_all.reshape(N_DEV, -1)
    c_all = pre_all[:, :D_MODEL]
    convw_full = pre_all[:, D_MODEL:D_MODEL + CONV_K * HEAD_DIM].reshape(N_DEV, CONV_K, HEAD_DIM)
    convw_full = jnp.transpose(convw_full, (1, 0, 2)).reshape(CONV_K, CONV_W)
    convw_pad = jnp.pad(convw_full, ((0, CONV_HALO - CONV_K), (0, 0)))

    ada_part = _ada_fwd(c_all, w_ada[0])
    (ada_all,) = _all_gather([ada_part], "gather_ada", vmem=True)
    ada = lax.dynamic_index_in_dim(ada_all, me, axis=1, keepdims=False).reshape(1, 3 * D_MODEL) + b_ada
    shift, scale, gate = ada[:, :D_MODEL], ada[:, D_MODEL:2 * D_MODEL], ada[:, 2 * D_MODEL:]

    wt_g, wa_g, wc_g, wo_g = _all_gather_chips(
        [_cast_bf16(w_in_t, "cast_win"), _cast_bf16(w_attn_proj[0], "cast_wa"), _cast_bf16(w_conv_proj[0], "cast_wc"),
         _cast_bf16(w_out[0], "cast_wo")], "gather_weights")
    wt = wt_g.reshape(IN_W, D_MODEL)
    wa = _cols_from_slots(wa_g, "cols_wa")
    wc = _cols_from_slots(wc_g, "cols_wc")
    wo = wo_g.reshape(D_MODEL, D_MODEL)

    h, ht = _norm_fwd(x2, norm_w, scale, shift)
    proj = _mm_in(h, wt)
    qw2 = jnp.tile(q_norm_w, (1, 2))
    kw2 = jnp.tile(k_norm_w, (1, 2))
    o_all, l_all, qn, kn, vn = _attn_fwd(proj, qw2, kw2)
    o3, l3 = [o_all] * N_GROUPS, [l_all] * N_GROUPS
    head_id = jnp.arange(ATTN_W) // HEAD_DIM
    bd = (head_id[:, None] == head_id[None, :]).astype(BF16)
    (dout, da, lse_delta, dcv, mt, yat, yct, dmo, dya, dyc, dproj,
     dgate, dbg, dlnw, dlnb, dcb, loss_p) = _tail(
        x2, tgt2, proj, o3, l3, wa, wc, wo, gate, b_gate[:, :D_MODEL], b_gate[:, D_MODEL:], convw_pad,
        conv_b, conv_ln_w, conv_ln_b, bd)

    dproj, dconvw8 = _conv_bwd(dcv, proj, convw_pad, dproj)
    dconvw = jnp.sum(dconvw8.reshape(CONV_HALO, 8, CONV_W), axis=1)
    dproj, dqw_all, dkw_all = _attn_bwd(proj, qn, kn, vn, da, lse_delta, qw2, kw2, dproj)
    dqw_g3, dkw_g3 = [dqw_all], [dkw_all]
    dw_in_p = _mm_dw(ht, dproj).reshape(N_DEV, SHARD_W, D_MODEL)
    dwo_p = _mm_acc(mt, dmo, "mm_dwo", col_slots=False).reshape(N_DEV, D_MODEL // N_DEV, D_MODEL)
    dwa_p = _mm_acc(yat, dya, "mm_dwa", col_slots=True)
    dwc_p = _mm_acc(yct, dyc, "mm_dwc", col_slots=True)

    partials = [dw_in_p, dwa_p, dwc_p, dwo_p]
    me_arr = jnp.reshape(me, (1,)).astype(jnp.int32)
    from_sib = _exchange_sibling(partials, "exchange_sibling")
    presums = [_presum(p, f, me_arr, f"presum{i}") for i, (p, f) in enumerate(zip(partials, from_sib))]
    s_sems, r_sems, pre_thru, land_thru, token = _exchange_chips_start(presums, "exchange_chips_start")
    gx, dsh, dsc, dnw = _mm_dh_norm_bwd(dproj, wt, x2, dout, norm_w, scale, token)
    small_p = _pack_partials(dsh, dsc, dgate, dnw, dbg, dqw_g3, dkw_g3, dcb, dlnw, dlnb, loss_p)
    small_all, dconvw_all = _all_gather([small_p, dconvw], "gather_small", vmem=True)

    small_w = (b_ada, norm_w, b_gate, q_norm_w, k_norm_w, conv_b, conv_ln_w, conv_ln_b)
    small_m = (m_b_ada, m_norm_w, m_b_gate, m_q_norm_w, m_k_norm_w, m_conv_b, m_conv_ln_w, m_conv_ln_b)
    small_v = (v_b_ada, v_norm_w, v_b_gate, v_q_norm_w, v_k_norm_w, v_conv_b, v_conv_ln_w, v_conv_ln_b)
    r_small, loss_row = _adamw_small(small_all, small_w, small_m, small_v)
    dcw_mine = lax.dynamic_slice_in_dim(dconvw_all[:, :CONV_K, :], me * HEAD_DIM, HEAD_DIM, axis=2)
    r_convw = _adamw([(dcw_mine, N_DEV)], conv_w[0], m_conv_w[0], v_conv_w[0], "adamw_conv_w", stacked=True)

    d_ada_all = small_all[:, 0:3, :].reshape(N_DEV, 3 * D_MODEL)
    d_ada_cols = lax.dynamic_slice_in_dim(d_ada_all, me * (3 * D_MODEL // N_DEV), 3 * D_MODEL // N_DEV, axis=1)
    g_wada = _ada_bwd(c_all, d_ada_cols)
    r_ada = _adamw(g_wada, w_ada[0], m_w_ada[0], v_w_ada[0], "adamw_w_ada", stacked=False)
    pres, lands = _exchange_chips_wait(s_sems, r_sems, pre_thru, land_thru, r_ada[1], "exchange_chips_wait")
    terms = [[(p, 1), (l, len(CHIP_K))] for p, l in zip(pres, lands)]
    r_win = [jnp.transpose(r) for r in _adamw(terms[0], w_in_t, m_w_in_t, v_w_in_t, "adamw_w_in", stacked=True)]
    r_wap = _adamw(terms[1], w_attn_proj[0], m_w_attn_proj[0], v_w_attn_proj[0], "adamw_w_attn_proj", stacked=True)
    r_wcp = _adamw(terms[2], w_conv_proj[0], m_w_conv_proj[0], v_w_conv_proj[0], "adamw_w_conv_proj", stacked=True)
    r_wout = _adamw(terms[3], w_out[0], m_w_out[0], v_w_out[0], "adamw_w_out", stacked=True)

    outs = [loss_row[0, 0], gx[None]]
    for k in range(4):
        b_ada_k, norm_w_k, b_gate_k, qn_k, kn_k, conv_b_k, ln_w_k, ln_b_k = r_small[k]
        outs += [r_ada[k][None], b_ada_k, norm_w_k, r_win[k][None], b_gate_k, qn_k, kn_k, r_wap[k][None],
                 r_convw[k][None], conv_b_k, ln_w_k, ln_b_k, r_wcp[k][None], r_wout[k][None]]
    return tuple(outs)
```

```python
import functools

import jax
import jax.numpy as jnp
from jax import lax
from jax.experimental import pallas as pl
from jax.experimental.pallas import tpu as pltpu

F32 = jnp.float32
BF16 = jnp.bfloat16
SDS = jax.ShapeDtypeStruct
MESH = pl.DeviceIdType.MESH

N_DEV = 8
D_MODEL = 1024
HEAD_DIM = 64
N_GROUPS = 3
DILATIONS = (1, 4, 16)
BAND = 128
BWD_UNROLL = 8
ATTN_W = 512
CONV_W = 512
CONV_K = 31
CONV_HALO = 32
IN_W = 8704
SHARD_W = IN_W // N_DEV
PAIR_W = 2 * SHARD_W
Q0, K0, V0, ZA0, U0, ZC0, G0 = 0, 1536, 3072, 4608, 5120, 6144, 6656
EPS = 1e-6
LANE = 128
VMEM_LIMIT = 56 * 1024 * 1024

ADAM_LR, ADAM_B1, ADAM_B2, ADAM_EPS, ADAM_WD, ADAM_STEP = 0.001, 0.9, 0.999, 1e-08, 0.01, 10

CONVW_FLAT = 2048


def _params(**kw):
    return pltpu.CompilerParams(vmem_limit_bytes=VMEM_LIMIT, **kw)


def _sigmoid(z):
    return 0.5 * jnp.tanh(0.5 * z) + 0.5


def _dot(a, b):
    return jnp.dot(a, b, preferred_element_type=F32)


def _dot_nt(a, b):
    return lax.dot_general(a, b, (((1,), (1,)), ((), ())), preferred_element_type=F32)


def _dot_tn(a, b):
    return lax.dot_general(a, b, (((0,), (0,)), ((), ())), preferred_element_type=F32)


def _peer(x, y, c, k):
    px = 1 - x if (k >> 2) & 1 else x
    py = 1 - y if (k >> 1) & 1 else y
    pc = 1 - c if k & 1 else c
    return (px, py, pc), 4 * px + 2 * py + pc


def _all_gather(arrays, name, vmem):
    n = len(arrays)
    space = pltpu.VMEM if vmem else pl.ANY

    def body(*refs):
        ins, outs = refs[:n], refs[n:2 * n]
        send_sems, recv_sems, local_sems = refs[2 * n:]
        x, y, c = lax.axis_index("x"), lax.axis_index("y"), lax.axis_index("c")
        me = 4 * x + 2 * y + c
        locals_ = [pltpu.make_async_copy(ins[a], outs[a].at[me], local_sems.at[a]) for a in range(n)]
        for cp in locals_:
            cp.start()
        sends = []
        for k in range(1, N_DEV):
            peer, _ = _peer(x, y, c, k)
            for a in range(n):
                cp = pltpu.make_async_remote_copy(
                    src_ref=ins[a], dst_ref=outs[a].at[me], send_sem=send_sems.at[a, k - 1],
                    recv_sem=recv_sems.at[a, k - 1], device_id=peer, device_id_type=MESH)
                cp.start()
                sends.append(cp)
        for k in range(1, N_DEV):
            peer, pidx = _peer(x, y, c, k)
            for a in range(n):
                pltpu.make_async_remote_copy(
                    src_ref=ins[a], dst_ref=outs[a].at[pidx], send_sem=send_sems.at[a, k - 1],
                    recv_sem=recv_sems.at[a, k - 1], device_id=peer, device_id_type=MESH).wait_recv()
        for cp in sends:
            cp.wait_send()
        for cp in locals_:
            cp.wait()

    return pl.pallas_call(
        body, name=name,
        out_shape=[SDS((N_DEV,) + a.shape, a.dtype) for a in arrays],
        in_specs=[pl.BlockSpec(memory_space=space)] * n,
        out_specs=[pl.BlockSpec(memory_space=space)] * n,
        scratch_shapes=[pltpu.SemaphoreType.DMA((n, N_DEV - 1)), pltpu.SemaphoreType.DMA((n, N_DEV - 1)),
                        pltpu.SemaphoreType.DMA((n,))],
        compiler_params=_params(),
    )(*arrays)


CHIP_K = (2, 4, 6)


def _all_gather_chips(arrays, name):
    n = len(arrays)
    k_y, k_x, k_d = CHIP_K

    def body(*refs):
        ins, outs = refs[:n], refs[n:2 * n]
        send_sems, recv_sems, local_sems = refs[2 * n:]
        x, y, c = lax.axis_index("x"), lax.axis_index("y"), lax.axis_index("c")
        me = 4 * x + 2 * y + c
        sib, sib_idx = _peer(x, y, c, 1)
        nbr_y, idx_y = _peer(x, y, c, k_y)
        nbr_x, idx_x = _peer(x, y, c, k_x)
        _, idx_d = _peer(x, y, c, k_d)

        def copy(a, slot, block, to, src=None):
            return pltpu.make_async_remote_copy(
                src_ref=outs[a].at[block] if src is None else src, dst_ref=outs[a].at[block],
                send_sem=send_sems.at[a, slot], recv_sem=recv_sems.at[a, slot], device_id=to, device_id_type=MESH)

        locals_ = [pltpu.make_async_copy(ins[a], outs[a].at[me], local_sems.at[a]) for a in range(n)]
        for cp in locals_:
            cp.start()
        for a in range(n):
            copy(a, 0, me, sib, src=ins[a]).start()
            copy(a, 1, me, nbr_y, src=ins[a]).start()
            copy(a, 2, me, nbr_x, src=ins[a]).start()

        def arrived(slot, block, frm, send_on_to=None):
            for a in range(n):
                copy(a, slot, block, frm).wait_recv()
                if send_on_to is not None:
                    copy(a, 3, block, send_on_to).start()
                copy(a, 3 + slot, block, sib).start()

        @pl.when(c == 0)
        def _():
            arrived(1, idx_y, nbr_y, send_on_to=nbr_x)
            arrived(2, idx_x, nbr_x)

        @pl.when(c == 1)
        def _():
            arrived(2, idx_x, nbr_x, send_on_to=nbr_y)
            arrived(1, idx_y, nbr_y)

        arrived(3, idx_d, nbr_x)
        for a in range(n):
            copy(a, 0, sib_idx, sib).wait_recv()
        for slot, k in ((4, k_y), (5, k_x), (6, k_d)):
            _, pidx = _peer(x, y, 1 - c, k)
            for a in range(n):
                copy(a, slot, pidx, sib).wait_recv()
        for slot in range(N_DEV - 1):
            for a in range(n):
                copy(a, slot, me, sib).wait_send()
        for cp in locals_:
            cp.wait()

    return pl.pallas_call(
        body, name=name,
        out_shape=[SDS((N_DEV,) + a.shape, a.dtype) for a in arrays],
        in_specs=[pl.BlockSpec(memory_space=pl.ANY)] * n,
        out_specs=[pl.BlockSpec(memory_space=pl.ANY)] * n,
        scratch_shapes=[pltpu.SemaphoreType.DMA((n, N_DEV - 1)), pltpu.SemaphoreType.DMA((n, N_DEV - 1)),
                        pltpu.SemaphoreType.DMA((n,))],
        compiler_params=_params(),
    )(*arrays)


def _exchange_sibling(arrays, name):
    n = len(arrays)
    ks = (0,) + CHIP_K

    def body(*refs):
        ins, outs = refs[:n], refs[n:2 * n]
        send_sems, recv_sems = refs[2 * n:]
        x, y, c = lax.axis_index("x"), lax.axis_index("y"), lax.axis_index("c")
        sib, sib_idx = _peer(x, y, c, 1)
        sends = []
        for i, k in enumerate(ks):
            _, tgt = _peer(x, y, 1 - c, k) if k else (None, sib_idx)
            for a in range(n):
                cp = pltpu.make_async_remote_copy(
                    src_ref=ins[a].at[tgt], dst_ref=outs[a].at[i], send_sem=send_sems.at[a, i],
                    recv_sem=recv_sems.at[a, i], device_id=sib, device_id_type=MESH)
                cp.start()
                sends.append(cp)
        for cp in sends:
            cp.wait_recv()
        for cp in sends:
            cp.wait_send()

    return pl.pallas_call(
        body, name=name,
        out_shape=[SDS((len(ks),) + a.shape[1:], a.dtype) for a in arrays],
        in_specs=[pl.BlockSpec(memory_space=pl.ANY)] * n,
        out_specs=[pl.BlockSpec(memory_space=pl.ANY)] * n,
        scratch_shapes=[pltpu.SemaphoreType.DMA((n, len(ks))), pltpu.SemaphoreType.DMA((n, len(ks)))],
        compiler_params=_params(),
    )(*arrays)


def _presum(mine, from_sib, me_arr, name):
    _, rows, cols = mine.shape
    tr = _row_tile(rows)
    ns = 1 + len(CHIP_K)

    def body(me_ref, a_ref, b_ref, o_ref):
        del me_ref
        o_ref[...] = (a_ref[...].astype(F32) + b_ref[...].astype(F32)).astype(o_ref.dtype)

    grid_spec = pltpu.PrefetchScalarGridSpec(
        num_scalar_prefetch=1, grid=(ns, rows // tr),
        in_specs=[pl.BlockSpec((1, tr, cols), lambda j, i, me: (jnp.bitwise_xor(me[0], 2 * j), i, 0)),
                  pl.BlockSpec((1, tr, cols), lambda j, i, me: (j, i, 0))],
        out_specs=pl.BlockSpec((1, tr, cols), lambda j, i, me: (j, i, 0)))
    return pl.pallas_call(body, name=name, grid_spec=grid_spec, out_shape=SDS((ns, rows, cols), mine.dtype),
                          compiler_params=_params())(me_arr, mine, from_sib)


HBM_SPEC = pl.BlockSpec(memory_space=pltpu.HBM)
SEM_SPEC = pl.BlockSpec(memory_space=pltpu.SEMAPHORE)
SIDE_EFFECT = pltpu.SideEffectType.DATAFLOW_SIDE_EFFECTING


def _chips_copies(pre_refs, land_refs, send_sems, recv_sems):
    x, y, c = lax.axis_index("x"), lax.axis_index("y"), lax.axis_index("c")
    copies = []
    for j, k in enumerate(CHIP_K):
        peer, _ = _peer(x, y, c, k)
        for a in range(len(pre_refs)):
            copies.append(pltpu.make_async_remote_copy(
                src_ref=pre_refs[a].at[1 + j], dst_ref=land_refs[a].at[j], send_sem=send_sems.at[a * len(CHIP_K) + j],
                recv_sem=recv_sems.at[a * len(CHIP_K) + j], device_id=peer, device_id_type=MESH))
    return copies


def _exchange_chips_start(presums, name):
    n = len(presums)

    def body(*refs):
        pre, land = refs[:n], refs[n:2 * n]
        send_sems, recv_sems = refs[2 * n], refs[2 * n + 1]
        token = refs[-1]
        for cp in _chips_copies(pre, land, send_sems, recv_sems):
            cp.start()
        token[...] = jnp.zeros_like(token)

    nk = len(CHIP_K)
    hbm = [pltpu.HBM(p.shape, p.dtype) for p in presums]
    hbm_land = [pltpu.HBM((nk,) + p.shape[1:], p.dtype) for p in presums]
    res = pl.pallas_call(
        body, name=name,
        out_shape=(pltpu.SemaphoreType.DMA((n * nk,)), pltpu.SemaphoreType.DMA((n * nk,)), *hbm, *hbm_land, SDS((8, LANE), F32)),
        in_specs=[HBM_SPEC] * (2 * n),
        out_specs=(SEM_SPEC, SEM_SPEC, *([HBM_SPEC] * (2 * n)), pl.BlockSpec(memory_space=pltpu.VMEM)),
        input_output_aliases={i: 2 + i for i in range(2 * n)},
        compiler_params=pltpu.CompilerParams(has_side_effects=SIDE_EFFECT),
    )(*[pltpu.with_memory_space_constraint(p, pltpu.HBM) for p in presums],
      *[pltpu.with_memory_space_constraint(lax.empty((nk,) + p.shape[1:], p.dtype), pltpu.HBM) for p in presums])
    return res[0], res[1], res[2:2 + n], res[2 + n:2 + 2 * n], res[-1]


def _exchange_chips_wait(send_sems, recv_sems, pre_thru, land_thru, after, name):
    n = len(pre_thru)

    def body(*refs):
        pre, land = refs[:n], refs[n:2 * n]
        s_sems, r_sems = refs[2 * n], refs[2 * n + 1]
        for cp in _chips_copies(pre, land, s_sems, r_sems):
            cp.wait_send()
            cp.wait_recv()

    hbm = [pltpu.HBM(p.shape, p.dtype) for p in (*pre_thru, *land_thru)]
    res = pl.pallas_call(
        body, name=name, out_shape=tuple(hbm),
        in_specs=[HBM_SPEC] * (2 * n) + [SEM_SPEC, SEM_SPEC, pl.BlockSpec(memory_space=pl.ANY)],
        out_specs=tuple([HBM_SPEC] * (2 * n)),
        input_output_aliases={i: i for i in range(2 * n)},
        compiler_params=pltpu.CompilerParams(has_side_effects=SIDE_EFFECT),
    )(*pre_thru, *land_thru, send_sems, recv_sems, after)
    return res[:n], res[n:]


def _cast_bf16(w, name):
    def body(w_ref, o_ref):
        o_ref[...] = w_ref[...].astype(BF16)

    return pl.pallas_call(body, name=name, out_shape=SDS(w.shape, BF16), compiler_params=_params())(w)


def _cols_from_slots(wg, name):
    _, rows, cols = wg.shape

    def body(w_ref, o_ref):
        for j in range(N_DEV):
            o_ref[:, j * cols:(j + 1) * cols] = w_ref[j]

    return pl.pallas_call(body, name=name, out_shape=SDS((rows, N_DEV * cols), wg.dtype), compiler_params=_params())(wg)


def _ada_fwd(c_all, w_ada):
    def body(c_ref, w_ref, o_ref):
        cv = c_ref[...]
        sc = (cv * _sigmoid(cv)).astype(BF16)
        o_ref[...] = _dot(sc, w_ref[...].astype(BF16))

    return pl.pallas_call(body, name="ada_fwd", out_shape=SDS((N_DEV, w_ada.shape[1]), F32),
                          compiler_params=_params())(c_all, w_ada)


def _ada_bwd(c_all, d_ada_cols):
    def body(c_ref, d_ref, o_ref):
        cv = c_ref[...]
        sc = (cv * _sigmoid(cv)).astype(BF16)
        o_ref[...] = _dot_tn(sc, d_ref[...].astype(BF16))

    return pl.pallas_call(body, name="ada_bwd", out_shape=SDS((D_MODEL, d_ada_cols.shape[1]), F32),
                          compiler_params=_params())(c_all, d_ada_cols)


def _norm_fwd(x, norm_w, scale, shift):
    s = x.shape[0]
    tr = 1024

    def body(x_ref, nw_ref, sc_ref, sh_ref, h_ref, ht_ref):
        xv = x_ref[...]
        r = lax.rsqrt(jnp.mean(xv * xv, axis=-1, keepdims=True) + EPS)
        h = (xv * r * nw_ref[...]) * (1.0 + sc_ref[...]) + sh_ref[...]
        h_ref[...] = h.astype(BF16)
        ht_ref[...] = h.T.astype(BF16)

    vec = pl.BlockSpec((1, D_MODEL), lambda i: (0, 0))
    return pl.pallas_call(
        body, name="norm_fwd", grid=(s // tr,),
        in_specs=[pl.BlockSpec((tr, D_MODEL), lambda i: (i, 0)), vec, vec, vec],
        out_specs=[pl.BlockSpec((tr, D_MODEL), lambda i: (i, 0)), pl.BlockSpec((D_MODEL, tr), lambda i: (0, i))],
        out_shape=[SDS((s, D_MODEL), BF16), SDS((D_MODEL, s), BF16)], compiler_params=_params(),
    )(x, norm_w, scale, shift)


def _mm_in(h, wt):
    s = h.shape[0]
    tm = 1024

    def body(h_ref, w_ref, o_ref):
        o_ref[...] = _dot_nt(h_ref[...], w_ref[...])

    return pl.pallas_call(
        body, name="mm_in", grid=(IN_W // PAIR_W, s // tm),
        in_specs=[pl.BlockSpec((tm, D_MODEL), lambda p, m: (m, 0)),
                  pl.BlockSpec((PAIR_W, D_MODEL), lambda p, m: (p, 0))],
        out_specs=pl.BlockSpec((tm, PAIR_W), lambda p, m: (m, p)),
        out_shape=SDS((s, IN_W), F32), compiler_params=_params(),
    )(h, wt)


def _head_ones():
    a = lax.broadcasted_iota(jnp.int32, (LANE, LANE), 0) // HEAD_DIM
    b = lax.broadcasted_iota(jnp.int32, (LANE, LANE), 1) // HEAD_DIM
    return (a == b).astype(BF16)


def _head_sums(t, ones):
    return _dot(t.astype(BF16), ones)


def _band_bias(bias, transposed=False):
    qi = lax.broadcasted_iota(jnp.int32, (2 * BAND, 2 * BAND), 1 if transposed else 0) % BAND
    kj = lax.broadcasted_iota(jnp.int32, (2 * BAND, 2 * BAND), 0 if transposed else 1)
    dist = qi + BAND - kj
    valid = (dist >= 0) & (dist <= BAND)
    bias[1] = jnp.where(valid, 0.0, -1e30)
    bias[0] = jnp.where(valid & (kj >= BAND), 0.0, -1e30)


def _token_rows(j, d, chunk, per_r):
    return pl.ds(j // per_r + (j % per_r) * (chunk * d), chunk, stride=d)


def _deinterleave_many(jobs, ones, d, sub_len, chunk, unroll):
    per_r = sub_len // chunk

    def step(j, _):
        tok = _token_rows(j, d, chunk, per_r)
        for src_ref, dst_ref, w_ref, scale, dst_off in jobs:
            t = src_ref[tok, :]
            if w_ref is not None:
                ms = _head_sums(t * t, ones) * (1.0 / HEAD_DIM)
                t = t * lax.rsqrt(ms + EPS) * (w_ref[...] * scale)
            dst_ref[pl.ds(pl.multiple_of(dst_off + j * chunk, BAND), chunk), :] = t.astype(dst_ref.dtype)
        return 0
    lax.fori_loop(0, d * per_r, step, 0, unroll=unroll)


def _deinterleave(src_ref, dst_ref, w_ref, ones, d, sub_len, chunk, scale, dst_off):
    _deinterleave_many([(src_ref, dst_ref, w_ref, scale, dst_off)], ones, d, sub_len, chunk, 4)


N_PAIRS = ATTN_W // LANE


def _attn_fwd(proj, qw2, kw2):
    s = proj.shape[0]

    def group_body(g, step, q_ref, k_ref, v_ref, qw_ref, kw_ref, o_ref, l_ref, qd, kd, vd, od, ld, bias):
        d = DILATIONS[g]
        sub_len = s // d
        nb = sub_len // BAND
        chunk = min(sub_len, 256)
        lo = lax.broadcasted_iota(jnp.int32, (1, LANE), 1) < HEAD_DIM
        ones = _head_ones()

        @pl.when(step == 0)
        def _():
            _band_bias(bias)

        kd[0:BAND, :] = jnp.zeros((BAND, LANE), BF16)
        vd[0:BAND, :] = jnp.zeros((BAND, LANE), BF16)
        _deinterleave_many([(q_ref, qd, qw_ref, HEAD_DIM ** -0.5, 0), (k_ref, kd, kw_ref, 1.0, BAND),
                            (v_ref, vd, None, 1.0, BAND)], ones, d, sub_len, chunk, 4)

        def block(t, _):
            base = pl.multiple_of(t * BAND, BAND)
            q = qd[pl.ds(base, BAND), :]
            k2 = kd[pl.ds(base, 2 * BAND), :]
            v2 = vd[pl.ds(base, 2 * BAND), :]
            zero = jnp.zeros_like(q)
            qs = jnp.concatenate([jnp.where(lo, q, zero), jnp.where(lo, zero, q)], axis=0)
            sc = _dot_nt(qs, k2) + bias[jnp.minimum(t % nb, 1)]
            m = jnp.max(sc, axis=-1, keepdims=True)
            p = jnp.exp(sc - m)
            den = jnp.sum(p, axis=-1, keepdims=True)
            u = _dot(p.astype(BF16), v2) * (1.0 / den)
            lse = m + jnp.log(den)
            od[pl.ds(base, BAND), :] = jnp.where(lo, u[:BAND], u[BAND:])
            ld[pl.ds(base, BAND), :] = jnp.where(lo, lse[:BAND], lse[BAND:])
            return 0
        lax.fori_loop(0, s // BAND, block, 0, unroll=16)

        per_r = sub_len // chunk

        def back(j, _):
            src = pl.ds(pl.multiple_of(j * chunk, chunk), chunk)
            dst = _token_rows(j, d, chunk, per_r)
            o_ref[dst, :] = od[src, :]
            l_ref[dst, :] = ld[src, :]
            return 0
        lax.fori_loop(0, d * per_r, back, 0, unroll=2)

    def body(*refs):
        step = pl.program_id(0)
        for g in range(N_GROUPS):
            pl.when(step // N_PAIRS == g)(functools.partial(group_body, g, step, *refs))

    col = lambda off: pl.BlockSpec((s, LANE), lambda i, off=off: (0, off // LANE + i))
    vec = pl.BlockSpec((1, LANE), lambda i: (0, 0))
    out = pl.BlockSpec((s, LANE), lambda i: (0, i))
    width = N_GROUPS * ATTN_W
    return pl.pallas_call(
        body, name="attn_fwd", grid=(N_GROUPS * N_PAIRS,),
        in_specs=[col(Q0), col(K0), col(V0), vec, vec],
        out_specs=[out, out, out, pl.BlockSpec((s + BAND, LANE), lambda i: (0, i)),
                   pl.BlockSpec((s + BAND, LANE), lambda i: (0, i))],
        out_shape=[SDS((s, width), F32)] * 2 + [SDS((s, width), BF16)] + [SDS((s + BAND, width), BF16)] * 2,
        scratch_shapes=[pltpu.VMEM((s, LANE), F32), pltpu.VMEM((s, LANE), F32),
                        pltpu.VMEM((2, 2 * BAND, 2 * BAND), F32)],
        compiler_params=_params(),
    )(proj, proj, proj, qw2, kw2)


def _attn_bwd(proj, qn, kn, vn, da, lse_delta, qw2, kw2, dproj):
    s = proj.shape[0]
    n_steps = N_GROUPS * N_PAIRS

    def group_body(g, hp, q_ref, k_ref, qn_ref, kd, vd, da_ref, ld_ref, qw_ref, kw_ref, dp_in, dp_out,
                   dqw_ref, dkw_ref, kdt, dad, lst, dlt, dqt, dqd, dkd, dvd, st, st_k, stb, bias_t, wacc, sem):
        del dp_in
        d = DILATIONS[g]
        sub_len = s // d
        nb = sub_len // BAND
        chunk = min(sub_len, 256)
        lo = lax.broadcasted_iota(jnp.int32, (1, LANE), 1) < HEAD_DIM
        row_lo = lax.broadcasted_iota(jnp.int32, (LANE, 1), 0) < HEAD_DIM
        ones = _head_ones()
        per_r = sub_len // chunk
        cblk = chunk // BAND

        @pl.when(hp == 0)
        def _():
            _band_bias(bias_t, transposed=True)

        kdt[0] = jnp.zeros((LANE, BAND), BF16)

        def k_step(t, _):
            kdt[1 + t] = kd[pl.ds(pl.multiple_of(BAND + t * BAND, BAND), BAND), :].astype(F32).T.astype(BF16)
            return 0
        lax.fori_loop(0, s // BAND, k_step, 0, unroll=4)
        _deinterleave(da_ref, dad, None, ones, d, sub_len, chunk, 1.0, 0)

        def rows_step(j, _):
            tok = _token_rows(j, d, chunk, per_r)
            tt = ld_ref[tok, :].T
            for u in range(cblk):
                cols = slice(u * BAND, (u + 1) * BAND)
                lst[j * cblk + u, 0:1, :] = tt[0:1, cols]
                lst[j * cblk + u, 1:2, :] = tt[HEAD_DIM:HEAD_DIM + 1, cols]
                dlt[j * cblk + u, 0:1, :] = tt[HEAD_DIM // 2:HEAD_DIM // 2 + 1, cols]
                dlt[j * cblk + u, 1:2, :] = tt[HEAD_DIM + HEAD_DIM // 2:HEAD_DIM + HEAD_DIM // 2 + 1, cols]
            return 0
        lax.fori_loop(0, d * per_r, rows_step, 0, unroll=4)

        def block(t, carry):
            ck, cv = carry
            base = pl.multiple_of(t * BAND, BAND)
            q = qn_ref[pl.ds(base, BAND), :]
            k2 = kd[pl.ds(base, 2 * BAND), :]
            v2 = vd[pl.ds(base, 2 * BAND), :]
            k2t = jnp.concatenate([kdt[t], kdt[t + 1]], axis=1)
            dav = dad[pl.ds(base, BAND), :]
            zero = jnp.zeros_like(q)
            qs = jnp.concatenate([jnp.where(lo, q, zero), jnp.where(lo, zero, q)], axis=0)
            das = jnp.concatenate([jnp.where(lo, dav, zero), jnp.where(lo, zero, dav)], axis=0)
            ls_row = jnp.concatenate([lst[t, 0:1, :], lst[t, 1:2, :]], axis=1)
            dl_row = jnp.concatenate([dlt[t, 0:1, :], dlt[t, 1:2, :]], axis=1)
            sc_t = _dot_nt(k2, qs) + bias_t[jnp.minimum(t % nb, 1)]
            p_t = jnp.exp(sc_t - ls_row)
            dp_t = _dot_nt(v2, das)
            ds_t = (p_t * (dp_t - dl_row)).astype(BF16)
            dv2 = _dot(p_t.astype(BF16), das)
            dk2 = _dot(ds_t, qs)
            dvd[pl.ds(base, BAND), :] = cv + dv2[:BAND]
            dkd[pl.ds(base, BAND), :] = ck + dk2[:BAND]
            dq_t = _dot(k2t, ds_t)
            dqt[t] = jnp.where(row_lo, dq_t[:, :BAND], dq_t[:, BAND:])
            return dk2[BAND:], dv2[BAND:]

        def blocks(i, carry):
            for u in range(BWD_UNROLL):
                carry = block(i * BWD_UNROLL + u, carry)
            return carry
        zeros = jnp.zeros((BAND, LANE), F32)
        ck, cv = lax.fori_loop(0, s // (BAND * BWD_UNROLL), blocks, (zeros, zeros))
        dkd[s:s + BAND, :] = ck
        dvd[s:s + BAND, :] = cv

        def dq_rows(t, _):
            dqd[pl.ds(pl.multiple_of(t * BAND, BAND), BAND), :] = dqt[t].T
            return 0
        lax.fori_loop(0, s // BAND, dq_rows, 0, unroll=4)

        def col_copy(slot, col0):
            return pltpu.make_async_copy(
                stb.at[slot], dp_out.at[:, pl.ds(pl.multiple_of(col0 + LANE * hp, LANE), LANE)], sem.at[slot])

        def store_cols(slot, col0, src):
            @pl.when(hp > 0)
            def _():
                col_copy(slot, col0).wait()
            stb[slot] = src[...].astype(BF16)
            col_copy(slot, col0).start()

        sides = ((q_ref, dqd, 0, qw_ref, HEAD_DIM ** -0.5, st), (k_ref, dkd, BAND, kw_ref, 1.0, st_k))
        wacc[...] = jnp.zeros_like(wacc)

        def norm_step(j, _):
            tok = _token_rows(j, d, chunk, per_r)
            for i, (src_ref, dy_ref, dy_off, w_ref, scale, dst) in enumerate(sides):
                t = src_ref[tok, :]
                dy = dy_ref[pl.ds(pl.multiple_of(dy_off + j * chunk, BAND), chunk), :]
                rr = lax.rsqrt(_head_sums(t * t, ones) * (1.0 / HEAD_DIM) + EPS)
                nrm = t * rr
                wacc[i] += jnp.sum((dy * nrm).reshape(chunk // 8, 8, LANE), axis=0)
                dn = dy * (w_ref[...] * scale)
                dst[tok, :] = rr * (dn - nrm * (_head_sums(dn * nrm, ones) * (1.0 / HEAD_DIM)))
            return 0
        lax.fori_loop(0, d * per_r, norm_step, 0, unroll=4)

        @pl.when(hp == 0)
        def _():
            dqw_ref[...] = jnp.zeros_like(dqw_ref)
            dkw_ref[...] = jnp.zeros_like(dkw_ref)

        for i, dw_ref in enumerate((dqw_ref, dkw_ref)):
            dw_ref[...] += jnp.broadcast_to(jnp.sum(wacc[i], axis=0, keepdims=True) * sides[i][4], dw_ref.shape)
        store_cols(0, Q0, st)
        store_cols(1, K0, st_k)

        def v_back(j, _):
            src = pl.ds(pl.multiple_of(BAND + j * chunk, BAND), chunk)
            st[_token_rows(j, d, chunk, per_r), :] = dvd[src, :]
            return 0
        lax.fori_loop(0, d * per_r, v_back, 0, unroll=2)
        store_cols(2, V0, st)

        @pl.when(hp == n_steps - 1)
        def _():
            for slot, col0 in enumerate((Q0, K0, V0)):
                col_copy(slot, col0).wait()

    def body(*refs):
        step = pl.program_id(0)
        for g in range(N_GROUPS):
            pl.when(step // N_PAIRS == g)(functools.partial(group_body, g, step, *refs))

    col = lambda off: pl.BlockSpec((s, LANE), lambda i, off=off: (0, off // LANE + i))
    mid = pl.BlockSpec((s, LANE), lambda i: (0, i))
    padded = pl.BlockSpec((s + BAND, LANE), lambda i: (0, i))
    slot4 = pl.BlockSpec((s, LANE), lambda i: (0, i % N_PAIRS))
    vec = pl.BlockSpec((1, LANE), lambda i: (0, 0))
    acc = pl.BlockSpec((8, LANE), lambda i: (0, 0))
    any_ = pl.BlockSpec(memory_space=pl.ANY)
    return pl.pallas_call(
        body, name="attn_bwd", grid=(n_steps,),
        in_specs=[col(Q0), col(K0), mid, padded, padded, slot4, slot4, vec, vec, any_],
        out_specs=[any_, acc, acc],
        out_shape=[SDS(dproj.shape, dproj.dtype), SDS((8, LANE), F32), SDS((8, LANE), F32)],
        input_output_aliases={9: 0},
        scratch_shapes=[pltpu.VMEM((s // BAND + 1, LANE, BAND), BF16), pltpu.VMEM((s, LANE), BF16),
                        pltpu.VMEM((s // BAND, 8, BAND), F32), pltpu.VMEM((s // BAND, 8, BAND), F32),
                        pltpu.VMEM((s // BAND, LANE, BAND), F32),
                        pltpu.VMEM((s, LANE), F32), pltpu.VMEM((s + BAND, LANE), F32), pltpu.VMEM((s + BAND, LANE), F32),
                        pltpu.VMEM((s, LANE), F32), pltpu.VMEM((s, LANE), F32), pltpu.VMEM((3, s, LANE), BF16),
                        pltpu.VMEM((2, 2 * BAND, 2 * BAND), F32), pltpu.VMEM((2, 8, LANE), F32),
                        pltpu.SemaphoreType.DMA((3,))],
        compiler_params=_params(),
    )(proj, proj, qn, kn, vn, da, lse_delta, qw2, kw2, dproj)


def _tap_views(ext_ref, sh_ref, offsets, tr, cols):
    for b in range(8):
        group = [j for j, o in enumerate(offsets) if o % 8 == b]
        if not group:
            continue
        first = min(offsets[j] for j in group)
        span = tr + max(offsets[j] for j in group) - first
        sh_ref[0:span, cols] = ext_ref[first:first + span, cols]
        for j in group:
            yield j, sh_ref[offsets[j] - first:offsets[j] - first + tr, cols]


def _silu_grad(z, sg):
    return sg * (1.0 + z * (1.0 - sg))


def _glu(u):
    a_h, b_h = u[:, :CONV_W], u[:, CONV_W:]
    sg = _sigmoid(b_h)
    return a_h, sg, a_h * sg


def _tail(x, tgt, proj, o3, l3, wa, wc, wo, gate, bga, bgc, convw, convb, lnw, lnb, bd):
    s = x.shape[0]
    tr = 256

    def body(x_ref, t_ref, za_ref, u_ref, uh_ref, zc_ref, g0_ref, g1_ref, g2_ref, g3_ref,
             o0_ref, o1_ref, o2_ref, l0_ref, l1_ref, l2_ref, wa_ref, wc_ref, wo_ref,
             gate_ref, bga_ref, bgc_ref, cw_ref, cb_ref, lnw_ref, lnb_ref, bd_ref,
             dout_ref, da_ref, ld_ref, dcv_ref, mt_ref, yat_ref, yct_ref, dmo_ref, dya_ref, dyc_ref, dp_ref,
             dgate_ref, dbg_ref, dlnw_ref, dlnb_ref, dcb_ref, loss_ref,
             ext, sh, st_za, st_zc, st_g, sems):
        i = pl.program_id(0)

        @pl.when(i == 0)
        def _():
            for r in (dgate_ref, dbg_ref, dlnw_ref, dlnb_ref, dcb_ref, loss_ref):
                r[...] = jnp.zeros_like(r)

        def acc_rows(ref, v):
            ref[...] += jnp.broadcast_to(jnp.sum(v, axis=0, keepdims=True), ref.shape)

        la, lb, lc = l0_ref[...], l1_ref[...], l2_ref[...]
        mx = jnp.maximum(jnp.maximum(la, lb), lc)
        ea, eb, ec = jnp.exp(la - mx), jnp.exp(lb - mx), jnp.exp(lc - mx)
        den = ea + eb + ec
        inv = 1.0 / den
        attn = (ea * inv) * o0_ref[...] + (eb * inv) * o1_ref[...] + (ec * inv) * o2_ref[...]
        lse = mx + jnp.log(den)

        za = za_ref[...]
        sga = _sigmoid(za)
        sa = za * sga
        ya_in = attn * sa
        y_attn = _dot(ya_in.astype(BF16), wa_ref[...])

        _, _, glu = _glu(u_ref[...])
        _, _, glu_h = _glu(uh_ref[...])
        ext[0:CONV_HALO, :] = jnp.where(i > 0, glu_h, 0.0)
        ext[CONV_HALO:CONV_HALO + tr, :] = glu
        cv_blocks = []
        for cb in range(CONV_W // LANE):
            cols = slice(cb * LANE, (cb + 1) * LANE)
            cv_c = jnp.broadcast_to(cb_ref[:, cols], (tr, LANE))
            for j, rows in _tap_views(ext, sh, [CONV_HALO - (CONV_K - 1) + j for j in range(CONV_K)], tr, cols):
                cv_c = cv_c + cw_ref[j:j + 1, cols] * rows
            cv_blocks.append(cv_c)
        cv = jnp.concatenate(cv_blocks, axis=1)
        mu = jnp.mean(cv, axis=-1, keepdims=True)
        xc = cv - mu
        rstd = lax.rsqrt(jnp.mean(xc * xc, axis=-1, keepdims=True) + EPS)
        nrm = xc * rstd
        ln = nrm * lnw_ref[...] + lnb_ref[...]
        sgl = _sigmoid(ln)
        cs = ln * sgl
        zc = zc_ref[...]
        sgc = _sigmoid(zc)
        scz = zc * sgc
        yc_in = cs * scz
        y_conv = _dot(yc_in.astype(BF16), wc_ref[...])

        ga = _sigmoid(jnp.concatenate([g0_ref[...], g1_ref[...]], axis=1) + bga_ref[...])
        gc = _sigmoid(jnp.concatenate([g2_ref[...], g3_ref[...]], axis=1) + bgc_ref[...])
        merged = ga * y_attn + gc * y_conv
        mo = _dot(merged.astype(BF16), wo_ref[...])
        gate_v = gate_ref[...]
        err = (x_ref[...] + gate_v * mo) - t_ref[...]
        loss_ref[...] += 0.5 * jnp.sum(jnp.mean(err * err, axis=-1, keepdims=True))
        d_out = err * (1.0 / D_MODEL)
        dout_ref[...] = d_out

        rows = pl.ds(pl.multiple_of(i * tr, tr), tr)
        cps = [pltpu.make_async_copy(st_za, dp_ref.at[rows, pl.ds(ZA0, ATTN_W)], sems.at[0]),
               pltpu.make_async_copy(st_zc, dp_ref.at[rows, pl.ds(ZC0, CONV_W)], sems.at[1]),
               pltpu.make_async_copy(st_g, dp_ref.at[rows, pl.ds(G0, 2 * D_MODEL)], sems.at[2])]

        @pl.when(i > 0)
        def _():
            for cp in cps:
                cp.wait()

        acc_rows(dgate_ref, d_out * mo)
        dmo_b = (d_out * gate_v).astype(BF16)
        dmo_ref[...] = dmo_b
        mt_ref[...] = merged.T.astype(BF16)
        d_merged = _dot_nt(dmo_b, wo_ref[...])
        d_ya = (d_merged * ga).astype(BF16)
        d_yc = (d_merged * gc).astype(BF16)
        dya_ref[...] = d_ya
        dyc_ref[...] = d_yc
        dga = d_merged * y_attn * (ga * (1.0 - ga))
        dgc = d_merged * y_conv * (gc * (1.0 - gc))
        dgs = jnp.concatenate([dga, dgc], axis=1)
        acc_rows(dbg_ref, dgs)
        st_g[...] = dgs.astype(BF16)

        yat_ref[...] = ya_in.T.astype(BF16)
        d_ya_in = _dot_nt(d_ya, wa_ref[...])
        d_attn = d_ya_in * sa
        da_ref[...] = d_attn
        st_za[...] = (d_ya_in * attn * _silu_grad(za, sga)).astype(BF16)
        prod = d_attn * attn
        hi = prod.astype(BF16)
        lo_ = (prod - hi.astype(F32)).astype(BF16)
        delta = _dot(hi, bd_ref[...]) + _dot(lo_, bd_ref[...])
        first_half = (lax.broadcasted_iota(jnp.int32, (1, ATTN_W), 1) % HEAD_DIM) < HEAD_DIM // 2
        ld_ref[...] = jnp.where(first_half, lse, delta)

        yct_ref[...] = yc_in.T.astype(BF16)
        d_yc_in = _dot_nt(d_yc, wc_ref[...])
        st_zc[...] = (d_yc_in * cs * _silu_grad(zc, sgc)).astype(BF16)
        d_ln = (d_yc_in * scz) * _silu_grad(ln, sgl)
        acc_rows(dlnw_ref, d_ln * nrm)
        acc_rows(dlnb_ref, d_ln)
        d_nrm = d_ln * lnw_ref[...]
        d_cv = rstd * (d_nrm - jnp.mean(d_nrm, axis=-1, keepdims=True)
                       - nrm * jnp.mean(d_nrm * nrm, axis=-1, keepdims=True))
        acc_rows(dcb_ref, d_cv)
        dcv_ref[...] = d_cv

        for cp in cps:
            cp.start()

        @pl.when(i == s // tr - 1)
        def _():
            for cp in cps:
                cp.wait()

    def rows(width, colblk=0):
        return pl.BlockSpec((tr, width), lambda i, colblk=colblk: (i, colblk))

    def const(shape):
        return pl.BlockSpec(shape, lambda i: (0,) * len(shape))

    halo = pl.BlockSpec((CONV_HALO, D_MODEL), lambda i: (jnp.maximum(i * (tr // CONV_HALO) - 1, 0), U0 // D_MODEL))
    in_specs = [rows(D_MODEL), rows(D_MODEL), rows(ATTN_W, ZA0 // ATTN_W), rows(D_MODEL, U0 // D_MODEL), halo,
                rows(CONV_W, ZC0 // CONV_W)]
    in_specs += [rows(512, G0 // 512 + j) for j in range(4)]
    in_specs += [rows(ATTN_W, g) for g in range(N_GROUPS)] * 2
    in_specs += [const(wa.shape), const(wc.shape), const(wo.shape), const((1, D_MODEL)), const((1, D_MODEL)),
                 const((1, D_MODEL)), const(convw.shape), const((1, CONV_W)), const((1, CONV_W)), const((1, CONV_W)),
                 const(bd.shape)]
    tcol = lambda width: pl.BlockSpec((width, tr), lambda i: (0, i))
    out_specs = [rows(D_MODEL), rows(ATTN_W), rows(ATTN_W), rows(CONV_W),
                 tcol(D_MODEL), tcol(ATTN_W), tcol(CONV_W), rows(D_MODEL), rows(D_MODEL), rows(D_MODEL),
                 pl.BlockSpec(memory_space=pl.ANY),
                 const((8, D_MODEL)), const((8, 2 * D_MODEL)), const((8, CONV_W)), const((8, CONV_W)), const((8, CONV_W)),
                 const((8, LANE))]
    out_shape = [SDS((s, D_MODEL), F32), SDS((s, ATTN_W), F32), SDS((s, ATTN_W), F32),
                 SDS((s, CONV_W), F32),
                 SDS((D_MODEL, s), BF16), SDS((ATTN_W, s), BF16), SDS((CONV_W, s), BF16),
                 SDS((s, D_MODEL), BF16), SDS((s, D_MODEL), BF16), SDS((s, D_MODEL), BF16),
                 SDS((s, IN_W), BF16),
                 SDS((8, D_MODEL), F32), SDS((8, 2 * D_MODEL), F32), SDS((8, CONV_W), F32), SDS((8, CONV_W), F32),
                 SDS((8, CONV_W), F32), SDS((8, LANE), F32)]
    return pl.pallas_call(
        body, name="tail", grid=(s // tr,), in_specs=in_specs, out_specs=out_specs, out_shape=out_shape,
        scratch_shapes=[pltpu.VMEM((CONV_HALO + tr, CONV_W), F32), pltpu.VMEM((CONV_HALO + tr, CONV_W), F32),
                        pltpu.VMEM((tr, ATTN_W), BF16),
                        pltpu.VMEM((tr, CONV_W), BF16), pltpu.VMEM((tr, 2 * D_MODEL), BF16),
                        pltpu.SemaphoreType.DMA((3,))],
        compiler_params=_params(),
    )(x, tgt, proj, proj, proj, proj, proj, proj, proj, proj, *o3, *l3, wa, wc, wo, gate, bga, bgc,
      convw, convb, lnw, lnb, bd)


def _conv_bwd(dcv, proj, convw, dproj):
    s = dcv.shape[0]
    tr = 128
    nt = s // tr

    def body(dcv_ref, dcvn_ref, u_ref, uh_ref, cw_ref, dp_in, dp_out, dw_ref, extg, extd, sh):
        del dp_in
        i = pl.program_id(0)

        @pl.when(i == 0)
        def _():
            dw_ref[...] = jnp.zeros_like(dw_ref)

        _, _, glu = _glu(u_ref[...])
        _, _, glu_h = _glu(uh_ref[...])
        extg[0:CONV_HALO, :] = jnp.where(i > 0, glu_h, 0.0)
        extg[CONV_HALO:CONV_HALO + tr, :] = glu
        extd[0:tr, :] = dcv_ref[...]
        extd[tr:tr + CONV_HALO, :] = jnp.where(i < nt - 1, dcvn_ref[...], 0.0)
        for cb in range(CONV_W // LANE):
            cols = slice(cb * LANE, (cb + 1) * LANE)
            dglu = jnp.zeros((tr, LANE), F32)
            for j, rows in _tap_views(extd, sh, [CONV_K - 1 - j for j in range(CONV_K)], tr, cols):
                dglu = dglu + cw_ref[j:j + 1, cols] * rows
            dcv_c = dcv_ref[:, cols]
            for j, rows in _tap_views(extg, sh, [CONV_HALO - (CONV_K - 1) + j for j in range(CONV_K)], tr, cols):
                dw_ref[8 * j:8 * j + 8, cols] += jnp.sum((dcv_c * rows).reshape(tr // 8, 8, LANE), axis=0)
            a_h = u_ref[:, cols]
            sgb = _sigmoid(u_ref[:, CONV_W + cb * LANE:CONV_W + (cb + 1) * LANE])
            dp_out[:, cols] = (dglu * sgb).astype(BF16)
            dp_out[:, CONV_W + cb * LANE:CONV_W + (cb + 1) * LANE] = (dglu * a_h * (sgb * (1.0 - sgb))).astype(BF16)

    ucol = U0 // D_MODEL
    return pl.pallas_call(
        body, name="conv_bwd", grid=(nt,),
        in_specs=[pl.BlockSpec((tr, CONV_W), lambda i: (i, 0)),
                  pl.BlockSpec((CONV_HALO, CONV_W), lambda i: (jnp.minimum((i + 1) * (tr // CONV_HALO), s // CONV_HALO - 1), 0)),
                  pl.BlockSpec((tr, D_MODEL), lambda i: (i, ucol)),
                  pl.BlockSpec((CONV_HALO, D_MODEL), lambda i: (jnp.maximum(i * (tr // CONV_HALO) - 1, 0), ucol)),
                  pl.BlockSpec(convw.shape, lambda i: (0, 0)),
                  pl.BlockSpec(memory_space=pl.ANY)],
        out_specs=[pl.BlockSpec((tr, D_MODEL), lambda i: (i, ucol)), pl.BlockSpec((8 * CONV_HALO, CONV_W), lambda i: (0, 0))],
        out_shape=[SDS(dproj.shape, dproj.dtype), SDS((8 * CONV_HALO, CONV_W), F32)],
        input_output_aliases={5: 0},
        scratch_shapes=[pltpu.VMEM((CONV_HALO + tr, CONV_W), F32)] * 3,
        compiler_params=_params(),
    )(dcv, dcv, proj, proj, convw, dproj)


def _mm_acc(at, b, name, col_slots):
    m, s = at.shape
    n = b.shape[1]
    tk = 2048
    nk = s // tk

    def body(a_ref, b_ref, o_ref, acc):
        k = pl.program_id(0)

        @pl.when(k == 0)
        def _():
            acc[...] = jnp.zeros_like(acc)

        acc[...] += _dot(a_ref[...], b_ref[...])

        @pl.when(k == nk - 1)
        def _():
            if col_slots:
                w = n // N_DEV
                for j in range(N_DEV):
                    o_ref[j] = acc[:, j * w:(j + 1) * w].astype(BF16)
            else:
                o_ref[...] = acc[...].astype(BF16)

    if col_slots:
        out_shape = SDS((N_DEV, m, n // N_DEV), BF16)
        out_spec = pl.BlockSpec((N_DEV, m, n // N_DEV), lambda k: (0, 0, 0))
    else:
        out_shape = SDS((m, n), BF16)
        out_spec = pl.BlockSpec((m, n), lambda k: (0, 0))
    return pl.pallas_call(
        body, name=name, grid=(nk,),
        in_specs=[pl.BlockSpec((m, tk), lambda k: (0, k)), pl.BlockSpec((tk, n), lambda k: (k, 0))],
        out_specs=out_spec, out_shape=out_shape, scratch_shapes=[pltpu.VMEM((m, n), F32)],
        compiler_params=_params(),
    )(at, b)


def _mm_dw(ht, dproj):
    s = ht.shape[1]
    tk = 2048
    nk = s // tk

    def body(a_ref, b_ref, o_ref, acc):
        k = pl.program_id(1)

        @pl.when(k == 0)
        def _():
            acc[...] = jnp.zeros_like(acc)

        acc[...] += _dot(a_ref[...], b_ref[...])

        @pl.when(k == nk - 1)
        def _():
            o_ref[...] = acc[...].T.astype(BF16)

    return pl.pallas_call(
        body, name="mm_dw", grid=(IN_W // PAIR_W, nk),
        in_specs=[pl.BlockSpec((D_MODEL, tk), lambda p, k: (0, k)), pl.BlockSpec((tk, PAIR_W), lambda p, k: (k, p))],
        out_specs=pl.BlockSpec((PAIR_W, D_MODEL), lambda p, k: (p, 0)),
        out_shape=SDS((IN_W, D_MODEL), BF16), scratch_shapes=[pltpu.VMEM((D_MODEL, PAIR_W), F32)],
        compiler_params=_params(),
    )(ht, dproj)


def _mm_dh_norm_bwd(dproj, wt, x, dout, norm_w, scale, token):
    s = dproj.shape[0]
    tm = 1024
    n_p = IN_W // PAIR_W

    def body(dp_ref, w_ref, x_ref, do_ref, nw_ref, sc_ref, tok_ref, gx_ref, dsh_ref, dsc_ref, dnw_ref, dh_acc):
        del tok_ref
        m, p = pl.program_id(0), pl.program_id(1)
        part = _dot(dp_ref[...], w_ref[...])

        @pl.when(p == 0)
        def _():
            dh_acc[...] = part

        @pl.when(p > 0)
        def _():
            dh_acc[...] += part

        @pl.when((m == 0) & (p == 0))
        def _():
            for r in (dsh_ref, dsc_ref, dnw_ref):
                r[...] = jnp.zeros_like(r)

        @pl.when(p == n_p - 1)
        def _():
            def acc_rows(ref, v):
                ref[...] += jnp.broadcast_to(jnp.sum(v, axis=0, keepdims=True), ref.shape)

            xv = x_ref[...]
            dh_v = dh_acc[...]
            r = lax.rsqrt(jnp.mean(xv * xv, axis=-1, keepdims=True) + EPS)
            xn = xv * r
            one_sc = 1.0 + sc_ref[...]
            acc_rows(dsh_ref, dh_v)
            acc_rows(dsc_ref, dh_v * (xn * nw_ref[...]))
            acc_rows(dnw_ref, dh_v * xn * one_sc)
            dxn = dh_v * (nw_ref[...] * one_sc)
            gx_ref[...] = do_ref[...] + r * (dxn - xn * jnp.mean(dxn * xn, axis=-1, keepdims=True))

    rows = pl.BlockSpec((tm, D_MODEL), lambda m, p: (m, 0))
    vec = pl.BlockSpec((1, D_MODEL), lambda m, p: (0, 0))
    acc = pl.BlockSpec((8, D_MODEL), lambda m, p: (0, 0))
    return pl.pallas_call(
        body, name="mm_dh_norm_bwd", grid=(s // tm, n_p),
        in_specs=[pl.BlockSpec((tm, PAIR_W), lambda m, p: (m, p)),
                  pl.BlockSpec((PAIR_W, D_MODEL), lambda m, p: (p, 0)),
                  rows, rows, vec, vec, pl.BlockSpec(token.shape, lambda m, p: (0, 0))],
        out_specs=[rows, acc, acc, acc],
        out_shape=[SDS((s, D_MODEL), F32)] + [SDS((8, D_MODEL), F32)] * 3,
        scratch_shapes=[pltpu.VMEM((tm, D_MODEL), F32)], compiler_params=_params(),
    )(dproj, wt, x, dout, norm_w, scale, token)


SMALL_ROWS = 8
QN_COL, KN_COL, CB_COL, LOSS_COL = 0, LANE, 2 * LANE, 2 * LANE + CONV_W


def _pack_partials(dsh, dsc, dgate, dnw, dbg, dqw3, dkw3, dcb, dlnw, dlnb, loss_p):
    n3 = len(dqw3)

    def body(*refs):
        dsh_r, dsc_r, dgate_r, dnw_r, dbg_r = refs[:5]
        dq_r, dk_r = refs[5:5 + n3], refs[5 + n3:5 + 2 * n3]
        dcb_r, dlnw_r, dlnb_r, loss_r, o_ref = refs[5 + 2 * n3:]

        def both_heads(rs):
            t = rs[0][0:1, :]
            for r in rs[1:]:
                t = t + r[0:1, :]
            return t + pltpu.roll(t, HEAD_DIM, axis=1)

        o_ref[0:1, :] = dsh_r[0:1, :]
        o_ref[1:2, :] = dsc_r[0:1, :]
        o_ref[2:3, :] = dgate_r[0:1, :]
        o_ref[3:4, :] = dnw_r[0:1, :]
        o_ref[4:5, :] = dbg_r[0:1, 0:D_MODEL]
        o_ref[5:6, :] = dbg_r[0:1, D_MODEL:]
        o_ref[6:7, QN_COL:QN_COL + LANE] = both_heads(dq_r)
        o_ref[6:7, KN_COL:KN_COL + LANE] = both_heads(dk_r)
        o_ref[6:7, CB_COL:CB_COL + CONV_W] = dcb_r[0:1, :]
        o_ref[6:7, LOSS_COL:LOSS_COL + LANE] = loss_r[0:1, :]
        o_ref[6:7, LOSS_COL + LANE:] = jnp.zeros((1, D_MODEL - LOSS_COL - LANE), F32)
        o_ref[7:8, 0:CONV_W] = dlnw_r[0:1, :]
        o_ref[7:8, CONV_W:] = dlnb_r[0:1, :]

    return pl.pallas_call(body, name="pack_partials", out_shape=SDS((SMALL_ROWS, D_MODEL), F32),
                          compiler_params=_params())(dsh, dsc, dgate, dnw, dbg, *dqw3, *dkw3, dcb, dlnw, dlnb, loss_p)


def _adamw_update(g, w, m, v):
    bc1 = 1.0 - ADAM_B1 ** ADAM_STEP
    bc2 = 1.0 - ADAM_B2 ** ADAM_STEP
    m_new = ADAM_B1 * m + (1.0 - ADAM_B1) * g
    v_new = ADAM_B2 * v + (1.0 - ADAM_B2) * (g * g)
    delta = -ADAM_LR * ((m_new / bc1) / (jnp.sqrt(v_new / bc2) + ADAM_EPS) + ADAM_WD * w)
    return delta, m_new, v_new


def _adamw_small(small_all, ws, ms, vs):
    n = len(ws)
    where = [(slice(0, 3), None), (slice(3, 4), None), (slice(4, 6), None), (6, QN_COL), (6, KN_COL), (6, CB_COL),
             (7, 0), (7, CONV_W)]

    def body(*refs):
        g_ref = refs[0]
        w_r, m_r, v_r = refs[1:1 + n], refs[1 + n:1 + 2 * n], refs[1 + 2 * n:1 + 3 * n]
        outs = refs[1 + 3 * n:]
        g_o, d_o, m_o, v_o, loss_o = outs[:n], outs[n:2 * n], outs[2 * n:3 * n], outs[3 * n:4 * n], outs[4 * n]
        gsum = g_ref[0]
        for dev in range(1, N_DEV):
            gsum = gsum + g_ref[dev]
        loss_o[...] = gsum[6:7, LOSS_COL:LOSS_COL + LANE]
        for i, (rows, col) in enumerate(where):
            width = w_r[i].shape[1]
            if col is None:
                g = jnp.concatenate([gsum[r:r + 1, :] for r in range(rows.start, rows.stop)], axis=1)
            else:
                g = gsum[rows:rows + 1, col:col + width]
            delta, m_new, v_new = _adamw_update(g, w_r[i][...], m_r[i][...], v_r[i][...])
            g_o[i][...] = g
            d_o[i][...] = delta
            m_o[i][...] = m_new
            v_o[i][...] = v_new

    shapes = [SDS(w.shape, F32) for w in ws]
    res = pl.pallas_call(body, name="adamw_small", out_shape=shapes * 4 + [SDS((1, LANE), F32)],
                         compiler_params=_params())(small_all, *ws, *ms, *vs)
    return [res[k * n:(k + 1) * n] for k in range(4)], res[4 * n]


def _row_tile(rows):
    if rows <= 128:
        return rows
    if rows % 256 == 0:
        return 256
    return 128 if rows % 128 == 0 else SHARD_W // 4


def _adamw(gsrc, w, m, v, name, stacked):
    rows, cols = w.shape
    tr = _row_tile(rows)
    n_src = len(gsrc) if stacked else 1

    def body(*refs):
        g_refs, (w_ref, m_ref, v_ref, go_ref, d_ref, mo_ref, vo_ref) = refs[:n_src], refs[n_src:]
        if stacked:
            g = None
            for g_ref, (_, slots) in zip(g_refs, gsrc):
                for j in range(slots):
                    t = g_ref[j].astype(F32)
                    g = t if g is None else g + t
        else:
            g = g_refs[0][...]
        delta, m_new, v_new = _adamw_update(g, w_ref[...], m_ref[...], v_ref[...])
        go_ref[...] = g
        d_ref[...] = delta
        mo_ref[...] = m_new
        vo_ref[...] = v_new

    blk = pl.BlockSpec((tr, cols), lambda i: (i, 0))
    if stacked:
        gspecs = [pl.BlockSpec((slots, tr, arr.shape[2]), lambda i: (0, i, 0)) for arr, slots in gsrc]
        gargs = [arr for arr, _ in gsrc]
    else:
        gspecs, gargs = [blk], [gsrc]
    in_specs = gspecs + [blk, blk, blk]
    args = gargs + [w, m, v]
    return pl.pallas_call(
        body, name=name, grid=(rows // tr,), in_specs=in_specs, out_specs=[blk] * 4,
        out_shape=[SDS((rows, cols), F32)] * 4, compiler_params=_params(),
    )(*args)


def kernel(x, c, w_ada, b_ada, norm_w, w_in, b_gate, q_norm_w, k_norm_w, w_attn_proj, conv_w, conv_b, conv_ln_w, conv_ln_b, w_conv_proj, w_out, loss_target, m_w_ada, m_b_ada, m_norm_w, m_w_in, m_b_gate, m_q_norm_w, m_k_norm_w, m_w_attn_proj, m_conv_w, m_conv_b, m_conv_ln_w, m_conv_ln_b, m_w_conv_proj, m_w_out, v_w_ada, v_b_ada, v_norm_w, v_w_in, v_b_gate, v_q_norm_w, v_k_norm_w, v_w_attn_proj, v_conv_w, v_conv_b, v_conv_ln_w, v_conv_ln_b, v_w_conv_proj, v_w_out):
    xi, yi, ci = lax.axis_index("x"), lax.axis_index("y"), lax.axis_index("c")
    me = 4 * xi + 2 * yi + ci
    x2, tgt2 = x[0], loss_target[0]
    w_in_t, m_w_in_t, v_w_in_t = (jnp.transpose(a[0]) for a in (w_in, m_w_in, v_w_in))
    s = x2.shape[0]

    cw_flat = jnp.pad(conv_w[0].reshape(1, -1), ((0, 0), (0, CONVW_FLAT - CONV_K * HEAD_DIM)))
    pre = jnp.concatenate([c, cw_flat], axis=1).reshape(8, -1)
    (pre_all,) = _all_gather([pre], "gather_c_convw", vmem=True)
    pre_all = pre_all.reshape(N_DEV, -1)
    c_all = pre_all[:, :D_MODEL]
    convw_full = pre_all[:, D_MODEL:D_MODEL + CONV_K * HEAD_DIM].reshape(N_DEV, CONV_K, HEAD_DIM)
    convw_full = jnp.transpose(convw_full, (1, 0, 2)).reshape(CONV_K, CONV_W)
    convw_pad = jnp.pad(convw_full, ((0, CONV_HALO - CONV_K), (0, 0)))

    ada_part = _ada_fwd(c_all, w_ada[0])
    (ada_all,) = _all_gather([ada_part], "gather_ada", vmem=True)
    ada = lax.dynamic_index_in_dim(ada_all, me, axis=1, keepdims=False).reshape(1, 3 * D_MODEL) + b_ada
    shift, scale, gate = ada[:, :D_MODEL], ada[:, D_MODEL:2 * D_MODEL], ada[:, 2 * D_MODEL:]

    wt_g, wa_g, wc_g, wo_g = _all_gather_chips(
        [_cast_bf16(w_in_t, "cast_win"), _cast_bf16(w_attn_proj[0], "cast_wa"), _cast_bf16(w_conv_proj[0], "cast_wc"),
         _cast_bf16(w_out[0], "cast_wo")], "gather_weights")
    wt = wt_g.reshape(IN_W, D_MODEL)
    wa = _cols_from_slots(wa_g, "cols_wa")
    wc = _cols_from_slots(wc_g, "cols_wc")
    wo = wo_g.reshape(D_MODEL, D_MODEL)

    h, ht = _norm_fwd(x2, norm_w, scale, shift)
    proj = _mm_in(h, wt)
    qw2 = jnp.tile(q_norm_w, (1, 2))
    kw2 = jnp.tile(k_norm_w, (1, 2))
    o_all, l_all, qn, kn, vn = _attn_fwd(proj, qw2, kw2)
    o3, l3 = [o_all] * N_GROUPS, [l_all] * N_GROUPS
    head_id = jnp.arange(ATTN_W) // HEAD_DIM
    bd = (head_id[:, None] == head_id[None, :]).astype(BF16)
    (dout, da, lse_delta, dcv, mt, yat, yct, dmo, dya, dyc, dproj,
     dgate, dbg, dlnw, dlnb, dcb, loss_p) = _tail(
        x2, tgt2, proj, o3, l3, wa, wc, wo, gate, b_gate[:, :D_MODEL], b_gate[:, D_MODEL:], convw_pad,
        conv_b, conv_ln_w, conv_ln_b, bd)

    dproj, dconvw8 = _conv_bwd(dcv, proj, convw_pad, dproj)
    dconvw = jnp.sum(dconvw8.reshape(CONV_HALO, 8, CONV_W), axis=1)
    dproj, dqw_all, dkw_all = _attn_bwd(proj, qn, kn, vn, da, lse_delta, qw2, kw2, dproj)
    dqw_g3, dkw_g3 = [dqw_all], [dkw_all]
    dw_in_p = _mm_dw(ht, dproj).reshape(N_DEV, SHARD_W, D_MODEL)
    dwo_p = _mm_acc(mt, dmo, "mm_dwo", col_slots=False).reshape(N_DEV, D_MODEL // N_DEV, D_MODEL)
    dwa_p = _mm_acc(yat, dya, "mm_dwa", col_slots=True)
    dwc_p = _mm_acc(yct, dyc, "mm_dwc", col_slots=True)

    partials = [dw_in_p, dwa_p, dwc_p, dwo_p]
    me_arr = jnp.reshape(me, (1,)).astype(jnp.int32)
    from_sib = _exchange_sibling(partials, "exchange_sibling")
    presums = [_presum(p, f, me_arr, f"presum{i}") for i, (p, f) in enumerate(zip(partials, from_sib))]
    s_sems, r_sems, pre_thru, land_thru, token = _exchange_chips_start(presums, "exchange_chips_start")
    gx, dsh, dsc, dnw = _mm_dh_norm_bwd(dproj, wt, x2, dout, norm_w, scale, token)
    small_p = _pack_partials(dsh, dsc, dgate, dnw, dbg, dqw_g3, dkw_g3, dcb, dlnw, dlnb, loss_p)
    small_all, dconvw_all = _all_gather([small_p, dconvw], "gather_small", vmem=True)

    small_w = (b_ada, norm_w, b_gate, q_norm_w, k_norm_w, conv_b, conv_ln_w, conv_ln_b)
    small_m = (m_b_ada, m_norm_w, m_b_gate, m_q_norm_w, m_k_norm_w, m_conv_b, m_conv_ln_w, m_conv_ln_b)
    small_v = (v_b_ada, v_norm_w, v_b_gate, v_q_norm_w, v_k_norm_w, v_conv_b, v_conv_ln_w, v_conv_ln_b)
    r_small, loss_row = _adamw_small(small_all, small_w, small_m, small_v)
    dcw_mine = lax.dynamic_slice_in_dim(dconvw_all[:, :CONV_K, :], me * HEAD_DIM, HEAD_DIM, axis=2)
    r_convw = _adamw([(dcw_mine, N_DEV)], conv_w[0], m_conv_w[0], v_conv_w[0], "adamw_conv_w", stacked=True)

    d_ada_all = small_all[:, 0:3, :].reshape(N_DEV, 3 * D_MODEL)
    d_ada_cols = lax.dynamic_slice_in_dim(d_ada_all, me * (3 * D_MODEL // N_DEV), 3 * D_MODEL // N_DEV, axis=1)
    g_wada = _ada_bwd(c_all, d_ada_cols)
    r_ada = _adamw(g_wada, w_ada[0], m_w_ada[0], v_w_ada[0], "adamw_w_ada", stacked=False)
    pres, lands = _exchange_chips_wait(s_sems, r_sems, pre_thru, land_thru, r_ada[1], "exchange_chips_wait")
    terms = [[(p, 1), (l, len(CHIP_K))] for p, l in zip(pres, lands)]
    r_win = [jnp.transpose(r) for r in _adamw(terms[0], w_in_t, m_w_in_t, v_w_in_t, "adamw_w_in", stacked=True)]
    r_wap = _adamw(terms[1], w_attn_proj[0], m_w_attn_proj[0], v_w_attn_proj[0], "adamw_w_attn_proj", stacked=True)
    r_wcp = _adamw(terms[2], w_conv_proj[0], m_w_conv_proj[0], v_w_conv_proj[0], "adamw_w_conv_proj", stacked=True)
    r_wout = _adamw(terms[3], w_out[0], m_w_out[0], v_w_out[0], "adamw_w_out", stacked=True)

    outs = [loss_row[0, 0], gx[None]]
    for k in range(4):
        b_ada_k, norm_w_k, b_gate_k, qn_k, kn_k, conv_b_k, ln_w_k, ln_b_k = r_small[k]
        outs += [r_ada[k][None], b_ada_k, norm_w_k, r_win[k][None], b_gate_k, qn_k, kn_k, r_wap[k][None],
                 r_convw[k][None], conv_b_k, ln_w_k, ln_b_k, r_wcp[k][None], r_wout[k][None]]
    return tuple(outs)
```

```python
import functools

import jax
import jax.numpy as jnp
from jax import lax
from jax.experimental import pallas as pl
from jax.experimental.pallas import tpu as pltpu

F32 = jnp.float32
BF16 = jnp.bfloat16
SDS = jax.ShapeDtypeStruct
MESH = pl.DeviceIdType.MESH

N_DEV = 8
D_MODEL = 1024
HEAD_DIM = 64
N_GROUPS = 3
DILATIONS = (1, 4, 16)
BAND = 128
BWD_UNROLL = 8
ATTN_W = 512
CONV_W = 512
CONV_K = 31
CONV_HALO = 32
IN_W = 8704
SHARD_W = IN_W // N_DEV
PAIR_W = 2 * SHARD_W
Q0, K0, V0, ZA0, U0, ZC0, G0 = 0, 1536, 3072, 4608, 5120, 6144, 6656
EPS = 1e-6
LANE = 128
VMEM_LIMIT = 56 * 1024 * 1024

ADAM_LR, ADAM_B1, ADAM_B2, ADAM_EPS, ADAM_WD, ADAM_STEP = 0.001, 0.9, 0.999, 1e-08, 0.01, 10

CONVW_FLAT = 2048


def _params(**kw):
    return pltpu.CompilerParams(vmem_limit_bytes=VMEM_LIMIT, **kw)


def _sigmoid(z):
    return 0.5 * jnp.tanh(0.5 * z) + 0.5


def _dot(a, b):
    return jnp.dot(a, b, preferred_element_type=F32)


def _dot_nt(a, b):
    return lax.dot_general(a, b, (((1,), (1,)), ((), ())), preferred_element_type=F32)


def _dot_tn(a, b):
    return lax.dot_general(a, b, (((0,), (0,)), ((), ())), preferred_element_type=F32)


def _peer(x, y, c, k):
    px = 1 - x if (k >> 2) & 1 else x
    py = 1 - y if (k >> 1) & 1 else y
    pc = 1 - c if k & 1 else c
    return (px, py, pc), 4 * px + 2 * py + pc


def _all_gather(arrays, name, vmem):
    n = len(arrays)
    space = pltpu.VMEM if vmem else pl.ANY

    def body(*refs):
        ins, outs = refs[:n], refs[n:2 * n]
        send_sems, recv_sems, local_sems = refs[2 * n:]
        x, y, c = lax.axis_index("x"), lax.axis_index("y"), lax.axis_index("c")
        me = 4 * x + 2 * y + c
        locals_ = [pltpu.make_async_copy(ins[a], outs[a].at[me], local_sems.at[a]) for a in range(n)]
        for cp in locals_:
            cp.start()
        sends = []
        for k in range(1, N_DEV):
            peer, _ = _peer(x, y, c, k)
            for a in range(n):
                cp = pltpu.make_async_remote_copy(
                    src_ref=ins[a], dst_ref=outs[a].at[me], send_sem=send_sems.at[a, k - 1],
                    recv_sem=recv_sems.at[a, k - 1], device_id=peer, device_id_type=MESH)
                cp.start()
                sends.append(cp)
        for k in range(1, N_DEV):
            peer, pidx = _peer(x, y, c, k)
            for a in range(n):
                pltpu.make_async_remote_copy(
                    src_ref=ins[a], dst_ref=outs[a].at[pidx], send_sem=send_sems.at[a, k - 1],
                    recv_sem=recv_sems.at[a, k - 1], device_id=peer, device_id_type=MESH).wait_recv()
        for cp in sends:
            cp.wait_send()
        for cp in locals_:
            cp.wait()

    return pl.pallas_call(
        body, name=name,
        out_shape=[SDS((N_DEV,) + a.shape, a.dtype) for a in arrays],
        in_specs=[pl.BlockSpec(memory_space=space)] * n,
        out_specs=[pl.BlockSpec(memory_space=space)] * n,
        scratch_shapes=[pltpu.SemaphoreType.DMA((n, N_DEV - 1)), pltpu.SemaphoreType.DMA((n, N_DEV - 1)),
                        pltpu.SemaphoreType.DMA((n,))],
        compiler_params=_params(),
    )(*arrays)


CHIP_K = (2, 4, 6)


def _all_gather_chips(arrays, name):
    n = len(arrays)
    k_y, k_x, k_d = CHIP_K

    def body(*refs):
        ins, outs = refs[:n], refs[n:2 * n]
        send_sems, recv_sems, local_sems = refs[2 * n:]
        x, y, c = lax.axis_index("x"), lax.axis_index("y"), lax.axis_index("c")
        me = 4 * x + 2 * y + c
        sib, sib_idx = _peer(x, y, c, 1)
        nbr_y, idx_y = _peer(x, y, c, k_y)
        nbr_x, idx_x = _peer(x, y, c, k_x)
        _, idx_d = _peer(x, y, c, k_d)

        def copy(a, slot, block, to, src=None):
            return pltpu.make_async_remote_copy(
                src_ref=outs[a].at[block] if src is None else src, dst_ref=outs[a].at[block],
                send_sem=send_sems.at[a, slot], recv_sem=recv_sems.at[a, slot], device_id=to, device_id_type=MESH)

        locals_ = [pltpu.make_async_copy(ins[a], outs[a].at[me], local_sems.at[a]) for a in range(n)]
        for cp in locals_:
            cp.start()
        for a in range(n):
            copy(a, 0, me, sib, src=ins[a]).start()
            copy(a, 1, me, nbr_y, src=ins[a]).start()
            copy(a, 2, me, nbr_x, src=ins[a]).start()

        def arrived(slot, block, frm, send_on_to=None):
            for a in range(n):
                copy(a, slot, block, frm).wait_recv()
                if send_on_to is not None:
                    copy(a, 3, block, send_on_to).start()
                copy(a, 3 + slot, block, sib).start()

        @pl.when(c == 0)
        def _():
            arrived(1, idx_y, nbr_y, send_on_to=nbr_x)
            arrived(2, idx_x, nbr_x)

        @pl.when(c == 1)
        def _():
            arrived(2, idx_x, nbr_x, send_on_to=nbr_y)
            arrived(1, idx_y, nbr_y)

        arrived(3, idx_d, nbr_x)
        for a in range(n):
            copy(a, 0, sib_idx, sib).wait_recv()
        for slot, k in ((4, k_y), (5, k_x), (6, k_d)):
            _, pidx = _peer(x, y, 1 - c, k)
            for a in range(n):
                copy(a, slot, pidx, sib).wait_recv()
        for slot in range(N_DEV - 1):
            for a in range(n):
                copy(a, slot, me, sib).wait_send()
        for cp in locals_:
            cp.wait()

    return pl.pallas_call(
        body, name=name,
        out_shape=[SDS((N_DEV,) + a.shape, a.dtype) for a in arrays],
        in_specs=[pl.BlockSpec(memory_space=pl.ANY)] * n,
        out_specs=[pl.BlockSpec(memory_space=pl.ANY)] * n,
        scratch_shapes=[pltpu.SemaphoreType.DMA((n, N_DEV - 1)), pltpu.SemaphoreType.DMA((n, N_DEV - 1)),
                        pltpu.SemaphoreType.DMA((n,))],
        compiler_params=_params(),
    )(*arrays)


def _exchange_sibling(arrays, name):
    n = len(arrays)
    ks = (0,) + CHIP_K

    def body(*refs):
        ins, outs = refs[:n], refs[n:2 * n]
        send_sems, recv_sems = refs[2 * n:]
        x, y, c = lax.axis_index("x"), lax.axis_index("y"), lax.axis_index("c")
        sib, sib_idx = _peer(x, y, c, 1)
        sends = []
        for i, k in enumerate(ks):
            _, tgt = _peer(x, y, 1 - c, k) if k else (None, sib_idx)
            for a in range(n):
                cp = pltpu.make_async_remote_copy(
                    src_ref=ins[a].at[tgt], dst_ref=outs[a].at[i], send_sem=send_sems.at[a, i],
                    recv_sem=recv_sems.at[a, i], device_id=sib, device_id_type=MESH)
                cp.start()
                sends.append(cp)
        for cp in sends:
            cp.wait_recv()
        for cp in sends:
            cp.wait_send()

    return pl.pallas_call(
        body, name=name,
        out_shape=[SDS((len(ks),) + a.shape[1:], a.dtype) for a in arrays],
        in_specs=[pl.BlockSpec(memory_space=pl.ANY)] * n,
        out_specs=[pl.BlockSpec(memory_space=pl.ANY)] * n,
        scratch_shapes=[pltpu.SemaphoreType.DMA((n, len(ks))), pltpu.SemaphoreType.DMA((n, len(ks)))],
        compiler_params=_params(),
    )(*arrays)


def _sibling_copies(ins, outs, send_sems, recv_sems):
    x, y, c = lax.axis_index("x"), lax.axis_index("y"), lax.axis_index("c")
    sib, sib_idx = _peer(x, y, c, 1)
    ks = (0,) + CHIP_K
    copies = []
    for i, k in enumerate(ks):
        _, tgt = _peer(x, y, 1 - c, k) if k else (None, sib_idx)
        for a in range(len(ins)):
            copies.append(pltpu.make_async_remote_copy(
                src_ref=ins[a].at[tgt], dst_ref=outs[a].at[i], send_sem=send_sems.at[a * len(ks) + i],
                recv_sem=recv_sems.at[a * len(ks) + i], device_id=sib, device_id_type=MESH))
    return copies


def _exchange_sibling_start(arrays, name):
    n = len(arrays)
    ns = 1 + len(CHIP_K)

    def body(*refs):
        ins, outs = refs[:n], refs[n:2 * n]
        send_sems, recv_sems = refs[2 * n], refs[2 * n + 1]
        token = refs[-1]
        for cp in _sibling_copies(ins, outs, send_sems, recv_sems):
            cp.start()
        token[...] = jnp.zeros_like(token)

    hbm_in = [pltpu.HBM(a.shape, a.dtype) for a in arrays]
    hbm_out = [pltpu.HBM((ns,) + a.shape[1:], a.dtype) for a in arrays]
    res = pl.pallas_call(
        body, name=name,
        out_shape=(pltpu.SemaphoreType.DMA((n * ns,)), pltpu.SemaphoreType.DMA((n * ns,)), *hbm_in, *hbm_out,
                   SDS((8, LANE), F32)),
        in_specs=[HBM_SPEC] * (2 * n),
        out_specs=(SEM_SPEC, SEM_SPEC, *([HBM_SPEC] * (2 * n)), pl.BlockSpec(memory_space=pltpu.VMEM)),
        input_output_aliases={i: 2 + i for i in range(2 * n)},
        compiler_params=pltpu.CompilerParams(has_side_effects=SIDE_EFFECT),
    )(*[pltpu.with_memory_space_constraint(a, pltpu.HBM) for a in arrays],
      *[pltpu.with_memory_space_constraint(lax.empty((ns,) + a.shape[1:], a.dtype), pltpu.HBM) for a in arrays])
    return res[0], res[1], res[2:2 + n], res[2 + n:2 + 2 * n], res[-1]


def _exchange_sibling_wait(send_sems, recv_sems, ins_thru, outs_thru, after, name):
    n = len(ins_thru)

    def body(*refs):
        ins, outs = refs[:n], refs[n:2 * n]
        s_sems, r_sems = refs[2 * n], refs[2 * n + 1]
        for cp in _sibling_copies(ins, outs, s_sems, r_sems):
            cp.wait_send()
            cp.wait_recv()

    hbm = [pltpu.HBM(a.shape, a.dtype) for a in (*ins_thru, *outs_thru)]
    res = pl.pallas_call(
        body, name=name, out_shape=tuple(hbm),
        in_specs=[HBM_SPEC] * (2 * n) + [SEM_SPEC, SEM_SPEC, pl.BlockSpec(memory_space=pl.ANY)],
        out_specs=tuple([HBM_SPEC] * (2 * n)),
        input_output_aliases={i: i for i in range(2 * n)},
        compiler_params=pltpu.CompilerParams(has_side_effects=SIDE_EFFECT),
    )(*ins_thru, *outs_thru, send_sems, recv_sems, after)
    return res[:n], res[n:]


def _presum(mine, from_sib, me_arr, name):
    _, rows, cols = mine.shape
    tr = _row_tile(rows)
    ns = 1 + len(CHIP_K)

    def body(me_ref, a_ref, b_ref, o_ref):
        del me_ref
        o_ref[...] = (a_ref[...].astype(F32) + b_ref[...].astype(F32)).astype(o_ref.dtype)

    grid_spec = pltpu.PrefetchScalarGridSpec(
        num_scalar_prefetch=1, grid=(ns, rows // tr),
        in_specs=[pl.BlockSpec((1, tr, cols), lambda j, i, me: (jnp.bitwise_xor(me[0], 2 * j), i, 0)),
                  pl.BlockSpec((1, tr, cols), lambda j, i, me: (j, i, 0))],
        out_specs=pl.BlockSpec((1, tr, cols), lambda j, i, me: (j, i, 0)))
    return pl.pallas_call(body, name=name, grid_spec=grid_spec, out_shape=SDS((ns, rows, cols), mine.dtype),
                          compiler_params=_params())(me_arr, mine, from_sib)


HBM_SPEC = pl.BlockSpec(memory_space=pltpu.HBM)
SEM_SPEC = pl.BlockSpec(memory_space=pltpu.SEMAPHORE)
SIDE_EFFECT = pltpu.SideEffectType.DATAFLOW_SIDE_EFFECTING


def _chips_copies(pre_refs, land_refs, send_sems, recv_sems):
    x, y, c = lax.axis_index("x"), lax.axis_index("y"), lax.axis_index("c")
    copies = []
    for j, k in enumerate(CHIP_K):
        peer, _ = _peer(x, y, c, k)
        for a in range(len(pre_refs)):
            copies.append(pltpu.make_async_remote_copy(
                src_ref=pre_refs[a].at[1 + j], dst_ref=land_refs[a].at[j], send_sem=send_sems.at[a * len(CHIP_K) + j],
                recv_sem=recv_sems.at[a * len(CHIP_K) + j], device_id=peer, device_id_type=MESH))
    return copies


def _exchange_chips_start(presums, name):
    n = len(presums)

    def body(*refs):
        pre, land = refs[:n], refs[n:2 * n]
        send_sems, recv_sems = refs[2 * n], refs[2 * n + 1]
        token = refs[-1]
        for cp in _chips_copies(pre, land, send_sems, recv_sems):
            cp.start()
        token[...] = jnp.zeros_like(token)

    nk = len(CHIP_K)
    hbm = [pltpu.HBM(p.shape, p.dtype) for p in presums]
    hbm_land = [pltpu.HBM((nk,) + p.shape[1:], p.dtype) for p in presums]
    res = pl.pallas_call(
        body, name=name,
        out_shape=(pltpu.SemaphoreType.DMA((n * nk,)), pltpu.SemaphoreType.DMA((n * nk,)), *hbm, *hbm_land, SDS((8, LANE), F32)),
        in_specs=[HBM_SPEC] * (2 * n),
        out_specs=(SEM_SPEC, SEM_SPEC, *([HBM_SPEC] * (2 * n)), pl.BlockSpec(memory_space=pltpu.VMEM)),
        input_output_aliases={i: 2 + i for i in range(2 * n)},
        compiler_params=pltpu.CompilerParams(has_side_effects=SIDE_EFFECT),
    )(*[pltpu.with_memory_space_constraint(p, pltpu.HBM) for p in presums],
      *[pltpu.with_memory_space_constraint(lax.empty((nk,) + p.shape[1:], p.dtype), pltpu.HBM) for p in presums])
    return res[0], res[1], res[2:2 + n], res[2 + n:2 + 2 * n], res[-1]


def _exchange_chips_wait(send_sems, recv_sems, pre_thru, land_thru, after, name):
    n = len(pre_thru)

    def body(*refs):
        pre, land = refs[:n], refs[n:2 * n]
        s_sems, r_sems = refs[2 * n], refs[2 * n + 1]
        for cp in _chips_copies(pre, land, s_sems, r_sems):
            cp.wait_send()
            cp.wait_recv()

    hbm = [pltpu.HBM(p.shape, p.dtype) for p in (*pre_thru, *land_thru)]
    res = pl.pallas_call(
        body, name=name, out_shape=tuple(hbm),
        in_specs=[HBM_SPEC] * (2 * n) + [SEM_SPEC, SEM_SPEC, pl.BlockSpec(memory_space=pl.ANY)],
        out_specs=tuple([HBM_SPEC] * (2 * n)),
        input_output_aliases={i: i for i in range(2 * n)},
        compiler_params=pltpu.CompilerParams(has_side_effects=SIDE_EFFECT),
    )(*pre_thru, *land_thru, send_sems, recv_sems, after)
    return res[:n], res[n:]


def _cast_bf16(w, name):
    def body(w_ref, o_ref):
        o_ref[...] = w_ref[...].astype(BF16)

    return pl.pallas_call(body, name=name, out_shape=SDS(w.shape, BF16), compiler_params=_params())(w)


def _cols_from_slots(wg, name):
    _, rows, cols = wg.shape

    def body(w_ref, o_ref):
        for j in range(N_DEV):
            o_ref[:, j * cols:(j + 1) * cols] = w_ref[j]

    return pl.pallas_call(body, name=name, out_shape=SDS((rows, N_DEV * cols), wg.dtype), compiler_params=_params())(wg)


def _ada_fwd(c_all, w_ada):
    def body(c_ref, w_ref, o_ref):
        cv = c_ref[...]
        sc = (cv * _sigmoid(cv)).astype(BF16)
        o_ref[...] = _dot(sc, w_ref[...].astype(BF16))

    return pl.pallas_call(body, name="ada_fwd", out_shape=SDS((N_DEV, w_ada.shape[1]), F32),
                          compiler_params=_params())(c_all, w_ada)


def _ada_bwd(c_all, d_ada_cols):
    def body(c_ref, d_ref, o_ref):
        cv = c_ref[...]
        sc = (cv * _sigmoid(cv)).astype(BF16)
        o_ref[...] = _dot_tn(sc, d_ref[...].astype(BF16))

    return pl.pallas_call(body, name="ada_bwd", out_shape=SDS((D_MODEL, d_ada_cols.shape[1]), F32),
                          compiler_params=_params())(c_all, d_ada_cols)


def _norm_fwd(x, norm_w, scale, shift):
    s = x.shape[0]
    tr = 1024

    def body(x_ref, nw_ref, sc_ref, sh_ref, h_ref, ht_ref):
        xv = x_ref[...]
        r = lax.rsqrt(jnp.mean(xv * xv, axis=-1, keepdims=True) + EPS)
        h = (xv * r * nw_ref[...]) * (1.0 + sc_ref[...]) + sh_ref[...]
        h_ref[...] = h.astype(BF16)
        ht_ref[...] = h.T.astype(BF16)

    vec = pl.BlockSpec((1, D_MODEL), lambda i: (0, 0))
    return pl.pallas_call(
        body, name="norm_fwd", grid=(s // tr,),
        in_specs=[pl.BlockSpec((tr, D_MODEL), lambda i: (i, 0)), vec, vec, vec],
        out_specs=[pl.BlockSpec((tr, D_MODEL), lambda i: (i, 0)), pl.BlockSpec((D_MODEL, tr), lambda i: (0, i))],
        out_shape=[SDS((s, D_MODEL), BF16), SDS((D_MODEL, s), BF16)], compiler_params=_params(),
    )(x, norm_w, scale, shift)


def _mm_in(h, wt):
    s = h.shape[0]
    tm = 1024

    def body(h_ref, w_ref, o_ref):
        o_ref[...] = _dot_nt(h_ref[...], w_ref[...])

    return pl.pallas_call(
        body, name="mm_in", grid=(IN_W // PAIR_W, s // tm),
        in_specs=[pl.BlockSpec((tm, D_MODEL), lambda p, m: (m, 0)),
                  pl.BlockSpec((PAIR_W, D_MODEL), lambda p, m: (p, 0))],
        out_specs=pl.BlockSpec((tm, PAIR_W), lambda p, m: (m, p)),
        out_shape=SDS((s, IN_W), F32), compiler_params=_params(),
    )(h, wt)


def _head_ones():
    a = lax.broadcasted_iota(jnp.int32, (LANE, LANE), 0) // HEAD_DIM
    b = lax.broadcasted_iota(jnp.int32, (LANE, LANE), 1) // HEAD_DIM
    return (a == b).astype(BF16)


def _head_sums(t, ones):
    return _dot(t.astype(BF16), ones)


def _band_bias(bias, transposed=False):
    qi = lax.broadcasted_iota(jnp.int32, (2 * BAND, 2 * BAND), 1 if transposed else 0) % BAND
    kj = lax.broadcasted_iota(jnp.int32, (2 * BAND, 2 * BAND), 0 if transposed else 1)
    dist = qi + BAND - kj
    valid = (dist >= 0) & (dist <= BAND)
    bias[1] = jnp.where(valid, 0.0, -1e30)
    bias[0] = jnp.where(valid & (kj >= BAND), 0.0, -1e30)


def _token_rows(j, d, chunk, per_r):
    return pl.ds(j // per_r + (j % per_r) * (chunk * d), chunk, stride=d)


def _deinterleave_many(jobs, ones, d, sub_len, chunk, unroll):
    per_r = sub_len // chunk

    def step(j, _):
        tok = _token_rows(j, d, chunk, per_r)
        for src_ref, dst_ref, w_ref, scale, dst_off in jobs:
            t = src_ref[tok, :]
            if w_ref is not None:
                ms = _head_sums(t * t, ones) * (1.0 / HEAD_DIM)
                t = t * lax.rsqrt(ms + EPS) * (w_ref[...] * scale)
            dst_ref[pl.ds(pl.multiple_of(dst_off + j * chunk, BAND), chunk), :] = t.astype(dst_ref.dtype)
        return 0
    lax.fori_loop(0, d * per_r, step, 0, unroll=unroll)


def _deinterleave(src_ref, dst_ref, w_ref, ones, d, sub_len, chunk, scale, dst_off):
    _deinterleave_many([(src_ref, dst_ref, w_ref, scale, dst_off)], ones, d, sub_len, chunk, 4)


N_PAIRS = ATTN_W // LANE


def _attn_fwd(proj, qw2, kw2):
    s = proj.shape[0]

    def group_body(g, step, q_ref, k_ref, v_ref, qw_ref, kw_ref, o_ref, l_ref, qd, kd, vd, od, ld, bias):
        d = DILATIONS[g]
        sub_len = s // d
        nb = sub_len // BAND
        chunk = min(sub_len, 256)
        lo = lax.broadcasted_iota(jnp.int32, (1, LANE), 1) < HEAD_DIM
        ones = _head_ones()

        @pl.when(step == 0)
        def _():
            _band_bias(bias)

        kd[0:BAND, :] = jnp.zeros((BAND, LANE), BF16)
        vd[0:BAND, :] = jnp.zeros((BAND, LANE), BF16)
        _deinterleave_many([(q_ref, qd, qw_ref, HEAD_DIM ** -0.5, 0), (k_ref, kd, kw_ref, 1.0, BAND),
                            (v_ref, vd, None, 1.0, BAND)], ones, d, sub_len, chunk, 4)

        def block(t, _):
            base = pl.multiple_of(t * BAND, BAND)
            q = qd[pl.ds(base, BAND), :]
            k2 = kd[pl.ds(base, 2 * BAND), :]
            v2 = vd[pl.ds(base, 2 * BAND), :]
            zero = jnp.zeros_like(q)
            qs = jnp.concatenate([jnp.where(lo, q, zero), jnp.where(lo, zero, q)], axis=0)
            sc = _dot_nt(qs, k2) + bias[jnp.minimum(t % nb, 1)]
            m = jnp.max(sc, axis=-1, keepdims=True)
            p = jnp.exp(sc - m)
            den = jnp.sum(p, axis=-1, keepdims=True)
            u = _dot(p.astype(BF16), v2) * (1.0 / den)
            lse = m + jnp.log(den)
            od[pl.ds(base, BAND), :] = jnp.where(lo, u[:BAND], u[BAND:])
            ld[pl.ds(base, BAND), :] = jnp.where(lo, lse[:BAND], lse[BAND:])
            return 0
        lax.fori_loop(0, s // BAND, block, 0, unroll=16)

        per_r = sub_len // chunk

        def back(j, _):
            src = pl.ds(pl.multiple_of(j * chunk, chunk), chunk)
            dst = _token_rows(j, d, chunk, per_r)
            o_ref[dst, :] = od[src, :]
            l_ref[dst, :] = ld[src, :]
            return 0
        lax.fori_loop(0, d * per_r, back, 0, unroll=2)

    def body(*refs):
        step = pl.program_id(0)
        for g in range(N_GROUPS):
            pl.when(step // N_PAIRS == g)(functools.partial(group_body, g, step, *refs))

    col = lambda off: pl.BlockSpec((s, LANE), lambda i, off=off: (0, off // LANE + i))
    vec = pl.BlockSpec((1, LANE), lambda i: (0, 0))
    out = pl.BlockSpec((s, LANE), lambda i: (0, i))
    width = N_GROUPS * ATTN_W
    return pl.pallas_call(
        body, name="attn_fwd", grid=(N_GROUPS * N_PAIRS,),
        in_specs=[col(Q0), col(K0), col(V0), vec, vec],
        out_specs=[out, out, out, pl.BlockSpec((s + BAND, LANE), lambda i: (0, i)),
                   pl.BlockSpec((s + BAND, LANE), lambda i: (0, i))],
        out_shape=[SDS((s, width), F32)] * 2 + [SDS((s, width), BF16)] + [SDS((s + BAND, width), BF16)] * 2,
        scratch_shapes=[pltpu.VMEM((s, LANE), F32), pltpu.VMEM((s, LANE), F32),
                        pltpu.VMEM((2, 2 * BAND, 2 * BAND), F32)],
        compiler_params=_params(),
    )(proj, proj, proj, qw2, kw2)


def _attn_bwd(proj, qn, kn, vn, da, lse_delta, qw2, kw2, dproj):
    s = proj.shape[0]
    n_steps = N_GROUPS * N_PAIRS

    def group_body(g, hp, q_ref, k_ref, qn_ref, kd, vd, da_ref, ld_ref, qw_ref, kw_ref, dp_in, dp_out,
                   dqw_ref, dkw_ref, kdt, dad, lst, dlt, dqt, dqd, dkd, dvd, st, st_k, stb, bias_t, wacc, sem):
        del dp_in
        d = DILATIONS[g]
        sub_len = s // d
        nb = sub_len // BAND
        chunk = min(sub_len, 256)
        lo = lax.broadcasted_iota(jnp.int32, (1, LANE), 1) < HEAD_DIM
        row_lo = lax.broadcasted_iota(jnp.int32, (LANE, 1), 0) < HEAD_DIM
        ones = _head_ones()
        per_r = sub_len // chunk
        cblk = chunk // BAND

        @pl.when(hp == 0)
        def _():
            _band_bias(bias_t, transposed=True)

        kdt[0] = jnp.zeros((LANE, BAND), BF16)

        def k_step(t, _):
            kdt[1 + t] = kd[pl.ds(pl.multiple_of(BAND + t * BAND, BAND), BAND), :].astype(F32).T.astype(BF16)
            return 0
        lax.fori_loop(0, s // BAND, k_step, 0, unroll=4)
        _deinterleave(da_ref, dad, None, ones, d, sub_len, chunk, 1.0, 0)

        def rows_step(j, _):
            tok = _token_rows(j, d, chunk, per_r)
            tt = ld_ref[tok, :].T
            for u in range(cblk):
                cols = slice(u * BAND, (u + 1) * BAND)
                lst[j * cblk + u, 0:1, :] = tt[0:1, cols]
                lst[j * cblk + u, 1:2, :] = tt[HEAD_DIM:HEAD_DIM + 1, cols]
                dlt[j * cblk + u, 0:1, :] = tt[HEAD_DIM // 2:HEAD_DIM // 2 + 1, cols]
                dlt[j * cblk + u, 1:2, :] = tt[HEAD_DIM + HEAD_DIM // 2:HEAD_DIM + HEAD_DIM // 2 + 1, cols]
            return 0
        lax.fori_loop(0, d * per_r, rows_step, 0, unroll=4)

        def block(t, carry):
            ck, cv = carry
            base = pl.multiple_of(t * BAND, BAND)
            q = qn_ref[pl.ds(base, BAND), :]
            k2 = kd[pl.ds(base, 2 * BAND), :]
            v2 = vd[pl.ds(base, 2 * BAND), :]
            k2t = jnp.concatenate([kdt[t], kdt[t + 1]], axis=1)
            dav = dad[pl.ds(base, BAND), :]
            zero = jnp.zeros_like(q)
            qs = jnp.concatenate([jnp.where(lo, q, zero), jnp.where(lo, zero, q)], axis=0)
            das = jnp.concatenate([jnp.where(lo, dav, zero), jnp.where(lo, zero, dav)], axis=0)
            ls_row = jnp.concatenate([lst[t, 0:1, :], lst[t, 1:2, :]], axis=1)
            dl_row = jnp.concatenate([dlt[t, 0:1, :], dlt[t, 1:2, :]], axis=1)
            sc_t = _dot_nt(k2, qs) + bias_t[jnp.minimum(t % nb, 1)]
            p_t = jnp.exp(sc_t - ls_row)
            dp_t = _dot_nt(v2, das)
            ds_t = (p_t * (dp_t - dl_row)).astype(BF16)
            dv2 = _dot(p_t.astype(BF16), das)
            dk2 = _dot(ds_t, qs)
            dvd[pl.ds(base, BAND), :] = cv + dv2[:BAND]
            dkd[pl.ds(base, BAND), :] = ck + dk2[:BAND]
            dq_t = _dot(k2t, ds_t)
            dqt[t] = jnp.where(row_lo, dq_t[:, :BAND], dq_t[:, BAND:])
            return dk2[BAND:], dv2[BAND:]

        def blocks(i, carry):
            for u in range(BWD_UNROLL):
                carry = block(i * BWD_UNROLL + u, carry)
            return carry
        zeros = jnp.zeros((BAND, LANE), F32)
        ck, cv = lax.fori_loop(0, s // (BAND * BWD_UNROLL), blocks, (zeros, zeros))
        dkd[s:s + BAND, :] = ck
        dvd[s:s + BAND, :] = cv

        def dq_rows(t, _):
            dqd[pl.ds(pl.multiple_of(t * BAND, BAND), BAND), :] = dqt[t].T
            return 0
        lax.fori_loop(0, s // BAND, dq_rows, 0, unroll=4)

        def col_copy(slot, col0):
            return pltpu.make_async_copy(
                stb.at[slot], dp_out.at[:, pl.ds(pl.multiple_of(col0 + LANE * hp, LANE), LANE)], sem.at[slot])

        def store_cols(slot, col0, src):
            @pl.when(hp > 0)
            def _():
                col_copy(slot, col0).wait()
            stb[slot] = src[...].astype(BF16)
            col_copy(slot, col0).start()

        sides = ((q_ref, dqd, 0, qw_ref, HEAD_DIM ** -0.5, st), (k_ref, dkd, BAND, kw_ref, 1.0, st_k))
        wacc[...] = jnp.zeros_like(wacc)

        def norm_step(j, _):
            tok = _token_rows(j, d, chunk, per_r)
            for i, (src_ref, dy_ref, dy_off, w_ref, scale, dst) in enumerate(sides):
                t = src_ref[tok, :]
                dy = dy_ref[pl.ds(pl.multiple_of(dy_off + j * chunk, BAND), chunk), :]
                rr = lax.rsqrt(_head_sums(t * t, ones) * (1.0 / HEAD_DIM) + EPS)
                nrm = t * rr
                wacc[i] += jnp.sum((dy * nrm).reshape(chunk // 8, 8, LANE), axis=0)
                dn = dy * (w_ref[...] * scale)
                dst[tok, :] = rr * (dn - nrm * (_head_sums(dn * nrm, ones) * (1.0 / HEAD_DIM)))
            return 0
        lax.fori_loop(0, d * per_r, norm_step, 0, unroll=4)

        @pl.when(hp == 0)
        def _():
            dqw_ref[...] = jnp.zeros_like(dqw_ref)
            dkw_ref[...] = jnp.zeros_like(dkw_ref)

        for i, dw_ref in enumerate((dqw_ref, dkw_ref)):
            dw_ref[...] += jnp.broadcast_to(jnp.sum(wacc[i], axis=0, keepdims=True) * sides[i][4], dw_ref.shape)
        store_cols(0, Q0, st)
        store_cols(1, K0, st_k)

        def v_back(j, _):
            src = pl.ds(pl.multiple_of(BAND + j * chunk, BAND), chunk)
            st[_token_rows(j, d, chunk, per_r), :] = dvd[src, :]
            return 0
        lax.fori_loop(0, d * per_r, v_back, 0, unroll=2)
        store_cols(2, V0, st)

        @pl.when(hp == n_steps - 1)
        def _():
            for slot, col0 in enumerate((Q0, K0, V0)):
                col_copy(slot, col0).wait()

    def body(*refs):
        step = pl.program_id(0)
        for g in range(N_GROUPS):
            pl.when(step // N_PAIRS == g)(functools.partial(group_body, g, step, *refs))

    col = lambda off: pl.BlockSpec((s, LANE), lambda i, off=off: (0, off // LANE + i))
    mid = pl.BlockSpec((s, LANE), lambda i: (0, i))
    padded = pl.BlockSpec((s + BAND, LANE), lambda i: (0, i))
    slot4 = pl.BlockSpec((s, LANE), lambda i: (0, i % N_PAIRS))
    vec = pl.BlockSpec((1, LANE), lambda i: (0, 0))
    acc = pl.BlockSpec((8, LANE), lambda i: (0, 0))
    any_ = pl.BlockSpec(memory_space=pl.ANY)
    return pl.pallas_call(
        body, name="attn_bwd", grid=(n_steps,),
        in_specs=[col(Q0), col(K0), mid, padded, padded, slot4, slot4, vec, vec, any_],
        out_specs=[any_, acc, acc],
        out_shape=[SDS(dproj.shape, dproj.dtype), SDS((8, LANE), F32), SDS((8, LANE), F32)],
        input_output_aliases={9: 0},
        scratch_shapes=[pltpu.VMEM((s // BAND + 1, LANE, BAND), BF16), pltpu.VMEM((s, LANE), BF16),
                        pltpu.VMEM((s // BAND, 8, BAND), F32), pltpu.VMEM((s // BAND, 8, BAND), F32),
                        pltpu.VMEM((s // BAND, LANE, BAND), F32),
                        pltpu.VMEM((s, LANE), F32), pltpu.VMEM((s + BAND, LANE), F32), pltpu.VMEM((s + BAND, LANE), F32),
                        pltpu.VMEM((s, LANE), F32), pltpu.VMEM((s, LANE), F32), pltpu.VMEM((3, s, LANE), BF16),
                        pltpu.VMEM((2, 2 * BAND, 2 * BAND), F32), pltpu.VMEM((2, 8, LANE), F32),
                        pltpu.SemaphoreType.DMA((3,))],
        compiler_params=_params(),
    )(proj, proj, qn, kn, vn, da, lse_delta, qw2, kw2, dproj)


def _tap_views(ext_ref, sh_ref, offsets, tr, cols):
    for b in range(8):
        group = [j for j, o in enumerate(offsets) if o % 8 == b]
        if not group:
            continue
        first = min(offsets[j] for j in group)
        span = tr + max(offsets[j] for j in group) - first
        sh_ref[0:span, cols] = ext_ref[first:first + span, cols]
        for j in group:
            yield j, sh_ref[offsets[j] - first:offsets[j] - first + tr, cols]


def _silu_grad(z, sg):
    return sg * (1.0 + z * (1.0 - sg))


def _glu(u):
    a_h, b_h = u[:, :CONV_W], u[:, CONV_W:]
    sg = _sigmoid(b_h)
    return a_h, sg, a_h * sg


def _tail(x, tgt, proj, o3, l3, wa, wc, wo, gate, bga, bgc, convw, convb, lnw, lnb, bd):
    s = x.shape[0]
    tr = 256

    def body(x_ref, t_ref, za_ref, u_ref, uh_ref, zc_ref, g0_ref, g1_ref, g2_ref, g3_ref,
             o0_ref, o1_ref, o2_ref, l0_ref, l1_ref, l2_ref, wa_ref, wc_ref, wo_ref,
             gate_ref, bga_ref, bgc_ref, cw_ref, cb_ref, lnw_ref, lnb_ref, bd_ref,
             dout_ref, da_ref, ld_ref, dcv_ref, mt_ref, yat_ref, yct_ref, dmo_ref, dya_ref, dyc_ref, dp_ref,
             dgate_ref, dbg_ref, dlnw_ref, dlnb_ref, dcb_ref, loss_ref,
             ext, sh, st_za, st_zc, st_g, sems):
        i = pl.program_id(0)

        @pl.when(i == 0)
        def _():
            for r in (dgate_ref, dbg_ref, dlnw_ref, dlnb_ref, dcb_ref, loss_ref):
                r[...] = jnp.zeros_like(r)

        def acc_rows(ref, v):
            ref[...] += jnp.broadcast_to(jnp.sum(v, axis=0, keepdims=True), ref.shape)

        la, lb, lc = l0_ref[...], l1_ref[...], l2_ref[...]
        mx = jnp.maximum(jnp.maximum(la, lb), lc)
        ea, eb, ec = jnp.exp(la - mx), jnp.exp(lb - mx), jnp.exp(lc - mx)
        den = ea + eb + ec
        inv = 1.0 / den
        attn = (ea * inv) * o0_ref[...] + (eb * inv) * o1_ref[...] + (ec * inv) * o2_ref[...]
        lse = mx + jnp.log(den)

        za = za_ref[...]
        sga = _sigmoid(za)
        sa = za * sga
        ya_in = attn * sa
        y_attn = _dot(ya_in.astype(BF16), wa_ref[...])

        _, _, glu = _glu(u_ref[...])
        _, _, glu_h = _glu(uh_ref[...])
        ext[0:CONV_HALO, :] = jnp.where(i > 0, glu_h, 0.0)
        ext[CONV_HALO:CONV_HALO + tr, :] = glu
        cv_blocks = []
        for cb in range(CONV_W // LANE):
            cols = slice(cb * LANE, (cb + 1) * LANE)
            cv_c = jnp.broadcast_to(cb_ref[:, cols], (tr, LANE))
            for j, rows in _tap_views(ext, sh, [CONV_HALO - (CONV_K - 1) + j for j in range(CONV_K)], tr, cols):
                cv_c = cv_c + cw_ref[j:j + 1, cols] * rows
            cv_blocks.append(cv_c)
        cv = jnp.concatenate(cv_blocks, axis=1)
        mu = jnp.mean(cv, axis=-1, keepdims=True)
        xc = cv - mu
        rstd = lax.rsqrt(jnp.mean(xc * xc, axis=-1, keepdims=True) + EPS)
        nrm = xc * rstd
        ln = nrm * lnw_ref[...] + lnb_ref[...]
        sgl = _sigmoid(ln)
        cs = ln * sgl
        zc = zc_ref[...]
        sgc = _sigmoid(zc)
        scz = zc * sgc
        yc_in = cs * scz
        y_conv = _dot(yc_in.astype(BF16), wc_ref[...])

        ga = _sigmoid(jnp.concatenate([g0_ref[...], g1_ref[...]], axis=1) + bga_ref[...])
        gc = _sigmoid(jnp.concatenate([g2_ref[...], g3_ref[...]], axis=1) + bgc_ref[...])
        merged = ga * y_attn + gc * y_conv
        mo = _dot(merged.astype(BF16), wo_ref[...])
        gate_v = gate_ref[...]
        err = (x_ref[...] + gate_v * mo) - t_ref[...]
        loss_ref[...] += 0.5 * jnp.sum(jnp.mean(err * err, axis=-1, keepdims=True))
        d_out = err * (1.0 / D_MODEL)
        dout_ref[...] = d_out

        rows = pl.ds(pl.multiple_of(i * tr, tr), tr)
        cps = [pltpu.make_async_copy(st_za, dp_ref.at[rows, pl.ds(ZA0, ATTN_W)], sems.at[0]),
               pltpu.make_async_copy(st_zc, dp_ref.at[rows, pl.ds(ZC0, CONV_W)], sems.at[1]),
               pltpu.make_async_copy(st_g, dp_ref.at[rows, pl.ds(G0, 2 * D_MODEL)], sems.at[2])]

        @pl.when(i > 0)
        def _():
            for cp in cps:
                cp.wait()

        acc_rows(dgate_ref, d_out * mo)
        dmo_b = (d_out * gate_v).astype(BF16)
        dmo_ref[...] = dmo_b
        mt_ref[...] = merged.T.astype(BF16)
        d_merged = _dot_nt(dmo_b, wo_ref[...])
        d_ya = (d_merged * ga).astype(BF16)
        d_yc = (d_merged * gc).astype(BF16)
        dya_ref[...] = d_ya
        dyc_ref[...] = d_yc
        dga = d_merged * y_attn * (ga * (1.0 - ga))
        dgc = d_merged * y_conv * (gc * (1.0 - gc))
        dgs = jnp.concatenate([dga, dgc], axis=1)
        acc_rows(dbg_ref, dgs)
        st_g[...] = dgs.astype(BF16)

        yat_ref[...] = ya_in.T.astype(BF16)
        d_ya_in = _dot_nt(d_ya, wa_ref[...])
        d_attn = d_ya_in * sa
        da_ref[...] = d_attn
        st_za[...] = (d_ya_in * attn * _silu_grad(za, sga)).astype(BF16)
        prod = d_attn * attn
        hi = prod.astype(BF16)
        lo_ = (prod - hi.astype(F32)).astype(BF16)
        delta = _dot(hi, bd_ref[...]) + _dot(lo_, bd_ref[...])
        first_half = (lax.broadcasted_iota(jnp.int32, (1, ATTN_W), 1) % HEAD_DIM) < HEAD_DIM // 2
        ld_ref[...] = jnp.where(first_half, lse, delta)

        yct_ref[...] = yc_in.T.astype(BF16)
        d_yc_in = _dot_nt(d_yc, wc_ref[...])
        st_zc[...] = (d_yc_in * cs * _silu_grad(zc, sgc)).astype(BF16)
        d_ln = (d_yc_in * scz) * _silu_grad(ln, sgl)
        acc_rows(dlnw_ref, d_ln * nrm)
        acc_rows(dlnb_ref, d_ln)
        d_nrm = d_ln * lnw_ref[...]
        d_cv = rstd * (d_nrm - jnp.mean(d_nrm, axis=-1, keepdims=True)
                       - nrm * jnp.mean(d_nrm * nrm, axis=-1, keepdims=True))
        acc_rows(dcb_ref, d_cv)
        dcv_ref[...] = d_cv

        for cp in cps:
            cp.start()

        @pl.when(i == s // tr - 1)
        def _():
            for cp in cps:
                cp.wait()

    def rows(width, colblk=0):
        return pl.BlockSpec((tr, width), lambda i, colblk=colblk: (i, colblk))

    def const(shape):
        return pl.BlockSpec(shape, lambda i: (0,) * len(shape))

    halo = pl.BlockSpec((CONV_HALO, D_MODEL), lambda i: (jnp.maximum(i * (tr // CONV_HALO) - 1, 0), U0 // D_MODEL))
    in_specs = [rows(D_MODEL), rows(D_MODEL), rows(ATTN_W, ZA0 // ATTN_W), rows(D_MODEL, U0 // D_MODEL), halo,
                rows(CONV_W, ZC0 // CONV_W)]
    in_specs += [rows(512, G0 // 512 + j) for j in range(4)]
    in_specs += [rows(ATTN_W, g) for g in range(N_GROUPS)] * 2
    in_specs += [const(wa.shape), const(wc.shape), const(wo.shape), const((1, D_MODEL)), const((1, D_MODEL)),
                 const((1, D_MODEL)), const(convw.shape), const((1, CONV_W)), const((1, CONV_W)), const((1, CONV_W)),
                 const(bd.shape)]
    tcol = lambda width: pl.BlockSpec((width, tr), lambda i: (0, i))
    out_specs = [rows(D_MODEL), rows(ATTN_W), rows(ATTN_W), rows(CONV_W),
                 tcol(D_MODEL), tcol(ATTN_W), tcol(CONV_W), rows(D_MODEL), rows(D_MODEL), rows(D_MODEL),
                 pl.BlockSpec(memory_space=pl.ANY),
                 const((8, D_MODEL)), const((8, 2 * D_MODEL)), const((8, CONV_W)), const((8, CONV_W)), const((8, CONV_W)),
                 const((8, LANE))]
    out_shape = [SDS((s, D_MODEL), F32), SDS((s, ATTN_W), F32), SDS((s, ATTN_W), F32),
                 SDS((s, CONV_W), F32),
                 SDS((D_MODEL, s), BF16), SDS((ATTN_W, s), BF16), SDS((CONV_W, s), BF16),
                 SDS((s, D_MODEL), BF16), SDS((s, D_MODEL), BF16), SDS((s, D_MODEL), BF16),
                 SDS((s, IN_W), BF16),
                 SDS((8, D_MODEL), F32), SDS((8, 2 * D_MODEL), F32), SDS((8, CONV_W), F32), SDS((8, CONV_W), F32),
                 SDS((8, CONV_W), F32), SDS((8, LANE), F32)]
    return pl.pallas_call(
        body, name="tail", grid=(s // tr,), in_specs=in_specs, out_specs=out_specs, out_shape=out_shape,
        scratch_shapes=[pltpu.VMEM((CONV_HALO + tr, CONV_W), F32), pltpu.VMEM((CONV_HALO + tr, CONV_W), F32),
                        pltpu.VMEM((tr, ATTN_W), BF16),
                        pltpu.VMEM((tr, CONV_W), BF16), pltpu.VMEM((tr, 2 * D_MODEL), BF16),
                        pltpu.SemaphoreType.DMA((3,))],
        compiler_params=_params(),
    )(x, tgt, proj, proj, proj, proj, proj, proj, proj, proj, *o3, *l3, wa, wc, wo, gate, bga, bgc,
      convw, convb, lnw, lnb, bd)


def _conv_bwd(dcv, proj, convw, dproj):
    s = dcv.shape[0]
    tr = 128
    nt = s // tr

    def body(dcv_ref, dcvn_ref, u_ref, uh_ref, cw_ref, dp_in, dp_out, dw_ref, extg, extd, sh):
        del dp_in
        i = pl.program_id(0)

        @pl.when(i == 0)
        def _():
            dw_ref[...] = jnp.zeros_like(dw_ref)

        _, _, glu = _glu(u_ref[...])
        _, _, glu_h = _glu(uh_ref[...])
        extg[0:CONV_HALO, :] = jnp.where(i > 0, glu_h, 0.0)
        extg[CONV_HALO:CONV_HALO + tr, :] = glu
        extd[0:tr, :] = dcv_ref[...]
        extd[tr:tr + CONV_HALO, :] = jnp.where(i < nt - 1, dcvn_ref[...], 0.0)
        for cb in range(CONV_W // LANE):
            cols = slice(cb * LANE, (cb + 1) * LANE)
            dglu = jnp.zeros((tr, LANE), F32)
            for j, rows in _tap_views(extd, sh, [CONV_K - 1 - j for j in range(CONV_K)], tr, cols):
                dglu = dglu + cw_ref[j:j + 1, cols] * rows
            dcv_c = dcv_ref[:, cols]
            for j, rows in _tap_views(extg, sh, [CONV_HALO - (CONV_K - 1) + j for j in range(CONV_K)], tr, cols):
                dw_ref[8 * j:8 * j + 8, cols] += jnp.sum((dcv_c * rows).reshape(tr // 8, 8, LANE), axis=0)
            a_h = u_ref[:, cols]
            sgb = _sigmoid(u_ref[:, CONV_W + cb * LANE:CONV_W + (cb + 1) * LANE])
            dp_out[:, cols] = (dglu * sgb).astype(BF16)
            dp_out[:, CONV_W + cb * LANE:CONV_W + (cb + 1) * LANE] = (dglu * a_h * (sgb * (1.0 - sgb))).astype(BF16)

    ucol = U0 // D_MODEL
    return pl.pallas_call(
        body, name="conv_bwd", grid=(nt,),
        in_specs=[pl.BlockSpec((tr, CONV_W), lambda i: (i, 0)),
                  pl.BlockSpec((CONV_HALO, CONV_W), lambda i: (jnp.minimum((i + 1) * (tr // CONV_HALO), s // CONV_HALO - 1), 0)),
                  pl.BlockSpec((tr, D_MODEL), lambda i: (i, ucol)),
                  pl.BlockSpec((CONV_HALO, D_MODEL), lambda i: (jnp.maximum(i * (tr // CONV_HALO) - 1, 0), ucol)),
                  pl.BlockSpec(convw.shape, lambda i: (0, 0)),
                  pl.BlockSpec(memory_space=pl.ANY)],
        out_specs=[pl.BlockSpec((tr, D_MODEL), lambda i: (i, ucol)), pl.BlockSpec((8 * CONV_HALO, CONV_W), lambda i: (0, 0))],
        out_shape=[SDS(dproj.shape, dproj.dtype), SDS((8 * CONV_HALO, CONV_W), F32)],
        input_output_aliases={5: 0},
        scratch_shapes=[pltpu.VMEM((CONV_HALO + tr, CONV_W), F32)] * 3,
        compiler_params=_params(),
    )(dcv, dcv, proj, proj, convw, dproj)


def _mm_acc(at, b, token, name, col_slots):
    m, s = at.shape
    n = b.shape[1]
    tk = 2048
    nk = s // tk

    def body(a_ref, b_ref, tok_ref, o_ref, acc):
        del tok_ref
        k = pl.program_id(0)

        @pl.when(k == 0)
        def _():
            acc[...] = jnp.zeros_like(acc)

        acc[...] += _dot(a_ref[...], b_ref[...])

        @pl.when(k == nk - 1)
        def _():
            if col_slots:
                w = n // N_DEV
                for j in range(N_DEV):
                    o_ref[j] = acc[:, j * w:(j + 1) * w].astype(BF16)
            else:
                o_ref[...] = acc[...].astype(BF16)

    if col_slots:
        out_shape = SDS((N_DEV, m, n // N_DEV), BF16)
        out_spec = pl.BlockSpec((N_DEV, m, n // N_DEV), lambda k: (0, 0, 0))
    else:
        out_shape = SDS((m, n), BF16)
        out_spec = pl.BlockSpec((m, n), lambda k: (0, 0))
    return pl.pallas_call(
        body, name=name, grid=(nk,),
        in_specs=[pl.BlockSpec((m, tk), lambda k: (0, k)), pl.BlockSpec((tk, n), lambda k: (k, 0)),
                  pl.BlockSpec(token.shape, lambda k: (0, 0))],
        out_specs=out_spec, out_shape=out_shape, scratch_shapes=[pltpu.VMEM((m, n), F32)],
        compiler_params=_params(),
    )(at, b, token)


def _mm_dw(ht, dproj):
    s = ht.shape[1]
    tk = 2048
    nk = s // tk

    def body(a_ref, b_ref, o_ref, acc):
        k = pl.program_id(1)

        @pl.when(k == 0)
        def _():
            acc[...] = jnp.zeros_like(acc)

        acc[...] += _dot(a_ref[...], b_ref[...])

        @pl.when(k == nk - 1)
        def _():
            o_ref[...] = acc[...].T.astype(BF16)

    return pl.pallas_call(
        body, name="mm_dw", grid=(IN_W // PAIR_W, nk),
        in_specs=[pl.BlockSpec((D_MODEL, tk), lambda p, k: (0, k)), pl.BlockSpec((tk, PAIR_W), lambda p, k: (k, p))],
        out_specs=pl.BlockSpec((PAIR_W, D_MODEL), lambda p, k: (p, 0)),
        out_shape=SDS((IN_W, D_MODEL), BF16), scratch_shapes=[pltpu.VMEM((D_MODEL, PAIR_W), F32)],
        compiler_params=_params(),
    )(ht, dproj)


def _mm_dh_norm_bwd(dproj, wt, x, dout, norm_w, scale, token):
    s = dproj.shape[0]
    tm = 1024
    n_p = IN_W // PAIR_W

    def body(dp_ref, w_ref, x_ref, do_ref, nw_ref, sc_ref, tok_ref, gx_ref, dsh_ref, dsc_ref, dnw_ref, dh_acc):
        del tok_ref
        m, p = pl.program_id(0), pl.program_id(1)
        part = _dot(dp_ref[...], w_ref[...])

        @pl.when(p == 0)
        def _():
            dh_acc[...] = part

        @pl.when(p > 0)
        def _():
            dh_acc[...] += part

        @pl.when((m == 0) & (p == 0))
        def _():
            for r in (dsh_ref, dsc_ref, dnw_ref):
                r[...] = jnp.zeros_like(r)

        @pl.when(p == n_p - 1)
        def _():
            def acc_rows(ref, v):
                ref[...] += jnp.broadcast_to(jnp.sum(v, axis=0, keepdims=True), ref.shape)

            xv = x_ref[...]
            dh_v = dh_acc[...]
            r = lax.rsqrt(jnp.mean(xv * xv, axis=-1, keepdims=True) + EPS)
            xn = xv * r
            one_sc = 1.0 + sc_ref[...]
            acc_rows(dsh_ref, dh_v)
            acc_rows(dsc_ref, dh_v * (xn * nw_ref[...]))
            acc_rows(dnw_ref, dh_v * xn * one_sc)
            dxn = dh_v * (nw_ref[...] * one_sc)
            gx_ref[...] = do_ref[...] + r * (dxn - xn * jnp.mean(dxn * xn, axis=-1, keepdims=True))

    rows = pl.BlockSpec((tm, D_MODEL), lambda m, p: (m, 0))
    vec = pl.BlockSpec((1, D_MODEL), lambda m, p: (0, 0))
    acc = pl.BlockSpec((8, D_MODEL), lambda m, p: (0, 0))
    return pl.pallas_call(
        body, name="mm_dh_norm_bwd", grid=(s // tm, n_p),
        in_specs=[pl.BlockSpec((tm, PAIR_W), lambda m, p: (m, p)),
                  pl.BlockSpec((PAIR_W, D_MODEL), lambda m, p: (p, 0)),
                  rows, rows, vec, vec, pl.BlockSpec(token.shape, lambda m, p: (0, 0))],
        out_specs=[rows, acc, acc, acc],
        out_shape=[SDS((s, D_MODEL), F32)] + [SDS((8, D_MODEL), F32)] * 3,
        scratch_shapes=[pltpu.VMEM((tm, D_MODEL), F32)], compiler_params=_params(),
    )(dproj, wt, x, dout, norm_w, scale, token)


SMALL_ROWS = 8
QN_COL, KN_COL, CB_COL, LOSS_COL = 0, LANE, 2 * LANE, 2 * LANE + CONV_W


def _pack_partials(dsh, dsc, dgate, dnw, dbg, dqw3, dkw3, dcb, dlnw, dlnb, loss_p):
    n3 = len(dqw3)

    def body(*refs):
        dsh_r, dsc_r, dgate_r, dnw_r, dbg_r = refs[:5]
        dq_r, dk_r = refs[5:5 + n3], refs[5 + n3:5 + 2 * n3]
        dcb_r, dlnw_r, dlnb_r, loss_r, o_ref = refs[5 + 2 * n3:]

        def both_heads(rs):
            t = rs[0][0:1, :]
            for r in rs[1:]:
                t = t + r[0:1, :]
            return t + pltpu.roll(t, HEAD_DIM, axis=1)

        o_ref[0:1, :] = dsh_r[0:1, :]
        o_ref[1:2, :] = dsc_r[0:1, :]
        o_ref[2:3, :] = dgate_r[0:1, :]
        o_ref[3:4, :] = dnw_r[0:1, :]
        o_ref[4:5, :] = dbg_r[0:1, 0:D_MODEL]
        o_ref[5:6, :] = dbg_r[0:1, D_MODEL:]
        o_ref[6:7, QN_COL:QN_COL + LANE] = both_heads(dq_r)
        o_ref[6:7, KN_COL:KN_COL + LANE] = both_heads(dk_r)
        o_ref[6:7, CB_COL:CB_COL + CONV_W] = dcb_r[0:1, :]
        o_ref[6:7, LOSS_COL:LOSS_COL + LANE] = loss_r[0:1, :]
        o_ref[6:7, LOSS_COL + LANE:] = jnp.zeros((1, D_MODEL - LOSS_COL - LANE), F32)
        o_ref[7:8, 0:CONV_W] = dlnw_r[0:1, :]
        o_ref[7:8, CONV_W:] = dlnb_r[0:1, :]

    return pl.pallas_call(body, name="pack_partials", out_shape=SDS((SMALL_ROWS, D_MODEL), F32),
                          compiler_params=_params())(dsh, dsc, dgate, dnw, dbg, *dqw3, *dkw3, dcb, dlnw, dlnb, loss_p)


def _adamw_update(g, w, m, v):
    bc1 = 1.0 - ADAM_B1 ** ADAM_STEP
    bc2 = 1.0 - ADAM_B2 ** ADAM_STEP
    m_new = ADAM_B1 * m + (1.0 - ADAM_B1) * g
    v_new = ADAM_B2 * v + (1.0 - ADAM_B2) * (g * g)
    delta = -ADAM_LR * ((m_new / bc1) / (jnp.sqrt(v_new / bc2) + ADAM_EPS) + ADAM_WD * w)
    return delta, m_new, v_new


def _adamw_small(small_all, ws, ms, vs):
    n = len(ws)
    where = [(slice(0, 3), None), (slice(3, 4), None), (slice(4, 6), None), (6, QN_COL), (6, KN_COL), (6, CB_COL),
             (7, 0), (7, CONV_W)]

    def body(*refs):
        g_ref = refs[0]
        w_r, m_r, v_r = refs[1:1 + n], refs[1 + n:1 + 2 * n], refs[1 + 2 * n:1 + 3 * n]
        outs = refs[1 + 3 * n:]
        g_o, d_o, m_o, v_o, loss_o = outs[:n], outs[n:2 * n], outs[2 * n:3 * n], outs[3 * n:4 * n], outs[4 * n]
        gsum = g_ref[0]
        for dev in range(1, N_DEV):
            gsum = gsum + g_ref[dev]
        loss_o[...] = gsum[6:7, LOSS_COL:LOSS_COL + LANE]
        for i, (rows, col) in enumerate(where):
            width = w_r[i].shape[1]
            if col is None:
                g = jnp.concatenate([gsum[r:r + 1, :] for r in range(rows.start, rows.stop)], axis=1)
            else:
                g = gsum[rows:rows + 1, col:col + width]
            delta, m_new, v_new = _adamw_update(g, w_r[i][...], m_r[i][...], v_r[i][...])
            g_o[i][...] = g
            d_o[i][...] = delta
            m_o[i][...] = m_new
            v_o[i][...] = v_new

    shapes = [SDS(w.shape, F32) for w in ws]
    res = pl.pallas_call(body, name="adamw_small", out_shape=shapes * 4 + [SDS((1, LANE), F32)],
                         compiler_params=_params())(small_all, *ws, *ms, *vs)
    return [res[k * n:(k + 1) * n] for k in range(4)], res[4 * n]


def _row_tile(rows):
    if rows <= 128:
        return rows
    if rows % 256 == 0:
        return 256
    return 128 if rows % 128 == 0 else SHARD_W // 4


def _adamw(gsrc, w, m, v, name, stacked):
    rows, cols = w.shape
    tr = _row_tile(rows)
    n_src = len(gsrc) if stacked else 1

    def body(*refs):
        g_refs, (w_ref, m_ref, v_ref, go_ref, d_ref, mo_ref, vo_ref) = refs[:n_src], refs[n_src:]
        if stacked:
            g = None
            for g_ref, (_, slots) in zip(g_refs, gsrc):
                for j in range(slots):
                    t = g_ref[j].astype(F32)
                    g = t if g is None else g + t
        else:
            g = g_refs[0][...]
        delta, m_new, v_new = _adamw_update(g, w_ref[...], m_ref[...], v_ref[...])
        go_ref[...] = g
        d_ref[...] = delta
        mo_ref[...] = m_new
        vo_ref[...] = v_new

    blk = pl.BlockSpec((tr, cols), lambda i: (i, 0))
    if stacked:
        gspecs = [pl.BlockSpec((slots, tr, arr.shape[2]), lambda i: (0, i, 0)) for arr, slots in gsrc]
        gargs = [arr for arr, _ in gsrc]
    else:
        gspecs, gargs = [blk], [gsrc]
    in_specs = gspecs + [blk, blk, blk]
    args = gargs + [w, m, v]
    return pl.pallas_call(
        body, name=name, grid=(rows // tr,), in_specs=in_specs, out_specs=[blk] * 4,
        out_shape=[SDS((rows, cols), F32)] * 4, compiler_params=_params(),
    )(*args)


def kernel(x, c, w_ada, b_ada, norm_w, w_in, b_gate, q_norm_w, k_norm_w, w_attn_proj, conv_w, conv_b, conv_ln_w, conv_ln_b, w_conv_proj, w_out, loss_target, m_w_ada, m_b_ada, m_norm_w, m_w_in, m_b_gate, m_q_norm_w, m_k_norm_w, m_w_attn_proj, m_conv_w, m_conv_b, m_conv_ln_w, m_conv_ln_b, m_w_conv_proj, m_w_out, v_w_ada, v_b_ada, v_norm_w, v_w_in, v_b_gate, v_q_norm_w, v_k_norm_w, v_w_attn_proj, v_conv_w, v_conv_b, v_conv_ln_w, v_conv_ln_b, v_w_conv_proj, v_w_out):
    xi, yi, ci = lax.axis_index("x"), lax.axis_index("y"), lax.axis_index("c")
    me = 4 * xi + 2 * yi + ci
    x2, tgt2 = x[0], loss_target[0]
    w_in_t, m_w_in_t, v_w_in_t = (jnp.transpose(a[0]) for a in (w_in, m_w_in, v_w_in))
    s = x2.shape[0]

    cw_flat = jnp.pad(conv_w[0].reshape(1, -1), ((0, 0), (0, CONVW_FLAT - CONV_K * HEAD_DIM)))
    pre = jnp.concatenate([c, cw_flat], axis=1).reshape(8, -1)
    (pre_all,) = _all_gather([pre], "gather_c_convw", vmem=True)
    pre_all = pre_all.reshape(N_DEV, -1)
    c_all = pre_all[:, :D_MODEL]
    convw_full = pre_all[:, D_MODEL:D_MODEL + CONV_K * HEAD_DIM].reshape(N_DEV, CONV_K, HEAD_DIM)
    convw_full = jnp.transpose(convw_full, (1, 0, 2)).reshape(CONV_K, CONV_W)
    convw_pad = jnp.pad(convw_full, ((0, CONV_HALO - CONV_K), (0, 0)))

    ada_part = _ada_fwd(c_all, w_ada[0])
    (ada_all,) = _all_gather([ada_part], "gather_ada", vmem=True)
    ada = lax.dynamic_index_in_dim(ada_all, me, axis=1, keepdims=False).reshape(1, 3 * D_MODEL) + b_ada
    shift, scale, gate = ada[:, :D_MODEL], ada[:, D_MODEL:2 * D_MODEL], ada[:, 2 * D_MODEL:]

    wt_g, wa_g, wc_g, wo_g = _all_gather_chips(
        [_cast_bf16(w_in_t, "cast_win"), _cast_bf16(w_attn_proj[0], "cast_wa"), _cast_bf16(w_conv_proj[0], "cast_wc"),
         _cast_bf16(w_out[0], "cast_wo")], "gather_weights")
    wt = wt_g.reshape(IN_W, D_MODEL)
    wa = _cols_from_slots(wa_g, "cols_wa")
    wc = _cols_from_slots(wc_g, "cols_wc")
    wo = wo_g.reshape(D_MODEL, D_MODEL)

    h, ht = _norm_fwd(x2, norm_w, scale, shift)
    proj = _mm_in(h, wt)
    qw2 = jnp.tile(q_norm_w, (1, 2))
    kw2 = jnp.tile(k_norm_w, (1, 2))
    o_all, l_all, qn, kn, vn = _attn_fwd(proj, qw2, kw2)
    o3, l3 = [o_all] * N_GROUPS, [l_all] * N_GROUPS
    head_id = jnp.arange(ATTN_W) // HEAD_DIM
    bd = (head_id[:, None] == head_id[None, :]).astype(BF16)
    (dout, da, lse_delta, dcv, mt, yat, yct, dmo, dya, dyc, dproj,
     dgate, dbg, dlnw, dlnb, dcb, loss_p) = _tail(
        x2, tgt2, proj, o3, l3, wa, wc, wo, gate, b_gate[:, :D_MODEL], b_gate[:, D_MODEL:], convw_pad,
        conv_b, conv_ln_w, conv_ln_b, bd)

    dproj, dconvw8 = _conv_bwd(dcv, proj, convw_pad, dproj)
    dconvw = jnp.sum(dconvw8.reshape(CONV_HALO, 8, CONV_W), axis=1)
    dproj, dqw_all, dkw_all = _attn_bwd(proj, qn, kn, vn, da, lse_delta, qw2, kw2, dproj)
    dqw_g3, dkw_g3 = [dqw_all], [dkw_all]
    dw_in_p = _mm_dw(ht, dproj).reshape(N_DEV, SHARD_W, D_MODEL)
    sb_send, sb_recv, sb_ins, sb_outs, sb_token = _exchange_sibling_start([dw_in_p], "exchange_sibling_start")
    dwo_p = _mm_acc(mt, dmo, sb_token, "mm_dwo", col_slots=False).reshape(N_DEV, D_MODEL // N_DEV, D_MODEL)
    dwa_p = _mm_acc(yat, dya, sb_token, "mm_dwa", col_slots=True)
    dwc_p = _mm_acc(yct, dyc, sb_token, "mm_dwc", col_slots=True)
    small_from_sib = _exchange_sibling([dwa_p, dwc_p, dwo_p], "exchange_sibling_small")
    (dw_in_p,), (win_from_sib,) = _exchange_sibling_wait(sb_send, sb_recv, sb_ins, sb_outs, small_from_sib[0],
                                                         "exchange_sibling_wait")
    partials = [dw_in_p, dwa_p, dwc_p, dwo_p]
    from_sib = [win_from_sib] + list(small_from_sib)
    me_arr = jnp.reshape(me, (1,)).astype(jnp.int32)
    presums = [_presum(p, f, me_arr, f"presum{i}") for i, (p, f) in enumerate(zip(partials, from_sib))]
    s_sems, r_sems, pre_thru, land_thru, token = _exchange_chips_start(presums, "exchange_chips_start")
    gx, dsh, dsc, dnw = _mm_dh_norm_bwd(dproj, wt, x2, dout, norm_w, scale, token)
    small_p = _pack_partials(dsh, dsc, dgate, dnw, dbg, dqw_g3, dkw_g3, dcb, dlnw, dlnb, loss_p)
    small_all, dconvw_all = _all_gather([small_p, dconvw], "gather_small", vmem=True)

    small_w = (b_ada, norm_w, b_gate, q_norm_w, k_norm_w, conv_b, conv_ln_w, conv_ln_b)
    small_m = (m_b_ada, m_norm_w, m_b_gate, m_q_norm_w, m_k_norm_w, m_conv_b, m_conv_ln_w, m_conv_ln_b)
    small_v = (v_b_ada, v_norm_w, v_b_gate, v_q_norm_w, v_k_norm_w, v_conv_b, v_conv_ln_w, v_conv_ln_b)
    r_small, loss_row = _adamw_small(small_all, small_w, small_m, small_v)
    dcw_mine = lax.dynamic_slice_in_dim(dconvw_all[:, :CONV_K, :], me * HEAD_DIM, HEAD_DIM, axis=2)
    r_convw = _adamw([(dcw_mine, N_DEV)], conv_w[0], m_conv_w[0], v_conv_w[0], "adamw_conv_w", stacked=True)

    d_ada_all = small_all[:, 0:3, :].reshape(N_DEV, 3 * D_MODEL)
    d_ada_cols = lax.dynamic_slice_in_dim(d_ada_all, me * (3 * D_MODEL // N_DEV), 3 * D_MODEL // N_DEV, axis=1)
    g_wada = _ada_bwd(c_all, d_ada_cols)
    r_ada = _adamw(g_wada, w_ada[0], m_w_ada[0], v_w_ada[0], "adamw_w_ada", stacked=False)
    pres, lands = _exchange_chips_wait(s_sems, r_sems, pre_thru, land_thru, r_ada[1], "exchange_chips_wait")
    terms = [[(p, 1), (l, len(CHIP_K))] for p, l in zip(pres, lands)]
    r_win = [jnp.transpose(r) for r in _adamw(terms[0], w_in_t, m_w_in_t, v_w_in_t, "adamw_w_in", stacked=True)]
    r_wap = _adamw(terms[1], w_attn_proj[0], m_w_attn_proj[0], v_w_attn_proj[0], "adamw_w_attn_proj", stacked=True)
    r_wcp = _adamw(terms[2], w_conv_proj[0], m_w_conv_proj[0], v_w_conv_proj[0], "adamw_w_conv_proj", stacked=True)
    r_wout = _adamw(terms[3], w_out[0], m_w_out[0], v_w_out[0], "adamw_w_out", stacked=True)

    outs = [loss_row[0, 0], gx[None]]
    for k in range(4):
        b_ada_k, norm_w_k, b_gate_k, qn_k, kn_k, conv_b_k, ln_w_k, ln_b_k = r_small[k]
        outs += [r_ada[k][None], b_ada_k, norm_w_k, r_win[k][None], b_gate_k, qn_k, kn_k, r_wap[k][None],
                 r_convw[k][None], conv_b_k, ln_w_k, ln_b_k, r_wcp[k][None], r_wout[k][None]]
    return tuple(outs)
```

```python
import functools

import jax
import jax.numpy as jnp
from jax import lax
from jax.experimental import pallas as pl
from jax.experimental.pallas import tpu as pltpu

F32 = jnp.float32
BF16 = jnp.bfloat16
SDS = jax.ShapeDtypeStruct
MESH = pl.DeviceIdType.MESH

N_DEV = 8
D_MODEL = 1024
HEAD_DIM = 64
N_GROUPS = 3
DILATIONS = (1, 4, 16)
BAND = 128
BWD_UNROLL = 8
ATTN_W = 512
CONV_W = 512
CONV_K = 31
CONV_HALO = 32
IN_W = 8704
SHARD_W = IN_W // N_DEV
PAIR_W = 2 * SHARD_W
Q0, K0, V0, ZA0, U0, ZC0, G0 = 0, 1536, 3072, 4608, 5120, 6144, 6656
EPS = 1e-6
LANE = 128
VMEM_LIMIT = 56 * 1024 * 1024

ADAM_LR, ADAM_B1, ADAM_B2, ADAM_EPS, ADAM_WD, ADAM_STEP = 0.001, 0.9, 0.999, 1e-08, 0.01, 10

CONVW_FLAT = 2048


def _params(**kw):
    return pltpu.CompilerParams(vmem_limit_bytes=VMEM_LIMIT, **kw)


def _sigmoid(z):
    return 0.5 * jnp.tanh(0.5 * z) + 0.5


def _dot(a, b):
    return jnp.dot(a, b, preferred_element_type=F32)


def _dot_nt(a, b):
    return lax.dot_general(a, b, (((1,), (1,)), ((), ())), preferred_element_type=F32)


def _dot_tn(a, b):
    return lax.dot_general(a, b, (((0,), (0,)), ((), ())), preferred_element_type=F32)


def _peer(x, y, c, k):
    px = 1 - x if (k >> 2) & 1 else x
    py = 1 - y if (k >> 1) & 1 else y
    pc = 1 - c if k & 1 else c
    return (px, py, pc), 4 * px + 2 * py + pc


def _all_gather(arrays, name, vmem):
    n = len(arrays)
    space = pltpu.VMEM if vmem else pl.ANY

    def body(*refs):
        ins, outs = refs[:n], refs[n:2 * n]
        send_sems, recv_sems, local_sems = refs[2 * n:]
        x, y, c = lax.axis_index("x"), lax.axis_index("y"), lax.axis_index("c")
        me = 4 * x + 2 * y + c
        locals_ = [pltpu.make_async_copy(ins[a], outs[a].at[me], local_sems.at[a]) for a in range(n)]
        for cp in locals_:
            cp.start()
        sends = []
        for k in range(1, N_DEV):
            peer, _ = _peer(x, y, c, k)
            for a in range(n):
                cp = pltpu.make_async_remote_copy(
                    src_ref=ins[a], dst_ref=outs[a].at[me], send_sem=send_sems.at[a, k - 1],
                    recv_sem=recv_sems.at[a, k - 1], device_id=peer, device_id_type=MESH)
                cp.start()
                sends.append(cp)
        for k in range(1, N_DEV):
            peer, pidx = _peer(x, y, c, k)
            for a in range(n):
                pltpu.make_async_remote_copy(
                    src_ref=ins[a], dst_ref=outs[a].at[pidx], send_sem=send_sems.at[a, k - 1],
                    recv_sem=recv_sems.at[a, k - 1], device_id=peer, device_id_type=MESH).wait_recv()
        for cp in sends:
            cp.wait_send()
        for cp in locals_:
            cp.wait()

    return pl.pallas_call(
        body, name=name,
        out_shape=[SDS((N_DEV,) + a.shape, a.dtype) for a in arrays],
        in_specs=[pl.BlockSpec(memory_space=space)] * n,
        out_specs=[pl.BlockSpec(memory_space=space)] * n,
        scratch_shapes=[pltpu.SemaphoreType.DMA((n, N_DEV - 1)), pltpu.SemaphoreType.DMA((n, N_DEV - 1)),
                        pltpu.SemaphoreType.DMA((n,))],
        compiler_params=_params(),
    )(*arrays)


CHIP_K = (2, 4, 6)


def _all_gather_chips(arrays, name):
    n = len(arrays)
    k_y, k_x, k_d = CHIP_K

    def body(*refs):
        ins, outs = refs[:n], refs[n:2 * n]
        send_sems, recv_sems, local_sems = refs[2 * n:]
        x, y, c = lax.axis_index("x"), lax.axis_index("y"), lax.axis_index("c")
        me = 4 * x + 2 * y + c
        sib, sib_idx = _peer(x, y, c, 1)
        nbr_y, idx_y = _peer(x, y, c, k_y)
        nbr_x, idx_x = _peer(x, y, c, k_x)
        _, idx_d = _peer(x, y, c, k_d)

        def copy(a, slot, block, to, src=None):
            return pltpu.make_async_remote_copy(
                src_ref=outs[a].at[block] if src is None else src, dst_ref=outs[a].at[block],
                send_sem=send_sems.at[a, slot], recv_sem=recv_sems.at[a, slot], device_id=to, device_id_type=MESH)

        locals_ = [pltpu.make_async_copy(ins[a], outs[a].at[me], local_sems.at[a]) for a in range(n)]
        for cp in locals_:
            cp.start()
        for a in range(n):
            copy(a, 0, me, sib, src=ins[a]).start()
            copy(a, 1, me, nbr_y, src=ins[a]).start()
            copy(a, 2, me, nbr_x, src=ins[a]).start()

        def arrived(slot, block, frm, send_on_to=None):
            for a in range(n):
                copy(a, slot, block, frm).wait_recv()
                if send_on_to is not None:
                    copy(a, 3, block, send_on_to).start()
                copy(a, 3 + slot, block, sib).start()

        @pl.when(c == 0)
        def _():
            arrived(1, idx_y, nbr_y, send_on_to=nbr_x)
            arrived(2, idx_x, nbr_x)

        @pl.when(c == 1)
        def _():
            arrived(2, idx_x, nbr_x, send_on_to=nbr_y)
            arrived(1, idx_y, nbr_y)

        arrived(3, idx_d, nbr_x)
        for a in range(n):
            copy(a, 0, sib_idx, sib).wait_recv()
        for slot, k in ((4, k_y), (5, k_x), (6, k_d)):
            _, pidx = _peer(x, y, 1 - c, k)
            for a in range(n):
                copy(a, slot, pidx, sib).wait_recv()
        for slot in range(N_DEV - 1):
            for a in range(n):
                copy(a, slot, me, sib).wait_send()
        for cp in locals_:
            cp.wait()

    return pl.pallas_call(
        body, name=name,
        out_shape=[SDS((N_DEV,) + a.shape, a.dtype) for a in arrays],
        in_specs=[pl.BlockSpec(memory_space=pl.ANY)] * n,
        out_specs=[pl.BlockSpec(memory_space=pl.ANY)] * n,
        scratch_shapes=[pltpu.SemaphoreType.DMA((n, N_DEV - 1)), pltpu.SemaphoreType.DMA((n, N_DEV - 1)),
                        pltpu.SemaphoreType.DMA((n,))],
        compiler_params=_params(),
    )(*arrays)


def _exchange_sibling(arrays, name):
    n = len(arrays)
    ks = (0,) + CHIP_K

    def body(*refs):
        ins, outs = refs[:n], refs[n:2 * n]
        send_sems, recv_sems = refs[2 * n:]
        x, y, c = lax.axis_index("x"), lax.axis_index("y"), lax.axis_index("c")
        sib, sib_idx = _peer(x, y, c, 1)
        sends = []
        for i, k in enumerate(ks):
            _, tgt = _peer(x, y, 1 - c, k) if k else (None, sib_idx)
            for a in range(n):
                cp = pltpu.make_async_remote_copy(
                    src_ref=ins[a].at[tgt], dst_ref=outs[a].at[i], send_sem=send_sems.at[a, i],
                    recv_sem=recv_sems.at[a, i], device_id=sib, device_id_type=MESH)
                cp.start()
                sends.append(cp)
        for cp in sends:
            cp.wait_recv()
        for cp in sends:
            cp.wait_send()

    return pl.pallas_call(
        body, name=name,
        out_shape=[SDS((len(ks),) + a.shape[1:], a.dtype) for a in arrays],
        in_specs=[pl.BlockSpec(memory_space=pl.ANY)] * n,
        out_specs=[pl.BlockSpec(memory_space=pl.ANY)] * n,
        scratch_shapes=[pltpu.SemaphoreType.DMA((n, len(ks))), pltpu.SemaphoreType.DMA((n, len(ks)))],
        compiler_params=_params(),
    )(*arrays)


def _sibling_copies(ins, outs, send_sems, recv_sems):
    x, y, c = lax.axis_index("x"), lax.axis_index("y"), lax.axis_index("c")
    sib, sib_idx = _peer(x, y, c, 1)
    ks = (0,) + CHIP_K
    copies = []
    for i, k in enumerate(ks):
        _, tgt = _peer(x, y, 1 - c, k) if k else (None, sib_idx)
        for a in range(len(ins)):
            copies.append(pltpu.make_async_remote_copy(
                src_ref=ins[a].at[tgt], dst_ref=outs[a].at[i], send_sem=send_sems.at[a * len(ks) + i],
                recv_sem=recv_sems.at[a * len(ks) + i], device_id=sib, device_id_type=MESH))
    return copies


def _exchange_sibling_start(arrays, name):
    n = len(arrays)
    ns = 1 + len(CHIP_K)

    def body(*refs):
        ins, outs = refs[:n], refs[n:2 * n]
        send_sems, recv_sems = refs[2 * n], refs[2 * n + 1]
        token = refs[-1]
        for cp in _sibling_copies(ins, outs, send_sems, recv_sems):
            cp.start()
        token[...] = jnp.zeros_like(token)

    hbm_in = [pltpu.HBM(a.shape, a.dtype) for a in arrays]
    hbm_out = [pltpu.HBM((ns,) + a.shape[1:], a.dtype) for a in arrays]
    res = pl.pallas_call(
        body, name=name,
        out_shape=(pltpu.SemaphoreType.DMA((n * ns,)), pltpu.SemaphoreType.DMA((n * ns,)), *hbm_in, *hbm_out,
                   SDS((8, LANE), F32)),
        in_specs=[HBM_SPEC] * (2 * n),
        out_specs=(SEM_SPEC, SEM_SPEC, *([HBM_SPEC] * (2 * n)), pl.BlockSpec(memory_space=pltpu.VMEM)),
        input_output_aliases={i: 2 + i for i in range(2 * n)},
        compiler_params=pltpu.CompilerParams(has_side_effects=SIDE_EFFECT),
    )(*[pltpu.with_memory_space_constraint(a, pltpu.HBM) for a in arrays],
      *[pltpu.with_memory_space_constraint(lax.empty((ns,) + a.shape[1:], a.dtype), pltpu.HBM) for a in arrays])
    return res[0], res[1], res[2:2 + n], res[2 + n:2 + 2 * n], res[-1]


def _exchange_sibling_wait(send_sems, recv_sems, ins_thru, outs_thru, after, name):
    n = len(ins_thru)

    def body(*refs):
        ins, outs = refs[:n], refs[n:2 * n]
        s_sems, r_sems = refs[2 * n], refs[2 * n + 1]
        for cp in _sibling_copies(ins, outs, s_sems, r_sems):
            cp.wait_send()
            cp.wait_recv()

    hbm = [pltpu.HBM(a.shape, a.dtype) for a in (*ins_thru, *outs_thru)]
    res = pl.pallas_call(
        body, name=name, out_shape=tuple(hbm),
        in_specs=[HBM_SPEC] * (2 * n) + [SEM_SPEC, SEM_SPEC, pl.BlockSpec(memory_space=pl.ANY)],
        out_specs=tuple([HBM_SPEC] * (2 * n)),
        input_output_aliases={i: i for i in range(2 * n)},
        compiler_params=pltpu.CompilerParams(has_side_effects=SIDE_EFFECT),
    )(*ins_thru, *outs_thru, send_sems, recv_sems, after)
    return res[:n], res[n:]


def _presum(mine, from_sib, me_arr, name):
    _, rows, cols = mine.shape
    tr = _row_tile(rows)
    ns = 1 + len(CHIP_K)

    def body(me_ref, a_ref, b_ref, o_ref):
        del me_ref
        o_ref[...] = (a_ref[...].astype(F32) + b_ref[...].astype(F32)).astype(o_ref.dtype)

    grid_spec = pltpu.PrefetchScalarGridSpec(
        num_scalar_prefetch=1, grid=(ns, rows // tr),
        in_specs=[pl.BlockSpec((1, tr, cols), lambda j, i, me: (jnp.bitwise_xor(me[0], 2 * j), i, 0)),
                  pl.BlockSpec((1, tr, cols), lambda j, i, me: (j, i, 0))],
        out_specs=pl.BlockSpec((1, tr, cols), lambda j, i, me: (j, i, 0)))
    return pl.pallas_call(body, name=name, grid_spec=grid_spec, out_shape=SDS((ns, rows, cols), mine.dtype),
                          compiler_params=_params())(me_arr, mine, from_sib)


HBM_SPEC = pl.BlockSpec(memory_space=pltpu.HBM)
SEM_SPEC = pl.BlockSpec(memory_space=pltpu.SEMAPHORE)
SIDE_EFFECT = pltpu.SideEffectType.DATAFLOW_SIDE_EFFECTING


def _chips_copies(pre_refs, land_refs, send_sems, recv_sems):
    x, y, c = lax.axis_index("x"), lax.axis_index("y"), lax.axis_index("c")
    copies = []
    for j, k in enumerate(CHIP_K):
        peer, _ = _peer(x, y, c, k)
        for a in range(len(pre_refs)):
            copies.append(pltpu.make_async_remote_copy(
                src_ref=pre_refs[a].at[1 + j], dst_ref=land_refs[a].at[j], send_sem=send_sems.at[a * len(CHIP_K) + j],
                recv_sem=recv_sems.at[a * len(CHIP_K) + j], device_id=peer, device_id_type=MESH))
    return copies


def _exchange_chips_start(presums, name):
    n = len(presums)

    def body(*refs):
        pre, land = refs[:n], refs[n:2 * n]
        send_sems, recv_sems = refs[2 * n], refs[2 * n + 1]
        token = refs[-1]
        for cp in _chips_copies(pre, land, send_sems, recv_sems):
            cp.start()
        token[...] = jnp.zeros_like(token)

    nk = len(CHIP_K)
    hbm = [pltpu.HBM(p.shape, p.dtype) for p in presums]
    hbm_land = [pltpu.HBM((nk,) + p.shape[1:], p.dtype) for p in presums]
    res = pl.pallas_call(
        body, name=name,
        out_shape=(pltpu.SemaphoreType.DMA((n * nk,)), pltpu.SemaphoreType.DMA((n * nk,)), *hbm, *hbm_land, SDS((8, LANE), F32)),
        in_specs=[HBM_SPEC] * (2 * n),
        out_specs=(SEM_SPEC, SEM_SPEC, *([HBM_SPEC] * (2 * n)), pl.BlockSpec(memory_space=pltpu.VMEM)),
        input_output_aliases={i: 2 + i for i in range(2 * n)},
        compiler_params=pltpu.CompilerParams(has_side_effects=SIDE_EFFECT),
    )(*[pltpu.with_memory_space_constraint(p, pltpu.HBM) for p in presums],
      *[pltpu.with_memory_space_constraint(lax.empty((nk,) + p.shape[1:], p.dtype), pltpu.HBM) for p in presums])
    return res[0], res[1], res[2:2 + n], res[2 + n:2 + 2 * n], res[-1]


def _exchange_chips_wait(send_sems, recv_sems, pre_thru, land_thru, after, name):
    n = len(pre_thru)

    def body(*refs):
        pre, land = refs[:n], refs[n:2 * n]
        s_sems, r_sems = refs[2 * n], refs[2 * n + 1]
        for cp in _chips_copies(pre, land, s_sems, r_sems):
            cp.wait_send()
            cp.wait_recv()

    hbm = [pltpu.HBM(p.shape, p.dtype) for p in (*pre_thru, *land_thru)]
    res = pl.pallas_call(
        body, name=name, out_shape=tuple(hbm),
        in_specs=[HBM_SPEC] * (2 * n) + [SEM_SPEC, SEM_SPEC, pl.BlockSpec(memory_space=pl.ANY)],
        out_specs=tuple([HBM_SPEC] * (2 * n)),
        input_output_aliases={i: i for i in range(2 * n)},
        compiler_params=pltpu.CompilerParams(has_side_effects=SIDE_EFFECT),
    )(*pre_thru, *land_thru, send_sems, recv_sems, after)
    return res[:n], res[n:]


def _cast_bf16(w, name):
    def body(w_ref, o_ref):
        o_ref[...] = w_ref[...].astype(BF16)

    return pl.pallas_call(body, name=name, out_shape=SDS(w.shape, BF16), compiler_params=_params())(w)


def _cols_from_slots(wg, name):
    _, rows, cols = wg.shape

    def body(w_ref, o_ref):
        for j in range(N_DEV):
            o_ref[:, j * cols:(j + 1) * cols] = w_ref[j]

    return pl.pallas_call(body, name=name, out_shape=SDS((rows, N_DEV * cols), wg.dtype), compiler_params=_params())(wg)


def _ada_fwd(c_all, w_ada):
    def body(c_ref, w_ref, o_ref):
        cv = c_ref[...]
        sc = (cv * _sigmoid(cv)).astype(BF16)
        o_ref[...] = _dot(sc, w_ref[...].astype(BF16))

    return pl.pallas_call(body, name="ada_fwd", out_shape=SDS((N_DEV, w_ada.shape[1]), F32),
                          compiler_params=_params())(c_all, w_ada)


def _ada_bwd(c_all, d_ada_cols):
    def body(c_ref, d_ref, o_ref):
        cv = c_ref[...]
        sc = (cv * _sigmoid(cv)).astype(BF16)
        o_ref[...] = _dot_tn(sc, d_ref[...].astype(BF16))

    return pl.pallas_call(body, name="ada_bwd", out_shape=SDS((D_MODEL, d_ada_cols.shape[1]), F32),
                          compiler_params=_params())(c_all, d_ada_cols)


def _norm_fwd(x, norm_w, scale, shift):
    s = x.shape[0]
    tr = 1024

    def body(x_ref, nw_ref, sc_ref, sh_ref, h_ref, ht_ref):
        xv = x_ref[...]
        r = lax.rsqrt(jnp.mean(xv * xv, axis=-1, keepdims=True) + EPS)
        h = (xv * r * nw_ref[...]) * (1.0 + sc_ref[...]) + sh_ref[...]
        h_ref[...] = h.astype(BF16)
        ht_ref[...] = h.T.astype(BF16)

    vec = pl.BlockSpec((1, D_MODEL), lambda i: (0, 0))
    return pl.pallas_call(
        body, name="norm_fwd", grid=(s // tr,),
        in_specs=[pl.BlockSpec((tr, D_MODEL), lambda i: (i, 0)), vec, vec, vec],
        out_specs=[pl.BlockSpec((tr, D_MODEL), lambda i: (i, 0)), pl.BlockSpec((D_MODEL, tr), lambda i: (0, i))],
        out_shape=[SDS((s, D_MODEL), BF16), SDS((D_MODEL, s), BF16)], compiler_params=_params(),
    )(x, norm_w, scale, shift)


def _mm_in(h, wt):
    s = h.shape[0]
    tm = 1024

    def body(h_ref, w_ref, o_ref):
        o_ref[...] = _dot_nt(h_ref[...], w_ref[...])

    return pl.pallas_call(
        body, name="mm_in", grid=(IN_W // PAIR_W, s // tm),
        in_specs=[pl.BlockSpec((tm, D_MODEL), lambda p, m: (m, 0)),
                  pl.BlockSpec((PAIR_W, D_MODEL), lambda p, m: (p, 0))],
        out_specs=pl.BlockSpec((tm, PAIR_W), lambda p, m: (m, p)),
        out_shape=SDS((s, IN_W), F32), compiler_params=_params(),
    )(h, wt)


def _head_ones():
    a = lax.broadcasted_iota(jnp.int32, (LANE, LANE), 0) // HEAD_DIM
    b = lax.broadcasted_iota(jnp.int32, (LANE, LANE), 1) // HEAD_DIM
    return (a == b).astype(BF16)


def _head_sums(t, ones):
    return _dot(t.astype(BF16), ones)


def _band_bias(bias, transposed=False):
    qi = lax.broadcasted_iota(jnp.int32, (2 * BAND, 2 * BAND), 1 if transposed else 0) % BAND
    kj = lax.broadcasted_iota(jnp.int32, (2 * BAND, 2 * BAND), 0 if transposed else 1)
    dist = qi + BAND - kj
    valid = (dist >= 0) & (dist <= BAND)
    bias[1] = jnp.where(valid, 0.0, -1e30)
    bias[0] = jnp.where(valid & (kj >= BAND), 0.0, -1e30)


def _token_rows(j, d, chunk, per_r):
    return pl.ds(j // per_r + (j % per_r) * (chunk * d), chunk, stride=d)


def _deinterleave_many(jobs, ones, d, sub_len, chunk, unroll):
    per_r = sub_len // chunk

    def step(j, _):
        tok = _token_rows(j, d, chunk, per_r)
        for src_ref, dst_ref, w_ref, scale, dst_off in jobs:
            t = src_ref[tok, :]
            if w_ref is not None:
                ms = _head_sums(t * t, ones) * (1.0 / HEAD_DIM)
                t = t * lax.rsqrt(ms + EPS) * (w_ref[...] * scale)
            dst_ref[pl.ds(pl.multiple_of(dst_off + j * chunk, BAND), chunk), :] = t.astype(dst_ref.dtype)
        return 0
    lax.fori_loop(0, d * per_r, step, 0, unroll=unroll)


def _deinterleave(src_ref, dst_ref, w_ref, ones, d, sub_len, chunk, scale, dst_off):
    _deinterleave_many([(src_ref, dst_ref, w_ref, scale, dst_off)], ones, d, sub_len, chunk, 4)


N_PAIRS = ATTN_W // LANE


def _attn_fwd(proj, qw2, kw2):
    s = proj.shape[0]

    def group_body(g, step, q_ref, k_ref, v_ref, qw_ref, kw_ref, o_ref, l_ref, qd, kd, vd, od, ld, bias):
        d = DILATIONS[g]
        sub_len = s // d
        nb = sub_len // BAND
        chunk = min(sub_len, 256)
        lo = lax.broadcasted_iota(jnp.int32, (1, LANE), 1) < HEAD_DIM
        ones = _head_ones()

        @pl.when(step == 0)
        def _():
            _band_bias(bias)

        kd[0:BAND, :] = jnp.zeros((BAND, LANE), BF16)
        vd[0:BAND, :] = jnp.zeros((BAND, LANE), BF16)
        _deinterleave_many([(q_ref, qd, qw_ref, HEAD_DIM ** -0.5, 0), (k_ref, kd, kw_ref, 1.0, BAND),
                            (v_ref, vd, None, 1.0, BAND)], ones, d, sub_len, chunk, 4)

        def block(t, _):
            base = pl.multiple_of(t * BAND, BAND)
            q = qd[pl.ds(base, BAND), :]
            k2 = kd[pl.ds(base, 2 * BAND), :]
            v2 = vd[pl.ds(base, 2 * BAND), :]
            zero = jnp.zeros_like(q)
            qs = jnp.concatenate([jnp.where(lo, q, zero), jnp.where(lo, zero, q)], axis=0)
            sc = _dot_nt(qs, k2) + bias[jnp.minimum(t % nb, 1)]
            m = jnp.max(sc, axis=-1, keepdims=True)
            p = jnp.exp(sc - m)
            den = jnp.sum(p, axis=-1, keepdims=True)
            u = _dot(p.astype(BF16), v2) * (1.0 / den)
            lse = m + jnp.log(den)
            od[pl.ds(base, BAND), :] = jnp.where(lo, u[:BAND], u[BAND:])
            ld[pl.ds(base, BAND), :] = jnp.where(lo, lse[:BAND], lse[BAND:])
            return 0
        lax.fori_loop(0, s // BAND, block, 0, unroll=16)

        per_r = sub_len // chunk

        def back(j, _):
            src = pl.ds(pl.multiple_of(j * chunk, chunk), chunk)
            dst = _token_rows(j, d, chunk, per_r)
            o_ref[dst, :] = od[src, :]
            l_ref[dst, :] = ld[src, :]
            return 0
        lax.fori_loop(0, d * per_r, back, 0, unroll=2)

    def body(*refs):
        step = pl.program_id(0)
        for g in range(N_GROUPS):
            pl.when(step // N_PAIRS == g)(functools.partial(group_body, g, step, *refs))

    col = lambda off: pl.BlockSpec((s, LANE), lambda i, off=off: (0, off // LANE + i))
    vec = pl.BlockSpec((1, LANE), lambda i: (0, 0))
    out = pl.BlockSpec((s, LANE), lambda i: (0, i))
    width = N_GROUPS * ATTN_W
    return pl.pallas_call(
        body, name="attn_fwd", grid=(N_GROUPS * N_PAIRS,),
        in_specs=[col(Q0), col(K0), col(V0), vec, vec],
        out_specs=[out, out, out, pl.BlockSpec((s + BAND, LANE), lambda i: (0, i)),
                   pl.BlockSpec((s + BAND, LANE), lambda i: (0, i))],
        out_shape=[SDS((s, width), F32)] * 2 + [SDS((s, width), BF16)] + [SDS((s + BAND, width), BF16)] * 2,
        scratch_shapes=[pltpu.VMEM((s, LANE), F32), pltpu.VMEM((s, LANE), F32),
                        pltpu.VMEM((2, 2 * BAND, 2 * BAND), F32)],
        compiler_params=_params(),
    )(proj, proj, proj, qw2, kw2)


def _attn_bwd(proj, qn, kn, vn, da, lse_delta, qw2, kw2, dproj):
    s = proj.shape[0]
    n_steps = N_GROUPS * N_PAIRS

    def group_body(g, hp, q_ref, k_ref, qn_ref, kd, vd, da_ref, ld_ref, qw_ref, kw_ref, dp_in, dp_out,
                   dqw_ref, dkw_ref, kdt, dad, lst, dlt, dqt, dqd, dkd, dvd, st, st_k, stb, bias_t, wacc, sem):
        del dp_in
        d = DILATIONS[g]
        sub_len = s // d
        nb = sub_len // BAND
        chunk = min(sub_len, 256)
        lo = lax.broadcasted_iota(jnp.int32, (1, LANE), 1) < HEAD_DIM
        row_lo = lax.broadcasted_iota(jnp.int32, (LANE, 1), 0) < HEAD_DIM
        ones = _head_ones()
        per_r = sub_len // chunk
        cblk = chunk // BAND

        @pl.when(hp == 0)
        def _():
            _band_bias(bias_t, transposed=True)

        kdt[0] = jnp.zeros((LANE, BAND), BF16)

        def k_step(t, _):
            kdt[1 + t] = kd[pl.ds(pl.multiple_of(BAND + t * BAND, BAND), BAND), :].astype(F32).T.astype(BF16)
            return 0
        lax.fori_loop(0, s // BAND, k_step, 0, unroll=4)
        _deinterleave(da_ref, dad, None, ones, d, sub_len, chunk, 1.0, 0)

        def rows_step(j, _):
            tok = _token_rows(j, d, chunk, per_r)
            tt = ld_ref[tok, :].T
            for u in range(cblk):
                cols = slice(u * BAND, (u + 1) * BAND)
                lst[j * cblk + u, 0:1, :] = tt[0:1, cols]
                lst[j * cblk + u, 1:2, :] = tt[HEAD_DIM:HEAD_DIM + 1, cols]
                dlt[j * cblk + u, 0:1, :] = tt[HEAD_DIM // 2:HEAD_DIM // 2 + 1, cols]
                dlt[j * cblk + u, 1:2, :] = tt[HEAD_DIM + HEAD_DIM // 2:HEAD_DIM + HEAD_DIM // 2 + 1, cols]
            return 0
        lax.fori_loop(0, d * per_r, rows_step, 0, unroll=4)

        def block(t, carry):
            ck, cv = carry
            base = pl.multiple_of(t * BAND, BAND)
            q = qn_ref[pl.ds(base, BAND), :]
            k2 = kd[pl.ds(base, 2 * BAND), :]
            v2 = vd[pl.ds(base, 2 * BAND), :]
            k2t = jnp.concatenate([kdt[t], kdt[t + 1]], axis=1)
            dav = dad[pl.ds(base, BAND), :]
            zero = jnp.zeros_like(q)
            qs = jnp.concatenate([jnp.where(lo, q, zero), jnp.where(lo, zero, q)], axis=0)
            das = jnp.concatenate([jnp.where(lo, dav, zero), jnp.where(lo, zero, dav)], axis=0)
            ls_row = jnp.concatenate([lst[t, 0:1, :], lst[t, 1:2, :]], axis=1)
            dl_row = jnp.concatenate([dlt[t, 0:1, :], dlt[t, 1:2, :]], axis=1)
            sc_t = _dot_nt(k2, qs) + bias_t[jnp.minimum(t % nb, 1)]
            p_t = jnp.exp(sc_t - ls_row)
            dp_t = _dot_nt(v2, das)
            ds_t = (p_t * (dp_t - dl_row)).astype(BF16)
            dv2 = _dot(p_t.astype(BF16), das)
            dk2 = _dot(ds_t, qs)
            dvd[pl.ds(base, BAND), :] = cv + dv2[:BAND]
            dkd[pl.ds(base, BAND), :] = ck + dk2[:BAND]
            dq_t = _dot(k2t, ds_t)
            dqt[t] = jnp.where(row_lo, dq_t[:, :BAND], dq_t[:, BAND:])
            return dk2[BAND:], dv2[BAND:]

        def blocks(i, carry):
            for u in range(BWD_UNROLL):
                carry = block(i * BWD_UNROLL + u, carry)
            return carry
        zeros = jnp.zeros((BAND, LANE), F32)
        ck, cv = lax.fori_loop(0, s // (BAND * BWD_UNROLL), blocks, (zeros, zeros))
        dkd[s:s + BAND, :] = ck
        dvd[s:s + BAND, :] = cv

        def dq_rows(t, _):
            dqd[pl.ds(pl.multiple_of(t * BAND, BAND), BAND), :] = dqt[t].T
            return 0
        lax.fori_loop(0, s // BAND, dq_rows, 0, unroll=4)

        def col_copy(slot, col0):
            return pltpu.make_async_copy(
                stb.at[slot], dp_out.at[:, pl.ds(pl.multiple_of(col0 + LANE * hp, LANE), LANE)], sem.at[slot])

        def store_cols(slot, col0, src):
            @pl.when(hp > 0)
            def _():
                col_copy(slot, col0).wait()
            stb[slot] = src[...].astype(BF16)
            col_copy(slot, col0).start()

        sides = ((q_ref, dqd, 0, qw_ref, HEAD_DIM ** -0.5, st), (k_ref, dkd, BAND, kw_ref, 1.0, st_k))
        wacc[...] = jnp.zeros_like(wacc)

        def norm_step(j, _):
            tok = _token_rows(j, d, chunk, per_r)
            for i, (src_ref, dy_ref, dy_off, w_ref, scale, dst) in enumerate(sides):
                t = src_ref[tok, :]
                dy = dy_ref[pl.ds(pl.multiple_of(dy_off + j * chunk, BAND), chunk), :]
                rr = lax.rsqrt(_head_sums(t * t, ones) * (1.0 / HEAD_DIM) + EPS)
                nrm = t * rr
                wacc[i] += jnp.sum((dy * nrm).reshape(chunk // 8, 8, LANE), axis=0)
                dn = dy * (w_ref[...] * scale)
                dst[tok, :] = rr * (dn - nrm * (_head_sums(dn * nrm, ones) * (1.0 / HEAD_DIM)))
            return 0
        lax.fori_loop(0, d * per_r, norm_step, 0, unroll=4)

        @pl.when(hp == 0)
        def _():
            dqw_ref[...] = jnp.zeros_like(dqw_ref)
            dkw_ref[...] = jnp.zeros_like(dkw_ref)

        for i, dw_ref in enumerate((dqw_ref, dkw_ref)):
            dw_ref[...] += jnp.broadcast_to(jnp.sum(wacc[i], axis=0, keepdims=True) * sides[i][4], dw_ref.shape)
        store_cols(0, Q0, st)
        store_cols(1, K0, st_k)

        def v_back(j, _):
            src = pl.ds(pl.multiple_of(BAND + j * chunk, BAND), chunk)
            st[_token_rows(j, d, chunk, per_r), :] = dvd[src, :]
            return 0
        lax.fori_loop(0, d * per_r, v_back, 0, unroll=2)
        store_cols(2, V0, st)

        @pl.when(hp == n_steps - 1)
        def _():
            for slot, col0 in enumerate((Q0, K0, V0)):
                col_copy(slot, col0).wait()

    def body(*refs):
        step = pl.program_id(0)
        for g in range(N_GROUPS):
            pl.when(step // N_PAIRS == g)(functools.partial(group_body, g, step, *refs))

    col = lambda off: pl.BlockSpec((s, LANE), lambda i, off=off: (0, off // LANE + i))
    mid = pl.BlockSpec((s, LANE), lambda i: (0, i))
    padded = pl.BlockSpec((s + BAND, LANE), lambda i: (0, i))
    slot4 = pl.BlockSpec((s, LANE), lambda i: (0, i % N_PAIRS))
    vec = pl.BlockSpec((1, LANE), lambda i: (0, 0))
    acc = pl.BlockSpec((8, LANE), lambda i: (0, 0))
    any_ = pl.BlockSpec(memory_space=pl.ANY)
    return pl.pallas_call(
        body, name="attn_bwd", grid=(n_steps,),
        in_specs=[col(Q0), col(K0), mid, padded, padded, slot4, slot4, vec, vec, any_],
        out_specs=[any_, acc, acc],
        out_shape=[SDS(dproj.shape, dproj.dtype), SDS((8, LANE), F32), SDS((8, LANE), F32)],
        input_output_aliases={9: 0},
        scratch_shapes=[pltpu.VMEM((s // BAND + 1, LANE, BAND), BF16), pltpu.VMEM((s, LANE), BF16),
                        pltpu.VMEM((s // BAND, 8, BAND), F32), pltpu.VMEM((s // BAND, 8, BAND), F32),
                        pltpu.VMEM((s // BAND, LANE, BAND), F32),
                        pltpu.VMEM((s, LANE), F32), pltpu.VMEM((s + BAND, LANE), F32), pltpu.VMEM((s + BAND, LANE), F32),
                        pltpu.VMEM((s, LANE), F32), pltpu.VMEM((s, LANE), F32), pltpu.VMEM((3, s, LANE), BF16),
                        pltpu.VMEM((2, 2 * BAND, 2 * BAND), F32), pltpu.VMEM((2, 8, LANE), F32),
                        pltpu.SemaphoreType.DMA((3,))],
        compiler_params=_params(),
    )(proj, proj, qn, kn, vn, da, lse_delta, qw2, kw2, dproj)


def _tap_views(ext_ref, sh_ref, offsets, tr, cols):
    for b in range(8):
        group = [j for j, o in enumerate(offsets) if o % 8 == b]
        if not group:
            continue
        first = min(offsets[j] for j in group)
        span = tr + max(offsets[j] for j in group) - first
        sh_ref[0:span, cols] = ext_ref[first:first + span, cols]
        for j in group:
            yield j, sh_ref[offsets[j] - first:offsets[j] - first + tr, cols]


def _silu_grad(z, sg):
    return sg * (1.0 + z * (1.0 - sg))


def _glu(u):
    a_h, b_h = u[:, :CONV_W], u[:, CONV_W:]
    sg = _sigmoid(b_h)
    return a_h, sg, a_h * sg


def _tail(x, tgt, proj, o3, l3, wa, wc, wo, gate, bga, bgc, convw, convb, lnw, lnb, bd):
    s = x.shape[0]
    tr = 256

    def body(x_ref, t_ref, za_ref, u_ref, uh_ref, zc_ref, g0_ref, g1_ref, g2_ref, g3_ref,
             o0_ref, o1_ref, o2_ref, l0_ref, l1_ref, l2_ref, wa_ref, wc_ref, wo_ref,
             gate_ref, bga_ref, bgc_ref, cw_ref, cb_ref, lnw_ref, lnb_ref, bd_ref,
             dout_ref, da_ref, ld_ref, dcv_ref, mt_ref, yat_ref, yct_ref, dmo_ref, dya_ref, dyc_ref, dp_ref,
             dgate_ref, dbg_ref, dlnw_ref, dlnb_ref, dcb_ref, loss_ref,
             ext, sh, st_za, st_zc, st_g, sems):
        i = pl.program_id(0)

        @pl.when(i == 0)
        def _():
            for r in (dgate_ref, dbg_ref, dlnw_ref, dlnb_ref, dcb_ref, loss_ref):
                r[...] = jnp.zeros_like(r)

        def acc_rows(ref, v):
            ref[...] += jnp.broadcast_to(jnp.sum(v, axis=0, keepdims=True), ref.shape)

        la, lb, lc = l0_ref[...], l1_ref[...], l2_ref[...]
        mx = jnp.maximum(jnp.maximum(la, lb), lc)
        ea, eb, ec = jnp.exp(la - mx), jnp.exp(lb - mx), jnp.exp(lc - mx)
        den = ea + eb + ec
        inv = 1.0 / den
        attn = (ea * inv) * o0_ref[...] + (eb * inv) * o1_ref[...] + (ec * inv) * o2_ref[...]
        lse = mx + jnp.log(den)

        za = za_ref[...]
        sga = _sigmoid(za)
        sa = za * sga
        ya_in = attn * sa
        y_attn = _dot(ya_in.astype(BF16), wa_ref[...])

        _, _, glu = _glu(u_ref[...])
        _, _, glu_h = _glu(uh_ref[...])
        ext[0:CONV_HALO, :] = jnp.where(i > 0, glu_h, 0.0)
        ext[CONV_HALO:CONV_HALO + tr, :] = glu
        cv_blocks = []
        for cb in range(CONV_W // LANE):
            cols = slice(cb * LANE, (cb + 1) * LANE)
            cv_c = jnp.broadcast_to(cb_ref[:, cols], (tr, LANE))
            for j, rows in _tap_views(ext, sh, [CONV_HALO - (CONV_K - 1) + j for j in range(CONV_K)], tr, cols):
                cv_c = cv_c + cw_ref[j:j + 1, cols] * rows
            cv_blocks.append(cv_c)
        cv = jnp.concatenate(cv_blocks, axis=1)
        mu = jnp.mean(cv, axis=-1, keepdims=True)
        xc = cv - mu
        rstd = lax.rsqrt(jnp.mean(xc * xc, axis=-1, keepdims=True) + EPS)
        nrm = xc * rstd
        ln = nrm * lnw_ref[...] + lnb_ref[...]
        sgl = _sigmoid(ln)
        cs = ln * sgl
        zc = zc_ref[...]
        sgc = _sigmoid(zc)
        scz = zc * sgc
        yc_in = cs * scz
        y_conv = _dot(yc_in.astype(BF16), wc_ref[...])

        ga = _sigmoid(jnp.concatenate([g0_ref[...], g1_ref[...]], axis=1) + bga_ref[...])
        gc = _sigmoid(jnp.concatenate([g2_ref[...], g3_ref[...]], axis=1) + bgc_ref[...])
        merged = ga * y_attn + gc * y_conv
        mo = _dot(merged.astype(BF16), wo_ref[...])
        gate_v = gate_ref[...]
        err = (x_ref[...] + gate_v * mo) - t_ref[...]
        loss_ref[...] += 0.5 * jnp.sum(jnp.mean(err * err, axis=-1, keepdims=True))
        d_out = err * (1.0 / D_MODEL)
        dout_ref[...] = d_out

        rows = pl.ds(pl.multiple_of(i * tr, tr), tr)
        cps = [pltpu.make_async_copy(st_za, dp_ref.at[rows, pl.ds(ZA0, ATTN_W)], sems.at[0]),
               pltpu.make_async_copy(st_zc, dp_ref.at[rows, pl.ds(ZC0, CONV_W)], sems.at[1]),
               pltpu.make_async_copy(st_g, dp_ref.at[rows, pl.ds(G0, 2 * D_MODEL)], sems.at[2])]

        @pl.when(i > 0)
        def _():
            for cp in cps:
                cp.wait()

        acc_rows(dgate_ref, d_out * mo)
        dmo_b = (d_out * gate_v).astype(BF16)
        dmo_ref[...] = dmo_b
        mt_ref[...] = merged.T.astype(BF16)
        d_merged = _dot_nt(dmo_b, wo_ref[...])
        d_ya = (d_merged * ga).astype(BF16)
        d_yc = (d_merged * gc).astype(BF16)
        dya_ref[...] = d_ya
        dyc_ref[...] = d_yc
        dga = d_merged * y_attn * (ga * (1.0 - ga))
        dgc = d_merged * y_conv * (gc * (1.0 - gc))
        dgs = jnp.concatenate([dga, dgc], axis=1)
        acc_rows(dbg_ref, dgs)
        st_g[...] = dgs.astype(BF16)

        yat_ref[...] = ya_in.T.astype(BF16)
        d_ya_in = _dot_nt(d_ya, wa_ref[...])
        d_attn = d_ya_in * sa
        da_ref[...] = d_attn
        st_za[...] = (d_ya_in * attn * _silu_grad(za, sga)).astype(BF16)
        prod = d_attn * attn
        hi = prod.astype(BF16)
        lo_ = (prod - hi.astype(F32)).astype(BF16)
        delta = _dot(hi, bd_ref[...]) + _dot(lo_, bd_ref[...])
        first_half = (lax.broadcasted_iota(jnp.int32, (1, ATTN_W), 1) % HEAD_DIM) < HEAD_DIM // 2
        ld_ref[...] = jnp.where(first_half, lse, delta)

        yct_ref[...] = yc_in.T.astype(BF16)
        d_yc_in = _dot_nt(d_yc, wc_ref[...])
        st_zc[...] = (d_yc_in * cs * _silu_grad(zc, sgc)).astype(BF16)
        d_ln = (d_yc_in * scz) * _silu_grad(ln, sgl)
        acc_rows(dlnw_ref, d_ln * nrm)
        acc_rows(dlnb_ref, d_ln)
        d_nrm = d_ln * lnw_ref[...]
        d_cv = rstd * (d_nrm - jnp.mean(d_nrm, axis=-1, keepdims=True)
                       - nrm * jnp.mean(d_nrm * nrm, axis=-1, keepdims=True))
        acc_rows(dcb_ref, d_cv)
        dcv_ref[...] = d_cv

        for cp in cps:
            cp.start()

        @pl.when(i == s // tr - 1)
        def _():
            for cp in cps:
                cp.wait()

    def rows(width, colblk=0):
        return pl.BlockSpec((tr, width), lambda i, colblk=colblk: (i, colblk))

    def const(shape):
        return pl.BlockSpec(shape, lambda i: (0,) * len(shape))

    halo = pl.BlockSpec((CONV_HALO, D_MODEL), lambda i: (jnp.maximum(i * (tr // CONV_HALO) - 1, 0), U0 // D_MODEL))
    in_specs = [rows(D_MODEL), rows(D_MODEL), rows(ATTN_W, ZA0 // ATTN_W), rows(D_MODEL, U0 // D_MODEL), halo,
                rows(CONV_W, ZC0 // CONV_W)]
    in_specs += [rows(512, G0 // 512 + j) for j in range(4)]
    in_specs += [rows(ATTN_W, g) for g in range(N_GROUPS)] * 2
    in_specs += [const(wa.shape), const(wc.shape), const(wo.shape), const((1, D_MODEL)), const((1, D_MODEL)),
                 const((1, D_MODEL)), const(convw.shape), const((1, CONV_W)), const((1, CONV_W)), const((1, CONV_W)),
                 const(bd.shape)]
    tcol = lambda width: pl.BlockSpec((width, tr), lambda i: (0, i))
    out_specs = [rows(D_MODEL), rows(ATTN_W), rows(ATTN_W), rows(CONV_W),
                 tcol(D_MODEL), tcol(ATTN_W), tcol(CONV_W), rows(D_MODEL), rows(D_MODEL), rows(D_MODEL),
                 pl.BlockSpec(memory_space=pl.ANY),
                 const((8, D_MODEL)), const((8, 2 * D_MODEL)), const((8, CONV_W)), const((8, CONV_W)), const((8, CONV_W)),
                 const((8, LANE))]
    out_shape = [SDS((s, D_MODEL), F32), SDS((s, ATTN_W), F32), SDS((s, ATTN_W), F32),
                 SDS((s, CONV_W), F32),
                 SDS((D_MODEL, s), BF16), SDS((ATTN_W, s), BF16), SDS((CONV_W, s), BF16),
                 SDS((s, D_MODEL), BF16), SDS((s, D_MODEL), BF16), SDS((s, D_MODEL), BF16),
                 SDS((s, IN_W), BF16),
                 SDS((8, D_MODEL), F32), SDS((8, 2 * D_MODEL), F32), SDS((8, CONV_W), F32), SDS((8, CONV_W), F32),
                 SDS((8, CONV_W), F32), SDS((8, LANE), F32)]
    return pl.pallas_call(
        body, name="tail", grid=(s // tr,), in_specs=in_specs, out_specs=out_specs, out_shape=out_shape,
        scratch_shapes=[pltpu.VMEM((CONV_HALO + tr, CONV_W), F32), pltpu.VMEM((CONV_HALO + tr, CONV_W), F32),
                        pltpu.VMEM((tr, ATTN_W), BF16),
                        pltpu.VMEM((tr, CONV_W), BF16), pltpu.VMEM((tr, 2 * D_MODEL), BF16),
                        pltpu.SemaphoreType.DMA((3,))],
        compiler_params=_params(),
    )(x, tgt, proj, proj, proj, proj, proj, proj, proj, proj, *o3, *l3, wa, wc, wo, gate, bga, bgc,
      convw, convb, lnw, lnb, bd)


def _conv_bwd(dcv, proj, convw, dproj):
    s = dcv.shape[0]
    tr = 128
    nt = s // tr

    def body(dcv_ref, dcvn_ref, u_ref, uh_ref, cw_ref, dp_in, dp_out, dw_ref, extg, extd, sh):
        del dp_in
        i = pl.program_id(0)

        @pl.when(i == 0)
        def _():
            dw_ref[...] = jnp.zeros_like(dw_ref)

        _, _, glu = _glu(u_ref[...])
        _, _, glu_h = _glu(uh_ref[...])
        extg[0:CONV_HALO, :] = jnp.where(i > 0, glu_h, 0.0)
        extg[CONV_HALO:CONV_HALO + tr, :] = glu
        extd[0:tr, :] = dcv_ref[...]
        extd[tr:tr + CONV_HALO, :] = jnp.where(i < nt - 1, dcvn_ref[...], 0.0)
        for cb in range(CONV_W // LANE):
            cols = slice(cb * LANE, (cb + 1) * LANE)
            dglu = jnp.zeros((tr, LANE), F32)
            for j, rows in _tap_views(extd, sh, [CONV_K - 1 - j for j in range(CONV_K)], tr, cols):
                dglu = dglu + cw_ref[j:j + 1, cols] * rows
            dcv_c = dcv_ref[:, cols]
            for j, rows in _tap_views(extg, sh, [CONV_HALO - (CONV_K - 1) + j for j in range(CONV_K)], tr, cols):
                dw_ref[8 * j:8 * j + 8, cols] += jnp.sum((dcv_c * rows).reshape(tr // 8, 8, LANE), axis=0)
            a_h = u_ref[:, cols]
            sgb = _sigmoid(u_ref[:, CONV_W + cb * LANE:CONV_W + (cb + 1) * LANE])
            dp_out[:, cols] = (dglu * sgb).astype(BF16)
            dp_out[:, CONV_W + cb * LANE:CONV_W + (cb + 1) * LANE] = (dglu * a_h * (sgb * (1.0 - sgb))).astype(BF16)

    ucol = U0 // D_MODEL
    return pl.pallas_call(
        body, name="conv_bwd", grid=(nt,),
        in_specs=[pl.BlockSpec((tr, CONV_W), lambda i: (i, 0)),
                  pl.BlockSpec((CONV_HALO, CONV_W), lambda i: (jnp.minimum((i + 1) * (tr // CONV_HALO), s // CONV_HALO - 1), 0)),
                  pl.BlockSpec((tr, D_MODEL), lambda i: (i, ucol)),
                  pl.BlockSpec((CONV_HALO, D_MODEL), lambda i: (jnp.maximum(i * (tr // CONV_HALO) - 1, 0), ucol)),
                  pl.BlockSpec(convw.shape, lambda i: (0, 0)),
                  pl.BlockSpec(memory_space=pl.ANY)],
        out_specs=[pl.BlockSpec((tr, D_MODEL), lambda i: (i, ucol)), pl.BlockSpec((8 * CONV_HALO, CONV_W), lambda i: (0, 0))],
        out_shape=[SDS(dproj.shape, dproj.dtype), SDS((8 * CONV_HALO, CONV_W), F32)],
        input_output_aliases={5: 0},
        scratch_shapes=[pltpu.VMEM((CONV_HALO + tr, CONV_W), F32)] * 3,
        compiler_params=_params(),
    )(dcv, dcv, proj, proj, convw, dproj)


def _mm_acc(at, b, token, name, col_slots):
    m, s = at.shape
    n = b.shape[1]
    tk = 2048
    nk = s // tk

    def body(a_ref, b_ref, tok_ref, o_ref, acc):
        del tok_ref
        k = pl.program_id(0)

        @pl.when(k == 0)
        def _():
            acc[...] = jnp.zeros_like(acc)

        acc[...] += _dot(a_ref[...], b_ref[...])

        @pl.when(k == nk - 1)
        def _():
            if col_slots:
                w = n // N_DEV
                for j in range(N_DEV):
                    o_ref[j] = acc[:, j * w:(j + 1) * w].astype(BF16)
            else:
                o_ref[...] = acc[...].astype(BF16)

    if col_slots:
        out_shape = SDS((N_DEV, m, n // N_DEV), BF16)
        out_spec = pl.BlockSpec((N_DEV, m, n // N_DEV), lambda k: (0, 0, 0))
    else:
        out_shape = SDS((m, n), BF16)
        out_spec = pl.BlockSpec((m, n), lambda k: (0, 0))
    return pl.pallas_call(
        body, name=name, grid=(nk,),
        in_specs=[pl.BlockSpec((m, tk), lambda k: (0, k)), pl.BlockSpec((tk, n), lambda k: (k, 0)),
                  pl.BlockSpec(token.shape, lambda k: (0, 0))],
        out_specs=out_spec, out_shape=out_shape, scratch_shapes=[pltpu.VMEM((m, n), F32)],
        compiler_params=_params(),
    )(at, b, token)


def _mm_dw(ht, dproj):
    s = ht.shape[1]
    tk = 2048
    nk = s // tk

    def body(a_ref, b_ref, o_ref, acc):
        k = pl.program_id(1)

        @pl.when(k == 0)
        def _():
            acc[...] = jnp.zeros_like(acc)

        acc[...] += _dot(a_ref[...], b_ref[...])

        @pl.when(k == nk - 1)
        def _():
            o_ref[...] = acc[...].T.astype(BF16)

    return pl.pallas_call(
        body, name="mm_dw", grid=(IN_W // PAIR_W, nk),
        in_specs=[pl.BlockSpec((D_MODEL, tk), lambda p, k: (0, k)), pl.BlockSpec((tk, PAIR_W), lambda p, k: (k, p))],
        out_specs=pl.BlockSpec((PAIR_W, D_MODEL), lambda p, k: (p, 0)),
        out_shape=SDS((IN_W, D_MODEL), BF16), scratch_shapes=[pltpu.VMEM((D_MODEL, PAIR_W), F32)],
        compiler_params=_params(),
    )(ht, dproj)


def _mm_dh_norm_bwd(dproj, wt, x, dout, norm_w, scale, token):
    s = dproj.shape[0]
    tm = 1024
    n_p = IN_W // PAIR_W

    def body(dp_ref, w_ref, x_ref, do_ref, nw_ref, sc_ref, tok_ref, gx_ref, dsh_ref, dsc_ref, dnw_ref, dh_acc):
        del tok_ref
        m, p = pl.program_id(0), pl.program_id(1)
        part = _dot(dp_ref[...], w_ref[...])

        @pl.when(p == 0)
        def _():
            dh_acc[...] = part

        @pl.when(p > 0)
        def _():
            dh_acc[...] += part

        @pl.when((m == 0) & (p == 0))
        def _():
            for r in (dsh_ref, dsc_ref, dnw_ref):
                r[...] = jnp.zeros_like(r)

        @pl.when(p == n_p - 1)
        def _():
            def acc_rows(ref, v):
                ref[...] += jnp.broadcast_to(jnp.sum(v, axis=0, keepdims=True), ref.shape)

            xv = x_ref[...]
            dh_v = dh_acc[...]
            r = lax.rsqrt(jnp.mean(xv * xv, axis=-1, keepdims=True) + EPS)
            xn = xv * r
            one_sc = 1.0 + sc_ref[...]
            acc_rows(dsh_ref, dh_v)
            acc_rows(dsc_ref, dh_v * (xn * nw_ref[...]))
            acc_rows(dnw_ref, dh_v * xn * one_sc)
            dxn = dh_v * (nw_ref[...] * one_sc)
            gx_ref[...] = do_ref[...] + r * (dxn - xn * jnp.mean(dxn * xn, axis=-1, keepdims=True))

    rows = pl.BlockSpec((tm, D_MODEL), lambda m, p: (m, 0))
    vec = pl.BlockSpec((1, D_MODEL), lambda m, p: (0, 0))
    acc = pl.BlockSpec((8, D_MODEL), lambda m, p: (0, 0))
    return pl.pallas_call(
        body, name="mm_dh_norm_bwd", grid=(s // tm, n_p),
        in_specs=[pl.BlockSpec((tm, PAIR_W), lambda m, p: (m, p)),
                  pl.BlockSpec((PAIR_W, D_MODEL), lambda m, p: (p, 0)),
                  rows, rows, vec, vec, pl.BlockSpec(token.shape, lambda m, p: (0, 0))],
        out_specs=[rows, acc, acc, acc],
        out_shape=[SDS((s, D_MODEL), F32)] + [SDS((8, D_MODEL), F32)] * 3,
        scratch_shapes=[pltpu.VMEM((tm, D_MODEL), F32)], compiler_params=_params(),
    )(dproj, wt, x, dout, norm_w, scale, token)


SMALL_ROWS = 8
QN_COL, KN_COL, CB_COL, LOSS_COL = 0, LANE, 2 * LANE, 2 * LANE + CONV_W


def _pack_partials(dsh, dsc, dgate, dnw, dbg, dqw3, dkw3, dcb, dlnw, dlnb, loss_p):
    n3 = len(dqw3)

    def body(*refs):
        dsh_r, dsc_r, dgate_r, dnw_r, dbg_r = refs[:5]
        dq_r, dk_r = refs[5:5 + n3], refs[5 + n3:5 + 2 * n3]
        dcb_r, dlnw_r, dlnb_r, loss_r, o_ref = refs[5 + 2 * n3:]

        def both_heads(rs):
            t = rs[0][0:1, :]
            for r in rs[1:]:
                t = t + r[0:1, :]
            return t + pltpu.roll(t, HEAD_DIM, axis=1)

        o_ref[0:1, :] = dsh_r[0:1, :]
        o_ref[1:2, :] = dsc_r[0:1, :]
        o_ref[2:3, :] = dgate_r[0:1, :]
        o_ref[3:4, :] = dnw_r[0:1, :]
        o_ref[4:5, :] = dbg_r[0:1, 0:D_MODEL]
        o_ref[5:6, :] = dbg_r[0:1, D_MODEL:]
        o_ref[6:7, QN_COL:QN_COL + LANE] = both_heads(dq_r)
        o_ref[6:7, KN_COL:KN_COL + LANE] = both_heads(dk_r)
        o_ref[6:7, CB_COL:CB_COL + CONV_W] = dcb_r[0:1, :]
        o_ref[6:7, LOSS_COL:LOSS_COL + LANE] = loss_r[0:1, :]
        o_ref[6:7, LOSS_COL + LANE:] = jnp.zeros((1, D_MODEL - LOSS_COL - LANE), F32)
        o_ref[7:8, 0:CONV_W] = dlnw_r[0:1, :]
        o_ref[7:8, CONV_W:] = dlnb_r[0:1, :]

    return pl.pallas_call(body, name="pack_partials", out_shape=SDS((SMALL_ROWS, D_MODEL), F32),
                          compiler_params=_params())(dsh, dsc, dgate, dnw, dbg, *dqw3, *dkw3, dcb, dlnw, dlnb, loss_p)


def _adamw_update(g, w, m, v):
    bc1 = 1.0 - ADAM_B1 ** ADAM_STEP
    bc2 = 1.0 - ADAM_B2 ** ADAM_STEP
    m_new = ADAM_B1 * m + (1.0 - ADAM_B1) * g
    v_new = ADAM_B2 * v + (1.0 - ADAM_B2) * (g * g)
    delta = -ADAM_LR * ((m_new / bc1) / (jnp.sqrt(v_new / bc2) + ADAM_EPS) + ADAM_WD * w)
    return delta, m_new, v_new


def _adamw_small(small_all, ws, ms, vs):
    n = len(ws)
    where = [(slice(0, 3), None), (slice(3, 4), None), (slice(4, 6), None), (6, QN_COL), (6, KN_COL), (6, CB_COL),
             (7, 0), (7, CONV_W)]

    def body(*refs):
        g_ref = refs[0]
        w_r, m_r, v_r = refs[1:1 + n], refs[1 + n:1 + 2 * n], refs[1 + 2 * n:1 + 3 * n]
        outs = refs[1 + 3 * n:]
        g_o, d_o, m_o, v_o, loss_o = outs[:n], outs[n:2 * n], outs[2 * n:3 * n], outs[3 * n:4 * n], outs[4 * n]
        gsum = g_ref[0]
        for dev in range(1, N_DEV):
            gsum = gsum + g_ref[dev]
        loss_o[...] = gsum[6:7, LOSS_COL:LOSS_COL + LANE]
        for i, (rows, col) in enumerate(where):
            width = w_r[i].shape[1]
            if col is None:
                g = jnp.concatenate([gsum[r:r + 1, :] for r in range(rows.start, rows.stop)], axis=1)
            else:
                g = gsum[rows:rows + 1, col:col + width]
            delta, m_new, v_new = _adamw_update(g, w_r[i][...], m_r[i][...], v_r[i][...])
            g_o[i][...] = g
            d_o[i][...] = delta
            m_o[i][...] = m_new
            v_o[i][...] = v_new

    shapes = [SDS(w.shape, F32) for w in ws]
    res = pl.pallas_call(body, name="adamw_small", out_shape=shapes * 4 + [SDS((1, LANE), F32)],
                         compiler_params=_params())(small_all, *ws, *ms, *vs)
    return [res[k * n:(k + 1) * n] for k in range(4)], res[4 * n]


def _row_tile(rows):
    if rows <= 128:
        return rows
    if rows % 256 == 0:
        return 256
    return 128 if rows % 128 == 0 else SHARD_W // 4


def _adamw(gsrc, w, m, v, name, stacked):
    rows, cols = w.shape
    tr = _row_tile(rows)
    n_src = len(gsrc) if stacked else 1

    def body(*refs):
        g_refs, (w_ref, m_ref, v_ref, go_ref, d_ref, mo_ref, vo_ref) = refs[:n_src], refs[n_src:]
        if stacked:
            g = None
            for g_ref, (_, slots) in zip(g_refs, gsrc):
                for j in range(slots):
                    t = g_ref[j].astype(F32)
                    g = t if g is None else g + t
        else:
            g = g_refs[0][...]
        delta, m_new, v_new = _adamw_update(g, w_ref[...], m_ref[...], v_ref[...])
        go_ref[...] = g
        d_ref[...] = delta
        mo_ref[...] = m_new
        vo_ref[...] = v_new

    blk = pl.BlockSpec((tr, cols), lambda i: (i, 0))
    if stacked:
        gspecs = [pl.BlockSpec((slots, tr, arr.shape[2]), lambda i: (0, i, 0)) for arr, slots in gsrc]
        gargs = [arr for arr, _ in gsrc]
    else:
        gspecs, gargs = [blk], [gsrc]
    in_specs = gspecs + [blk, blk, blk]
    args = gargs + [w, m, v]
    return pl.pallas_call(
        body, name=name, grid=(rows // tr,), in_specs=in_specs, out_specs=[blk] * 4,
        out_shape=[SDS((rows, cols), F32)] * 4, compiler_params=_params(),
    )(*args)


def kernel(x, c, w_ada, b_ada, norm_w, w_in, b_gate, q_norm_w, k_norm_w, w_attn_proj, conv_w, conv_b, conv_ln_w, conv_ln_b, w_conv_proj, w_out, loss_target, m_w_ada, m_b_ada, m_norm_w, m_w_in, m_b_gate, m_q_norm_w, m_k_norm_w, m_w_attn_proj, m_conv_w, m_conv_b, m_conv_ln_w, m_conv_ln_b, m_w_conv_proj, m_w_out, v_w_ada, v_b_ada, v_norm_w, v_w_in, v_b_gate, v_q_norm_w, v_k_norm_w, v_w_attn_proj, v_conv_w, v_conv_b, v_conv_ln_w, v_conv_ln_b, v_w_conv_proj, v_w_out):
    xi, yi, ci = lax.axis_index("x"), lax.axis_index("y"), lax.axis_index("c")
    me = 4 * xi + 2 * yi + ci
    x2, tgt2 = x[0], loss_target[0]
    w_in_t, m_w_in_t, v_w_in_t = (jnp.transpose(a[0]) for a in (w_in, m_w_in, v_w_in))
    s = x2.shape[0]

    cw_flat = jnp.pad(conv_w[0].reshape(1, -1), ((0, 0), (0, CONVW_FLAT - CONV_K * HEAD_DIM)))
    pre = jnp.concatenate([c, cw_flat], axis=1).reshape(8, -1)
    (pre_all,) = _all_gather([pre], "gather_c_convw", vmem=True)
    pre_all = pre_all.reshape(N_DEV, -1)
    c_all = pre_all[:, :D_MODEL]
    convw_full = pre_all[:, D_MODEL:D_MODEL + CONV_K * HEAD_DIM].reshape(N_DEV, CONV_K, HEAD_DIM)
    convw_full = jnp.transpose(convw_full, (1, 0, 2)).reshape(CONV_K, CONV_W)
    convw_pad = jnp.pad(convw_full, ((0, CONV_HALO - CONV_K), (0, 0)))

    ada_part = _ada_fwd(c_all, w_ada[0])
    (ada_all,) = _all_gather([ada_part], "gather_ada", vmem=True)
    ada = lax.dynamic_index_in_dim(ada_all, me, axis=1, keepdims=False).reshape(1, 3 * D_MODEL) + b_ada
    shift, scale, gate = ada[:, :D_MODEL], ada[:, D_MODEL:2 * D_MODEL], ada[:, 2 * D_MODEL:]

    wt_g, wa_g, wc_g, wo_g = _all_gather_chips(
        [_cast_bf16(w_in_t, "cast_win"), _cast_bf16(w_attn_proj[0], "cast_wa"), _cast_bf16(w_conv_proj[0], "cast_wc"),
         _cast_bf16(w_out[0], "cast_wo")], "gather_weights")
    wt = wt_g.reshape(IN_W, D_MODEL)
    wa = _cols_from_slots(wa_g, "cols_wa")
    wc = _cols_from_slots(wc_g, "cols_wc")
    wo = wo_g.reshape(D_MODEL, D_MODEL)

    h, ht = _norm_fwd(x2, norm_w, scale, shift)
    proj = _mm_in(h, wt)
    qw2 = jnp.tile(q_norm_w, (1, 2))
    kw2 = jnp.tile(k_norm_w, (1, 2))
    o_all, l_all, qn, kn, vn = _attn_fwd(proj, qw2, kw2)
    o3, l3 = [o_all] * N_GROUPS, [l_all] * N_GROUPS
    head_id = jnp.arange(ATTN_W) // HEAD_DIM
    bd = (head_id[:, None] == head_id[None, :]).astype(BF16)
    (dout, da, lse_delta, dcv, mt, yat, yct, dmo, dya, dyc, dproj,
     dgate, dbg, dlnw, dlnb, dcb, loss_p) = _tail(
        x2, tgt2, proj, o3, l3, wa, wc, wo, gate, b_gate[:, :D_MODEL], b_gate[:, D_MODEL:], convw_pad,
        conv_b, conv_ln_w, conv_ln_b, bd)

    dproj, dconvw8 = _conv_bwd(dcv, proj, convw_pad, dproj)
    dconvw = jnp.sum(dconvw8.reshape(CONV_HALO, 8, CONV_W), axis=1)
    dproj, dqw_all, dkw_all = _attn_bwd(proj, qn, kn, vn, da, lse_delta, qw2, kw2, dproj)
    dqw_g3, dkw_g3 = [dqw_all], [dkw_all]
    dw_in_p = _mm_dw(ht, dproj).reshape(N_DEV, SHARD_W, D_MODEL)
    sb_send, sb_recv, sb_ins, sb_outs, sb_token = _exchange_sibling_start([dw_in_p], "exchange_sibling_start")
    dwo_p = _mm_acc(mt, dmo, sb_token, "mm_dwo", col_slots=False).reshape(N_DEV, D_MODEL // N_DEV, D_MODEL)
    dwa_p = _mm_acc(yat, dya, sb_token, "mm_dwa", col_slots=True)
    dwc_p = _mm_acc(yct, dyc, sb_token, "mm_dwc", col_slots=True)
    sm_send, sm_recv, sm_ins, sm_outs, sm_token = _exchange_sibling_start([dwa_p, dwc_p, dwo_p], "exchange_small_start")
    (dw_in_p,), (win_from_sib,) = _exchange_sibling_wait(sb_send, sb_recv, sb_ins, sb_outs, sm_token,
                                                         "exchange_sibling_wait")
    me_arr = jnp.reshape(me, (1,)).astype(jnp.int32)
    presum_win = _presum(dw_in_p, win_from_sib, me_arr, "presum0")
    small_p, small_from_sib = _exchange_sibling_wait(sm_send, sm_recv, sm_ins, sm_outs, presum_win,
                                                     "exchange_small_wait")
    presums = [presum_win] + [_presum(p, f, me_arr, f"presum{i + 1}")
                              for i, (p, f) in enumerate(zip(small_p, small_from_sib))]
    s_sems, r_sems, pre_thru, land_thru, token = _exchange_chips_start(presums, "exchange_chips_start")
    gx, dsh, dsc, dnw = _mm_dh_norm_bwd(dproj, wt, x2, dout, norm_w, scale, token)
    small_p = _pack_partials(dsh, dsc, dgate, dnw, dbg, dqw_g3, dkw_g3, dcb, dlnw, dlnb, loss_p)
    small_all, dconvw_all = _all_gather([small_p, dconvw], "gather_small", vmem=True)

    small_w = (b_ada, norm_w, b_gate, q_norm_w, k_norm_w, conv_b, conv_ln_w, conv_ln_b)
    small_m = (m_b_ada, m_norm_w, m_b_gate, m_q_norm_w, m_k_norm_w, m_conv_b, m_conv_ln_w, m_conv_ln_b)
    small_v = (v_b_ada, v_norm_w, v_b_gate, v_q_norm_w, v_k_norm_w, v_conv_b, v_conv_ln_w, v_conv_ln_b)
    r_small, loss_row = _adamw_small(small_all, small_w, small_m, small_v)
    dcw_mine = lax.dynamic_slice_in_dim(dconvw_all[:, :CONV_K, :], me * HEAD_DIM, HEAD_DIM, axis=2)
    r_convw = _adamw([(dcw_mine, N_DEV)], conv_w[0], m_conv_w[0], v_conv_w[0], "adamw_conv_w", stacked=True)

    d_ada_all = small_all[:, 0:3, :].reshape(N_DEV, 3 * D_MODEL)
    d_ada_cols = lax.dynamic_slice_in_dim(d_ada_all, me * (3 * D_MODEL // N_DEV), 3 * D_MODEL // N_DEV, axis=1)
    g_wada = _ada_bwd(c_all, d_ada_cols)
    r_ada = _adamw(g_wada, w_ada[0], m_w_ada[0], v_w_ada[0], "adamw_w_ada", stacked=False)
    pres, lands = _exchange_chips_wait(s_sems, r_sems, pre_thru, land_thru, r_ada[1], "exchange_chips_wait")
    terms = [[(p, 1), (l, len(CHIP_K))] for p, l in zip(pres, lands)]
    r_win = [jnp.transpose(r) for r in _adamw(terms[0], w_in_t, m_w_in_t, v_w_in_t, "adamw_w_in", stacked=True)]
    r_wap = _adamw(terms[1], w_attn_proj[0], m_w_attn_proj[0], v_w_attn_proj[0], "adamw_w_attn_proj", stacked=True)
    r_wcp = _adamw(terms[2], w_conv_proj[0], m_w_conv_proj[0], v_w_conv_proj[0], "adamw_w_conv_proj", stacked=True)
    r_wout = _adamw(terms[3], w_out[0], m_w_out[0], v_w_out[0], "adamw_w_out", stacked=True)

    outs = [loss_row[0, 0], gx[None]]
    for k in range(4):
        b_ada_k, norm_w_k, b_gate_k, qn_k, kn_k, conv_b_k, ln_w_k, ln_b_k = r_small[k]
        outs += [r_ada[k][None], b_ada_k, norm_w_k, r_win[k][None], b_gate_k, qn_k, kn_k, r_wap[k][None],
                 r_convw[k][None], conv_b_k, ln_w_k, ln_b_k, r_wcp[k][None], r_wout[k][None]]
    return tuple(outs)
```
